```python
import math
import jax, jax.numpy as jnp
from jax import lax
import numpy as np

D_MODEL = 2048
BATCH = 8
SEQ = 2048
DEPTH = 2

MEM_LEN = 256
CHUNK = 128
Q_BLOCK = 128
D_A = D_MODEL // 2
A_GROUPS = 8
A_GROUP_DIM = D_A // A_GROUPS
D_B = D_MODEL // 4
B_HEADS = 4
B_HEAD_DIM = D_B // B_HEADS
D_C = D_MODEL // 4
C_HEADS = 4
C_HEAD_DIM = D_C // C_HEADS
SPLIT_SIZES = (D_A, D_A, D_A, D_B, D_B, D_B, D_B, D_C, D_C)
IN_WIDTH = sum(SPLIT_SIZES)
EPS = 1e-6

kernel_name = "hybrid_sgu_stickbreak_memxattn"


def rms_norm(x, g):
    xf = x.astype(jnp.float32)
    y = xf * lax.rsqrt(jnp.mean(xf * xf, axis=-1, keepdims=True) + EPS)
    return (y * g.astype(jnp.float32)).astype(x.dtype)


def layer_norm(x, g, b):
    xf = x.astype(jnp.float32)
    mu = jnp.mean(xf, axis=-1, keepdims=True)
    xc = xf - mu
    y = xc * lax.rsqrt(jnp.mean(xc * xc, axis=-1, keepdims=True) + EPS)
    return (y * g.astype(jnp.float32) + b.astype(jnp.float32)).astype(x.dtype)


def sgu_mixer(u, v, ln_g, ln_b, w_s, b_s):
    bsz, s_len, _ = v.shape
    n_chunks = s_len // CHUNK
    v = layer_norm(v, ln_g, ln_b)
    vc = v.reshape(bsz, n_chunks, CHUNK, A_GROUPS, A_GROUP_DIM)
    mask = jnp.tril(jnp.ones((CHUNK, CHUNK), dtype=bool))
    w = jnp.where(mask[None], w_s, jnp.zeros_like(w_s))
    mixed = jnp.einsum('gts,bcsgd->bctgd', w, vc) + b_s.T[None, None, :, :, None]
    return u * mixed.reshape(bsz, s_len, D_A)


def stick_breaking_attention(q, k, v):
    s_len = q.shape[1]
    scale = 1.0 / math.sqrt(q.shape[-1])
    outs = []
    for i in range(s_len // Q_BLOCK):
        start = i * Q_BLOCK
        kv_len = start + Q_BLOCK
        q_blk = q[:, start:kv_len]
        k_blk = k[:, :kv_len]
        v_blk = v[:, :kv_len]
        z = jnp.einsum('bthd,bshd->bhts', q_blk, k_blk).astype(jnp.float32) * scale
        t_idx = start + jnp.arange(Q_BLOCK)[:, None]
        s_idx = jnp.arange(kv_len)[None, :]
        causal = s_idx < t_idx
        log_beta = jax.nn.log_sigmoid(z)
        log_1mb = jnp.where(causal, jax.nn.log_sigmoid(-z), 0.0)
        rc = lax.cumsum(log_1mb, axis=3, reverse=True)
        after = jnp.pad(rc[..., 1:], ((0, 0), (0, 0), (0, 0), (0, 1)))
        a = jnp.where(causal, jnp.exp(log_beta + after), 0.0)
        outs.append(jnp.einsum('bhts,bshd->bthd', a.astype(v.dtype), v_blk))
    return jnp.concatenate(outs, axis=1)


def memory_attention(q, mem_k, mem_v, q_g, k_g):
    scale = 1.0 / math.sqrt(q.shape[-1])
    qn = rms_norm(q, q_g)
    kn = rms_norm(mem_k, k_g)
    s = jnp.einsum('bthd,bmhd->bhtm', qn, kn).astype(jnp.float32) * scale
    p = jax.nn.softmax(s, axis=-1)
    return jnp.einsum('bhtm,bmhd->bthd', p.astype(mem_v.dtype), mem_v)


def _fwd_setup_inputs(seed: int = 0) -> dict:
    key = jax.random.key(seed)
    ks = jax.random.split(key, 16)
    f32 = jnp.float32
    x = jax.random.normal(ks[0], (BATCH, SEQ, D_MODEL), f32)
    mem = jax.random.normal(ks[1], (BATCH, MEM_LEN, D_MODEL), f32)
    norm_g = 1.0 + 0.01 * jax.random.normal(ks[2], (DEPTH, D_MODEL), f32)
    w_in = jax.random.normal(ks[3], (DEPTH, D_MODEL, IN_WIDTH), f32) * D_MODEL ** -0.5
    sgu_ln_g = 1.0 + 0.01 * jax.random.normal(ks[4], (DEPTH, D_A), f32)
    sgu_ln_b = 0.01 * jax.random.normal(ks[5], (DEPTH, D_A), f32)
    sgu_w = jax.random.normal(ks[6], (DEPTH, A_GROUPS, CHUNK, CHUNK), f32) * CHUNK ** -0.5
    sgu_b = 1.0 + 0.01 * jax.random.normal(ks[7], (DEPTH, A_GROUPS, CHUNK), f32)
    mem_norm_g = 1.0 + 0.01 * jax.random.normal(ks[8], (DEPTH, D_MODEL), f32)
    w_mem_kv = jax.random.normal(ks[9], (DEPTH, D_MODEL, 2 * D_C), f32) * D_MODEL ** -0.5
    q_norm_g = 1.0 + 0.01 * jax.random.normal(ks[10], (DEPTH, C_HEAD_DIM), f32)
    k_norm_g = 1.0 + 0.01 * jax.random.normal(ks[11], (DEPTH, C_HEAD_DIM), f32)
    w_out = jax.random.normal(ks[12], (DEPTH, D_MODEL, D_MODEL), f32) * D_MODEL ** -0.5
    return {"x": x, "mem": mem, "norm_g": norm_g, "w_in": w_in,
            "sgu_ln_g": sgu_ln_g, "sgu_ln_b": sgu_ln_b, "sgu_w": sgu_w, "sgu_b": sgu_b,
            "mem_norm_g": mem_norm_g, "w_mem_kv": w_mem_kv,
            "q_norm_g": q_norm_g, "k_norm_g": k_norm_g, "w_out": w_out}


def _fwd_reference(x, mem, norm_g, w_in, sgu_ln_g, sgu_ln_b, sgu_w, sgu_b,
              mem_norm_g, w_mem_kv, q_norm_g, k_norm_g, w_out):
    bsz, s_len, _ = x.shape
    split_idx = list(np.cumsum(SPLIT_SIZES)[:-1])
    for l in range(DEPTH):
        h = rms_norm(x, norm_g[l])
        proj = jnp.matmul(h, w_in[l])
        u_a, v_a, z_a, q_b, k_b, v_b, z_b, q_c, z_c = jnp.split(proj, split_idx, axis=-1)

        u_a = jax.nn.gelu(u_a, approximate=False)
        v_a = jax.nn.gelu(v_a, approximate=False)
        y_a = sgu_mixer(u_a, v_a, sgu_ln_g[l], sgu_ln_b[l], sgu_w[l], sgu_b[l]) * jax.nn.silu(z_a)

        hb = (bsz, s_len, B_HEADS, B_HEAD_DIM)
        y_b = stick_breaking_attention(q_b.reshape(hb), k_b.reshape(hb), v_b.reshape(hb))
        y_b = y_b.reshape(bsz, s_len, D_B) * jax.nn.silu(z_b)

        mem_h = rms_norm(mem, mem_norm_g[l])
        mem_kv = jnp.matmul(mem_h, w_mem_kv[l])
        mem_k, mem_v = jnp.split(mem_kv, 2, axis=-1)
        hm = (bsz, mem.shape[1], C_HEADS, C_HEAD_DIM)
        y_c = memory_attention(q_c.reshape(bsz, s_len, C_HEADS, C_HEAD_DIM),
                               mem_k.reshape(hm), mem_v.reshape(hm), q_norm_g[l], k_norm_g[l])
        y_c = y_c.reshape(bsz, s_len, D_C) * jax.nn.silu(z_c)

        y = jnp.concatenate([y_a, y_b, y_c], axis=-1)
        x = x + jnp.matmul(y, w_out[l])
    return x


import jax as _jax
import jax.numpy as _jnp

TWIN_FORMAT = 'train_step'
FWD_PARAMS = ['x', 'mem', 'norm_g', 'w_in', 'sgu_ln_g', 'sgu_ln_b', 'sgu_w', 'sgu_b', 'mem_norm_g', 'w_mem_kv', 'q_norm_g', 'k_norm_g', 'w_out']
TWIN_WEIGHTS = ['norm_g', 'w_in', 'sgu_ln_g', 'sgu_ln_b', 'sgu_w', 'sgu_b', 'mem_norm_g', 'w_mem_kv', 'q_norm_g', 'k_norm_g', 'w_out']
TWIN_DIFF_INPUT = 'x'
TWIN_INPUTS = ['x', 'mem', 'norm_g', 'w_in', 'sgu_ln_g', 'sgu_ln_b', 'sgu_w', 'sgu_b', 'mem_norm_g', 'w_mem_kv', 'q_norm_g', 'k_norm_g', 'w_out', 'loss_target', 'm_norm_g', 'm_w_in', 'm_sgu_ln_g', 'm_sgu_ln_b', 'm_sgu_w', 'm_sgu_b', 'm_mem_norm_g', 'm_w_mem_kv', 'm_q_norm_g', 'm_k_norm_g', 'm_w_out', 'v_norm_g', 'v_w_in', 'v_sgu_ln_g', 'v_sgu_ln_b', 'v_sgu_w', 'v_sgu_b', 'v_mem_norm_g', 'v_w_mem_kv', 'v_q_norm_g', 'v_k_norm_g', 'v_w_out']
TWIN_OUTPUTS = ['loss', 'grad_x', 'grad_norm_g', 'grad_w_in', 'grad_sgu_ln_g', 'grad_sgu_ln_b', 'grad_sgu_w', 'grad_sgu_b', 'grad_mem_norm_g', 'grad_w_mem_kv', 'grad_q_norm_g', 'grad_k_norm_g', 'grad_w_out', 'delta_norm_g', 'delta_w_in', 'delta_sgu_ln_g', 'delta_sgu_ln_b', 'delta_sgu_w', 'delta_sgu_b', 'delta_mem_norm_g', 'delta_w_mem_kv', 'delta_q_norm_g', 'delta_k_norm_g', 'delta_w_out', 'new_m_norm_g', 'new_m_w_in', 'new_m_sgu_ln_g', 'new_m_sgu_ln_b', 'new_m_sgu_w', 'new_m_sgu_b', 'new_m_mem_norm_g', 'new_m_w_mem_kv', 'new_m_q_norm_g', 'new_m_k_norm_g', 'new_m_w_out', 'new_v_norm_g', 'new_v_w_in', 'new_v_sgu_ln_g', 'new_v_sgu_ln_b', 'new_v_sgu_w', 'new_v_sgu_b', 'new_v_mem_norm_g', 'new_v_w_mem_kv', 'new_v_q_norm_g', 'new_v_k_norm_g', 'new_v_w_out']
TWIN_LEAF_KINDS = {'loss': 'loss', 'grad_x': 'grad_x', 'grad_norm_g': 'grad_w', 'grad_w_in': 'grad_w', 'grad_sgu_ln_g': 'grad_w', 'grad_sgu_ln_b': 'grad_w', 'grad_sgu_w': 'grad_w', 'grad_sgu_b': 'grad_w', 'grad_mem_norm_g': 'grad_w', 'grad_w_mem_kv': 'grad_w', 'grad_q_norm_g': 'grad_w', 'grad_k_norm_g': 'grad_w', 'grad_w_out': 'grad_w', 'delta_norm_g': 'delta_w', 'delta_w_in': 'delta_w', 'delta_sgu_ln_g': 'delta_w', 'delta_sgu_ln_b': 'delta_w', 'delta_sgu_w': 'delta_w', 'delta_sgu_b': 'delta_w', 'delta_mem_norm_g': 'delta_w', 'delta_w_mem_kv': 'delta_w', 'delta_q_norm_g': 'delta_w', 'delta_k_norm_g': 'delta_w', 'delta_w_out': 'delta_w', 'new_m_norm_g': 'new_m', 'new_m_w_in': 'new_m', 'new_m_sgu_ln_g': 'new_m', 'new_m_sgu_ln_b': 'new_m', 'new_m_sgu_w': 'new_m', 'new_m_sgu_b': 'new_m', 'new_m_mem_norm_g': 'new_m', 'new_m_w_mem_kv': 'new_m', 'new_m_q_norm_g': 'new_m', 'new_m_k_norm_g': 'new_m', 'new_m_w_out': 'new_m', 'new_v_norm_g': 'new_v', 'new_v_w_in': 'new_v', 'new_v_sgu_ln_g': 'new_v', 'new_v_sgu_ln_b': 'new_v', 'new_v_sgu_w': 'new_v', 'new_v_sgu_b': 'new_v', 'new_v_mem_norm_g': 'new_v', 'new_v_w_mem_kv': 'new_v', 'new_v_q_norm_g': 'new_v', 'new_v_k_norm_g': 'new_v', 'new_v_w_out': 'new_v'}


def _forward(args):
    return _fwd_reference(*[args[k] for k in FWD_PARAMS])


def _output_shape():
    out = _jax.eval_shape(lambda: _forward(_fwd_setup_inputs(0)))
    return out.shape, out.dtype

N_MICROBATCH = 1
ADAM_LR = 0.001
ADAM_B1 = 0.9
ADAM_B2 = 0.999
ADAM_EPS = 1e-08
ADAM_WD = 0.01
ADAM_STEP = 10
PER_EXAMPLE_BATCH_AXIS = {'x': 0, 'mem': 0, 'loss_target': 0}
SHARED_INPUTS = []
_WEIGHT_DTYPES = {'norm_g': _jnp.float32, 'w_in': _jnp.float32, 'sgu_ln_g': _jnp.float32, 'sgu_ln_b': _jnp.float32, 'sgu_w': _jnp.float32, 'sgu_b': _jnp.float32, 'mem_norm_g': _jnp.float32, 'w_mem_kv': _jnp.float32, 'q_norm_g': _jnp.float32, 'k_norm_g': _jnp.float32, 'w_out': _jnp.float32}
MOMENT_SCALE = {'norm_g': 2.825425e+00, 'w_in': 8.549429e-02, 'sgu_ln_g': 5.709195e-01, 'sgu_ln_b': 8.982743e-02, 'sgu_w': 8.869736e-02, 'sgu_b': 1.184374e+00, 'mem_norm_g': 8.898759e-03, 'w_mem_kv': 9.124801e-03, 'q_norm_g': 1.069980e-01, 'k_norm_g': 1.068689e-01, 'w_out': 6.592033e-02}


def _to_microbatches(a, axis):
    t = _jnp.moveaxis(a, axis, 0)
    t = t.reshape((N_MICROBATCH, t.shape[0] // N_MICROBATCH) + t.shape[1:])
    return _jnp.moveaxis(t, 1, axis + 1)


def setup_inputs(seed: int = 0) -> dict:
    inp = _fwd_setup_inputs(seed)
    key = _jax.random.fold_in(_jax.random.key(seed), 7919)
    shape, _ = _output_shape()
    out = dict(inp)
    out["loss_target"] = _jax.random.normal(_jax.random.fold_in(key, 0), shape, _jnp.float32)
    for i, name in enumerate(TWIN_WEIGHTS):
        w = inp[name].astype(_jnp.float32)
        if MOMENT_SCALE is None:
            s = _jnp.sqrt(_jnp.mean(_jnp.square(w)) + 1e-30)
        else:
            s = MOMENT_SCALE[name]
        km, kv = _jax.random.split(_jax.random.fold_in(key, i + 1))
        out[name] = w
        out["m_" + name] = s * _jax.random.normal(km, w.shape, _jnp.float32)
        out["v_" + name] = (s * s) * _jax.random.uniform(kv, w.shape, _jnp.float32, 0.5, 1.5)
    if N_MICROBATCH > 1:
        for name, axis in PER_EXAMPLE_BATCH_AXIS.items():
            out[name] = _to_microbatches(out[name], axis)
    return {'x': out['x'], 'mem': out['mem'], 'norm_g': out['norm_g'], 'w_in': out['w_in'], 'sgu_ln_g': out['sgu_ln_g'], 'sgu_ln_b': out['sgu_ln_b'], 'sgu_w': out['sgu_w'], 'sgu_b': out['sgu_b'], 'mem_norm_g': out['mem_norm_g'], 'w_mem_kv': out['w_mem_kv'], 'q_norm_g': out['q_norm_g'], 'k_norm_g': out['k_norm_g'], 'w_out': out['w_out'], 'loss_target': out['loss_target'], 'm_norm_g': out['m_norm_g'], 'm_w_in': out['m_w_in'], 'm_sgu_ln_g': out['m_sgu_ln_g'], 'm_sgu_ln_b': out['m_sgu_ln_b'], 'm_sgu_w': out['m_sgu_w'], 'm_sgu_b': out['m_sgu_b'], 'm_mem_norm_g': out['m_mem_norm_g'], 'm_w_mem_kv': out['m_w_mem_kv'], 'm_q_norm_g': out['m_q_norm_g'], 'm_k_norm_g': out['m_k_norm_g'], 'm_w_out': out['m_w_out'], 'v_norm_g': out['v_norm_g'], 'v_w_in': out['v_w_in'], 'v_sgu_ln_g': out['v_sgu_ln_g'], 'v_sgu_ln_b': out['v_sgu_ln_b'], 'v_sgu_w': out['v_sgu_w'], 'v_sgu_b': out['v_sgu_b'], 'v_mem_norm_g': out['v_mem_norm_g'], 'v_w_mem_kv': out['v_w_mem_kv'], 'v_q_norm_g': out['v_q_norm_g'], 'v_k_norm_g': out['v_k_norm_g'], 'v_w_out': out['v_w_out']}


def _loss(weights, diff, rest, loss_target):
    with _jax.named_scope("forward"):
        args = {**rest, TWIN_DIFF_INPUT: diff, **{k: w.astype(_WEIGHT_DTYPES[k]) for k, w in weights.items()}}
        y = _forward(args)
    with _jax.named_scope("loss_head"):
        err = _jnp.square(y.astype(_jnp.float32) - loss_target)
        return 0.5 * _jnp.sum(_jnp.mean(err, axis=-1)) if err.ndim else 0.5 * err


def _adamw(w, g, m, v):
    m = ADAM_B1 * m + (1.0 - ADAM_B1) * g
    v = ADAM_B2 * v + (1.0 - ADAM_B2) * _jnp.square(g)
    m_hat = m / (1.0 - ADAM_B1 ** ADAM_STEP)
    v_hat = v / (1.0 - ADAM_B2 ** ADAM_STEP)
    delta = -ADAM_LR * (m_hat / (_jnp.sqrt(v_hat) + ADAM_EPS) + ADAM_WD * w)
    return delta, m, v


def reference(x, mem, norm_g, w_in, sgu_ln_g, sgu_ln_b, sgu_w, sgu_b, mem_norm_g, w_mem_kv, q_norm_g, k_norm_g, w_out, loss_target, m_norm_g, m_w_in, m_sgu_ln_g, m_sgu_ln_b, m_sgu_w, m_sgu_b, m_mem_norm_g, m_w_mem_kv, m_q_norm_g, m_k_norm_g, m_w_out, v_norm_g, v_w_in, v_sgu_ln_g, v_sgu_ln_b, v_sgu_w, v_sgu_b, v_mem_norm_g, v_w_mem_kv, v_q_norm_g, v_k_norm_g, v_w_out):
    given = dict(x=x, mem=mem, norm_g=norm_g, w_in=w_in, sgu_ln_g=sgu_ln_g, sgu_ln_b=sgu_ln_b, sgu_w=sgu_w, sgu_b=sgu_b, mem_norm_g=mem_norm_g, w_mem_kv=w_mem_kv, q_norm_g=q_norm_g, k_norm_g=k_norm_g, w_out=w_out, loss_target=loss_target, m_norm_g=m_norm_g, m_w_in=m_w_in, m_sgu_ln_g=m_sgu_ln_g, m_sgu_ln_b=m_sgu_ln_b, m_sgu_w=m_sgu_w, m_sgu_b=m_sgu_b, m_mem_norm_g=m_mem_norm_g, m_w_mem_kv=m_w_mem_kv, m_q_norm_g=m_q_norm_g, m_k_norm_g=m_k_norm_g, m_w_out=m_w_out, v_norm_g=v_norm_g, v_w_in=v_w_in, v_sgu_ln_g=v_sgu_ln_g, v_sgu_ln_b=v_sgu_ln_b, v_sgu_w=v_sgu_w, v_sgu_b=v_sgu_b, v_mem_norm_g=v_mem_norm_g, v_w_mem_kv=v_w_mem_kv, v_q_norm_g=v_q_norm_g, v_k_norm_g=v_k_norm_g, v_w_out=v_w_out)
    weights = {n: given[n] for n in TWIN_WEIGHTS}
    shared = {n: given[n] for n in SHARED_INPUTS}
    per_example = {n: given[n] for n in ['x', 'mem']}
    grad_fn = _jax.value_and_grad(_loss, argnums=(0, 1))

    def one_microbatch(ex, loss_target):
        ex = dict(ex)
        diff = ex.pop(TWIN_DIFF_INPUT)
        return grad_fn(weights, diff, {**shared, **ex}, loss_target)

    if N_MICROBATCH == 1:
        loss, (grad_w, grad_x) = one_microbatch(per_example, given["loss_target"])
    else:
        def body(carry, xs):
            loss_sum, grad_sum = carry
            l_k, (gw_k, gx_k) = one_microbatch(xs[0], xs[1])
            with _jax.named_scope("update"):
                return (loss_sum + l_k, _jax.tree.map(_jnp.add, grad_sum, gw_k)), gx_k

        init = (_jnp.zeros((), _jnp.float32), _jax.tree.map(_jnp.zeros_like, weights))
        (loss, grad_w), grad_x = _jax.lax.scan(body, init, (per_example, given["loss_target"]))
    with _jax.named_scope("update"):
        delta_w, new_m, new_v = {}, {}, {}
        for n in TWIN_WEIGHTS:
            delta_w[n], new_m[n], new_v[n] = _adamw(weights[n], grad_w[n], given["m_" + n], given["v_" + n])
    return (loss, grad_x, *[grad_w[n] for n in TWIN_WEIGHTS], *[delta_w[n] for n in TWIN_WEIGHTS],
            *[new_m[n] for n in TWIN_WEIGHTS], *[new_v[n] for n in TWIN_WEIGHTS])
```

```python
import functools
import math

import jax
import jax.numpy as jnp
from jax import lax
from jax.experimental import pallas as pl
from jax.experimental.pallas import tpu as pltpu

f32 = jnp.float32
bf16 = jnp.bfloat16
SDS = jax.ShapeDtypeStruct

N_DEV = 8
EPS = 1e-6
CHUNK = 128
A_GROUPS = 8
HEAD_DIM = 128
N_HEADS = 4
TQ = 256
TK = 128
ADAM_LR, ADAM_B1, ADAM_B2, ADAM_EPS, ADAM_WD, ADAM_STEP = 0.001, 0.9, 0.999, 1e-08, 0.01, 10
MIB = 1024 * 1024

NT = (((1,), (1,)), ((), ()))
TN = (((0,), (0,)), ((), ()))


def _params(vmem_mib=48):
    return pltpu.CompilerParams(vmem_limit_bytes=vmem_mib * MIB)


def _gelu_and_grad(x):
    e = lax.erf(x * (1.0 / math.sqrt(2.0)))
    cdf = 0.5 * (1.0 + e)
    pdf = jnp.exp(-0.5 * x * x) * (1.0 / math.sqrt(2.0 * math.pi))
    return x * cdf, cdf + x * pdf


def _gelu(x):
    return 0.5 * x * (1.0 + lax.erf(x * (1.0 / math.sqrt(2.0))))


def _silu_and_grad(z):
    sg = jax.nn.sigmoid(z)
    return z * sg, sg * (1.0 + z * (1.0 - sg))


def _dot(a, b, dims=None):
    if dims is None:
        return jnp.dot(a, b, preferred_element_type=f32)
    return lax.dot_general(a, b, dims, preferred_element_type=f32)


def _exchange(name, srcs, scatter):
    n = len(srcs)
    if scatter:
        out_shape = [SDS(s.shape, s.dtype) for s in srcs]
    else:
        out_shape = [SDS((N_DEV,) + s.shape, s.dtype) for s in srcs]

    def body(*refs):
        src_refs, dst_refs = refs[:n], refs[n:2 * n]
        send_sems, recv_sems, loc_sems = refs[2 * n:]
        x, y, c = lax.axis_index("x"), lax.axis_index("y"), lax.axis_index("c")
        me = 4 * x + 2 * y + c
        copies, locals_ = [], []
        for a in range(n):
            s, d = src_refs[a], dst_refs[a]
            loc = pltpu.make_async_copy(s.at[me] if scatter else s, d.at[me], loc_sems.at[a])
            loc.start()
            locals_.append(loc)
            for m in range(1, N_DEV):
                px = 1 - x if m & 4 else x
                py = 1 - y if m & 2 else y
                pc = 1 - c if m & 1 else c
                peer = 4 * px + 2 * py + pc
                cp = pltpu.make_async_remote_copy(
                    src_ref=s.at[peer] if scatter else s, dst_ref=d.at[me],
                    send_sem=send_sems.at[a * 7 + m - 1], recv_sem=recv_sems.at[a * 7 + m - 1],
                    device_id=(px, py, pc), device_id_type=pl.DeviceIdType.MESH)
                cp.start()
                copies.append(cp)
        for cp in copies:
            cp.wait_recv()
        for cp in copies:
            cp.wait_send()
        for loc in locals_:
            loc.wait()

    any_spec = pl.BlockSpec(memory_space=pl.ANY)
    outs = pl.pallas_call(
        body, name=name, out_shape=out_shape,
        in_specs=[any_spec] * n, out_specs=[any_spec] * n,
        scratch_shapes=[pltpu.SemaphoreType.DMA((n * 7,)), pltpu.SemaphoreType.DMA((n * 7,)),
                        pltpu.SemaphoreType.DMA((n,))],
    )(*srcs)
    return list(outs)


def _cast_bf16(name, w, tr):
    R, C = w.shape

    def body(w_ref, o_ref):
        o_ref[...] = w_ref[...].astype(bf16)

    return pl.pallas_call(
        body, name=name, grid=(R // tr,),
        in_specs=[pl.BlockSpec((tr, C), lambda i: (i, 0))],
        out_specs=pl.BlockSpec((tr, C), lambda i: (i, 0)),
        out_shape=SDS((R, C), bf16), compiler_params=_params(),
    )(w)


def _adam_math(w, g, m, v):
    m2 = ADAM_B1 * m + (1.0 - ADAM_B1) * g
    v2 = ADAM_B2 * v + (1.0 - ADAM_B2) * (g * g)
    m_hat = m2 / (1.0 - ADAM_B1 ** ADAM_STEP)
    v_hat = v2 / (1.0 - ADAM_B2 ** ADAM_STEP)
    delta = -ADAM_LR * (m_hat / (jnp.sqrt(v_hat) + ADAM_EPS) + ADAM_WD * w)
    return delta, m2, v2


def _sum_adam(name, recv, w, m, v, tr):
    _, L, R, C = recv.shape

    def body(r_ref, w_ref, m_ref, v_ref, g_ref, d_ref, m2_ref, v2_ref):
        g = r_ref[0].astype(f32)
        for s in range(1, N_DEV):
            g = g + r_ref[s].astype(f32)
        d, m2, v2 = _adam_math(w_ref[...], g, m_ref[...], v_ref[...])
        g_ref[...] = g
        d_ref[...] = d
        m2_ref[...] = m2
        v2_ref[...] = v2

    wspec = pl.BlockSpec((None, tr, C), lambda l, i: (l, i, 0))
    return pl.pallas_call(
        body, name=name, grid=(L, R // tr),
        in_specs=[pl.BlockSpec((N_DEV, None, tr, C), lambda l, i: (0, l, i, 0)), wspec, wspec, wspec],
        out_specs=[wspec] * 4, out_shape=[SDS((L, R, C), f32)] * 4, compiler_params=_params(),
    )(recv, w, m, v)


def _small_sum_adam(recv, w, m, v):
    _, R, C = recv.shape

    def body(r_ref, w_ref, m_ref, v_ref, g_ref, d_ref, m2_ref, v2_ref):
        g = r_ref[0]
        for s in range(1, N_DEV):
            g = g + r_ref[s]
        d, m2, v2 = _adam_math(w_ref[...], g, m_ref[...], v_ref[...])
        g_ref[...] = g
        d_ref[...] = d
        m2_ref[...] = m2
        v2_ref[...] = v2

    return pl.pallas_call(
        body, name="small_sum_adam", out_shape=[SDS((R, C), f32)] * 4, compiler_params=_params(),
    )(recv, w, m, v)


def _loss_and_grad(xf, tgt, tm):
    S, D = xf.shape

    def body(x_ref, t_ref, dx_ref, dxb_ref, l_ref):
        i = pl.program_id(0)
        d = x_ref[...] - t_ref[...]
        dx = d * (1.0 / D)
        dx_ref[...] = dx
        dxb_ref[...] = dx.astype(bf16)
        e = d * d
        part = e[:, 0:128]
        for k in range(1, D // 128):
            part = part + e[:, k * 128:(k + 1) * 128]
        part = jnp.sum(part.reshape(tm // 8, 8, 128), axis=0)

        @pl.when(i == 0)
        def _():
            l_ref[...] = jnp.zeros_like(l_ref)

        l_ref[...] += part

        @pl.when(i == pl.num_programs(0) - 1)
        def _():
            tot = jnp.sum(l_ref[...], axis=1, keepdims=True)
            tot = jnp.sum(tot, axis=0, keepdims=True)
            l_ref[...] = jnp.broadcast_to(tot * (0.5 / D), l_ref.shape)

    row = pl.BlockSpec((tm, D), lambda i: (i, 0))
    return pl.pallas_call(
        body, name="loss_grad", grid=(S // tm,),
        in_specs=[row, row], out_specs=[row, row, pl.BlockSpec((8, 128), lambda i: (0, 0))],
        out_shape=[SDS((S, D), f32), SDS((S, D), bf16), SDS((8, 128), f32)], compiler_params=_params(),
    )(xf, tgt)


def _rms_proj(x, g_row, w_in_g, l, tm=512):
    S, D = x.shape
    wc = w_in_g.shape[3]
    tn = 256
    per = wc // tn
    n_out = N_DEV * wc

    def body(x_ref, g_ref, w_ref, proj_ref, h_ref):
        @pl.when(pl.program_id(1) == 0)
        def _():
            xv = x_ref[...]
            r = lax.rsqrt(jnp.mean(xv * xv, axis=-1, keepdims=True) + EPS)
            h_ref[...] = (xv * r * g_ref[...]).astype(bf16)

        proj_ref[...] = _dot(h_ref[...], w_ref[...])

    return pl.pallas_call(
        body, name="rms_proj", grid=(S // tm, n_out // tn),
        in_specs=[pl.BlockSpec((tm, D), lambda i, j: (i, 0)),
                  pl.BlockSpec((1, D), lambda i, j: (0, 0)),
                  pl.BlockSpec((None, None, D, tn), lambda i, j: (j // per, l, 0, j % per))],
        out_specs=[pl.BlockSpec((tm, tn), lambda i, j: (i, j)), pl.BlockSpec((tm, D), lambda i, j: (i, 0))],
        out_shape=[SDS((S, n_out), f32), SDS((S, D), bf16)], compiler_params=_params(),
    )(x, g_row, w_in_g)


def _out_proj(x, y, w_out_g, l, tm=512, tn=512):
    S, D = x.shape
    rb = w_out_g.shape[2]

    def body(x_ref, y_ref, w_ref, o_ref):
        w = w_ref[...].reshape(N_DEV * rb, tn)
        o_ref[...] = x_ref[...] + _dot(y_ref[...], w)

    return pl.pallas_call(
        body, name="out_proj", grid=(S // tm, D // tn),
        in_specs=[pl.BlockSpec((tm, tn), lambda i, j: (i, j)),
                  pl.BlockSpec((tm, D), lambda i, j: (i, 0)),
                  pl.BlockSpec((N_DEV, None, rb, tn), lambda i, j: (0, l, 0, j))],
        out_specs=pl.BlockSpec((tm, tn), lambda i, j: (i, j)),
        out_shape=SDS((S, D), f32), compiler_params=_params(),
    )(x, y, w_out_g)


def _out_bwd_dy(dxb, w_out_g, l, tm=512):
    S, D = dxb.shape
    rb = w_out_g.shape[2]

    def body(dx_ref, w_ref, o_ref):
        o_ref[...] = _dot(dx_ref[...], w_ref[...], NT)

    return pl.pallas_call(
        body, name="out_bwd_dy", grid=(S // tm, N_DEV),
        in_specs=[pl.BlockSpec((tm, D), lambda i, j: (i, 0)),
                  pl.BlockSpec((None, None, rb, D), lambda i, j: (j, l, 0, 0))],
        out_specs=pl.BlockSpec((tm, rb), lambda i, j: (i, j)),
        out_shape=SDS((S, D), f32), compiler_params=_params(),
    )(dxb, w_out_g)


def _tn_grad(name, a, b, prev, l, tm, tn, rows_major):
    S, M = a.shape
    N = b.shape[1]
    if rows_major:
        out_shape = SDS((N_DEV, 2, M // N_DEV, N), bf16)
        assert tm == M // N_DEV
        out_spec = pl.BlockSpec((None, None, tm, tn), lambda i, j: (i, l, 0, j))
    else:
        out_shape = SDS((N_DEV, 2, M, N // N_DEV), bf16)
        assert tn == N // N_DEV
        out_spec = pl.BlockSpec((None, None, tm, tn), lambda i, j: (j, l, i, 0))

    def body(a_ref, b_ref, *rest):
        o_ref = rest[-1]
        o_ref[...] = _dot(a_ref[...], b_ref[...], TN).astype(bf16)

    in_specs = [pl.BlockSpec((S, tm), lambda i, j: (0, i)), pl.BlockSpec((S, tn), lambda i, j: (0, j))]
    args = [a, b]
    aliases = {}
    if prev is not None:
        in_specs.append(pl.BlockSpec(memory_space=pl.ANY))
        args.append(prev)
        aliases = {2: 0}
    return pl.pallas_call(
        body, name=name, grid=(M // tm, N // tn), in_specs=in_specs, out_specs=out_spec,
        out_shape=out_shape, input_output_aliases=aliases, compiler_params=_params(),
    )(*args)


def _in_bwd_dx(dproj, w_in_g, l, x, g_row, dx_next, tm=512):
    S, D = x.shape
    wc = w_in_g.shape[3]

    def body(dp_ref, w_ref, x_ref, g_ref, dxn_ref, dx_ref, dxb_ref, dg_ref, acc_ref):
        i, k = pl.program_id(0), pl.program_id(1)

        @pl.when(k == 0)
        def _():
            acc_ref[...] = jnp.zeros_like(acc_ref)

        acc_ref[...] += _dot(dp_ref[...], w_ref[...], NT)

        @pl.when(jnp.logical_and(i == 0, k == 0))
        def _():
            dg_ref[...] = jnp.zeros_like(dg_ref)

        @pl.when(k == N_DEV - 1)
        def _():
            dh = acc_ref[...]
            xv = x_ref[...]
            r = lax.rsqrt(jnp.mean(xv * xv, axis=-1, keepdims=True) + EPS)
            xhat = xv * r
            dxhat = dh * g_ref[...]
            dx = r * (dxhat - xhat * jnp.mean(dxhat * xhat, axis=-1, keepdims=True)) + dxn_ref[...]
            dx_ref[...] = dx
            dxb_ref[...] = dx.astype(bf16)
            dg_ref[...] += jnp.sum(dh * xhat, axis=0, keepdims=True)

    row = pl.BlockSpec((tm, D), lambda i, k: (i, 0))
    return pl.pallas_call(
        body, name="in_bwd_dx", grid=(S // tm, N_DEV),
        in_specs=[pl.BlockSpec((tm, wc), lambda i, k: (i, k)),
                  pl.BlockSpec((None, None, D, wc), lambda i, k: (k, l, 0, 0)),
                  row, pl.BlockSpec((1, D), lambda i, k: (0, 0)), row],
        out_specs=[row, row, pl.BlockSpec((1, D), lambda i, k: (0, 0))],
        out_shape=[SDS((S, D), f32), SDS((S, D), bf16), SDS((1, D), f32)],
        scratch_shapes=[pltpu.VMEM((tm, D), f32)], compiler_params=_params(56),
    )(dproj, w_in_g, x, g_row, dx_next)


def _sgu_fwd(proj, ln_g, ln_b, w_s, b_t):
    S = proj.shape[0]
    da = A_GROUPS * HEAD_DIM
    D = 2 * da

    def body(u_ref, v_ref, z_ref, lg_ref, lb_ref, w_ref, bt_ref, y_ref):
        u = _gelu(u_ref[...])
        v = _gelu(v_ref[...])
        z = z_ref[...]
        mu = jnp.mean(v, axis=-1, keepdims=True)
        xc = v - mu
        rs = lax.rsqrt(jnp.mean(xc * xc, axis=-1, keepdims=True) + EPS)
        vn = (xc * rs * lg_ref[...] + lb_ref[...]).astype(bf16)
        gate = u * (z * jax.nn.sigmoid(z))
        tri = lax.broadcasted_iota(jnp.int32, (CHUNK, CHUNK), 0) >= lax.broadcasted_iota(jnp.int32, (CHUNK, CHUNK), 1)
        for g in range(A_GROUPS):
            sl = slice(g * HEAD_DIM, (g + 1) * HEAD_DIM)
            wm = jnp.where(tri, w_ref[g], 0.0).astype(bf16)
            mixed = _dot(wm, vn[:, sl]) + bt_ref[:, g:g + 1]
            y_ref[:, sl] = (gate[:, sl] * mixed).astype(bf16)

    blk = lambda cb: pl.BlockSpec((CHUNK, da), lambda c: (c, cb))
    full = lambda shp: pl.BlockSpec(shp, lambda c: (0,) * len(shp))
    return pl.pallas_call(
        body, name="sgu_fwd", grid=(S // CHUNK,),
        in_specs=[blk(0), blk(1), blk(2), full((1, da)), full((1, da)),
                  full((A_GROUPS, CHUNK, CHUNK)), full((CHUNK, A_GROUPS))],
        out_specs=blk(0), out_shape=SDS((S, D), bf16), compiler_params=_params(),
    )(proj, proj, proj, ln_g, ln_b, w_s, b_t)


def _sgu_bwd(proj, dy, ln_g, ln_b, w_s, b_t):
    S = proj.shape[0]
    da = A_GROUPS * HEAD_DIM
    n_proj = proj.shape[1]

    def body(u_ref, v_ref, z_ref, dy_ref, lg_ref, lb_ref, w_ref, bt_ref,
             dp_ref, dw_ref, db_ref, dlg_ref, dlb_ref, dvn_ref):
        @pl.when(pl.program_id(0) == 0)
        def _():
            dw_ref[...] = jnp.zeros_like(dw_ref)
            db_ref[...] = jnp.zeros_like(db_ref)
            dlg_ref[...] = jnp.zeros_like(dlg_ref)
            dlb_ref[...] = jnp.zeros_like(dlb_ref)

        up, vp, z, dy = u_ref[...], v_ref[...], z_ref[...], dy_ref[...]
        u, gu = _gelu_and_grad(up)
        v, gv = _gelu_and_grad(vp)
        s, gs = _silu_and_grad(z)
        mu = jnp.mean(v, axis=-1, keepdims=True)
        xc = v - mu
        rs = lax.rsqrt(jnp.mean(xc * xc, axis=-1, keepdims=True) + EPS)
        vhat = xc * rs
        lg = lg_ref[...]
        vn = (vhat * lg + lb_ref[...]).astype(bf16)
        tri = lax.broadcasted_iota(jnp.int32, (CHUNK, CHUNK), 0) >= lax.broadcasted_iota(jnp.int32, (CHUNK, CHUNK), 1)
        lane = lax.broadcasted_iota(jnp.int32, (CHUNK, HEAD_DIM), 1)
        dys = dy * s
        db = jnp.zeros((CHUNK, HEAD_DIM), f32)
        for g in range(A_GROUPS):
            sl = slice(g * HEAD_DIM, (g + 1) * HEAD_DIM)
            wm = jnp.where(tri, w_ref[g], 0.0).astype(bf16)
            mixed = _dot(wm, vn[:, sl]) + bt_ref[:, g:g + 1]
            dmix = dys[:, sl] * u[:, sl]
            dp_ref[:, sl] = (dys[:, sl] * mixed * gu[:, sl]).astype(bf16)
            dp_ref[:, 2 * da + g * HEAD_DIM:2 * da + (g + 1) * HEAD_DIM] = (
                dy[:, sl] * u[:, sl] * mixed * gs[:, sl]).astype(bf16)
            dmb = dmix.astype(bf16)
            dw_ref[g] += jnp.where(tri, _dot(dmb, vn[:, sl], NT), 0.0)
            dvn_ref[:, sl] = _dot(wm, dmb, TN)
            db = db + jnp.where(lane == g, jnp.sum(dmix, axis=1, keepdims=True), 0.0)
        db_ref[...] += db
        dvn = dvn_ref[...]
        dlg_ref[...] += jnp.sum(dvn * vhat, axis=0, keepdims=True)
        dlb_ref[...] += jnp.sum(dvn, axis=0, keepdims=True)
        dvhat = dvn * lg
        dv = rs * (dvhat - jnp.mean(dvhat, axis=-1, keepdims=True)
                   - vhat * jnp.mean(dvhat * vhat, axis=-1, keepdims=True))
        dp_ref[:, da:2 * da] = (dv * gv).astype(bf16)

    blk = lambda cb: pl.BlockSpec((CHUNK, da), lambda c: (c, cb))
    full = lambda shp: pl.BlockSpec(shp, lambda c: (0,) * len(shp))
    return pl.pallas_call(
        body, name="sgu_bwd", grid=(S // CHUNK,),
        in_specs=[blk(0), blk(1), blk(2), blk(0), full((1, da)), full((1, da)),
                  full((A_GROUPS, CHUNK, CHUNK)), full((CHUNK, A_GROUPS))],
        out_specs=[pl.BlockSpec((CHUNK, 3 * da), lambda c: (c, 0)), full((A_GROUPS, CHUNK, CHUNK)),
                   full((CHUNK, HEAD_DIM)), full((1, da)), full((1, da))],
        out_shape=[SDS((S, n_proj), bf16), SDS((A_GROUPS, CHUNK, CHUNK), f32), SDS((CHUNK, HEAD_DIM), f32),
                   SDS((1, da), f32), SDS((1, da), f32)],
        scratch_shapes=[pltpu.VMEM((CHUNK, da), f32)], compiler_params=_params(),
    )(proj, proj, proj, dy, ln_g, ln_b, w_s, b_t)


def _sb_scores(q, kblk, kb, rows, cols):
    z = _dot(q, kblk, NT) * (1.0 / math.sqrt(HEAD_DIM))
    causal = (cols + kb * TK) < rows
    t = jnp.log1p(jnp.exp(-jnp.abs(z)))
    log_1mb = jnp.where(causal, -(jnp.maximum(z, 0.0) + t), 0.0)
    log_beta = jnp.minimum(z, 0.0) - t
    return causal, log_beta, log_1mb


def _suffix_sum(log_1mb, upper):
    hi = log_1mb.astype(bf16)
    lo = (log_1mb - hi.astype(f32)).astype(bf16)
    return _dot(hi, upper) + _dot(lo, upper)


def _sb_fwd(proj, y_prev, col0):
    S = proj.shape[0]
    D = y_prev.shape[1]
    ycol0 = (A_GROUPS * HEAD_DIM) // HEAD_DIM

    def body(q_ref, k_ref, v_ref, z_ref, yp_ref, y_ref, o_ref, car_ref):
        i = pl.program_id(1)
        q = q_ref[...].astype(bf16)
        nkb = (i + 1) * (TQ // TK)
        rows = i * TQ + lax.broadcasted_iota(jnp.int32, (TQ, TK), 0)
        cols = lax.broadcasted_iota(jnp.int32, (TQ, TK), 1)
        cols_rowwise = cols + lax.shift_right_arithmetic(rows, 31)
        upper = (lax.broadcasted_iota(jnp.int32, (TK, TK), 0) > lax.broadcasted_iota(jnp.int32, (TK, TK), 1)).astype(bf16)

        def step(jj, carry):
            c, acc, car = carry
            kb = nkb - 1 - jj
            off = pl.multiple_of(kb * TK, TK)
            kblk = k_ref[pl.ds(off, TK), :].astype(bf16)
            vblk = v_ref[pl.ds(off, TK), :].astype(bf16)
            causal, log_beta, log_1mb = _sb_scores(q, kblk, kb, rows, cols)
            after = _suffix_sum(log_1mb, upper) + c
            a = jnp.where(causal, jnp.exp(log_beta + after), 0.0)
            acc = acc + _dot(a.astype(bf16), vblk)
            car = jnp.where(cols_rowwise == kb, c, car)
            c = c + jnp.sum(log_1mb, axis=1, keepdims=True)
            return c, acc, car

        init = (jnp.zeros((TQ, TK), f32), jnp.zeros((TQ, HEAD_DIM), f32), jnp.zeros((TQ, TK), f32))
        _, acc, car = lax.fori_loop(0, nkb, step, init)
        o_ref[...] = acc
        car_ref[...] = car
        z = z_ref[...]
        y_ref[...] = (acc * (z * jax.nn.sigmoid(z))).astype(bf16)

    qspec = lambda cb: pl.BlockSpec((TQ, HEAD_DIM), lambda h, i: (i, col0 + cb + h))
    kspec = lambda cb: pl.BlockSpec((S, HEAD_DIM), lambda h, i: (0, col0 + cb + h))
    return pl.pallas_call(
        body, name="sb_fwd", grid=(N_HEADS, S // TQ),
        in_specs=[qspec(0), kspec(4), kspec(8), qspec(12), pl.BlockSpec(memory_space=pl.ANY)],
        out_specs=[pl.BlockSpec((TQ, HEAD_DIM), lambda h, i: (i, ycol0 + h)),
                   pl.BlockSpec((TQ, HEAD_DIM), lambda h, i: (i, h)),
                   pl.BlockSpec((None, TQ, TK), lambda h, i: (h, i, 0))],
        out_shape=[SDS((S, D), bf16), SDS((S, N_HEADS * HEAD_DIM), f32), SDS((N_HEADS, S, TK), f32)],
        input_output_aliases={4: 0}, compiler_params=_params(),
    )(proj, proj, proj, proj, y_prev)


def _sb_bwd(proj, o, car, dy, dproj_prev, col0):
    S = proj.shape[0]
    n_i = S // TQ
    ycol0 = (A_GROUPS * HEAD_DIM) // HEAD_DIM

    def body(q_ref, k_ref, v_ref, z_ref, o_ref, car_ref, dy_ref, dpp_ref,
             dp_ref, dq_ref, dz_ref, dk_ref, dv_ref, dk_acc, dv_acc):
        h, i = pl.program_id(0), pl.program_id(1)

        def put(stage_ref, row0, nrows, cb):
            col = pl.multiple_of((col0 + cb + h) * HEAD_DIM, HEAD_DIM)
            pltpu.sync_copy(stage_ref, dp_ref.at[pl.ds(row0, nrows), pl.ds(col, HEAD_DIM)])

        @pl.when(i == 0)
        def _():
            dk_acc[...] = jnp.zeros_like(dk_acc)
            dv_acc[...] = jnp.zeros_like(dv_acc)

        s, gs = _silu_and_grad(z_ref[...])
        dy = dy_ref[...]
        dz_ref[...] = (dy * o_ref[...] * gs).astype(bf16)
        dob = (dy * s).astype(bf16)
        q = q_ref[...].astype(bf16)
        car = car_ref[...]
        nkb = (i + 1) * (TQ // TK)
        rows = i * TQ + lax.broadcasted_iota(jnp.int32, (TQ, TK), 0)
        cols = lax.broadcasted_iota(jnp.int32, (TQ, TK), 1)
        r_i = lax.broadcasted_iota(jnp.int32, (TK, TK), 0)
        c_i = lax.broadcasted_iota(jnp.int32, (TK, TK), 1)
        upper = (r_i > c_i).astype(bf16)
        lower = (r_i < c_i).astype(bf16)
        scale = 1.0 / math.sqrt(HEAD_DIM)

        def step(kb, carry):
            p, dq = carry
            off = pl.multiple_of(kb * TK, TK)
            kblk = k_ref[pl.ds(off, TK), :].astype(bf16)
            vblk = v_ref[pl.ds(off, TK), :].astype(bf16)
            causal, log_beta, log_1mb = _sb_scores(q, kblk, kb, rows, cols)
            c = jnp.sum(jnp.where(cols == kb, car, 0.0), axis=1, keepdims=True)
            after = _suffix_sum(log_1mb, upper) + c
            a = jnp.where(causal, jnp.exp(log_beta + after), 0.0)
            g = _dot(dob, vblk, NT) * a
            g_hi = g.astype(bf16)
            g_lo = (g - g_hi.astype(f32)).astype(bf16)
            prefix = _dot(g_hi, lower) + _dot(g_lo, lower) + p
            beta = jnp.exp(log_beta)
            dzz = (jnp.where(causal, g * (1.0 - beta) - beta * prefix, 0.0) * scale).astype(bf16)
            dq = dq + _dot(dzz, kblk)
            dk_acc[pl.ds(off, TK), :] += _dot(dzz, q, TN)
            dv_acc[pl.ds(off, TK), :] += _dot(a.astype(bf16), dob, TN)
            p = p + jnp.sum(g, axis=1, keepdims=True)
            return p, dq

        _, dq = lax.fori_loop(0, nkb, step, (jnp.zeros((TQ, TK), f32), jnp.zeros((TQ, HEAD_DIM), f32)))
        dq_ref[...] = dq.astype(bf16)
        row0 = pl.multiple_of(i * TQ, TQ)
        put(dq_ref, row0, TQ, 0)
        put(dz_ref, row0, TQ, 12)

        @pl.when(i == n_i - 1)
        def _():
            dk_ref[...] = dk_acc[...].astype(bf16)
            dv_ref[...] = dv_acc[...].astype(bf16)
            put(dk_ref, 0, S, 4)
            put(dv_ref, 0, S, 8)

    qspec = lambda cb: pl.BlockSpec((TQ, HEAD_DIM), lambda h, i: (i, col0 + cb + h))
    kspec = lambda cb: pl.BlockSpec((S, HEAD_DIM), lambda h, i: (0, col0 + cb + h))
    return pl.pallas_call(
        body, name="sb_bwd", grid=(N_HEADS, n_i),
        in_specs=[qspec(0), kspec(4), kspec(8), qspec(12),
                  pl.BlockSpec((TQ, HEAD_DIM), lambda h, i: (i, h)),
                  pl.BlockSpec((None, TQ, TK), lambda h, i: (h, i, 0)),
                  pl.BlockSpec((TQ, HEAD_DIM), lambda h, i: (i, ycol0 + h)),
                  pl.BlockSpec(memory_space=pl.ANY)],
        out_specs=pl.BlockSpec(memory_space=pl.ANY),
        out_shape=SDS(dproj_prev.shape, bf16),
        input_output_aliases={7: 0},
        scratch_shapes=[pltpu.VMEM((TQ, HEAD_DIM), bf16), pltpu.VMEM((TQ, HEAD_DIM), bf16),
                        pltpu.VMEM((S, HEAD_DIM), bf16), pltpu.VMEM((S, HEAD_DIM), bf16),
                        pltpu.VMEM((S, HEAD_DIM), f32), pltpu.VMEM((S, HEAD_DIM), f32)],
        compiler_params=_params(),
    )(proj, proj, proj, proj, o, car, dy, dproj_prev)


def _mem_kv(mem, mg_row, w_kv_g, l):
    M, D = mem.shape
    rb, n = w_kv_g.shape[2], w_kv_g.shape[3]

    def body(m_ref, g_ref, w_ref, kv_ref):
        mv = m_ref[...]
        r = lax.rsqrt(jnp.mean(mv * mv, axis=-1, keepdims=True) + EPS)
        mh = (mv * r * g_ref[...]).astype(bf16)
        kv_ref[...] = _dot(mh, w_ref[...].reshape(N_DEV * rb, n))

    return pl.pallas_call(
        body, name="mem_kv", grid=(1,),
        in_specs=[pl.BlockSpec((M, D), lambda i: (0, 0)), pl.BlockSpec((1, D), lambda i: (0, 0)),
                  pl.BlockSpec((N_DEV, None, rb, n), lambda i: (0, l, 0, 0))],
        out_specs=pl.BlockSpec((M, n), lambda i: (0, 0)),
        out_shape=SDS((M, n), f32), compiler_params=_params(),
    )(mem, mg_row, w_kv_g)


def _xattn_head(q_ref, kv_ref, qg, kg, h):
    dc = N_HEADS * HEAD_DIM
    sl = slice(h * HEAD_DIM, (h + 1) * HEAD_DIM)
    qh = q_ref[:, sl]
    rq = lax.rsqrt(jnp.mean(qh * qh, axis=-1, keepdims=True) + EPS)
    qhat = qh * rq
    qn = (qhat * qg).astype(bf16)
    kh = kv_ref[:, sl]
    rk = lax.rsqrt(jnp.mean(kh * kh, axis=-1, keepdims=True) + EPS)
    kn = (kh * rk * kg).astype(bf16)
    vh = kv_ref[:, dc + h * HEAD_DIM:dc + (h + 1) * HEAD_DIM].astype(bf16)
    s = _dot(qn, kn, NT) * (1.0 / math.sqrt(HEAD_DIM))
    e = jnp.exp(s - jnp.max(s, axis=-1, keepdims=True))
    p = e / jnp.sum(e, axis=-1, keepdims=True)
    o = _dot(p.astype(bf16), vh)
    return sl, rq, qhat, qn, kn, vh, p, o


def _xattn_fwd(proj, kv, qg_row, kg_row, y_prev, col0, tq=512):
    S = proj.shape[0]
    D = y_prev.shape[1]
    dc = N_HEADS * HEAD_DIM
    M = kv.shape[0]

    def body(q_ref, z_ref, kv_ref, qg_ref, kg_ref, yp_ref, y_ref):
        for h in range(N_HEADS):
            sl, _, _, _, _, _, _, o = _xattn_head(q_ref, kv_ref, qg_ref[...], kg_ref[...], h)
            z = z_ref[:, sl]
            y_ref[:, sl] = (o * (z * jax.nn.sigmoid(z))).astype(bf16)

    full = lambda shp: pl.BlockSpec(shp, lambda i: (0,) * len(shp))
    return pl.pallas_call(
        body, name="xattn_fwd", grid=(S // tq,),
        in_specs=[pl.BlockSpec((tq, dc), lambda i: (i, col0)), pl.BlockSpec((tq, dc), lambda i: (i, col0 + 1)),
                  full((M, 2 * dc)), full((1, HEAD_DIM)), full((1, HEAD_DIM)), pl.BlockSpec(memory_space=pl.ANY)],
        out_specs=pl.BlockSpec((tq, dc), lambda i: (i, D // dc - 1)),
        out_shape=SDS((S, D), bf16), input_output_aliases={5: 0}, compiler_params=_params(),
    )(proj, proj, kv, qg_row, kg_row, y_prev)


def _xattn_bwd(proj, kv, qg_row, kg_row, dy, dproj_prev, col0, tq=512):
    S = proj.shape[0]
    D = dy.shape[1]
    dc = N_HEADS * HEAD_DIM
    M = kv.shape[0]

    def body(q_ref, z_ref, kv_ref, qg_ref, kg_ref, dy_ref, dpp_ref, dp_ref, dkn_ref, dv_ref, dqg_ref):
        @pl.when(pl.program_id(0) == 0)
        def _():
            dkn_ref[...] = jnp.zeros_like(dkn_ref)
            dv_ref[...] = jnp.zeros_like(dv_ref)
            dqg_ref[...] = jnp.zeros_like(dqg_ref)

        qg = qg_ref[...]
        for h in range(N_HEADS):
            sl, rq, qhat, qn, kn, vh, p, o = _xattn_head(q_ref, kv_ref, qg, kg_ref[...], h)
            s, gs = _silu_and_grad(z_ref[:, sl])
            dyh = dy_ref[:, sl]
            dp_ref[:, dc + h * HEAD_DIM:dc + (h + 1) * HEAD_DIM] = (dyh * o * gs).astype(bf16)
            dob = (dyh * s).astype(bf16)
            dpr = _dot(dob, vh, NT)
            dv_ref[:, sl] += _dot(p.astype(bf16), dob, TN)
            ds = (p * (dpr - jnp.sum(p * dpr, axis=-1, keepdims=True)) * (1.0 / math.sqrt(HEAD_DIM))).astype(bf16)
            dqn = _dot(ds, kn)
            dkn_ref[:, sl] += _dot(ds, qn, TN)
            dqg_ref[...] += jnp.sum(dqn * qhat, axis=0, keepdims=True)
            dqhat = dqn * qg
            dp_ref[:, sl] = (rq * (dqhat - qhat * jnp.mean(dqhat * qhat, axis=-1, keepdims=True))).astype(bf16)

    full = lambda shp: pl.BlockSpec(shp, lambda i: (0,) * len(shp))
    return pl.pallas_call(
        body, name="xattn_bwd", grid=(S // tq,),
        in_specs=[pl.BlockSpec((tq, dc), lambda i: (i, col0)), pl.BlockSpec((tq, dc), lambda i: (i, col0 + 1)),
                  full((M, 2 * dc)), full((1, HEAD_DIM)), full((1, HEAD_DIM)),
                  pl.BlockSpec((tq, dc), lambda i: (i, D // dc - 1)), pl.BlockSpec(memory_space=pl.ANY)],
        out_specs=[pl.BlockSpec((tq, 2 * dc), lambda i: (i, col0 // 2)), full((M, dc)), full((M, dc)),
                   full((1, HEAD_DIM))],
        out_shape=[SDS(dproj_prev.shape, bf16), SDS((M, dc), f32), SDS((M, dc), f32), SDS((1, HEAD_DIM), f32)],
        input_output_aliases={6: 0}, compiler_params=_params(),
    )(proj, proj, kv, qg_row, kg_row, dy, dproj_prev)


def _mem_bwd(mem, mg_row, kv, dkn, dv, kg_row, w_kv_g, g_prev, l):
    M, D = mem.shape
    rb, n = w_kv_g.shape[2], w_kv_g.shape[3]
    dc = n // 2

    def body(m_ref, g_ref, kv_ref, dkn_ref, dv_ref, kg_ref, w_ref, *rest):
        dw_ref, dmg_ref, dkg_ref, dkv_ref = rest[-4:]
        mv = m_ref[...]
        r = lax.rsqrt(jnp.mean(mv * mv, axis=-1, keepdims=True) + EPS)
        mhat = mv * r
        mh = (mhat * g_ref[...]).astype(bf16)
        kg = kg_ref[...]
        dkg = jnp.zeros((1, HEAD_DIM), f32)
        for h in range(N_HEADS):
            sl = slice(h * HEAD_DIM, (h + 1) * HEAD_DIM)
            kh = kv_ref[:, sl]
            rk = lax.rsqrt(jnp.mean(kh * kh, axis=-1, keepdims=True) + EPS)
            khat = kh * rk
            dkn_h = dkn_ref[:, sl]
            dkg = dkg + jnp.sum(dkn_h * khat, axis=0, keepdims=True)
            dkhat = dkn_h * kg
            dkv_ref[:, sl] = (rk * (dkhat - khat * jnp.mean(dkhat * khat, axis=-1, keepdims=True))).astype(bf16)
        dkv_ref[:, dc:] = dv_ref[...].astype(bf16)
        dkg_ref[...] = dkg
        dkv = dkv_ref[...]
        dw_ref[...] = _dot(mh, dkv, TN).astype(bf16).reshape(N_DEV, rb, n)
        dmh = _dot(dkv, w_ref[...].reshape(N_DEV * rb, n), NT)
        dmg_ref[...] = jnp.sum(dmh * mhat, axis=0, keepdims=True)

    full = lambda shp: pl.BlockSpec(shp, lambda i: (0,) * len(shp))
    wspec = pl.BlockSpec((N_DEV, None, rb, n), lambda i: (0, l, 0, 0))
    in_specs = [full((M, D)), full((1, D)), full((M, n)), full((M, dc)), full((M, dc)), full((1, HEAD_DIM)), wspec]
    args = [mem, mg_row, kv, dkn, dv, kg_row, w_kv_g]
    aliases = {}
    if g_prev is not None:
        in_specs.append(pl.BlockSpec(memory_space=pl.ANY))
        args.append(g_prev)
        aliases = {7: 0}
    return pl.pallas_call(
        body, name="mem_bwd", grid=(1,), in_specs=in_specs,
        out_specs=[wspec, full((1, D)), full((1, HEAD_DIM))],
        out_shape=[SDS((N_DEV, 2, rb, n), bf16), SDS((1, D), f32), SDS((1, HEAD_DIM), f32)],
        input_output_aliases=aliases, scratch_shapes=[pltpu.VMEM((M, n), bf16)], compiler_params=_params(),
    )(*args)


SMALL = ("norm_g", "sgu_ln_g", "sgu_ln_b", "sgu_w", "sgu_b", "mem_norm_g", "q_norm_g", "k_norm_g")


def _pack_small(parts):
    flat = jnp.concatenate([parts[n].reshape(-1) for n in SMALL])
    pad = (-flat.shape[0]) % (8 * 128)
    return jnp.pad(flat, (0, pad)).reshape(-1, 128)


def _unpack_small(packed, like):
    flat = packed.reshape(-1)
    out, off = {}, 0
    for n in SMALL:
        size = math.prod(like[n].shape)
        out[n] = flat[off:off + size].reshape(like[n].shape)
        off += size
    return out


def kernel(x, mem, norm_g, w_in, sgu_ln_g, sgu_ln_b, sgu_w, sgu_b, mem_norm_g, w_mem_kv, q_norm_g, k_norm_g, w_out, loss_target, m_norm_g, m_w_in, m_sgu_ln_g, m_sgu_ln_b, m_sgu_w, m_sgu_b, m_mem_norm_g, m_w_mem_kv, m_q_norm_g, m_k_norm_g, m_w_out, v_norm_g, v_w_in, v_sgu_ln_g, v_sgu_ln_b, v_sgu_w, v_sgu_b, v_mem_norm_g, v_w_mem_kv, v_q_norm_g, v_k_norm_g, v_w_out):
    L, D, wc = w_in.shape
    S = x.shape[1]
    da = D // 2
    xs = x.reshape(S, D)
    mems = mem.reshape(mem.shape[1], D)
    tgt = loss_target.reshape(S, D)

    w_in_b = _cast_bf16("cast_w_in", w_in.reshape(L * D, wc), 512).reshape(L, D, wc)
    w_kv_b = _cast_bf16("cast_w_kv", w_mem_kv.reshape(-1, w_mem_kv.shape[2]), 256).reshape(w_mem_kv.shape)
    w_out_b = _cast_bf16("cast_w_out", w_out.reshape(-1, D), 256).reshape(w_out.shape)
    w_in_g, w_kv_g, w_out_g = _exchange("gather_weights", [w_in_b, w_kv_b, w_out_b], scatter=False)

    row = lambda a, l: a[l].reshape(1, -1)
    acts = []
    xl = xs
    for l in range(L):
        proj, h = _rms_proj(xl, row(norm_g, l), w_in_g, l)
        y = _sgu_fwd(proj, row(sgu_ln_g, l), row(sgu_ln_b, l), sgu_w[l], sgu_b[l].T)
        y, o_b, car = _sb_fwd(proj, y, 3 * da // HEAD_DIM)
        kv = _mem_kv(mems, row(mem_norm_g, l), w_kv_g, l)
        y = _xattn_fwd(proj, kv, row(q_norm_g, l), row(k_norm_g, l), y, (3 * da + D) // (D // 4))
        x_next = _out_proj(xl, y, w_out_g, l)
        acts.append((xl, proj, h, y, o_b, car, kv))
        xl = x_next

    dx, dxb, loss_part = _loss_and_grad(xl, tgt, 512)
    loss = lax.psum(loss_part[0, 0], ("x", "y", "c"))

    g_in = g_kv = g_out = None
    small = {n: [None] * L for n in SMALL}
    for l in reversed(range(L)):
        xl, proj, h, y, o_b, car, kv = acts[l]
        dy = _out_bwd_dy(dxb, w_out_g, l)
        g_out = _tn_grad("out_bwd_dw", y, dxb, g_out, l, D // N_DEV, 512, rows_major=True)
        dproj, d_sw, d_sb, d_lg, d_lb = _sgu_bwd(proj, dy, row(sgu_ln_g, l), row(sgu_ln_b, l), sgu_w[l], sgu_b[l].T)
        dproj = _sb_bwd(proj, o_b, car, dy, dproj, 3 * da // HEAD_DIM)
        dproj, dkn, dv, d_qg = _xattn_bwd(proj, kv, row(q_norm_g, l), row(k_norm_g, l), dy, dproj,
                                          (3 * da + D) // (D // 4))
        g_kv, d_mg, d_kg = _mem_bwd(mems, row(mem_norm_g, l), kv, dkn, dv, row(k_norm_g, l), w_kv_g, g_kv, l)
        g_in = _tn_grad("in_bwd_dw", h, dproj, g_in, l, 512, wc, rows_major=False)
        dx, dxb, d_ng = _in_bwd_dx(dproj, w_in_g, l, xl, row(norm_g, l), dx)
        for n, val in (("norm_g", d_ng), ("sgu_ln_g", d_lg), ("sgu_ln_b", d_lb), ("sgu_w", d_sw),
                       ("sgu_b", d_sb[:, :A_GROUPS].T), ("mem_norm_g", d_mg), ("q_norm_g", d_qg), ("k_norm_g", d_kg)):
            small[n][l] = val.reshape(-1)

    r_in, r_kv, r_out = _exchange("scatter_grads", [g_in, g_kv, g_out], scatter=True)
    weights = dict(norm_g=norm_g, sgu_ln_g=sgu_ln_g, sgu_ln_b=sgu_ln_b, sgu_w=sgu_w, sgu_b=sgu_b,
                   mem_norm_g=mem_norm_g, q_norm_g=q_norm_g, k_norm_g=k_norm_g)
    moms_m = dict(norm_g=m_norm_g, sgu_ln_g=m_sgu_ln_g, sgu_ln_b=m_sgu_ln_b, sgu_w=m_sgu_w, sgu_b=m_sgu_b,
                  mem_norm_g=m_mem_norm_g, q_norm_g=m_q_norm_g, k_norm_g=m_k_norm_g)
    moms_v = dict(norm_g=v_norm_g, sgu_ln_g=v_sgu_ln_g, sgu_ln_b=v_sgu_ln_b, sgu_w=v_sgu_w, sgu_b=v_sgu_b,
                  mem_norm_g=v_mem_norm_g, q_norm_g=v_q_norm_g, k_norm_g=v_k_norm_g)
    part = _pack_small({n: jnp.stack(small[n]).reshape(weights[n].shape) for n in SMALL})
    (r_small,) = _exchange("gather_small_grads", [part], scatter=False)

    res = {}
    res["w_in"] = _sum_adam("adam_w_in", r_in, w_in, m_w_in, v_w_in, 256)
    res["w_mem_kv"] = _sum_adam("adam_w_kv", r_kv, w_mem_kv, m_w_mem_kv, v_w_mem_kv, 256)
    res["w_out"] = _sum_adam("adam_w_out", r_out, w_out, m_w_out, v_w_out, 256)
    sm = _small_sum_adam(r_small, _pack_small(weights), _pack_small(moms_m), _pack_small(moms_v))
    sm = [_unpack_small(a, weights) for a in sm]
    for n in SMALL:
        res[n] = [a[n] for a in sm]

    order = ("norm_g", "w_in", "sgu_ln_g", "sgu_ln_b", "sgu_w", "sgu_b", "mem_norm_g", "w_mem_kv", "q_norm_g",
             "k_norm_g", "w_out")
    outs = [loss, dx.reshape(x.shape)]
    for k in range(4):
        outs += [res[n][k] for n in order]
    return tuple(outs)
```

```python
import functools
import math

import jax
import jax.numpy as jnp
from jax import lax
from jax.experimental import pallas as pl
from jax.experimental.pallas import tpu as pltpu

f32 = jnp.float32
bf16 = jnp.bfloat16
SDS = jax.ShapeDtypeStruct

N_DEV = 8
EPS = 1e-6
CHUNK = 128
A_GROUPS = 8
HEAD_DIM = 128
N_HEADS = 4
TQ = 256
TK = 128
ADAM_LR, ADAM_B1, ADAM_B2, ADAM_EPS, ADAM_WD, ADAM_STEP = 0.001, 0.9, 0.999, 1e-08, 0.01, 10
MIB = 1024 * 1024

NT = (((1,), (1,)), ((), ()))
TN = (((0,), (0,)), ((), ()))


def _params(vmem_mib=48):
    return pltpu.CompilerParams(vmem_limit_bytes=vmem_mib * MIB)


def _gelu_and_grad(x):
    e = lax.erf(x * (1.0 / math.sqrt(2.0)))
    cdf = 0.5 * (1.0 + e)
    pdf = jnp.exp(-0.5 * x * x) * (1.0 / math.sqrt(2.0 * math.pi))
    return x * cdf, cdf + x * pdf


def _gelu(x):
    return 0.5 * x * (1.0 + lax.erf(x * (1.0 / math.sqrt(2.0))))


def _silu_and_grad(z):
    sg = jax.nn.sigmoid(z)
    return z * sg, sg * (1.0 + z * (1.0 - sg))


def _dot(a, b, dims=None):
    if dims is None:
        return jnp.dot(a, b, preferred_element_type=f32)
    return lax.dot_general(a, b, dims, preferred_element_type=f32)


def _exchange_copies(src_refs, dst_refs, send_sems, recv_sems, loc_sems, scatter):
    x, y, c = lax.axis_index("x"), lax.axis_index("y"), lax.axis_index("c")
    me = 4 * x + 2 * y + c
    locals_, remotes = [], []
    for a, (s, d) in enumerate(zip(src_refs, dst_refs)):
        locals_.append(pltpu.make_async_copy(s.at[me] if scatter else s, d.at[me], loc_sems.at[a]))
        for m in range(1, N_DEV):
            px = 1 - x if m & 4 else x
            py = 1 - y if m & 2 else y
            pc = 1 - c if m & 1 else c
            peer = 4 * px + 2 * py + pc
            remotes.append(pltpu.make_async_remote_copy(
                src_ref=s.at[peer] if scatter else s, dst_ref=d.at[me],
                send_sem=send_sems.at[a * 7 + m - 1], recv_sem=recv_sems.at[a * 7 + m - 1],
                device_id=(px, py, pc), device_id_type=pl.DeviceIdType.MESH))
    return locals_, remotes


def _exchange_shapes(srcs, scatter):
    return [(s.shape if scatter else (N_DEV,) + s.shape) for s in srcs]


def _exchange(name, srcs, scatter):
    n = len(srcs)

    def body(*refs):
        locals_, remotes = _exchange_copies(refs[:n], refs[n:2 * n], *refs[2 * n:], scatter)
        for cp in locals_ + remotes:
            cp.start()
        for cp in remotes:
            cp.wait_recv()
        for cp in remotes:
            cp.wait_send()
        for cp in locals_:
            cp.wait()

    any_spec = pl.BlockSpec(memory_space=pl.ANY)
    outs = pl.pallas_call(
        body, name=name, out_shape=[SDS(shp, s.dtype) for shp, s in zip(_exchange_shapes(srcs, scatter), srcs)],
        in_specs=[any_spec] * n, out_specs=[any_spec] * n,
        scratch_shapes=[pltpu.SemaphoreType.DMA((n * 7,)), pltpu.SemaphoreType.DMA((n * 7,)),
                        pltpu.SemaphoreType.DMA((n,))],
    )(*srcs)
    return list(outs)


_HBM = pl.BlockSpec(memory_space=pltpu.HBM)
_SEM = pl.BlockSpec(memory_space=pltpu.SEMAPHORE)
_EFFECT = pltpu.SideEffectType.DATAFLOW_SIDE_EFFECTING


def _exchange_start(name, srcs, scatter):
    n = len(srcs)
    shapes = _exchange_shapes(srcs, scatter)

    def body(*refs):
        token = refs[-1]
        locals_, remotes = _exchange_copies(refs[:n], refs[n:2 * n], *refs[2 * n:2 * n + 3], scatter)
        for cp in locals_ + remotes:
            cp.start()
        token[...] = jnp.zeros_like(token)

    hbm = lambda a: pltpu.with_memory_space_constraint(a, pltpu.HBM)
    outs = pl.pallas_call(
        body, name=name,
        out_shape=(pltpu.SemaphoreType.DMA((n * 7,)), pltpu.SemaphoreType.DMA((n * 7,)), pltpu.SemaphoreType.DMA((n,)),
                   *[pltpu.HBM(s.shape, s.dtype) for s in srcs],
                   *[pltpu.HBM(shp, s.dtype) for shp, s in zip(shapes, srcs)],
                   SDS((8, 128), f32)),
        in_specs=[_HBM] * (2 * n),
        out_specs=(_SEM, _SEM, _SEM, *[_HBM] * (2 * n), pl.BlockSpec(memory_space=pltpu.VMEM)),
        input_output_aliases={k: 3 + k for k in range(2 * n)},
        compiler_params=pltpu.CompilerParams(has_side_effects=_EFFECT),
    )(*[hbm(s) for s in srcs], *[hbm(lax.empty(shp, s.dtype)) for shp, s in zip(shapes, srcs)])
    return dict(name=name, n=n, scatter=scatter, sems=outs[:3], bufs=outs[3:3 + 2 * n], token=outs[-1])


def _exchange_wait(handle, after):
    n, scatter = handle["n"], handle["scatter"]

    def body(*refs):
        locals_, remotes = _exchange_copies(refs[:n], refs[n:2 * n], *refs[2 * n:2 * n + 3], scatter)
        for cp in remotes:
            cp.wait_recv()
        for cp in remotes:
            cp.wait_send()
        for cp in locals_:
            cp.wait()

    bufs = handle["bufs"]
    outs = pl.pallas_call(
        body, name=handle["name"] + "_wait",
        out_shape=tuple(pltpu.HBM(b.shape, b.dtype) for b in bufs),
        in_specs=[_HBM] * (2 * n) + [_SEM] * 3 + [pl.BlockSpec(memory_space=pl.ANY)],
        out_specs=tuple([_HBM] * (2 * n)),
        input_output_aliases={k: k for k in range(2 * n)},
        compiler_params=pltpu.CompilerParams(has_side_effects=_EFFECT),
    )(*bufs, *handle["sems"], after)
    return list(outs[n:])


def _cast_bf16(name, w, l, tr):
    _, R, C = w.shape

    def body(w_ref, o_ref):
        o_ref[...] = w_ref[...].astype(bf16)

    return pl.pallas_call(
        body, name=name, grid=(R // tr,),
        in_specs=[pl.BlockSpec((None, tr, C), lambda i: (l, i, 0))],
        out_specs=pl.BlockSpec((tr, C), lambda i: (i, 0)),
        out_shape=SDS((R, C), bf16), compiler_params=_params(),
    )(w)


def _adam_math(w, g, m, v):
    m2 = ADAM_B1 * m + (1.0 - ADAM_B1) * g
    v2 = ADAM_B2 * v + (1.0 - ADAM_B2) * (g * g)
    m_hat = m2 / (1.0 - ADAM_B1 ** ADAM_STEP)
    v_hat = v2 / (1.0 - ADAM_B2 ** ADAM_STEP)
    delta = -ADAM_LR * (m_hat / (jnp.sqrt(v_hat) + ADAM_EPS) + ADAM_WD * w)
    return delta, m2, v2


def _sum_adam(name, recvs, w, m, v, tr):
    L, R, C = w.shape
    n_i = R // tr

    def body(*refs):
        r_refs = refs[:L]
        w_ref, m_ref, v_ref, g_ref, d_ref, m2_ref, v2_ref = refs[L:]
        for lay in range(L):
            @pl.when(pl.program_id(0) == lay)
            def _(r_ref=r_refs[lay]):
                g = r_ref[0].astype(f32)
                for s in range(1, N_DEV):
                    g = g + r_ref[s].astype(f32)
                d, m2, v2 = _adam_math(w_ref[...], g, m_ref[...], v_ref[...])
                g_ref[...] = g
                d_ref[...] = d
                m2_ref[...] = m2
                v2_ref[...] = v2

    def rspec(lay):
        return pl.BlockSpec((N_DEV, tr, C), lambda l, i: (0, jnp.where(l == lay, i, jnp.where(l > lay, n_i - 1, 0)), 0))

    wspec = pl.BlockSpec((None, tr, C), lambda l, i: (l, i, 0))
    return pl.pallas_call(
        body, name=name, grid=(L, n_i),
        in_specs=[rspec(lay) for lay in range(L)] + [wspec, wspec, wspec],
        out_specs=[wspec] * 4, out_shape=[SDS((L, R, C), f32)] * 4, compiler_params=_params(),
    )(*recvs, w, m, v)


def _small_sum_adam(recv, w, m, v):
    _, R, C = recv.shape

    def body(r_ref, w_ref, m_ref, v_ref, g_ref, d_ref, m2_ref, v2_ref):
        g = r_ref[0]
        for s in range(1, N_DEV):
            g = g + r_ref[s]
        d, m2, v2 = _adam_math(w_ref[...], g, m_ref[...], v_ref[...])
        g_ref[...] = g
        d_ref[...] = d
        m2_ref[...] = m2
        v2_ref[...] = v2

    return pl.pallas_call(
        body, name="small_sum_adam", out_shape=[SDS((R, C), f32)] * 4, compiler_params=_params(),
    )(recv, w, m, v)


def _loss_and_grad(xf, tgt, tm):
    S, D = xf.shape

    def body(x_ref, t_ref, dx_ref, dxb_ref, l_ref):
        i = pl.program_id(0)
        d = x_ref[...] - t_ref[...]
        dx = d * (1.0 / D)
        dx_ref[...] = dx
        dxb_ref[...] = dx.astype(bf16)
        e = d * d
        part = e[:, 0:128]
        for k in range(1, D // 128):
            part = part + e[:, k * 128:(k + 1) * 128]
        part = jnp.sum(part.reshape(tm // 8, 8, 128), axis=0)

        @pl.when(i == 0)
        def _():
            l_ref[...] = jnp.zeros_like(l_ref)

        l_ref[...] += part

        @pl.when(i == pl.num_programs(0) - 1)
        def _():
            tot = jnp.sum(l_ref[...], axis=1, keepdims=True)
            tot = jnp.sum(tot, axis=0, keepdims=True)
            l_ref[...] = jnp.broadcast_to(tot * (0.5 / D), l_ref.shape)

    row = pl.BlockSpec((tm, D), lambda i: (i, 0))
    return pl.pallas_call(
        body, name="loss_grad", grid=(S // tm,),
        in_specs=[row, row], out_specs=[row, row, pl.BlockSpec((8, 128), lambda i: (0, 0))],
        out_shape=[SDS((S, D), f32), SDS((S, D), bf16), SDS((8, 128), f32)], compiler_params=_params(),
    )(xf, tgt)


def _rms_proj(x, g_row, w_in_g, tm=512):
    S, D = x.shape
    wc = w_in_g.shape[2]
    tn = 256
    per = wc // tn
    n_out = N_DEV * wc

    def body(x_ref, g_ref, w_ref, proj_ref, h_ref):
        @pl.when(pl.program_id(1) == 0)
        def _():
            xv = x_ref[...]
            r = lax.rsqrt(jnp.mean(xv * xv, axis=-1, keepdims=True) + EPS)
            h_ref[...] = (xv * r * g_ref[...]).astype(bf16)

        proj_ref[...] = _dot(h_ref[...], w_ref[...])

    return pl.pallas_call(
        body, name="rms_proj", grid=(S // tm, n_out // tn),
        in_specs=[pl.BlockSpec((tm, D), lambda i, j: (i, 0)),
                  pl.BlockSpec((1, D), lambda i, j: (0, 0)),
                  pl.BlockSpec((None, D, tn), lambda i, j: (j // per, 0, j % per))],
        out_specs=[pl.BlockSpec((tm, tn), lambda i, j: (i, j)), pl.BlockSpec((tm, D), lambda i, j: (i, 0))],
        out_shape=[SDS((S, n_out), f32), SDS((S, D), bf16)], compiler_params=_params(),
    )(x, g_row, w_in_g)


def _out_proj(x, y, w_out_g, tm=512, tn=512):
    S, D = x.shape
    rb = w_out_g.shape[1]

    def body(x_ref, y_ref, w_ref, o_ref):
        w = w_ref[...].reshape(N_DEV * rb, tn)
        o_ref[...] = x_ref[...] + _dot(y_ref[...], w)

    return pl.pallas_call(
        body, name="out_proj", grid=(S // tm, D // tn),
        in_specs=[pl.BlockSpec((tm, tn), lambda i, j: (i, j)),
                  pl.BlockSpec((tm, D), lambda i, j: (i, 0)),
                  pl.BlockSpec((N_DEV, rb, tn), lambda i, j: (0, 0, j))],
        out_specs=pl.BlockSpec((tm, tn), lambda i, j: (i, j)),
        out_shape=SDS((S, D), f32), compiler_params=_params(),
    )(x, y, w_out_g)


def _out_bwd_dy(dxb, w_out_g, tm=512):
    S, D = dxb.shape
    rb = w_out_g.shape[1]

    def body(dx_ref, w_ref, o_ref):
        o_ref[...] = _dot(dx_ref[...], w_ref[...], NT)

    return pl.pallas_call(
        body, name="out_bwd_dy", grid=(S // tm, N_DEV),
        in_specs=[pl.BlockSpec((tm, D), lambda i, j: (i, 0)),
                  pl.BlockSpec((None, rb, D), lambda i, j: (j, 0, 0))],
        out_specs=pl.BlockSpec((tm, rb), lambda i, j: (i, j)),
        out_shape=SDS((S, D), f32), compiler_params=_params(),
    )(dxb, w_out_g)


def _tn_grad(name, a, b, tm, tn, rows_major):
    S, M = a.shape
    N = b.shape[1]
    if rows_major:
        out_shape = SDS((N_DEV, M // N_DEV, N), bf16)
        assert tm == M // N_DEV
        out_spec = pl.BlockSpec((None, tm, tn), lambda i, j: (i, 0, j))
    else:
        out_shape = SDS((N_DEV, M, N // N_DEV), bf16)
        assert tn == N // N_DEV
        out_spec = pl.BlockSpec((None, tm, tn), lambda i, j: (j, i, 0))

    def body(a_ref, b_ref, o_ref):
        o_ref[...] = _dot(a_ref[...], b_ref[...], TN).astype(bf16)

    return pl.pallas_call(
        body, name=name, grid=(M // tm, N // tn),
        in_specs=[pl.BlockSpec((S, tm), lambda i, j: (0, i)), pl.BlockSpec((S, tn), lambda i, j: (0, j))],
        out_specs=out_spec, out_shape=out_shape, compiler_params=_params(),
    )(a, b)


def _in_bwd_dx(dproj, w_in_g, x, g_row, dx_next, tm=512):
    S, D = x.shape
    wc = w_in_g.shape[2]

    def body(dp_ref, w_ref, x_ref, g_ref, dxn_ref, dx_ref, dxb_ref, dg_ref, acc_ref):
        i, k = pl.program_id(0), pl.program_id(1)

        @pl.when(k == 0)
        def _():
            acc_ref[...] = jnp.zeros_like(acc_ref)

        acc_ref[...] += _dot(dp_ref[...], w_ref[...], NT)

        @pl.when(jnp.logical_and(i == 0, k == 0))
        def _():
            dg_ref[...] = jnp.zeros_like(dg_ref)

        @pl.when(k == N_DEV - 1)
        def _():
            dh = acc_ref[...]
            xv = x_ref[...]
            r = lax.rsqrt(jnp.mean(xv * xv, axis=-1, keepdims=True) + EPS)
            xhat = xv * r
            dxhat = dh * g_ref[...]
            dx = r * (dxhat - xhat * jnp.mean(dxhat * xhat, axis=-1, keepdims=True)) + dxn_ref[...]
            dx_ref[...] = dx
            dxb_ref[...] = dx.astype(bf16)
            dg_ref[...] += jnp.sum(dh * xhat, axis=0, keepdims=True)

    row = pl.BlockSpec((tm, D), lambda i, k: (i, 0))
    return pl.pallas_call(
        body, name="in_bwd_dx", grid=(S // tm, N_DEV),
        in_specs=[pl.BlockSpec((tm, wc), lambda i, k: (i, k)),
                  pl.BlockSpec((None, D, wc), lambda i, k: (k, 0, 0)),
                  row, pl.BlockSpec((1, D), lambda i, k: (0, 0)), row],
        out_specs=[row, row, pl.BlockSpec((1, D), lambda i, k: (0, 0))],
        out_shape=[SDS((S, D), f32), SDS((S, D), bf16), SDS((1, D), f32)],
        scratch_shapes=[pltpu.VMEM((tm, D), f32)], compiler_params=_params(56),
    )(dproj, w_in_g, x, g_row, dx_next)


def _sgu_fwd(proj, ln_g, ln_b, w_s, b_t):
    S = proj.shape[0]
    da = A_GROUPS * HEAD_DIM
    D = 2 * da

    def body(u_ref, v_ref, z_ref, lg_ref, lb_ref, w_ref, bt_ref, y_ref):
        u = _gelu(u_ref[...])
        v = _gelu(v_ref[...])
        z = z_ref[...]
        mu = jnp.mean(v, axis=-1, keepdims=True)
        xc = v - mu
        rs = lax.rsqrt(jnp.mean(xc * xc, axis=-1, keepdims=True) + EPS)
        vn = (xc * rs * lg_ref[...] + lb_ref[...]).astype(bf16)
        gate = u * (z * jax.nn.sigmoid(z))
        tri = lax.broadcasted_iota(jnp.int32, (CHUNK, CHUNK), 0) >= lax.broadcasted_iota(jnp.int32, (CHUNK, CHUNK), 1)
        for g in range(A_GROUPS):
            sl = slice(g * HEAD_DIM, (g + 1) * HEAD_DIM)
            wm = jnp.where(tri, w_ref[g], 0.0).astype(bf16)
            mixed = _dot(wm, vn[:, sl]) + bt_ref[:, g:g + 1]
            y_ref[:, sl] = (gate[:, sl] * mixed).astype(bf16)

    blk = lambda cb: pl.BlockSpec((CHUNK, da), lambda c: (c, cb))
    full = lambda shp: pl.BlockSpec(shp, lambda c: (0,) * len(shp))
    return pl.pallas_call(
        body, name="sgu_fwd", grid=(S // CHUNK,),
        in_specs=[blk(0), blk(1), blk(2), full((1, da)), full((1, da)),
                  full((A_GROUPS, CHUNK, CHUNK)), full((CHUNK, A_GROUPS))],
        out_specs=blk(0), out_shape=SDS((S, D), bf16), compiler_params=_params(),
    )(proj, proj, proj, ln_g, ln_b, w_s, b_t)


def _sgu_bwd(proj, dy, ln_g, ln_b, w_s, b_t):
    S = proj.shape[0]
    da = A_GROUPS * HEAD_DIM
    n_proj = proj.shape[1]

    def body(u_ref, v_ref, z_ref, dy_ref, lg_ref, lb_ref, w_ref, bt_ref,
             dp_ref, dw_ref, db_ref, dlg_ref, dlb_ref, dvn_ref):
        @pl.when(pl.program_id(0) == 0)
        def _():
            dw_ref[...] = jnp.zeros_like(dw_ref)
            db_ref[...] = jnp.zeros_like(db_ref)
            dlg_ref[...] = jnp.zeros_like(dlg_ref)
            dlb_ref[...] = jnp.zeros_like(dlb_ref)

        up, vp, z, dy = u_ref[...], v_ref[...], z_ref[...], dy_ref[...]
        u, gu = _gelu_and_grad(up)
        v, gv = _gelu_and_grad(vp)
        s, gs = _silu_and_grad(z)
        mu = jnp.mean(v, axis=-1, keepdims=True)
        xc = v - mu
        rs = lax.rsqrt(jnp.mean(xc * xc, axis=-1, keepdims=True) + EPS)
        vhat = xc * rs
        lg = lg_ref[...]
        vn = (vhat * lg + lb_ref[...]).astype(bf16)
        tri = lax.broadcasted_iota(jnp.int32, (CHUNK, CHUNK), 0) >= lax.broadcasted_iota(jnp.int32, (CHUNK, CHUNK), 1)
        lane = lax.broadcasted_iota(jnp.int32, (CHUNK, HEAD_DIM), 1)
        dys = dy * s
        db = jnp.zeros((CHUNK, HEAD_DIM), f32)
        for g in range(A_GROUPS):
            sl = slice(g * HEAD_DIM, (g + 1) * HEAD_DIM)
            wm = jnp.where(tri, w_ref[g], 0.0).astype(bf16)
            mixed = _dot(wm, vn[:, sl]) + bt_ref[:, g:g + 1]
            dmix = dys[:, sl] * u[:, sl]
            dp_ref[:, sl] = (dys[:, sl] * mixed * gu[:, sl]).astype(bf16)
            dp_ref[:, 2 * da + g * HEAD_DIM:2 * da + (g + 1) * HEAD_DIM] = (
                dy[:, sl] * u[:, sl] * mixed * gs[:, sl]).astype(bf16)
            dmb = dmix.astype(bf16)
            dw_ref[g] += jnp.where(tri, _dot(dmb, vn[:, sl], NT), 0.0)
            dvn_ref[:, sl] = _dot(wm, dmb, TN)
            db = db + jnp.where(lane == g, jnp.sum(dmix, axis=1, keepdims=True), 0.0)
        db_ref[...] += db
        dvn = dvn_ref[...]
        dlg_ref[...] += jnp.sum(dvn * vhat, axis=0, keepdims=True)
        dlb_ref[...] += jnp.sum(dvn, axis=0, keepdims=True)
        dvhat = dvn * lg
        dv = rs * (dvhat - jnp.mean(dvhat, axis=-1, keepdims=True)
                   - vhat * jnp.mean(dvhat * vhat, axis=-1, keepdims=True))
        dp_ref[:, da:2 * da] = (dv * gv).astype(bf16)

    blk = lambda cb: pl.BlockSpec((CHUNK, da), lambda c: (c, cb))
    full = lambda shp: pl.BlockSpec(shp, lambda c: (0,) * len(shp))
    return pl.pallas_call(
        body, name="sgu_bwd", grid=(S // CHUNK,),
        in_specs=[blk(0), blk(1), blk(2), blk(0), full((1, da)), full((1, da)),
                  full((A_GROUPS, CHUNK, CHUNK)), full((CHUNK, A_GROUPS))],
        out_specs=[pl.BlockSpec((CHUNK, 3 * da), lambda c: (c, 0)), full((A_GROUPS, CHUNK, CHUNK)),
                   full((CHUNK, HEAD_DIM)), full((1, da)), full((1, da))],
        out_shape=[SDS((S, n_proj), bf16), SDS((A_GROUPS, CHUNK, CHUNK), f32), SDS((CHUNK, HEAD_DIM), f32),
                   SDS((1, da), f32), SDS((1, da), f32)],
        scratch_shapes=[pltpu.VMEM((CHUNK, da), f32)], compiler_params=_params(),
    )(proj, proj, proj, dy, ln_g, ln_b, w_s, b_t)


def _sb_scores(q, kblk, kb, rows, cols):
    z = _dot(q, kblk, NT) * (1.0 / math.sqrt(HEAD_DIM))
    causal = (cols + kb * TK) < rows
    t = jnp.log1p(jnp.exp(-jnp.abs(z)))
    log_1mb = jnp.where(causal, -(jnp.maximum(z, 0.0) + t), 0.0)
    log_beta = jnp.minimum(z, 0.0) - t
    return causal, log_beta, log_1mb


def _suffix_sum(log_1mb, upper):
    hi = log_1mb.astype(bf16)
    lo = (log_1mb - hi.astype(f32)).astype(bf16)
    return _dot(hi, upper) + _dot(lo, upper)


def _sb_fwd(proj, y_prev, col0):
    S = proj.shape[0]
    D = y_prev.shape[1]
    ycol0 = (A_GROUPS * HEAD_DIM) // HEAD_DIM

    def body(q_ref, k_ref, v_ref, z_ref, yp_ref, y_ref, o_ref, car_ref):
        i = pl.program_id(1)
        q = q_ref[...].astype(bf16)
        nkb = (i + 1) * (TQ // TK)
        rows = i * TQ + lax.broadcasted_iota(jnp.int32, (TQ, TK), 0)
        cols = lax.broadcasted_iota(jnp.int32, (TQ, TK), 1)
        cols_rowwise = cols + lax.shift_right_arithmetic(rows, 31)
        upper = (lax.broadcasted_iota(jnp.int32, (TK, TK), 0) > lax.broadcasted_iota(jnp.int32, (TK, TK), 1)).astype(bf16)

        def step(jj, carry):
            c, acc, car = carry
            kb = nkb - 1 - jj
            off = pl.multiple_of(kb * TK, TK)
            kblk = k_ref[pl.ds(off, TK), :].astype(bf16)
            vblk = v_ref[pl.ds(off, TK), :].astype(bf16)
            causal, log_beta, log_1mb = _sb_scores(q, kblk, kb, rows, cols)
            after = _suffix_sum(log_1mb, upper) + c
            a = jnp.where(causal, jnp.exp(log_beta + after), 0.0)
            acc = acc + _dot(a.astype(bf16), vblk)
            car = jnp.where(cols_rowwise == kb, c, car)
            c = c + jnp.sum(log_1mb, axis=1, keepdims=True)
            return c, acc, car

        init = (jnp.zeros((TQ, TK), f32), jnp.zeros((TQ, HEAD_DIM), f32), jnp.zeros((TQ, TK), f32))
        _, acc, car = lax.fori_loop(0, nkb, step, init)
        o_ref[...] = acc
        car_ref[...] = car
        z = z_ref[...]
        y_ref[...] = (acc * (z * jax.nn.sigmoid(z))).astype(bf16)

    qspec = lambda cb: pl.BlockSpec((TQ, HEAD_DIM), lambda h, i: (i, col0 + cb + h))
    kspec = lambda cb: pl.BlockSpec((S, HEAD_DIM), lambda h, i: (0, col0 + cb + h))
    return pl.pallas_call(
        body, name="sb_fwd", grid=(N_HEADS, S // TQ),
        in_specs=[qspec(0), kspec(4), kspec(8), qspec(12), pl.BlockSpec(memory_space=pl.ANY)],
        out_specs=[pl.BlockSpec((TQ, HEAD_DIM), lambda h, i: (i, ycol0 + h)),
                   pl.BlockSpec((TQ, HEAD_DIM), lambda h, i: (i, h)),
                   pl.BlockSpec((None, TQ, TK), lambda h, i: (h, i, 0))],
        out_shape=[SDS((S, D), bf16), SDS((S, N_HEADS * HEAD_DIM), f32), SDS((N_HEADS, S, TK), f32)],
        input_output_aliases={4: 0}, compiler_params=_params(),
    )(proj, proj, proj, proj, y_prev)


def _sb_bwd(proj, o, car, dy, dproj_prev, col0):
    S = proj.shape[0]
    n_i = S // TQ
    ycol0 = (A_GROUPS * HEAD_DIM) // HEAD_DIM

    def body(q_ref, k_ref, v_ref, z_ref, o_ref, car_ref, dy_ref, dpp_ref,
             dp_ref, dq_ref, dz_ref, dk_ref, dv_ref, dk_acc, dv_acc):
        h, i = pl.program_id(0), pl.program_id(1)

        def put(stage_ref, row0, nrows, cb):
            col = pl.multiple_of((col0 + cb + h) * HEAD_DIM, HEAD_DIM)
            pltpu.sync_copy(stage_ref, dp_ref.at[pl.ds(row0, nrows), pl.ds(col, HEAD_DIM)])

        @pl.when(i == 0)
        def _():
            dk_acc[...] = jnp.zeros_like(dk_acc)
            dv_acc[...] = jnp.zeros_like(dv_acc)

        s, gs = _silu_and_grad(z_ref[...])
        dy = dy_ref[...]
        dz_ref[...] = (dy * o_ref[...] * gs).astype(bf16)
        dob = (dy * s).astype(bf16)
        q = q_ref[...].astype(bf16)
        car = car_ref[...]
        nkb = (i + 1) * (TQ // TK)
        rows = i * TQ + lax.broadcasted_iota(jnp.int32, (TQ, TK), 0)
        cols = lax.broadcasted_iota(jnp.int32, (TQ, TK), 1)
        r_i = lax.broadcasted_iota(jnp.int32, (TK, TK), 0)
        c_i = lax.broadcasted_iota(jnp.int32, (TK, TK), 1)
        upper = (r_i > c_i).astype(bf16)
        lower = (r_i < c_i).astype(bf16)
        scale = 1.0 / math.sqrt(HEAD_DIM)

        def step(kb, carry):
            p, dq = carry
            off = pl.multiple_of(kb * TK, TK)
            kblk = k_ref[pl.ds(off, TK), :].astype(bf16)
            vblk = v_ref[pl.ds(off, TK), :].astype(bf16)
            causal, log_beta, log_1mb = _sb_scores(q, kblk, kb, rows, cols)
            c = jnp.sum(jnp.where(cols == kb, car, 0.0), axis=1, keepdims=True)
            after = _suffix_sum(log_1mb, upper) + c
            a = jnp.where(causal, jnp.exp(log_beta + after), 0.0)
            g = _dot(dob, vblk, NT) * a
            g_hi = g.astype(bf16)
            g_lo = (g - g_hi.astype(f32)).astype(bf16)
            prefix = _dot(g_hi, lower) + _dot(g_lo, lower) + p
            beta = jnp.exp(log_beta)
            dzz = (jnp.where(causal, g * (1.0 - beta) - beta * prefix, 0.0) * scale).astype(bf16)
            dq = dq + _dot(dzz, kblk)
            dk_acc[pl.ds(off, TK), :] += _dot(dzz, q, TN)
            dv_acc[pl.ds(off, TK), :] += _dot(a.astype(bf16), dob, TN)
            p = p + jnp.sum(g, axis=1, keepdims=True)
            return p, dq

        _, dq = lax.fori_loop(0, nkb, step, (jnp.zeros((TQ, TK), f32), jnp.zeros((TQ, HEAD_DIM), f32)))
        dq_ref[...] = dq.astype(bf16)
        row0 = pl.multiple_of(i * TQ, TQ)
        put(dq_ref, row0, TQ, 0)
        put(dz_ref, row0, TQ, 12)

        @pl.when(i == n_i - 1)
        def _():
            dk_ref[...] = dk_acc[...].astype(bf16)
            dv_ref[...] = dv_acc[...].astype(bf16)
            put(dk_ref, 0, S, 4)
            put(dv_ref, 0, S, 8)

    qspec = lambda cb: pl.BlockSpec((TQ, HEAD_DIM), lambda h, i: (i, col0 + cb + h))
    kspec = lambda cb: pl.BlockSpec((S, HEAD_DIM), lambda h, i: (0, col0 + cb + h))
    return pl.pallas_call(
        body, name="sb_bwd", grid=(N_HEADS, n_i),
        in_specs=[qspec(0), kspec(4), kspec(8), qspec(12),
                  pl.BlockSpec((TQ, HEAD_DIM), lambda h, i: (i, h)),
                  pl.BlockSpec((None, TQ, TK), lambda h, i: (h, i, 0)),
                  pl.BlockSpec((TQ, HEAD_DIM), lambda h, i: (i, ycol0 + h)),
                  pl.BlockSpec(memory_space=pl.ANY)],
        out_specs=pl.BlockSpec(memory_space=pl.ANY),
        out_shape=SDS(dproj_prev.shape, bf16),
        input_output_aliases={7: 0},
        scratch_shapes=[pltpu.VMEM((TQ, HEAD_DIM), bf16), pltpu.VMEM((TQ, HEAD_DIM), bf16),
                        pltpu.VMEM((S, HEAD_DIM), bf16), pltpu.VMEM((S, HEAD_DIM), bf16),
                        pltpu.VMEM((S, HEAD_DIM), f32), pltpu.VMEM((S, HEAD_DIM), f32)],
        compiler_params=_params(),
    )(proj, proj, proj, proj, o, car, dy, dproj_prev)


def _mem_kv(mem, mg_row, w_kv_g):
    M, D = mem.shape
    rb, n = w_kv_g.shape[1], w_kv_g.shape[2]

    def body(m_ref, g_ref, w_ref, kv_ref):
        mv = m_ref[...]
        r = lax.rsqrt(jnp.mean(mv * mv, axis=-1, keepdims=True) + EPS)
        mh = (mv * r * g_ref[...]).astype(bf16)
        kv_ref[...] = _dot(mh, w_ref[...].reshape(N_DEV * rb, n))

    return pl.pallas_call(
        body, name="mem_kv", grid=(1,),
        in_specs=[pl.BlockSpec((M, D), lambda i: (0, 0)), pl.BlockSpec((1, D), lambda i: (0, 0)),
                  pl.BlockSpec((N_DEV, rb, n), lambda i: (0, 0, 0))],
        out_specs=pl.BlockSpec((M, n), lambda i: (0, 0)),
        out_shape=SDS((M, n), f32), compiler_params=_params(),
    )(mem, mg_row, w_kv_g)


def _xattn_head(q_ref, kv_ref, qg, kg, h):
    dc = N_HEADS * HEAD_DIM
    sl = slice(h * HEAD_DIM, (h + 1) * HEAD_DIM)
    qh = q_ref[:, sl]
    rq = lax.rsqrt(jnp.mean(qh * qh, axis=-1, keepdims=True) + EPS)
    qhat = qh * rq
    qn = (qhat * qg).astype(bf16)
    kh = kv_ref[:, sl]
    rk = lax.rsqrt(jnp.mean(kh * kh, axis=-1, keepdims=True) + EPS)
    kn = (kh * rk * kg).astype(bf16)
    vh = kv_ref[:, dc + h * HEAD_DIM:dc + (h + 1) * HEAD_DIM].astype(bf16)
    s = _dot(qn, kn, NT) * (1.0 / math.sqrt(HEAD_DIM))
    e = jnp.exp(s - jnp.max(s, axis=-1, keepdims=True))
    p = e / jnp.sum(e, axis=-1, keepdims=True)
    o = _dot(p.astype(bf16), vh)
    return sl, rq, qhat, qn, kn, vh, p, o


def _xattn_fwd(proj, kv, qg_row, kg_row, y_prev, col0, tq=512):
    S = proj.shape[0]
    D = y_prev.shape[1]
    dc = N_HEADS * HEAD_DIM
    M = kv.shape[0]

    def body(q_ref, z_ref, kv_ref, qg_ref, kg_ref, yp_ref, y_ref):
        for h in range(N_HEADS):
            sl, _, _, _, _, _, _, o = _xattn_head(q_ref, kv_ref, qg_ref[...], kg_ref[...], h)
            z = z_ref[:, sl]
            y_ref[:, sl] = (o * (z * jax.nn.sigmoid(z))).astype(bf16)

    full = lambda shp: pl.BlockSpec(shp, lambda i: (0,) * len(shp))
    return pl.pallas_call(
        body, name="xattn_fwd", grid=(S // tq,),
        in_specs=[pl.BlockSpec((tq, dc), lambda i: (i, col0)), pl.BlockSpec((tq, dc), lambda i: (i, col0 + 1)),
                  full((M, 2 * dc)), full((1, HEAD_DIM)), full((1, HEAD_DIM)), pl.BlockSpec(memory_space=pl.ANY)],
        out_specs=pl.BlockSpec((tq, dc), lambda i: (i, D // dc - 1)),
        out_shape=SDS((S, D), bf16), input_output_aliases={5: 0}, compiler_params=_params(),
    )(proj, proj, kv, qg_row, kg_row, y_prev)


def _xattn_bwd(proj, kv, qg_row, kg_row, dy, dproj_prev, col0, tq=512):
    S = proj.shape[0]
    D = dy.shape[1]
    dc = N_HEADS * HEAD_DIM
    M = kv.shape[0]

    def body(q_ref, z_ref, kv_ref, qg_ref, kg_ref, dy_ref, dpp_ref, dp_ref, dkn_ref, dv_ref, dqg_ref):
        @pl.when(pl.program_id(0) == 0)
        def _():
            dkn_ref[...] = jnp.zeros_like(dkn_ref)
            dv_ref[...] = jnp.zeros_like(dv_ref)
            dqg_ref[...] = jnp.zeros_like(dqg_ref)

        qg = qg_ref[...]
        for h in range(N_HEADS):
            sl, rq, qhat, qn, kn, vh, p, o = _xattn_head(q_ref, kv_ref, qg, kg_ref[...], h)
            s, gs = _silu_and_grad(z_ref[:, sl])
            dyh = dy_ref[:, sl]
            dp_ref[:, dc + h * HEAD_DIM:dc + (h + 1) * HEAD_DIM] = (dyh * o * gs).astype(bf16)
            dob = (dyh * s).astype(bf16)
            dpr = _dot(dob, vh, NT)
            dv_ref[:, sl] += _dot(p.astype(bf16), dob, TN)
            ds = (p * (dpr - jnp.sum(p * dpr, axis=-1, keepdims=True)) * (1.0 / math.sqrt(HEAD_DIM))).astype(bf16)
            dqn = _dot(ds, kn)
            dkn_ref[:, sl] += _dot(ds, qn, TN)
            dqg_ref[...] += jnp.sum(dqn * qhat, axis=0, keepdims=True)
            dqhat = dqn * qg
            dp_ref[:, sl] = (rq * (dqhat - qhat * jnp.mean(dqhat * qhat, axis=-1, keepdims=True))).astype(bf16)

    full = lambda shp: pl.BlockSpec(shp, lambda i: (0,) * len(shp))
    return pl.pallas_call(
        body, name="xattn_bwd", grid=(S // tq,),
        in_specs=[pl.BlockSpec((tq, dc), lambda i: (i, col0)), pl.BlockSpec((tq, dc), lambda i: (i, col0 + 1)),
                  full((M, 2 * dc)), full((1, HEAD_DIM)), full((1, HEAD_DIM)),
                  pl.BlockSpec((tq, dc), lambda i: (i, D // dc - 1)), pl.BlockSpec(memory_space=pl.ANY)],
        out_specs=[pl.BlockSpec((tq, 2 * dc), lambda i: (i, col0 // 2)), full((M, dc)), full((M, dc)),
                   full((1, HEAD_DIM))],
        out_shape=[SDS(dproj_prev.shape, bf16), SDS((M, dc), f32), SDS((M, dc), f32), SDS((1, HEAD_DIM), f32)],
        input_output_aliases={6: 0}, compiler_params=_params(),
    )(proj, proj, kv, qg_row, kg_row, dy, dproj_prev)


def _mem_bwd(mem, mg_row, kv, dkn, dv, kg_row, w_kv_g):
    M, D = mem.shape
    rb, n = w_kv_g.shape[1], w_kv_g.shape[2]
    dc = n // 2

    def body(m_ref, g_ref, kv_ref, dkn_ref, dv_ref, kg_ref, w_ref, dw_ref, dmg_ref, dkg_ref, dkv_ref):
        mv = m_ref[...]
        r = lax.rsqrt(jnp.mean(mv * mv, axis=-1, keepdims=True) + EPS)
        mhat = mv * r
        mh = (mhat * g_ref[...]).astype(bf16)
        kg = kg_ref[...]
        dkg = jnp.zeros((1, HEAD_DIM), f32)
        for h in range(N_HEADS):
            sl = slice(h * HEAD_DIM, (h + 1) * HEAD_DIM)
            kh = kv_ref[:, sl]
            rk = lax.rsqrt(jnp.mean(kh * kh, axis=-1, keepdims=True) + EPS)
            khat = kh * rk
            dkn_h = dkn_ref[:, sl]
            dkg = dkg + jnp.sum(dkn_h * khat, axis=0, keepdims=True)
            dkhat = dkn_h * kg
            dkv_ref[:, sl] = (rk * (dkhat - khat * jnp.mean(dkhat * khat, axis=-1, keepdims=True))).astype(bf16)
        dkv_ref[:, dc:] = dv_ref[...].astype(bf16)
        dkg_ref[...] = dkg
        dkv = dkv_ref[...]
        dw_ref[...] = _dot(mh, dkv, TN).astype(bf16).reshape(N_DEV, rb, n)
        dmh = _dot(dkv, w_ref[...].reshape(N_DEV * rb, n), NT)
        dmg_ref[...] = jnp.sum(dmh * mhat, axis=0, keepdims=True)

    full = lambda shp: pl.BlockSpec(shp, lambda i: (0,) * len(shp))
    wspec = full((N_DEV, rb, n))
    return pl.pallas_call(
        body, name="mem_bwd", grid=(1,),
        in_specs=[full((M, D)), full((1, D)), full((M, n)), full((M, dc)), full((M, dc)), full((1, HEAD_DIM)), wspec],
        out_specs=[wspec, full((1, D)), full((1, HEAD_DIM))],
        out_shape=[SDS((N_DEV, rb, n), bf16), SDS((1, D), f32), SDS((1, HEAD_DIM), f32)],
        scratch_shapes=[pltpu.VMEM((M, n), bf16)], compiler_params=_params(),
    )(mem, mg_row, kv, dkn, dv, kg_row, w_kv_g)


SMALL = ("norm_g", "sgu_ln_g", "sgu_ln_b", "sgu_w", "sgu_b", "mem_norm_g", "q_norm_g", "k_norm_g")


def _pack_small(parts):
    flat = jnp.concatenate([parts[n].reshape(-1) for n in SMALL])
    pad = (-flat.shape[0]) % (8 * 128)
    return jnp.pad(flat, (0, pad)).reshape(-1, 128)


def _unpack_small(packed, like):
    flat = packed.reshape(-1)
    out, off = {}, 0
    for n in SMALL:
        size = math.prod(like[n].shape)
        out[n] = flat[off:off + size].reshape(like[n].shape)
        off += size
    return out


def kernel(x, mem, norm_g, w_in, sgu_ln_g, sgu_ln_b, sgu_w, sgu_b, mem_norm_g, w_mem_kv, q_norm_g, k_norm_g, w_out, loss_target, m_norm_g, m_w_in, m_sgu_ln_g, m_sgu_ln_b, m_sgu_w, m_sgu_b, m_mem_norm_g, m_w_mem_kv, m_q_norm_g, m_k_norm_g, m_w_out, v_norm_g, v_w_in, v_sgu_ln_g, v_sgu_ln_b, v_sgu_w, v_sgu_b, v_mem_norm_g, v_w_mem_kv, v_q_norm_g, v_k_norm_g, v_w_out):
    L, D, wc = w_in.shape
    S = x.shape[1]
    da = D // 2
    xs = x.reshape(S, D)
    mems = mem.reshape(mem.shape[1], D)
    tgt = loss_target.reshape(S, D)
    row = lambda a, l: a[l].reshape(1, -1)
    tie = lambda a, tok: a + tok[0:1, 0:1]
    sb_col, xa_col = 3 * da // HEAD_DIM, (3 * da + D) // (D // 4)

    w_b = [(_cast_bf16("cast_w_in", w_in, l, 512), _cast_bf16("cast_w_kv", w_mem_kv, l, 256),
            _cast_bf16("cast_w_out", w_out, l, 256)) for l in range(L)]
    gathers = []
    for l in range(L):
        gathers.append(_exchange_start(f"gather_w_in{l}", [w_b[l][0]], scatter=False))
        gathers.append(_exchange_start(f"gather_w_rest{l}", [w_b[l][1], w_b[l][2]], scatter=False))
    first_g = row(norm_g, 0)
    for hd in gathers:
        first_g = tie(first_g, hd["token"])

    acts = []
    xl = xs
    for l in range(L):
        g_row = first_g if l == 0 else row(norm_g, l)
        (w_in_g,) = _exchange_wait(gathers[2 * l], g_row)
        proj, h = _rms_proj(xl, g_row, w_in_g)
        y = _sgu_fwd(proj, row(sgu_ln_g, l), row(sgu_ln_b, l), sgu_w[l], sgu_b[l].T)
        y, o_b, car = _sb_fwd(proj, y, sb_col)
        w_kv_g, w_out_g = _exchange_wait(gathers[2 * l + 1], o_b)
        kv = _mem_kv(mems, row(mem_norm_g, l), w_kv_g)
        y = _xattn_fwd(proj, kv, row(q_norm_g, l), row(k_norm_g, l), y, xa_col)
        x_next = _out_proj(xl, y, w_out_g)
        acts.append((xl, proj, h, y, o_b, car, kv, w_in_g, w_kv_g, w_out_g))
        xl = x_next

    dx, dxb, loss_part = _loss_and_grad(xl, tgt, 512)
    loss = lax.psum(loss_part[0, 0], ("x", "y", "c"))

    scatters = [None] * L
    small = {n: [None] * L for n in SMALL}
    for l in reversed(range(L)):
        xl, proj, h, y, o_b, car, kv, w_in_g, w_kv_g, w_out_g = acts[l]
        dy = _out_bwd_dy(dxb, w_out_g)
        g_out = _tn_grad("out_bwd_dw", y, dxb, D // N_DEV, 512, rows_major=True)
        s_out = _exchange_start(f"scatter_g_out{l}", [g_out], scatter=True)
        dproj, d_sw, d_sb, d_lg, d_lb = _sgu_bwd(proj, dy, tie(row(sgu_ln_g, l), s_out["token"]), row(sgu_ln_b, l),
                                                 sgu_w[l], sgu_b[l].T)
        dproj = _sb_bwd(proj, o_b, car, dy, dproj, sb_col)
        dproj, dkn, dv, d_qg = _xattn_bwd(proj, kv, row(q_norm_g, l), row(k_norm_g, l), dy, dproj, xa_col)
        g_kv, d_mg, d_kg = _mem_bwd(mems, row(mem_norm_g, l), kv, dkn, dv, row(k_norm_g, l), w_kv_g)
        g_in = _tn_grad("in_bwd_dw", h, dproj, 512, wc, rows_major=False)
        s_rest = _exchange_start(f"scatter_g_rest{l}", [g_kv, g_in], scatter=True)
        dx, dxb, d_ng = _in_bwd_dx(dproj, w_in_g, xl, tie(row(norm_g, l), s_rest["token"]), dx)
        scatters[l] = (s_out, s_rest)
        for n, val in (("norm_g", d_ng), ("sgu_ln_g", d_lg), ("sgu_ln_b", d_lb), ("sgu_w", d_sw),
                       ("sgu_b", d_sb[:, :A_GROUPS].T), ("mem_norm_g", d_mg), ("q_norm_g", d_qg), ("k_norm_g", d_kg)):
            small[n][l] = val.reshape(-1)

    weights = dict(norm_g=norm_g, sgu_ln_g=sgu_ln_g, sgu_ln_b=sgu_ln_b, sgu_w=sgu_w, sgu_b=sgu_b,
                   mem_norm_g=mem_norm_g, q_norm_g=q_norm_g, k_norm_g=k_norm_g)
    moms_m = dict(norm_g=m_norm_g, sgu_ln_g=m_sgu_ln_g, sgu_ln_b=m_sgu_ln_b, sgu_w=m_sgu_w, sgu_b=m_sgu_b,
                  mem_norm_g=m_mem_norm_g, q_norm_g=m_q_norm_g, k_norm_g=m_k_norm_g)
    moms_v = dict(norm_g=v_norm_g, sgu_ln_g=v_sgu_ln_g, sgu_ln_b=v_sgu_ln_b, sgu_w=v_sgu_w, sgu_b=v_sgu_b,
                  mem_norm_g=v_mem_norm_g, q_norm_g=v_q_norm_g, k_norm_g=v_k_norm_g)
    part = _pack_small({n: jnp.stack(small[n]).reshape(weights[n].shape) for n in SMALL})
    (r_small,) = _exchange("gather_small_grads", [part], scatter=False)
    sm = _small_sum_adam(r_small, _pack_small(weights), _pack_small(moms_m), _pack_small(moms_v))

    r_out = [_exchange_wait(scatters[l][0], sm[0])[0] for l in range(L)]
    res = {}
    res["w_out"] = _sum_adam("adam_w_out", r_out, w_out, m_w_out, v_w_out, 128)
    r_rest = [_exchange_wait(scatters[l][1], res["w_out"][0]) for l in reversed(range(L))][::-1]
    res["w_mem_kv"] = _sum_adam("adam_w_kv", [r[0] for r in r_rest], w_mem_kv, m_w_mem_kv, v_w_mem_kv, 256)
    res["w_in"] = _sum_adam("adam_w_in", [r[1] for r in r_rest], w_in, m_w_in, v_w_in, 256)
    sm = [_unpack_small(a, weights) for a in sm]
    for n in SMALL:
        res[n] = [a[n] for a in sm]

    order = ("norm_g", "w_in", "sgu_ln_g", "sgu_ln_b", "sgu_w", "sgu_b", "mem_norm_g", "w_mem_kv", "q_norm_g",
             "k_norm_g", "w_out")
    outs = [loss, dx.reshape(x.shape)]
    for k in range(4):
        outs += [res[n][k] for n in order]
    return tuple(outs)
```

```python
import functools
import math

import jax
import jax.numpy as jnp
from jax import lax
from jax.experimental import pallas as pl
from jax.experimental.pallas import tpu as pltpu

f32 = jnp.float32
bf16 = jnp.bfloat16
SDS = jax.ShapeDtypeStruct

N_DEV = 8
EPS = 1e-6
CHUNK = 128
A_GROUPS = 8
HEAD_DIM = 128
N_HEADS = 4
TQ = 256
TK = 128
ADAM_LR, ADAM_B1, ADAM_B2, ADAM_EPS, ADAM_WD, ADAM_STEP = 0.001, 0.9, 0.999, 1e-08, 0.01, 10
MIB = 1024 * 1024

NT = (((1,), (1,)), ((), ()))
TN = (((0,), (0,)), ((), ()))


def _params(vmem_mib=48):
    return pltpu.CompilerParams(vmem_limit_bytes=vmem_mib * MIB)


def _gelu_and_grad(x):
    e = lax.erf(x * (1.0 / math.sqrt(2.0)))
    cdf = 0.5 * (1.0 + e)
    pdf = jnp.exp(-0.5 * x * x) * (1.0 / math.sqrt(2.0 * math.pi))
    return x * cdf, cdf + x * pdf


def _gelu(x):
    return 0.5 * x * (1.0 + lax.erf(x * (1.0 / math.sqrt(2.0))))


def _silu_and_grad(z):
    sg = jax.nn.sigmoid(z)
    return z * sg, sg * (1.0 + z * (1.0 - sg))


def _dot(a, b, dims=None):
    if dims is None:
        return jnp.dot(a, b, preferred_element_type=f32)
    return lax.dot_general(a, b, dims, preferred_element_type=f32)


def _exchange_copies(src_refs, dst_refs, send_sems, recv_sems, loc_sems, scatter):
    x, y, c = lax.axis_index("x"), lax.axis_index("y"), lax.axis_index("c")
    me = 4 * x + 2 * y + c
    locals_, remotes = [], []
    for a, (s, d) in enumerate(zip(src_refs, dst_refs)):
        locals_.append(pltpu.make_async_copy(s.at[me] if scatter else s, d.at[me], loc_sems.at[a]))
        for m in range(1, N_DEV):
            px = 1 - x if m & 4 else x
            py = 1 - y if m & 2 else y
            pc = 1 - c if m & 1 else c
            peer = 4 * px + 2 * py + pc
            remotes.append(pltpu.make_async_remote_copy(
                src_ref=s.at[peer] if scatter else s, dst_ref=d.at[me],
                send_sem=send_sems.at[a * 7 + m - 1], recv_sem=recv_sems.at[a * 7 + m - 1],
                device_id=(px, py, pc), device_id_type=pl.DeviceIdType.MESH))
    return locals_, remotes


def _exchange_shapes(srcs, scatter):
    return [(s.shape if scatter else (N_DEV,) + s.shape) for s in srcs]


def _exchange(name, srcs, scatter):
    n = len(srcs)

    def body(*refs):
        locals_, remotes = _exchange_copies(refs[:n], refs[n:2 * n], *refs[2 * n:], scatter)
        for cp in locals_ + remotes:
            cp.start()
        for cp in remotes:
            cp.wait_recv()
        for cp in remotes:
            cp.wait_send()
        for cp in locals_:
            cp.wait()

    any_spec = pl.BlockSpec(memory_space=pl.ANY)
    outs = pl.pallas_call(
        body, name=name, out_shape=[SDS(shp, s.dtype) for shp, s in zip(_exchange_shapes(srcs, scatter), srcs)],
        in_specs=[any_spec] * n, out_specs=[any_spec] * n,
        scratch_shapes=[pltpu.SemaphoreType.DMA((n * 7,)), pltpu.SemaphoreType.DMA((n * 7,)),
                        pltpu.SemaphoreType.DMA((n,))],
    )(*srcs)
    return list(outs)


_HBM = pl.BlockSpec(memory_space=pltpu.HBM)
_SEM = pl.BlockSpec(memory_space=pltpu.SEMAPHORE)
_EFFECT = pltpu.SideEffectType.DATAFLOW_SIDE_EFFECTING


def _exchange_start(name, srcs, scatter, after):
    n = len(srcs)
    shapes = _exchange_shapes(srcs, scatter)

    def body(*refs):
        token = refs[-1]
        locals_, remotes = _exchange_copies(refs[:n], refs[n:2 * n], *refs[2 * n + 1:2 * n + 4], scatter)
        for cp in locals_ + remotes:
            cp.start()
        token[...] = jnp.zeros_like(token)

    hbm = lambda a: pltpu.with_memory_space_constraint(a, pltpu.HBM)
    outs = pl.pallas_call(
        body, name=name,
        out_shape=(pltpu.SemaphoreType.DMA((n * 7,)), pltpu.SemaphoreType.DMA((n * 7,)), pltpu.SemaphoreType.DMA((n,)),
                   *[pltpu.HBM(s.shape, s.dtype) for s in srcs],
                   *[pltpu.HBM(shp, s.dtype) for shp, s in zip(shapes, srcs)],
                   SDS((8, 128), f32)),
        in_specs=[_HBM] * (2 * n) + [pl.BlockSpec(memory_space=pl.ANY)],
        out_specs=(_SEM, _SEM, _SEM, *[_HBM] * (2 * n), pl.BlockSpec(memory_space=pltpu.VMEM)),
        input_output_aliases={k: 3 + k for k in range(2 * n)},
        compiler_params=pltpu.CompilerParams(has_side_effects=_EFFECT),
    )(*[hbm(s) for s in srcs], *[hbm(lax.empty(shp, s.dtype)) for shp, s in zip(shapes, srcs)], after)
    return dict(name=name, n=n, scatter=scatter, sems=outs[:3], bufs=outs[3:3 + 2 * n], token=outs[-1])


def _exchange_wait(handle, after):
    n, scatter = handle["n"], handle["scatter"]

    def body(*refs):
        locals_, remotes = _exchange_copies(refs[:n], refs[n:2 * n], *refs[2 * n:2 * n + 3], scatter)
        for cp in remotes:
            cp.wait_recv()
        for cp in remotes:
            cp.wait_send()
        for cp in locals_:
            cp.wait()

    bufs = handle["bufs"]
    outs = pl.pallas_call(
        body, name=handle["name"] + "_wait",
        out_shape=tuple(pltpu.HBM(b.shape, b.dtype) for b in bufs),
        in_specs=[_HBM] * (2 * n) + [_SEM] * 3 + [pl.BlockSpec(memory_space=pl.ANY)],
        out_specs=tuple([_HBM] * (2 * n)),
        input_output_aliases={k: k for k in range(2 * n)},
        compiler_params=pltpu.CompilerParams(has_side_effects=_EFFECT),
    )(*bufs, *handle["sems"], after)
    return list(outs[n:])


def _cast_bf16(name, w, l, tr):
    _, R, C = w.shape

    def body(w_ref, o_ref):
        o_ref[...] = w_ref[...].astype(bf16)

    return pl.pallas_call(
        body, name=name, grid=(R // tr,),
        in_specs=[pl.BlockSpec((None, tr, C), lambda i: (l, i, 0))],
        out_specs=pl.BlockSpec((tr, C), lambda i: (i, 0)),
        out_shape=SDS((R, C), bf16), compiler_params=_params(),
    )(w)


def _adam_math(w, g, m, v):
    m2 = ADAM_B1 * m + (1.0 - ADAM_B1) * g
    v2 = ADAM_B2 * v + (1.0 - ADAM_B2) * (g * g)
    m_hat = m2 / (1.0 - ADAM_B1 ** ADAM_STEP)
    v_hat = v2 / (1.0 - ADAM_B2 ** ADAM_STEP)
    delta = -ADAM_LR * (m_hat / (jnp.sqrt(v_hat) + ADAM_EPS) + ADAM_WD * w)
    return delta, m2, v2


def _sum_adam(name, recvs, w, m, v, tr):
    L, R, C = w.shape
    n_i = R // tr

    def body(*refs):
        r_refs = refs[:L]
        w_ref, m_ref, v_ref, g_ref, d_ref, m2_ref, v2_ref = refs[L:]
        for lay in range(L):
            @pl.when(pl.program_id(0) == lay)
            def _(r_ref=r_refs[lay]):
                g = r_ref[0].astype(f32)
                for s in range(1, N_DEV):
                    g = g + r_ref[s].astype(f32)
                d, m2, v2 = _adam_math(w_ref[...], g, m_ref[...], v_ref[...])
                g_ref[...] = g
                d_ref[...] = d
                m2_ref[...] = m2
                v2_ref[...] = v2

    def rspec(lay):
        return pl.BlockSpec((N_DEV, tr, C), lambda l, i: (0, jnp.where(l == lay, i, jnp.where(l > lay, n_i - 1, 0)), 0))

    wspec = pl.BlockSpec((None, tr, C), lambda l, i: (l, i, 0))
    return pl.pallas_call(
        body, name=name, grid=(L, n_i),
        in_specs=[rspec(lay) for lay in range(L)] + [wspec, wspec, wspec],
        out_specs=[wspec] * 4, out_shape=[SDS((L, R, C), f32)] * 4, compiler_params=_params(),
    )(*recvs, w, m, v)


def _small_sum_adam(recv, w, m, v):
    _, R, C = recv.shape

    def body(r_ref, w_ref, m_ref, v_ref, g_ref, d_ref, m2_ref, v2_ref):
        g = r_ref[0]
        for s in range(1, N_DEV):
            g = g + r_ref[s]
        d, m2, v2 = _adam_math(w_ref[...], g, m_ref[...], v_ref[...])
        g_ref[...] = g
        d_ref[...] = d
        m2_ref[...] = m2
        v2_ref[...] = v2

    return pl.pallas_call(
        body, name="small_sum_adam", out_shape=[SDS((R, C), f32)] * 4, compiler_params=_params(),
    )(recv, w, m, v)


def _loss_and_grad(xf, tgt, tm):
    S, D = xf.shape

    def body(x_ref, t_ref, dx_ref, dxb_ref, l_ref):
        i = pl.program_id(0)
        d = x_ref[...] - t_ref[...]
        dx = d * (1.0 / D)
        dx_ref[...] = dx
        dxb_ref[...] = dx.astype(bf16)
        e = d * d
        part = e[:, 0:128]
        for k in range(1, D // 128):
            part = part + e[:, k * 128:(k + 1) * 128]
        part = jnp.sum(part.reshape(tm // 8, 8, 128), axis=0)

        @pl.when(i == 0)
        def _():
            l_ref[...] = jnp.zeros_like(l_ref)

        l_ref[...] += part

        @pl.when(i == pl.num_programs(0) - 1)
        def _():
            tot = jnp.sum(l_ref[...], axis=1, keepdims=True)
            tot = jnp.sum(tot, axis=0, keepdims=True)
            l_ref[...] = jnp.broadcast_to(tot * (0.5 / D), l_ref.shape)

    row = pl.BlockSpec((tm, D), lambda i: (i, 0))
    return pl.pallas_call(
        body, name="loss_grad", grid=(S // tm,),
        in_specs=[row, row], out_specs=[row, row, pl.BlockSpec((8, 128), lambda i: (0, 0))],
        out_shape=[SDS((S, D), f32), SDS((S, D), bf16), SDS((8, 128), f32)], compiler_params=_params(),
    )(xf, tgt)


def _rms_proj(x, g_row, w_in_g, tm=512):
    S, D = x.shape
    wc = w_in_g.shape[2]
    tn = 256
    per = wc // tn
    n_out = N_DEV * wc

    def body(x_ref, g_ref, w_ref, proj_ref, h_ref):
        @pl.when(pl.program_id(1) == 0)
        def _():
            xv = x_ref[...]
            r = lax.rsqrt(jnp.mean(xv * xv, axis=-1, keepdims=True) + EPS)
            h_ref[...] = (xv * r * g_ref[...]).astype(bf16)

        proj_ref[...] = _dot(h_ref[...], w_ref[...])

    return pl.pallas_call(
        body, name="rms_proj", grid=(S // tm, n_out // tn),
        in_specs=[pl.BlockSpec((tm, D), lambda i, j: (i, 0)),
                  pl.BlockSpec((1, D), lambda i, j: (0, 0)),
                  pl.BlockSpec((None, D, tn), lambda i, j: (j // per, 0, j % per))],
        out_specs=[pl.BlockSpec((tm, tn), lambda i, j: (i, j)), pl.BlockSpec((tm, D), lambda i, j: (i, 0))],
        out_shape=[SDS((S, n_out), f32), SDS((S, D), bf16)], compiler_params=_params(),
    )(x, g_row, w_in_g)


def _out_proj(x, y, w_out_g, tm=512, tn=512):
    S, D = x.shape
    rb = w_out_g.shape[1]

    def body(x_ref, y_ref, w_ref, o_ref):
        w = w_ref[...].reshape(N_DEV * rb, tn)
        o_ref[...] = x_ref[...] + _dot(y_ref[...], w)

    return pl.pallas_call(
        body, name="out_proj", grid=(S // tm, D // tn),
        in_specs=[pl.BlockSpec((tm, tn), lambda i, j: (i, j)),
                  pl.BlockSpec((tm, D), lambda i, j: (i, 0)),
                  pl.BlockSpec((N_DEV, rb, tn), lambda i, j: (0, 0, j))],
        out_specs=pl.BlockSpec((tm, tn), lambda i, j: (i, j)),
        out_shape=SDS((S, D), f32), compiler_params=_params(),
    )(x, y, w_out_g)


def _out_bwd_dy(dxb, w_out_g, tm=512):
    S, D = dxb.shape
    rb = w_out_g.shape[1]

    def body(dx_ref, w_ref, o_ref):
        o_ref[...] = _dot(dx_ref[...], w_ref[...], NT)

    return pl.pallas_call(
        body, name="out_bwd_dy", grid=(S // tm, N_DEV),
        in_specs=[pl.BlockSpec((tm, D), lambda i, j: (i, 0)),
                  pl.BlockSpec((None, rb, D), lambda i, j: (j, 0, 0))],
        out_specs=pl.BlockSpec((tm, rb), lambda i, j: (i, j)),
        out_shape=SDS((S, D), f32), compiler_params=_params(),
    )(dxb, w_out_g)


def _tn_grad(name, a, b, tm, tn, rows_major):
    S, M = a.shape
    N = b.shape[1]
    if rows_major:
        out_shape = SDS((N_DEV, M // N_DEV, N), bf16)
        assert tm == M // N_DEV
        out_spec = pl.BlockSpec((None, tm, tn), lambda i, j: (i, 0, j))
    else:
        out_shape = SDS((N_DEV, M, N // N_DEV), bf16)
        assert tn == N // N_DEV
        out_spec = pl.BlockSpec((None, tm, tn), lambda i, j: (j, i, 0))

    def body(a_ref, b_ref, o_ref):
        o_ref[...] = _dot(a_ref[...], b_ref[...], TN).astype(bf16)

    return pl.pallas_call(
        body, name=name, grid=(M // tm, N // tn),
        in_specs=[pl.BlockSpec((S, tm), lambda i, j: (0, i)), pl.BlockSpec((S, tn), lambda i, j: (0, j))],
        out_specs=out_spec, out_shape=out_shape, compiler_params=_params(),
    )(a, b)


def _in_bwd_dx(dproj, w_in_g, x, g_row, dx_next, tm=512):
    S, D = x.shape
    wc = w_in_g.shape[2]

    def body(dp_ref, w_ref, x_ref, g_ref, dxn_ref, dx_ref, dxb_ref, dg_ref, acc_ref):
        i, k = pl.program_id(0), pl.program_id(1)

        @pl.when(k == 0)
        def _():
            acc_ref[...] = jnp.zeros_like(acc_ref)

        acc_ref[...] += _dot(dp_ref[...], w_ref[...], NT)

        @pl.when(jnp.logical_and(i == 0, k == 0))
        def _():
            dg_ref[...] = jnp.zeros_like(dg_ref)

        @pl.when(k == N_DEV - 1)
        def _():
            dh = acc_ref[...]
            xv = x_ref[...]
            r = lax.rsqrt(jnp.mean(xv * xv, axis=-1, keepdims=True) + EPS)
            xhat = xv * r
            dxhat = dh * g_ref[...]
            dx = r * (dxhat - xhat * jnp.mean(dxhat * xhat, axis=-1, keepdims=True)) + dxn_ref[...]
            dx_ref[...] = dx
            dxb_ref[...] = dx.astype(bf16)
            dg_ref[...] += jnp.sum(dh * xhat, axis=0, keepdims=True)

    row = pl.BlockSpec((tm, D), lambda i, k: (i, 0))
    return pl.pallas_call(
        body, name="in_bwd_dx", grid=(S // tm, N_DEV),
        in_specs=[pl.BlockSpec((tm, wc), lambda i, k: (i, k)),
                  pl.BlockSpec((None, D, wc), lambda i, k: (k, 0, 0)),
                  row, pl.BlockSpec((1, D), lambda i, k: (0, 0)), row],
        out_specs=[row, row, pl.BlockSpec((1, D), lambda i, k: (0, 0))],
        out_shape=[SDS((S, D), f32), SDS((S, D), bf16), SDS((1, D), f32)],
        scratch_shapes=[pltpu.VMEM((tm, D), f32)], compiler_params=_params(56),
    )(dproj, w_in_g, x, g_row, dx_next)


def _sgu_fwd(proj, ln_g, ln_b, w_s, b_t):
    S = proj.shape[0]
    da = A_GROUPS * HEAD_DIM
    D = 2 * da

    def body(u_ref, v_ref, z_ref, lg_ref, lb_ref, w_ref, bt_ref, y_ref):
        u = _gelu(u_ref[...])
        v = _gelu(v_ref[...])
        z = z_ref[...]
        mu = jnp.mean(v, axis=-1, keepdims=True)
        xc = v - mu
        rs = lax.rsqrt(jnp.mean(xc * xc, axis=-1, keepdims=True) + EPS)
        vn = (xc * rs * lg_ref[...] + lb_ref[...]).astype(bf16)
        gate = u * (z * jax.nn.sigmoid(z))
        tri = lax.broadcasted_iota(jnp.int32, (CHUNK, CHUNK), 0) >= lax.broadcasted_iota(jnp.int32, (CHUNK, CHUNK), 1)
        for g in range(A_GROUPS):
            sl = slice(g * HEAD_DIM, (g + 1) * HEAD_DIM)
            wm = jnp.where(tri, w_ref[g], 0.0).astype(bf16)
            mixed = _dot(wm, vn[:, sl]) + bt_ref[:, g:g + 1]
            y_ref[:, sl] = (gate[:, sl] * mixed).astype(bf16)

    blk = lambda cb: pl.BlockSpec((CHUNK, da), lambda c: (c, cb))
    full = lambda shp: pl.BlockSpec(shp, lambda c: (0,) * len(shp))
    return pl.pallas_call(
        body, name="sgu_fwd", grid=(S // CHUNK,),
        in_specs=[blk(0), blk(1), blk(2), full((1, da)), full((1, da)),
                  full((A_GROUPS, CHUNK, CHUNK)), full((CHUNK, A_GROUPS))],
        out_specs=blk(0), out_shape=SDS((S, D), bf16), compiler_params=_params(),
    )(proj, proj, proj, ln_g, ln_b, w_s, b_t)


def _sgu_bwd(proj, dy, ln_g, ln_b, w_s, b_t):
    S = proj.shape[0]
    da = A_GROUPS * HEAD_DIM
    n_proj = proj.shape[1]

    def body(u_ref, v_ref, z_ref, dy_ref, lg_ref, lb_ref, w_ref, bt_ref,
             dp_ref, dw_ref, db_ref, dlg_ref, dlb_ref, dvn_ref):
        @pl.when(pl.program_id(0) == 0)
        def _():
            dw_ref[...] = jnp.zeros_like(dw_ref)
            db_ref[...] = jnp.zeros_like(db_ref)
            dlg_ref[...] = jnp.zeros_like(dlg_ref)
            dlb_ref[...] = jnp.zeros_like(dlb_ref)

        up, vp, z, dy = u_ref[...], v_ref[...], z_ref[...], dy_ref[...]
        u, gu = _gelu_and_grad(up)
        v, gv = _gelu_and_grad(vp)
        s, gs = _silu_and_grad(z)
        mu = jnp.mean(v, axis=-1, keepdims=True)
        xc = v - mu
        rs = lax.rsqrt(jnp.mean(xc * xc, axis=-1, keepdims=True) + EPS)
        vhat = xc * rs
        lg = lg_ref[...]
        vn = (vhat * lg + lb_ref[...]).astype(bf16)
        tri = lax.broadcasted_iota(jnp.int32, (CHUNK, CHUNK), 0) >= lax.broadcasted_iota(jnp.int32, (CHUNK, CHUNK), 1)
        lane = lax.broadcasted_iota(jnp.int32, (CHUNK, HEAD_DIM), 1)
        dys = dy * s
        db = jnp.zeros((CHUNK, HEAD_DIM), f32)
        for g in range(A_GROUPS):
            sl = slice(g * HEAD_DIM, (g + 1) * HEAD_DIM)
            wm = jnp.where(tri, w_ref[g], 0.0).astype(bf16)
            mixed = _dot(wm, vn[:, sl]) + bt_ref[:, g:g + 1]
            dmix = dys[:, sl] * u[:, sl]
            dp_ref[:, sl] = (dys[:, sl] * mixed * gu[:, sl]).astype(bf16)
            dp_ref[:, 2 * da + g * HEAD_DIM:2 * da + (g + 1) * HEAD_DIM] = (
                dy[:, sl] * u[:, sl] * mixed * gs[:, sl]).astype(bf16)
            dmb = dmix.astype(bf16)
            dw_ref[g] += jnp.where(tri, _dot(dmb, vn[:, sl], NT), 0.0)
            dvn_ref[:, sl] = _dot(wm, dmb, TN)
            db = db + jnp.where(lane == g, jnp.sum(dmix, axis=1, keepdims=True), 0.0)
        db_ref[...] += db
        dvn = dvn_ref[...]
        dlg_ref[...] += jnp.sum(dvn * vhat, axis=0, keepdims=True)
        dlb_ref[...] += jnp.sum(dvn, axis=0, keepdims=True)
        dvhat = dvn * lg
        dv = rs * (dvhat - jnp.mean(dvhat, axis=-1, keepdims=True)
                   - vhat * jnp.mean(dvhat * vhat, axis=-1, keepdims=True))
        dp_ref[:, da:2 * da] = (dv * gv).astype(bf16)

    blk = lambda cb: pl.BlockSpec((CHUNK, da), lambda c: (c, cb))
    full = lambda shp: pl.BlockSpec(shp, lambda c: (0,) * len(shp))
    return pl.pallas_call(
        body, name="sgu_bwd", grid=(S // CHUNK,),
        in_specs=[blk(0), blk(1), blk(2), blk(0), full((1, da)), full((1, da)),
                  full((A_GROUPS, CHUNK, CHUNK)), full((CHUNK, A_GROUPS))],
        out_specs=[pl.BlockSpec((CHUNK, 3 * da), lambda c: (c, 0)), full((A_GROUPS, CHUNK, CHUNK)),
                   full((CHUNK, HEAD_DIM)), full((1, da)), full((1, da))],
        out_shape=[SDS((S, n_proj), bf16), SDS((A_GROUPS, CHUNK, CHUNK), f32), SDS((CHUNK, HEAD_DIM), f32),
                   SDS((1, da), f32), SDS((1, da), f32)],
        scratch_shapes=[pltpu.VMEM((CHUNK, da), f32)], compiler_params=_params(),
    )(proj, proj, proj, dy, ln_g, ln_b, w_s, b_t)


def _sb_scores(q, kblk, kb, rows, cols, masked):
    z = _dot(q, kblk, NT) * (1.0 / math.sqrt(HEAD_DIM))
    t = jnp.log1p(jnp.exp(-jnp.abs(z)))
    log_1mb = -(jnp.maximum(z, 0.0) + t)
    log_beta = jnp.minimum(z, 0.0) - t
    if not masked:
        return None, log_beta, log_1mb
    causal = (cols + kb * TK) < rows
    return causal, log_beta, jnp.where(causal, log_1mb, 0.0)


def _sb_tiles(i):
    rows = i * TQ + lax.broadcasted_iota(jnp.int32, (TQ, TK), 0)
    cols = lax.broadcasted_iota(jnp.int32, (TQ, TK), 1)
    r_i = lax.broadcasted_iota(jnp.int32, (TK, TK), 0)
    c_i = lax.broadcasted_iota(jnp.int32, (TK, TK), 1)
    return rows, cols, (r_i > c_i).astype(bf16), (r_i < c_i).astype(bf16)


def _suffix_sum(t, tri):
    hi = t.astype(bf16)
    lo = (t - hi.astype(f32)).astype(bf16)
    return _dot(hi, tri) + _dot(lo, tri)


def _sb_fwd(proj, y_prev, col0):
    S = proj.shape[0]
    D = y_prev.shape[1]
    dh = N_HEADS * HEAD_DIM
    n_diag = TQ // TK

    def body(q_ref, k_ref, v_ref, z_ref, yp_ref, y_ref, o_ref, car_ref, qb, kb_s, vb_s, c_ref):
        i = pl.program_id(0)

        @pl.when(i == 0)
        def _():
            kb_s[...] = k_ref[...].astype(bf16)
            vb_s[...] = v_ref[...].astype(bf16)

        qb[...] = q_ref[...].astype(bf16)
        o_ref[...] = jnp.zeros_like(o_ref)
        c_ref[...] = jnp.zeros_like(c_ref)
        car_ref[...] = jnp.zeros_like(car_ref)
        nkb = (i + 1) * n_diag
        rows, cols, upper, _ = _sb_tiles(i)

        def make_step(masked):
            def step(jj, carry):
                kb = nkb - 1 - jj
                off = pl.multiple_of(kb * TK, TK)
                for h in range(N_HEADS):
                    sl = slice(h * HEAD_DIM, (h + 1) * HEAD_DIM)
                    causal, log_beta, log_1mb = _sb_scores(qb[:, sl], kb_s[pl.ds(off, TK), sl], kb, rows, cols, masked)
                    c = c_ref[h]
                    e = jnp.exp(log_beta + _suffix_sum(log_1mb, upper) + c)
                    a = jnp.where(causal, e, 0.0) if masked else e
                    o_ref[:, sl] += _dot(a.astype(bf16), vb_s[pl.ds(off, TK), sl])
                    car_ref[h] = jnp.where(cols == kb, c, car_ref[h])
                    c_ref[h] = c + jnp.sum(log_1mb, axis=1, keepdims=True)
                return carry
            return step

        lax.fori_loop(0, n_diag, make_step(True), 0)
        lax.fori_loop(n_diag, nkb, make_step(False), 0)
        z = z_ref[...]
        y_ref[...] = (o_ref[...] * (z * jax.nn.sigmoid(z))).astype(bf16)

    cb = col0 * HEAD_DIM // dh
    qspec = lambda k: pl.BlockSpec((TQ, dh), lambda i: (i, cb + k))
    kspec = lambda k: pl.BlockSpec((S, dh), lambda i: (0, cb + k))
    return pl.pallas_call(
        body, name="sb_fwd", grid=(S // TQ,),
        in_specs=[qspec(0), kspec(1), kspec(2), qspec(3), pl.BlockSpec(memory_space=pl.ANY)],
        out_specs=[pl.BlockSpec((TQ, dh), lambda i: (i, A_GROUPS * HEAD_DIM // dh)),
                   pl.BlockSpec((TQ, dh), lambda i: (i, 0)),
                   pl.BlockSpec((N_HEADS, TQ, TK), lambda i: (0, i, 0))],
        out_shape=[SDS((S, D), bf16), SDS((S, dh), f32), SDS((N_HEADS, S, TK), f32)],
        input_output_aliases={4: 0},
        scratch_shapes=[pltpu.VMEM((TQ, dh), bf16), pltpu.VMEM((S, dh), bf16), pltpu.VMEM((S, dh), bf16),
                        pltpu.VMEM((N_HEADS, TQ, TK), f32)],
        compiler_params=_params(),
    )(proj, proj, proj, proj, y_prev)


def _sb_bwd(proj, o, car, dy, dproj_prev, col0):
    S = proj.shape[0]
    n_i = S // TQ
    dh = N_HEADS * HEAD_DIM
    n_diag = TQ // TK
    cb = col0 * HEAD_DIM // dh
    scale = 1.0 / math.sqrt(HEAD_DIM)

    def body(q_ref, k_ref, v_ref, z_ref, o_ref, car_ref, dy_ref, dpp_ref,
             dp_ref, qb, kb_s, vb_s, dob, p_ref, dq_acc, dk_acc, dv_acc, st_a, st_b, st_k, st_v):
        i = pl.program_id(0)

        def put(stage_ref, row0, nrows, k):
            pltpu.sync_copy(stage_ref, dp_ref.at[pl.ds(row0, nrows), pl.ds((cb + k) * dh, dh)])

        @pl.when(i == 0)
        def _():
            kb_s[...] = k_ref[...].astype(bf16)
            vb_s[...] = v_ref[...].astype(bf16)
            dk_acc[...] = jnp.zeros_like(dk_acc)
            dv_acc[...] = jnp.zeros_like(dv_acc)

        s, gs = _silu_and_grad(z_ref[...])
        dy = dy_ref[...]
        st_b[...] = (dy * o_ref[...] * gs).astype(bf16)
        dob[...] = (dy * s).astype(bf16)
        qb[...] = q_ref[...].astype(bf16)
        p_ref[...] = jnp.zeros_like(p_ref)
        dq_acc[...] = jnp.zeros_like(dq_acc)
        nkb = (i + 1) * n_diag
        rows, cols, upper, lower = _sb_tiles(i)

        def make_step(masked):
            def step(kb, carry):
                off = pl.multiple_of(kb * TK, TK)
                for h in range(N_HEADS):
                    sl = slice(h * HEAD_DIM, (h + 1) * HEAD_DIM)
                    q, kblk = qb[:, sl], kb_s[pl.ds(off, TK), sl]
                    do = dob[:, sl]
                    causal, log_beta, log_1mb = _sb_scores(q, kblk, kb, rows, cols, masked)
                    c = jnp.sum(jnp.where(cols == kb, car_ref[h], 0.0), axis=1, keepdims=True)
                    e = jnp.exp(log_beta + _suffix_sum(log_1mb, upper) + c)
                    a = jnp.where(causal, e, 0.0) if masked else e
                    g = _dot(do, vb_s[pl.ds(off, TK), sl], NT) * a
                    p = p_ref[h]
                    prefix = _suffix_sum(g, lower) + p
                    beta = jnp.exp(log_beta)
                    dzz = g * (1.0 - beta) - beta * prefix
                    if masked:
                        dzz = jnp.where(causal, dzz, 0.0)
                    dzz = (dzz * scale).astype(bf16)
                    dq_acc[:, sl] += _dot(dzz, kblk)
                    dk_acc[pl.ds(off, TK), sl] += _dot(dzz, q, TN)
                    dv_acc[pl.ds(off, TK), sl] += _dot(a.astype(bf16), do, TN)
                    p_ref[h] = p + jnp.sum(g, axis=1, keepdims=True)
                return carry
            return step

        lax.fori_loop(0, nkb - n_diag, make_step(False), 0)
        lax.fori_loop(nkb - n_diag, nkb, make_step(True), 0)
        st_a[...] = dq_acc[...].astype(bf16)
        row0 = pl.multiple_of(i * TQ, TQ)
        put(st_a, row0, TQ, 0)
        put(st_b, row0, TQ, 3)

        @pl.when(i == n_i - 1)
        def _():
            st_k[...] = dk_acc[...].astype(bf16)
            st_v[...] = dv_acc[...].astype(bf16)
            put(st_k, 0, S, 1)
            put(st_v, 0, S, 2)

    qspec = lambda k: pl.BlockSpec((TQ, dh), lambda i: (i, cb + k))
    kspec = lambda k: pl.BlockSpec((S, dh), lambda i: (0, cb + k))
    return pl.pallas_call(
        body, name="sb_bwd", grid=(n_i,),
        in_specs=[qspec(0), kspec(1), kspec(2), qspec(3),
                  pl.BlockSpec((TQ, dh), lambda i: (i, 0)),
                  pl.BlockSpec((N_HEADS, TQ, TK), lambda i: (0, i, 0)),
                  pl.BlockSpec((TQ, dh), lambda i: (i, A_GROUPS * HEAD_DIM // dh)),
                  pl.BlockSpec(memory_space=pl.ANY)],
        out_specs=pl.BlockSpec(memory_space=pl.ANY),
        out_shape=SDS(dproj_prev.shape, bf16),
        input_output_aliases={7: 0},
        scratch_shapes=[pltpu.VMEM((TQ, dh), bf16), pltpu.VMEM((S, dh), bf16), pltpu.VMEM((S, dh), bf16),
                        pltpu.VMEM((TQ, dh), bf16), pltpu.VMEM((N_HEADS, TQ, TK), f32), pltpu.VMEM((TQ, dh), f32),
                        pltpu.VMEM((S, dh), f32), pltpu.VMEM((S, dh), f32),
                        pltpu.VMEM((TQ, dh), bf16), pltpu.VMEM((TQ, dh), bf16),
                        pltpu.VMEM((S, dh), bf16), pltpu.VMEM((S, dh), bf16)],
        compiler_params=_params(56),
    )(proj, proj, proj, proj, o, car, dy, dproj_prev)


def _mem_kv(mem, mg_row, w_kv_g):
    M, D = mem.shape
    rb, n = w_kv_g.shape[1], w_kv_g.shape[2]

    def body(m_ref, g_ref, w_ref, kv_ref):
        mv = m_ref[...]
        r = lax.rsqrt(jnp.mean(mv * mv, axis=-1, keepdims=True) + EPS)
        mh = (mv * r * g_ref[...]).astype(bf16)
        kv_ref[...] = _dot(mh, w_ref[...].reshape(N_DEV * rb, n))

    return pl.pallas_call(
        body, name="mem_kv", grid=(1,),
        in_specs=[pl.BlockSpec((M, D), lambda i: (0, 0)), pl.BlockSpec((1, D), lambda i: (0, 0)),
                  pl.BlockSpec((N_DEV, rb, n), lambda i: (0, 0, 0))],
        out_specs=pl.BlockSpec((M, n), lambda i: (0, 0)),
        out_shape=SDS((M, n), f32), compiler_params=_params(),
    )(mem, mg_row, w_kv_g)


def _xattn_head(q_ref, kv_ref, qg, kg, h):
    dc = N_HEADS * HEAD_DIM
    sl = slice(h * HEAD_DIM, (h + 1) * HEAD_DIM)
    qh = q_ref[:, sl]
    rq = lax.rsqrt(jnp.mean(qh * qh, axis=-1, keepdims=True) + EPS)
    qhat = qh * rq
    qn = (qhat * qg).astype(bf16)
    kh = kv_ref[:, sl]
    rk = lax.rsqrt(jnp.mean(kh * kh, axis=-1, keepdims=True) + EPS)
    kn = (kh * rk * kg).astype(bf16)
    vh = kv_ref[:, dc + h * HEAD_DIM:dc + (h + 1) * HEAD_DIM].astype(bf16)
    s = _dot(qn, kn, NT) * (1.0 / math.sqrt(HEAD_DIM))
    e = jnp.exp(s - jnp.max(s, axis=-1, keepdims=True))
    p = e / jnp.sum(e, axis=-1, keepdims=True)
    o = _dot(p.astype(bf16), vh)
    return sl, rq, qhat, qn, kn, vh, p, o


def _xattn_fwd(proj, kv, qg_row, kg_row, y_prev, col0, tq=512):
    S = proj.shape[0]
    D = y_prev.shape[1]
    dc = N_HEADS * HEAD_DIM
    M = kv.shape[0]

    def body(q_ref, z_ref, kv_ref, qg_ref, kg_ref, yp_ref, y_ref):
        for h in range(N_HEADS):
            sl, _, _, _, _, _, _, o = _xattn_head(q_ref, kv_ref, qg_ref[...], kg_ref[...], h)
            z = z_ref[:, sl]
            y_ref[:, sl] = (o * (z * jax.nn.sigmoid(z))).astype(bf16)

    full = lambda shp: pl.BlockSpec(shp, lambda i: (0,) * len(shp))
    return pl.pallas_call(
        body, name="xattn_fwd", grid=(S // tq,),
        in_specs=[pl.BlockSpec((tq, dc), lambda i: (i, col0)), pl.BlockSpec((tq, dc), lambda i: (i, col0 + 1)),
                  full((M, 2 * dc)), full((1, HEAD_DIM)), full((1, HEAD_DIM)), pl.BlockSpec(memory_space=pl.ANY)],
        out_specs=pl.BlockSpec((tq, dc), lambda i: (i, D // dc - 1)),
        out_shape=SDS((S, D), bf16), input_output_aliases={5: 0}, compiler_params=_params(),
    )(proj, proj, kv, qg_row, kg_row, y_prev)


def _xattn_bwd(proj, kv, qg_row, kg_row, dy, dproj_prev, col0, tq=512):
    S = proj.shape[0]
    D = dy.shape[1]
    dc = N_HEADS * HEAD_DIM
    M = kv.shape[0]

    def body(q_ref, z_ref, kv_ref, qg_ref, kg_ref, dy_ref, dpp_ref, dp_ref, dkn_ref, dv_ref, dqg_ref):
        @pl.when(pl.program_id(0) == 0)
        def _():
            dkn_ref[...] = jnp.zeros_like(dkn_ref)
            dv_ref[...] = jnp.zeros_like(dv_ref)
            dqg_ref[...] = jnp.zeros_like(dqg_ref)

        qg = qg_ref[...]
        for h in range(N_HEADS):
            sl, rq, qhat, qn, kn, vh, p, o = _xattn_head(q_ref, kv_ref, qg, kg_ref[...], h)
            s, gs = _silu_and_grad(z_ref[:, sl])
            dyh = dy_ref[:, sl]
            dp_ref[:, dc + h * HEAD_DIM:dc + (h + 1) * HEAD_DIM] = (dyh * o * gs).astype(bf16)
            dob = (dyh * s).astype(bf16)
            dpr = _dot(dob, vh, NT)
            dv_ref[:, sl] += _dot(p.astype(bf16), dob, TN)
            ds = (p * (dpr - jnp.sum(p * dpr, axis=-1, keepdims=True)) * (1.0 / math.sqrt(HEAD_DIM))).astype(bf16)
            dqn = _dot(ds, kn)
            dkn_ref[:, sl] += _dot(ds, qn, TN)
            dqg_ref[...] += jnp.sum(dqn * qhat, axis=0, keepdims=True)
            dqhat = dqn * qg
            dp_ref[:, sl] = (rq * (dqhat - qhat * jnp.mean(dqhat * qhat, axis=-1, keepdims=True))).astype(bf16)

    full = lambda shp: pl.BlockSpec(shp, lambda i: (0,) * len(shp))
    return pl.pallas_call(
        body, name="xattn_bwd", grid=(S // tq,),
        in_specs=[pl.BlockSpec((tq, dc), lambda i: (i, col0)), pl.BlockSpec((tq, dc), lambda i: (i, col0 + 1)),
                  full((M, 2 * dc)), full((1, HEAD_DIM)), full((1, HEAD_DIM)),
                  pl.BlockSpec((tq, dc), lambda i: (i, D // dc - 1)), pl.BlockSpec(memory_space=pl.ANY)],
        out_specs=[pl.BlockSpec((tq, 2 * dc), lambda i: (i, col0 // 2)), full((M, dc)), full((M, dc)),
                   full((1, HEAD_DIM))],
        out_shape=[SDS(dproj_prev.shape, bf16), SDS((M, dc), f32), SDS((M, dc), f32), SDS((1, HEAD_DIM), f32)],
        input_output_aliases={6: 0}, compiler_params=_params(),
    )(proj, proj, kv, qg_row, kg_row, dy, dproj_prev)


def _mem_bwd(mem, mg_row, kv, dkn, dv, kg_row, w_kv_g):
    M, D = mem.shape
    rb, n = w_kv_g.shape[1], w_kv_g.shape[2]
    dc = n // 2

    def body(m_ref, g_ref, kv_ref, dkn_ref, dv_ref, kg_ref, w_ref, dw_ref, dmg_ref, dkg_ref, dkv_ref):
        mv = m_ref[...]
        r = lax.rsqrt(jnp.mean(mv * mv, axis=-1, keepdims=True) + EPS)
        mhat = mv * r
        mh = (mhat * g_ref[...]).astype(bf16)
        kg = kg_ref[...]
        dkg = jnp.zeros((1, HEAD_DIM), f32)
        for h in range(N_HEADS):
            sl = slice(h * HEAD_DIM, (h + 1) * HEAD_DIM)
            kh = kv_ref[:, sl]
            rk = lax.rsqrt(jnp.mean(kh * kh, axis=-1, keepdims=True) + EPS)
            khat = kh * rk
            dkn_h = dkn_ref[:, sl]
            dkg = dkg + jnp.sum(dkn_h * khat, axis=0, keepdims=True)
            dkhat = dkn_h * kg
            dkv_ref[:, sl] = (rk * (dkhat - khat * jnp.mean(dkhat * khat, axis=-1, keepdims=True))).astype(bf16)
        dkv_ref[:, dc:] = dv_ref[...].astype(bf16)
        dkg_ref[...] = dkg
        dkv = dkv_ref[...]
        dw_ref[...] = _dot(mh, dkv, TN).astype(bf16).reshape(N_DEV, rb, n)
        dmh = _dot(dkv, w_ref[...].reshape(N_DEV * rb, n), NT)
        dmg_ref[...] = jnp.sum(dmh * mhat, axis=0, keepdims=True)

    full = lambda shp: pl.BlockSpec(shp, lambda i: (0,) * len(shp))
    wspec = full((N_DEV, rb, n))
    return pl.pallas_call(
        body, name="mem_bwd", grid=(1,),
        in_specs=[full((M, D)), full((1, D)), full((M, n)), full((M, dc)), full((M, dc)), full((1, HEAD_DIM)), wspec],
        out_specs=[wspec, full((1, D)), full((1, HEAD_DIM))],
        out_shape=[SDS((N_DEV, rb, n), bf16), SDS((1, D), f32), SDS((1, HEAD_DIM), f32)],
        scratch_shapes=[pltpu.VMEM((M, n), bf16)], compiler_params=_params(),
    )(mem, mg_row, kv, dkn, dv, kg_row, w_kv_g)


SMALL = ("norm_g", "sgu_ln_g", "sgu_ln_b", "sgu_w", "sgu_b", "mem_norm_g", "q_norm_g", "k_norm_g")


def _pack_small(parts):
    flat = jnp.concatenate([parts[n].reshape(-1) for n in SMALL])
    pad = (-flat.shape[0]) % (8 * 128)
    return jnp.pad(flat, (0, pad)).reshape(-1, 128)


def _unpack_small(packed, like):
    flat = packed.reshape(-1)
    out, off = {}, 0
    for n in SMALL:
        size = math.prod(like[n].shape)
        out[n] = flat[off:off + size].reshape(like[n].shape)
        off += size
    return out


def kernel(x, mem, norm_g, w_in, sgu_ln_g, sgu_ln_b, sgu_w, sgu_b, mem_norm_g, w_mem_kv, q_norm_g, k_norm_g, w_out, loss_target, m_norm_g, m_w_in, m_sgu_ln_g, m_sgu_ln_b, m_sgu_w, m_sgu_b, m_mem_norm_g, m_w_mem_kv, m_q_norm_g, m_k_norm_g, m_w_out, v_norm_g, v_w_in, v_sgu_ln_g, v_sgu_ln_b, v_sgu_w, v_sgu_b, v_mem_norm_g, v_w_mem_kv, v_q_norm_g, v_k_norm_g, v_w_out):
    L, D, wc = w_in.shape
    S = x.shape[1]
    da = D // 2
    xs = x.reshape(S, D)
    mems = mem.reshape(mem.shape[1], D)
    tgt = loss_target.reshape(S, D)
    row = lambda a, l: a[l].reshape(1, -1)
    tie = lambda a, tok: a + tok[0:1, 0:1]
    sb_col, xa_col = 3 * da // HEAD_DIM, (3 * da + D) // (D // 4)

    w_b = [(_cast_bf16("cast_w_in", w_in, l, 512), _cast_bf16("cast_w_kv", w_mem_kv, l, 256),
            _cast_bf16("cast_w_out", w_out, l, 256)) for l in range(L)]
    g_in = [None] * L
    g_rest = [None] * L
    g_in[0] = _exchange_start("gather_w_in0", [w_b[0][0]], False, w_b[0][0])

    acts = []
    xl = xs
    for l in range(L):
        g_row = row(norm_g, l)
        if l == 0:
            g_row = tie(g_row, g_in[0]["token"])
        (w_in_g,) = _exchange_wait(g_in[l], g_row)
        g_rest[l] = _exchange_start(f"gather_w_rest{l}", [w_b[l][1], w_b[l][2]], False, w_in_g)
        proj, h = _rms_proj(xl, tie(g_row, g_rest[l]["token"]), w_in_g)
        lg_row = row(sgu_ln_g, l)
        if l + 1 < L:
            g_in[l + 1] = _exchange_start(f"gather_w_in{l + 1}", [w_b[l + 1][0]], False, proj)
            lg_row = tie(lg_row, g_in[l + 1]["token"])
        y = _sgu_fwd(proj, lg_row, row(sgu_ln_b, l), sgu_w[l], sgu_b[l].T)
        y, o_b, car = _sb_fwd(proj, y, sb_col)
        w_kv_g, w_out_g = _exchange_wait(g_rest[l], o_b)
        kv = _mem_kv(mems, row(mem_norm_g, l), w_kv_g)
        y = _xattn_fwd(proj, kv, row(q_norm_g, l), row(k_norm_g, l), y, xa_col)
        x_next = _out_proj(xl, y, w_out_g)
        acts.append((xl, proj, h, y, o_b, car, kv, w_in_g, w_kv_g, w_out_g))
        xl = x_next

    dx, dxb, loss_part = _loss_and_grad(xl, tgt, 512)
    loss = lax.psum(loss_part[0, 0], ("x", "y", "c"))

    scatters = [None] * L
    small = {n: [None] * L for n in SMALL}
    for l in reversed(range(L)):
        xl, proj, h, y, o_b, car, kv, w_in_g, w_kv_g, w_out_g = acts[l]
        dy = _out_bwd_dy(dxb, w_out_g)
        g_out = _tn_grad("out_bwd_dw", y, dxb, D // N_DEV, 512, rows_major=True)
        s_out = _exchange_start(f"scatter_g_out{l}", [g_out], True, g_out)
        dproj, d_sw, d_sb, d_lg, d_lb = _sgu_bwd(proj, dy, tie(row(sgu_ln_g, l), s_out["token"]), row(sgu_ln_b, l),
                                                 sgu_w[l], sgu_b[l].T)
        dproj = _sb_bwd(proj, o_b, car, dy, dproj, sb_col)
        dproj, dkn, dv, d_qg = _xattn_bwd(proj, kv, row(q_norm_g, l), row(k_norm_g, l), dy, dproj, xa_col)
        g_kv, d_mg, d_kg = _mem_bwd(mems, row(mem_norm_g, l), kv, dkn, dv, row(k_norm_g, l), w_kv_g)
        g_in_l = _tn_grad("in_bwd_dw", h, dproj, 512, wc, rows_major=False)
        s_rest = _exchange_start(f"scatter_g_rest{l}", [g_kv, g_in_l], True, g_in_l)
        dx, dxb, d_ng = _in_bwd_dx(dproj, w_in_g, xl, tie(row(norm_g, l), s_rest["token"]), dx)
        scatters[l] = (s_out, s_rest)
        for n, val in (("norm_g", d_ng), ("sgu_ln_g", d_lg), ("sgu_ln_b", d_lb), ("sgu_w", d_sw),
                       ("sgu_b", d_sb[:, :A_GROUPS].T), ("mem_norm_g", d_mg), ("q_norm_g", d_qg), ("k_norm_g", d_kg)):
            small[n][l] = val.reshape(-1)

    weights = dict(norm_g=norm_g, sgu_ln_g=sgu_ln_g, sgu_ln_b=sgu_ln_b, sgu_w=sgu_w, sgu_b=sgu_b,
                   mem_norm_g=mem_norm_g, q_norm_g=q_norm_g, k_norm_g=k_norm_g)
    moms_m = dict(norm_g=m_norm_g, sgu_ln_g=m_sgu_ln_g, sgu_ln_b=m_sgu_ln_b, sgu_w=m_sgu_w, sgu_b=m_sgu_b,
                  mem_norm_g=m_mem_norm_g, q_norm_g=m_q_norm_g, k_norm_g=m_k_norm_g)
    moms_v = dict(norm_g=v_norm_g, sgu_ln_g=v_sgu_ln_g, sgu_ln_b=v_sgu_ln_b, sgu_w=v_sgu_w, sgu_b=v_sgu_b,
                  mem_norm_g=v_mem_norm_g, q_norm_g=v_q_norm_g, k_norm_g=v_k_norm_g)
    part = _pack_small({n: jnp.stack(small[n]).reshape(weights[n].shape) for n in SMALL})
    (r_small,) = _exchange("gather_small_grads", [part], scatter=False)
    sm = _small_sum_adam(r_small, _pack_small(weights), _pack_small(moms_m), _pack_small(moms_v))

    r_out = [_exchange_wait(scatters[l][0], sm[0])[0] for l in range(L)]
    res = {}
    res["w_out"] = _sum_adam("adam_w_out", r_out, w_out, m_w_out, v_w_out, 128)
    r_rest = [_exchange_wait(scatters[l][1], res["w_out"][0]) for l in reversed(range(L))][::-1]
    res["w_mem_kv"] = _sum_adam("adam_w_kv", [r[0] for r in r_rest], w_mem_kv, m_w_mem_kv, v_w_mem_kv, 256)
    res["w_in"] = _sum_adam("adam_w_in", [r[1] for r in r_rest], w_in, m_w_in, v_w_in, 256)
    sm = [_unpack_small(a, weights) for a in sm]
    for n in SMALL:
        res[n] = [a[n] for a in sm]

    order = ("norm_g", "w_in", "sgu_ln_g", "sgu_ln_b", "sgu_w", "sgu_b", "mem_norm_g", "w_mem_kv", "q_norm_g",
             "k_norm_g", "w_out")
    outs = [loss, dx.reshape(x.shape)]
    for k in range(4):
        outs += [res[n][k] for n in order]
    return tuple(outs)
```

```python
import functools
import math

import jax
import jax.numpy as jnp
from jax import lax
from jax.experimental import pallas as pl
from jax.experimental.pallas import tpu as pltpu

f32 = jnp.float32
bf16 = jnp.bfloat16
SDS = jax.ShapeDtypeStruct

N_DEV = 8
EPS = 1e-6
CHUNK = 128
A_GROUPS = 8
HEAD_DIM = 128
N_HEADS = 4
TQ = 256
TK = 128
ADAM_LR, ADAM_B1, ADAM_B2, ADAM_EPS, ADAM_WD, ADAM_STEP = 0.001, 0.9, 0.999, 1e-08, 0.01, 10
MIB = 1024 * 1024

NT = (((1,), (1,)), ((), ()))
TN = (((0,), (0,)), ((), ()))


def _params(vmem_mib=48):
    return pltpu.CompilerParams(vmem_limit_bytes=vmem_mib * MIB)


def _gelu_and_grad(x):
    e = lax.erf(x * (1.0 / math.sqrt(2.0)))
    cdf = 0.5 * (1.0 + e)
    pdf = jnp.exp(-0.5 * x * x) * (1.0 / math.sqrt(2.0 * math.pi))
    return x * cdf, cdf + x * pdf


def _gelu(x):
    return 0.5 * x * (1.0 + lax.erf(x * (1.0 / math.sqrt(2.0))))


def _silu_and_grad(z):
    sg = jax.nn.sigmoid(z)
    return z * sg, sg * (1.0 + z * (1.0 - sg))


def _dot(a, b, dims=None):
    if dims is None:
        return jnp.dot(a, b, preferred_element_type=f32)
    return lax.dot_general(a, b, dims, preferred_element_type=f32)


def _exchange_copies(src_refs, dst_refs, send_sems, recv_sems, loc_sems, scatter):
    x, y, c = lax.axis_index("x"), lax.axis_index("y"), lax.axis_index("c")
    me = 4 * x + 2 * y + c
    locals_, remotes = [], []
    for a, (s, d) in enumerate(zip(src_refs, dst_refs)):
        locals_.append(pltpu.make_async_copy(s.at[me] if scatter else s, d.at[me], loc_sems.at[a]))
        for m in range(1, N_DEV):
            px = 1 - x if m & 4 else x
            py = 1 - y if m & 2 else y
            pc = 1 - c if m & 1 else c
            peer = 4 * px + 2 * py + pc
            remotes.append(pltpu.make_async_remote_copy(
                src_ref=s.at[peer] if scatter else s, dst_ref=d.at[me],
                send_sem=send_sems.at[a * 7 + m - 1], recv_sem=recv_sems.at[a * 7 + m - 1],
                device_id=(px, py, pc), device_id_type=pl.DeviceIdType.MESH))
    return locals_, remotes


def _exchange_shapes(srcs, scatter):
    return [(s.shape if scatter else (N_DEV,) + s.shape) for s in srcs]


def _exchange(name, srcs, scatter):
    n = len(srcs)

    def body(*refs):
        locals_, remotes = _exchange_copies(refs[:n], refs[n:2 * n], *refs[2 * n:], scatter)
        for cp in locals_ + remotes:
            cp.start()
        for cp in remotes:
            cp.wait_recv()
        for cp in remotes:
            cp.wait_send()
        for cp in locals_:
            cp.wait()

    any_spec = pl.BlockSpec(memory_space=pl.ANY)
    outs = pl.pallas_call(
        body, name=name, out_shape=[SDS(shp, s.dtype) for shp, s in zip(_exchange_shapes(srcs, scatter), srcs)],
        in_specs=[any_spec] * n, out_specs=[any_spec] * n,
        scratch_shapes=[pltpu.SemaphoreType.DMA((n * 7,)), pltpu.SemaphoreType.DMA((n * 7,)),
                        pltpu.SemaphoreType.DMA((n,))],
    )(*srcs)
    return list(outs)


_HBM = pl.BlockSpec(memory_space=pltpu.HBM)
_SEM = pl.BlockSpec(memory_space=pltpu.SEMAPHORE)
_EFFECT = pltpu.SideEffectType.DATAFLOW_SIDE_EFFECTING


def _split_start(name, bufs, n_remote, n_local, build, after):
    nb = len(bufs)

    def body(*refs):
        token = refs[-1]
        locals_, remotes = build(refs[:nb], *refs[nb + 1:nb + 4])
        for cp in locals_ + remotes:
            cp.start()
        token[...] = jnp.zeros_like(token)

    hbm = lambda a: pltpu.with_memory_space_constraint(a, pltpu.HBM)
    outs = pl.pallas_call(
        body, name=name,
        out_shape=(pltpu.SemaphoreType.DMA((n_remote,)), pltpu.SemaphoreType.DMA((n_remote,)),
                   pltpu.SemaphoreType.DMA((max(n_local, 1),)),
                   *[pltpu.HBM(b.shape, b.dtype) for b in bufs], SDS((8, 128), f32)),
        in_specs=[_HBM] * nb + [pl.BlockSpec(memory_space=pl.ANY)],
        out_specs=(_SEM, _SEM, _SEM, *[_HBM] * nb, pl.BlockSpec(memory_space=pltpu.VMEM)),
        input_output_aliases={k: 3 + k for k in range(nb)},
        compiler_params=pltpu.CompilerParams(has_side_effects=_EFFECT),
    )(*[hbm(b) for b in bufs], after)
    return dict(name=name, build=build, sems=outs[:3], bufs=outs[3:3 + nb], token=outs[-1])


def _split_wait(handle, after):
    build, bufs = handle["build"], handle["bufs"]
    nb = len(bufs)

    def body(*refs):
        locals_, remotes = build(refs[:nb], *refs[nb:nb + 3])
        for cp in remotes:
            cp.wait_recv()
        for cp in remotes:
            cp.wait_send()
        for cp in locals_:
            cp.wait()

    outs = pl.pallas_call(
        body, name=handle["name"] + "_wait",
        out_shape=tuple(pltpu.HBM(b.shape, b.dtype) for b in bufs),
        in_specs=[_HBM] * nb + [_SEM] * 3 + [pl.BlockSpec(memory_space=pl.ANY)],
        out_specs=tuple([_HBM] * nb),
        input_output_aliases={k: k for k in range(nb)},
        compiler_params=pltpu.CompilerParams(has_side_effects=_EFFECT),
    )(*bufs, *handle["sems"], after)
    return list(outs)


def _exchange_start(name, srcs, scatter, after):
    n = len(srcs)
    lands = [lax.empty(shp, s.dtype) for shp, s in zip(_exchange_shapes(srcs, scatter), srcs)]
    build = lambda refs, send, recv, loc: _exchange_copies(refs[:n], refs[n:], send, recv, loc, scatter)
    return _split_start(name, list(srcs) + lands, 7 * n, n, build, after)


def _exchange_wait(handle, after):
    outs = _split_wait(handle, after)
    return outs[len(outs) // 2:]


def _remote(src, dst, send_sems, recv_sems, k, to):
    return pltpu.make_async_remote_copy(src_ref=src, dst_ref=dst, send_sem=send_sems.at[k], recv_sem=recv_sems.at[k],
                                        device_id=to, device_id_type=pl.DeviceIdType.MESH)


def _other_chips(x, y):
    return [(1 - x, y), (x, 1 - y), (1 - x, 1 - y)]


def _gather2_start(name, src, after):
    def build(refs, send, recv, loc):
        s, d = refs
        x, y, c = lax.axis_index("x"), lax.axis_index("y"), lax.axis_index("c")
        me = 4 * x + 2 * y + c
        remotes = [_remote(s, d.at[me], send, recv, 0, (x, y, 1 - c))]
        remotes += [_remote(s, d.at[me], send, recv, 1 + k, (px, py, c)) for k, (px, py) in enumerate(_other_chips(x, y))]
        return [pltpu.make_async_copy(s, d.at[me], loc.at[0])], remotes

    return _split_start(name, [src, lax.empty((N_DEV,) + src.shape, src.dtype)], 4, 1, build, after)


def _gather2_forward(name, land, after):
    def build(refs, send, recv, loc):
        (d,) = refs
        x, y, c = lax.axis_index("x"), lax.axis_index("y"), lax.axis_index("c")
        slots = [4 * px + 2 * py + c for px, py in _other_chips(x, y)]
        return [], [_remote(d.at[sl], d.at[sl], send, recv, k, (x, y, 1 - c)) for k, sl in enumerate(slots)]

    return _split_start(name, [land], 3, 0, build, after)


def _scatter2_pair_start(name, srcs, after):
    n = len(srcs)

    def build(refs, send, recv, loc):
        x, y, c = lax.axis_index("x"), lax.axis_index("y"), lax.axis_index("c")
        locals_, remotes = [], []
        for a in range(n):
            s, mine, theirs = refs[a], refs[n + a], refs[2 * n + a]
            for q in range(4):
                locals_.append(pltpu.make_async_copy(s.at[2 * q + c], mine.at[q], loc.at[4 * a + q]))
                remotes.append(_remote(s.at[2 * q + 1 - c], theirs.at[q], send, recv, 4 * a + q, (x, y, 1 - c)))
        return locals_, remotes

    half = [lax.empty((4,) + s.shape[1:], s.dtype) for s in srcs]
    return _split_start(name, list(srcs) + half + [lax.empty(h.shape, h.dtype) for h in half], 4 * n, 4 * n, build, after)


def _scatter2_chip_start(name, pairs, after):
    n = len(pairs)

    def build(refs, send, recv, loc):
        x, y, c = lax.axis_index("x"), lax.axis_index("y"), lax.axis_index("c")
        my_q = 2 * x + y
        locals_, remotes = [], []
        for a in range(n):
            s, d = refs[a], refs[n + a]
            locals_.append(pltpu.make_async_copy(s.at[my_q], d.at[my_q], loc.at[a]))
            remotes += [_remote(s.at[2 * px + py], d.at[my_q], send, recv, 3 * a + k, (px, py, c))
                        for k, (px, py) in enumerate(_other_chips(x, y))]
        return locals_, remotes

    return _split_start(name, list(pairs) + [lax.empty(p.shape, p.dtype) for p in pairs], 3 * n, n, build, after)


def _pair_sum(name, mine, theirs, tr):
    _, R, C = mine.shape

    def body(a_ref, b_ref, o_ref):
        o_ref[...] = (a_ref[...].astype(f32) + b_ref[...].astype(f32)).astype(bf16)

    spec = pl.BlockSpec((None, tr, C), lambda q, i: (q, i, 0))
    return pl.pallas_call(
        body, name=name, grid=(4, R // tr), in_specs=[spec, spec], out_specs=spec,
        out_shape=SDS(mine.shape, bf16), compiler_params=_params(),
    )(mine, theirs)


def _cast_bf16(name, w, l, tr):
    _, R, C = w.shape

    def body(w_ref, o_ref):
        o_ref[...] = w_ref[...].astype(bf16)

    return pl.pallas_call(
        body, name=name, grid=(R // tr,),
        in_specs=[pl.BlockSpec((None, tr, C), lambda i: (l, i, 0))],
        out_specs=pl.BlockSpec((tr, C), lambda i: (i, 0)),
        out_shape=SDS((R, C), bf16), compiler_params=_params(),
    )(w)


def _adam_math(w, g, m, v):
    m2 = ADAM_B1 * m + (1.0 - ADAM_B1) * g
    v2 = ADAM_B2 * v + (1.0 - ADAM_B2) * (g * g)
    m_hat = m2 / (1.0 - ADAM_B1 ** ADAM_STEP)
    v_hat = v2 / (1.0 - ADAM_B2 ** ADAM_STEP)
    delta = -ADAM_LR * (m_hat / (jnp.sqrt(v_hat) + ADAM_EPS) + ADAM_WD * w)
    return delta, m2, v2


def _sum_adam(name, recv, w, m, v, l, prev, tr):
    L, R, C = w.shape
    slots = recv.shape[0]

    def body(r_ref, w_ref, m_ref, v_ref, *rest):
        g_ref, d_ref, m2_ref, v2_ref = rest[-4:]
        g = r_ref[0].astype(f32)
        for s in range(1, slots):
            g = g + r_ref[s].astype(f32)
        d, m2, v2 = _adam_math(w_ref[...], g, m_ref[...], v_ref[...])
        g_ref[...] = g
        d_ref[...] = d
        m2_ref[...] = m2
        v2_ref[...] = v2

    wspec = pl.BlockSpec((None, tr, C), lambda i: (l, i, 0))
    in_specs = [pl.BlockSpec((slots, tr, C), lambda i: (0, i, 0)), wspec, wspec, wspec]
    args = [recv, w, m, v]
    aliases = {}
    if prev is not None:
        in_specs += [pl.BlockSpec(memory_space=pl.ANY)] * 4
        args += list(prev)
        aliases = {4 + k: k for k in range(4)}
    return pl.pallas_call(
        body, name=name, grid=(R // tr,), in_specs=in_specs, out_specs=[wspec] * 4,
        out_shape=[SDS((L, R, C), f32)] * 4, input_output_aliases=aliases, compiler_params=_params(),
    )(*args)


def _small_sum_adam(recv, w, m, v):
    _, R, C = recv.shape

    def body(r_ref, w_ref, m_ref, v_ref, g_ref, d_ref, m2_ref, v2_ref):
        g = r_ref[0]
        for s in range(1, N_DEV):
            g = g + r_ref[s]
        d, m2, v2 = _adam_math(w_ref[...], g, m_ref[...], v_ref[...])
        g_ref[...] = g
        d_ref[...] = d
        m2_ref[...] = m2
        v2_ref[...] = v2

    return pl.pallas_call(
        body, name="small_sum_adam", out_shape=[SDS((R, C), f32)] * 4, compiler_params=_params(),
    )(recv, w, m, v)


def _loss_and_grad(xf, tgt, tm):
    S, D = xf.shape

    def body(x_ref, t_ref, dx_ref, dxb_ref, l_ref):
        i = pl.program_id(0)
        d = x_ref[...] - t_ref[...]
        dx = d * (1.0 / D)
        dx_ref[...] = dx
        dxb_ref[...] = dx.astype(bf16)
        e = d * d
        part = e[:, 0:128]
        for k in range(1, D // 128):
            part = part + e[:, k * 128:(k + 1) * 128]
        part = jnp.sum(part.reshape(tm // 8, 8, 128), axis=0)

        @pl.when(i == 0)
        def _():
            l_ref[...] = jnp.zeros_like(l_ref)

        l_ref[...] += part

        @pl.when(i == pl.num_programs(0) - 1)
        def _():
            tot = jnp.sum(l_ref[...], axis=1, keepdims=True)
            tot = jnp.sum(tot, axis=0, keepdims=True)
            l_ref[...] = jnp.broadcast_to(tot * (0.5 / D), l_ref.shape)

    row = pl.BlockSpec((tm, D), lambda i: (i, 0))
    return pl.pallas_call(
        body, name="loss_grad", grid=(S // tm,),
        in_specs=[row, row], out_specs=[row, row, pl.BlockSpec((8, 128), lambda i: (0, 0))],
        out_shape=[SDS((S, D), f32), SDS((S, D), bf16), SDS((8, 128), f32)], compiler_params=_params(),
    )(xf, tgt)


def _rms_proj(x, g_row, w_in_g, tm=512):
    S, D = x.shape
    wc = w_in_g.shape[2]
    tn = 256
    per = wc // tn
    n_out = N_DEV * wc

    def body(x_ref, g_ref, w_ref, proj_ref, h_ref):
        @pl.when(pl.program_id(1) == 0)
        def _():
            xv = x_ref[...]
            r = lax.rsqrt(jnp.mean(xv * xv, axis=-1, keepdims=True) + EPS)
            h_ref[...] = (xv * r * g_ref[...]).astype(bf16)

        proj_ref[...] = _dot(h_ref[...], w_ref[...])

    return pl.pallas_call(
        body, name="rms_proj", grid=(S // tm, n_out // tn),
        in_specs=[pl.BlockSpec((tm, D), lambda i, j: (i, 0)),
                  pl.BlockSpec((1, D), lambda i, j: (0, 0)),
                  pl.BlockSpec((None, D, tn), lambda i, j: (j // per, 0, j % per))],
        out_specs=[pl.BlockSpec((tm, tn), lambda i, j: (i, j)), pl.BlockSpec((tm, D), lambda i, j: (i, 0))],
        out_shape=[SDS((S, n_out), f32), SDS((S, D), bf16)], compiler_params=_params(),
    )(x, g_row, w_in_g)


def _out_proj(x, y, w_out_g, tm=512, tn=512):
    S, D = x.shape
    rb = w_out_g.shape[1]

    def body(x_ref, y_ref, w_ref, o_ref):
        w = w_ref[...].reshape(N_DEV * rb, tn)
        o_ref[...] = x_ref[...] + _dot(y_ref[...], w)

    return pl.pallas_call(
        body, name="out_proj", grid=(S // tm, D // tn),
        in_specs=[pl.BlockSpec((tm, tn), lambda i, j: (i, j)),
                  pl.BlockSpec((tm, D), lambda i, j: (i, 0)),
                  pl.BlockSpec((N_DEV, rb, tn), lambda i, j: (0, 0, j))],
        out_specs=pl.BlockSpec((tm, tn), lambda i, j: (i, j)),
        out_shape=SDS((S, D), f32), compiler_params=_params(),
    )(x, y, w_out_g)


def _out_bwd_dy(dxb, w_out_g, tm=512):
    S, D = dxb.shape
    rb = w_out_g.shape[1]

    def body(dx_ref, w_ref, o_ref):
        o_ref[...] = _dot(dx_ref[...], w_ref[...], NT)

    return pl.pallas_call(
        body, name="out_bwd_dy", grid=(S // tm, N_DEV),
        in_specs=[pl.BlockSpec((tm, D), lambda i, j: (i, 0)),
                  pl.BlockSpec((None, rb, D), lambda i, j: (j, 0, 0))],
        out_specs=pl.BlockSpec((tm, rb), lambda i, j: (i, j)),
        out_shape=SDS((S, D), f32), compiler_params=_params(),
    )(dxb, w_out_g)


def _tn_grad(name, a, b, tm, tn, rows_major):
    S, M = a.shape
    N = b.shape[1]
    if rows_major:
        out_shape = SDS((N_DEV, M // N_DEV, N), bf16)
        assert tm == M // N_DEV
        out_spec = pl.BlockSpec((None, tm, tn), lambda i, j: (i, 0, j))
    else:
        out_shape = SDS((N_DEV, M, N // N_DEV), bf16)
        assert tn == N // N_DEV
        out_spec = pl.BlockSpec((None, tm, tn), lambda i, j: (j, i, 0))

    def body(a_ref, b_ref, o_ref):
        o_ref[...] = _dot(a_ref[...], b_ref[...], TN).astype(bf16)

    return pl.pallas_call(
        body, name=name, grid=(M // tm, N // tn),
        in_specs=[pl.BlockSpec((S, tm), lambda i, j: (0, i)), pl.BlockSpec((S, tn), lambda i, j: (0, j))],
        out_specs=out_spec, out_shape=out_shape, compiler_params=_params(),
    )(a, b)


def _in_bwd_dx(dproj, w_in_g, x, g_row, dx_next, tm=512):
    S, D = x.shape
    wc = w_in_g.shape[2]

    def body(dp_ref, w_ref, x_ref, g_ref, dxn_ref, dx_ref, dxb_ref, dg_ref, acc_ref):
        i, k = pl.program_id(0), pl.program_id(1)

        @pl.when(k == 0)
        def _():
            acc_ref[...] = jnp.zeros_like(acc_ref)

        acc_ref[...] += _dot(dp_ref[...], w_ref[...], NT)

        @pl.when(jnp.logical_and(i == 0, k == 0))
        def _():
            dg_ref[...] = jnp.zeros_like(dg_ref)

        @pl.when(k == N_DEV - 1)
        def _():
            dh = acc_ref[...]
            xv = x_ref[...]
            r = lax.rsqrt(jnp.mean(xv * xv, axis=-1, keepdims=True) + EPS)
            xhat = xv * r
            dxhat = dh * g_ref[...]
            dx = r * (dxhat - xhat * jnp.mean(dxhat * xhat, axis=-1, keepdims=True)) + dxn_ref[...]
            dx_ref[...] = dx
            dxb_ref[...] = dx.astype(bf16)
            dg_ref[...] += jnp.sum(dh * xhat, axis=0, keepdims=True)

    row = pl.BlockSpec((tm, D), lambda i, k: (i, 0))
    return pl.pallas_call(
        body, name="in_bwd_dx", grid=(S // tm, N_DEV),
        in_specs=[pl.BlockSpec((tm, wc), lambda i, k: (i, k)),
                  pl.BlockSpec((None, D, wc), lambda i, k: (k, 0, 0)),
                  row, pl.BlockSpec((1, D), lambda i, k: (0, 0)), row],
        out_specs=[row, row, pl.BlockSpec((1, D), lambda i, k: (0, 0))],
        out_shape=[SDS((S, D), f32), SDS((S, D), bf16), SDS((1, D), f32)],
        scratch_shapes=[pltpu.VMEM((tm, D), f32)], compiler_params=_params(56),
    )(dproj, w_in_g, x, g_row, dx_next)


def _sgu_fwd(proj, ln_g, ln_b, w_s, b_t):
    S = proj.shape[0]
    da = A_GROUPS * HEAD_DIM
    D = 2 * da

    def body(u_ref, v_ref, z_ref, lg_ref, lb_ref, w_ref, bt_ref, y_ref):
        u = _gelu(u_ref[...])
        v = _gelu(v_ref[...])
        z = z_ref[...]
        mu = jnp.mean(v, axis=-1, keepdims=True)
        xc = v - mu
        rs = lax.rsqrt(jnp.mean(xc * xc, axis=-1, keepdims=True) + EPS)
        vn = (xc * rs * lg_ref[...] + lb_ref[...]).astype(bf16)
        gate = u * (z * jax.nn.sigmoid(z))
        tri = lax.broadcasted_iota(jnp.int32, (CHUNK, CHUNK), 0) >= lax.broadcasted_iota(jnp.int32, (CHUNK, CHUNK), 1)
        for g in range(A_GROUPS):
            sl = slice(g * HEAD_DIM, (g + 1) * HEAD_DIM)
            wm = jnp.where(tri, w_ref[g], 0.0).astype(bf16)
            mixed = _dot(wm, vn[:, sl]) + bt_ref[:, g:g + 1]
            y_ref[:, sl] = (gate[:, sl] * mixed).astype(bf16)

    blk = lambda cb: pl.BlockSpec((CHUNK, da), lambda c: (c, cb))
    full = lambda shp: pl.BlockSpec(shp, lambda c: (0,) * len(shp))
    return pl.pallas_call(
        body, name="sgu_fwd", grid=(S // CHUNK,),
        in_specs=[blk(0), blk(1), blk(2), full((1, da)), full((1, da)),
                  full((A_GROUPS, CHUNK, CHUNK)), full((CHUNK, A_GROUPS))],
        out_specs=blk(0), out_shape=SDS((S, D), bf16), compiler_params=_params(),
    )(proj, proj, proj, ln_g, ln_b, w_s, b_t)


def _sgu_bwd(proj, dy, ln_g, ln_b, w_s, b_t):
    S = proj.shape[0]
    da = A_GROUPS * HEAD_DIM
    n_proj = proj.shape[1]

    def body(u_ref, v_ref, z_ref, dy_ref, lg_ref, lb_ref, w_ref, bt_ref,
             dp_ref, dw_ref, db_ref, dlg_ref, dlb_ref, dvn_ref):
        @pl.when(pl.program_id(0) == 0)
        def _():
            dw_ref[...] = jnp.zeros_like(dw_ref)
            db_ref[...] = jnp.zeros_like(db_ref)
            dlg_ref[...] = jnp.zeros_like(dlg_ref)
            dlb_ref[...] = jnp.zeros_like(dlb_ref)

        up, vp, z, dy = u_ref[...], v_ref[...], z_ref[...], dy_ref[...]
        u, gu = _gelu_and_grad(up)
        v, gv = _gelu_and_grad(vp)
        s, gs = _silu_and_grad(z)
        mu = jnp.mean(v, axis=-1, keepdims=True)
        xc = v - mu
        rs = lax.rsqrt(jnp.mean(xc * xc, axis=-1, keepdims=True) + EPS)
        vhat = xc * rs
        lg = lg_ref[...]
        vn = (vhat * lg + lb_ref[...]).astype(bf16)
        tri = lax.broadcasted_iota(jnp.int32, (CHUNK, CHUNK), 0) >= lax.broadcasted_iota(jnp.int32, (CHUNK, CHUNK), 1)
        lane = lax.broadcasted_iota(jnp.int32, (CHUNK, HEAD_DIM), 1)
        dys = dy * s
        db = jnp.zeros((CHUNK, HEAD_DIM), f32)
        for g in range(A_GROUPS):
            sl = slice(g * HEAD_DIM, (g + 1) * HEAD_DIM)
            wm = jnp.where(tri, w_ref[g], 0.0).astype(bf16)
            mixed = _dot(wm, vn[:, sl]) + bt_ref[:, g:g + 1]
            dmix = dys[:, sl] * u[:, sl]
            dp_ref[:, sl] = (dys[:, sl] * mixed * gu[:, sl]).astype(bf16)
            dp_ref[:, 2 * da + g * HEAD_DIM:2 * da + (g + 1) * HEAD_DIM] = (
                dy[:, sl] * u[:, sl] * mixed * gs[:, sl]).astype(bf16)
            dmb = dmix.astype(bf16)
            dw_ref[g] += jnp.where(tri, _dot(dmb, vn[:, sl], NT), 0.0)
            dvn_ref[:, sl] = _dot(wm, dmb, TN)
            db = db + jnp.where(lane == g, jnp.sum(dmix, axis=1, keepdims=True), 0.0)
        db_ref[...] += db
        dvn = dvn_ref[...]
        dlg_ref[...] += jnp.sum(dvn * vhat, axis=0, keepdims=True)
        dlb_ref[...] += jnp.sum(dvn, axis=0, keepdims=True)
        dvhat = dvn * lg
        dv = rs * (dvhat - jnp.mean(dvhat, axis=-1, keepdims=True)
                   - vhat * jnp.mean(dvhat * vhat, axis=-1, keepdims=True))
        dp_ref[:, da:2 * da] = (dv * gv).astype(bf16)

    blk = lambda cb: pl.BlockSpec((CHUNK, da), lambda c: (c, cb))
    full = lambda shp: pl.BlockSpec(shp, lambda c: (0,) * len(shp))
    return pl.pallas_call(
        body, name="sgu_bwd", grid=(S // CHUNK,),
        in_specs=[blk(0), blk(1), blk(2), blk(0), full((1, da)), full((1, da)),
                  full((A_GROUPS, CHUNK, CHUNK)), full((CHUNK, A_GROUPS))],
        out_specs=[pl.BlockSpec((CHUNK, 3 * da), lambda c: (c, 0)), full((A_GROUPS, CHUNK, CHUNK)),
                   full((CHUNK, HEAD_DIM)), full((1, da)), full((1, da))],
        out_shape=[SDS((S, n_proj), bf16), SDS((A_GROUPS, CHUNK, CHUNK), f32), SDS((CHUNK, HEAD_DIM), f32),
                   SDS((1, da), f32), SDS((1, da), f32)],
        scratch_shapes=[pltpu.VMEM((CHUNK, da), f32)], compiler_params=_params(),
    )(proj, proj, proj, dy, ln_g, ln_b, w_s, b_t)


def _sb_scores(q, kblk, kb, rows, cols, masked):
    z = _dot(q, kblk, NT) * (1.0 / math.sqrt(HEAD_DIM))
    t = jnp.log1p(jnp.exp(-jnp.abs(z)))
    log_1mb = -(jnp.maximum(z, 0.0) + t)
    log_beta = jnp.minimum(z, 0.0) - t
    if not masked:
        return None, log_beta, log_1mb
    causal = (cols + kb * TK) < rows
    return causal, log_beta, jnp.where(causal, log_1mb, 0.0)


def _sb_tiles(i):
    rows = i * TQ + lax.broadcasted_iota(jnp.int32, (TQ, TK), 0)
    cols = lax.broadcasted_iota(jnp.int32, (TQ, TK), 1)
    r_i = lax.broadcasted_iota(jnp.int32, (TK, TK), 0)
    c_i = lax.broadcasted_iota(jnp.int32, (TK, TK), 1)
    return rows, cols, (r_i > c_i).astype(bf16), (r_i < c_i).astype(bf16)


def _suffix_sum(t, tri):
    hi = t.astype(bf16)
    lo = (t - hi.astype(f32)).astype(bf16)
    return _dot(hi, tri) + _dot(lo, tri)


def _sb_fwd(proj, y_prev, col0):
    S = proj.shape[0]
    D = y_prev.shape[1]
    dh = N_HEADS * HEAD_DIM
    n_diag = TQ // TK

    def body(q_ref, k_ref, v_ref, z_ref, yp_ref, y_ref, o_ref, car_ref, qb, kb_s, vb_s, c_ref):
        i = pl.program_id(0)

        @pl.when(i == 0)
        def _():
            kb_s[...] = k_ref[...].astype(bf16)
            vb_s[...] = v_ref[...].astype(bf16)

        qb[...] = q_ref[...].astype(bf16)
        o_ref[...] = jnp.zeros_like(o_ref)
        c_ref[...] = jnp.zeros_like(c_ref)
        car_ref[...] = jnp.zeros_like(car_ref)
        nkb = (i + 1) * n_diag
        rows, cols, upper, _ = _sb_tiles(i)

        def make_step(masked):
            def step(jj, carry):
                kb = nkb - 1 - jj
                off = pl.multiple_of(kb * TK, TK)
                hs = range(N_HEADS)
                sls = [slice(h * HEAD_DIM, (h + 1) * HEAD_DIM) for h in hs]
                sc = [_sb_scores(qb[:, sls[h]], kb_s[pl.ds(off, TK), sls[h]], kb, rows, cols, masked) for h in hs]
                suf = [_suffix_sum(sc[h][2], upper) for h in hs]
                cs = [c_ref[h] for h in hs]
                es = [jnp.exp(sc[h][1] + suf[h] + cs[h]) for h in hs]
                if masked:
                    es = [jnp.where(sc[h][0], es[h], 0.0) for h in hs]
                pv = [_dot(es[h].astype(bf16), vb_s[pl.ds(off, TK), sls[h]]) for h in hs]
                for h in hs:
                    o_ref[:, sls[h]] += pv[h]
                    car_ref[h] = jnp.where(cols == kb, cs[h], car_ref[h])
                    c_ref[h] = cs[h] + jnp.sum(sc[h][2], axis=1, keepdims=True)
                return carry
            return step

        lax.fori_loop(0, n_diag, make_step(True), 0)
        lax.fori_loop(n_diag, nkb, make_step(False), 0)
        z = z_ref[...]
        y_ref[...] = (o_ref[...] * (z * jax.nn.sigmoid(z))).astype(bf16)

    cb = col0 * HEAD_DIM // dh
    qspec = lambda k: pl.BlockSpec((TQ, dh), lambda i: (i, cb + k))
    kspec = lambda k: pl.BlockSpec((S, dh), lambda i: (0, cb + k))
    return pl.pallas_call(
        body, name="sb_fwd", grid=(S // TQ,),
        in_specs=[qspec(0), kspec(1), kspec(2), qspec(3), pl.BlockSpec(memory_space=pl.ANY)],
        out_specs=[pl.BlockSpec((TQ, dh), lambda i: (i, A_GROUPS * HEAD_DIM // dh)),
                   pl.BlockSpec((TQ, dh), lambda i: (i, 0)),
                   pl.BlockSpec((N_HEADS, TQ, TK), lambda i: (0, i, 0))],
        out_shape=[SDS((S, D), bf16), SDS((S, dh), f32), SDS((N_HEADS, S, TK), f32)],
        input_output_aliases={4: 0},
        scratch_shapes=[pltpu.VMEM((TQ, dh), bf16), pltpu.VMEM((S, dh), bf16), pltpu.VMEM((S, dh), bf16),
                        pltpu.VMEM((N_HEADS, TQ, TK), f32)],
        compiler_params=_params(),
    )(proj, proj, proj, proj, y_prev)


def _sb_bwd(proj, o, car, dy, dproj_prev, col0):
    S = proj.shape[0]
    n_i = S // TQ
    dh = N_HEADS * HEAD_DIM
    n_diag = TQ // TK
    cb = col0 * HEAD_DIM // dh
    scale = 1.0 / math.sqrt(HEAD_DIM)

    def body(q_ref, k_ref, v_ref, z_ref, o_ref, car_ref, dy_ref, dpp_ref,
             dp_ref, qb, kb_s, vb_s, dob, p_ref, dq_acc, dk_acc, dv_acc, st_a, st_b, st_k, st_v):
        i = pl.program_id(0)

        def put(stage_ref, row0, nrows, k):
            pltpu.sync_copy(stage_ref, dp_ref.at[pl.ds(row0, nrows), pl.ds((cb + k) * dh, dh)])

        @pl.when(i == 0)
        def _():
            kb_s[...] = k_ref[...].astype(bf16)
            vb_s[...] = v_ref[...].astype(bf16)
            dk_acc[...] = jnp.zeros_like(dk_acc)
            dv_acc[...] = jnp.zeros_like(dv_acc)

        s, gs = _silu_and_grad(z_ref[...])
        dy = dy_ref[...]
        st_b[...] = (dy * o_ref[...] * gs).astype(bf16)
        dob[...] = (dy * s).astype(bf16)
        qb[...] = q_ref[...].astype(bf16)
        p_ref[...] = jnp.zeros_like(p_ref)
        dq_acc[...] = jnp.zeros_like(dq_acc)
        nkb = (i + 1) * n_diag
        rows, cols, upper, lower = _sb_tiles(i)

        def make_step(masked):
            def step(kb, carry):
                off = pl.multiple_of(kb * TK, TK)
                hs = range(N_HEADS)
                sls = [slice(h * HEAD_DIM, (h + 1) * HEAD_DIM) for h in hs]
                qs = [qb[:, sls[h]] for h in hs]
                ks = [kb_s[pl.ds(off, TK), sls[h]] for h in hs]
                dos = [dob[:, sls[h]] for h in hs]
                sc = [_sb_scores(qs[h], ks[h], kb, rows, cols, masked) for h in hs]
                da = [_dot(dos[h], vb_s[pl.ds(off, TK), sls[h]], NT) for h in hs]
                suf = [_suffix_sum(sc[h][2], upper) for h in hs]
                onehot = cols == kb
                cs = [jnp.sum(jnp.where(onehot, car_ref[h], 0.0), axis=1, keepdims=True) for h in hs]
                es = [jnp.exp(sc[h][1] + suf[h] + cs[h]) for h in hs]
                if masked:
                    es = [jnp.where(sc[h][0], es[h], 0.0) for h in hs]
                gs_ = [da[h] * es[h] for h in hs]
                ps = [p_ref[h] for h in hs]
                pre = [_suffix_sum(gs_[h], lower) + ps[h] for h in hs]
                dzs = []
                for h in hs:
                    beta = jnp.exp(sc[h][1])
                    dzz = gs_[h] * (1.0 - beta) - beta * pre[h]
                    if masked:
                        dzz = jnp.where(sc[h][0], dzz, 0.0)
                    dzs.append((dzz * scale).astype(bf16))
                dqs = [_dot(dzs[h], ks[h]) for h in hs]
                dks = [_dot(dzs[h], qs[h], TN) for h in hs]
                dvs = [_dot(es[h].astype(bf16), dos[h], TN) for h in hs]
                for h in hs:
                    dq_acc[:, sls[h]] += dqs[h]
                    dk_acc[pl.ds(off, TK), sls[h]] += dks[h]
                    dv_acc[pl.ds(off, TK), sls[h]] += dvs[h]
                    p_ref[h] = ps[h] + jnp.sum(gs_[h], axis=1, keepdims=True)
                return carry
            return step

        lax.fori_loop(0, nkb - n_diag, make_step(False), 0)
        lax.fori_loop(nkb - n_diag, nkb, make_step(True), 0)
        st_a[...] = dq_acc[...].astype(bf16)
        row0 = pl.multiple_of(i * TQ, TQ)
        put(st_a, row0, TQ, 0)
        put(st_b, row0, TQ, 3)

        @pl.when(i == n_i - 1)
        def _():
            st_k[...] = dk_acc[...].astype(bf16)
            st_v[...] = dv_acc[...].astype(bf16)
            put(st_k, 0, S, 1)
            put(st_v, 0, S, 2)

    qspec = lambda k: pl.BlockSpec((TQ, dh), lambda i: (i, cb + k))
    kspec = lambda k: pl.BlockSpec((S, dh), lambda i: (0, cb + k))
    return pl.pallas_call(
        body, name="sb_bwd", grid=(n_i,),
        in_specs=[qspec(0), kspec(1), kspec(2), qspec(3),
                  pl.BlockSpec((TQ, dh), lambda i: (i, 0)),
                  pl.BlockSpec((N_HEADS, TQ, TK), lambda i: (0, i, 0)),
                  pl.BlockSpec((TQ, dh), lambda i: (i, A_GROUPS * HEAD_DIM // dh)),
                  pl.BlockSpec(memory_space=pl.ANY)],
        out_specs=pl.BlockSpec(memory_space=pl.ANY),
        out_shape=SDS(dproj_prev.shape, bf16),
        input_output_aliases={7: 0},
        scratch_shapes=[pltpu.VMEM((TQ, dh), bf16), pltpu.VMEM((S, dh), bf16), pltpu.VMEM((S, dh), bf16),
                        pltpu.VMEM((TQ, dh), bf16), pltpu.VMEM((N_HEADS, TQ, TK), f32), pltpu.VMEM((TQ, dh), f32),
                        pltpu.VMEM((S, dh), f32), pltpu.VMEM((S, dh), f32),
                        pltpu.VMEM((TQ, dh), bf16), pltpu.VMEM((TQ, dh), bf16),
                        pltpu.VMEM((S, dh), bf16), pltpu.VMEM((S, dh), bf16)],
        compiler_params=_params(56),
    )(proj, proj, proj, proj, o, car, dy, dproj_prev)


def _mem_kv(mem, mg_row, w_kv_g):
    M, D = mem.shape
    rb, n = w_kv_g.shape[1], w_kv_g.shape[2]

    def body(m_ref, g_ref, w_ref, kv_ref):
        mv = m_ref[...]
        r = lax.rsqrt(jnp.mean(mv * mv, axis=-1, keepdims=True) + EPS)
        mh = (mv * r * g_ref[...]).astype(bf16)
        kv_ref[...] = _dot(mh, w_ref[...].reshape(N_DEV * rb, n))

    return pl.pallas_call(
        body, name="mem_kv", grid=(1,),
        in_specs=[pl.BlockSpec((M, D), lambda i: (0, 0)), pl.BlockSpec((1, D), lambda i: (0, 0)),
                  pl.BlockSpec((N_DEV, rb, n), lambda i: (0, 0, 0))],
        out_specs=pl.BlockSpec((M, n), lambda i: (0, 0)),
        out_shape=SDS((M, n), f32), compiler_params=_params(),
    )(mem, mg_row, w_kv_g)


def _xattn_head(q_ref, kv_ref, qg, kg, h):
    dc = N_HEADS * HEAD_DIM
    sl = slice(h * HEAD_DIM, (h + 1) * HEAD_DIM)
    qh = q_ref[:, sl]
    rq = lax.rsqrt(jnp.mean(qh * qh, axis=-1, keepdims=True) + EPS)
    qhat = qh * rq
    qn = (qhat * qg).astype(bf16)
    kh = kv_ref[:, sl]
    rk = lax.rsqrt(jnp.mean(kh * kh, axis=-1, keepdims=True) + EPS)
    kn = (kh * rk * kg).astype(bf16)
    vh = kv_ref[:, dc + h * HEAD_DIM:dc + (h + 1) * HEAD_DIM].astype(bf16)
    s = _dot(qn, kn, NT) * (1.0 / math.sqrt(HEAD_DIM))
    e = jnp.exp(s - jnp.max(s, axis=-1, keepdims=True))
    p = e / jnp.sum(e, axis=-1, keepdims=True)
    o = _dot(p.astype(bf16), vh)
    return sl, rq, qhat, qn, kn, vh, p, o


def _xattn_fwd(proj, kv, qg_row, kg_row, y_prev, col0, tq=512):
    S = proj.shape[0]
    D = y_prev.shape[1]
    dc = N_HEADS * HEAD_DIM
    M = kv.shape[0]

    def body(q_ref, z_ref, kv_ref, qg_ref, kg_ref, yp_ref, y_ref):
        for h in range(N_HEADS):
            sl, _, _, _, _, _, _, o = _xattn_head(q_ref, kv_ref, qg_ref[...], kg_ref[...], h)
            z = z_ref[:, sl]
            y_ref[:, sl] = (o * (z * jax.nn.sigmoid(z))).astype(bf16)

    full = lambda shp: pl.BlockSpec(shp, lambda i: (0,) * len(shp))
    return pl.pallas_call(
        body, name="xattn_fwd", grid=(S // tq,),
        in_specs=[pl.BlockSpec((tq, dc), lambda i: (i, col0)), pl.BlockSpec((tq, dc), lambda i: (i, col0 + 1)),
                  full((M, 2 * dc)), full((1, HEAD_DIM)), full((1, HEAD_DIM)), pl.BlockSpec(memory_space=pl.ANY)],
        out_specs=pl.BlockSpec((tq, dc), lambda i: (i, D // dc - 1)),
        out_shape=SDS((S, D), bf16), input_output_aliases={5: 0}, compiler_params=_params(),
    )(proj, proj, kv, qg_row, kg_row, y_prev)


def _xattn_bwd(proj, kv, qg_row, kg_row, dy, dproj_prev, col0, tq=512):
    S = proj.shape[0]
    D = dy.shape[1]
    dc = N_HEADS * HEAD_DIM
    M = kv.shape[0]

    def body(q_ref, z_ref, kv_ref, qg_ref, kg_ref, dy_ref, dpp_ref, dp_ref, dkn_ref, dv_ref, dqg_ref):
        @pl.when(pl.program_id(0) == 0)
        def _():
            dkn_ref[...] = jnp.zeros_like(dkn_ref)
            dv_ref[...] = jnp.zeros_like(dv_ref)
            dqg_ref[...] = jnp.zeros_like(dqg_ref)

        qg = qg_ref[...]
        for h in range(N_HEADS):
            sl, rq, qhat, qn, kn, vh, p, o = _xattn_head(q_ref, kv_ref, qg, kg_ref[...], h)
            s, gs = _silu_and_grad(z_ref[:, sl])
            dyh = dy_ref[:, sl]
            dp_ref[:, dc + h * HEAD_DIM:dc + (h + 1) * HEAD_DIM] = (dyh * o * gs).astype(bf16)
            dob = (dyh * s).astype(bf16)
            dpr = _dot(dob, vh, NT)
            dv_ref[:, sl] += _dot(p.astype(bf16), dob, TN)
            ds = (p * (dpr - jnp.sum(p * dpr, axis=-1, keepdims=True)) * (1.0 / math.sqrt(HEAD_DIM))).astype(bf16)
            dqn = _dot(ds, kn)
            dkn_ref[:, sl] += _dot(ds, qn, TN)
            dqg_ref[...] += jnp.sum(dqn * qhat, axis=0, keepdims=True)
            dqhat = dqn * qg
            dp_ref[:, sl] = (rq * (dqhat - qhat * jnp.mean(dqhat * qhat, axis=-1, keepdims=True))).astype(bf16)

    full = lambda shp: pl.BlockSpec(shp, lambda i: (0,) * len(shp))
    return pl.pallas_call(
        body, name="xattn_bwd", grid=(S // tq,),
        in_specs=[pl.BlockSpec((tq, dc), lambda i: (i, col0)), pl.BlockSpec((tq, dc), lambda i: (i, col0 + 1)),
                  full((M, 2 * dc)), full((1, HEAD_DIM)), full((1, HEAD_DIM)),
                  pl.BlockSpec((tq, dc), lambda i: (i, D // dc - 1)), pl.BlockSpec(memory_space=pl.ANY)],
        out_specs=[pl.BlockSpec((tq, 2 * dc), lambda i: (i, col0 // 2)), full((M, dc)), full((M, dc)),
                   full((1, HEAD_DIM))],
        out_shape=[SDS(dproj_prev.shape, bf16), SDS((M, dc), f32), SDS((M, dc), f32), SDS((1, HEAD_DIM), f32)],
        input_output_aliases={6: 0}, compiler_params=_params(),
    )(proj, proj, kv, qg_row, kg_row, dy, dproj_prev)


def _mem_bwd(mem, mg_row, kv, dkn, dv, kg_row, w_kv_g):
    M, D = mem.shape
    rb, n = w_kv_g.shape[1], w_kv_g.shape[2]
    dc = n // 2

    def body(m_ref, g_ref, kv_ref, dkn_ref, dv_ref, kg_ref, w_ref, dw_ref, dmg_ref, dkg_ref, dkv_ref):
        mv = m_ref[...]
        r = lax.rsqrt(jnp.mean(mv * mv, axis=-1, keepdims=True) + EPS)
        mhat = mv * r
        mh = (mhat * g_ref[...]).astype(bf16)
        kg = kg_ref[...]
        dkg = jnp.zeros((1, HEAD_DIM), f32)
        for h in range(N_HEADS):
            sl = slice(h * HEAD_DIM, (h + 1) * HEAD_DIM)
            kh = kv_ref[:, sl]
            rk = lax.rsqrt(jnp.mean(kh * kh, axis=-1, keepdims=True) + EPS)
            khat = kh * rk
            dkn_h = dkn_ref[:, sl]
            dkg = dkg + jnp.sum(dkn_h * khat, axis=0, keepdims=True)
            dkhat = dkn_h * kg
            dkv_ref[:, sl] = (rk * (dkhat - khat * jnp.mean(dkhat * khat, axis=-1, keepdims=True))).astype(bf16)
        dkv_ref[:, dc:] = dv_ref[...].astype(bf16)
        dkg_ref[...] = dkg
        dkv = dkv_ref[...]
        dw_ref[...] = _dot(mh, dkv, TN).astype(bf16).reshape(N_DEV, rb, n)
        dmh = _dot(dkv, w_ref[...].reshape(N_DEV * rb, n), NT)
        dmg_ref[...] = jnp.sum(dmh * mhat, axis=0, keepdims=True)

    full = lambda shp: pl.BlockSpec(shp, lambda i: (0,) * len(shp))
    wspec = full((N_DEV, rb, n))
    return pl.pallas_call(
        body, name="mem_bwd", grid=(1,),
        in_specs=[full((M, D)), full((1, D)), full((M, n)), full((M, dc)), full((M, dc)), full((1, HEAD_DIM)), wspec],
        out_specs=[wspec, full((1, D)), full((1, HEAD_DIM))],
        out_shape=[SDS((N_DEV, rb, n), bf16), SDS((1, D), f32), SDS((1, HEAD_DIM), f32)],
        scratch_shapes=[pltpu.VMEM((M, n), bf16)], compiler_params=_params(),
    )(mem, mg_row, kv, dkn, dv, kg_row, w_kv_g)


SMALL = ("norm_g", "sgu_ln_g", "sgu_ln_b", "sgu_w", "sgu_b", "mem_norm_g", "q_norm_g", "k_norm_g")


def _pack_small(parts):
    flat = jnp.concatenate([parts[n].reshape(-1) for n in SMALL])
    pad = (-flat.shape[0]) % (8 * 128)
    return jnp.pad(flat, (0, pad)).reshape(-1, 128)


def _unpack_small(packed, like):
    flat = packed.reshape(-1)
    out, off = {}, 0
    for n in SMALL:
        size = math.prod(like[n].shape)
        out[n] = flat[off:off + size].reshape(like[n].shape)
        off += size
    return out


def kernel(x, mem, norm_g, w_in, sgu_ln_g, sgu_ln_b, sgu_w, sgu_b, mem_norm_g, w_mem_kv, q_norm_g, k_norm_g, w_out, loss_target, m_norm_g, m_w_in, m_sgu_ln_g, m_sgu_ln_b, m_sgu_w, m_sgu_b, m_mem_norm_g, m_w_mem_kv, m_q_norm_g, m_k_norm_g, m_w_out, v_norm_g, v_w_in, v_sgu_ln_g, v_sgu_ln_b, v_sgu_w, v_sgu_b, v_mem_norm_g, v_w_mem_kv, v_q_norm_g, v_k_norm_g, v_w_out):
    L, D, wc = w_in.shape
    S = x.shape[1]
    da = D // 2
    xs = x.reshape(S, D)
    mems = mem.reshape(mem.shape[1], D)
    tgt = loss_target.reshape(S, D)
    row = lambda a, l: a[l].reshape(1, -1)
    tie = lambda a, tok: a + tok[0:1, 0:1]
    sb_col, xa_col = 3 * da // HEAD_DIM, (3 * da + D) // (D // 4)

    w_b = [(_cast_bf16("cast_w_in", w_in, l, 512), _cast_bf16("cast_w_kv", w_mem_kv, l, 256),
            _cast_bf16("cast_w_out", w_out, l, 256)) for l in range(L)]
    g_in = [None] * L
    g_rest = [None] * L
    g_in[0] = _gather2_start("gather_w_in0", w_b[0][0], w_b[0][0])

    acts = []
    xl = xs
    for l in range(L):
        g_row = row(norm_g, l)
        if l == 0:
            _, land = _split_wait(g_in[0], g_in[0]["token"])
            fwd = _gather2_forward("gather_w_in0_forward", land, land)
            (w_in_g,) = _split_wait(fwd, fwd["token"])
        else:
            (w_in_g,) = _exchange_wait(g_in[l], g_row)
        g_rest[l] = _exchange_start(f"gather_w_rest{l}", [w_b[l][1], w_b[l][2]], False, w_in_g)
        proj, h = _rms_proj(xl, tie(g_row, g_rest[l]["token"]), w_in_g)
        lg_row = row(sgu_ln_g, l)
        if l + 1 < L:
            g_in[l + 1] = _exchange_start(f"gather_w_in{l + 1}", [w_b[l + 1][0]], False, proj)
            lg_row = tie(lg_row, g_in[l + 1]["token"])
        y = _sgu_fwd(proj, lg_row, row(sgu_ln_b, l), sgu_w[l], sgu_b[l].T)
        y, o_b, car = _sb_fwd(proj, y, sb_col)
        w_kv_g, w_out_g = _exchange_wait(g_rest[l], o_b)
        kv = _mem_kv(mems, row(mem_norm_g, l), w_kv_g)
        y = _xattn_fwd(proj, kv, row(q_norm_g, l), row(k_norm_g, l), y, xa_col)
        x_next = _out_proj(xl, y, w_out_g)
        acts.append((xl, proj, h, y, o_b, car, kv, w_in_g, w_kv_g, w_out_g))
        xl = x_next

    dx, dxb, loss_part = _loss_and_grad(xl, tgt, 512)
    loss = lax.psum(loss_part[0, 0], ("x", "y", "c"))

    scatters = [None] * L
    small = {n: [None] * L for n in SMALL}
    for l in reversed(range(L)):
        xl, proj, h, y, o_b, car, kv, w_in_g, w_kv_g, w_out_g = acts[l]
        dy = _out_bwd_dy(dxb, w_out_g)
        g_out = _tn_grad("out_bwd_dw", y, dxb, D // N_DEV, 512, rows_major=True)
        s_out = _exchange_start(f"scatter_g_out{l}", [g_out], True, g_out)
        dproj, d_sw, d_sb, d_lg, d_lb = _sgu_bwd(proj, dy, tie(row(sgu_ln_g, l), s_out["token"]), row(sgu_ln_b, l),
                                                 sgu_w[l], sgu_b[l].T)
        dproj = _sb_bwd(proj, o_b, car, dy, dproj, sb_col)
        dproj, dkn, dv, d_qg = _xattn_bwd(proj, kv, row(q_norm_g, l), row(k_norm_g, l), dy, dproj, xa_col)
        g_kv, d_mg, d_kg = _mem_bwd(mems, row(mem_norm_g, l), kv, dkn, dv, row(k_norm_g, l), w_kv_g)
        g_in_l = _tn_grad("in_bwd_dw", h, dproj, 512, wc, rows_major=False)
        if l == 0:
            s_rest = _scatter2_pair_start("scatter_g_rest0_pair", [g_kv, g_in_l], g_in_l)
        else:
            s_rest = _exchange_start(f"scatter_g_rest{l}", [g_kv, g_in_l], True, g_in_l)
        dx, dxb, d_ng = _in_bwd_dx(dproj, w_in_g, xl, tie(row(norm_g, l), s_rest["token"]), dx)
        scatters[l] = (s_out, s_rest)
        for n, val in (("norm_g", d_ng), ("sgu_ln_g", d_lg), ("sgu_ln_b", d_lb), ("sgu_w", d_sw),
                       ("sgu_b", d_sb[:, :A_GROUPS].T), ("mem_norm_g", d_mg), ("q_norm_g", d_qg), ("k_norm_g", d_kg)):
            small[n][l] = val.reshape(-1)
    _, _, mine_kv, mine_in, theirs_kv, theirs_in = _split_wait(scatters[0][1], dx)
    pairs = [_pair_sum("pair_sum_w_kv", mine_kv, theirs_kv, 256), _pair_sum("pair_sum_w_in", mine_in, theirs_in, 256)]
    s_chip = _scatter2_chip_start("scatter_g_rest0_chip", pairs, pairs[1])

    weights = dict(norm_g=norm_g, sgu_ln_g=sgu_ln_g, sgu_ln_b=sgu_ln_b, sgu_w=sgu_w, sgu_b=sgu_b,
                   mem_norm_g=mem_norm_g, q_norm_g=q_norm_g, k_norm_g=k_norm_g)
    moms_m = dict(norm_g=m_norm_g, sgu_ln_g=m_sgu_ln_g, sgu_ln_b=m_sgu_ln_b, sgu_w=m_sgu_w, sgu_b=m_sgu_b,
                  mem_norm_g=m_mem_norm_g, q_norm_g=m_q_norm_g, k_norm_g=m_k_norm_g)
    moms_v = dict(norm_g=v_norm_g, sgu_ln_g=v_sgu_ln_g, sgu_ln_b=v_sgu_ln_b, sgu_w=v_sgu_w, sgu_b=v_sgu_b,
                  mem_norm_g=v_mem_norm_g, q_norm_g=v_q_norm_g, k_norm_g=v_k_norm_g)
    part = _pack_small({n: jnp.stack(small[n]).reshape(weights[n].shape) for n in SMALL})
    (r_small,) = _exchange("gather_small_grads", [tie(part, s_chip["token"])], scatter=False)
    sm = _small_sum_adam(r_small, _pack_small(weights), _pack_small(moms_m), _pack_small(moms_v))

    (r_out1,) = _exchange_wait(scatters[1][0], sm[0])
    r_kv1, r_in1 = _exchange_wait(scatters[1][1], r_out1)
    a_out = _sum_adam("adam_w_out", r_out1, w_out, m_w_out, v_w_out, 1, None, 128)
    a_kv = _sum_adam("adam_w_kv", r_kv1, w_mem_kv, m_w_mem_kv, v_w_mem_kv, 1, None, 256)
    a_in = _sum_adam("adam_w_in", r_in1, w_in, m_w_in, v_w_in, 1, None, 256)
    (r_out0,) = _exchange_wait(scatters[0][0], a_in[0])
    a_out = _sum_adam("adam_w_out", r_out0, w_out, m_w_out, v_w_out, 0, a_out, 128)
    _, _, r_kv0, r_in0 = _split_wait(s_chip, a_out[0])
    a_kv = _sum_adam("adam_w_kv", r_kv0, w_mem_kv, m_w_mem_kv, v_w_mem_kv, 0, a_kv, 256)
    a_in = _sum_adam("adam_w_in", r_in0, w_in, m_w_in, v_w_in, 0, a_in, 256)
    res = {"w_out": a_out, "w_mem_kv": a_kv, "w_in": a_in}
    sm = [_unpack_small(a, weights) for a in sm]
    for n in SMALL:
        res[n] = [a[n] for a in sm]

    order = ("norm_g", "w_in", "sgu_ln_g", "sgu_ln_b", "sgu_w", "sgu_b", "mem_norm_g", "w_mem_kv", "q_norm_g",
             "k_norm_g", "w_out")
    outs = [loss, dx.reshape(x.shape)]
    for k in range(4):
        outs += [res[n][k] for n in order]
    return tuple(outs)
```

```python
import functools
import math

import jax
import jax.numpy as jnp
from jax import lax
from jax.experimental import pallas as pl
from jax.experimental.pallas import tpu as pltpu

f32 = jnp.float32
bf16 = jnp.bfloat16
SDS = jax.ShapeDtypeStruct

N_DEV = 8
EPS = 1e-6
CHUNK = 128
A_GROUPS = 8
HEAD_DIM = 128
N_HEADS = 4
TQ = 256
TK = 128
ADAM_LR, ADAM_B1, ADAM_B2, ADAM_EPS, ADAM_WD, ADAM_STEP = 0.001, 0.9, 0.999, 1e-08, 0.01, 10
MIB = 1024 * 1024

NT = (((1,), (1,)), ((), ()))
TN = (((0,), (0,)), ((), ()))


def _params(vmem_mib=48):
    return pltpu.CompilerParams(vmem_limit_bytes=vmem_mib * MIB)


def _gelu_and_grad(x):
    e = lax.erf(x * (1.0 / math.sqrt(2.0)))
    cdf = 0.5 * (1.0 + e)
    pdf = jnp.exp(-0.5 * x * x) * (1.0 / math.sqrt(2.0 * math.pi))
    return x * cdf, cdf + x * pdf


def _gelu(x):
    return 0.5 * x * (1.0 + lax.erf(x * (1.0 / math.sqrt(2.0))))


def _silu_and_grad(z):
    sg = jax.nn.sigmoid(z)
    return z * sg, sg * (1.0 + z * (1.0 - sg))


def _dot(a, b, dims=None):
    if dims is None:
        return jnp.dot(a, b, preferred_element_type=f32)
    return lax.dot_general(a, b, dims, preferred_element_type=f32)


def _exchange_copies(src_refs, dst_refs, send_sems, recv_sems, loc_sems, scatter):
    x, y, c = lax.axis_index("x"), lax.axis_index("y"), lax.axis_index("c")
    me = 4 * x + 2 * y + c
    locals_, remotes = [], []
    for a, (s, d) in enumerate(zip(src_refs, dst_refs)):
        locals_.append(pltpu.make_async_copy(s.at[me] if scatter else s, d.at[me], loc_sems.at[a]))
        for m in range(1, N_DEV):
            px = 1 - x if m & 4 else x
            py = 1 - y if m & 2 else y
            pc = 1 - c if m & 1 else c
            peer = 4 * px + 2 * py + pc
            remotes.append(pltpu.make_async_remote_copy(
                src_ref=s.at[peer] if scatter else s, dst_ref=d.at[me],
                send_sem=send_sems.at[a * 7 + m - 1], recv_sem=recv_sems.at[a * 7 + m - 1],
                device_id=(px, py, pc), device_id_type=pl.DeviceIdType.MESH))
    return locals_, remotes


def _exchange_shapes(srcs, scatter):
    return [(s.shape if scatter else (N_DEV,) + s.shape) for s in srcs]


def _exchange(name, srcs, scatter):
    n = len(srcs)

    def body(*refs):
        locals_, remotes = _exchange_copies(refs[:n], refs[n:2 * n], *refs[2 * n:], scatter)
        for cp in locals_ + remotes:
            cp.start()
        for cp in remotes:
            cp.wait_recv()
        for cp in remotes:
            cp.wait_send()
        for cp in locals_:
            cp.wait()

    any_spec = pl.BlockSpec(memory_space=pl.ANY)
    outs = pl.pallas_call(
        body, name=name, out_shape=[SDS(shp, s.dtype) for shp, s in zip(_exchange_shapes(srcs, scatter), srcs)],
        in_specs=[any_spec] * n, out_specs=[any_spec] * n,
        scratch_shapes=[pltpu.SemaphoreType.DMA((n * 7,)), pltpu.SemaphoreType.DMA((n * 7,)),
                        pltpu.SemaphoreType.DMA((n,))],
    )(*srcs)
    return list(outs)


_HBM = pl.BlockSpec(memory_space=pltpu.HBM)
_SEM = pl.BlockSpec(memory_space=pltpu.SEMAPHORE)
_EFFECT = pltpu.SideEffectType.DATAFLOW_SIDE_EFFECTING


def _split_start(name, bufs, n_remote, n_local, build, after):
    nb = len(bufs)

    def body(*refs):
        token = refs[-1]
        locals_, remotes = build(refs[:nb], *refs[nb + 1:nb + 4])
        for cp in locals_ + remotes:
            cp.start()
        token[...] = jnp.zeros_like(token)

    hbm = lambda a: pltpu.with_memory_space_constraint(a, pltpu.HBM)
    outs = pl.pallas_call(
        body, name=name,
        out_shape=(pltpu.SemaphoreType.DMA((n_remote,)), pltpu.SemaphoreType.DMA((n_remote,)),
                   pltpu.SemaphoreType.DMA((max(n_local, 1),)),
                   *[pltpu.HBM(b.shape, b.dtype) for b in bufs], SDS((8, 128), f32)),
        in_specs=[_HBM] * nb + [pl.BlockSpec(memory_space=pl.ANY)],
        out_specs=(_SEM, _SEM, _SEM, *[_HBM] * nb, pl.BlockSpec(memory_space=pltpu.VMEM)),
        input_output_aliases={k: 3 + k for k in range(nb)},
        compiler_params=pltpu.CompilerParams(has_side_effects=_EFFECT),
    )(*[hbm(b) for b in bufs], after)
    return dict(name=name, build=build, sems=outs[:3], bufs=outs[3:3 + nb], token=outs[-1])


def _split_wait(handle, after):
    build, bufs = handle["build"], handle["bufs"]
    nb = len(bufs)

    def body(*refs):
        locals_, remotes = build(refs[:nb], *refs[nb:nb + 3])
        for cp in remotes:
            cp.wait_recv()
        for cp in remotes:
            cp.wait_send()
        for cp in locals_:
            cp.wait()

    outs = pl.pallas_call(
        body, name=handle["name"] + "_wait",
        out_shape=tuple(pltpu.HBM(b.shape, b.dtype) for b in bufs),
        in_specs=[_HBM] * nb + [_SEM] * 3 + [pl.BlockSpec(memory_space=pl.ANY)],
        out_specs=tuple([_HBM] * nb),
        input_output_aliases={k: k for k in range(nb)},
        compiler_params=pltpu.CompilerParams(has_side_effects=_EFFECT),
    )(*bufs, *handle["sems"], after)
    return list(outs)


def _exchange_start(name, srcs, scatter, after):
    n = len(srcs)
    lands = [lax.empty(shp, s.dtype) for shp, s in zip(_exchange_shapes(srcs, scatter), srcs)]
    build = lambda refs, send, recv, loc: _exchange_copies(refs[:n], refs[n:], send, recv, loc, scatter)
    return _split_start(name, list(srcs) + lands, 7 * n, n, build, after)


def _exchange_wait(handle, after):
    outs = _split_wait(handle, after)
    return outs[len(outs) // 2:]


def _remote(src, dst, send_sems, recv_sems, k, to):
    return pltpu.make_async_remote_copy(src_ref=src, dst_ref=dst, send_sem=send_sems.at[k], recv_sem=recv_sems.at[k],
                                        device_id=to, device_id_type=pl.DeviceIdType.MESH)


def _other_chips(x, y):
    return [(1 - x, y), (x, 1 - y), (1 - x, 1 - y)]


def _gather2_start(name, srcs, after):
    n = len(srcs)

    def build(refs, send, recv, loc):
        x, y, c = lax.axis_index("x"), lax.axis_index("y"), lax.axis_index("c")
        me = 4 * x + 2 * y + c
        locals_, remotes = [], []
        for a in range(n):
            s, d = refs[a], refs[n + a]
            locals_.append(pltpu.make_async_copy(s, d.at[me], loc.at[a]))
            remotes.append(_remote(s, d.at[me], send, recv, 4 * a, (x, y, 1 - c)))
            remotes += [_remote(s, d.at[me], send, recv, 4 * a + 1 + k, (px, py, c))
                        for k, (px, py) in enumerate(_other_chips(x, y))]
        return locals_, remotes

    return _split_start(name, list(srcs) + [lax.empty((N_DEV,) + s.shape, s.dtype) for s in srcs], 4 * n, n, build, after)


def _gather2_forward(name, lands, after):
    n = len(lands)

    def build(refs, send, recv, loc):
        x, y, c = lax.axis_index("x"), lax.axis_index("y"), lax.axis_index("c")
        slots = [4 * px + 2 * py + c for px, py in _other_chips(x, y)]
        return [], [_remote(d.at[sl], d.at[sl], send, recv, 3 * a + k, (x, y, 1 - c))
                    for a, d in enumerate(refs) for k, sl in enumerate(slots)]

    return _split_start(name, list(lands), 3 * n, 0, build, after)


def _scatter2_pair_start(name, srcs, after):
    n = len(srcs)

    def build(refs, send, recv, loc):
        x, y, c = lax.axis_index("x"), lax.axis_index("y"), lax.axis_index("c")
        locals_, remotes = [], []
        for a in range(n):
            s, mine, theirs = refs[a], refs[n + a], refs[2 * n + a]
            for q in range(4):
                locals_.append(pltpu.make_async_copy(s.at[2 * q + c], mine.at[q], loc.at[4 * a + q]))
                remotes.append(_remote(s.at[2 * q + 1 - c], theirs.at[q], send, recv, 4 * a + q, (x, y, 1 - c)))
        return locals_, remotes

    half = [lax.empty((4,) + s.shape[1:], s.dtype) for s in srcs]
    return _split_start(name, list(srcs) + half + [lax.empty(h.shape, h.dtype) for h in half], 4 * n, 4 * n, build, after)


def _scatter2_chip_start(name, pairs, after):
    n = len(pairs)

    def build(refs, send, recv, loc):
        x, y, c = lax.axis_index("x"), lax.axis_index("y"), lax.axis_index("c")
        my_q = 2 * x + y
        locals_, remotes = [], []
        for a in range(n):
            s, d = refs[a], refs[n + a]
            locals_.append(pltpu.make_async_copy(s.at[my_q], d.at[my_q], loc.at[a]))
            remotes += [_remote(s.at[2 * px + py], d.at[my_q], send, recv, 3 * a + k, (px, py, c))
                        for k, (px, py) in enumerate(_other_chips(x, y))]
        return locals_, remotes

    return _split_start(name, list(pairs) + [lax.empty(p.shape, p.dtype) for p in pairs], 3 * n, n, build, after)


def _pair_sum(name, mine, theirs):
    _, R, C = mine.shape
    tr = min(R, 1024)

    def body(a_ref, b_ref, o_ref):
        o_ref[...] = (a_ref[...].astype(f32) + b_ref[...].astype(f32)).astype(bf16)

    spec = pl.BlockSpec((None, tr, C), lambda q, i: (q, i, 0))
    return pl.pallas_call(
        body, name=name, grid=(4, R // tr), in_specs=[spec, spec], out_specs=spec,
        out_shape=SDS(mine.shape, bf16), compiler_params=_params(),
    )(mine, theirs)


def _scatter2_second_level(name, first, after):
    outs = _split_wait(first, after)
    n = len(outs) // 3
    pairs = [_pair_sum(f"pair_sum_{name}{a}", outs[n + a], outs[2 * n + a]) for a in range(n)]
    return _scatter2_chip_start(f"scatter_{name}_chip", pairs, pairs[-1])


def _scatter2_finish(second, after):
    outs = _split_wait(second, after)
    return outs[len(outs) // 2:]


def _cast_bf16(name, w, l, tr):
    _, R, C = w.shape

    def body(w_ref, o_ref):
        o_ref[...] = w_ref[...].astype(bf16)

    return pl.pallas_call(
        body, name=name, grid=(R // tr,),
        in_specs=[pl.BlockSpec((None, tr, C), lambda i: (l, i, 0))],
        out_specs=pl.BlockSpec((tr, C), lambda i: (i, 0)),
        out_shape=SDS((R, C), bf16), compiler_params=_params(),
    )(w)


def _adam_math(w, g, m, v):
    m2 = ADAM_B1 * m + (1.0 - ADAM_B1) * g
    v2 = ADAM_B2 * v + (1.0 - ADAM_B2) * (g * g)
    m_hat = m2 / (1.0 - ADAM_B1 ** ADAM_STEP)
    v_hat = v2 / (1.0 - ADAM_B2 ** ADAM_STEP)
    delta = -ADAM_LR * (m_hat / (jnp.sqrt(v_hat) + ADAM_EPS) + ADAM_WD * w)
    return delta, m2, v2


def _sum_adam(name, recv, w, m, v, l, prev, tr):
    L, R, C = w.shape
    slots = recv.shape[0]

    def body(r_ref, w_ref, m_ref, v_ref, *rest):
        g_ref, d_ref, m2_ref, v2_ref = rest[-4:]
        g = r_ref[0].astype(f32)
        for s in range(1, slots):
            g = g + r_ref[s].astype(f32)
        d, m2, v2 = _adam_math(w_ref[...], g, m_ref[...], v_ref[...])
        g_ref[...] = g
        d_ref[...] = d
        m2_ref[...] = m2
        v2_ref[...] = v2

    wspec = pl.BlockSpec((None, tr, C), lambda i: (l, i, 0))
    in_specs = [pl.BlockSpec((slots, tr, C), lambda i: (0, i, 0)), wspec, wspec, wspec]
    args = [recv, w, m, v]
    aliases = {}
    if prev is not None:
        in_specs += [pl.BlockSpec(memory_space=pl.ANY)] * 4
        args += list(prev)
        aliases = {4 + k: k for k in range(4)}
    return pl.pallas_call(
        body, name=name, grid=(R // tr,), in_specs=in_specs, out_specs=[wspec] * 4,
        out_shape=[SDS((L, R, C), f32)] * 4, input_output_aliases=aliases, compiler_params=_params(),
    )(*args)


def _small_sum_adam(recv, w, m, v):
    _, R, C = recv.shape

    def body(r_ref, w_ref, m_ref, v_ref, g_ref, d_ref, m2_ref, v2_ref):
        g = r_ref[0]
        for s in range(1, N_DEV):
            g = g + r_ref[s]
        d, m2, v2 = _adam_math(w_ref[...], g, m_ref[...], v_ref[...])
        g_ref[...] = g
        d_ref[...] = d
        m2_ref[...] = m2
        v2_ref[...] = v2

    return pl.pallas_call(
        body, name="small_sum_adam", out_shape=[SDS((R, C), f32)] * 4, compiler_params=_params(),
    )(recv, w, m, v)


def _loss_and_grad(xf, tgt, tm):
    S, D = xf.shape

    def body(x_ref, t_ref, dx_ref, dxb_ref, l_ref):
        i = pl.program_id(0)
        d = x_ref[...] - t_ref[...]
        dx = d * (1.0 / D)
        dx_ref[...] = dx
        dxb_ref[...] = dx.astype(bf16)
        e = d * d
        part = e[:, 0:128]
        for k in range(1, D // 128):
            part = part + e[:, k * 128:(k + 1) * 128]
        part = jnp.sum(part.reshape(tm // 8, 8, 128), axis=0)

        @pl.when(i == 0)
        def _():
            l_ref[...] = jnp.zeros_like(l_ref)

        l_ref[...] += part

        @pl.when(i == pl.num_programs(0) - 1)
        def _():
            tot = jnp.sum(l_ref[...], axis=1, keepdims=True)
            tot = jnp.sum(tot, axis=0, keepdims=True)
            l_ref[...] = jnp.broadcast_to(tot * (0.5 / D), l_ref.shape)

    row = pl.BlockSpec((tm, D), lambda i: (i, 0))
    return pl.pallas_call(
        body, name="loss_grad", grid=(S // tm,),
        in_specs=[row, row], out_specs=[row, row, pl.BlockSpec((8, 128), lambda i: (0, 0))],
        out_shape=[SDS((S, D), f32), SDS((S, D), bf16), SDS((8, 128), f32)], compiler_params=_params(),
    )(xf, tgt)


def _rms_proj(x, g_row, w_in_g, tm=512):
    S, D = x.shape
    wc = w_in_g.shape[2]
    tn = 256
    per = wc // tn
    n_out = N_DEV * wc

    def body(x_ref, g_ref, w_ref, proj_ref, h_ref):
        @pl.when(pl.program_id(1) == 0)
        def _():
            xv = x_ref[...]
            r = lax.rsqrt(jnp.mean(xv * xv, axis=-1, keepdims=True) + EPS)
            h_ref[...] = (xv * r * g_ref[...]).astype(bf16)

        proj_ref[...] = _dot(h_ref[...], w_ref[...])

    return pl.pallas_call(
        body, name="rms_proj", grid=(S // tm, n_out // tn),
        in_specs=[pl.BlockSpec((tm, D), lambda i, j: (i, 0)),
                  pl.BlockSpec((1, D), lambda i, j: (0, 0)),
                  pl.BlockSpec((None, D, tn), lambda i, j: (j // per, 0, j % per))],
        out_specs=[pl.BlockSpec((tm, tn), lambda i, j: (i, j)), pl.BlockSpec((tm, D), lambda i, j: (i, 0))],
        out_shape=[SDS((S, n_out), f32), SDS((S, D), bf16)], compiler_params=_params(),
    )(x, g_row, w_in_g)


def _out_proj(x, y, w_out_g, tm=512, tn=512):
    S, D = x.shape
    rb = w_out_g.shape[1]

    def body(x_ref, y_ref, w_ref, o_ref):
        w = w_ref[...].reshape(N_DEV * rb, tn)
        o_ref[...] = x_ref[...] + _dot(y_ref[...], w)

    return pl.pallas_call(
        body, name="out_proj", grid=(S // tm, D // tn),
        in_specs=[pl.BlockSpec((tm, tn), lambda i, j: (i, j)),
                  pl.BlockSpec((tm, D), lambda i, j: (i, 0)),
                  pl.BlockSpec((N_DEV, rb, tn), lambda i, j: (0, 0, j))],
        out_specs=pl.BlockSpec((tm, tn), lambda i, j: (i, j)),
        out_shape=SDS((S, D), f32), compiler_params=_params(),
    )(x, y, w_out_g)


def _out_bwd_dy(dxb, w_out_g, tm=512):
    S, D = dxb.shape
    rb = w_out_g.shape[1]

    def body(dx_ref, w_ref, o_ref):
        o_ref[...] = _dot(dx_ref[...], w_ref[...], NT)

    return pl.pallas_call(
        body, name="out_bwd_dy", grid=(S // tm, N_DEV),
        in_specs=[pl.BlockSpec((tm, D), lambda i, j: (i, 0)),
                  pl.BlockSpec((None, rb, D), lambda i, j: (j, 0, 0))],
        out_specs=pl.BlockSpec((tm, rb), lambda i, j: (i, j)),
        out_shape=SDS((S, D), f32), compiler_params=_params(),
    )(dxb, w_out_g)


def _tn_grad(name, a, b, tm, tn, rows_major, after):
    S, M = a.shape
    N = b.shape[1]
    if rows_major:
        out_shape = SDS((N_DEV, M // N_DEV, N), bf16)
        assert tm == M // N_DEV
        out_spec = pl.BlockSpec((None, tm, tn), lambda i, j: (i, 0, j))
    else:
        out_shape = SDS((N_DEV, M, N // N_DEV), bf16)
        assert tn == N // N_DEV
        out_spec = pl.BlockSpec((None, tm, tn), lambda i, j: (j, i, 0))

    def body(a_ref, b_ref, after_ref, o_ref):
        o_ref[...] = _dot(a_ref[...], b_ref[...], TN).astype(bf16)

    return pl.pallas_call(
        body, name=name, grid=(M // tm, N // tn),
        in_specs=[pl.BlockSpec((S, tm), lambda i, j: (0, i)), pl.BlockSpec((S, tn), lambda i, j: (0, j)),
                  pl.BlockSpec(memory_space=pl.ANY)],
        out_specs=out_spec, out_shape=out_shape, compiler_params=_params(),
    )(a, b, after)


def _in_bwd_dx(dproj, w_in_g, x, g_row, dx_next, tm=512):
    S, D = x.shape
    wc = w_in_g.shape[2]

    def body(dp_ref, w_ref, x_ref, g_ref, dxn_ref, dx_ref, dxb_ref, dg_ref, acc_ref):
        i, k = pl.program_id(0), pl.program_id(1)

        @pl.when(k == 0)
        def _():
            acc_ref[...] = jnp.zeros_like(acc_ref)

        acc_ref[...] += _dot(dp_ref[...], w_ref[...], NT)

        @pl.when(jnp.logical_and(i == 0, k == 0))
        def _():
            dg_ref[...] = jnp.zeros_like(dg_ref)

        @pl.when(k == N_DEV - 1)
        def _():
            dh = acc_ref[...]
            xv = x_ref[...]
            r = lax.rsqrt(jnp.mean(xv * xv, axis=-1, keepdims=True) + EPS)
            xhat = xv * r
            dxhat = dh * g_ref[...]
            dx = r * (dxhat - xhat * jnp.mean(dxhat * xhat, axis=-1, keepdims=True)) + dxn_ref[...]
            dx_ref[...] = dx
            dxb_ref[...] = dx.astype(bf16)
            dg_ref[...] += jnp.sum(dh * xhat, axis=0, keepdims=True)

    row = pl.BlockSpec((tm, D), lambda i, k: (i, 0))
    return pl.pallas_call(
        body, name="in_bwd_dx", grid=(S // tm, N_DEV),
        in_specs=[pl.BlockSpec((tm, wc), lambda i, k: (i, k)),
                  pl.BlockSpec((None, D, wc), lambda i, k: (k, 0, 0)),
                  row, pl.BlockSpec((1, D), lambda i, k: (0, 0)), row],
        out_specs=[row, row, pl.BlockSpec((1, D), lambda i, k: (0, 0))],
        out_shape=[SDS((S, D), f32), SDS((S, D), bf16), SDS((1, D), f32)],
        scratch_shapes=[pltpu.VMEM((tm, D), f32)], compiler_params=_params(56),
    )(dproj, w_in_g, x, g_row, dx_next)


def _sgu_fwd(proj, ln_g, ln_b, w_s, b_t):
    S = proj.shape[0]
    da = A_GROUPS * HEAD_DIM
    D = 2 * da

    def body(u_ref, v_ref, z_ref, lg_ref, lb_ref, w_ref, bt_ref, y_ref):
        u = _gelu(u_ref[...])
        v = _gelu(v_ref[...])
        z = z_ref[...]
        mu = jnp.mean(v, axis=-1, keepdims=True)
        xc = v - mu
        rs = lax.rsqrt(jnp.mean(xc * xc, axis=-1, keepdims=True) + EPS)
        vn = (xc * rs * lg_ref[...] + lb_ref[...]).astype(bf16)
        gate = u * (z * jax.nn.sigmoid(z))
        tri = lax.broadcasted_iota(jnp.int32, (CHUNK, CHUNK), 0) >= lax.broadcasted_iota(jnp.int32, (CHUNK, CHUNK), 1)
        for g in range(A_GROUPS):
            sl = slice(g * HEAD_DIM, (g + 1) * HEAD_DIM)
            wm = jnp.where(tri, w_ref[g], 0.0).astype(bf16)
            mixed = _dot(wm, vn[:, sl]) + bt_ref[:, g:g + 1]
            y_ref[:, sl] = (gate[:, sl] * mixed).astype(bf16)

    blk = lambda cb: pl.BlockSpec((CHUNK, da), lambda c: (c, cb))
    full = lambda shp: pl.BlockSpec(shp, lambda c: (0,) * len(shp))
    return pl.pallas_call(
        body, name="sgu_fwd", grid=(S // CHUNK,),
        in_specs=[blk(0), blk(1), blk(2), full((1, da)), full((1, da)),
                  full((A_GROUPS, CHUNK, CHUNK)), full((CHUNK, A_GROUPS))],
        out_specs=blk(0), out_shape=SDS((S, D), bf16), compiler_params=_params(),
    )(proj, proj, proj, ln_g, ln_b, w_s, b_t)


def _sgu_bwd(proj, dy, ln_g, ln_b, w_s, b_t):
    S = proj.shape[0]
    da = A_GROUPS * HEAD_DIM
    n_proj = proj.shape[1]

    def body(u_ref, v_ref, z_ref, dy_ref, lg_ref, lb_ref, w_ref, bt_ref,
             dp_ref, dw_ref, db_ref, dlg_ref, dlb_ref, dvn_ref):
        @pl.when(pl.program_id(0) == 0)
        def _():
            dw_ref[...] = jnp.zeros_like(dw_ref)
            db_ref[...] = jnp.zeros_like(db_ref)
            dlg_ref[...] = jnp.zeros_like(dlg_ref)
            dlb_ref[...] = jnp.zeros_like(dlb_ref)

        up, vp, z, dy = u_ref[...], v_ref[...], z_ref[...], dy_ref[...]
        u, gu = _gelu_and_grad(up)
        v, gv = _gelu_and_grad(vp)
        s, gs = _silu_and_grad(z)
        mu = jnp.mean(v, axis=-1, keepdims=True)
        xc = v - mu
        rs = lax.rsqrt(jnp.mean(xc * xc, axis=-1, keepdims=True) + EPS)
        vhat = xc * rs
        lg = lg_ref[...]
        vn = (vhat * lg + lb_ref[...]).astype(bf16)
        tri = lax.broadcasted_iota(jnp.int32, (CHUNK, CHUNK), 0) >= lax.broadcasted_iota(jnp.int32, (CHUNK, CHUNK), 1)
        lane = lax.broadcasted_iota(jnp.int32, (CHUNK, HEAD_DIM), 1)
        dys = dy * s
        db = jnp.zeros((CHUNK, HEAD_DIM), f32)
        for g in range(A_GROUPS):
            sl = slice(g * HEAD_DIM, (g + 1) * HEAD_DIM)
            wm = jnp.where(tri, w_ref[g], 0.0).astype(bf16)
            mixed = _dot(wm, vn[:, sl]) + bt_ref[:, g:g + 1]
            dmix = dys[:, sl] * u[:, sl]
            dp_ref[:, sl] = (dys[:, sl] * mixed * gu[:, sl]).astype(bf16)
            dp_ref[:, 2 * da + g * HEAD_DIM:2 * da + (g + 1) * HEAD_DIM] = (
                dy[:, sl] * u[:, sl] * mixed * gs[:, sl]).astype(bf16)
            dmb = dmix.astype(bf16)
            dw_ref[g] += jnp.where(tri, _dot(dmb, vn[:, sl], NT), 0.0)
            dvn_ref[:, sl] = _dot(wm, dmb, TN)
            db = db + jnp.where(lane == g, jnp.sum(dmix, axis=1, keepdims=True), 0.0)
        db_ref[...] += db
        dvn = dvn_ref[...]
        dlg_ref[...] += jnp.sum(dvn * vhat, axis=0, keepdims=True)
        dlb_ref[...] += jnp.sum(dvn, axis=0, keepdims=True)
        dvhat = dvn * lg
        dv = rs * (dvhat - jnp.mean(dvhat, axis=-1, keepdims=True)
                   - vhat * jnp.mean(dvhat * vhat, axis=-1, keepdims=True))
        dp_ref[:, da:2 * da] = (dv * gv).astype(bf16)

    blk = lambda cb: pl.BlockSpec((CHUNK, da), lambda c: (c, cb))
    full = lambda shp: pl.BlockSpec(shp, lambda c: (0,) * len(shp))
    return pl.pallas_call(
        body, name="sgu_bwd", grid=(S // CHUNK,),
        in_specs=[blk(0), blk(1), blk(2), blk(0), full((1, da)), full((1, da)),
                  full((A_GROUPS, CHUNK, CHUNK)), full((CHUNK, A_GROUPS))],
        out_specs=[pl.BlockSpec((CHUNK, 3 * da), lambda c: (c, 0)), full((A_GROUPS, CHUNK, CHUNK)),
                   full((CHUNK, HEAD_DIM)), full((1, da)), full((1, da))],
        out_shape=[SDS((S, n_proj), bf16), SDS((A_GROUPS, CHUNK, CHUNK), f32), SDS((CHUNK, HEAD_DIM), f32),
                   SDS((1, da), f32), SDS((1, da), f32)],
        scratch_shapes=[pltpu.VMEM((CHUNK, da), f32)], compiler_params=_params(),
    )(proj, proj, proj, dy, ln_g, ln_b, w_s, b_t)


def _sb_scores(q, kblk, kb, rows, cols, masked):
    z = _dot(q, kblk, NT) * (1.0 / math.sqrt(HEAD_DIM))
    t = jnp.log1p(jnp.exp(-jnp.abs(z)))
    log_1mb = -(jnp.maximum(z, 0.0) + t)
    log_beta = jnp.minimum(z, 0.0) - t
    if not masked:
        return None, log_beta, log_1mb
    causal = (cols + kb * TK) < rows
    return causal, log_beta, jnp.where(causal, log_1mb, 0.0)


def _sb_tiles(i):
    rows = i * TQ + lax.broadcasted_iota(jnp.int32, (TQ, TK), 0)
    cols = lax.broadcasted_iota(jnp.int32, (TQ, TK), 1)
    r_i = lax.broadcasted_iota(jnp.int32, (TK, TK), 0)
    c_i = lax.broadcasted_iota(jnp.int32, (TK, TK), 1)
    return rows, cols, (r_i > c_i).astype(bf16), (r_i < c_i).astype(bf16)


def _suffix_sum(t, tri):
    hi = t.astype(bf16)
    lo = (t - hi.astype(f32)).astype(bf16)
    return _dot(hi, tri) + _dot(lo, tri)


def _sb_fwd(proj, y_prev, col0, after):
    S = proj.shape[0]
    D = y_prev.shape[1]
    dh = N_HEADS * HEAD_DIM
    n_diag = TQ // TK

    def body(q_ref, k_ref, v_ref, z_ref, yp_ref, after_ref, y_ref, o_ref, car_ref, qb, kb_s, vb_s, c_ref):
        i = pl.program_id(0)

        @pl.when(i == 0)
        def _():
            kb_s[...] = k_ref[...].astype(bf16)
            vb_s[...] = v_ref[...].astype(bf16)

        qb[...] = q_ref[...].astype(bf16)
        o_ref[...] = jnp.zeros_like(o_ref)
        c_ref[...] = jnp.zeros_like(c_ref)
        car_ref[...] = jnp.zeros_like(car_ref)
        nkb = (i + 1) * n_diag
        rows, cols, upper, _ = _sb_tiles(i)

        def make_step(masked):
            def step(jj, carry):
                kb = nkb - 1 - jj
                off = pl.multiple_of(kb * TK, TK)
                hs = range(N_HEADS)
                sls = [slice(h * HEAD_DIM, (h + 1) * HEAD_DIM) for h in hs]
                sc = [_sb_scores(qb[:, sls[h]], kb_s[pl.ds(off, TK), sls[h]], kb, rows, cols, masked) for h in hs]
                suf = [_suffix_sum(sc[h][2], upper) for h in hs]
                cs = [c_ref[h] for h in hs]
                es = [jnp.exp(sc[h][1] + suf[h] + cs[h]) for h in hs]
                if masked:
                    es = [jnp.where(sc[h][0], es[h], 0.0) for h in hs]
                pv = [_dot(es[h].astype(bf16), vb_s[pl.ds(off, TK), sls[h]]) for h in hs]
                for h in hs:
                    o_ref[:, sls[h]] += pv[h]
                    car_ref[h] = jnp.where(cols == kb, cs[h], car_ref[h])
                    c_ref[h] = cs[h] + jnp.sum(sc[h][2], axis=1, keepdims=True)
                return carry
            return step

        lax.fori_loop(0, n_diag, make_step(True), 0)
        lax.fori_loop(n_diag, nkb, make_step(False), 0)
        z = z_ref[...]
        y_ref[...] = (o_ref[...] * (z * jax.nn.sigmoid(z))).astype(bf16)

    cb = col0 * HEAD_DIM // dh
    qspec = lambda k: pl.BlockSpec((TQ, dh), lambda i: (i, cb + k))
    kspec = lambda k: pl.BlockSpec((S, dh), lambda i: (0, cb + k))
    return pl.pallas_call(
        body, name="sb_fwd", grid=(S // TQ,),
        in_specs=[qspec(0), kspec(1), kspec(2), qspec(3), pl.BlockSpec(memory_space=pl.ANY),
                  pl.BlockSpec(memory_space=pl.ANY)],
        out_specs=[pl.BlockSpec((TQ, dh), lambda i: (i, A_GROUPS * HEAD_DIM // dh)),
                   pl.BlockSpec((TQ, dh), lambda i: (i, 0)),
                   pl.BlockSpec((N_HEADS, TQ, TK), lambda i: (0, i, 0))],
        out_shape=[SDS((S, D), bf16), SDS((S, dh), f32), SDS((N_HEADS, S, TK), f32)],
        input_output_aliases={4: 0},
        scratch_shapes=[pltpu.VMEM((TQ, dh), bf16), pltpu.VMEM((S, dh), bf16), pltpu.VMEM((S, dh), bf16),
                        pltpu.VMEM((N_HEADS, TQ, TK), f32)],
        compiler_params=_params(),
    )(proj, proj, proj, proj, y_prev, after)


def _sb_bwd(proj, o, car, dy, dproj_prev, col0, after):
    S = proj.shape[0]
    n_i = S // TQ
    dh = N_HEADS * HEAD_DIM
    n_diag = TQ // TK
    cb = col0 * HEAD_DIM // dh
    scale = 1.0 / math.sqrt(HEAD_DIM)

    def body(q_ref, k_ref, v_ref, z_ref, o_ref, car_ref, dy_ref, dpp_ref, after_ref,
             dp_ref, qb, kb_s, vb_s, dob, p_ref, dq_acc, dk_acc, dv_acc, st_a, st_b, st_k, st_v):
        i = pl.program_id(0)

        def put(stage_ref, row0, nrows, k):
            pltpu.sync_copy(stage_ref, dp_ref.at[pl.ds(row0, nrows), pl.ds((cb + k) * dh, dh)])

        @pl.when(i == 0)
        def _():
            kb_s[...] = k_ref[...].astype(bf16)
            vb_s[...] = v_ref[...].astype(bf16)
            dk_acc[...] = jnp.zeros_like(dk_acc)
            dv_acc[...] = jnp.zeros_like(dv_acc)

        s, gs = _silu_and_grad(z_ref[...])
        dy = dy_ref[...]
        st_b[...] = (dy * o_ref[...] * gs).astype(bf16)
        dob[...] = (dy * s).astype(bf16)
        qb[...] = q_ref[...].astype(bf16)
        p_ref[...] = jnp.zeros_like(p_ref)
        dq_acc[...] = jnp.zeros_like(dq_acc)
        nkb = (i + 1) * n_diag
        rows, cols, upper, lower = _sb_tiles(i)

        def make_step(masked):
            def step(kb, carry):
                off = pl.multiple_of(kb * TK, TK)
                hs = range(N_HEADS)
                sls = [slice(h * HEAD_DIM, (h + 1) * HEAD_DIM) for h in hs]
                qs = [qb[:, sls[h]] for h in hs]
                ks = [kb_s[pl.ds(off, TK), sls[h]] for h in hs]
                dos = [dob[:, sls[h]] for h in hs]
                sc = [_sb_scores(qs[h], ks[h], kb, rows, cols, masked) for h in hs]
                da = [_dot(dos[h], vb_s[pl.ds(off, TK), sls[h]], NT) for h in hs]
                suf = [_suffix_sum(sc[h][2], upper) for h in hs]
                onehot = cols == kb
                cs = [jnp.sum(jnp.where(onehot, car_ref[h], 0.0), axis=1, keepdims=True) for h in hs]
                es = [jnp.exp(sc[h][1] + suf[h] + cs[h]) for h in hs]
                if masked:
                    es = [jnp.where(sc[h][0], es[h], 0.0) for h in hs]
                gs_ = [da[h] * es[h] for h in hs]
                ps = [p_ref[h] for h in hs]
                pre = [_suffix_sum(gs_[h], lower) + ps[h] for h in hs]
                dzs = []
                for h in hs:
                    beta = jnp.exp(sc[h][1])
                    dzz = gs_[h] * (1.0 - beta) - beta * pre[h]
                    if masked:
                        dzz = jnp.where(sc[h][0], dzz, 0.0)
                    dzs.append((dzz * scale).astype(bf16))
                dqs = [_dot(dzs[h], ks[h]) for h in hs]
                dks = [_dot(dzs[h], qs[h], TN) for h in hs]
                dvs = [_dot(es[h].astype(bf16), dos[h], TN) for h in hs]
                for h in hs:
                    dq_acc[:, sls[h]] += dqs[h]
                    dk_acc[pl.ds(off, TK), sls[h]] += dks[h]
                    dv_acc[pl.ds(off, TK), sls[h]] += dvs[h]
                    p_ref[h] = ps[h] + jnp.sum(gs_[h], axis=1, keepdims=True)
                return carry
            return step

        lax.fori_loop(0, nkb - n_diag, make_step(False), 0)
        lax.fori_loop(nkb - n_diag, nkb, make_step(True), 0)
        st_a[...] = dq_acc[...].astype(bf16)
        row0 = pl.multiple_of(i * TQ, TQ)
        put(st_a, row0, TQ, 0)
        put(st_b, row0, TQ, 3)

        @pl.when(i == n_i - 1)
        def _():
            st_k[...] = dk_acc[...].astype(bf16)
            st_v[...] = dv_acc[...].astype(bf16)
            put(st_k, 0, S, 1)
            put(st_v, 0, S, 2)

    qspec = lambda k: pl.BlockSpec((TQ, dh), lambda i: (i, cb + k))
    kspec = lambda k: pl.BlockSpec((S, dh), lambda i: (0, cb + k))
    return pl.pallas_call(
        body, name="sb_bwd", grid=(n_i,),
        in_specs=[qspec(0), kspec(1), kspec(2), qspec(3),
                  pl.BlockSpec((TQ, dh), lambda i: (i, 0)),
                  pl.BlockSpec((N_HEADS, TQ, TK), lambda i: (0, i, 0)),
                  pl.BlockSpec((TQ, dh), lambda i: (i, A_GROUPS * HEAD_DIM // dh)),
                  pl.BlockSpec(memory_space=pl.ANY), pl.BlockSpec(memory_space=pl.ANY)],
        out_specs=pl.BlockSpec(memory_space=pl.ANY),
        out_shape=SDS(dproj_prev.shape, bf16),
        input_output_aliases={7: 0},
        scratch_shapes=[pltpu.VMEM((TQ, dh), bf16), pltpu.VMEM((S, dh), bf16), pltpu.VMEM((S, dh), bf16),
                        pltpu.VMEM((TQ, dh), bf16), pltpu.VMEM((N_HEADS, TQ, TK), f32), pltpu.VMEM((TQ, dh), f32),
                        pltpu.VMEM((S, dh), f32), pltpu.VMEM((S, dh), f32),
                        pltpu.VMEM((TQ, dh), bf16), pltpu.VMEM((TQ, dh), bf16),
                        pltpu.VMEM((S, dh), bf16), pltpu.VMEM((S, dh), bf16)],
        compiler_params=_params(56),
    )(proj, proj, proj, proj, o, car, dy, dproj_prev, after)


def _mem_kv(mem, mg_row, w_kv_g):
    M, D = mem.shape
    rb, n = w_kv_g.shape[1], w_kv_g.shape[2]

    def body(m_ref, g_ref, w_ref, kv_ref):
        mv = m_ref[...]
        r = lax.rsqrt(jnp.mean(mv * mv, axis=-1, keepdims=True) + EPS)
        mh = (mv * r * g_ref[...]).astype(bf16)
        kv_ref[...] = _dot(mh, w_ref[...].reshape(N_DEV * rb, n))

    return pl.pallas_call(
        body, name="mem_kv", grid=(1,),
        in_specs=[pl.BlockSpec((M, D), lambda i: (0, 0)), pl.BlockSpec((1, D), lambda i: (0, 0)),
                  pl.BlockSpec((N_DEV, rb, n), lambda i: (0, 0, 0))],
        out_specs=pl.BlockSpec((M, n), lambda i: (0, 0)),
        out_shape=SDS((M, n), f32), compiler_params=_params(),
    )(mem, mg_row, w_kv_g)


def _xattn_head(q_ref, kv_ref, qg, kg, h):
    dc = N_HEADS * HEAD_DIM
    sl = slice(h * HEAD_DIM, (h + 1) * HEAD_DIM)
    qh = q_ref[:, sl]
    rq = lax.rsqrt(jnp.mean(qh * qh, axis=-1, keepdims=True) + EPS)
    qhat = qh * rq
    qn = (qhat * qg).astype(bf16)
    kh = kv_ref[:, sl]
    rk = lax.rsqrt(jnp.mean(kh * kh, axis=-1, keepdims=True) + EPS)
    kn = (kh * rk * kg).astype(bf16)
    vh = kv_ref[:, dc + h * HEAD_DIM:dc + (h + 1) * HEAD_DIM].astype(bf16)
    s = _dot(qn, kn, NT) * (1.0 / math.sqrt(HEAD_DIM))
    e = jnp.exp(s - jnp.max(s, axis=-1, keepdims=True))
    p = e / jnp.sum(e, axis=-1, keepdims=True)
    o = _dot(p.astype(bf16), vh)
    return sl, rq, qhat, qn, kn, vh, p, o


def _xattn_fwd(proj, kv, qg_row, kg_row, y_prev, col0, tq=512):
    S = proj.shape[0]
    D = y_prev.shape[1]
    dc = N_HEADS * HEAD_DIM
    M = kv.shape[0]

    def body(q_ref, z_ref, kv_ref, qg_ref, kg_ref, yp_ref, y_ref):
        for h in range(N_HEADS):
            sl, _, _, _, _, _, _, o = _xattn_head(q_ref, kv_ref, qg_ref[...], kg_ref[...], h)
            z = z_ref[:, sl]
            y_ref[:, sl] = (o * (z * jax.nn.sigmoid(z))).astype(bf16)

    full = lambda shp: pl.BlockSpec(shp, lambda i: (0,) * len(shp))
    return pl.pallas_call(
        body, name="xattn_fwd", grid=(S // tq,),
        in_specs=[pl.BlockSpec((tq, dc), lambda i: (i, col0)), pl.BlockSpec((tq, dc), lambda i: (i, col0 + 1)),
                  full((M, 2 * dc)), full((1, HEAD_DIM)), full((1, HEAD_DIM)), pl.BlockSpec(memory_space=pl.ANY)],
        out_specs=pl.BlockSpec((tq, dc), lambda i: (i, D // dc - 1)),
        out_shape=SDS((S, D), bf16), input_output_aliases={5: 0}, compiler_params=_params(),
    )(proj, proj, kv, qg_row, kg_row, y_prev)


def _xattn_bwd(proj, kv, qg_row, kg_row, dy, dproj_prev, col0, tq=512):
    S = proj.shape[0]
    D = dy.shape[1]
    dc = N_HEADS * HEAD_DIM
    M = kv.shape[0]

    def body(q_ref, z_ref, kv_ref, qg_ref, kg_ref, dy_ref, dpp_ref, dp_ref, dkn_ref, dv_ref, dqg_ref):
        @pl.when(pl.program_id(0) == 0)
        def _():
            dkn_ref[...] = jnp.zeros_like(dkn_ref)
            dv_ref[...] = jnp.zeros_like(dv_ref)
            dqg_ref[...] = jnp.zeros_like(dqg_ref)

        qg = qg_ref[...]
        for h in range(N_HEADS):
            sl, rq, qhat, qn, kn, vh, p, o = _xattn_head(q_ref, kv_ref, qg, kg_ref[...], h)
            s, gs = _silu_and_grad(z_ref[:, sl])
            dyh = dy_ref[:, sl]
            dp_ref[:, dc + h * HEAD_DIM:dc + (h + 1) * HEAD_DIM] = (dyh * o * gs).astype(bf16)
            dob = (dyh * s).astype(bf16)
            dpr = _dot(dob, vh, NT)
            dv_ref[:, sl] += _dot(p.astype(bf16), dob, TN)
            ds = (p * (dpr - jnp.sum(p * dpr, axis=-1, keepdims=True)) * (1.0 / math.sqrt(HEAD_DIM))).astype(bf16)
            dqn = _dot(ds, kn)
            dkn_ref[:, sl] += _dot(ds, qn, TN)
            dqg_ref[...] += jnp.sum(dqn * qhat, axis=0, keepdims=True)
            dqhat = dqn * qg
            dp_ref[:, sl] = (rq * (dqhat - qhat * jnp.mean(dqhat * qhat, axis=-1, keepdims=True))).astype(bf16)

    full = lambda shp: pl.BlockSpec(shp, lambda i: (0,) * len(shp))
    return pl.pallas_call(
        body, name="xattn_bwd", grid=(S // tq,),
        in_specs=[pl.BlockSpec((tq, dc), lambda i: (i, col0)), pl.BlockSpec((tq, dc), lambda i: (i, col0 + 1)),
                  full((M, 2 * dc)), full((1, HEAD_DIM)), full((1, HEAD_DIM)),
                  pl.BlockSpec((tq, dc), lambda i: (i, D // dc - 1)), pl.BlockSpec(memory_space=pl.ANY)],
        out_specs=[pl.BlockSpec((tq, 2 * dc), lambda i: (i, col0 // 2)), full((M, dc)), full((M, dc)),
                   full((1, HEAD_DIM))],
        out_shape=[SDS(dproj_prev.shape, bf16), SDS((M, dc), f32), SDS((M, dc), f32), SDS((1, HEAD_DIM), f32)],
        input_output_aliases={6: 0}, compiler_params=_params(),
    )(proj, proj, kv, qg_row, kg_row, dy, dproj_prev)


def _mem_bwd(mem, mg_row, kv, dkn, dv, kg_row, w_kv_g):
    M, D = mem.shape
    rb, n = w_kv_g.shape[1], w_kv_g.shape[2]
    dc = n // 2

    def body(m_ref, g_ref, kv_ref, dkn_ref, dv_ref, kg_ref, w_ref, dw_ref, dmg_ref, dkg_ref, dkv_ref):
        mv = m_ref[...]
        r = lax.rsqrt(jnp.mean(mv * mv, axis=-1, keepdims=True) + EPS)
        mhat = mv * r
        mh = (mhat * g_ref[...]).astype(bf16)
        kg = kg_ref[...]
        dkg = jnp.zeros((1, HEAD_DIM), f32)
        for h in range(N_HEADS):
            sl = slice(h * HEAD_DIM, (h + 1) * HEAD_DIM)
            kh = kv_ref[:, sl]
            rk = lax.rsqrt(jnp.mean(kh * kh, axis=-1, keepdims=True) + EPS)
            khat = kh * rk
            dkn_h = dkn_ref[:, sl]
            dkg = dkg + jnp.sum(dkn_h * khat, axis=0, keepdims=True)
            dkhat = dkn_h * kg
            dkv_ref[:, sl] = (rk * (dkhat - khat * jnp.mean(dkhat * khat, axis=-1, keepdims=True))).astype(bf16)
        dkv_ref[:, dc:] = dv_ref[...].astype(bf16)
        dkg_ref[...] = dkg
        dkv = dkv_ref[...]
        dw_ref[...] = _dot(mh, dkv, TN).astype(bf16).reshape(N_DEV, rb, n)
        dmh = _dot(dkv, w_ref[...].reshape(N_DEV * rb, n), NT)
        dmg_ref[...] = jnp.sum(dmh * mhat, axis=0, keepdims=True)

    full = lambda shp: pl.BlockSpec(shp, lambda i: (0,) * len(shp))
    wspec = full((N_DEV, rb, n))
    return pl.pallas_call(
        body, name="mem_bwd", grid=(1,),
        in_specs=[full((M, D)), full((1, D)), full((M, n)), full((M, dc)), full((M, dc)), full((1, HEAD_DIM)), wspec],
        out_specs=[wspec, full((1, D)), full((1, HEAD_DIM))],
        out_shape=[SDS((N_DEV, rb, n), bf16), SDS((1, D), f32), SDS((1, HEAD_DIM), f32)],
        scratch_shapes=[pltpu.VMEM((M, n), bf16)], compiler_params=_params(),
    )(mem, mg_row, kv, dkn, dv, kg_row, w_kv_g)


SMALL = ("norm_g", "sgu_ln_g", "sgu_ln_b", "sgu_w", "sgu_b", "mem_norm_g", "q_norm_g", "k_norm_g")


def _pack_small(parts):
    flat = jnp.concatenate([parts[n].reshape(-1) for n in SMALL])
    pad = (-flat.shape[0]) % (8 * 128)
    return jnp.pad(flat, (0, pad)).reshape(-1, 128)


def _unpack_small(packed, like):
    flat = packed.reshape(-1)
    out, off = {}, 0
    for n in SMALL:
        size = math.prod(like[n].shape)
        out[n] = flat[off:off + size].reshape(like[n].shape)
        off += size
    return out


def kernel(x, mem, norm_g, w_in, sgu_ln_g, sgu_ln_b, sgu_w, sgu_b, mem_norm_g, w_mem_kv, q_norm_g, k_norm_g, w_out, loss_target, m_norm_g, m_w_in, m_sgu_ln_g, m_sgu_ln_b, m_sgu_w, m_sgu_b, m_mem_norm_g, m_w_mem_kv, m_q_norm_g, m_k_norm_g, m_w_out, v_norm_g, v_w_in, v_sgu_ln_g, v_sgu_ln_b, v_sgu_w, v_sgu_b, v_mem_norm_g, v_w_mem_kv, v_q_norm_g, v_k_norm_g, v_w_out):
    L, D, wc = w_in.shape
    S = x.shape[1]
    da = D // 2
    xs = x.reshape(S, D)
    mems = mem.reshape(mem.shape[1], D)
    tgt = loss_target.reshape(S, D)
    row = lambda a, l: a[l].reshape(1, -1)
    tie = lambda a, tok: a + tok[0:1, 0:1]
    sb_col, xa_col = 3 * da // HEAD_DIM, (3 * da + D) // (D // 4)

    w_b = [(_cast_bf16("cast_w_in", w_in, l, 512), _cast_bf16("cast_w_kv", w_mem_kv, l, 256),
            _cast_bf16("cast_w_out", w_out, l, 256)) for l in range(L)]
    (w_in_land,) = _split_wait(_gather2_start("gather_w_in0", [w_b[0][0]], w_b[0][0]), w_b[0][0])[1:]
    in_fwd = _gather2_forward("gather_w_in0_forward", [w_in_land], w_in_land)

    acts = []
    xl = xs
    for l in range(L):
        g_row = row(norm_g, l)
        (w_in_g,) = _split_wait(in_fwd, g_row if l else in_fwd["token"])
        rest = _gather2_start(f"gather_w_rest{l}", [w_b[l][1], w_b[l][2]], w_in_g)
        g_row = tie(g_row, rest["token"])
        if l + 1 < L:
            nxt = _gather2_start(f"gather_w_in{l + 1}", [w_b[l + 1][0]], w_in_g)
            g_row = tie(g_row, nxt["token"])
        proj, h = _rms_proj(xl, g_row, w_in_g)
        rest_fwd = _gather2_forward(f"gather_w_rest{l}_forward", _split_wait(rest, proj)[2:], proj)
        y = _sgu_fwd(proj, tie(row(sgu_ln_g, l), rest_fwd["token"]), row(sgu_ln_b, l), sgu_w[l], sgu_b[l].T)
        y, o_b, car = _sb_fwd(proj, y, sb_col, rest_fwd["token"])
        w_kv_g, w_out_g = _split_wait(rest_fwd, o_b)
        mg_row = row(mem_norm_g, l)
        if l + 1 < L:
            in_fwd = _gather2_forward(f"gather_w_in{l + 1}_forward", _split_wait(nxt, o_b)[1:], o_b)
            mg_row = tie(mg_row, in_fwd["token"])
        kv = _mem_kv(mems, mg_row, w_kv_g)
        y = _xattn_fwd(proj, kv, row(q_norm_g, l), row(k_norm_g, l), y, xa_col)
        x_next = _out_proj(xl, y, w_out_g)
        acts.append((xl, proj, h, y, o_b, car, kv, w_in_g, w_kv_g, w_out_g))
        xl = x_next

    dx, dxb, loss_part = _loss_and_grad(xl, tgt, 512)
    loss = lax.psum(loss_part[0, 0], ("x", "y", "c"))

    seconds = {}
    pending = None
    small = {n: [None] * L for n in SMALL}
    for l in reversed(range(L)):
        xl, proj, h, y, o_b, car, kv, w_in_g, w_kv_g, w_out_g = acts[l]
        dy = _out_bwd_dy(dxb, w_out_g)
        order = dy
        if pending is not None:
            seconds[pending[0]] = _scatter2_second_level(pending[0], pending[1], dy)
            order = seconds[pending[0]]["token"]
        g_out = _tn_grad("out_bwd_dw", y, dxb, D // N_DEV, 512, True, order)
        first = _scatter2_pair_start(f"scatter_g_out{l}_pair", [g_out], g_out)
        dproj, d_sw, d_sb, d_lg, d_lb = _sgu_bwd(proj, dy, tie(row(sgu_ln_g, l), first["token"]), row(sgu_ln_b, l),
                                                 sgu_w[l], sgu_b[l].T)
        seconds[f"g_out{l}"] = _scatter2_second_level(f"g_out{l}", first, dproj)
        dproj = _sb_bwd(proj, o_b, car, dy, dproj, sb_col, seconds[f"g_out{l}"]["token"])
        dproj, dkn, dv, d_qg = _xattn_bwd(proj, kv, row(q_norm_g, l), row(k_norm_g, l), dy, dproj, xa_col)
        g_kv, d_mg, d_kg = _mem_bwd(mems, row(mem_norm_g, l), kv, dkn, dv, row(k_norm_g, l), w_kv_g)
        g_in_l = _tn_grad("in_bwd_dw", h, dproj, 512, wc, False, d_kg)
        first = _scatter2_pair_start(f"scatter_g_rest{l}_pair", [g_kv, g_in_l], g_in_l)
        dx, dxb, d_ng = _in_bwd_dx(dproj, w_in_g, xl, tie(row(norm_g, l), first["token"]), dx)
        pending = (f"g_rest{l}", first)
        for n, val in (("norm_g", d_ng), ("sgu_ln_g", d_lg), ("sgu_ln_b", d_lb), ("sgu_w", d_sw),
                       ("sgu_b", d_sb[:, :A_GROUPS].T), ("mem_norm_g", d_mg), ("q_norm_g", d_qg), ("k_norm_g", d_kg)):
            small[n][l] = val.reshape(-1)
    seconds[pending[0]] = _scatter2_second_level(pending[0], pending[1], dx)

    weights = dict(norm_g=norm_g, sgu_ln_g=sgu_ln_g, sgu_ln_b=sgu_ln_b, sgu_w=sgu_w, sgu_b=sgu_b,
                   mem_norm_g=mem_norm_g, q_norm_g=q_norm_g, k_norm_g=k_norm_g)
    moms_m = dict(norm_g=m_norm_g, sgu_ln_g=m_sgu_ln_g, sgu_ln_b=m_sgu_ln_b, sgu_w=m_sgu_w, sgu_b=m_sgu_b,
                  mem_norm_g=m_mem_norm_g, q_norm_g=m_q_norm_g, k_norm_g=m_k_norm_g)
    moms_v = dict(norm_g=v_norm_g, sgu_ln_g=v_sgu_ln_g, sgu_ln_b=v_sgu_ln_b, sgu_w=v_sgu_w, sgu_b=v_sgu_b,
                  mem_norm_g=v_mem_norm_g, q_norm_g=v_q_norm_g, k_norm_g=v_k_norm_g)
    part = _pack_small({n: jnp.stack(small[n]).reshape(weights[n].shape) for n in SMALL})
    (r_small,) = _exchange("gather_small_grads", [tie(part, seconds[pending[0]]["token"])], scatter=False)
    sm = _small_sum_adam(r_small, _pack_small(weights), _pack_small(moms_m), _pack_small(moms_v))

    a_out = a_kv = a_in = None
    order = sm[0]
    for l in reversed(range(L)):
        (r_out,) = _scatter2_finish(seconds[f"g_out{l}"], order)
        a_out = _sum_adam("adam_w_out", r_out, w_out, m_w_out, v_w_out, l, a_out, 128)
        r_kv, r_in = _scatter2_finish(seconds[f"g_rest{l}"], a_out[0])
        a_kv = _sum_adam("adam_w_kv", r_kv, w_mem_kv, m_w_mem_kv, v_w_mem_kv, l, a_kv, 256)
        a_in = _sum_adam("adam_w_in", r_in, w_in, m_w_in, v_w_in, l, a_in, 256)
        order = a_in[0]
    res = {"w_out": a_out, "w_mem_kv": a_kv, "w_in": a_in}
    sm = [_unpack_small(a, weights) for a in sm]
    for n in SMALL:
        res[n] = [a[n] for a in sm]

    order = ("norm_g", "w_in", "sgu_ln_g", "sgu_ln_b", "sgu_w", "sgu_b", "mem_norm_g", "w_mem_kv", "q_norm_g",
             "k_norm_g", "w_out")
    outs = [loss, dx.reshape(x.shape)]
    for k in range(4):
        outs += [res[n][k] for n in order]
    return tuple(outs)
```

```python
import functools
import math

import jax
import jax.numpy as jnp
from jax import lax
from jax.experimental import pallas as pl
from jax.experimental.pallas import tpu as pltpu

f32 = jnp.float32
bf16 = jnp.bfloat16
SDS = jax.ShapeDtypeStruct

N_DEV = 8
EPS = 1e-6
CHUNK = 128
A_GROUPS = 8
HEAD_DIM = 128
N_HEADS = 4
TQ = 256
TK = 128
ADAM_LR, ADAM_B1, ADAM_B2, ADAM_EPS, ADAM_WD, ADAM_STEP = 0.001, 0.9, 0.999, 1e-08, 0.01, 10
MIB = 1024 * 1024

NT = (((1,), (1,)), ((), ()))
TN = (((0,), (0,)), ((), ()))


def _params(vmem_mib=48):
    return pltpu.CompilerParams(vmem_limit_bytes=vmem_mib * MIB)


def _gelu_and_grad(x):
    e = lax.erf(x * (1.0 / math.sqrt(2.0)))
    cdf = 0.5 * (1.0 + e)
    pdf = jnp.exp(-0.5 * x * x) * (1.0 / math.sqrt(2.0 * math.pi))
    return x * cdf, cdf + x * pdf


def _gelu(x):
    return 0.5 * x * (1.0 + lax.erf(x * (1.0 / math.sqrt(2.0))))


def _silu_and_grad(z):
    sg = jax.nn.sigmoid(z)
    return z * sg, sg * (1.0 + z * (1.0 - sg))


def _dot(a, b, dims=None):
    if dims is None:
        return jnp.dot(a, b, preferred_element_type=f32)
    return lax.dot_general(a, b, dims, preferred_element_type=f32)


def _exchange_copies(src_refs, dst_refs, send_sems, recv_sems, loc_sems, scatter):
    x, y, c = lax.axis_index("x"), lax.axis_index("y"), lax.axis_index("c")
    me = 4 * x + 2 * y + c
    locals_, remotes = [], []
    for a, (s, d) in enumerate(zip(src_refs, dst_refs)):
        locals_.append(pltpu.make_async_copy(s.at[me] if scatter else s, d.at[me], loc_sems.at[a]))
        for m in range(1, N_DEV):
            px = 1 - x if m & 4 else x
            py = 1 - y if m & 2 else y
            pc = 1 - c if m & 1 else c
            peer = 4 * px + 2 * py + pc
            remotes.append(pltpu.make_async_remote_copy(
                src_ref=s.at[peer] if scatter else s, dst_ref=d.at[me],
                send_sem=send_sems.at[a * 7 + m - 1], recv_sem=recv_sems.at[a * 7 + m - 1],
                device_id=(px, py, pc), device_id_type=pl.DeviceIdType.MESH))
    return locals_, remotes


def _exchange_shapes(srcs, scatter):
    return [(s.shape if scatter else (N_DEV,) + s.shape) for s in srcs]


def _exchange(name, srcs, scatter):
    n = len(srcs)

    def body(*refs):
        locals_, remotes = _exchange_copies(refs[:n], refs[n:2 * n], *refs[2 * n:], scatter)
        for cp in locals_ + remotes:
            cp.start()
        for cp in remotes:
            cp.wait_recv()
        for cp in remotes:
            cp.wait_send()
        for cp in locals_:
            cp.wait()

    any_spec = pl.BlockSpec(memory_space=pl.ANY)
    outs = pl.pallas_call(
        body, name=name, out_shape=[SDS(shp, s.dtype) for shp, s in zip(_exchange_shapes(srcs, scatter), srcs)],
        in_specs=[any_spec] * n, out_specs=[any_spec] * n,
        scratch_shapes=[pltpu.SemaphoreType.DMA((n * 7,)), pltpu.SemaphoreType.DMA((n * 7,)),
                        pltpu.SemaphoreType.DMA((n,))],
    )(*srcs)
    return list(outs)


_HBM = pl.BlockSpec(memory_space=pltpu.HBM)
_SEM = pl.BlockSpec(memory_space=pltpu.SEMAPHORE)
_EFFECT = pltpu.SideEffectType.DATAFLOW_SIDE_EFFECTING


def _split_start(name, bufs, n_remote, n_local, build, after):
    nb = len(bufs)

    def body(*refs):
        token = refs[-1]
        locals_, remotes = build(refs[:nb], *refs[nb + 1:nb + 4])
        for cp in locals_ + remotes:
            cp.start()
        token[...] = jnp.zeros_like(token)

    hbm = lambda a: pltpu.with_memory_space_constraint(a, pltpu.HBM)
    outs = pl.pallas_call(
        body, name=name,
        out_shape=(pltpu.SemaphoreType.DMA((n_remote,)), pltpu.SemaphoreType.DMA((n_remote,)),
                   pltpu.SemaphoreType.DMA((max(n_local, 1),)),
                   *[pltpu.HBM(b.shape, b.dtype) for b in bufs], SDS((8, 128), f32)),
        in_specs=[_HBM] * nb + [pl.BlockSpec(memory_space=pl.ANY)],
        out_specs=(_SEM, _SEM, _SEM, *[_HBM] * nb, pl.BlockSpec(memory_space=pltpu.VMEM)),
        input_output_aliases={k: 3 + k for k in range(nb)},
        compiler_params=pltpu.CompilerParams(has_side_effects=_EFFECT),
    )(*[hbm(b) for b in bufs], after)
    return dict(name=name, build=build, sems=outs[:3], bufs=outs[3:3 + nb], token=outs[-1])


def _split_wait(handle, after):
    build, bufs = handle["build"], handle["bufs"]
    nb = len(bufs)

    def body(*refs):
        locals_, remotes = build(refs[:nb], *refs[nb:nb + 3])
        for cp in remotes:
            cp.wait_recv()
        for cp in remotes:
            cp.wait_send()
        for cp in locals_:
            cp.wait()

    outs = pl.pallas_call(
        body, name=handle["name"] + "_wait",
        out_shape=tuple(pltpu.HBM(b.shape, b.dtype) for b in bufs),
        in_specs=[_HBM] * nb + [_SEM] * 3 + [pl.BlockSpec(memory_space=pl.ANY)],
        out_specs=tuple([_HBM] * nb),
        input_output_aliases={k: k for k in range(nb)},
        compiler_params=pltpu.CompilerParams(has_side_effects=_EFFECT),
    )(*bufs, *handle["sems"], after)
    return list(outs)


def _remote(src, dst, send_sems, recv_sems, k, to):
    return pltpu.make_async_remote_copy(src_ref=src, dst_ref=dst, send_sem=send_sems.at[k], recv_sem=recv_sems.at[k],
                                        device_id=to, device_id_type=pl.DeviceIdType.MESH)


def _other_chips(x, y):
    return [(1 - x, y), (x, 1 - y), (1 - x, 1 - y)]


def _gather2_start(name, lands, after):
    def build(refs, send, recv, loc):
        x, y, c = lax.axis_index("x"), lax.axis_index("y"), lax.axis_index("c")
        me = 4 * x + 2 * y + c
        remotes = []
        for a, d in enumerate(refs):
            remotes.append(_remote(d.at[me], d.at[me], send, recv, 4 * a, (x, y, 1 - c)))
            remotes += [_remote(d.at[me], d.at[me], send, recv, 4 * a + 1 + k, (px, py, c))
                        for k, (px, py) in enumerate(_other_chips(x, y))]
        return [], remotes

    return _split_start(name, list(lands), 4 * len(lands), 0, build, after)


def _gather2_forward(name, lands, after):
    n = len(lands)

    def build(refs, send, recv, loc):
        x, y, c = lax.axis_index("x"), lax.axis_index("y"), lax.axis_index("c")
        slots = [4 * px + 2 * py + c for px, py in _other_chips(x, y)]
        return [], [_remote(d.at[sl], d.at[sl], send, recv, 3 * a + k, (x, y, 1 - c))
                    for a, d in enumerate(refs) for k, sl in enumerate(slots)]

    return _split_start(name, list(lands), 3 * n, 0, build, after)


def _scatter2_pair_start(name, srcs, after):
    n = len(srcs)

    def build(refs, send, recv, loc):
        x, y, c = lax.axis_index("x"), lax.axis_index("y"), lax.axis_index("c")
        return [], [_remote(refs[a].at[2 * q + 1 - c], refs[n + a].at[q], send, recv, 4 * a + q, (x, y, 1 - c))
                    for a in range(n) for q in range(4)]

    lands = [lax.empty((4,) + s.shape[1:], s.dtype) for s in srcs]
    return _split_start(name, list(srcs) + lands, 4 * n, 0, build, after)


def _scatter2_chip_start(name, pairs, after):
    n = len(pairs)

    def build(refs, send, recv, loc):
        x, y, c = lax.axis_index("x"), lax.axis_index("y"), lax.axis_index("c")
        return [], [_remote(refs[a].at[2 * px + py], refs[n + a].at[2 * x + y], send, recv, 3 * a + k, (px, py, c))
                    for a in range(n) for k, (px, py) in enumerate(_other_chips(x, y))]

    return _split_start(name, list(pairs) + [lax.empty(p.shape, p.dtype) for p in pairs], 3 * n, 0, build, after)


def _pair_sum(name, src, theirs, ids):
    _, R, C = theirs.shape
    tr = min(R, 1024)

    def body(ids_ref, a_ref, b_ref, o_ref):
        o_ref[...] = (a_ref[...].astype(f32) + b_ref[...].astype(f32)).astype(bf16)

    spec = pl.BlockSpec((None, tr, C), lambda q, i, ids: (q, i, 0))
    return pl.pallas_call(
        body, name=name,
        grid_spec=pltpu.PrefetchScalarGridSpec(
            num_scalar_prefetch=1, grid=(4, R // tr),
            in_specs=[pl.BlockSpec((None, tr, C), lambda q, i, ids: (2 * q + ids[2], i, 0)), spec], out_specs=spec),
        out_shape=SDS(theirs.shape, bf16), compiler_params=_params(),
    )(ids, src, theirs)


def _scatter2_second_level(name, first, after, ids):
    outs = _split_wait(first, after)
    n = len(outs) // 2
    pairs = [_pair_sum(f"pair_sum_{name}{a}", outs[a], outs[n + a], ids) for a in range(n)]
    return _scatter2_chip_start(f"scatter_{name}_chip", pairs, pairs[-1])


def _scatter2_finish(second, after):
    outs = _split_wait(second, after)
    n = len(outs) // 2
    return [(outs[a], outs[n + a]) for a in range(n)]


def _cast_into_slot(name, w, l, tr, ids):
    _, R, C = w.shape

    def body(ids_ref, w_ref, o_ref):
        o_ref[...] = w_ref[...].astype(bf16)

    return pl.pallas_call(
        body, name=name,
        grid_spec=pltpu.PrefetchScalarGridSpec(
            num_scalar_prefetch=1, grid=(R // tr,),
            in_specs=[pl.BlockSpec((None, tr, C), lambda i, ids: (l, i, 0))],
            out_specs=pl.BlockSpec((None, tr, C), lambda i, ids: (ids[0], i, 0))),
        out_shape=SDS((N_DEV, R, C), bf16), compiler_params=_params(),
    )(ids, w)


def _adam_math(w, g, m, v):
    m2 = ADAM_B1 * m + (1.0 - ADAM_B1) * g
    v2 = ADAM_B2 * v + (1.0 - ADAM_B2) * (g * g)
    m_hat = m2 / (1.0 - ADAM_B1 ** ADAM_STEP)
    v_hat = v2 / (1.0 - ADAM_B2 ** ADAM_STEP)
    delta = -ADAM_LR * (m_hat / (jnp.sqrt(v_hat) + ADAM_EPS) + ADAM_WD * w)
    return delta, m2, v2


def _sum_adam(name, pair_recv, w, m, v, l, prev, tr, ids):
    own, recv = pair_recv
    L, R, C = w.shape
    slots = recv.shape[0]

    def body(ids_ref, r_ref, own_ref, w_ref, m_ref, v_ref, *rest):
        g_ref, d_ref, m2_ref, v2_ref = rest[-4:]
        terms = [jnp.where(ids_ref[1] == q, own_ref[...], r_ref[q]).astype(f32) for q in range(slots)]
        g = terms[0]
        for t in terms[1:]:
            g = g + t
        d, m2, v2 = _adam_math(w_ref[...], g, m_ref[...], v_ref[...])
        g_ref[...] = g
        d_ref[...] = d
        m2_ref[...] = m2
        v2_ref[...] = v2

    wspec = pl.BlockSpec((None, tr, C), lambda i, ids: (l, i, 0))
    in_specs = [pl.BlockSpec((slots, tr, C), lambda i, ids: (0, i, 0)),
                pl.BlockSpec((None, tr, C), lambda i, ids: (ids[1], i, 0)), wspec, wspec, wspec]
    args = [ids, recv, own, w, m, v]
    aliases = {}
    if prev is not None:
        in_specs += [pl.BlockSpec(memory_space=pl.ANY)] * 4
        args += list(prev)
        aliases = {6 + k: k for k in range(4)}
    return pl.pallas_call(
        body, name=name,
        grid_spec=pltpu.PrefetchScalarGridSpec(num_scalar_prefetch=1, grid=(R // tr,), in_specs=in_specs,
                                               out_specs=[wspec] * 4),
        out_shape=[SDS((L, R, C), f32)] * 4, input_output_aliases=aliases, compiler_params=_params(),
    )(*args)


def _small_sum_adam(recv, w, m, v):
    _, R, C = recv.shape

    def body(r_ref, w_ref, m_ref, v_ref, g_ref, d_ref, m2_ref, v2_ref):
        g = r_ref[0]
        for s in range(1, N_DEV):
            g = g + r_ref[s]
        d, m2, v2 = _adam_math(w_ref[...], g, m_ref[...], v_ref[...])
        g_ref[...] = g
        d_ref[...] = d
        m2_ref[...] = m2
        v2_ref[...] = v2

    return pl.pallas_call(
        body, name="small_sum_adam", out_shape=[SDS((R, C), f32)] * 4, compiler_params=_params(),
    )(recv, w, m, v)


def _loss_and_grad(xf, tgt, tm):
    S, D = xf.shape

    def body(x_ref, t_ref, dx_ref, dxb_ref, l_ref):
        i = pl.program_id(0)
        d = x_ref[...] - t_ref[...]
        dx = d * (1.0 / D)
        dx_ref[...] = dx
        dxb_ref[...] = dx.astype(bf16)
        e = d * d
        part = e[:, 0:128]
        for k in range(1, D // 128):
            part = part + e[:, k * 128:(k + 1) * 128]
        part = jnp.sum(part.reshape(tm // 8, 8, 128), axis=0)

        @pl.when(i == 0)
        def _():
            l_ref[...] = jnp.zeros_like(l_ref)

        l_ref[...] += part

        @pl.when(i == pl.num_programs(0) - 1)
        def _():
            tot = jnp.sum(l_ref[...], axis=1, keepdims=True)
            tot = jnp.sum(tot, axis=0, keepdims=True)
            l_ref[...] = jnp.broadcast_to(tot * (0.5 / D), l_ref.shape)

    row = pl.BlockSpec((tm, D), lambda i: (i, 0))
    return pl.pallas_call(
        body, name="loss_grad", grid=(S // tm,),
        in_specs=[row, row], out_specs=[row, row, pl.BlockSpec((8, 128), lambda i: (0, 0))],
        out_shape=[SDS((S, D), f32), SDS((S, D), bf16), SDS((8, 128), f32)], compiler_params=_params(),
    )(xf, tgt)


def _rms_proj(x, g_row, w_in_g, tm=512):
    S, D = x.shape
    wc = w_in_g.shape[2]
    tn = 256
    per = wc // tn
    n_out = N_DEV * wc

    def body(x_ref, g_ref, w_ref, proj_ref, h_ref):
        @pl.when(pl.program_id(1) == 0)
        def _():
            xv = x_ref[...]
            r = lax.rsqrt(jnp.mean(xv * xv, axis=-1, keepdims=True) + EPS)
            h_ref[...] = (xv * r * g_ref[...]).astype(bf16)

        proj_ref[...] = _dot(h_ref[...], w_ref[...])

    return pl.pallas_call(
        body, name="rms_proj", grid=(S // tm, n_out // tn),
        in_specs=[pl.BlockSpec((tm, D), lambda i, j: (i, 0)),
                  pl.BlockSpec((1, D), lambda i, j: (0, 0)),
                  pl.BlockSpec((None, D, tn), lambda i, j: (j // per, 0, j % per))],
        out_specs=[pl.BlockSpec((tm, tn), lambda i, j: (i, j)), pl.BlockSpec((tm, D), lambda i, j: (i, 0))],
        out_shape=[SDS((S, n_out), f32), SDS((S, D), bf16)], compiler_params=_params(),
    )(x, g_row, w_in_g)


def _out_proj(x, y, w_out_g, tm=512, tn=512):
    S, D = x.shape
    rb = w_out_g.shape[1]

    def body(x_ref, y_ref, w_ref, o_ref):
        w = w_ref[...].reshape(N_DEV * rb, tn)
        o_ref[...] = x_ref[...] + _dot(y_ref[...], w)

    return pl.pallas_call(
        body, name="out_proj", grid=(S // tm, D // tn),
        in_specs=[pl.BlockSpec((tm, tn), lambda i, j: (i, j)),
                  pl.BlockSpec((tm, D), lambda i, j: (i, 0)),
                  pl.BlockSpec((N_DEV, rb, tn), lambda i, j: (0, 0, j))],
        out_specs=pl.BlockSpec((tm, tn), lambda i, j: (i, j)),
        out_shape=SDS((S, D), f32), compiler_params=_params(),
    )(x, y, w_out_g)


def _out_bwd_dy(dxb, w_out_g, tm=512):
    S, D = dxb.shape
    rb = w_out_g.shape[1]

    def body(dx_ref, w_ref, o_ref):
        o_ref[...] = _dot(dx_ref[...], w_ref[...], NT)

    return pl.pallas_call(
        body, name="out_bwd_dy", grid=(S // tm, N_DEV),
        in_specs=[pl.BlockSpec((tm, D), lambda i, j: (i, 0)),
                  pl.BlockSpec((None, rb, D), lambda i, j: (j, 0, 0))],
        out_specs=pl.BlockSpec((tm, rb), lambda i, j: (i, j)),
        out_shape=SDS((S, D), f32), compiler_params=_params(),
    )(dxb, w_out_g)


def _tn_grad(name, a, b, tm, tn, rows_major, after):
    S, M = a.shape
    N = b.shape[1]
    if rows_major:
        out_shape = SDS((N_DEV, M // N_DEV, N), bf16)
        assert tm == M // N_DEV
        out_spec = pl.BlockSpec((None, tm, tn), lambda i, j: (i, 0, j))
    else:
        out_shape = SDS((N_DEV, M, N // N_DEV), bf16)
        assert tn == N // N_DEV
        out_spec = pl.BlockSpec((None, tm, tn), lambda i, j: (j, i, 0))

    def body(a_ref, b_ref, after_ref, o_ref):
        o_ref[...] = _dot(a_ref[...], b_ref[...], TN).astype(bf16)

    return pl.pallas_call(
        body, name=name, grid=(M // tm, N // tn),
        in_specs=[pl.BlockSpec((S, tm), lambda i, j: (0, i)), pl.BlockSpec((S, tn), lambda i, j: (0, j)),
                  pl.BlockSpec(memory_space=pl.ANY)],
        out_specs=out_spec, out_shape=out_shape, compiler_params=_params(),
    )(a, b, after)


def _in_bwd_dx(dproj, w_in_g, x, g_row, dx_next, tm=512):
    S, D = x.shape
    wc = w_in_g.shape[2]

    def body(dp_ref, w_ref, x_ref, g_ref, dxn_ref, dx_ref, dxb_ref, dg_ref, acc_ref):
        i, k = pl.program_id(0), pl.program_id(1)

        @pl.when(k == 0)
        def _():
            acc_ref[...] = jnp.zeros_like(acc_ref)

        acc_ref[...] += _dot(dp_ref[...], w_ref[...], NT)

        @pl.when(jnp.logical_and(i == 0, k == 0))
        def _():
            dg_ref[...] = jnp.zeros_like(dg_ref)

        @pl.when(k == N_DEV - 1)
        def _():
            dh = acc_ref[...]
            xv = x_ref[...]
            r = lax.rsqrt(jnp.mean(xv * xv, axis=-1, keepdims=True) + EPS)
            xhat = xv * r
            dxhat = dh * g_ref[...]
            dx = r * (dxhat - xhat * jnp.mean(dxhat * xhat, axis=-1, keepdims=True)) + dxn_ref[...]
            dx_ref[...] = dx
            dxb_ref[...] = dx.astype(bf16)
            dg_ref[...] += jnp.sum(dh * xhat, axis=0, keepdims=True)

    row = pl.BlockSpec((tm, D), lambda i, k: (i, 0))
    return pl.pallas_call(
        body, name="in_bwd_dx", grid=(S // tm, N_DEV),
        in_specs=[pl.BlockSpec((tm, wc), lambda i, k: (i, k)),
                  pl.BlockSpec((None, D, wc), lambda i, k: (k, 0, 0)),
                  row, pl.BlockSpec((1, D), lambda i, k: (0, 0)), row],
        out_specs=[row, row, pl.BlockSpec((1, D), lambda i, k: (0, 0))],
        out_shape=[SDS((S, D), f32), SDS((S, D), bf16), SDS((1, D), f32)],
        scratch_shapes=[pltpu.VMEM((tm, D), f32)], compiler_params=_params(56),
    )(dproj, w_in_g, x, g_row, dx_next)


def _sgu_fwd(proj, ln_g, ln_b, w_s, b_t):
    S = proj.shape[0]
    da = A_GROUPS * HEAD_DIM
    D = 2 * da

    def body(u_ref, v_ref, z_ref, lg_ref, lb_ref, w_ref, bt_ref, y_ref):
        u = _gelu(u_ref[...])
        v = _gelu(v_ref[...])
        z = z_ref[...]
        mu = jnp.mean(v, axis=-1, keepdims=True)
        xc = v - mu
        rs = lax.rsqrt(jnp.mean(xc * xc, axis=-1, keepdims=True) + EPS)
        vn = (xc * rs * lg_ref[...] + lb_ref[...]).astype(bf16)
        gate = u * (z * jax.nn.sigmoid(z))
        tri = lax.broadcasted_iota(jnp.int32, (CHUNK, CHUNK), 0) >= lax.broadcasted_iota(jnp.int32, (CHUNK, CHUNK), 1)
        for g in range(A_GROUPS):
            sl = slice(g * HEAD_DIM, (g + 1) * HEAD_DIM)
            wm = jnp.where(tri, w_ref[g], 0.0).astype(bf16)
            mixed = _dot(wm, vn[:, sl]) + bt_ref[:, g:g + 1]
            y_ref[:, sl] = (gate[:, sl] * mixed).astype(bf16)

    blk = lambda cb: pl.BlockSpec((CHUNK, da), lambda c: (c, cb))
    full = lambda shp: pl.BlockSpec(shp, lambda c: (0,) * len(shp))
    return pl.pallas_call(
        body, name="sgu_fwd", grid=(S // CHUNK,),
        in_specs=[blk(0), blk(1), blk(2), full((1, da)), full((1, da)),
                  full((A_GROUPS, CHUNK, CHUNK)), full((CHUNK, A_GROUPS))],
        out_specs=blk(0), out_shape=SDS((S, D), bf16), compiler_params=_params(),
    )(proj, proj, proj, ln_g, ln_b, w_s, b_t)


def _sgu_bwd(proj, dy, ln_g, ln_b, w_s, b_t):
    S = proj.shape[0]
    da = A_GROUPS * HEAD_DIM
    n_proj = proj.shape[1]

    def body(u_ref, v_ref, z_ref, dy_ref, lg_ref, lb_ref, w_ref, bt_ref,
             dp_ref, dw_ref, db_ref, dlg_ref, dlb_ref, dvn_ref):
        @pl.when(pl.program_id(0) == 0)
        def _():
            dw_ref[...] = jnp.zeros_like(dw_ref)
            db_ref[...] = jnp.zeros_like(db_ref)
            dlg_ref[...] = jnp.zeros_like(dlg_ref)
            dlb_ref[...] = jnp.zeros_like(dlb_ref)

        up, vp, z, dy = u_ref[...], v_ref[...], z_ref[...], dy_ref[...]
        u, gu = _gelu_and_grad(up)
        v, gv = _gelu_and_grad(vp)
        s, gs = _silu_and_grad(z)
        mu = jnp.mean(v, axis=-1, keepdims=True)
        xc = v - mu
        rs = lax.rsqrt(jnp.mean(xc * xc, axis=-1, keepdims=True) + EPS)
        vhat = xc * rs
        lg = lg_ref[...]
        vn = (vhat * lg + lb_ref[...]).astype(bf16)
        tri = lax.broadcasted_iota(jnp.int32, (CHUNK, CHUNK), 0) >= lax.broadcasted_iota(jnp.int32, (CHUNK, CHUNK), 1)
        lane = lax.broadcasted_iota(jnp.int32, (CHUNK, HEAD_DIM), 1)
        dys = dy * s
        db = jnp.zeros((CHUNK, HEAD_DIM), f32)
        for g in range(A_GROUPS):
            sl = slice(g * HEAD_DIM, (g + 1) * HEAD_DIM)
            wm = jnp.where(tri, w_ref[g], 0.0).astype(bf16)
            mixed = _dot(wm, vn[:, sl]) + bt_ref[:, g:g + 1]
            dmix = dys[:, sl] * u[:, sl]
            dp_ref[:, sl] = (dys[:, sl] * mixed * gu[:, sl]).astype(bf16)
            dp_ref[:, 2 * da + g * HEAD_DIM:2 * da + (g + 1) * HEAD_DIM] = (
                dy[:, sl] * u[:, sl] * mixed * gs[:, sl]).astype(bf16)
            dmb = dmix.astype(bf16)
            dw_ref[g] += jnp.where(tri, _dot(dmb, vn[:, sl], NT), 0.0)
            dvn_ref[:, sl] = _dot(wm, dmb, TN)
            db = db + jnp.where(lane == g, jnp.sum(dmix, axis=1, keepdims=True), 0.0)
        db_ref[...] += db
        dvn = dvn_ref[...]
        dlg_ref[...] += jnp.sum(dvn * vhat, axis=0, keepdims=True)
        dlb_ref[...] += jnp.sum(dvn, axis=0, keepdims=True)
        dvhat = dvn * lg
        dv = rs * (dvhat - jnp.mean(dvhat, axis=-1, keepdims=True)
                   - vhat * jnp.mean(dvhat * vhat, axis=-1, keepdims=True))
        dp_ref[:, da:2 * da] = (dv * gv).astype(bf16)

    blk = lambda cb: pl.BlockSpec((CHUNK, da), lambda c: (c, cb))
    full = lambda shp: pl.BlockSpec(shp, lambda c: (0,) * len(shp))
    return pl.pallas_call(
        body, name="sgu_bwd", grid=(S // CHUNK,),
        in_specs=[blk(0), blk(1), blk(2), blk(0), full((1, da)), full((1, da)),
                  full((A_GROUPS, CHUNK, CHUNK)), full((CHUNK, A_GROUPS))],
        out_specs=[pl.BlockSpec((CHUNK, 3 * da), lambda c: (c, 0)), full((A_GROUPS, CHUNK, CHUNK)),
                   full((CHUNK, HEAD_DIM)), full((1, da)), full((1, da))],
        out_shape=[SDS((S, n_proj), bf16), SDS((A_GROUPS, CHUNK, CHUNK), f32), SDS((CHUNK, HEAD_DIM), f32),
                   SDS((1, da), f32), SDS((1, da), f32)],
        scratch_shapes=[pltpu.VMEM((CHUNK, da), f32)], compiler_params=_params(),
    )(proj, proj, proj, dy, ln_g, ln_b, w_s, b_t)


def _sb_scores(q, kblk, kb, rows, cols, masked):
    z = _dot(q, kblk, NT) * (1.0 / math.sqrt(HEAD_DIM))
    t = jnp.log1p(jnp.exp(-jnp.abs(z)))
    log_1mb = -(jnp.maximum(z, 0.0) + t)
    log_beta = jnp.minimum(z, 0.0) - t
    if not masked:
        return None, log_beta, log_1mb
    causal = (cols + kb * TK) < rows
    return causal, log_beta, jnp.where(causal, log_1mb, 0.0)


def _sb_tiles(i):
    rows = i * TQ + lax.broadcasted_iota(jnp.int32, (TQ, TK), 0)
    cols = lax.broadcasted_iota(jnp.int32, (TQ, TK), 1)
    r_i = lax.broadcasted_iota(jnp.int32, (TK, TK), 0)
    c_i = lax.broadcasted_iota(jnp.int32, (TK, TK), 1)
    return rows, cols, (r_i > c_i).astype(bf16), (r_i < c_i).astype(bf16)


def _suffix_sum(t, tri):
    hi = t.astype(bf16)
    lo = (t - hi.astype(f32)).astype(bf16)
    return _dot(hi, tri) + _dot(lo, tri)


def _sb_fwd(proj, y_prev, col0, after):
    S = proj.shape[0]
    D = y_prev.shape[1]
    dh = N_HEADS * HEAD_DIM
    n_diag = TQ // TK

    def body(q_ref, k_ref, v_ref, z_ref, yp_ref, after_ref, y_ref, o_ref, car_ref, qb, kb_s, vb_s, c_ref):
        i = pl.program_id(0)

        @pl.when(i == 0)
        def _():
            kb_s[...] = k_ref[...].astype(bf16)
            vb_s[...] = v_ref[...].astype(bf16)

        qb[...] = q_ref[...].astype(bf16)
        o_ref[...] = jnp.zeros_like(o_ref)
        c_ref[...] = jnp.zeros_like(c_ref)
        car_ref[...] = jnp.zeros_like(car_ref)
        nkb = (i + 1) * n_diag
        rows, cols, upper, _ = _sb_tiles(i)

        def make_step(masked):
            def step(jj, carry):
                kb = nkb - 1 - jj
                off = pl.multiple_of(kb * TK, TK)
                hs = range(N_HEADS)
                sls = [slice(h * HEAD_DIM, (h + 1) * HEAD_DIM) for h in hs]
                sc = [_sb_scores(qb[:, sls[h]], kb_s[pl.ds(off, TK), sls[h]], kb, rows, cols, masked) for h in hs]
                suf = [_suffix_sum(sc[h][2], upper) for h in hs]
                cs = [c_ref[h] for h in hs]
                es = [jnp.exp(sc[h][1] + suf[h] + cs[h]) for h in hs]
                if masked:
                    es = [jnp.where(sc[h][0], es[h], 0.0) for h in hs]
                pv = [_dot(es[h].astype(bf16), vb_s[pl.ds(off, TK), sls[h]]) for h in hs]
                for h in hs:
                    o_ref[:, sls[h]] += pv[h]
                    car_ref[h] = jnp.where(cols == kb, cs[h], car_ref[h])
                    c_ref[h] = cs[h] + jnp.sum(sc[h][2], axis=1, keepdims=True)
                return carry
            return step

        lax.fori_loop(0, n_diag, make_step(True), 0)
        lax.fori_loop(n_diag, nkb, make_step(False), 0)
        z = z_ref[...]
        y_ref[...] = (o_ref[...] * (z * jax.nn.sigmoid(z))).astype(bf16)

    cb = col0 * HEAD_DIM // dh
    qspec = lambda k: pl.BlockSpec((TQ, dh), lambda i: (i, cb + k))
    kspec = lambda k: pl.BlockSpec((S, dh), lambda i: (0, cb + k))
    return pl.pallas_call(
        body, name="sb_fwd", grid=(S // TQ,),
        in_specs=[qspec(0), kspec(1), kspec(2), qspec(3), pl.BlockSpec(memory_space=pl.ANY),
                  pl.BlockSpec(memory_space=pl.ANY)],
        out_specs=[pl.BlockSpec((TQ, dh), lambda i: (i, A_GROUPS * HEAD_DIM // dh)),
                   pl.BlockSpec((TQ, dh), lambda i: (i, 0)),
                   pl.BlockSpec((N_HEADS, TQ, TK), lambda i: (0, i, 0))],
        out_shape=[SDS((S, D), bf16), SDS((S, dh), f32), SDS((N_HEADS, S, TK), f32)],
        input_output_aliases={4: 0},
        scratch_shapes=[pltpu.VMEM((TQ, dh), bf16), pltpu.VMEM((S, dh), bf16), pltpu.VMEM((S, dh), bf16),
                        pltpu.VMEM((N_HEADS, TQ, TK), f32)],
        compiler_params=_params(),
    )(proj, proj, proj, proj, y_prev, after)


def _sb_bwd(proj, o, car, dy, dproj_prev, col0, after):
    S = proj.shape[0]
    n_i = S // TQ
    dh = N_HEADS * HEAD_DIM
    n_diag = TQ // TK
    cb = col0 * HEAD_DIM // dh
    scale = 1.0 / math.sqrt(HEAD_DIM)

    def body(q_ref, k_ref, v_ref, z_ref, o_ref, car_ref, dy_ref, dpp_ref, after_ref,
             dp_ref, qb, kb_s, vb_s, dob, p_ref, dq_acc, dk_acc, dv_acc, st_a, st_b, st_k, st_v):
        i = pl.program_id(0)

        def put(stage_ref, row0, nrows, k):
            pltpu.sync_copy(stage_ref, dp_ref.at[pl.ds(row0, nrows), pl.ds((cb + k) * dh, dh)])

        @pl.when(i == 0)
        def _():
            kb_s[...] = k_ref[...].astype(bf16)
            vb_s[...] = v_ref[...].astype(bf16)
            dk_acc[...] = jnp.zeros_like(dk_acc)
            dv_acc[...] = jnp.zeros_like(dv_acc)

        s, gs = _silu_and_grad(z_ref[...])
        dy = dy_ref[...]
        st_b[...] = (dy * o_ref[...] * gs).astype(bf16)
        dob[...] = (dy * s).astype(bf16)
        qb[...] = q_ref[...].astype(bf16)
        p_ref[...] = jnp.zeros_like(p_ref)
        dq_acc[...] = jnp.zeros_like(dq_acc)
        nkb = (i + 1) * n_diag
        rows, cols, upper, lower = _sb_tiles(i)

        def make_step(masked):
            def step(kb, carry):
                off = pl.multiple_of(kb * TK, TK)
                hs = range(N_HEADS)
                sls = [slice(h * HEAD_DIM, (h + 1) * HEAD_DIM) for h in hs]
                qs = [qb[:, sls[h]] for h in hs]
                ks = [kb_s[pl.ds(off, TK), sls[h]] for h in hs]
                dos = [dob[:, sls[h]] for h in hs]
                sc = [_sb_scores(qs[h], ks[h], kb, rows, cols, masked) for h in hs]
                da = [_dot(dos[h], vb_s[pl.ds(off, TK), sls[h]], NT) for h in hs]
                suf = [_suffix_sum(sc[h][2], upper) for h in hs]
                onehot = cols == kb
                cs = [jnp.sum(jnp.where(onehot, car_ref[h], 0.0), axis=1, keepdims=True) for h in hs]
                es = [jnp.exp(sc[h][1] + suf[h] + cs[h]) for h in hs]
                if masked:
                    es = [jnp.where(sc[h][0], es[h], 0.0) for h in hs]
                gs_ = [da[h] * es[h] for h in hs]
                ps = [p_ref[h] for h in hs]
                pre = [_suffix_sum(gs_[h], lower) + ps[h] for h in hs]
                dzs = []
                for h in hs:
                    beta = jnp.exp(sc[h][1])
                    dzz = gs_[h] * (1.0 - beta) - beta * pre[h]
                    if masked:
                        dzz = jnp.where(sc[h][0], dzz, 0.0)
                    dzs.append((dzz * scale).astype(bf16))
                dqs = [_dot(dzs[h], ks[h]) for h in hs]
                dks = [_dot(dzs[h], qs[h], TN) for h in hs]
                dvs = [_dot(es[h].astype(bf16), dos[h], TN) for h in hs]
                for h in hs:
                    dq_acc[:, sls[h]] += dqs[h]
                    dk_acc[pl.ds(off, TK), sls[h]] += dks[h]
                    dv_acc[pl.ds(off, TK), sls[h]] += dvs[h]
                    p_ref[h] = ps[h] + jnp.sum(gs_[h], axis=1, keepdims=True)
                return carry
            return step

        lax.fori_loop(0, nkb - n_diag, make_step(False), 0)
        lax.fori_loop(nkb - n_diag, nkb, make_step(True), 0)
        st_a[...] = dq_acc[...].astype(bf16)
        row0 = pl.multiple_of(i * TQ, TQ)
        put(st_a, row0, TQ, 0)
        put(st_b, row0, TQ, 3)

        @pl.when(i == n_i - 1)
        def _():
            st_k[...] = dk_acc[...].astype(bf16)
            st_v[...] = dv_acc[...].astype(bf16)
            put(st_k, 0, S, 1)
            put(st_v, 0, S, 2)

    qspec = lambda k: pl.BlockSpec((TQ, dh), lambda i: (i, cb + k))
    kspec = lambda k: pl.BlockSpec((S, dh), lambda i: (0, cb + k))
    return pl.pallas_call(
        body, name="sb_bwd", grid=(n_i,),
        in_specs=[qspec(0), kspec(1), kspec(2), qspec(3),
                  pl.BlockSpec((TQ, dh), lambda i: (i, 0)),
                  pl.BlockSpec((N_HEADS, TQ, TK), lambda i: (0, i, 0)),
                  pl.BlockSpec((TQ, dh), lambda i: (i, A_GROUPS * HEAD_DIM // dh)),
                  pl.BlockSpec(memory_space=pl.ANY), pl.BlockSpec(memory_space=pl.ANY)],
        out_specs=pl.BlockSpec(memory_space=pl.ANY),
        out_shape=SDS(dproj_prev.shape, bf16),
        input_output_aliases={7: 0},
        scratch_shapes=[pltpu.VMEM((TQ, dh), bf16), pltpu.VMEM((S, dh), bf16), pltpu.VMEM((S, dh), bf16),
                        pltpu.VMEM((TQ, dh), bf16), pltpu.VMEM((N_HEADS, TQ, TK), f32), pltpu.VMEM((TQ, dh), f32),
                        pltpu.VMEM((S, dh), f32), pltpu.VMEM((S, dh), f32),
                        pltpu.VMEM((TQ, dh), bf16), pltpu.VMEM((TQ, dh), bf16),
                        pltpu.VMEM((S, dh), bf16), pltpu.VMEM((S, dh), bf16)],
        compiler_params=_params(56),
    )(proj, proj, proj, proj, o, car, dy, dproj_prev, after)


def _mem_kv(mem, mg_row, w_kv_g):
    M, D = mem.shape
    rb, n = w_kv_g.shape[1], w_kv_g.shape[2]

    def body(m_ref, g_ref, w_ref, kv_ref):
        mv = m_ref[...]
        r = lax.rsqrt(jnp.mean(mv * mv, axis=-1, keepdims=True) + EPS)
        mh = (mv * r * g_ref[...]).astype(bf16)
        kv_ref[...] = _dot(mh, w_ref[...].reshape(N_DEV * rb, n))

    return pl.pallas_call(
        body, name="mem_kv", grid=(1,),
        in_specs=[pl.BlockSpec((M, D), lambda i: (0, 0)), pl.BlockSpec((1, D), lambda i: (0, 0)),
                  pl.BlockSpec((N_DEV, rb, n), lambda i: (0, 0, 0))],
        out_specs=pl.BlockSpec((M, n), lambda i: (0, 0)),
        out_shape=SDS((M, n), f32), compiler_params=_params(),
    )(mem, mg_row, w_kv_g)


def _xattn_head(q_ref, kv_ref, qg, kg, h):
    dc = N_HEADS * HEAD_DIM
    sl = slice(h * HEAD_DIM, (h + 1) * HEAD_DIM)
    qh = q_ref[:, sl]
    rq = lax.rsqrt(jnp.mean(qh * qh, axis=-1, keepdims=True) + EPS)
    qhat = qh * rq
    qn = (qhat * qg).astype(bf16)
    kh = kv_ref[:, sl]
    rk = lax.rsqrt(jnp.mean(kh * kh, axis=-1, keepdims=True) + EPS)
    kn = (kh * rk * kg).astype(bf16)
    vh = kv_ref[:, dc + h * HEAD_DIM:dc + (h + 1) * HEAD_DIM].astype(bf16)
    s = _dot(qn, kn, NT) * (1.0 / math.sqrt(HEAD_DIM))
    e = jnp.exp(s - jnp.max(s, axis=-1, keepdims=True))
    p = e / jnp.sum(e, axis=-1, keepdims=True)
    o = _dot(p.astype(bf16), vh)
    return sl, rq, qhat, qn, kn, vh, p, o


def _xattn_fwd(proj, kv, qg_row, kg_row, y_prev, col0, tq=512):
    S = proj.shape[0]
    D = y_prev.shape[1]
    dc = N_HEADS * HEAD_DIM
    M = kv.shape[0]

    def body(q_ref, z_ref, kv_ref, qg_ref, kg_ref, yp_ref, y_ref):
        for h in range(N_HEADS):
            sl, _, _, _, _, _, _, o = _xattn_head(q_ref, kv_ref, qg_ref[...], kg_ref[...], h)
            z = z_ref[:, sl]
            y_ref[:, sl] = (o * (z * jax.nn.sigmoid(z))).astype(bf16)

    full = lambda shp: pl.BlockSpec(shp, lambda i: (0,) * len(shp))
    return pl.pallas_call(
        body, name="xattn_fwd", grid=(S // tq,),
        in_specs=[pl.BlockSpec((tq, dc), lambda i: (i, col0)), pl.BlockSpec((tq, dc), lambda i: (i, col0 + 1)),
                  full((M, 2 * dc)), full((1, HEAD_DIM)), full((1, HEAD_DIM)), pl.BlockSpec(memory_space=pl.ANY)],
        out_specs=pl.BlockSpec((tq, dc), lambda i: (i, D // dc - 1)),
        out_shape=SDS((S, D), bf16), input_output_aliases={5: 0}, compiler_params=_params(),
    )(proj, proj, kv, qg_row, kg_row, y_prev)


def _xattn_bwd(proj, kv, qg_row, kg_row, dy, dproj_prev, col0, tq=512):
    S = proj.shape[0]
    D = dy.shape[1]
    dc = N_HEADS * HEAD_DIM
    M = kv.shape[0]

    def body(q_ref, z_ref, kv_ref, qg_ref, kg_ref, dy_ref, dpp_ref, dp_ref, dkn_ref, dv_ref, dqg_ref):
        @pl.when(pl.program_id(0) == 0)
        def _():
            dkn_ref[...] = jnp.zeros_like(dkn_ref)
            dv_ref[...] = jnp.zeros_like(dv_ref)
            dqg_ref[...] = jnp.zeros_like(dqg_ref)

        qg = qg_ref[...]
        for h in range(N_HEADS):
            sl, rq, qhat, qn, kn, vh, p, o = _xattn_head(q_ref, kv_ref, qg, kg_ref[...], h)
            s, gs = _silu_and_grad(z_ref[:, sl])
            dyh = dy_ref[:, sl]
            dp_ref[:, dc + h * HEAD_DIM:dc + (h + 1) * HEAD_DIM] = (dyh * o * gs).astype(bf16)
            dob = (dyh * s).astype(bf16)
            dpr = _dot(dob, vh, NT)
            dv_ref[:, sl] += _dot(p.astype(bf16), dob, TN)
            ds = (p * (dpr - jnp.sum(p * dpr, axis=-1, keepdims=True)) * (1.0 / math.sqrt(HEAD_DIM))).astype(bf16)
            dqn = _dot(ds, kn)
            dkn_ref[:, sl] += _dot(ds, qn, TN)
            dqg_ref[...] += jnp.sum(dqn * qhat, axis=0, keepdims=True)
            dqhat = dqn * qg
            dp_ref[:, sl] = (rq * (dqhat - qhat * jnp.mean(dqhat * qhat, axis=-1, keepdims=True))).astype(bf16)

    full = lambda shp: pl.BlockSpec(shp, lambda i: (0,) * len(shp))
    return pl.pallas_call(
        body, name="xattn_bwd", grid=(S // tq,),
        in_specs=[pl.BlockSpec((tq, dc), lambda i: (i, col0)), pl.BlockSpec((tq, dc), lambda i: (i, col0 + 1)),
                  full((M, 2 * dc)), full((1, HEAD_DIM)), full((1, HEAD_DIM)),
                  pl.BlockSpec((tq, dc), lambda i: (i, D // dc - 1)), pl.BlockSpec(memory_space=pl.ANY)],
        out_specs=[pl.BlockSpec((tq, 2 * dc), lambda i: (i, col0 // 2)), full((M, dc)), full((M, dc)),
                   full((1, HEAD_DIM))],
        out_shape=[SDS(dproj_prev.shape, bf16), SDS((M, dc), f32), SDS((M, dc), f32), SDS((1, HEAD_DIM), f32)],
        input_output_aliases={6: 0}, compiler_params=_params(),
    )(proj, proj, kv, qg_row, kg_row, dy, dproj_prev)


def _mem_bwd(mem, mg_row, kv, dkn, dv, kg_row, w_kv_g):
    M, D = mem.shape
    rb, n = w_kv_g.shape[1], w_kv_g.shape[2]
    dc = n // 2

    def body(m_ref, g_ref, kv_ref, dkn_ref, dv_ref, kg_ref, w_ref, dw_ref, dmg_ref, dkg_ref, dkv_ref):
        mv = m_ref[...]
        r = lax.rsqrt(jnp.mean(mv * mv, axis=-1, keepdims=True) + EPS)
        mhat = mv * r
        mh = (mhat * g_ref[...]).astype(bf16)
        kg = kg_ref[...]
        dkg = jnp.zeros((1, HEAD_DIM), f32)
        for h in range(N_HEADS):
            sl = slice(h * HEAD_DIM, (h + 1) * HEAD_DIM)
            kh = kv_ref[:, sl]
            rk = lax.rsqrt(jnp.mean(kh * kh, axis=-1, keepdims=True) + EPS)
            khat = kh * rk
            dkn_h = dkn_ref[:, sl]
            dkg = dkg + jnp.sum(dkn_h * khat, axis=0, keepdims=True)
            dkhat = dkn_h * kg
            dkv_ref[:, sl] = (rk * (dkhat - khat * jnp.mean(dkhat * khat, axis=-1, keepdims=True))).astype(bf16)
        dkv_ref[:, dc:] = dv_ref[...].astype(bf16)
        dkg_ref[...] = dkg
        dkv = dkv_ref[...]
        dw_ref[...] = _dot(mh, dkv, TN).astype(bf16).reshape(N_DEV, rb, n)
        dmh = _dot(dkv, w_ref[...].reshape(N_DEV * rb, n), NT)
        dmg_ref[...] = jnp.sum(dmh * mhat, axis=0, keepdims=True)

    full = lambda shp: pl.BlockSpec(shp, lambda i: (0,) * len(shp))
    wspec = full((N_DEV, rb, n))
    return pl.pallas_call(
        body, name="mem_bwd", grid=(1,),
        in_specs=[full((M, D)), full((1, D)), full((M, n)), full((M, dc)), full((M, dc)), full((1, HEAD_DIM)), wspec],
        out_specs=[wspec, full((1, D)), full((1, HEAD_DIM))],
        out_shape=[SDS((N_DEV, rb, n), bf16), SDS((1, D), f32), SDS((1, HEAD_DIM), f32)],
        scratch_shapes=[pltpu.VMEM((M, n), bf16)], compiler_params=_params(),
    )(mem, mg_row, kv, dkn, dv, kg_row, w_kv_g)


SMALL = ("norm_g", "sgu_ln_g", "sgu_ln_b", "sgu_w", "sgu_b", "mem_norm_g", "q_norm_g", "k_norm_g")


def _pack_small(parts):
    flat = jnp.concatenate([parts[n].reshape(-1) for n in SMALL])
    pad = (-flat.shape[0]) % (8 * 128)
    return jnp.pad(flat, (0, pad)).reshape(-1, 128)


def _unpack_small(packed, like):
    flat = packed.reshape(-1)
    out, off = {}, 0
    for n in SMALL:
        size = math.prod(like[n].shape)
        out[n] = flat[off:off + size].reshape(like[n].shape)
        off += size
    return out


def kernel(x, mem, norm_g, w_in, sgu_ln_g, sgu_ln_b, sgu_w, sgu_b, mem_norm_g, w_mem_kv, q_norm_g, k_norm_g, w_out, loss_target, m_norm_g, m_w_in, m_sgu_ln_g, m_sgu_ln_b, m_sgu_w, m_sgu_b, m_mem_norm_g, m_w_mem_kv, m_q_norm_g, m_k_norm_g, m_w_out, v_norm_g, v_w_in, v_sgu_ln_g, v_sgu_ln_b, v_sgu_w, v_sgu_b, v_mem_norm_g, v_w_mem_kv, v_q_norm_g, v_k_norm_g, v_w_out):
    L, D, wc = w_in.shape
    S = x.shape[1]
    da = D // 2
    xs = x.reshape(S, D)
    mems = mem.reshape(mem.shape[1], D)
    tgt = loss_target.reshape(S, D)
    row = lambda a, l: a[l].reshape(1, -1)
    tie = lambda a, tok: a + tok[0:1, 0:1]
    sb_col, xa_col = 3 * da // HEAD_DIM, (3 * da + D) // (D // 4)

    ax, ay, ac = lax.axis_index("x"), lax.axis_index("y"), lax.axis_index("c")
    ids = jnp.stack([4 * ax + 2 * ay + ac, 2 * ax + ay, ac]).astype(jnp.int32)
    w_b = [(_cast_into_slot("cast_w_in", w_in, l, 512, ids), _cast_into_slot("cast_w_kv", w_mem_kv, l, 256, ids),
            _cast_into_slot("cast_w_out", w_out, l, 256, ids)) for l in range(L)]
    (w_in_land,) = _split_wait(_gather2_start("gather_w_in0", [w_b[0][0]], w_b[0][0]), w_b[0][0])
    in_fwd = _gather2_forward("gather_w_in0_forward", [w_in_land], w_in_land)

    acts = []
    xl = xs
    for l in range(L):
        g_row = row(norm_g, l)
        (w_in_g,) = _split_wait(in_fwd, g_row if l else in_fwd["token"])
        rest = _gather2_start(f"gather_w_rest{l}", [w_b[l][1], w_b[l][2]], w_in_g)
        g_row = tie(g_row, rest["token"])
        if l + 1 < L:
            nxt = _gather2_start(f"gather_w_in{l + 1}", [w_b[l + 1][0]], w_in_g)
            g_row = tie(g_row, nxt["token"])
        proj, h = _rms_proj(xl, g_row, w_in_g)
        rest_fwd = _gather2_forward(f"gather_w_rest{l}_forward", _split_wait(rest, proj), proj)
        y = _sgu_fwd(proj, tie(row(sgu_ln_g, l), rest_fwd["token"]), row(sgu_ln_b, l), sgu_w[l], sgu_b[l].T)
        y, o_b, car = _sb_fwd(proj, y, sb_col, rest_fwd["token"])
        w_kv_g, w_out_g = _split_wait(rest_fwd, o_b)
        mg_row = row(mem_norm_g, l)
        if l + 1 < L:
            in_fwd = _gather2_forward(f"gather_w_in{l + 1}_forward", _split_wait(nxt, o_b), o_b)
            mg_row = tie(mg_row, in_fwd["token"])
        kv = _mem_kv(mems, mg_row, w_kv_g)
        y = _xattn_fwd(proj, kv, row(q_norm_g, l), row(k_norm_g, l), y, xa_col)
        x_next = _out_proj(xl, y, w_out_g)
        acts.append((xl, proj, h, y, o_b, car, kv, w_in_g, w_kv_g, w_out_g))
        xl = x_next

    dx, dxb, loss_part = _loss_and_grad(xl, tgt, 512)
    loss = lax.psum(loss_part[0, 0], ("x", "y", "c"))

    seconds = {}
    pending = None
    small = {n: [None] * L for n in SMALL}
    for l in reversed(range(L)):
        xl, proj, h, y, o_b, car, kv, w_in_g, w_kv_g, w_out_g = acts[l]
        dy = _out_bwd_dy(dxb, w_out_g)
        order = dy
        if pending is not None:
            seconds[pending[0]] = _scatter2_second_level(pending[0], pending[1], dy, ids)
            order = seconds[pending[0]]["token"]
        g_out = _tn_grad("out_bwd_dw", y, dxb, D // N_DEV, 512, True, order)
        first = _scatter2_pair_start(f"scatter_g_out{l}_pair", [g_out], g_out)
        dproj, d_sw, d_sb, d_lg, d_lb = _sgu_bwd(proj, dy, tie(row(sgu_ln_g, l), first["token"]), row(sgu_ln_b, l),
                                                 sgu_w[l], sgu_b[l].T)
        seconds[f"g_out{l}"] = _scatter2_second_level(f"g_out{l}", first, dproj, ids)
        dproj = _sb_bwd(proj, o_b, car, dy, dproj, sb_col, seconds[f"g_out{l}"]["token"])
        dproj, dkn, dv, d_qg = _xattn_bwd(proj, kv, row(q_norm_g, l), row(k_norm_g, l), dy, dproj, xa_col)
        g_kv, d_mg, d_kg = _mem_bwd(mems, row(mem_norm_g, l), kv, dkn, dv, row(k_norm_g, l), w_kv_g)
        g_in_l = _tn_grad("in_bwd_dw", h, dproj, 512, wc, False, d_kg)
        first = _scatter2_pair_start(f"scatter_g_rest{l}_pair", [g_kv, g_in_l], g_in_l)
        dx, dxb, d_ng = _in_bwd_dx(dproj, w_in_g, xl, tie(row(norm_g, l), first["token"]), dx)
        pending = (f"g_rest{l}", first)
        for n, val in (("norm_g", d_ng), ("sgu_ln_g", d_lg), ("sgu_ln_b", d_lb), ("sgu_w", d_sw),
                       ("sgu_b", d_sb[:, :A_GROUPS].T), ("mem_norm_g", d_mg), ("q_norm_g", d_qg), ("k_norm_g", d_kg)):
            small[n][l] = val.reshape(-1)
    seconds[pending[0]] = _scatter2_second_level(pending[0], pending[1], dx, ids)

    weights = dict(norm_g=norm_g, sgu_ln_g=sgu_ln_g, sgu_ln_b=sgu_ln_b, sgu_w=sgu_w, sgu_b=sgu_b,
                   mem_norm_g=mem_norm_g, q_norm_g=q_norm_g, k_norm_g=k_norm_g)
    moms_m = dict(norm_g=m_norm_g, sgu_ln_g=m_sgu_ln_g, sgu_ln_b=m_sgu_ln_b, sgu_w=m_sgu_w, sgu_b=m_sgu_b,
                  mem_norm_g=m_mem_norm_g, q_norm_g=m_q_norm_g, k_norm_g=m_k_norm_g)
    moms_v = dict(norm_g=v_norm_g, sgu_ln_g=v_sgu_ln_g, sgu_ln_b=v_sgu_ln_b, sgu_w=v_sgu_w, sgu_b=v_sgu_b,
                  mem_norm_g=v_mem_norm_g, q_norm_g=v_q_norm_g, k_norm_g=v_k_norm_g)
    part = _pack_small({n: jnp.stack(small[n]).reshape(weights[n].shape) for n in SMALL})
    (r_small,) = _exchange("gather_small_grads", [tie(part, seconds[pending[0]]["token"])], scatter=False)
    sm = _small_sum_adam(r_small, _pack_small(weights), _pack_small(moms_m), _pack_small(moms_v))

    a_out = a_kv = a_in = None
    order = sm[0]
    for l in reversed(range(L)):
        (r_out,) = _scatter2_finish(seconds[f"g_out{l}"], order)
        a_out = _sum_adam("adam_w_out", r_out, w_out, m_w_out, v_w_out, l, a_out, 128, ids)
        r_kv, r_in = _scatter2_finish(seconds[f"g_rest{l}"], a_out[0])
        a_kv = _sum_adam("adam_w_kv", r_kv, w_mem_kv, m_w_mem_kv, v_w_mem_kv, l, a_kv, 256, ids)
        a_in = _sum_adam("adam_w_in", r_in, w_in, m_w_in, v_w_in, l, a_in, 256, ids)
        order = a_in[0]
    res = {"w_out": a_out, "w_mem_kv": a_kv, "w_in": a_in}
    sm = [_unpack_small(a, weights) for a in sm]
    for n in SMALL:
        res[n] = [a[n] for a in sm]

    order = ("norm_g", "w_in", "sgu_ln_g", "sgu_ln_b", "sgu_w", "sgu_b", "mem_norm_g", "w_mem_kv", "q_norm_g",
             "k_norm_g", "w_out")
    outs = [loss, dx.reshape(x.shape)]
    for k in range(4):
        outs += [res[n][k] for n in order]
    return tuple(outs)
```

```python
import functools
import math

import jax
import jax.numpy as jnp
from jax import lax
from jax.experimental import pallas as pl
from jax.experimental.pallas import tpu as pltpu

f32 = jnp.float32
bf16 = jnp.bfloat16
SDS = jax.ShapeDtypeStruct

N_DEV = 8
EPS = 1e-6
CHUNK = 128
A_GROUPS = 8
HEAD_DIM = 128
N_HEADS = 4
TQ = 256
TK = 128
ADAM_LR, ADAM_B1, ADAM_B2, ADAM_EPS, ADAM_WD, ADAM_STEP = 0.001, 0.9, 0.999, 1e-08, 0.01, 10
MIB = 1024 * 1024

NT = (((1,), (1,)), ((), ()))
TN = (((0,), (0,)), ((), ()))


def _params(vmem_mib=48):
    return pltpu.CompilerParams(vmem_limit_bytes=vmem_mib * MIB)


def _gelu_and_grad(x):
    e = lax.erf(x * (1.0 / math.sqrt(2.0)))
    cdf = 0.5 * (1.0 + e)
    pdf = jnp.exp(-0.5 * x * x) * (1.0 / math.sqrt(2.0 * math.pi))
    return x * cdf, cdf + x * pdf


def _gelu(x):
    return 0.5 * x * (1.0 + lax.erf(x * (1.0 / math.sqrt(2.0))))


def _silu_and_grad(z):
    sg = jax.nn.sigmoid(z)
    return z * sg, sg * (1.0 + z * (1.0 - sg))


def _dot(a, b, dims=None):
    if dims is None:
        return jnp.dot(a, b, preferred_element_type=f32)
    return lax.dot_general(a, b, dims, preferred_element_type=f32)


_HBM = pl.BlockSpec(memory_space=pltpu.HBM)
_SEM = pl.BlockSpec(memory_space=pltpu.SEMAPHORE)
_EFFECT = pltpu.SideEffectType.DATAFLOW_SIDE_EFFECTING


def _split_start(name, bufs, n_remote, n_local, build, after):
    nb = len(bufs)

    def body(*refs):
        token = refs[-1]
        locals_, remotes = build(refs[:nb], *refs[nb + 1:nb + 4])
        for cp in locals_ + remotes:
            cp.start()
        token[...] = jnp.zeros_like(token)

    hbm = lambda a: pltpu.with_memory_space_constraint(a, pltpu.HBM)
    outs = pl.pallas_call(
        body, name=name,
        out_shape=(pltpu.SemaphoreType.DMA((n_remote,)), pltpu.SemaphoreType.DMA((n_remote,)),
                   pltpu.SemaphoreType.DMA((max(n_local, 1),)),
                   *[pltpu.HBM(b.shape, b.dtype) for b in bufs], SDS((8, 128), f32)),
        in_specs=[_HBM] * nb + [pl.BlockSpec(memory_space=pl.ANY)],
        out_specs=(_SEM, _SEM, _SEM, *[_HBM] * nb, pl.BlockSpec(memory_space=pltpu.VMEM)),
        input_output_aliases={k: 3 + k for k in range(nb)},
        compiler_params=pltpu.CompilerParams(has_side_effects=_EFFECT),
    )(*[hbm(b) for b in bufs], after)
    return dict(name=name, build=build, sems=outs[:3], bufs=outs[3:3 + nb], token=outs[-1])


def _split_wait(handle, after):
    build, bufs = handle["build"], handle["bufs"]
    nb = len(bufs)

    def body(*refs):
        locals_, remotes = build(refs[:nb], *refs[nb:nb + 3])
        for cp in remotes:
            cp.wait_recv()
        for cp in remotes:
            cp.wait_send()
        for cp in locals_:
            cp.wait()

    outs = pl.pallas_call(
        body, name=handle["name"] + "_wait",
        out_shape=tuple(pltpu.HBM(b.shape, b.dtype) for b in bufs),
        in_specs=[_HBM] * nb + [_SEM] * 3 + [pl.BlockSpec(memory_space=pl.ANY)],
        out_specs=tuple([_HBM] * nb),
        input_output_aliases={k: k for k in range(nb)},
        compiler_params=pltpu.CompilerParams(has_side_effects=_EFFECT),
    )(*bufs, *handle["sems"], after)
    return list(outs)


def _remote(src, dst, send_sems, recv_sems, k, to):
    return pltpu.make_async_remote_copy(src_ref=src, dst_ref=dst, send_sem=send_sems.at[k], recv_sem=recv_sems.at[k],
                                        device_id=to, device_id_type=pl.DeviceIdType.MESH)


def _other_chips(x, y):
    return [(1 - x, y), (x, 1 - y), (1 - x, 1 - y)]


def _gather2_start(name, lands, after):
    def build(refs, send, recv, loc):
        x, y, c = lax.axis_index("x"), lax.axis_index("y"), lax.axis_index("c")
        me = 4 * x + 2 * y + c
        remotes = []
        for a, d in enumerate(refs):
            remotes.append(_remote(d.at[me], d.at[me], send, recv, 4 * a, (x, y, 1 - c)))
            remotes += [_remote(d.at[me], d.at[me], send, recv, 4 * a + 1 + k, (px, py, c))
                        for k, (px, py) in enumerate(_other_chips(x, y))]
        return [], remotes

    return _split_start(name, list(lands), 4 * len(lands), 0, build, after)


def _gather2_forward(name, lands, after):
    n = len(lands)

    def build(refs, send, recv, loc):
        x, y, c = lax.axis_index("x"), lax.axis_index("y"), lax.axis_index("c")
        slots = [4 * px + 2 * py + c for px, py in _other_chips(x, y)]
        return [], [_remote(d.at[sl], d.at[sl], send, recv, 3 * a + k, (x, y, 1 - c))
                    for a, d in enumerate(refs) for k, sl in enumerate(slots)]

    return _split_start(name, list(lands), 3 * n, 0, build, after)


def _scatter2_pair_start(name, srcs, after):
    n = len(srcs)

    def build(refs, send, recv, loc):
        x, y, c = lax.axis_index("x"), lax.axis_index("y"), lax.axis_index("c")
        return [], [_remote(refs[a].at[2 * q + 1 - c], refs[n + a].at[q], send, recv, 4 * a + q, (x, y, 1 - c))
                    for a in range(n) for q in range(4)]

    lands = [lax.empty((4,) + s.shape[1:], s.dtype) for s in srcs]
    return _split_start(name, list(srcs) + lands, 4 * n, 0, build, after)


def _scatter2_chip_start(name, pairs, after):
    n = len(pairs)

    def build(refs, send, recv, loc):
        x, y, c = lax.axis_index("x"), lax.axis_index("y"), lax.axis_index("c")
        return [], [_remote(refs[a].at[2 * px + py], refs[n + a].at[2 * x + y], send, recv, 3 * a + k, (px, py, c))
                    for a in range(n) for k, (px, py) in enumerate(_other_chips(x, y))]

    return _split_start(name, list(pairs) + [lax.empty(p.shape, p.dtype) for p in pairs], 3 * n, 0, build, after)


def _pair_sum(name, src, theirs, ids):
    _, R, C = theirs.shape
    tr = min(R, 1024)

    def body(ids_ref, a_ref, b_ref, o_ref):
        o_ref[...] = (a_ref[...].astype(f32) + b_ref[...].astype(f32)).astype(bf16)

    spec = pl.BlockSpec((None, tr, C), lambda q, i, ids: (q, i, 0))
    return pl.pallas_call(
        body, name=name,
        grid_spec=pltpu.PrefetchScalarGridSpec(
            num_scalar_prefetch=1, grid=(4, R // tr),
            in_specs=[pl.BlockSpec((None, tr, C), lambda q, i, ids: (2 * q + ids[2], i, 0)), spec], out_specs=spec),
        out_shape=SDS(theirs.shape, bf16), compiler_params=_params(),
    )(ids, src, theirs)


def _scatter2_second_level(name, first, after, ids):
    outs = _split_wait(first, after)
    n = len(outs) // 2
    pairs = [_pair_sum(f"pair_sum_{name}{a}", outs[a], outs[n + a], ids) for a in range(n)]
    return _scatter2_chip_start(f"scatter_{name}_chip", pairs, ids)


def _scatter2_finish(second, after):
    outs = _split_wait(second, after)
    n = len(outs) // 2
    return [(outs[a], outs[n + a]) for a in range(n)]


def _cast_into_slot(name, w, l, tr, ids):
    _, R, C = w.shape

    def body(ids_ref, w_ref, o_ref):
        o_ref[...] = w_ref[...].astype(bf16)

    return pl.pallas_call(
        body, name=name,
        grid_spec=pltpu.PrefetchScalarGridSpec(
            num_scalar_prefetch=1, grid=(R // tr,),
            in_specs=[pl.BlockSpec((None, tr, C), lambda i, ids: (l, i, 0))],
            out_specs=pl.BlockSpec((None, tr, C), lambda i, ids: (ids[0], i, 0))),
        out_shape=SDS((N_DEV, R, C), bf16), compiler_params=_params(),
    )(ids, w)


def _into_slot(name, a, ids):
    R, C = a.shape

    def body(ids_ref, a_ref, o_ref):
        o_ref[...] = a_ref[...]

    return pl.pallas_call(
        body, name=name,
        grid_spec=pltpu.PrefetchScalarGridSpec(
            num_scalar_prefetch=1, grid=(1,),
            in_specs=[pl.BlockSpec((R, C), lambda i, ids: (0, 0))],
            out_specs=pl.BlockSpec((None, R, C), lambda i, ids: (ids[0], 0, 0))),
        out_shape=SDS((N_DEV, R, C), f32), compiler_params=_params(),
    )(ids, a)


def _adam_math(w, g, m, v):
    m2 = ADAM_B1 * m + (1.0 - ADAM_B1) * g
    v2 = ADAM_B2 * v + (1.0 - ADAM_B2) * (g * g)
    m_hat = m2 / (1.0 - ADAM_B1 ** ADAM_STEP)
    v_hat = v2 / (1.0 - ADAM_B2 ** ADAM_STEP)
    delta = -ADAM_LR * (m_hat / (jnp.sqrt(v_hat) + ADAM_EPS) + ADAM_WD * w)
    return delta, m2, v2


def _sum_adam(name, pair_recv, w, m, v, l, prev, tr, ids):
    own, recv = pair_recv
    L, R, C = w.shape
    slots = recv.shape[0]

    def body(ids_ref, r_ref, own_ref, w_ref, m_ref, v_ref, *rest):
        g_ref, d_ref, m2_ref, v2_ref = rest[-4:]
        terms = [jnp.where(ids_ref[1] == q, own_ref[...], r_ref[q]).astype(f32) for q in range(slots)]
        g = terms[0]
        for t in terms[1:]:
            g = g + t
        d, m2, v2 = _adam_math(w_ref[...], g, m_ref[...], v_ref[...])
        g_ref[...] = g
        d_ref[...] = d
        m2_ref[...] = m2
        v2_ref[...] = v2

    wspec = pl.BlockSpec((None, tr, C), lambda i, ids: (l, i, 0))
    in_specs = [pl.BlockSpec((slots, tr, C), lambda i, ids: (0, i, 0)),
                pl.BlockSpec((None, tr, C), lambda i, ids: (ids[1], i, 0)), wspec, wspec, wspec]
    args = [ids, recv, own, w, m, v]
    aliases = {}
    if prev is not None:
        in_specs += [pl.BlockSpec(memory_space=pl.ANY)] * 4
        args += list(prev)
        aliases = {6 + k: k for k in range(4)}
    return pl.pallas_call(
        body, name=name,
        grid_spec=pltpu.PrefetchScalarGridSpec(num_scalar_prefetch=1, grid=(R // tr,), in_specs=in_specs,
                                               out_specs=[wspec] * 4),
        out_shape=[SDS((L, R, C), f32)] * 4, input_output_aliases=aliases, compiler_params=_params(),
    )(*args)


def _small_sum_adam(recv_head, recv_tail, w, m, v):
    r0 = recv_head.shape[1]
    R, C = w.shape

    def body(rh_ref, rt_ref, w_ref, m_ref, v_ref, g_ref, d_ref, m2_ref, v2_ref):
        for r_ref, rows in ((rh_ref, slice(0, r0)), (rt_ref, slice(r0, R))):
            g = r_ref[0]
            for s in range(1, N_DEV):
                g = g + r_ref[s]
            d, m2, v2 = _adam_math(w_ref[rows, :], g, m_ref[rows, :], v_ref[rows, :])
            g_ref[rows, :] = g
            d_ref[rows, :] = d
            m2_ref[rows, :] = m2
            v2_ref[rows, :] = v2

    return pl.pallas_call(
        body, name="small_sum_adam", out_shape=[SDS((R, C), f32)] * 4, compiler_params=_params(),
    )(recv_head, recv_tail, w, m, v)


def _loss_and_grad(xf, tgt, tm):
    S, D = xf.shape

    def body(x_ref, t_ref, dx_ref, dxb_ref, l_ref):
        i = pl.program_id(0)
        d = x_ref[...] - t_ref[...]
        dx = d * (1.0 / D)
        dx_ref[...] = dx
        dxb_ref[...] = dx.astype(bf16)
        e = d * d
        part = e[:, 0:128]
        for k in range(1, D // 128):
            part = part + e[:, k * 128:(k + 1) * 128]
        part = jnp.sum(part.reshape(tm // 8, 8, 128), axis=0)

        @pl.when(i == 0)
        def _():
            l_ref[...] = jnp.zeros_like(l_ref)

        l_ref[...] += part

        @pl.when(i == pl.num_programs(0) - 1)
        def _():
            tot = jnp.sum(l_ref[...], axis=1, keepdims=True)
            tot = jnp.sum(tot, axis=0, keepdims=True)
            l_ref[...] = jnp.broadcast_to(tot * (0.5 / D), l_ref.shape)

    row = pl.BlockSpec((tm, D), lambda i: (i, 0))
    return pl.pallas_call(
        body, name="loss_grad", grid=(S // tm,),
        in_specs=[row, row], out_specs=[row, row, pl.BlockSpec((8, 128), lambda i: (0, 0))],
        out_shape=[SDS((S, D), f32), SDS((S, D), bf16), SDS((8, 128), f32)], compiler_params=_params(),
    )(xf, tgt)


def _rms_proj(x, g_row, w_in_g, tm=512):
    S, D = x.shape
    wc = w_in_g.shape[2]
    tn = 256
    per = wc // tn
    n_out = N_DEV * wc

    def body(x_ref, g_ref, w_ref, proj_ref, h_ref):
        @pl.when(pl.program_id(1) == 0)
        def _():
            xv = x_ref[...]
            r = lax.rsqrt(jnp.mean(xv * xv, axis=-1, keepdims=True) + EPS)
            h_ref[...] = (xv * r * g_ref[...]).astype(bf16)

        proj_ref[...] = _dot(h_ref[...], w_ref[...])

    return pl.pallas_call(
        body, name="rms_proj", grid=(S // tm, n_out // tn),
        in_specs=[pl.BlockSpec((tm, D), lambda i, j: (i, 0)),
                  pl.BlockSpec((1, D), lambda i, j: (0, 0)),
                  pl.BlockSpec((None, D, tn), lambda i, j: (j // per, 0, j % per))],
        out_specs=[pl.BlockSpec((tm, tn), lambda i, j: (i, j)), pl.BlockSpec((tm, D), lambda i, j: (i, 0))],
        out_shape=[SDS((S, n_out), f32), SDS((S, D), bf16)], compiler_params=_params(),
    )(x, g_row, w_in_g)


def _out_proj(x, y, w_out_g, after, tm=512, tn=512):
    S, D = x.shape
    rb = w_out_g.shape[1]

    def body(x_ref, y_ref, w_ref, after_ref, o_ref):
        w = w_ref[...].reshape(N_DEV * rb, tn)
        o_ref[...] = x_ref[...] + _dot(y_ref[...], w)

    return pl.pallas_call(
        body, name="out_proj", grid=(S // tm, D // tn),
        in_specs=[pl.BlockSpec((tm, tn), lambda i, j: (i, j)),
                  pl.BlockSpec((tm, D), lambda i, j: (i, 0)),
                  pl.BlockSpec((N_DEV, rb, tn), lambda i, j: (0, 0, j)), pl.BlockSpec(memory_space=pl.ANY)],
        out_specs=pl.BlockSpec((tm, tn), lambda i, j: (i, j)),
        out_shape=SDS((S, D), f32), compiler_params=_params(),
    )(x, y, w_out_g, after)


def _out_bwd_dy(dxb, w_out_g, tm=512):
    S, D = dxb.shape
    rb = w_out_g.shape[1]

    def body(dx_ref, w_ref, o_ref):
        o_ref[...] = _dot(dx_ref[...], w_ref[...], NT)

    return pl.pallas_call(
        body, name="out_bwd_dy", grid=(S // tm, N_DEV),
        in_specs=[pl.BlockSpec((tm, D), lambda i, j: (i, 0)),
                  pl.BlockSpec((None, rb, D), lambda i, j: (j, 0, 0))],
        out_specs=pl.BlockSpec((tm, rb), lambda i, j: (i, j)),
        out_shape=SDS((S, D), f32), compiler_params=_params(),
    )(dxb, w_out_g)


def _tn_grad(name, a, b, tm, tn, rows_major, after):
    S, M = a.shape
    N = b.shape[1]
    if rows_major:
        out_shape = SDS((N_DEV, M // N_DEV, N), bf16)
        assert tm == M // N_DEV
        out_spec = pl.BlockSpec((None, tm, tn), lambda i, j: (i, 0, j))
    else:
        out_shape = SDS((N_DEV, M, N // N_DEV), bf16)
        assert tn == N // N_DEV
        out_spec = pl.BlockSpec((None, tm, tn), lambda i, j: (j, i, 0))

    def body(a_ref, b_ref, after_ref, o_ref):
        o_ref[...] = _dot(a_ref[...], b_ref[...], TN).astype(bf16)

    return pl.pallas_call(
        body, name=name, grid=(M // tm, N // tn),
        in_specs=[pl.BlockSpec((S, tm), lambda i, j: (0, i)), pl.BlockSpec((S, tn), lambda i, j: (0, j)),
                  pl.BlockSpec(memory_space=pl.ANY)],
        out_specs=out_spec, out_shape=out_shape, compiler_params=_params(),
    )(a, b, after)


def _in_bwd_dx(dproj, w_in_g, x, g_row, dx_next, tm=512):
    S, D = x.shape
    wc = w_in_g.shape[2]

    def body(dp_ref, w_ref, x_ref, g_ref, dxn_ref, dx_ref, dxb_ref, dg_ref, acc_ref):
        i, k = pl.program_id(0), pl.program_id(1)

        @pl.when(k == 0)
        def _():
            acc_ref[...] = jnp.zeros_like(acc_ref)

        acc_ref[...] += _dot(dp_ref[...], w_ref[...], NT)

        @pl.when(jnp.logical_and(i == 0, k == 0))
        def _():
            dg_ref[...] = jnp.zeros_like(dg_ref)

        @pl.when(k == N_DEV - 1)
        def _():
            dh = acc_ref[...]
            xv = x_ref[...]
            r = lax.rsqrt(jnp.mean(xv * xv, axis=-1, keepdims=True) + EPS)
            xhat = xv * r
            dxhat = dh * g_ref[...]
            dx = r * (dxhat - xhat * jnp.mean(dxhat * xhat, axis=-1, keepdims=True)) + dxn_ref[...]
            dx_ref[...] = dx
            dxb_ref[...] = dx.astype(bf16)
            dg_ref[...] += jnp.sum(dh * xhat, axis=0, keepdims=True)

    row = pl.BlockSpec((tm, D), lambda i, k: (i, 0))
    return pl.pallas_call(
        body, name="in_bwd_dx", grid=(S // tm, N_DEV),
        in_specs=[pl.BlockSpec((tm, wc), lambda i, k: (i, k)),
                  pl.BlockSpec((None, D, wc), lambda i, k: (k, 0, 0)),
                  row, pl.BlockSpec((1, D), lambda i, k: (0, 0)), row],
        out_specs=[row, row, pl.BlockSpec((1, D), lambda i, k: (0, 0))],
        out_shape=[SDS((S, D), f32), SDS((S, D), bf16), SDS((1, D), f32)],
        scratch_shapes=[pltpu.VMEM((tm, D), f32)], compiler_params=_params(56),
    )(dproj, w_in_g, x, g_row, dx_next)


def _sgu_fwd(proj, ln_g, ln_b, w_s, b_t):
    S = proj.shape[0]
    da = A_GROUPS * HEAD_DIM
    D = 2 * da

    def body(u_ref, v_ref, z_ref, lg_ref, lb_ref, w_ref, bt_ref, y_ref):
        u = _gelu(u_ref[...])
        v = _gelu(v_ref[...])
        z = z_ref[...]
        mu = jnp.mean(v, axis=-1, keepdims=True)
        xc = v - mu
        rs = lax.rsqrt(jnp.mean(xc * xc, axis=-1, keepdims=True) + EPS)
        vn = (xc * rs * lg_ref[...] + lb_ref[...]).astype(bf16)
        gate = u * (z * jax.nn.sigmoid(z))
        tri = lax.broadcasted_iota(jnp.int32, (CHUNK, CHUNK), 0) >= lax.broadcasted_iota(jnp.int32, (CHUNK, CHUNK), 1)
        for g in range(A_GROUPS):
            sl = slice(g * HEAD_DIM, (g + 1) * HEAD_DIM)
            wm = jnp.where(tri, w_ref[g], 0.0).astype(bf16)
            mixed = _dot(wm, vn[:, sl]) + bt_ref[:, g:g + 1]
            y_ref[:, sl] = (gate[:, sl] * mixed).astype(bf16)

    blk = lambda cb: pl.BlockSpec((CHUNK, da), lambda c: (c, cb))
    full = lambda shp: pl.BlockSpec(shp, lambda c: (0,) * len(shp))
    return pl.pallas_call(
        body, name="sgu_fwd", grid=(S // CHUNK,),
        in_specs=[blk(0), blk(1), blk(2), full((1, da)), full((1, da)),
                  full((A_GROUPS, CHUNK, CHUNK)), full((CHUNK, A_GROUPS))],
        out_specs=blk(0), out_shape=SDS((S, D), bf16), compiler_params=_params(),
    )(proj, proj, proj, ln_g, ln_b, w_s, b_t)


def _sgu_bwd(proj, dy, ln_g, ln_b, w_s, b_t):
    S = proj.shape[0]
    da = A_GROUPS * HEAD_DIM
    n_proj = proj.shape[1]

    def body(u_ref, v_ref, z_ref, dy_ref, lg_ref, lb_ref, w_ref, bt_ref,
             dp_ref, dw_ref, db_ref, dlg_ref, dlb_ref, dvn_ref):
        @pl.when(pl.program_id(0) == 0)
        def _():
            dw_ref[...] = jnp.zeros_like(dw_ref)
            db_ref[...] = jnp.zeros_like(db_ref)
            dlg_ref[...] = jnp.zeros_like(dlg_ref)
            dlb_ref[...] = jnp.zeros_like(dlb_ref)

        up, vp, z, dy = u_ref[...], v_ref[...], z_ref[...], dy_ref[...]
        u, gu = _gelu_and_grad(up)
        v, gv = _gelu_and_grad(vp)
        s, gs = _silu_and_grad(z)
        mu = jnp.mean(v, axis=-1, keepdims=True)
        xc = v - mu
        rs = lax.rsqrt(jnp.mean(xc * xc, axis=-1, keepdims=True) + EPS)
        vhat = xc * rs
        lg = lg_ref[...]
        vn = (vhat * lg + lb_ref[...]).astype(bf16)
        tri = lax.broadcasted_iota(jnp.int32, (CHUNK, CHUNK), 0) >= lax.broadcasted_iota(jnp.int32, (CHUNK, CHUNK), 1)
        lane = lax.broadcasted_iota(jnp.int32, (CHUNK, HEAD_DIM), 1)
        dys = dy * s
        db = jnp.zeros((CHUNK, HEAD_DIM), f32)
        for g in range(A_GROUPS):
            sl = slice(g * HEAD_DIM, (g + 1) * HEAD_DIM)
            wm = jnp.where(tri, w_ref[g], 0.0).astype(bf16)
            mixed = _dot(wm, vn[:, sl]) + bt_ref[:, g:g + 1]
            dmix = dys[:, sl] * u[:, sl]
            dp_ref[:, sl] = (dys[:, sl] * mixed * gu[:, sl]).astype(bf16)
            dp_ref[:, 2 * da + g * HEAD_DIM:2 * da + (g + 1) * HEAD_DIM] = (
                dy[:, sl] * u[:, sl] * mixed * gs[:, sl]).astype(bf16)
            dmb = dmix.astype(bf16)
            dw_ref[g] += jnp.where(tri, _dot(dmb, vn[:, sl], NT), 0.0)
            dvn_ref[:, sl] = _dot(wm, dmb, TN)
            db = db + jnp.where(lane == g, jnp.sum(dmix, axis=1, keepdims=True), 0.0)
        db_ref[...] += db
        dvn = dvn_ref[...]
        dlg_ref[...] += jnp.sum(dvn * vhat, axis=0, keepdims=True)
        dlb_ref[...] += jnp.sum(dvn, axis=0, keepdims=True)
        dvhat = dvn * lg
        dv = rs * (dvhat - jnp.mean(dvhat, axis=-1, keepdims=True)
                   - vhat * jnp.mean(dvhat * vhat, axis=-1, keepdims=True))
        dp_ref[:, da:2 * da] = (dv * gv).astype(bf16)

    blk = lambda cb: pl.BlockSpec((CHUNK, da), lambda c: (c, cb))
    full = lambda shp: pl.BlockSpec(shp, lambda c: (0,) * len(shp))
    return pl.pallas_call(
        body, name="sgu_bwd", grid=(S // CHUNK,),
        in_specs=[blk(0), blk(1), blk(2), blk(0), full((1, da)), full((1, da)),
                  full((A_GROUPS, CHUNK, CHUNK)), full((CHUNK, A_GROUPS))],
        out_specs=[pl.BlockSpec((CHUNK, 3 * da), lambda c: (c, 0)), full((A_GROUPS, CHUNK, CHUNK)),
                   full((CHUNK, HEAD_DIM)), full((1, da)), full((1, da))],
        out_shape=[SDS((S, n_proj), bf16), SDS((A_GROUPS, CHUNK, CHUNK), f32), SDS((CHUNK, HEAD_DIM), f32),
                   SDS((1, da), f32), SDS((1, da), f32)],
        scratch_shapes=[pltpu.VMEM((CHUNK, da), f32)], compiler_params=_params(),
    )(proj, proj, proj, dy, ln_g, ln_b, w_s, b_t)


def _sb_scores(q, kblk, kb, rows, cols, masked):
    z = _dot(q, kblk, NT) * (1.0 / math.sqrt(HEAD_DIM))
    t = jnp.log1p(jnp.exp(-jnp.abs(z)))
    log_1mb = -(jnp.maximum(z, 0.0) + t)
    log_beta = jnp.minimum(z, 0.0) - t
    if not masked:
        return None, log_beta, log_1mb
    causal = (cols + kb * TK) < rows
    return causal, log_beta, jnp.where(causal, log_1mb, 0.0)


def _sb_tiles(i):
    rows = i * TQ + lax.broadcasted_iota(jnp.int32, (TQ, TK), 0)
    cols = lax.broadcasted_iota(jnp.int32, (TQ, TK), 1)
    r_i = lax.broadcasted_iota(jnp.int32, (TK, TK), 0)
    c_i = lax.broadcasted_iota(jnp.int32, (TK, TK), 1)
    return rows, cols, (r_i > c_i).astype(bf16), (r_i < c_i).astype(bf16)


def _suffix_sum(t, tri):
    hi = t.astype(bf16)
    lo = (t - hi.astype(f32)).astype(bf16)
    return _dot(hi, tri) + _dot(lo, tri)


def _sb_fwd(proj, y_prev, col0, after):
    S = proj.shape[0]
    D = y_prev.shape[1]
    dh = N_HEADS * HEAD_DIM
    n_diag = TQ // TK

    def body(q_ref, k_ref, v_ref, z_ref, yp_ref, after_ref, y_ref, o_ref, car_ref, qb, kb_s, vb_s, c_ref):
        i = pl.program_id(0)

        @pl.when(i == 0)
        def _():
            kb_s[...] = k_ref[...].astype(bf16)
            vb_s[...] = v_ref[...].astype(bf16)

        qb[...] = q_ref[...].astype(bf16)
        o_ref[...] = jnp.zeros_like(o_ref)
        c_ref[...] = jnp.zeros_like(c_ref)
        car_ref[...] = jnp.zeros_like(car_ref)
        nkb = (i + 1) * n_diag
        rows, cols, upper, _ = _sb_tiles(i)

        def make_step(masked):
            def step(jj, carry):
                kb = nkb - 1 - jj
                off = pl.multiple_of(kb * TK, TK)
                hs = range(N_HEADS)
                sls = [slice(h * HEAD_DIM, (h + 1) * HEAD_DIM) for h in hs]
                sc = [_sb_scores(qb[:, sls[h]], kb_s[pl.ds(off, TK), sls[h]], kb, rows, cols, masked) for h in hs]
                suf = [_suffix_sum(sc[h][2], upper) for h in hs]
                cs = [c_ref[h] for h in hs]
                es = [jnp.exp(sc[h][1] + suf[h] + cs[h]) for h in hs]
                if masked:
                    es = [jnp.where(sc[h][0], es[h], 0.0) for h in hs]
                pv = [_dot(es[h].astype(bf16), vb_s[pl.ds(off, TK), sls[h]]) for h in hs]
                for h in hs:
                    o_ref[:, sls[h]] += pv[h]
                    car_ref[h] = jnp.where(cols == kb, cs[h], car_ref[h])
                    c_ref[h] = cs[h] + jnp.sum(sc[h][2], axis=1, keepdims=True)
                return carry
            return step

        lax.fori_loop(0, n_diag, make_step(True), 0)
        lax.fori_loop(n_diag, nkb, make_step(False), 0)
        z = z_ref[...]
        y_ref[...] = (o_ref[...] * (z * jax.nn.sigmoid(z))).astype(bf16)

    cb = col0 * HEAD_DIM // dh
    qspec = lambda k: pl.BlockSpec((TQ, dh), lambda i: (i, cb + k))
    kspec = lambda k: pl.BlockSpec((S, dh), lambda i: (0, cb + k))
    return pl.pallas_call(
        body, name="sb_fwd", grid=(S // TQ,),
        in_specs=[qspec(0), kspec(1), kspec(2), qspec(3), pl.BlockSpec(memory_space=pl.ANY),
                  pl.BlockSpec(memory_space=pl.ANY)],
        out_specs=[pl.BlockSpec((TQ, dh), lambda i: (i, A_GROUPS * HEAD_DIM // dh)),
                   pl.BlockSpec((TQ, dh), lambda i: (i, 0)),
                   pl.BlockSpec((N_HEADS, TQ, TK), lambda i: (0, i, 0))],
        out_shape=[SDS((S, D), bf16), SDS((S, dh), f32), SDS((N_HEADS, S, TK), f32)],
        input_output_aliases={4: 0},
        scratch_shapes=[pltpu.VMEM((TQ, dh), bf16), pltpu.VMEM((S, dh), bf16), pltpu.VMEM((S, dh), bf16),
                        pltpu.VMEM((N_HEADS, TQ, TK), f32)],
        compiler_params=_params(),
    )(proj, proj, proj, proj, y_prev, after)


def _sb_bwd(proj, o, car, dy, dproj_prev, col0, after):
    S = proj.shape[0]
    n_i = S // TQ
    dh = N_HEADS * HEAD_DIM
    n_diag = TQ // TK
    cb = col0 * HEAD_DIM // dh
    scale = 1.0 / math.sqrt(HEAD_DIM)

    def body(q_ref, k_ref, v_ref, z_ref, o_ref, car_ref, dy_ref, dpp_ref, after_ref,
             dp_ref, qb, kb_s, vb_s, dob, p_ref, dq_acc, dk_acc, dv_acc, st_a, st_b, st_k, st_v):
        i = pl.program_id(0)

        def put(stage_ref, row0, nrows, k):
            pltpu.sync_copy(stage_ref, dp_ref.at[pl.ds(row0, nrows), pl.ds((cb + k) * dh, dh)])

        @pl.when(i == 0)
        def _():
            kb_s[...] = k_ref[...].astype(bf16)
            vb_s[...] = v_ref[...].astype(bf16)
            dk_acc[...] = jnp.zeros_like(dk_acc)
            dv_acc[...] = jnp.zeros_like(dv_acc)

        s, gs = _silu_and_grad(z_ref[...])
        dy = dy_ref[...]
        st_b[...] = (dy * o_ref[...] * gs).astype(bf16)
        dob[...] = (dy * s).astype(bf16)
        qb[...] = q_ref[...].astype(bf16)
        p_ref[...] = jnp.zeros_like(p_ref)
        dq_acc[...] = jnp.zeros_like(dq_acc)
        nkb = (i + 1) * n_diag
        rows, cols, upper, lower = _sb_tiles(i)

        def make_step(masked):
            def step(kb, carry):
                off = pl.multiple_of(kb * TK, TK)
                hs = range(N_HEADS)
                sls = [slice(h * HEAD_DIM, (h + 1) * HEAD_DIM) for h in hs]
                qs = [qb[:, sls[h]] for h in hs]
                ks = [kb_s[pl.ds(off, TK), sls[h]] for h in hs]
                dos = [dob[:, sls[h]] for h in hs]
                sc = [_sb_scores(qs[h], ks[h], kb, rows, cols, masked) for h in hs]
                da = [_dot(dos[h], vb_s[pl.ds(off, TK), sls[h]], NT) for h in hs]
                suf = [_suffix_sum(sc[h][2], upper) for h in hs]
                onehot = cols == kb
                cs = [jnp.sum(jnp.where(onehot, car_ref[h], 0.0), axis=1, keepdims=True) for h in hs]
                es = [jnp.exp(sc[h][1] + suf[h] + cs[h]) for h in hs]
                if masked:
                    es = [jnp.where(sc[h][0], es[h], 0.0) for h in hs]
                gs_ = [da[h] * es[h] for h in hs]
                ps = [p_ref[h] for h in hs]
                pre = [_suffix_sum(gs_[h], lower) + ps[h] for h in hs]
                dzs = []
                for h in hs:
                    beta = jnp.exp(sc[h][1])
                    dzz = gs_[h] * (1.0 - beta) - beta * pre[h]
                    if masked:
                        dzz = jnp.where(sc[h][0], dzz, 0.0)
                    dzs.append((dzz * scale).astype(bf16))
                dqs = [_dot(dzs[h], ks[h]) for h in hs]
                dks = [_dot(dzs[h], qs[h], TN) for h in hs]
                dvs = [_dot(es[h].astype(bf16), dos[h], TN) for h in hs]
                for h in hs:
                    dq_acc[:, sls[h]] += dqs[h]
                    dk_acc[pl.ds(off, TK), sls[h]] += dks[h]
                    dv_acc[pl.ds(off, TK), sls[h]] += dvs[h]
                    p_ref[h] = ps[h] + jnp.sum(gs_[h], axis=1, keepdims=True)
                return carry
            return step

        lax.fori_loop(0, nkb - n_diag, make_step(False), 0)
        lax.fori_loop(nkb - n_diag, nkb, make_step(True), 0)
        st_a[...] = dq_acc[...].astype(bf16)
        row0 = pl.multiple_of(i * TQ, TQ)
        put(st_a, row0, TQ, 0)
        put(st_b, row0, TQ, 3)

        @pl.when(i == n_i - 1)
        def _():
            st_k[...] = dk_acc[...].astype(bf16)
            st_v[...] = dv_acc[...].astype(bf16)
            put(st_k, 0, S, 1)
            put(st_v, 0, S, 2)

    qspec = lambda k: pl.BlockSpec((TQ, dh), lambda i: (i, cb + k))
    kspec = lambda k: pl.BlockSpec((S, dh), lambda i: (0, cb + k))
    return pl.pallas_call(
        body, name="sb_bwd", grid=(n_i,),
        in_specs=[qspec(0), kspec(1), kspec(2), qspec(3),
                  pl.BlockSpec((TQ, dh), lambda i: (i, 0)),
                  pl.BlockSpec((N_HEADS, TQ, TK), lambda i: (0, i, 0)),
                  pl.BlockSpec((TQ, dh), lambda i: (i, A_GROUPS * HEAD_DIM // dh)),
                  pl.BlockSpec(memory_space=pl.ANY), pl.BlockSpec(memory_space=pl.ANY)],
        out_specs=pl.BlockSpec(memory_space=pl.ANY),
        out_shape=SDS(dproj_prev.shape, bf16),
        input_output_aliases={7: 0},
        scratch_shapes=[pltpu.VMEM((TQ, dh), bf16), pltpu.VMEM((S, dh), bf16), pltpu.VMEM((S, dh), bf16),
                        pltpu.VMEM((TQ, dh), bf16), pltpu.VMEM((N_HEADS, TQ, TK), f32), pltpu.VMEM((TQ, dh), f32),
                        pltpu.VMEM((S, dh), f32), pltpu.VMEM((S, dh), f32),
                        pltpu.VMEM((TQ, dh), bf16), pltpu.VMEM((TQ, dh), bf16),
                        pltpu.VMEM((S, dh), bf16), pltpu.VMEM((S, dh), bf16)],
        compiler_params=_params(56),
    )(proj, proj, proj, proj, o, car, dy, dproj_prev, after)


def _mem_kv(mem, mg_row, w_kv_g):
    M, D = mem.shape
    rb, n = w_kv_g.shape[1], w_kv_g.shape[2]

    def body(m_ref, g_ref, w_ref, kv_ref):
        mv = m_ref[...]
        r = lax.rsqrt(jnp.mean(mv * mv, axis=-1, keepdims=True) + EPS)
        mh = (mv * r * g_ref[...]).astype(bf16)
        kv_ref[...] = _dot(mh, w_ref[...].reshape(N_DEV * rb, n))

    return pl.pallas_call(
        body, name="mem_kv", grid=(1,),
        in_specs=[pl.BlockSpec((M, D), lambda i: (0, 0)), pl.BlockSpec((1, D), lambda i: (0, 0)),
                  pl.BlockSpec((N_DEV, rb, n), lambda i: (0, 0, 0))],
        out_specs=pl.BlockSpec((M, n), lambda i: (0, 0)),
        out_shape=SDS((M, n), f32), compiler_params=_params(),
    )(mem, mg_row, w_kv_g)


def _xattn_head(q_ref, kv_ref, qg, kg, h):
    dc = N_HEADS * HEAD_DIM
    sl = slice(h * HEAD_DIM, (h + 1) * HEAD_DIM)
    qh = q_ref[:, sl]
    rq = lax.rsqrt(jnp.mean(qh * qh, axis=-1, keepdims=True) + EPS)
    qhat = qh * rq
    qn = (qhat * qg).astype(bf16)
    kh = kv_ref[:, sl]
    rk = lax.rsqrt(jnp.mean(kh * kh, axis=-1, keepdims=True) + EPS)
    kn = (kh * rk * kg).astype(bf16)
    vh = kv_ref[:, dc + h * HEAD_DIM:dc + (h + 1) * HEAD_DIM].astype(bf16)
    s = _dot(qn, kn, NT) * (1.0 / math.sqrt(HEAD_DIM))
    e = jnp.exp(s - jnp.max(s, axis=-1, keepdims=True))
    p = e / jnp.sum(e, axis=-1, keepdims=True)
    o = _dot(p.astype(bf16), vh)
    return sl, rq, qhat, qn, kn, vh, p, o


def _xattn_fwd(proj, kv, qg_row, kg_row, y_prev, col0, tq=512):
    S = proj.shape[0]
    D = y_prev.shape[1]
    dc = N_HEADS * HEAD_DIM
    M = kv.shape[0]

    def body(q_ref, z_ref, kv_ref, qg_ref, kg_ref, yp_ref, y_ref):
        for h in range(N_HEADS):
            sl, _, _, _, _, _, _, o = _xattn_head(q_ref, kv_ref, qg_ref[...], kg_ref[...], h)
            z = z_ref[:, sl]
            y_ref[:, sl] = (o * (z * jax.nn.sigmoid(z))).astype(bf16)

    full = lambda shp: pl.BlockSpec(shp, lambda i: (0,) * len(shp))
    return pl.pallas_call(
        body, name="xattn_fwd", grid=(S // tq,),
        in_specs=[pl.BlockSpec((tq, dc), lambda i: (i, col0)), pl.BlockSpec((tq, dc), lambda i: (i, col0 + 1)),
                  full((M, 2 * dc)), full((1, HEAD_DIM)), full((1, HEAD_DIM)), pl.BlockSpec(memory_space=pl.ANY)],
        out_specs=pl.BlockSpec((tq, dc), lambda i: (i, D // dc - 1)),
        out_shape=SDS((S, D), bf16), input_output_aliases={5: 0}, compiler_params=_params(),
    )(proj, proj, kv, qg_row, kg_row, y_prev)


def _xattn_bwd(proj, kv, qg_row, kg_row, dy, dproj_prev, col0, tq=512):
    S = proj.shape[0]
    D = dy.shape[1]
    dc = N_HEADS * HEAD_DIM
    M = kv.shape[0]

    def body(q_ref, z_ref, kv_ref, qg_ref, kg_ref, dy_ref, dpp_ref, dp_ref, dkn_ref, dv_ref, dqg_ref):
        @pl.when(pl.program_id(0) == 0)
        def _():
            dkn_ref[...] = jnp.zeros_like(dkn_ref)
            dv_ref[...] = jnp.zeros_like(dv_ref)
            dqg_ref[...] = jnp.zeros_like(dqg_ref)

        qg = qg_ref[...]
        for h in range(N_HEADS):
            sl, rq, qhat, qn, kn, vh, p, o = _xattn_head(q_ref, kv_ref, qg, kg_ref[...], h)
            s, gs = _silu_and_grad(z_ref[:, sl])
            dyh = dy_ref[:, sl]
            dp_ref[:, dc + h * HEAD_DIM:dc + (h + 1) * HEAD_DIM] = (dyh * o * gs).astype(bf16)
            dob = (dyh * s).astype(bf16)
            dpr = _dot(dob, vh, NT)
            dv_ref[:, sl] += _dot(p.astype(bf16), dob, TN)
            ds = (p * (dpr - jnp.sum(p * dpr, axis=-1, keepdims=True)) * (1.0 / math.sqrt(HEAD_DIM))).astype(bf16)
            dqn = _dot(ds, kn)
            dkn_ref[:, sl] += _dot(ds, qn, TN)
            dqg_ref[...] += jnp.sum(dqn * qhat, axis=0, keepdims=True)
            dqhat = dqn * qg
            dp_ref[:, sl] = (rq * (dqhat - qhat * jnp.mean(dqhat * qhat, axis=-1, keepdims=True))).astype(bf16)

    full = lambda shp: pl.BlockSpec(shp, lambda i: (0,) * len(shp))
    return pl.pallas_call(
        body, name="xattn_bwd", grid=(S // tq,),
        in_specs=[pl.BlockSpec((tq, dc), lambda i: (i, col0)), pl.BlockSpec((tq, dc), lambda i: (i, col0 + 1)),
                  full((M, 2 * dc)), full((1, HEAD_DIM)), full((1, HEAD_DIM)),
                  pl.BlockSpec((tq, dc), lambda i: (i, D // dc - 1)), pl.BlockSpec(memory_space=pl.ANY)],
        out_specs=[pl.BlockSpec((tq, 2 * dc), lambda i: (i, col0 // 2)), full((M, dc)), full((M, dc)),
                   full((1, HEAD_DIM))],
        out_shape=[SDS(dproj_prev.shape, bf16), SDS((M, dc), f32), SDS((M, dc), f32), SDS((1, HEAD_DIM), f32)],
        input_output_aliases={6: 0}, compiler_params=_params(),
    )(proj, proj, kv, qg_row, kg_row, dy, dproj_prev)


def _mem_bwd(mem, mg_row, kv, dkn, dv, kg_row, w_kv_g):
    M, D = mem.shape
    rb, n = w_kv_g.shape[1], w_kv_g.shape[2]
    dc = n // 2

    def body(m_ref, g_ref, kv_ref, dkn_ref, dv_ref, kg_ref, w_ref, dw_ref, dmg_ref, dkg_ref, dkv_ref):
        mv = m_ref[...]
        r = lax.rsqrt(jnp.mean(mv * mv, axis=-1, keepdims=True) + EPS)
        mhat = mv * r
        mh = (mhat * g_ref[...]).astype(bf16)
        kg = kg_ref[...]
        dkg = jnp.zeros((1, HEAD_DIM), f32)
        for h in range(N_HEADS):
            sl = slice(h * HEAD_DIM, (h + 1) * HEAD_DIM)
            kh = kv_ref[:, sl]
            rk = lax.rsqrt(jnp.mean(kh * kh, axis=-1, keepdims=True) + EPS)
            khat = kh * rk
            dkn_h = dkn_ref[:, sl]
            dkg = dkg + jnp.sum(dkn_h * khat, axis=0, keepdims=True)
            dkhat = dkn_h * kg
            dkv_ref[:, sl] = (rk * (dkhat - khat * jnp.mean(dkhat * khat, axis=-1, keepdims=True))).astype(bf16)
        dkv_ref[:, dc:] = dv_ref[...].astype(bf16)
        dkg_ref[...] = dkg
        dkv = dkv_ref[...]
        dw_ref[...] = _dot(mh, dkv, TN).astype(bf16).reshape(N_DEV, rb, n)
        dmh = _dot(dkv, w_ref[...].reshape(N_DEV * rb, n), NT)
        dmg_ref[...] = jnp.sum(dmh * mhat, axis=0, keepdims=True)

    full = lambda shp: pl.BlockSpec(shp, lambda i: (0,) * len(shp))
    wspec = full((N_DEV, rb, n))
    return pl.pallas_call(
        body, name="mem_bwd", grid=(1,),
        in_specs=[full((M, D)), full((1, D)), full((M, n)), full((M, dc)), full((M, dc)), full((1, HEAD_DIM)), wspec],
        out_specs=[wspec, full((1, D)), full((1, HEAD_DIM))],
        out_shape=[SDS((N_DEV, rb, n), bf16), SDS((1, D), f32), SDS((1, HEAD_DIM), f32)],
        scratch_shapes=[pltpu.VMEM((M, n), bf16)], compiler_params=_params(),
    )(mem, mg_row, kv, dkn, dv, kg_row, w_kv_g)


SMALL = ("norm_g", "sgu_ln_g", "sgu_ln_b", "sgu_w", "sgu_b", "mem_norm_g", "q_norm_g", "k_norm_g")


def _pack_small(parts):
    flat = jnp.concatenate([parts[n].reshape(-1) for n in SMALL])
    pad = (-flat.shape[0]) % (8 * 128)
    return jnp.pad(flat, (0, pad)).reshape(-1, 128)


def _unpack_small(packed, like):
    flat = packed.reshape(-1)
    out, off = {}, 0
    for n in SMALL:
        size = math.prod(like[n].shape)
        out[n] = flat[off:off + size].reshape(like[n].shape)
        off += size
    return out


def kernel(x, mem, norm_g, w_in, sgu_ln_g, sgu_ln_b, sgu_w, sgu_b, mem_norm_g, w_mem_kv, q_norm_g, k_norm_g, w_out, loss_target, m_norm_g, m_w_in, m_sgu_ln_g, m_sgu_ln_b, m_sgu_w, m_sgu_b, m_mem_norm_g, m_w_mem_kv, m_q_norm_g, m_k_norm_g, m_w_out, v_norm_g, v_w_in, v_sgu_ln_g, v_sgu_ln_b, v_sgu_w, v_sgu_b, v_mem_norm_g, v_w_mem_kv, v_q_norm_g, v_k_norm_g, v_w_out):
    L, D, wc = w_in.shape
    S = x.shape[1]
    da = D // 2
    xs = x.reshape(S, D)
    mems = mem.reshape(mem.shape[1], D)
    tgt = loss_target.reshape(S, D)
    row = lambda a, l: a[l].reshape(1, -1)
    tie = lambda a, tok: a + tok[0:1, 0:1]
    sb_col, xa_col = 3 * da // HEAD_DIM, (3 * da + D) // (D // 4)

    ax, ay, ac = lax.axis_index("x"), lax.axis_index("y"), lax.axis_index("c")
    ids = jnp.stack([4 * ax + 2 * ay + ac, 2 * ax + ay, ac]).astype(jnp.int32)
    w_b = [(_cast_into_slot("cast_w_in", w_in, l, 512, ids), _cast_into_slot("cast_w_kv", w_mem_kv, l, 256, ids),
            _cast_into_slot("cast_w_out", w_out, l, 256, ids)) for l in range(L)]
    first = _gather2_start("gather_w_in0", [w_b[0][0]], ids)
    in_fwd = _gather2_forward("gather_w_in0_forward", _split_wait(first, first["token"]), ids)

    acts = []
    xl = xs
    for l in range(L):
        g_row = row(norm_g, l)
        (w_in_g,) = _split_wait(in_fwd, g_row if l else in_fwd["token"])
        rest = _gather2_start(f"gather_w_rest{l}", [w_b[l][1], w_b[l][2]], w_in_g)
        proj, h = _rms_proj(xl, tie(g_row, rest["token"]), w_in_g)
        rest_fwd = _gather2_forward(f"gather_w_rest{l}_forward", _split_wait(rest, proj), proj)
        lg_row = tie(row(sgu_ln_g, l), rest_fwd["token"])
        if l + 1 < L:
            nxt = _gather2_start(f"gather_w_in{l + 1}", [w_b[l + 1][0]], proj)
            lg_row = tie(lg_row, nxt["token"])
        y = _sgu_fwd(proj, lg_row, row(sgu_ln_b, l), sgu_w[l], sgu_b[l].T)
        y, o_b, car = _sb_fwd(proj, y, sb_col, lg_row)
        w_kv_g, w_out_g = _split_wait(rest_fwd, o_b)
        kv = _mem_kv(mems, row(mem_norm_g, l), w_kv_g)
        y = _xattn_fwd(proj, kv, row(q_norm_g, l), row(k_norm_g, l), y, xa_col)
        order = kv
        if l + 1 < L:
            in_fwd = _gather2_forward(f"gather_w_in{l + 1}_forward", _split_wait(nxt, y), y)
            order = in_fwd["token"]
        x_next = _out_proj(xl, y, w_out_g, order)
        acts.append((xl, proj, h, y, o_b, car, kv, w_in_g, w_kv_g, w_out_g))
        xl = x_next

    dx, dxb, loss_part = _loss_and_grad(xl, tgt, 512)
    loss = lax.psum(loss_part[0, 0], ("x", "y", "c"))

    weights = dict(norm_g=norm_g, sgu_ln_g=sgu_ln_g, sgu_ln_b=sgu_ln_b, sgu_w=sgu_w, sgu_b=sgu_b,
                   mem_norm_g=mem_norm_g, q_norm_g=q_norm_g, k_norm_g=k_norm_g)
    moms_m = dict(norm_g=m_norm_g, sgu_ln_g=m_sgu_ln_g, sgu_ln_b=m_sgu_ln_b, sgu_w=m_sgu_w, sgu_b=m_sgu_b,
                  mem_norm_g=m_mem_norm_g, q_norm_g=m_q_norm_g, k_norm_g=m_k_norm_g)
    moms_v = dict(norm_g=v_norm_g, sgu_ln_g=v_sgu_ln_g, sgu_ln_b=v_sgu_ln_b, sgu_w=v_sgu_w, sgu_b=v_sgu_b,
                  mem_norm_g=v_mem_norm_g, q_norm_g=v_q_norm_g, k_norm_g=v_k_norm_g)
    head_rows = 16
    assert D // 128 <= head_rows

    seconds = {}
    pending = None
    small = {n: [None] * L for n in SMALL}
    for l in reversed(range(L)):
        xl, proj, h, y, o_b, car, kv, w_in_g, w_kv_g, w_out_g = acts[l]
        dy = _out_bwd_dy(dxb, w_out_g)
        order = dy
        if pending is not None:
            seconds[pending[0]] = _scatter2_second_level(pending[0], pending[1], dy, ids)
            order = seconds[pending[0]]["token"]
        g_out = _tn_grad("out_bwd_dw", y, dxb, D // N_DEV, 512, True, order)
        first = _scatter2_pair_start(f"scatter_g_out{l}_pair", [g_out], ids)
        dproj, d_sw, d_sb, d_lg, d_lb = _sgu_bwd(proj, dy, tie(row(sgu_ln_g, l), first["token"]), row(sgu_ln_b, l),
                                                 sgu_w[l], sgu_b[l].T)
        seconds[f"g_out{l}"] = _scatter2_second_level(f"g_out{l}", first, dproj, ids)
        dproj = _sb_bwd(proj, o_b, car, dy, dproj, sb_col, seconds[f"g_out{l}"]["token"])
        dproj, dkn, dv, d_qg = _xattn_bwd(proj, kv, row(q_norm_g, l), row(k_norm_g, l), dy, dproj, xa_col)
        g_kv, d_mg, d_kg = _mem_bwd(mems, row(mem_norm_g, l), kv, dkn, dv, row(k_norm_g, l), w_kv_g)
        for n, val in (("sgu_ln_g", d_lg), ("sgu_ln_b", d_lb), ("sgu_w", d_sw), ("sgu_b", d_sb[:, :A_GROUPS].T),
                       ("mem_norm_g", d_mg), ("q_norm_g", d_qg), ("k_norm_g", d_kg)):
            small[n][l] = val.reshape(-1)
        order = d_kg
        if l == 0:
            small["norm_g"][0] = jnp.zeros_like(small["norm_g"][1])
            part = _pack_small({n: jnp.stack(small[n]).reshape(weights[n].shape) for n in SMALL})
            tail = _gather2_start("gather_small_tail", [_into_slot("small_tail_slot", part[head_rows:], ids)], ids)
            order = tail["token"]
        g_in_l = _tn_grad("in_bwd_dw", h, dproj, 512, wc, False, order)
        first = _scatter2_pair_start(f"scatter_g_rest{l}_pair", [g_kv, g_in_l], ids)
        g_row = tie(row(norm_g, l), first["token"])
        if l == 0:
            tail_fwd = _gather2_forward("gather_small_tail_forward", _split_wait(tail, g_in_l), g_in_l)
            g_row = tie(g_row, tail_fwd["token"])
        dx, dxb, d_ng = _in_bwd_dx(dproj, w_in_g, xl, g_row, dx)
        small["norm_g"][l] = d_ng.reshape(-1)
        pending = (f"g_rest{l}", first)
    head_part = jnp.concatenate([small["norm_g"][0].reshape(-1, 128), jnp.zeros((head_rows - D // 128, 128), f32)])
    head = _gather2_start("gather_small_head", [_into_slot("small_head_slot", head_part, ids)], ids)
    seconds[pending[0]] = _scatter2_second_level(pending[0], pending[1], head["token"], ids)
    head_fwd = _gather2_forward("gather_small_head_forward", _split_wait(head, seconds[pending[0]]["token"]), ids)

    (r_tail,) = _split_wait(tail_fwd, dx)
    (r_head,) = _split_wait(head_fwd, r_tail)
    sm = _small_sum_adam(r_head, r_tail, _pack_small(weights), _pack_small(moms_m), _pack_small(moms_v))

    a_out = a_kv = a_in = None
    order = sm[0]
    for l in reversed(range(L)):
        (r_out,) = _scatter2_finish(seconds[f"g_out{l}"], order)
        a_out = _sum_adam("adam_w_out", r_out, w_out, m_w_out, v_w_out, l, a_out, 128, ids)
        r_kv, r_in = _scatter2_finish(seconds[f"g_rest{l}"], a_out[0])
        a_kv = _sum_adam("adam_w_kv", r_kv, w_mem_kv, m_w_mem_kv, v_w_mem_kv, l, a_kv, 256, ids)
        a_in = _sum_adam("adam_w_in", r_in, w_in, m_w_in, v_w_in, l, a_in, 256, ids)
        order = a_in[0]
    res = {"w_out": a_out, "w_mem_kv": a_kv, "w_in": a_in}
    sm = [_unpack_small(a, weights) for a in sm]
    for n in SMALL:
        res[n] = [a[n] for a in sm]

    order = ("norm_g", "w_in", "sgu_ln_g", "sgu_ln_b", "sgu_w", "sgu_b", "mem_norm_g", "w_mem_kv", "q_norm_g",
             "k_norm_g", "w_out")
    outs = [loss, dx.reshape(x.shape)]
    for k in range(4):
        outs += [res[n][k] for n in order]
    return tuple(outs)
```

```python
import functools
import math

import jax
import jax.numpy as jnp
from jax import lax
from jax.experimental import pallas as pl
from jax.experimental.pallas import tpu as pltpu

f32 = jnp.float32
bf16 = jnp.bfloat16
SDS = jax.ShapeDtypeStruct

N_DEV = 8
EPS = 1e-6
CHUNK = 128
A_GROUPS = 8
HEAD_DIM = 128
N_HEADS = 4
TQ = 256
TK = 128
ADAM_LR, ADAM_B1, ADAM_B2, ADAM_EPS, ADAM_WD, ADAM_STEP = 0.001, 0.9, 0.999, 1e-08, 0.01, 10
MIB = 1024 * 1024

NT = (((1,), (1,)), ((), ()))
TN = (((0,), (0,)), ((), ()))


def _params(vmem_mib=48):
    return pltpu.CompilerParams(vmem_limit_bytes=vmem_mib * MIB)


def _gelu_and_grad(x):
    e = lax.erf(x * (1.0 / math.sqrt(2.0)))
    cdf = 0.5 * (1.0 + e)
    pdf = jnp.exp(-0.5 * x * x) * (1.0 / math.sqrt(2.0 * math.pi))
    return x * cdf, cdf + x * pdf


def _gelu(x):
    return 0.5 * x * (1.0 + lax.erf(x * (1.0 / math.sqrt(2.0))))


def _silu_and_grad(z):
    sg = jax.nn.sigmoid(z)
    return z * sg, sg * (1.0 + z * (1.0 - sg))


def _dot(a, b, dims=None):
    if dims is None:
        return jnp.dot(a, b, preferred_element_type=f32)
    return lax.dot_general(a, b, dims, preferred_element_type=f32)


_HBM = pl.BlockSpec(memory_space=pltpu.HBM)
_SEM = pl.BlockSpec(memory_space=pltpu.SEMAPHORE)
_EFFECT = pltpu.SideEffectType.DATAFLOW_SIDE_EFFECTING


def _split_start(name, bufs, n_remote, n_local, build, after):
    nb = len(bufs)

    def body(*refs):
        token = refs[-1]
        locals_, remotes = build(refs[:nb], *refs[nb + 1:nb + 4])
        for cp in locals_ + remotes:
            cp.start()
        token[...] = jnp.zeros_like(token)

    hbm = lambda a: pltpu.with_memory_space_constraint(a, pltpu.HBM)
    outs = pl.pallas_call(
        body, name=name,
        out_shape=(pltpu.SemaphoreType.DMA((n_remote,)), pltpu.SemaphoreType.DMA((n_remote,)),
                   pltpu.SemaphoreType.DMA((max(n_local, 1),)),
                   *[pltpu.HBM(b.shape, b.dtype) for b in bufs], SDS((8, 128), f32)),
        in_specs=[_HBM] * nb + [pl.BlockSpec(memory_space=pl.ANY)],
        out_specs=(_SEM, _SEM, _SEM, *[_HBM] * nb, pl.BlockSpec(memory_space=pltpu.VMEM)),
        input_output_aliases={k: 3 + k for k in range(nb)},
        compiler_params=pltpu.CompilerParams(has_side_effects=_EFFECT),
    )(*[hbm(b) for b in bufs], after)
    return dict(name=name, build=build, sems=outs[:3], bufs=outs[3:3 + nb], token=outs[-1])


def _split_wait(handle, after):
    build, bufs = handle["build"], handle["bufs"]
    nb = len(bufs)

    def body(*refs):
        locals_, remotes = build(refs[:nb], *refs[nb:nb + 3])
        for cp in remotes:
            cp.wait_recv()
        for cp in remotes:
            cp.wait_send()
        for cp in locals_:
            cp.wait()

    outs = pl.pallas_call(
        body, name=handle["name"] + "_wait",
        out_shape=tuple(pltpu.HBM(b.shape, b.dtype) for b in bufs),
        in_specs=[_HBM] * nb + [_SEM] * 3 + [pl.BlockSpec(memory_space=pl.ANY)],
        out_specs=tuple([_HBM] * nb),
        input_output_aliases={k: k for k in range(nb)},
        compiler_params=pltpu.CompilerParams(has_side_effects=_EFFECT),
    )(*bufs, *handle["sems"], after)
    return list(outs)


def _remote(src, dst, send_sems, recv_sems, k, to):
    return pltpu.make_async_remote_copy(src_ref=src, dst_ref=dst, send_sem=send_sems.at[k], recv_sem=recv_sems.at[k],
                                        device_id=to, device_id_type=pl.DeviceIdType.MESH)


def _other_chips(x, y):
    return [(1 - x, y), (x, 1 - y), (1 - x, 1 - y)]


def _gather2_start(name, lands, after):
    def build(refs, send, recv, loc):
        x, y, c = lax.axis_index("x"), lax.axis_index("y"), lax.axis_index("c")
        me = 4 * x + 2 * y + c
        remotes = []
        for a, d in enumerate(refs):
            remotes.append(_remote(d.at[me], d.at[me], send, recv, 4 * a, (x, y, 1 - c)))
            remotes += [_remote(d.at[me], d.at[me], send, recv, 4 * a + 1 + k, (px, py, c))
                        for k, (px, py) in enumerate(_other_chips(x, y))]
        return [], remotes

    return _split_start(name, list(lands), 4 * len(lands), 0, build, after)


def _gather2_forward(name, lands, after):
    n = len(lands)

    def build(refs, send, recv, loc):
        x, y, c = lax.axis_index("x"), lax.axis_index("y"), lax.axis_index("c")
        slots = [4 * px + 2 * py + c for px, py in _other_chips(x, y)]
        return [], [_remote(d.at[sl], d.at[sl], send, recv, 3 * a + k, (x, y, 1 - c))
                    for a, d in enumerate(refs) for k, sl in enumerate(slots)]

    return _split_start(name, list(lands), 3 * n, 0, build, after)


def _scatter2_pair_start(name, srcs, after):
    n = len(srcs)

    def build(refs, send, recv, loc):
        x, y, c = lax.axis_index("x"), lax.axis_index("y"), lax.axis_index("c")
        return [], [_remote(refs[a].at[2 * q + 1 - c], refs[n + a].at[q], send, recv, 4 * a + q, (x, y, 1 - c))
                    for a in range(n) for q in range(4)]

    lands = [lax.empty((4,) + s.shape[1:], s.dtype) for s in srcs]
    return _split_start(name, list(srcs) + lands, 4 * n, 0, build, after)


def _scatter2_chip_start(name, pairs, after):
    n = len(pairs)

    def build(refs, send, recv, loc):
        x, y, c = lax.axis_index("x"), lax.axis_index("y"), lax.axis_index("c")
        return [], [_remote(refs[a].at[2 * px + py], refs[n + a].at[2 * x + y], send, recv, 3 * a + k, (px, py, c))
                    for a in range(n) for k, (px, py) in enumerate(_other_chips(x, y))]

    return _split_start(name, list(pairs) + [lax.empty(p.shape, p.dtype) for p in pairs], 3 * n, 0, build, after)


def _pair_sum(name, src, theirs, ids):
    _, R, C = theirs.shape
    tr = min(R, 1024)

    def body(ids_ref, a_ref, b_ref, o_ref):
        o_ref[...] = (a_ref[...].astype(f32) + b_ref[...].astype(f32)).astype(bf16)

    spec = pl.BlockSpec((None, tr, C), lambda q, i, ids: (q, i, 0))
    return pl.pallas_call(
        body, name=name,
        grid_spec=pltpu.PrefetchScalarGridSpec(
            num_scalar_prefetch=1, grid=(4, R // tr),
            in_specs=[pl.BlockSpec((None, tr, C), lambda q, i, ids: (2 * q + ids[2], i, 0)), spec], out_specs=spec),
        out_shape=SDS(theirs.shape, bf16), compiler_params=_params(),
    )(ids, src, theirs)


def _scatter2_second_level(name, first, after, ids):
    outs = _split_wait(first, after)
    n = len(outs) // 2
    pairs = [_pair_sum(f"pair_sum_{name}{a}", outs[a], outs[n + a], ids) for a in range(n)]
    return _scatter2_chip_start(f"scatter_{name}_chip", pairs, ids)


def _scatter2_finish(second, after):
    outs = _split_wait(second, after)
    n = len(outs) // 2
    return [(outs[a], outs[n + a]) for a in range(n)]


def _cast_into_slot(name, w, l, tr, ids):
    _, R, C = w.shape

    def body(ids_ref, w_ref, o_ref):
        o_ref[...] = w_ref[...].astype(bf16)

    return pl.pallas_call(
        body, name=name,
        grid_spec=pltpu.PrefetchScalarGridSpec(
            num_scalar_prefetch=1, grid=(R // tr,),
            in_specs=[pl.BlockSpec((None, tr, C), lambda i, ids: (l, i, 0))],
            out_specs=pl.BlockSpec((None, tr, C), lambda i, ids: (ids[0], i, 0))),
        out_shape=SDS((N_DEV, R, C), bf16), compiler_params=_params(),
    )(ids, w)


def _into_slot(name, a, ids):
    R, C = a.shape

    def body(ids_ref, a_ref, o_ref):
        o_ref[...] = a_ref[...]

    return pl.pallas_call(
        body, name=name,
        grid_spec=pltpu.PrefetchScalarGridSpec(
            num_scalar_prefetch=1, grid=(1,),
            in_specs=[pl.BlockSpec((R, C), lambda i, ids: (0, 0))],
            out_specs=pl.BlockSpec((None, R, C), lambda i, ids: (ids[0], 0, 0))),
        out_shape=SDS((N_DEV, R, C), f32), compiler_params=_params(),
    )(ids, a)


def _adam_math(w, g, m, v):
    m2 = ADAM_B1 * m + (1.0 - ADAM_B1) * g
    v2 = ADAM_B2 * v + (1.0 - ADAM_B2) * (g * g)
    m_hat = m2 / (1.0 - ADAM_B1 ** ADAM_STEP)
    v_hat = v2 / (1.0 - ADAM_B2 ** ADAM_STEP)
    delta = -ADAM_LR * (m_hat / (jnp.sqrt(v_hat) + ADAM_EPS) + ADAM_WD * w)
    return delta, m2, v2


def _sum_adam(name, pair_recv, w, m, v, l, prev, tr, ids):
    own, recv = pair_recv
    L, R, C = w.shape
    slots = recv.shape[0]

    def body(ids_ref, r_ref, own_ref, w_ref, m_ref, v_ref, *rest):
        g_ref, d_ref, m2_ref, v2_ref = rest[-4:]
        terms = [jnp.where(ids_ref[1] == q, own_ref[...], r_ref[q]).astype(f32) for q in range(slots)]
        g = terms[0]
        for t in terms[1:]:
            g = g + t
        d, m2, v2 = _adam_math(w_ref[...], g, m_ref[...], v_ref[...])
        g_ref[...] = g
        d_ref[...] = d
        m2_ref[...] = m2
        v2_ref[...] = v2

    wspec = pl.BlockSpec((None, tr, C), lambda i, ids: (l, i, 0))
    in_specs = [pl.BlockSpec((slots, tr, C), lambda i, ids: (0, i, 0)),
                pl.BlockSpec((None, tr, C), lambda i, ids: (ids[1], i, 0)), wspec, wspec, wspec]
    args = [ids, recv, own, w, m, v]
    aliases = {}
    if prev is not None:
        in_specs += [pl.BlockSpec(memory_space=pl.ANY)] * 4
        args += list(prev)
        aliases = {6 + k: k for k in range(4)}
    return pl.pallas_call(
        body, name=name,
        grid_spec=pltpu.PrefetchScalarGridSpec(num_scalar_prefetch=1, grid=(R // tr,), in_specs=in_specs,
                                               out_specs=[wspec] * 4),
        out_shape=[SDS((L, R, C), f32)] * 4, input_output_aliases=aliases, compiler_params=_params(),
    )(*args)


def _small_sum_adam(recv_head, recv_tail, w, m, v):
    r0 = recv_head.shape[1]
    R, C = w.shape

    def body(rh_ref, rt_ref, w_ref, m_ref, v_ref, g_ref, d_ref, m2_ref, v2_ref):
        for r_ref, rows in ((rh_ref, slice(0, r0)), (rt_ref, slice(r0, R))):
            g = r_ref[0]
            for s in range(1, N_DEV):
                g = g + r_ref[s]
            d, m2, v2 = _adam_math(w_ref[rows, :], g, m_ref[rows, :], v_ref[rows, :])
            g_ref[rows, :] = g
            d_ref[rows, :] = d
            m2_ref[rows, :] = m2
            v2_ref[rows, :] = v2

    return pl.pallas_call(
        body, name="small_sum_adam", out_shape=[SDS((R, C), f32)] * 4, compiler_params=_params(),
    )(recv_head, recv_tail, w, m, v)


def _loss_and_grad(xf, tgt, tm):
    S, D = xf.shape

    def body(x_ref, t_ref, dx_ref, dxb_ref, l_ref):
        i = pl.program_id(0)
        d = x_ref[...] - t_ref[...]
        dx = d * (1.0 / D)
        dx_ref[...] = dx
        dxb_ref[...] = dx.astype(bf16)
        e = d * d
        part = e[:, 0:128]
        for k in range(1, D // 128):
            part = part + e[:, k * 128:(k + 1) * 128]
        part = jnp.sum(part.reshape(tm // 8, 8, 128), axis=0)

        @pl.when(i == 0)
        def _():
            l_ref[...] = jnp.zeros_like(l_ref)

        l_ref[...] += part

        @pl.when(i == pl.num_programs(0) - 1)
        def _():
            tot = jnp.sum(l_ref[...], axis=1, keepdims=True)
            tot = jnp.sum(tot, axis=0, keepdims=True)
            l_ref[...] = jnp.broadcast_to(tot * (0.5 / D), l_ref.shape)

    row = pl.BlockSpec((tm, D), lambda i: (i, 0))
    return pl.pallas_call(
        body, name="loss_grad", grid=(S // tm,),
        in_specs=[row, row], out_specs=[row, row, pl.BlockSpec((8, 128), lambda i: (0, 0))],
        out_shape=[SDS((S, D), f32), SDS((S, D), bf16), SDS((8, 128), f32)], compiler_params=_params(),
    )(xf, tgt)


def _rms_proj(x, g_row, w_in_g, tm=512):
    S, D = x.shape
    wc = w_in_g.shape[2]
    n_out = N_DEV * wc

    def body(x_ref, g_ref, w_ref, proj_ref, h_ref):
        @pl.when(pl.program_id(1) == 0)
        def _():
            xv = x_ref[...]
            r = lax.rsqrt(jnp.mean(xv * xv, axis=-1, keepdims=True) + EPS)
            h_ref[...] = (xv * r * g_ref[...]).astype(bf16)

        proj_ref[...] = _dot(h_ref[...], w_ref[...])

    return pl.pallas_call(
        body, name="rms_proj", grid=(S // tm, N_DEV),
        in_specs=[pl.BlockSpec((tm, D), lambda i, j: (i, 0)),
                  pl.BlockSpec((1, D), lambda i, j: (0, 0)),
                  pl.BlockSpec((None, D, wc), lambda i, j: (j, 0, 0))],
        out_specs=[pl.BlockSpec((tm, wc), lambda i, j: (i, j)), pl.BlockSpec((tm, D), lambda i, j: (i, 0))],
        out_shape=[SDS((S, n_out), f32), SDS((S, D), bf16)], compiler_params=_params(),
    )(x, g_row, w_in_g)


def _out_proj(x, y, w_out_g, after, tm=512):
    S, D = x.shape
    rb = w_out_g.shape[1]

    def body(x_ref, y_ref, w_ref, after_ref, o_ref):
        w = w_ref[...].reshape(N_DEV * rb, D)
        o_ref[...] = x_ref[...] + _dot(y_ref[...], w)

    row = pl.BlockSpec((tm, D), lambda i: (i, 0))
    return pl.pallas_call(
        body, name="out_proj", grid=(S // tm,),
        in_specs=[row, row, pl.BlockSpec((N_DEV, rb, D), lambda i: (0, 0, 0)), pl.BlockSpec(memory_space=pl.ANY)],
        out_specs=row, out_shape=SDS((S, D), f32), compiler_params=_params(),
    )(x, y, w_out_g, after)


def _out_bwd_dy(dxb, w_out_g, tm=512):
    S, D = dxb.shape
    rb = w_out_g.shape[1]

    nb = 2

    def body(dx_ref, w_ref, o_ref):
        o_ref[...] = _dot(dx_ref[...], w_ref[...].reshape(nb * rb, D), NT)

    return pl.pallas_call(
        body, name="out_bwd_dy", grid=(S // tm, N_DEV // nb),
        in_specs=[pl.BlockSpec((tm, D), lambda i, j: (i, 0)),
                  pl.BlockSpec((nb, rb, D), lambda i, j: (j, 0, 0))],
        out_specs=pl.BlockSpec((tm, nb * rb), lambda i, j: (i, j)),
        out_shape=SDS((S, D), f32), compiler_params=_params(),
    )(dxb, w_out_g)


def _tn_grad(name, a, b, tm, tn, rows_major, after):
    S, M = a.shape
    N = b.shape[1]
    if rows_major:
        rb = M // N_DEV
        nb = tm // rb
        out_shape = SDS((N_DEV, rb, N), bf16)
        out_spec = pl.BlockSpec((nb, rb, tn), lambda i, j: (i, 0, j))
    else:
        out_shape = SDS((N_DEV, M, N // N_DEV), bf16)
        assert tn == N // N_DEV
        out_spec = pl.BlockSpec((None, tm, tn), lambda i, j: (j, i, 0))

    def body(a_ref, b_ref, after_ref, o_ref):
        o_ref[...] = _dot(a_ref[...], b_ref[...], TN).astype(bf16).reshape(o_ref.shape)

    return pl.pallas_call(
        body, name=name, grid=(M // tm, N // tn),
        in_specs=[pl.BlockSpec((S, tm), lambda i, j: (0, i)), pl.BlockSpec((S, tn), lambda i, j: (0, j)),
                  pl.BlockSpec(memory_space=pl.ANY)],
        out_specs=out_spec, out_shape=out_shape, compiler_params=_params(),
    )(a, b, after)


def _in_bwd_dx(dproj, w_in_g, x, g_row, dx_next, tm=512, tn=256):
    S, D = x.shape
    wc = w_in_g.shape[2]
    n_j = D // tn

    def body(dp_ref, w_ref, x_ref, g_ref, dxn_ref, dx_ref, dxb_ref, dg_ref, dh_ref):
        i, j = pl.program_id(0), pl.program_id(1)
        acc = _dot(dp_ref[:, 0:wc], w_ref[0], NT)
        for k in range(1, N_DEV):
            acc = acc + _dot(dp_ref[:, k * wc:(k + 1) * wc], w_ref[k], NT)
        dh_ref[j] = acc

        @pl.when(jnp.logical_and(i == 0, j == 0))
        def _():
            dg_ref[...] = jnp.zeros_like(dg_ref)

        @pl.when(j == n_j - 1)
        def _():
            dh = jnp.concatenate([dh_ref[t] for t in range(n_j)], axis=1)
            xv = x_ref[...]
            r = lax.rsqrt(jnp.mean(xv * xv, axis=-1, keepdims=True) + EPS)
            xhat = xv * r
            dxhat = dh * g_ref[...]
            dx = r * (dxhat - xhat * jnp.mean(dxhat * xhat, axis=-1, keepdims=True)) + dxn_ref[...]
            dx_ref[...] = dx
            dxb_ref[...] = dx.astype(bf16)
            dg_ref[...] += jnp.sum(dh * xhat, axis=0, keepdims=True)

    row = pl.BlockSpec((tm, D), lambda i, j: (i, 0))
    once = lambda shp: pl.BlockSpec(shp, lambda i, j: (i, 0), pipeline_mode=pl.Buffered(1))
    return pl.pallas_call(
        body, name="in_bwd_dx", grid=(S // tm, n_j),
        in_specs=[once((tm, N_DEV * wc)),
                  pl.BlockSpec((N_DEV, tn, wc), lambda i, j: (0, j, 0)),
                  once((tm, D)), pl.BlockSpec((1, D), lambda i, j: (0, 0)), once((tm, D))],
        out_specs=[row, row, pl.BlockSpec((1, D), lambda i, j: (0, 0))],
        out_shape=[SDS((S, D), f32), SDS((S, D), bf16), SDS((1, D), f32)],
        scratch_shapes=[pltpu.VMEM((n_j, tm, tn), f32)], compiler_params=_params(56),
    )(dproj, w_in_g, x, g_row, dx_next)


def _sgu_fwd(proj, ln_g, ln_b, w_s, b_t):
    S = proj.shape[0]
    da = A_GROUPS * HEAD_DIM
    D = 2 * da

    def body(u_ref, v_ref, z_ref, lg_ref, lb_ref, w_ref, bt_ref, y_ref):
        u = _gelu(u_ref[...])
        v = _gelu(v_ref[...])
        z = z_ref[...]
        mu = jnp.mean(v, axis=-1, keepdims=True)
        xc = v - mu
        rs = lax.rsqrt(jnp.mean(xc * xc, axis=-1, keepdims=True) + EPS)
        vn = (xc * rs * lg_ref[...] + lb_ref[...]).astype(bf16)
        gate = u * (z * jax.nn.sigmoid(z))
        tri = lax.broadcasted_iota(jnp.int32, (CHUNK, CHUNK), 0) >= lax.broadcasted_iota(jnp.int32, (CHUNK, CHUNK), 1)
        for g in range(A_GROUPS):
            sl = slice(g * HEAD_DIM, (g + 1) * HEAD_DIM)
            wm = jnp.where(tri, w_ref[g], 0.0).astype(bf16)
            mixed = _dot(wm, vn[:, sl]) + bt_ref[:, g:g + 1]
            y_ref[:, sl] = (gate[:, sl] * mixed).astype(bf16)

    blk = lambda cb: pl.BlockSpec((CHUNK, da), lambda c: (c, cb))
    full = lambda shp: pl.BlockSpec(shp, lambda c: (0,) * len(shp))
    return pl.pallas_call(
        body, name="sgu_fwd", grid=(S // CHUNK,),
        in_specs=[blk(0), blk(1), blk(2), full((1, da)), full((1, da)),
                  full((A_GROUPS, CHUNK, CHUNK)), full((CHUNK, A_GROUPS))],
        out_specs=blk(0), out_shape=SDS((S, D), bf16), compiler_params=_params(),
    )(proj, proj, proj, ln_g, ln_b, w_s, b_t)


def _sgu_bwd(proj, dy, ln_g, ln_b, w_s, b_t):
    S = proj.shape[0]
    da = A_GROUPS * HEAD_DIM
    n_proj = proj.shape[1]

    def body(u_ref, v_ref, z_ref, dy_ref, lg_ref, lb_ref, w_ref, bt_ref,
             dp_ref, dw_ref, db_ref, dlg_ref, dlb_ref, dvn_ref):
        @pl.when(pl.program_id(0) == 0)
        def _():
            dw_ref[...] = jnp.zeros_like(dw_ref)
            db_ref[...] = jnp.zeros_like(db_ref)
            dlg_ref[...] = jnp.zeros_like(dlg_ref)
            dlb_ref[...] = jnp.zeros_like(dlb_ref)

        up, vp, z, dy = u_ref[...], v_ref[...], z_ref[...], dy_ref[...]
        u, gu = _gelu_and_grad(up)
        v, gv = _gelu_and_grad(vp)
        s, gs = _silu_and_grad(z)
        mu = jnp.mean(v, axis=-1, keepdims=True)
        xc = v - mu
        rs = lax.rsqrt(jnp.mean(xc * xc, axis=-1, keepdims=True) + EPS)
        vhat = xc * rs
        lg = lg_ref[...]
        vn = (vhat * lg + lb_ref[...]).astype(bf16)
        tri = lax.broadcasted_iota(jnp.int32, (CHUNK, CHUNK), 0) >= lax.broadcasted_iota(jnp.int32, (CHUNK, CHUNK), 1)
        lane = lax.broadcasted_iota(jnp.int32, (CHUNK, HEAD_DIM), 1)
        dys = dy * s
        db = jnp.zeros((CHUNK, HEAD_DIM), f32)
        for g in range(A_GROUPS):
            sl = slice(g * HEAD_DIM, (g + 1) * HEAD_DIM)
            wm = jnp.where(tri, w_ref[g], 0.0).astype(bf16)
            mixed = _dot(wm, vn[:, sl]) + bt_ref[:, g:g + 1]
            dmix = dys[:, sl] * u[:, sl]
            dp_ref[:, sl] = (dys[:, sl] * mixed * gu[:, sl]).astype(bf16)
            dp_ref[:, 2 * da + g * HEAD_DIM:2 * da + (g + 1) * HEAD_DIM] = (
                dy[:, sl] * u[:, sl] * mixed * gs[:, sl]).astype(bf16)
            dmb = dmix.astype(bf16)
            dw_ref[g] += jnp.where(tri, _dot(dmb, vn[:, sl], NT), 0.0)
            dvn_ref[:, sl] = _dot(wm, dmb, TN)
            db = db + jnp.where(lane == g, jnp.sum(dmix, axis=1, keepdims=True), 0.0)
        db_ref[...] += db
        dvn = dvn_ref[...]
        dlg_ref[...] += jnp.sum(dvn * vhat, axis=0, keepdims=True)
        dlb_ref[...] += jnp.sum(dvn, axis=0, keepdims=True)
        dvhat = dvn * lg
        dv = rs * (dvhat - jnp.mean(dvhat, axis=-1, keepdims=True)
                   - vhat * jnp.mean(dvhat * vhat, axis=-1, keepdims=True))
        dp_ref[:, da:2 * da] = (dv * gv).astype(bf16)

    blk = lambda cb: pl.BlockSpec((CHUNK, da), lambda c: (c, cb))
    full = lambda shp: pl.BlockSpec(shp, lambda c: (0,) * len(shp))
    return pl.pallas_call(
        body, name="sgu_bwd", grid=(S // CHUNK,),
        in_specs=[blk(0), blk(1), blk(2), blk(0), full((1, da)), full((1, da)),
                  full((A_GROUPS, CHUNK, CHUNK)), full((CHUNK, A_GROUPS))],
        out_specs=[pl.BlockSpec((CHUNK, 3 * da), lambda c: (c, 0)), full((A_GROUPS, CHUNK, CHUNK)),
                   full((CHUNK, HEAD_DIM)), full((1, da)), full((1, da))],
        out_shape=[SDS((S, n_proj), bf16), SDS((A_GROUPS, CHUNK, CHUNK), f32), SDS((CHUNK, HEAD_DIM), f32),
                   SDS((1, da), f32), SDS((1, da), f32)],
        scratch_shapes=[pltpu.VMEM((CHUNK, da), f32)], compiler_params=_params(),
    )(proj, proj, proj, dy, ln_g, ln_b, w_s, b_t)


def _sb_scores(q, kblk, kb, rows, cols, masked):
    z = _dot(q, kblk, NT) * (1.0 / math.sqrt(HEAD_DIM))
    t = jnp.log1p(jnp.exp(-jnp.abs(z)))
    log_1mb = -(jnp.maximum(z, 0.0) + t)
    log_beta = jnp.minimum(z, 0.0) - t
    if not masked:
        return None, log_beta, log_1mb
    causal = (cols + kb * TK) < rows
    return causal, log_beta, jnp.where(causal, log_1mb, 0.0)


def _sb_tiles(i):
    rows = i * TQ + lax.broadcasted_iota(jnp.int32, (TQ, TK), 0)
    cols = lax.broadcasted_iota(jnp.int32, (TQ, TK), 1)
    r_i = lax.broadcasted_iota(jnp.int32, (TK, TK), 0)
    c_i = lax.broadcasted_iota(jnp.int32, (TK, TK), 1)
    return rows, cols, (r_i > c_i).astype(bf16), (r_i < c_i).astype(bf16)


def _suffix_sum(t, tri):
    hi = t.astype(bf16)
    lo = (t - hi.astype(f32)).astype(bf16)
    return _dot(hi, tri) + _dot(lo, tri)


def _sb_fwd(proj, y_prev, col0, after):
    S = proj.shape[0]
    D = y_prev.shape[1]
    dh = N_HEADS * HEAD_DIM
    n_diag = TQ // TK

    def body(q_ref, k_ref, v_ref, z_ref, yp_ref, after_ref, y_ref, o_ref, car_ref, qb, kb_s, vb_s, c_ref):
        i = pl.program_id(0)

        @pl.when(i == 0)
        def _():
            kb_s[...] = k_ref[...].astype(bf16)
            vb_s[...] = v_ref[...].astype(bf16)

        qb[...] = q_ref[...].astype(bf16)
        o_ref[...] = jnp.zeros_like(o_ref)
        c_ref[...] = jnp.zeros_like(c_ref)
        car_ref[...] = jnp.zeros_like(car_ref)
        nkb = (i + 1) * n_diag
        rows, cols, upper, _ = _sb_tiles(i)

        def make_step(masked):
            def step(jj, carry):
                kb = nkb - 1 - jj
                off = pl.multiple_of(kb * TK, TK)
                hs = range(N_HEADS)
                sls = [slice(h * HEAD_DIM, (h + 1) * HEAD_DIM) for h in hs]
                sc = [_sb_scores(qb[:, sls[h]], kb_s[pl.ds(off, TK), sls[h]], kb, rows, cols, masked) for h in hs]
                suf = [_suffix_sum(sc[h][2], upper) for h in hs]
                cs = [c_ref[h] for h in hs]
                es = [jnp.exp(sc[h][1] + suf[h] + cs[h]) for h in hs]
                if masked:
                    es = [jnp.where(sc[h][0], es[h], 0.0) for h in hs]
                pv = [_dot(es[h].astype(bf16), vb_s[pl.ds(off, TK), sls[h]]) for h in hs]
                for h in hs:
                    o_ref[:, sls[h]] += pv[h]
                    car_ref[h] = jnp.where(cols == kb, cs[h], car_ref[h])
                    c_ref[h] = cs[h] + jnp.sum(sc[h][2], axis=1, keepdims=True)
                return carry
            return step

        lax.fori_loop(0, n_diag, make_step(True), 0)
        lax.fori_loop(n_diag, nkb, make_step(False), 0)
        z = z_ref[...]
        y_ref[...] = (o_ref[...] * (z * jax.nn.sigmoid(z))).astype(bf16)

    cb = col0 * HEAD_DIM // dh
    qspec = lambda k: pl.BlockSpec((TQ, dh), lambda i: (i, cb + k))
    kspec = lambda k: pl.BlockSpec((S, dh), lambda i: (0, cb + k))
    return pl.pallas_call(
        body, name="sb_fwd", grid=(S // TQ,),
        in_specs=[qspec(0), kspec(1), kspec(2), qspec(3), pl.BlockSpec(memory_space=pl.ANY),
                  pl.BlockSpec(memory_space=pl.ANY)],
        out_specs=[pl.BlockSpec((TQ, dh), lambda i: (i, A_GROUPS * HEAD_DIM // dh)),
                   pl.BlockSpec((TQ, dh), lambda i: (i, 0)),
                   pl.BlockSpec((N_HEADS, TQ, TK), lambda i: (0, i, 0))],
        out_shape=[SDS((S, D), bf16), SDS((S, dh), f32), SDS((N_HEADS, S, TK), f32)],
        input_output_aliases={4: 0},
        scratch_shapes=[pltpu.VMEM((TQ, dh), bf16), pltpu.VMEM((S, dh), bf16), pltpu.VMEM((S, dh), bf16),
                        pltpu.VMEM((N_HEADS, TQ, TK), f32)],
        compiler_params=_params(),
    )(proj, proj, proj, proj, y_prev, after)


def _sb_bwd(proj, o, car, dy, dproj_prev, col0, after):
    S = proj.shape[0]
    n_i = S // TQ
    dh = N_HEADS * HEAD_DIM
    n_diag = TQ // TK
    cb = col0 * HEAD_DIM // dh
    scale = 1.0 / math.sqrt(HEAD_DIM)

    def body(q_ref, k_ref, v_ref, z_ref, o_ref, car_ref, dy_ref, dpp_ref, after_ref,
             dp_ref, qb, kb_s, vb_s, dob, p_ref, dq_acc, dk_acc, dv_acc, st_a, st_b, st_k, st_v):
        i = pl.program_id(0)

        def put(stage_ref, row0, nrows, k):
            pltpu.sync_copy(stage_ref, dp_ref.at[pl.ds(row0, nrows), pl.ds((cb + k) * dh, dh)])

        @pl.when(i == 0)
        def _():
            kb_s[...] = k_ref[...].astype(bf16)
            vb_s[...] = v_ref[...].astype(bf16)
            dk_acc[...] = jnp.zeros_like(dk_acc)
            dv_acc[...] = jnp.zeros_like(dv_acc)

        s, gs = _silu_and_grad(z_ref[...])
        dy = dy_ref[...]
        st_b[...] = (dy * o_ref[...] * gs).astype(bf16)
        dob[...] = (dy * s).astype(bf16)
        qb[...] = q_ref[...].astype(bf16)
        p_ref[...] = jnp.zeros_like(p_ref)
        dq_acc[...] = jnp.zeros_like(dq_acc)
        nkb = (i + 1) * n_diag
        rows, cols, upper, lower = _sb_tiles(i)

        def make_step(masked):
            def step(kb, carry):
                off = pl.multiple_of(kb * TK, TK)
                hs = range(N_HEADS)
                sls = [slice(h * HEAD_DIM, (h + 1) * HEAD_DIM) for h in hs]
                qs = [qb[:, sls[h]] for h in hs]
                ks = [kb_s[pl.ds(off, TK), sls[h]] for h in hs]
                dos = [dob[:, sls[h]] for h in hs]
                sc = [_sb_scores(qs[h], ks[h], kb, rows, cols, masked) for h in hs]
                da = [_dot(dos[h], vb_s[pl.ds(off, TK), sls[h]], NT) for h in hs]
                suf = [_suffix_sum(sc[h][2], upper) for h in hs]
                onehot = cols == kb
                cs = [jnp.sum(jnp.where(onehot, car_ref[h], 0.0), axis=1, keepdims=True) for h in hs]
                es = [jnp.exp(sc[h][1] + suf[h] + cs[h]) for h in hs]
                if masked:
                    es = [jnp.where(sc[h][0], es[h], 0.0) for h in hs]
                gs_ = [da[h] * es[h] for h in hs]
                ps = [p_ref[h] for h in hs]
                pre = [_suffix_sum(gs_[h], lower) + ps[h] for h in hs]
                dzs = []
                for h in hs:
                    beta = jnp.exp(sc[h][1])
                    dzz = gs_[h] * (1.0 - beta) - beta * pre[h]
                    if masked:
                        dzz = jnp.where(sc[h][0], dzz, 0.0)
                    dzs.append((dzz * scale).astype(bf16))
                dqs = [_dot(dzs[h], ks[h]) for h in hs]
                dks = [_dot(dzs[h], qs[h], TN) for h in hs]
                dvs = [_dot(es[h].astype(bf16), dos[h], TN) for h in hs]
                for h in hs:
                    dq_acc[:, sls[h]] += dqs[h]
                    dk_acc[pl.ds(off, TK), sls[h]] += dks[h]
                    dv_acc[pl.ds(off, TK), sls[h]] += dvs[h]
                    p_ref[h] = ps[h] + jnp.sum(gs_[h], axis=1, keepdims=True)
                return carry
            return step

        lax.fori_loop(0, nkb - n_diag, make_step(False), 0)
        lax.fori_loop(nkb - n_diag, nkb, make_step(True), 0)
        st_a[...] = dq_acc[...].astype(bf16)
        row0 = pl.multiple_of(i * TQ, TQ)
        put(st_a, row0, TQ, 0)
        put(st_b, row0, TQ, 3)

        @pl.when(i == n_i - 1)
        def _():
            st_k[...] = dk_acc[...].astype(bf16)
            st_v[...] = dv_acc[...].astype(bf16)
            put(st_k, 0, S, 1)
            put(st_v, 0, S, 2)

    qspec = lambda k: pl.BlockSpec((TQ, dh), lambda i: (i, cb + k))
    kspec = lambda k: pl.BlockSpec((S, dh), lambda i: (0, cb + k))
    return pl.pallas_call(
        body, name="sb_bwd", grid=(n_i,),
        in_specs=[qspec(0), kspec(1), kspec(2), qspec(3),
                  pl.BlockSpec((TQ, dh), lambda i: (i, 0)),
                  pl.BlockSpec((N_HEADS, TQ, TK), lambda i: (0, i, 0)),
                  pl.BlockSpec((TQ, dh), lambda i: (i, A_GROUPS * HEAD_DIM // dh)),
                  pl.BlockSpec(memory_space=pl.ANY), pl.BlockSpec(memory_space=pl.ANY)],
        out_specs=pl.BlockSpec(memory_space=pl.ANY),
        out_shape=SDS(dproj_prev.shape, bf16),
        input_output_aliases={7: 0},
        scratch_shapes=[pltpu.VMEM((TQ, dh), bf16), pltpu.VMEM((S, dh), bf16), pltpu.VMEM((S, dh), bf16),
                        pltpu.VMEM((TQ, dh), bf16), pltpu.VMEM((N_HEADS, TQ, TK), f32), pltpu.VMEM((TQ, dh), f32),
                        pltpu.VMEM((S, dh), f32), pltpu.VMEM((S, dh), f32),
                        pltpu.VMEM((TQ, dh), bf16), pltpu.VMEM((TQ, dh), bf16),
                        pltpu.VMEM((S, dh), bf16), pltpu.VMEM((S, dh), bf16)],
        compiler_params=_params(56),
    )(proj, proj, proj, proj, o, car, dy, dproj_prev, after)


def _mem_kv(mem, mg_row, w_kv_g):
    M, D = mem.shape
    rb, n = w_kv_g.shape[1], w_kv_g.shape[2]

    def body(m_ref, g_ref, w_ref, kv_ref):
        mv = m_ref[...]
        r = lax.rsqrt(jnp.mean(mv * mv, axis=-1, keepdims=True) + EPS)
        mh = (mv * r * g_ref[...]).astype(bf16)
        kv_ref[...] = _dot(mh, w_ref[...].reshape(N_DEV * rb, n))

    return pl.pallas_call(
        body, name="mem_kv", grid=(1,),
        in_specs=[pl.BlockSpec((M, D), lambda i: (0, 0)), pl.BlockSpec((1, D), lambda i: (0, 0)),
                  pl.BlockSpec((N_DEV, rb, n), lambda i: (0, 0, 0))],
        out_specs=pl.BlockSpec((M, n), lambda i: (0, 0)),
        out_shape=SDS((M, n), f32), compiler_params=_params(),
    )(mem, mg_row, w_kv_g)


def _xattn_head(q_ref, kv_ref, qg, kg, h):
    dc = N_HEADS * HEAD_DIM
    sl = slice(h * HEAD_DIM, (h + 1) * HEAD_DIM)
    qh = q_ref[:, sl]
    rq = lax.rsqrt(jnp.mean(qh * qh, axis=-1, keepdims=True) + EPS)
    qhat = qh * rq
    qn = (qhat * qg).astype(bf16)
    kh = kv_ref[:, sl]
    rk = lax.rsqrt(jnp.mean(kh * kh, axis=-1, keepdims=True) + EPS)
    kn = (kh * rk * kg).astype(bf16)
    vh = kv_ref[:, dc + h * HEAD_DIM:dc + (h + 1) * HEAD_DIM].astype(bf16)
    s = _dot(qn, kn, NT) * (1.0 / math.sqrt(HEAD_DIM))
    e = jnp.exp(s - jnp.max(s, axis=-1, keepdims=True))
    p = e / jnp.sum(e, axis=-1, keepdims=True)
    o = _dot(p.astype(bf16), vh)
    return sl, rq, qhat, qn, kn, vh, p, o


def _xattn_fwd(proj, kv, qg_row, kg_row, y_prev, col0, tq=512):
    S = proj.shape[0]
    D = y_prev.shape[1]
    dc = N_HEADS * HEAD_DIM
    M = kv.shape[0]

    def body(q_ref, z_ref, kv_ref, qg_ref, kg_ref, yp_ref, y_ref):
        for h in range(N_HEADS):
            sl, _, _, _, _, _, _, o = _xattn_head(q_ref, kv_ref, qg_ref[...], kg_ref[...], h)
            z = z_ref[:, sl]
            y_ref[:, sl] = (o * (z * jax.nn.sigmoid(z))).astype(bf16)

    full = lambda shp: pl.BlockSpec(shp, lambda i: (0,) * len(shp))
    return pl.pallas_call(
        body, name="xattn_fwd", grid=(S // tq,),
        in_specs=[pl.BlockSpec((tq, dc), lambda i: (i, col0)), pl.BlockSpec((tq, dc), lambda i: (i, col0 + 1)),
                  full((M, 2 * dc)), full((1, HEAD_DIM)), full((1, HEAD_DIM)), pl.BlockSpec(memory_space=pl.ANY)],
        out_specs=pl.BlockSpec((tq, dc), lambda i: (i, D // dc - 1)),
        out_shape=SDS((S, D), bf16), input_output_aliases={5: 0}, compiler_params=_params(),
    )(proj, proj, kv, qg_row, kg_row, y_prev)


def _xattn_bwd(proj, kv, qg_row, kg_row, dy, dproj_prev, col0, tq=512):
    S = proj.shape[0]
    D = dy.shape[1]
    dc = N_HEADS * HEAD_DIM
    M = kv.shape[0]

    def body(q_ref, z_ref, kv_ref, qg_ref, kg_ref, dy_ref, dpp_ref, dp_ref, dkn_ref, dv_ref, dqg_ref):
        @pl.when(pl.program_id(0) == 0)
        def _():
            dkn_ref[...] = jnp.zeros_like(dkn_ref)
            dv_ref[...] = jnp.zeros_like(dv_ref)
            dqg_ref[...] = jnp.zeros_like(dqg_ref)

        qg = qg_ref[...]
        for h in range(N_HEADS):
            sl, rq, qhat, qn, kn, vh, p, o = _xattn_head(q_ref, kv_ref, qg, kg_ref[...], h)
            s, gs = _silu_and_grad(z_ref[:, sl])
            dyh = dy_ref[:, sl]
            dp_ref[:, dc + h * HEAD_DIM:dc + (h + 1) * HEAD_DIM] = (dyh * o * gs).astype(bf16)
            dob = (dyh * s).astype(bf16)
            dpr = _dot(dob, vh, NT)
            dv_ref[:, sl] += _dot(p.astype(bf16), dob, TN)
            ds = (p * (dpr - jnp.sum(p * dpr, axis=-1, keepdims=True)) * (1.0 / math.sqrt(HEAD_DIM))).astype(bf16)
            dqn = _dot(ds, kn)
            dkn_ref[:, sl] += _dot(ds, qn, TN)
            dqg_ref[...] += jnp.sum(dqn * qhat, axis=0, keepdims=True)
            dqhat = dqn * qg
            dp_ref[:, sl] = (rq * (dqhat - qhat * jnp.mean(dqhat * qhat, axis=-1, keepdims=True))).astype(bf16)

    full = lambda shp: pl.BlockSpec(shp, lambda i: (0,) * len(shp))
    return pl.pallas_call(
        body, name="xattn_bwd", grid=(S // tq,),
        in_specs=[pl.BlockSpec((tq, dc), lambda i: (i, col0)), pl.BlockSpec((tq, dc), lambda i: (i, col0 + 1)),
                  full((M, 2 * dc)), full((1, HEAD_DIM)), full((1, HEAD_DIM)),
                  pl.BlockSpec((tq, dc), lambda i: (i, D // dc - 1)), pl.BlockSpec(memory_space=pl.ANY)],
        out_specs=[pl.BlockSpec((tq, 2 * dc), lambda i: (i, col0 // 2)), full((M, dc)), full((M, dc)),
                   full((1, HEAD_DIM))],
        out_shape=[SDS(dproj_prev.shape, bf16), SDS((M, dc), f32), SDS((M, dc), f32), SDS((1, HEAD_DIM), f32)],
        input_output_aliases={6: 0}, compiler_params=_params(),
    )(proj, proj, kv, qg_row, kg_row, dy, dproj_prev)


def _mem_bwd(mem, mg_row, kv, dkn, dv, kg_row, w_kv_g):
    M, D = mem.shape
    rb, n = w_kv_g.shape[1], w_kv_g.shape[2]
    dc = n // 2

    def body(m_ref, g_ref, kv_ref, dkn_ref, dv_ref, kg_ref, w_ref, dw_ref, dmg_ref, dkg_ref, dkv_ref):
        mv = m_ref[...]
        r = lax.rsqrt(jnp.mean(mv * mv, axis=-1, keepdims=True) + EPS)
        mhat = mv * r
        mh = (mhat * g_ref[...]).astype(bf16)
        kg = kg_ref[...]
        dkg = jnp.zeros((1, HEAD_DIM), f32)
        for h in range(N_HEADS):
            sl = slice(h * HEAD_DIM, (h + 1) * HEAD_DIM)
            kh = kv_ref[:, sl]
            rk = lax.rsqrt(jnp.mean(kh * kh, axis=-1, keepdims=True) + EPS)
            khat = kh * rk
            dkn_h = dkn_ref[:, sl]
            dkg = dkg + jnp.sum(dkn_h * khat, axis=0, keepdims=True)
            dkhat = dkn_h * kg
            dkv_ref[:, sl] = (rk * (dkhat - khat * jnp.mean(dkhat * khat, axis=-1, keepdims=True))).astype(bf16)
        dkv_ref[:, dc:] = dv_ref[...].astype(bf16)
        dkg_ref[...] = dkg
        dkv = dkv_ref[...]
        dw_ref[...] = _dot(mh, dkv, TN).astype(bf16).reshape(N_DEV, rb, n)
        dmh = _dot(dkv, w_ref[...].reshape(N_DEV * rb, n), NT)
        dmg_ref[...] = jnp.sum(dmh * mhat, axis=0, keepdims=True)

    full = lambda shp: pl.BlockSpec(shp, lambda i: (0,) * len(shp))
    wspec = full((N_DEV, rb, n))
    return pl.pallas_call(
        body, name="mem_bwd", grid=(1,),
        in_specs=[full((M, D)), full((1, D)), full((M, n)), full((M, dc)), full((M, dc)), full((1, HEAD_DIM)), wspec],
        out_specs=[wspec, full((1, D)), full((1, HEAD_DIM))],
        out_shape=[SDS((N_DEV, rb, n), bf16), SDS((1, D), f32), SDS((1, HEAD_DIM), f32)],
        scratch_shapes=[pltpu.VMEM((M, n), bf16)], compiler_params=_params(),
    )(mem, mg_row, kv, dkn, dv, kg_row, w_kv_g)


SMALL = ("norm_g", "sgu_ln_g", "sgu_ln_b", "sgu_w", "sgu_b", "mem_norm_g", "q_norm_g", "k_norm_g")


def _pack_small(parts):
    flat = jnp.concatenate([parts[n].reshape(-1) for n in SMALL])
    pad = (-flat.shape[0]) % (8 * 128)
    return jnp.pad(flat, (0, pad)).reshape(-1, 128)


def _unpack_small(packed, like):
    flat = packed.reshape(-1)
    out, off = {}, 0
    for n in SMALL:
        size = math.prod(like[n].shape)
        out[n] = flat[off:off + size].reshape(like[n].shape)
        off += size
    return out


def kernel(x, mem, norm_g, w_in, sgu_ln_g, sgu_ln_b, sgu_w, sgu_b, mem_norm_g, w_mem_kv, q_norm_g, k_norm_g, w_out, loss_target, m_norm_g, m_w_in, m_sgu_ln_g, m_sgu_ln_b, m_sgu_w, m_sgu_b, m_mem_norm_g, m_w_mem_kv, m_q_norm_g, m_k_norm_g, m_w_out, v_norm_g, v_w_in, v_sgu_ln_g, v_sgu_ln_b, v_sgu_w, v_sgu_b, v_mem_norm_g, v_w_mem_kv, v_q_norm_g, v_k_norm_g, v_w_out):
    L, D, wc = w_in.shape
    S = x.shape[1]
    da = D // 2
    xs = x.reshape(S, D)
    mems = mem.reshape(mem.shape[1], D)
    tgt = loss_target.reshape(S, D)
    row = lambda a, l: a[l].reshape(1, -1)
    tie = lambda a, tok: a + tok[0:1, 0:1]
    sb_col, xa_col = 3 * da // HEAD_DIM, (3 * da + D) // (D // 4)

    ax, ay, ac = lax.axis_index("x"), lax.axis_index("y"), lax.axis_index("c")
    ids = jnp.stack([4 * ax + 2 * ay + ac, 2 * ax + ay, ac]).astype(jnp.int32)
    w_b = [(_cast_into_slot("cast_w_in", w_in, l, 512, ids), _cast_into_slot("cast_w_kv", w_mem_kv, l, 256, ids),
            _cast_into_slot("cast_w_out", w_out, l, 256, ids)) for l in range(L)]
    first = _gather2_start("gather_w_in0", [w_b[0][0]], ids)
    in_fwd = _gather2_forward("gather_w_in0_forward", _split_wait(first, first["token"]), ids)

    acts = []
    xl = xs
    for l in range(L):
        g_row = row(norm_g, l)
        (w_in_g,) = _split_wait(in_fwd, g_row if l else in_fwd["token"])
        rest = _gather2_start(f"gather_w_rest{l}", [w_b[l][1], w_b[l][2]], w_in_g)
        proj, h = _rms_proj(xl, tie(g_row, rest["token"]), w_in_g)
        rest_fwd = _gather2_forward(f"gather_w_rest{l}_forward", _split_wait(rest, proj), proj)
        lg_row = tie(row(sgu_ln_g, l), rest_fwd["token"])
        if l + 1 < L:
            nxt = _gather2_start(f"gather_w_in{l + 1}", [w_b[l + 1][0]], proj)
            lg_row = tie(lg_row, nxt["token"])
        y = _sgu_fwd(proj, lg_row, row(sgu_ln_b, l), sgu_w[l], sgu_b[l].T)
        y, o_b, car = _sb_fwd(proj, y, sb_col, lg_row)
        w_kv_g, w_out_g = _split_wait(rest_fwd, o_b)
        kv = _mem_kv(mems, row(mem_norm_g, l), w_kv_g)
        y = _xattn_fwd(proj, kv, row(q_norm_g, l), row(k_norm_g, l), y, xa_col)
        order = kv
        if l + 1 < L:
            in_fwd = _gather2_forward(f"gather_w_in{l + 1}_forward", _split_wait(nxt, y), y)
            order = in_fwd["token"]
        x_next = _out_proj(xl, y, w_out_g, order)
        acts.append((xl, proj, h, y, o_b, car, kv, w_in_g, w_kv_g, w_out_g))
        xl = x_next

    dx, dxb, loss_part = _loss_and_grad(xl, tgt, 512)
    loss = lax.psum(loss_part[0, 0], ("x", "y", "c"))

    weights = dict(norm_g=norm_g, sgu_ln_g=sgu_ln_g, sgu_ln_b=sgu_ln_b, sgu_w=sgu_w, sgu_b=sgu_b,
                   mem_norm_g=mem_norm_g, q_norm_g=q_norm_g, k_norm_g=k_norm_g)
    moms_m = dict(norm_g=m_norm_g, sgu_ln_g=m_sgu_ln_g, sgu_ln_b=m_sgu_ln_b, sgu_w=m_sgu_w, sgu_b=m_sgu_b,
                  mem_norm_g=m_mem_norm_g, q_norm_g=m_q_norm_g, k_norm_g=m_k_norm_g)
    moms_v = dict(norm_g=v_norm_g, sgu_ln_g=v_sgu_ln_g, sgu_ln_b=v_sgu_ln_b, sgu_w=v_sgu_w, sgu_b=v_sgu_b,
                  mem_norm_g=v_mem_norm_g, q_norm_g=v_q_norm_g, k_norm_g=v_k_norm_g)
    head_rows = 16
    assert D // 128 <= head_rows

    seconds = {}
    pending = None
    small = {n: [None] * L for n in SMALL}
    for l in reversed(range(L)):
        xl, proj, h, y, o_b, car, kv, w_in_g, w_kv_g, w_out_g = acts[l]
        dy = _out_bwd_dy(dxb, w_out_g)
        order = dy
        if pending is not None:
            seconds[pending[0]] = _scatter2_second_level(pending[0], pending[1], dy, ids)
            order = seconds[pending[0]]["token"]
        g_out = _tn_grad("out_bwd_dw", y, dxb, 512, 512, True, order)
        first = _scatter2_pair_start(f"scatter_g_out{l}_pair", [g_out], ids)
        dproj, d_sw, d_sb, d_lg, d_lb = _sgu_bwd(proj, dy, tie(row(sgu_ln_g, l), first["token"]), row(sgu_ln_b, l),
                                                 sgu_w[l], sgu_b[l].T)
        seconds[f"g_out{l}"] = _scatter2_second_level(f"g_out{l}", first, dproj, ids)
        dproj = _sb_bwd(proj, o_b, car, dy, dproj, sb_col, seconds[f"g_out{l}"]["token"])
        dproj, dkn, dv, d_qg = _xattn_bwd(proj, kv, row(q_norm_g, l), row(k_norm_g, l), dy, dproj, xa_col)
        g_kv, d_mg, d_kg = _mem_bwd(mems, row(mem_norm_g, l), kv, dkn, dv, row(k_norm_g, l), w_kv_g)
        for n, val in (("sgu_ln_g", d_lg), ("sgu_ln_b", d_lb), ("sgu_w", d_sw), ("sgu_b", d_sb[:, :A_GROUPS].T),
                       ("mem_norm_g", d_mg), ("q_norm_g", d_qg), ("k_norm_g", d_kg)):
            small[n][l] = val.reshape(-1)
        order = d_kg
        if l == 0:
            small["norm_g"][0] = jnp.zeros_like(small["norm_g"][1])
            part = _pack_small({n: jnp.stack(small[n]).reshape(weights[n].shape) for n in SMALL})
            tail = _gather2_start("gather_small_tail", [_into_slot("small_tail_slot", part[head_rows:], ids)], ids)
            order = tail["token"]
        g_in_l = _tn_grad("in_bwd_dw", h, dproj, 1024, wc, False, order)
        first = _scatter2_pair_start(f"scatter_g_rest{l}_pair", [g_kv, g_in_l], ids)
        g_row = tie(row(norm_g, l), first["token"])
        if l == 0:
            tail_fwd = _gather2_forward("gather_small_tail_forward", _split_wait(tail, g_in_l), g_in_l)
            g_row = tie(g_row, tail_fwd["token"])
        dx, dxb, d_ng = _in_bwd_dx(dproj, w_in_g, xl, g_row, dx)
        small["norm_g"][l] = d_ng.reshape(-1)
        pending = (f"g_rest{l}", first)
    head_part = jnp.concatenate([small["norm_g"][0].reshape(-1, 128), jnp.zeros((head_rows - D // 128, 128), f32)])
    head = _gather2_start("gather_small_head", [_into_slot("small_head_slot", head_part, ids)], ids)
    seconds[pending[0]] = _scatter2_second_level(pending[0], pending[1], head["token"], ids)
    head_fwd = _gather2_forward("gather_small_head_forward", _split_wait(head, seconds[pending[0]]["token"]), ids)

    (r_tail,) = _split_wait(tail_fwd, dx)
    (r_head,) = _split_wait(head_fwd, r_tail)
    sm = _small_sum_adam(r_head, r_tail, _pack_small(weights), _pack_small(moms_m), _pack_small(moms_v))

    a_out = a_kv = a_in = None
    order = sm[0]
    for l in reversed(range(L)):
        (r_out,) = _scatter2_finish(seconds[f"g_out{l}"], order)
        a_out = _sum_adam("adam_w_out", r_out, w_out, m_w_out, v_w_out, l, a_out, 128, ids)
        r_kv, r_in = _scatter2_finish(seconds[f"g_rest{l}"], a_out[0])
        a_kv = _sum_adam("adam_w_kv", r_kv, w_mem_kv, m_w_mem_kv, v_w_mem_kv, l, a_kv, 256, ids)
        a_in = _sum_adam("adam_w_in", r_in, w_in, m_w_in, v_w_in, l, a_in, 256, ids)
        order = a_in[0]
    res = {"w_out": a_out, "w_mem_kv": a_kv, "w_in": a_in}
    sm = [_unpack_small(a, weights) for a in sm]
    for n in SMALL:
        res[n] = [a[n] for a in sm]

    order = ("norm_g", "w_in", "sgu_ln_g", "sgu_ln_b", "sgu_w", "sgu_b", "mem_norm_g", "w_mem_kv", "q_norm_g",
             "k_norm_g", "w_out")
    outs = [loss, dx.reshape(x.shape)]
    for k in range(4):
        outs += [res[n][k] for n in order]
    return tuple(outs)
```

```python
import functools
import math

import jax
import jax.numpy as jnp
from jax import lax
from jax.experimental import pallas as pl
from jax.experimental.pallas import tpu as pltpu

f32 = jnp.float32
bf16 = jnp.bfloat16
SDS = jax.ShapeDtypeStruct

N_DEV = 8
EPS = 1e-6
CHUNK = 128
A_GROUPS = 8
HEAD_DIM = 128
N_HEADS = 4
TQ = 256
TK = 128
ADAM_LR, ADAM_B1, ADAM_B2, ADAM_EPS, ADAM_WD, ADAM_STEP = 0.001, 0.9, 0.999, 1e-08, 0.01, 10
MIB = 1024 * 1024

NT = (((1,), (1,)), ((), ()))
TN = (((0,), (0,)), ((), ()))


def _params(vmem_mib=48):
    return pltpu.CompilerParams(vmem_limit_bytes=vmem_mib * MIB)


def _gelu_and_grad(x):
    e = lax.erf(x * (1.0 / math.sqrt(2.0)))
    cdf = 0.5 * (1.0 + e)
    pdf = jnp.exp(-0.5 * x * x) * (1.0 / math.sqrt(2.0 * math.pi))
    return x * cdf, cdf + x * pdf


def _gelu(x):
    return 0.5 * x * (1.0 + lax.erf(x * (1.0 / math.sqrt(2.0))))


def _silu_and_grad(z):
    sg = jax.nn.sigmoid(z)
    return z * sg, sg * (1.0 + z * (1.0 - sg))


def _dot(a, b, dims=None):
    if dims is None:
        return jnp.dot(a, b, preferred_element_type=f32)
    return lax.dot_general(a, b, dims, preferred_element_type=f32)


_HBM = pl.BlockSpec(memory_space=pltpu.HBM)
_SEM = pl.BlockSpec(memory_space=pltpu.SEMAPHORE)
_EFFECT = pltpu.SideEffectType.DATAFLOW_SIDE_EFFECTING


def _split_start(name, bufs, n_remote, n_local, build, after):
    nb = len(bufs)

    def body(*refs):
        token = refs[-1]
        locals_, remotes = build(refs[:nb], *refs[nb + 1:nb + 4])
        for cp in locals_ + remotes:
            cp.start()
        token[...] = jnp.zeros_like(token)

    hbm = lambda a: pltpu.with_memory_space_constraint(a, pltpu.HBM)
    outs = pl.pallas_call(
        body, name=name,
        out_shape=(pltpu.SemaphoreType.DMA((n_remote,)), pltpu.SemaphoreType.DMA((n_remote,)),
                   pltpu.SemaphoreType.DMA((max(n_local, 1),)),
                   *[pltpu.HBM(b.shape, b.dtype) for b in bufs], SDS((8, 128), f32)),
        in_specs=[_HBM] * nb + [pl.BlockSpec(memory_space=pl.ANY)],
        out_specs=(_SEM, _SEM, _SEM, *[_HBM] * nb, pl.BlockSpec(memory_space=pltpu.VMEM)),
        input_output_aliases={k: 3 + k for k in range(nb)},
        compiler_params=pltpu.CompilerParams(has_side_effects=_EFFECT),
    )(*[hbm(b) for b in bufs], after)
    return dict(name=name, build=build, sems=outs[:3], bufs=outs[3:3 + nb], token=outs[-1])


def _split_wait(handle, after):
    build, bufs = handle["build"], handle["bufs"]
    nb = len(bufs)

    def body(*refs):
        locals_, remotes = build(refs[:nb], *refs[nb:nb + 3])
        for cp in remotes:
            cp.wait_recv()
        for cp in remotes:
            cp.wait_send()
        for cp in locals_:
            cp.wait()

    outs = pl.pallas_call(
        body, name=handle["name"] + "_wait",
        out_shape=tuple(pltpu.HBM(b.shape, b.dtype) for b in bufs),
        in_specs=[_HBM] * nb + [_SEM] * 3 + [pl.BlockSpec(memory_space=pl.ANY)],
        out_specs=tuple([_HBM] * nb),
        input_output_aliases={k: k for k in range(nb)},
        compiler_params=pltpu.CompilerParams(has_side_effects=_EFFECT),
    )(*bufs, *handle["sems"], after)
    return list(outs)


def _remote(src, dst, send_sems, recv_sems, k, to):
    return pltpu.make_async_remote_copy(src_ref=src, dst_ref=dst, send_sem=send_sems.at[k], recv_sem=recv_sems.at[k],
                                        device_id=to, device_id_type=pl.DeviceIdType.MESH)


def _other_chips(x, y):
    return [(1 - x, y), (x, 1 - y), (1 - x, 1 - y)]


def _gather2_start(name, lands, after):
    def build(refs, send, recv, loc):
        x, y, c = lax.axis_index("x"), lax.axis_index("y"), lax.axis_index("c")
        me = 4 * x + 2 * y + c
        remotes = []
        for a, d in enumerate(refs):
            remotes.append(_remote(d.at[me], d.at[me], send, recv, 4 * a, (x, y, 1 - c)))
            remotes += [_remote(d.at[me], d.at[me], send, recv, 4 * a + 1 + k, (px, py, c))
                        for k, (px, py) in enumerate(_other_chips(x, y))]
        return [], remotes

    return _split_start(name, list(lands), 4 * len(lands), 0, build, after)


def _gather2_forward(name, lands, after):
    n = len(lands)

    def build(refs, send, recv, loc):
        x, y, c = lax.axis_index("x"), lax.axis_index("y"), lax.axis_index("c")
        slots = [4 * px + 2 * py + c for px, py in _other_chips(x, y)]
        return [], [_remote(d.at[sl], d.at[sl], send, recv, 3 * a + k, (x, y, 1 - c))
                    for a, d in enumerate(refs) for k, sl in enumerate(slots)]

    return _split_start(name, list(lands), 3 * n, 0, build, after)


def _scatter2_pair_start(name, srcs, after):
    n = len(srcs)

    def build(refs, send, recv, loc):
        x, y, c = lax.axis_index("x"), lax.axis_index("y"), lax.axis_index("c")
        return [], [_remote(refs[a].at[2 * q + 1 - c], refs[n + a].at[q], send, recv, 4 * a + q, (x, y, 1 - c))
                    for a in range(n) for q in range(4)]

    lands = [lax.empty((4,) + s.shape[1:], s.dtype) for s in srcs]
    return _split_start(name, list(srcs) + lands, 4 * n, 0, build, after)


def _scatter2_chip_start(name, pairs, after):
    n = len(pairs)

    def build(refs, send, recv, loc):
        x, y, c = lax.axis_index("x"), lax.axis_index("y"), lax.axis_index("c")
        return [], [_remote(refs[a].at[2 * px + py], refs[n + a].at[2 * x + y], send, recv, 3 * a + k, (px, py, c))
                    for a in range(n) for k, (px, py) in enumerate(_other_chips(x, y))]

    return _split_start(name, list(pairs) + [lax.empty(p.shape, p.dtype) for p in pairs], 3 * n, 0, build, after)


def _pair_sum(name, src, theirs, ids):
    _, R, C = theirs.shape
    tr = min(R, 1024)

    def body(ids_ref, a_ref, b_ref, o_ref):
        o_ref[...] = (a_ref[...].astype(f32) + b_ref[...].astype(f32)).astype(bf16)

    spec = pl.BlockSpec((None, tr, C), lambda q, i, ids: (q, i, 0))
    return pl.pallas_call(
        body, name=name,
        grid_spec=pltpu.PrefetchScalarGridSpec(
            num_scalar_prefetch=1, grid=(4, R // tr),
            in_specs=[pl.BlockSpec((None, tr, C), lambda q, i, ids: (2 * q + ids[2], i, 0)), spec], out_specs=spec),
        out_shape=SDS(theirs.shape, bf16), compiler_params=_params(),
    )(ids, src, theirs)


def _scatter2_second_level(name, first, after, ids):
    outs = _split_wait(first, after)
    n = len(outs) // 2
    pairs = [_pair_sum(f"pair_sum_{name}{a}", outs[a], outs[n + a], ids) for a in range(n)]
    return _scatter2_chip_start(f"scatter_{name}_chip", pairs, ids)


def _scatter2_finish(second, after):
    outs = _split_wait(second, after)
    n = len(outs) // 2
    return [(outs[a], outs[n + a]) for a in range(n)]


def _cast_into_slot(name, w, l, tr, ids):
    _, R, C = w.shape

    def body(ids_ref, w_ref, o_ref):
        o_ref[...] = w_ref[...].astype(bf16)

    return pl.pallas_call(
        body, name=name,
        grid_spec=pltpu.PrefetchScalarGridSpec(
            num_scalar_prefetch=1, grid=(R // tr,),
            in_specs=[pl.BlockSpec((None, tr, C), lambda i, ids: (l, i, 0))],
            out_specs=pl.BlockSpec((None, tr, C), lambda i, ids: (ids[0], i, 0))),
        out_shape=SDS((N_DEV, R, C), bf16), compiler_params=_params(),
    )(ids, w)


def _into_slot(name, a, ids):
    R, C = a.shape

    def body(ids_ref, a_ref, o_ref):
        o_ref[...] = a_ref[...]

    return pl.pallas_call(
        body, name=name,
        grid_spec=pltpu.PrefetchScalarGridSpec(
            num_scalar_prefetch=1, grid=(1,),
            in_specs=[pl.BlockSpec((R, C), lambda i, ids: (0, 0))],
            out_specs=pl.BlockSpec((None, R, C), lambda i, ids: (ids[0], 0, 0))),
        out_shape=SDS((N_DEV, R, C), f32), compiler_params=_params(),
    )(ids, a)


def _adam_math(w, g, m, v):
    m2 = ADAM_B1 * m + (1.0 - ADAM_B1) * g
    v2 = ADAM_B2 * v + (1.0 - ADAM_B2) * (g * g)
    m_hat = m2 / (1.0 - ADAM_B1 ** ADAM_STEP)
    v_hat = v2 / (1.0 - ADAM_B2 ** ADAM_STEP)
    delta = -ADAM_LR * (m_hat / (jnp.sqrt(v_hat) + ADAM_EPS) + ADAM_WD * w)
    return delta, m2, v2


def _sum_adam(name, pair_recv, w, m, v, l, prev, tr, ids):
    own, recv = pair_recv
    L, R, C = w.shape
    slots = recv.shape[0]

    def body(ids_ref, r_ref, own_ref, w_ref, m_ref, v_ref, *rest):
        g_ref, d_ref, m2_ref, v2_ref = rest[-4:]
        terms = [jnp.where(ids_ref[1] == q, own_ref[...], r_ref[q]).astype(f32) for q in range(slots)]
        g = terms[0]
        for t in terms[1:]:
            g = g + t
        d, m2, v2 = _adam_math(w_ref[...], g, m_ref[...], v_ref[...])
        g_ref[...] = g
        d_ref[...] = d
        m2_ref[...] = m2
        v2_ref[...] = v2

    wspec = pl.BlockSpec((None, tr, C), lambda i, ids: (l, i, 0))
    in_specs = [pl.BlockSpec((slots, tr, C), lambda i, ids: (0, i, 0)),
                pl.BlockSpec((None, tr, C), lambda i, ids: (ids[1], i, 0)), wspec, wspec, wspec]
    args = [ids, recv, own, w, m, v]
    aliases = {}
    if prev is not None:
        in_specs += [pl.BlockSpec(memory_space=pl.ANY)] * 4
        args += list(prev)
        aliases = {6 + k: k for k in range(4)}
    return pl.pallas_call(
        body, name=name,
        grid_spec=pltpu.PrefetchScalarGridSpec(num_scalar_prefetch=1, grid=(R // tr,), in_specs=in_specs,
                                               out_specs=[wspec] * 4),
        out_shape=[SDS((L, R, C), f32)] * 4, input_output_aliases=aliases, compiler_params=_params(),
    )(*args)


def _small_sum_adam(recv_head, recv_tail, w, m, v):
    r0 = recv_head.shape[1]
    R, C = w.shape

    def body(rh_ref, rt_ref, w_ref, m_ref, v_ref, g_ref, d_ref, m2_ref, v2_ref):
        for r_ref, rows in ((rh_ref, slice(0, r0)), (rt_ref, slice(r0, R))):
            g = r_ref[0]
            for s in range(1, N_DEV):
                g = g + r_ref[s]
            d, m2, v2 = _adam_math(w_ref[rows, :], g, m_ref[rows, :], v_ref[rows, :])
            g_ref[rows, :] = g
            d_ref[rows, :] = d
            m2_ref[rows, :] = m2
            v2_ref[rows, :] = v2

    return pl.pallas_call(
        body, name="small_sum_adam", out_shape=[SDS((R, C), f32)] * 4, compiler_params=_params(),
    )(recv_head, recv_tail, w, m, v)


def _loss_and_grad(xf, tgt, tm):
    S, D = xf.shape

    def body(x_ref, t_ref, dx_ref, dxb_ref, l_ref):
        i = pl.program_id(0)
        d = x_ref[...] - t_ref[...]
        dx = d * (1.0 / D)
        dx_ref[...] = dx
        dxb_ref[...] = dx.astype(bf16)
        e = d * d
        part = e[:, 0:128]
        for k in range(1, D // 128):
            part = part + e[:, k * 128:(k + 1) * 128]
        part = jnp.sum(part.reshape(tm // 8, 8, 128), axis=0)

        @pl.when(i == 0)
        def _():
            l_ref[...] = jnp.zeros_like(l_ref)

        l_ref[...] += part

        @pl.when(i == pl.num_programs(0) - 1)
        def _():
            tot = jnp.sum(l_ref[...], axis=1, keepdims=True)
            tot = jnp.sum(tot, axis=0, keepdims=True)
            l_ref[...] = jnp.broadcast_to(tot * (0.5 / D), l_ref.shape)

    row = pl.BlockSpec((tm, D), lambda i: (i, 0))
    return pl.pallas_call(
        body, name="loss_grad", grid=(S // tm,),
        in_specs=[row, row], out_specs=[row, row, pl.BlockSpec((8, 128), lambda i: (0, 0))],
        out_shape=[SDS((S, D), f32), SDS((S, D), bf16), SDS((8, 128), f32)], compiler_params=_params(),
    )(xf, tgt)


def _rms_proj(x, g_row, w_in_g, tm=512):
    S, D = x.shape
    wc = w_in_g.shape[2]
    n_out = N_DEV * wc

    def body(x_ref, g_ref, w_ref, proj_ref, h_ref):
        @pl.when(pl.program_id(1) == 0)
        def _():
            xv = x_ref[...]
            r = lax.rsqrt(jnp.mean(xv * xv, axis=-1, keepdims=True) + EPS)
            h_ref[...] = (xv * r * g_ref[...]).astype(bf16)

        proj_ref[...] = _dot(h_ref[...], w_ref[...])

    return pl.pallas_call(
        body, name="rms_proj", grid=(S // tm, N_DEV),
        in_specs=[pl.BlockSpec((tm, D), lambda i, j: (i, 0)),
                  pl.BlockSpec((1, D), lambda i, j: (0, 0)),
                  pl.BlockSpec((None, D, wc), lambda i, j: (j, 0, 0))],
        out_specs=[pl.BlockSpec((tm, wc), lambda i, j: (i, j)), pl.BlockSpec((tm, D), lambda i, j: (i, 0))],
        out_shape=[SDS((S, n_out), f32), SDS((S, D), bf16)], compiler_params=_params(),
    )(x, g_row, w_in_g)


def _out_proj(x, y, w_out_g, after, tm=512):
    S, D = x.shape
    rb = w_out_g.shape[1]

    def body(x_ref, y_ref, w_ref, after_ref, o_ref):
        w = w_ref[...].reshape(N_DEV * rb, D)
        o_ref[...] = x_ref[...] + _dot(y_ref[...], w)

    row = pl.BlockSpec((tm, D), lambda i: (i, 0))
    return pl.pallas_call(
        body, name="out_proj", grid=(S // tm,),
        in_specs=[row, row, pl.BlockSpec((N_DEV, rb, D), lambda i: (0, 0, 0)), pl.BlockSpec(memory_space=pl.ANY)],
        out_specs=row, out_shape=SDS((S, D), f32), compiler_params=_params(),
    )(x, y, w_out_g, after)


def _out_bwd_dy(dxb, w_out_g, tm=512):
    S, D = dxb.shape
    rb = w_out_g.shape[1]

    nb = 2

    def body(dx_ref, w_ref, o_ref):
        o_ref[...] = _dot(dx_ref[...], w_ref[...].reshape(nb * rb, D), NT)

    return pl.pallas_call(
        body, name="out_bwd_dy", grid=(S // tm, N_DEV // nb),
        in_specs=[pl.BlockSpec((tm, D), lambda i, j: (i, 0)),
                  pl.BlockSpec((nb, rb, D), lambda i, j: (j, 0, 0))],
        out_specs=pl.BlockSpec((tm, nb * rb), lambda i, j: (i, j)),
        out_shape=SDS((S, D), f32), compiler_params=_params(),
    )(dxb, w_out_g)


def _tn_grad(name, a, b, tm, tn, rows_major, after):
    S, M = a.shape
    N = b.shape[1]
    if rows_major:
        rb = M // N_DEV
        nb = tm // rb
        out_shape = SDS((N_DEV, rb, N), bf16)
        out_spec = pl.BlockSpec((nb, rb, tn), lambda i, j: (i, 0, j))
    else:
        out_shape = SDS((N_DEV, M, N // N_DEV), bf16)
        assert tn == N // N_DEV
        out_spec = pl.BlockSpec((None, tm, tn), lambda i, j: (j, i, 0))

    def body(a_ref, b_ref, after_ref, o_ref):
        o_ref[...] = _dot(a_ref[...], b_ref[...], TN).astype(bf16).reshape(o_ref.shape)

    return pl.pallas_call(
        body, name=name, grid=(M // tm, N // tn),
        in_specs=[pl.BlockSpec((S, tm), lambda i, j: (0, i)), pl.BlockSpec((S, tn), lambda i, j: (0, j)),
                  pl.BlockSpec(memory_space=pl.ANY)],
        out_specs=out_spec, out_shape=out_shape, compiler_params=_params(),
    )(a, b, after)


def _in_bwd_dh(dproj, w_in_g, after, tm=1024, tn=256):
    S = dproj.shape[0]
    _, D, wc = w_in_g.shape
    tm = min(tm, S)

    def body(dp_ref, w_ref, after_ref, o_ref):
        acc = _dot(dp_ref[:, 0:wc], w_ref[0], NT)
        for k in range(1, N_DEV):
            acc = acc + _dot(dp_ref[:, k * wc:(k + 1) * wc], w_ref[k], NT)
        o_ref[...] = acc

    return pl.pallas_call(
        body, name="in_bwd_dh", grid=(S // tm, D // tn),
        in_specs=[pl.BlockSpec((tm, N_DEV * wc), lambda i, j: (i, 0)),
                  pl.BlockSpec((N_DEV, tn, wc), lambda i, j: (0, j, 0)), pl.BlockSpec(memory_space=pl.ANY)],
        out_specs=pl.BlockSpec((tm, tn), lambda i, j: (i, j)),
        out_shape=SDS((S, D), f32), compiler_params=_params(),
    )(dproj, w_in_g, after)


def _rms_bwd(dh, x, g_row, dx_next, tm=256):
    S, D = x.shape

    def body(dh_ref, x_ref, g_ref, dxn_ref, dx_ref, dxb_ref, dg_ref):
        @pl.when(pl.program_id(0) == 0)
        def _():
            dg_ref[...] = jnp.zeros_like(dg_ref)

        dh = dh_ref[...]
        xv = x_ref[...]
        r = lax.rsqrt(jnp.mean(xv * xv, axis=-1, keepdims=True) + EPS)
        xhat = xv * r
        dxhat = dh * g_ref[...]
        dx = r * (dxhat - xhat * jnp.mean(dxhat * xhat, axis=-1, keepdims=True)) + dxn_ref[...]
        dx_ref[...] = dx
        dxb_ref[...] = dx.astype(bf16)
        dg_ref[...] += jnp.sum(dh * xhat, axis=0, keepdims=True)

    row = pl.BlockSpec((tm, D), lambda i: (i, 0))
    vec = pl.BlockSpec((1, D), lambda i: (0, 0))
    return pl.pallas_call(
        body, name="rms_bwd", grid=(S // tm,), in_specs=[row, row, vec, row], out_specs=[row, row, vec],
        out_shape=[SDS((S, D), f32), SDS((S, D), bf16), SDS((1, D), f32)], compiler_params=_params(),
    )(dh, x, g_row, dx_next)


def _sgu_fwd(proj, ln_g, ln_b, w_s, b_t):
    S = proj.shape[0]
    da = A_GROUPS * HEAD_DIM
    D = 2 * da

    def body(u_ref, v_ref, z_ref, lg_ref, lb_ref, w_ref, bt_ref, y_ref):
        u = _gelu(u_ref[...])
        v = _gelu(v_ref[...])
        z = z_ref[...]
        mu = jnp.mean(v, axis=-1, keepdims=True)
        xc = v - mu
        rs = lax.rsqrt(jnp.mean(xc * xc, axis=-1, keepdims=True) + EPS)
        vn = (xc * rs * lg_ref[...] + lb_ref[...]).astype(bf16)
        gate = u * (z * jax.nn.sigmoid(z))
        tri = lax.broadcasted_iota(jnp.int32, (CHUNK, CHUNK), 0) >= lax.broadcasted_iota(jnp.int32, (CHUNK, CHUNK), 1)
        for g in range(A_GROUPS):
            sl = slice(g * HEAD_DIM, (g + 1) * HEAD_DIM)
            wm = jnp.where(tri, w_ref[g], 0.0).astype(bf16)
            mixed = _dot(wm, vn[:, sl]) + bt_ref[:, g:g + 1]
            y_ref[:, sl] = (gate[:, sl] * mixed).astype(bf16)

    blk = lambda cb: pl.BlockSpec((CHUNK, da), lambda c: (c, cb))
    full = lambda shp: pl.BlockSpec(shp, lambda c: (0,) * len(shp))
    return pl.pallas_call(
        body, name="sgu_fwd", grid=(S // CHUNK,),
        in_specs=[blk(0), blk(1), blk(2), full((1, da)), full((1, da)),
                  full((A_GROUPS, CHUNK, CHUNK)), full((CHUNK, A_GROUPS))],
        out_specs=blk(0), out_shape=SDS((S, D), bf16), compiler_params=_params(),
    )(proj, proj, proj, ln_g, ln_b, w_s, b_t)


def _sgu_bwd(proj, dy, ln_g, ln_b, w_s, b_t):
    S = proj.shape[0]
    da = A_GROUPS * HEAD_DIM
    n_proj = proj.shape[1]

    def body(u_ref, v_ref, z_ref, dy_ref, lg_ref, lb_ref, w_ref, bt_ref,
             dp_ref, dw_ref, db_ref, dlg_ref, dlb_ref, dvn_ref):
        @pl.when(pl.program_id(0) == 0)
        def _():
            dw_ref[...] = jnp.zeros_like(dw_ref)
            db_ref[...] = jnp.zeros_like(db_ref)
            dlg_ref[...] = jnp.zeros_like(dlg_ref)
            dlb_ref[...] = jnp.zeros_like(dlb_ref)

        up, vp, z, dy = u_ref[...], v_ref[...], z_ref[...], dy_ref[...]
        u, gu = _gelu_and_grad(up)
        v, gv = _gelu_and_grad(vp)
        s, gs = _silu_and_grad(z)
        mu = jnp.mean(v, axis=-1, keepdims=True)
        xc = v - mu
        rs = lax.rsqrt(jnp.mean(xc * xc, axis=-1, keepdims=True) + EPS)
        vhat = xc * rs
        lg = lg_ref[...]
        vn = (vhat * lg + lb_ref[...]).astype(bf16)
        tri = lax.broadcasted_iota(jnp.int32, (CHUNK, CHUNK), 0) >= lax.broadcasted_iota(jnp.int32, (CHUNK, CHUNK), 1)
        lane = lax.broadcasted_iota(jnp.int32, (CHUNK, HEAD_DIM), 1)
        dys = dy * s
        db = jnp.zeros((CHUNK, HEAD_DIM), f32)
        for g in range(A_GROUPS):
            sl = slice(g * HEAD_DIM, (g + 1) * HEAD_DIM)
            wm = jnp.where(tri, w_ref[g], 0.0).astype(bf16)
            mixed = _dot(wm, vn[:, sl]) + bt_ref[:, g:g + 1]
            dmix = dys[:, sl] * u[:, sl]
            dp_ref[:, sl] = (dys[:, sl] * mixed * gu[:, sl]).astype(bf16)
            dp_ref[:, 2 * da + g * HEAD_DIM:2 * da + (g + 1) * HEAD_DIM] = (
                dy[:, sl] * u[:, sl] * mixed * gs[:, sl]).astype(bf16)
            dmb = dmix.astype(bf16)
            dw_ref[g] += jnp.where(tri, _dot(dmb, vn[:, sl], NT), 0.0)
            dvn_ref[:, sl] = _dot(wm, dmb, TN)
            db = db + jnp.where(lane == g, jnp.sum(dmix, axis=1, keepdims=True), 0.0)
        db_ref[...] += db
        dvn = dvn_ref[...]
        dlg_ref[...] += jnp.sum(dvn * vhat, axis=0, keepdims=True)
        dlb_ref[...] += jnp.sum(dvn, axis=0, keepdims=True)
        dvhat = dvn * lg
        dv = rs * (dvhat - jnp.mean(dvhat, axis=-1, keepdims=True)
                   - vhat * jnp.mean(dvhat * vhat, axis=-1, keepdims=True))
        dp_ref[:, da:2 * da] = (dv * gv).astype(bf16)

    blk = lambda cb: pl.BlockSpec((CHUNK, da), lambda c: (c, cb))
    full = lambda shp: pl.BlockSpec(shp, lambda c: (0,) * len(shp))
    return pl.pallas_call(
        body, name="sgu_bwd", grid=(S // CHUNK,),
        in_specs=[blk(0), blk(1), blk(2), blk(0), full((1, da)), full((1, da)),
                  full((A_GROUPS, CHUNK, CHUNK)), full((CHUNK, A_GROUPS))],
        out_specs=[pl.BlockSpec((CHUNK, 3 * da), lambda c: (c, 0)), full((A_GROUPS, CHUNK, CHUNK)),
                   full((CHUNK, HEAD_DIM)), full((1, da)), full((1, da))],
        out_shape=[SDS((S, n_proj), bf16), SDS((A_GROUPS, CHUNK, CHUNK), f32), SDS((CHUNK, HEAD_DIM), f32),
                   SDS((1, da), f32), SDS((1, da), f32)],
        scratch_shapes=[pltpu.VMEM((CHUNK, da), f32)], compiler_params=_params(),
    )(proj, proj, proj, dy, ln_g, ln_b, w_s, b_t)


def _sb_scores(q, kblk, kb, rows, cols, masked):
    z = _dot(q, kblk, NT) * (1.0 / math.sqrt(HEAD_DIM))
    t = jnp.log1p(jnp.exp(-jnp.abs(z)))
    log_1mb = -(jnp.maximum(z, 0.0) + t)
    log_beta = jnp.minimum(z, 0.0) - t
    if not masked:
        return None, log_beta, log_1mb
    causal = (cols + kb * TK) < rows
    return causal, log_beta, jnp.where(causal, log_1mb, 0.0)


def _sb_tiles(i):
    rows = i * TQ + lax.broadcasted_iota(jnp.int32, (TQ, TK), 0)
    cols = lax.broadcasted_iota(jnp.int32, (TQ, TK), 1)
    r_i = lax.broadcasted_iota(jnp.int32, (TK, TK), 0)
    c_i = lax.broadcasted_iota(jnp.int32, (TK, TK), 1)
    return rows, cols, (r_i > c_i).astype(bf16), (r_i < c_i).astype(bf16)


def _suffix_sum(t, tri):
    hi = t.astype(bf16)
    lo = (t - hi.astype(f32)).astype(bf16)
    return _dot(hi, tri) + _dot(lo, tri)


def _sb_fwd(proj, y_prev, col0, after):
    S = proj.shape[0]
    D = y_prev.shape[1]
    dh = N_HEADS * HEAD_DIM
    n_diag = TQ // TK

    def body(q_ref, k_ref, v_ref, z_ref, yp_ref, after_ref, y_ref, o_ref, car_ref, qb, kb_s, vb_s, c_ref):
        i = pl.program_id(0)

        @pl.when(i == 0)
        def _():
            kb_s[...] = k_ref[...].astype(bf16)
            vb_s[...] = v_ref[...].astype(bf16)

        qb[...] = q_ref[...].astype(bf16)
        o_ref[...] = jnp.zeros_like(o_ref)
        c_ref[...] = jnp.zeros_like(c_ref)
        car_ref[...] = jnp.zeros_like(car_ref)
        nkb = (i + 1) * n_diag
        rows, cols, upper, _ = _sb_tiles(i)

        def make_step(masked):
            def step(jj, carry):
                kb = nkb - 1 - jj
                off = pl.multiple_of(kb * TK, TK)
                hs = range(N_HEADS)
                sls = [slice(h * HEAD_DIM, (h + 1) * HEAD_DIM) for h in hs]
                sc = [_sb_scores(qb[:, sls[h]], kb_s[pl.ds(off, TK), sls[h]], kb, rows, cols, masked) for h in hs]
                suf = [_suffix_sum(sc[h][2], upper) for h in hs]
                cs = [c_ref[h] for h in hs]
                es = [jnp.exp(sc[h][1] + suf[h] + cs[h]) for h in hs]
                if masked:
                    es = [jnp.where(sc[h][0], es[h], 0.0) for h in hs]
                pv = [_dot(es[h].astype(bf16), vb_s[pl.ds(off, TK), sls[h]]) for h in hs]
                for h in hs:
                    o_ref[:, sls[h]] += pv[h]
                    car_ref[h] = jnp.where(cols == kb, cs[h], car_ref[h])
                    c_ref[h] = cs[h] + jnp.sum(sc[h][2], axis=1, keepdims=True)
                return carry
            return step

        lax.fori_loop(0, n_diag, make_step(True), 0)
        lax.fori_loop(n_diag, nkb, make_step(False), 0)
        z = z_ref[...]
        y_ref[...] = (o_ref[...] * (z * jax.nn.sigmoid(z))).astype(bf16)

    cb = col0 * HEAD_DIM // dh
    qspec = lambda k: pl.BlockSpec((TQ, dh), lambda i: (i, cb + k))
    kspec = lambda k: pl.BlockSpec((S, dh), lambda i: (0, cb + k))
    return pl.pallas_call(
        body, name="sb_fwd", grid=(S // TQ,),
        in_specs=[qspec(0), kspec(1), kspec(2), qspec(3), pl.BlockSpec(memory_space=pl.ANY),
                  pl.BlockSpec(memory_space=pl.ANY)],
        out_specs=[pl.BlockSpec((TQ, dh), lambda i: (i, A_GROUPS * HEAD_DIM // dh)),
                   pl.BlockSpec((TQ, dh), lambda i: (i, 0)),
                   pl.BlockSpec((N_HEADS, TQ, TK), lambda i: (0, i, 0))],
        out_shape=[SDS((S, D), bf16), SDS((S, dh), f32), SDS((N_HEADS, S, TK), f32)],
        input_output_aliases={4: 0},
        scratch_shapes=[pltpu.VMEM((TQ, dh), bf16), pltpu.VMEM((S, dh), bf16), pltpu.VMEM((S, dh), bf16),
                        pltpu.VMEM((N_HEADS, TQ, TK), f32)],
        compiler_params=_params(),
    )(proj, proj, proj, proj, y_prev, after)


def _sb_bwd(proj, o, car, dy, dproj_prev, col0, after):
    S = proj.shape[0]
    n_i = S // TQ
    dh = N_HEADS * HEAD_DIM
    n_diag = TQ // TK
    cb = col0 * HEAD_DIM // dh
    scale = 1.0 / math.sqrt(HEAD_DIM)

    def body(q_ref, k_ref, v_ref, z_ref, o_ref, car_ref, dy_ref, dpp_ref, after_ref,
             dp_ref, qb, kb_s, vb_s, dob, p_ref, dq_acc, dk_acc, dv_acc, st_a, st_b, st_k, st_v):
        i = pl.program_id(0)

        def put(stage_ref, row0, nrows, k):
            pltpu.sync_copy(stage_ref, dp_ref.at[pl.ds(row0, nrows), pl.ds((cb + k) * dh, dh)])

        @pl.when(i == 0)
        def _():
            kb_s[...] = k_ref[...].astype(bf16)
            vb_s[...] = v_ref[...].astype(bf16)
            dk_acc[...] = jnp.zeros_like(dk_acc)
            dv_acc[...] = jnp.zeros_like(dv_acc)

        s, gs = _silu_and_grad(z_ref[...])
        dy = dy_ref[...]
        st_b[...] = (dy * o_ref[...] * gs).astype(bf16)
        dob[...] = (dy * s).astype(bf16)
        qb[...] = q_ref[...].astype(bf16)
        p_ref[...] = jnp.zeros_like(p_ref)
        dq_acc[...] = jnp.zeros_like(dq_acc)
        nkb = (i + 1) * n_diag
        rows, cols, upper, lower = _sb_tiles(i)

        def make_step(masked):
            def step(kb, carry):
                off = pl.multiple_of(kb * TK, TK)
                hs = range(N_HEADS)
                sls = [slice(h * HEAD_DIM, (h + 1) * HEAD_DIM) for h in hs]
                qs = [qb[:, sls[h]] for h in hs]
                ks = [kb_s[pl.ds(off, TK), sls[h]] for h in hs]
                dos = [dob[:, sls[h]] for h in hs]
                sc = [_sb_scores(qs[h], ks[h], kb, rows, cols, masked) for h in hs]
                da = [_dot(dos[h], vb_s[pl.ds(off, TK), sls[h]], NT) for h in hs]
                suf = [_suffix_sum(sc[h][2], upper) for h in hs]
                onehot = cols == kb
                cs = [jnp.sum(jnp.where(onehot, car_ref[h], 0.0), axis=1, keepdims=True) for h in hs]
                es = [jnp.exp(sc[h][1] + suf[h] + cs[h]) for h in hs]
                if masked:
                    es = [jnp.where(sc[h][0], es[h], 0.0) for h in hs]
                gs_ = [da[h] * es[h] for h in hs]
                ps = [p_ref[h] for h in hs]
                pre = [_suffix_sum(gs_[h], lower) + ps[h] for h in hs]
                dzs = []
                for h in hs:
                    beta = jnp.exp(sc[h][1])
                    dzz = gs_[h] * (1.0 - beta) - beta * pre[h]
                    if masked:
                        dzz = jnp.where(sc[h][0], dzz, 0.0)
                    dzs.append((dzz * scale).astype(bf16))
                dqs = [_dot(dzs[h], ks[h]) for h in hs]
                dks = [_dot(dzs[h], qs[h], TN) for h in hs]
                dvs = [_dot(es[h].astype(bf16), dos[h], TN) for h in hs]
                for h in hs:
                    dq_acc[:, sls[h]] += dqs[h]
                    dk_acc[pl.ds(off, TK), sls[h]] += dks[h]
                    dv_acc[pl.ds(off, TK), sls[h]] += dvs[h]
                    p_ref[h] = ps[h] + jnp.sum(gs_[h], axis=1, keepdims=True)
                return carry
            return step

        lax.fori_loop(0, nkb - n_diag, make_step(False), 0)
        lax.fori_loop(nkb - n_diag, nkb, make_step(True), 0)
        st_a[...] = dq_acc[...].astype(bf16)
        row0 = pl.multiple_of(i * TQ, TQ)
        put(st_a, row0, TQ, 0)
        put(st_b, row0, TQ, 3)

        @pl.when(i == n_i - 1)
        def _():
            st_k[...] = dk_acc[...].astype(bf16)
            st_v[...] = dv_acc[...].astype(bf16)
            put(st_k, 0, S, 1)
            put(st_v, 0, S, 2)

    qspec = lambda k: pl.BlockSpec((TQ, dh), lambda i: (i, cb + k))
    kspec = lambda k: pl.BlockSpec((S, dh), lambda i: (0, cb + k))
    return pl.pallas_call(
        body, name="sb_bwd", grid=(n_i,),
        in_specs=[qspec(0), kspec(1), kspec(2), qspec(3),
                  pl.BlockSpec((TQ, dh), lambda i: (i, 0)),
                  pl.BlockSpec((N_HEADS, TQ, TK), lambda i: (0, i, 0)),
                  pl.BlockSpec((TQ, dh), lambda i: (i, A_GROUPS * HEAD_DIM // dh)),
                  pl.BlockSpec(memory_space=pl.ANY), pl.BlockSpec(memory_space=pl.ANY)],
        out_specs=pl.BlockSpec(memory_space=pl.ANY),
        out_shape=SDS(dproj_prev.shape, bf16),
        input_output_aliases={7: 0},
        scratch_shapes=[pltpu.VMEM((TQ, dh), bf16), pltpu.VMEM((S, dh), bf16), pltpu.VMEM((S, dh), bf16),
                        pltpu.VMEM((TQ, dh), bf16), pltpu.VMEM((N_HEADS, TQ, TK), f32), pltpu.VMEM((TQ, dh), f32),
                        pltpu.VMEM((S, dh), f32), pltpu.VMEM((S, dh), f32),
                        pltpu.VMEM((TQ, dh), bf16), pltpu.VMEM((TQ, dh), bf16),
                        pltpu.VMEM((S, dh), bf16), pltpu.VMEM((S, dh), bf16)],
        compiler_params=_params(56),
    )(proj, proj, proj, proj, o, car, dy, dproj_prev, after)


def _mem_kv(mem, mg_row, w_kv_g):
    M, D = mem.shape
    rb, n = w_kv_g.shape[1], w_kv_g.shape[2]

    def body(m_ref, g_ref, w_ref, kv_ref):
        mv = m_ref[...]
        r = lax.rsqrt(jnp.mean(mv * mv, axis=-1, keepdims=True) + EPS)
        mh = (mv * r * g_ref[...]).astype(bf16)
        kv_ref[...] = _dot(mh, w_ref[...].reshape(N_DEV * rb, n))

    return pl.pallas_call(
        body, name="mem_kv", grid=(1,),
        in_specs=[pl.BlockSpec((M, D), lambda i: (0, 0)), pl.BlockSpec((1, D), lambda i: (0, 0)),
                  pl.BlockSpec((N_DEV, rb, n), lambda i: (0, 0, 0))],
        out_specs=pl.BlockSpec((M, n), lambda i: (0, 0)),
        out_shape=SDS((M, n), f32), compiler_params=_params(),
    )(mem, mg_row, w_kv_g)


def _xattn_head(q_ref, kv_ref, qg, kg, h):
    dc = N_HEADS * HEAD_DIM
    sl = slice(h * HEAD_DIM, (h + 1) * HEAD_DIM)
    qh = q_ref[:, sl]
    rq = lax.rsqrt(jnp.mean(qh * qh, axis=-1, keepdims=True) + EPS)
    qhat = qh * rq
    qn = (qhat * qg).astype(bf16)
    kh = kv_ref[:, sl]
    rk = lax.rsqrt(jnp.mean(kh * kh, axis=-1, keepdims=True) + EPS)
    kn = (kh * rk * kg).astype(bf16)
    vh = kv_ref[:, dc + h * HEAD_DIM:dc + (h + 1) * HEAD_DIM].astype(bf16)
    s = _dot(qn, kn, NT) * (1.0 / math.sqrt(HEAD_DIM))
    e = jnp.exp(s - jnp.max(s, axis=-1, keepdims=True))
    p = e / jnp.sum(e, axis=-1, keepdims=True)
    o = _dot(p.astype(bf16), vh)
    return sl, rq, qhat, qn, kn, vh, p, o


def _xattn_fwd(proj, kv, qg_row, kg_row, y_prev, col0, tq=512):
    S = proj.shape[0]
    D = y_prev.shape[1]
    dc = N_HEADS * HEAD_DIM
    M = kv.shape[0]

    def body(q_ref, z_ref, kv_ref, qg_ref, kg_ref, yp_ref, y_ref):
        for h in range(N_HEADS):
            sl, _, _, _, _, _, _, o = _xattn_head(q_ref, kv_ref, qg_ref[...], kg_ref[...], h)
            z = z_ref[:, sl]
            y_ref[:, sl] = (o * (z * jax.nn.sigmoid(z))).astype(bf16)

    full = lambda shp: pl.BlockSpec(shp, lambda i: (0,) * len(shp))
    return pl.pallas_call(
        body, name="xattn_fwd", grid=(S // tq,),
        in_specs=[pl.BlockSpec((tq, dc), lambda i: (i, col0)), pl.BlockSpec((tq, dc), lambda i: (i, col0 + 1)),
                  full((M, 2 * dc)), full((1, HEAD_DIM)), full((1, HEAD_DIM)), pl.BlockSpec(memory_space=pl.ANY)],
        out_specs=pl.BlockSpec((tq, dc), lambda i: (i, D // dc - 1)),
        out_shape=SDS((S, D), bf16), input_output_aliases={5: 0}, compiler_params=_params(),
    )(proj, proj, kv, qg_row, kg_row, y_prev)


def _xattn_bwd(proj, kv, qg_row, kg_row, dy, dproj_prev, col0, tq=512):
    S = proj.shape[0]
    D = dy.shape[1]
    dc = N_HEADS * HEAD_DIM
    M = kv.shape[0]

    def body(q_ref, z_ref, kv_ref, qg_ref, kg_ref, dy_ref, dpp_ref, dp_ref, dkn_ref, dv_ref, dqg_ref):
        @pl.when(pl.program_id(0) == 0)
        def _():
            dkn_ref[...] = jnp.zeros_like(dkn_ref)
            dv_ref[...] = jnp.zeros_like(dv_ref)
            dqg_ref[...] = jnp.zeros_like(dqg_ref)

        qg = qg_ref[...]
        for h in range(N_HEADS):
            sl, rq, qhat, qn, kn, vh, p, o = _xattn_head(q_ref, kv_ref, qg, kg_ref[...], h)
            s, gs = _silu_and_grad(z_ref[:, sl])
            dyh = dy_ref[:, sl]
            dp_ref[:, dc + h * HEAD_DIM:dc + (h + 1) * HEAD_DIM] = (dyh * o * gs).astype(bf16)
            dob = (dyh * s).astype(bf16)
            dpr = _dot(dob, vh, NT)
            dv_ref[:, sl] += _dot(p.astype(bf16), dob, TN)
            ds = (p * (dpr - jnp.sum(p * dpr, axis=-1, keepdims=True)) * (1.0 / math.sqrt(HEAD_DIM))).astype(bf16)
            dqn = _dot(ds, kn)
            dkn_ref[:, sl] += _dot(ds, qn, TN)
            dqg_ref[...] += jnp.sum(dqn * qhat, axis=0, keepdims=True)
            dqhat = dqn * qg
            dp_ref[:, sl] = (rq * (dqhat - qhat * jnp.mean(dqhat * qhat, axis=-1, keepdims=True))).astype(bf16)

    full = lambda shp: pl.BlockSpec(shp, lambda i: (0,) * len(shp))
    return pl.pallas_call(
        body, name="xattn_bwd", grid=(S // tq,),
        in_specs=[pl.BlockSpec((tq, dc), lambda i: (i, col0)), pl.BlockSpec((tq, dc), lambda i: (i, col0 + 1)),
                  full((M, 2 * dc)), full((1, HEAD_DIM)), full((1, HEAD_DIM)),
                  pl.BlockSpec((tq, dc), lambda i: (i, D // dc - 1)), pl.BlockSpec(memory_space=pl.ANY)],
        out_specs=[pl.BlockSpec((tq, 2 * dc), lambda i: (i, col0 // 2)), full((M, dc)), full((M, dc)),
                   full((1, HEAD_DIM))],
        out_shape=[SDS(dproj_prev.shape, bf16), SDS((M, dc), f32), SDS((M, dc), f32), SDS((1, HEAD_DIM), f32)],
        input_output_aliases={6: 0}, compiler_params=_params(),
    )(proj, proj, kv, qg_row, kg_row, dy, dproj_prev)


def _mem_bwd(mem, mg_row, kv, dkn, dv, kg_row, w_kv_g):
    M, D = mem.shape
    rb, n = w_kv_g.shape[1], w_kv_g.shape[2]
    dc = n // 2

    def body(m_ref, g_ref, kv_ref, dkn_ref, dv_ref, kg_ref, w_ref, dw_ref, dmg_ref, dkg_ref, dkv_ref):
        mv = m_ref[...]
        r = lax.rsqrt(jnp.mean(mv * mv, axis=-1, keepdims=True) + EPS)
        mhat = mv * r
        mh = (mhat * g_ref[...]).astype(bf16)
        kg = kg_ref[...]
        dkg = jnp.zeros((1, HEAD_DIM), f32)
        for h in range(N_HEADS):
            sl = slice(h * HEAD_DIM, (h + 1) * HEAD_DIM)
            kh = kv_ref[:, sl]
            rk = lax.rsqrt(jnp.mean(kh * kh, axis=-1, keepdims=True) + EPS)
            khat = kh * rk
            dkn_h = dkn_ref[:, sl]
            dkg = dkg + jnp.sum(dkn_h * khat, axis=0, keepdims=True)
            dkhat = dkn_h * kg
            dkv_ref[:, sl] = (rk * (dkhat - khat * jnp.mean(dkhat * khat, axis=-1, keepdims=True))).astype(bf16)
        dkv_ref[:, dc:] = dv_ref[...].astype(bf16)
        dkg_ref[...] = dkg
        dkv = dkv_ref[...]
        dw_ref[...] = _dot(mh, dkv, TN).astype(bf16).reshape(N_DEV, rb, n)
        dmh = _dot(dkv, w_ref[...].reshape(N_DEV * rb, n), NT)
        dmg_ref[...] = jnp.sum(dmh * mhat, axis=0, keepdims=True)

    full = lambda shp: pl.BlockSpec(shp, lambda i: (0,) * len(shp))
    wspec = full((N_DEV, rb, n))
    return pl.pallas_call(
        body, name="mem_bwd", grid=(1,),
        in_specs=[full((M, D)), full((1, D)), full((M, n)), full((M, dc)), full((M, dc)), full((1, HEAD_DIM)), wspec],
        out_specs=[wspec, full((1, D)), full((1, HEAD_DIM))],
        out_shape=[SDS((N_DEV, rb, n), bf16), SDS((1, D), f32), SDS((1, HEAD_DIM), f32)],
        scratch_shapes=[pltpu.VMEM((M, n), bf16)], compiler_params=_params(),
    )(mem, mg_row, kv, dkn, dv, kg_row, w_kv_g)


SMALL = ("norm_g", "sgu_ln_g", "sgu_ln_b", "sgu_w", "sgu_b", "mem_norm_g", "q_norm_g", "k_norm_g")


def _pack_small(parts):
    flat = jnp.concatenate([parts[n].reshape(-1) for n in SMALL])
    pad = (-flat.shape[0]) % (8 * 128)
    return jnp.pad(flat, (0, pad)).reshape(-1, 128)


def _unpack_small(packed, like):
    flat = packed.reshape(-1)
    out, off = {}, 0
    for n in SMALL:
        size = math.prod(like[n].shape)
        out[n] = flat[off:off + size].reshape(like[n].shape)
        off += size
    return out


def kernel(x, mem, norm_g, w_in, sgu_ln_g, sgu_ln_b, sgu_w, sgu_b, mem_norm_g, w_mem_kv, q_norm_g, k_norm_g, w_out, loss_target, m_norm_g, m_w_in, m_sgu_ln_g, m_sgu_ln_b, m_sgu_w, m_sgu_b, m_mem_norm_g, m_w_mem_kv, m_q_norm_g, m_k_norm_g, m_w_out, v_norm_g, v_w_in, v_sgu_ln_g, v_sgu_ln_b, v_sgu_w, v_sgu_b, v_mem_norm_g, v_w_mem_kv, v_q_norm_g, v_k_norm_g, v_w_out):
    L, D, wc = w_in.shape
    S = x.shape[1]
    da = D // 2
    xs = x.reshape(S, D)
    mems = mem.reshape(mem.shape[1], D)
    tgt = loss_target.reshape(S, D)
    row = lambda a, l: a[l].reshape(1, -1)
    tie = lambda a, tok: a + tok[0:1, 0:1]
    sb_col, xa_col = 3 * da // HEAD_DIM, (3 * da + D) // (D // 4)

    ax, ay, ac = lax.axis_index("x"), lax.axis_index("y"), lax.axis_index("c")
    ids = jnp.stack([4 * ax + 2 * ay + ac, 2 * ax + ay, ac]).astype(jnp.int32)
    w_b = [(_cast_into_slot("cast_w_in", w_in, l, 512, ids), _cast_into_slot("cast_w_kv", w_mem_kv, l, 256, ids),
            _cast_into_slot("cast_w_out", w_out, l, 256, ids)) for l in range(L)]
    first = _gather2_start("gather_w_in0", [w_b[0][0]], ids)
    in_fwd = _gather2_forward("gather_w_in0_forward", _split_wait(first, first["token"]), ids)

    acts = []
    xl = xs
    for l in range(L):
        g_row = row(norm_g, l)
        (w_in_g,) = _split_wait(in_fwd, g_row if l else in_fwd["token"])
        rest = _gather2_start(f"gather_w_rest{l}", [w_b[l][1], w_b[l][2]], w_in_g)
        proj, h = _rms_proj(xl, tie(g_row, rest["token"]), w_in_g)
        rest_fwd = _gather2_forward(f"gather_w_rest{l}_forward", _split_wait(rest, proj), proj)
        lg_row = tie(row(sgu_ln_g, l), rest_fwd["token"])
        if l + 1 < L:
            nxt = _gather2_start(f"gather_w_in{l + 1}", [w_b[l + 1][0]], proj)
            lg_row = tie(lg_row, nxt["token"])
        y = _sgu_fwd(proj, lg_row, row(sgu_ln_b, l), sgu_w[l], sgu_b[l].T)
        y, o_b, car = _sb_fwd(proj, y, sb_col, lg_row)
        w_kv_g, w_out_g = _split_wait(rest_fwd, o_b)
        kv = _mem_kv(mems, row(mem_norm_g, l), w_kv_g)
        y = _xattn_fwd(proj, kv, row(q_norm_g, l), row(k_norm_g, l), y, xa_col)
        order = kv
        if l + 1 < L:
            in_fwd = _gather2_forward(f"gather_w_in{l + 1}_forward", _split_wait(nxt, y), y)
            order = in_fwd["token"]
        x_next = _out_proj(xl, y, w_out_g, order)
        acts.append((xl, proj, h, y, o_b, car, kv, w_in_g, w_kv_g, w_out_g))
        xl = x_next

    dx, dxb, loss_part = _loss_and_grad(xl, tgt, 512)
    loss = lax.psum(loss_part[0, 0], ("x", "y", "c"))

    weights = dict(norm_g=norm_g, sgu_ln_g=sgu_ln_g, sgu_ln_b=sgu_ln_b, sgu_w=sgu_w, sgu_b=sgu_b,
                   mem_norm_g=mem_norm_g, q_norm_g=q_norm_g, k_norm_g=k_norm_g)
    moms_m = dict(norm_g=m_norm_g, sgu_ln_g=m_sgu_ln_g, sgu_ln_b=m_sgu_ln_b, sgu_w=m_sgu_w, sgu_b=m_sgu_b,
                  mem_norm_g=m_mem_norm_g, q_norm_g=m_q_norm_g, k_norm_g=m_k_norm_g)
    moms_v = dict(norm_g=v_norm_g, sgu_ln_g=v_sgu_ln_g, sgu_ln_b=v_sgu_ln_b, sgu_w=v_sgu_w, sgu_b=v_sgu_b,
                  mem_norm_g=v_mem_norm_g, q_norm_g=v_q_norm_g, k_norm_g=v_k_norm_g)
    head_rows = 16
    assert D // 128 <= head_rows

    seconds = {}
    pending = None
    small = {n: [None] * L for n in SMALL}
    for l in reversed(range(L)):
        xl, proj, h, y, o_b, car, kv, w_in_g, w_kv_g, w_out_g = acts[l]
        dy = _out_bwd_dy(dxb, w_out_g)
        order = dy
        if pending is not None:
            seconds[pending[0]] = _scatter2_second_level(pending[0], pending[1], dy, ids)
            order = seconds[pending[0]]["token"]
        g_out = _tn_grad("out_bwd_dw", y, dxb, 512, 512, True, order)
        first = _scatter2_pair_start(f"scatter_g_out{l}_pair", [g_out], ids)
        dproj, d_sw, d_sb, d_lg, d_lb = _sgu_bwd(proj, dy, tie(row(sgu_ln_g, l), first["token"]), row(sgu_ln_b, l),
                                                 sgu_w[l], sgu_b[l].T)
        seconds[f"g_out{l}"] = _scatter2_second_level(f"g_out{l}", first, dproj, ids)
        dproj = _sb_bwd(proj, o_b, car, dy, dproj, sb_col, seconds[f"g_out{l}"]["token"])
        dproj, dkn, dv, d_qg = _xattn_bwd(proj, kv, row(q_norm_g, l), row(k_norm_g, l), dy, dproj, xa_col)
        g_kv, d_mg, d_kg = _mem_bwd(mems, row(mem_norm_g, l), kv, dkn, dv, row(k_norm_g, l), w_kv_g)
        for n, val in (("sgu_ln_g", d_lg), ("sgu_ln_b", d_lb), ("sgu_w", d_sw), ("sgu_b", d_sb[:, :A_GROUPS].T),
                       ("mem_norm_g", d_mg), ("q_norm_g", d_qg), ("k_norm_g", d_kg)):
            small[n][l] = val.reshape(-1)
        order = d_kg
        if l == 0:
            small["norm_g"][0] = jnp.zeros_like(small["norm_g"][1])
            part = _pack_small({n: jnp.stack(small[n]).reshape(weights[n].shape) for n in SMALL})
            tail = _gather2_start("gather_small_tail", [_into_slot("small_tail_slot", part[head_rows:], ids)], ids)
            order = tail["token"]
        g_in_l = _tn_grad("in_bwd_dw", h, dproj, 1024, wc, False, order)
        first = _scatter2_pair_start(f"scatter_g_rest{l}_pair", [g_kv, g_in_l], ids)
        g_row = tie(row(norm_g, l), first["token"])
        pending = (f"g_rest{l}", first)
        if l == 0:
            tail_fwd = _gather2_forward("gather_small_tail_forward", _split_wait(tail, g_in_l), g_in_l)
            seconds[pending[0]] = _scatter2_second_level(pending[0], pending[1], tail_fwd["token"], ids)
            g_row = tie(g_row, seconds[pending[0]]["token"])
        dh = _in_bwd_dh(dproj, w_in_g, g_row)
        dx, dxb, d_ng = _rms_bwd(dh, xl, g_row, dx)
        small["norm_g"][l] = d_ng.reshape(-1)
    head_part = jnp.concatenate([small["norm_g"][0].reshape(-1, 128), jnp.zeros((head_rows - D // 128, 128), f32)])
    head = _gather2_start("gather_small_head", [_into_slot("small_head_slot", head_part, ids)], ids)
    head_fwd = _gather2_forward("gather_small_head_forward", _split_wait(head, head["token"]), ids)

    (r_tail,) = _split_wait(tail_fwd, dx)
    (r_head,) = _split_wait(head_fwd, r_tail)
    sm = _small_sum_adam(r_head, r_tail, _pack_small(weights), _pack_small(moms_m), _pack_small(moms_v))

    a_out = a_kv = a_in = None
    order = sm[0]
    for l in reversed(range(L)):
        (r_out,) = _scatter2_finish(seconds[f"g_out{l}"], order)
        a_out = _sum_adam("adam_w_out", r_out, w_out, m_w_out, v_w_out, l, a_out, 128, ids)
        r_kv, r_in = _scatter2_finish(seconds[f"g_rest{l}"], a_out[0])
        a_kv = _sum_adam("adam_w_kv", r_kv, w_mem_kv, m_w_mem_kv, v_w_mem_kv, l, a_kv, 256, ids)
        a_in = _sum_adam("adam_w_in", r_in, w_in, m_w_in, v_w_in, l, a_in, 256, ids)
        order = a_in[0]
    res = {"w_out": a_out, "w_mem_kv": a_kv, "w_in": a_in}
    sm = [_unpack_small(a, weights) for a in sm]
    for n in SMALL:
        res[n] = [a[n] for a in sm]

    order = ("norm_g", "w_in", "sgu_ln_g", "sgu_ln_b", "sgu_w", "sgu_b", "mem_norm_g", "w_mem_kv", "q_norm_g",
             "k_norm_g", "w_out")
    outs = [loss, dx.reshape(x.shape)]
    for k in range(4):
        outs += [res[n][k] for n in order]
    return tuple(outs)
```

```python
import functools
import math

import jax
import jax.numpy as jnp
from jax import lax
from jax.experimental import pallas as pl
from jax.experimental.pallas import tpu as pltpu

f32 = jnp.float32
bf16 = jnp.bfloat16
SDS = jax.ShapeDtypeStruct

N_DEV = 8
EPS = 1e-6
CHUNK = 128
A_GROUPS = 8
HEAD_DIM = 128
N_HEADS = 4
TQ = 256
TK = 128
ADAM_LR, ADAM_B1, ADAM_B2, ADAM_EPS, ADAM_WD, ADAM_STEP = 0.001, 0.9, 0.999, 1e-08, 0.01, 10
MIB = 1024 * 1024

NT = (((1,), (1,)), ((), ()))
TN = (((0,), (0,)), ((), ()))


def _params(vmem_mib=48):
    return pltpu.CompilerParams(vmem_limit_bytes=vmem_mib * MIB)


def _gelu_and_grad(x):
    e = lax.erf(x * (1.0 / math.sqrt(2.0)))
    cdf = 0.5 * (1.0 + e)
    pdf = jnp.exp(-0.5 * x * x) * (1.0 / math.sqrt(2.0 * math.pi))
    return x * cdf, cdf + x * pdf


def _gelu(x):
    return 0.5 * x * (1.0 + lax.erf(x * (1.0 / math.sqrt(2.0))))


def _silu_and_grad(z):
    sg = jax.nn.sigmoid(z)
    return z * sg, sg * (1.0 + z * (1.0 - sg))


def _dot(a, b, dims=None):
    if dims is None:
        return jnp.dot(a, b, preferred_element_type=f32)
    return lax.dot_general(a, b, dims, preferred_element_type=f32)


_HBM = pl.BlockSpec(memory_space=pltpu.HBM)
_SEM = pl.BlockSpec(memory_space=pltpu.SEMAPHORE)
_EFFECT = pltpu.SideEffectType.DATAFLOW_SIDE_EFFECTING


def _split_start(name, bufs, n_remote, n_local, build, after):
    nb = len(bufs)

    def body(*refs):
        token = refs[-1]
        locals_, remotes = build(refs[:nb], *refs[nb + 1:nb + 4])
        for cp in locals_ + remotes:
            cp.start()
        token[...] = jnp.zeros_like(token)

    hbm = lambda a: pltpu.with_memory_space_constraint(a, pltpu.HBM)
    outs = pl.pallas_call(
        body, name=name,
        out_shape=(pltpu.SemaphoreType.DMA((n_remote,)), pltpu.SemaphoreType.DMA((n_remote,)),
                   pltpu.SemaphoreType.DMA((max(n_local, 1),)),
                   *[pltpu.HBM(b.shape, b.dtype) for b in bufs], SDS((8, 128), f32)),
        in_specs=[_HBM] * nb + [pl.BlockSpec(memory_space=pl.ANY)],
        out_specs=(_SEM, _SEM, _SEM, *[_HBM] * nb, pl.BlockSpec(memory_space=pltpu.VMEM)),
        input_output_aliases={k: 3 + k for k in range(nb)},
        compiler_params=pltpu.CompilerParams(has_side_effects=_EFFECT),
    )(*[hbm(b) for b in bufs], after)
    return dict(name=name, build=build, sems=outs[:3], bufs=outs[3:3 + nb], token=outs[-1])


def _split_wait(handle, after):
    build, bufs = handle["build"], handle["bufs"]
    nb = len(bufs)

    def body(*refs):
        locals_, remotes = build(refs[:nb], *refs[nb:nb + 3])
        for cp in remotes:
            cp.wait_recv()
        for cp in remotes:
            cp.wait_send()
        for cp in locals_:
            cp.wait()

    outs = pl.pallas_call(
        body, name=handle["name"] + "_wait",
        out_shape=tuple(pltpu.HBM(b.shape, b.dtype) for b in bufs),
        in_specs=[_HBM] * nb + [_SEM] * 3 + [pl.BlockSpec(memory_space=pl.ANY)],
        out_specs=tuple([_HBM] * nb),
        input_output_aliases={k: k for k in range(nb)},
        compiler_params=pltpu.CompilerParams(has_side_effects=_EFFECT),
    )(*bufs, *handle["sems"], after)
    return list(outs)


def _remote(src, dst, send_sems, recv_sems, k, to):
    return pltpu.make_async_remote_copy(src_ref=src, dst_ref=dst, send_sem=send_sems.at[k], recv_sem=recv_sems.at[k],
                                        device_id=to, device_id_type=pl.DeviceIdType.MESH)


def _other_chips(x, y):
    return [(1 - x, y), (x, 1 - y), (1 - x, 1 - y)]


def _gather2_start(name, lands, after):
    def build(refs, send, recv, loc):
        x, y, c = lax.axis_index("x"), lax.axis_index("y"), lax.axis_index("c")
        me = 4 * x + 2 * y + c
        remotes = []
        for a, d in enumerate(refs):
            remotes.append(_remote(d.at[me], d.at[me], send, recv, 4 * a, (x, y, 1 - c)))
            remotes += [_remote(d.at[me], d.at[me], send, recv, 4 * a + 1 + k, (px, py, c))
                        for k, (px, py) in enumerate(_other_chips(x, y))]
        return [], remotes

    return _split_start(name, list(lands), 4 * len(lands), 0, build, after)


def _gather2_forward(name, lands, after):
    n = len(lands)

    def build(refs, send, recv, loc):
        x, y, c = lax.axis_index("x"), lax.axis_index("y"), lax.axis_index("c")
        slots = [4 * px + 2 * py + c for px, py in _other_chips(x, y)]
        return [], [_remote(d.at[sl], d.at[sl], send, recv, 3 * a + k, (x, y, 1 - c))
                    for a, d in enumerate(refs) for k, sl in enumerate(slots)]

    return _split_start(name, list(lands), 3 * n, 0, build, after)


def _scatter2_pair_start(name, srcs, after):
    n = len(srcs)

    def build(refs, send, recv, loc):
        x, y, c = lax.axis_index("x"), lax.axis_index("y"), lax.axis_index("c")
        return [], [_remote(refs[a].at[2 * q + 1 - c], refs[n + a].at[q], send, recv, 4 * a + q, (x, y, 1 - c))
                    for a in range(n) for q in range(4)]

    lands = [lax.empty((4,) + s.shape[1:], s.dtype) for s in srcs]
    return _split_start(name, list(srcs) + lands, 4 * n, 0, build, after)


def _scatter2_chip_start(name, pairs, after):
    n = len(pairs)

    def build(refs, send, recv, loc):
        x, y, c = lax.axis_index("x"), lax.axis_index("y"), lax.axis_index("c")
        return [], [_remote(refs[a].at[2 * px + py], refs[n + a].at[2 * x + y], send, recv, 3 * a + k, (px, py, c))
                    for a in range(n) for k, (px, py) in enumerate(_other_chips(x, y))]

    return _split_start(name, list(pairs) + [lax.empty(p.shape, p.dtype) for p in pairs], 3 * n, 0, build, after)


def _pair_sum(name, src, theirs, ids):
    _, R, C = theirs.shape
    tr = min(R, 1024)

    def body(ids_ref, a_ref, b_ref, o_ref):
        o_ref[...] = (a_ref[...].astype(f32) + b_ref[...].astype(f32)).astype(bf16)

    spec = pl.BlockSpec((None, tr, C), lambda q, i, ids: (q, i, 0))
    return pl.pallas_call(
        body, name=name,
        grid_spec=pltpu.PrefetchScalarGridSpec(
            num_scalar_prefetch=1, grid=(4, R // tr),
            in_specs=[pl.BlockSpec((None, tr, C), lambda q, i, ids: (2 * q + ids[2], i, 0)), spec], out_specs=spec),
        out_shape=SDS(theirs.shape, bf16), compiler_params=_params(),
    )(ids, src, theirs)


def _scatter2_second_level(name, first, after, ids):
    outs = _split_wait(first, after)
    n = len(outs) // 2
    pairs = [_pair_sum(f"pair_sum_{name}{a}", outs[a], outs[n + a], ids) for a in range(n)]
    return _scatter2_chip_start(f"scatter_{name}_chip", pairs, ids)


def _scatter2_finish(second, after):
    outs = _split_wait(second, after)
    n = len(outs) // 2
    return [(outs[a], outs[n + a]) for a in range(n)]


def _cast_into_slot(name, w, l, tr, ids):
    _, R, C = w.shape

    def body(ids_ref, w_ref, o_ref):
        o_ref[...] = w_ref[...].astype(bf16)

    return pl.pallas_call(
        body, name=name,
        grid_spec=pltpu.PrefetchScalarGridSpec(
            num_scalar_prefetch=1, grid=(R // tr,),
            in_specs=[pl.BlockSpec((None, tr, C), lambda i, ids: (l, i, 0))],
            out_specs=pl.BlockSpec((None, tr, C), lambda i, ids: (ids[0], i, 0))),
        out_shape=SDS((N_DEV, R, C), bf16), compiler_params=_params(),
    )(ids, w)


def _into_slot(name, a, ids):
    R, C = a.shape

    def body(ids_ref, a_ref, o_ref):
        o_ref[...] = a_ref[...]

    return pl.pallas_call(
        body, name=name,
        grid_spec=pltpu.PrefetchScalarGridSpec(
            num_scalar_prefetch=1, grid=(1,),
            in_specs=[pl.BlockSpec((R, C), lambda i, ids: (0, 0))],
            out_specs=pl.BlockSpec((None, R, C), lambda i, ids: (ids[0], 0, 0))),
        out_shape=SDS((N_DEV, R, C), f32), compiler_params=_params(),
    )(ids, a)


def _adam_math(w, g, m, v):
    m2 = ADAM_B1 * m + (1.0 - ADAM_B1) * g
    v2 = ADAM_B2 * v + (1.0 - ADAM_B2) * (g * g)
    m_hat = m2 / (1.0 - ADAM_B1 ** ADAM_STEP)
    v_hat = v2 / (1.0 - ADAM_B2 ** ADAM_STEP)
    delta = -ADAM_LR * (m_hat / (jnp.sqrt(v_hat) + ADAM_EPS) + ADAM_WD * w)
    return delta, m2, v2


def _sum_adam(name, pair_recv, w, m, v, l, prev, tr, ids):
    own, recv = pair_recv
    L, R, C = w.shape
    slots = recv.shape[0]

    def body(ids_ref, r_ref, own_ref, w_ref, m_ref, v_ref, *rest):
        g_ref, d_ref, m2_ref, v2_ref = rest[-4:]
        terms = [jnp.where(ids_ref[1] == q, own_ref[...], r_ref[q]).astype(f32) for q in range(slots)]
        g = terms[0]
        for t in terms[1:]:
            g = g + t
        d, m2, v2 = _adam_math(w_ref[...], g, m_ref[...], v_ref[...])
        g_ref[...] = g
        d_ref[...] = d
        m2_ref[...] = m2
        v2_ref[...] = v2

    wspec = pl.BlockSpec((None, tr, C), lambda i, ids: (l, i, 0))
    in_specs = [pl.BlockSpec((slots, tr, C), lambda i, ids: (0, i, 0)),
                pl.BlockSpec((None, tr, C), lambda i, ids: (ids[1], i, 0)), wspec, wspec, wspec]
    args = [ids, recv, own, w, m, v]
    aliases = {}
    if prev is not None:
        in_specs += [pl.BlockSpec(memory_space=pl.ANY)] * 4
        args += list(prev)
        aliases = {6 + k: k for k in range(4)}
    return pl.pallas_call(
        body, name=name,
        grid_spec=pltpu.PrefetchScalarGridSpec(num_scalar_prefetch=1, grid=(R // tr,), in_specs=in_specs,
                                               out_specs=[wspec] * 4),
        out_shape=[SDS((L, R, C), f32)] * 4, input_output_aliases=aliases, compiler_params=_params(),
    )(*args)


def _small_sum_adam(recv_head, recv_tail, w, m, v):
    r0 = recv_head.shape[1]
    R, C = w.shape

    def body(rh_ref, rt_ref, w_ref, m_ref, v_ref, g_ref, d_ref, m2_ref, v2_ref):
        for r_ref, rows in ((rh_ref, slice(0, r0)), (rt_ref, slice(r0, R))):
            g = r_ref[0]
            for s in range(1, N_DEV):
                g = g + r_ref[s]
            d, m2, v2 = _adam_math(w_ref[rows, :], g, m_ref[rows, :], v_ref[rows, :])
            g_ref[rows, :] = g
            d_ref[rows, :] = d
            m2_ref[rows, :] = m2
            v2_ref[rows, :] = v2

    return pl.pallas_call(
        body, name="small_sum_adam", out_shape=[SDS((R, C), f32)] * 4, compiler_params=_params(),
    )(recv_head, recv_tail, w, m, v)


def _loss_and_grad(xf, tgt, tm):
    S, D = xf.shape

    def body(x_ref, t_ref, dx_ref, dxb_ref, l_ref):
        i = pl.program_id(0)
        d = x_ref[...] - t_ref[...]
        dx = d * (1.0 / D)
        dx_ref[...] = dx
        dxb_ref[...] = dx.astype(bf16)
        e = d * d
        part = e[:, 0:128]
        for k in range(1, D // 128):
            part = part + e[:, k * 128:(k + 1) * 128]
        part = jnp.sum(part.reshape(tm // 8, 8, 128), axis=0)

        @pl.when(i == 0)
        def _():
            l_ref[...] = jnp.zeros_like(l_ref)

        l_ref[...] += part

        @pl.when(i == pl.num_programs(0) - 1)
        def _():
            tot = jnp.sum(l_ref[...], axis=1, keepdims=True)
            tot = jnp.sum(tot, axis=0, keepdims=True)
            l_ref[...] = jnp.broadcast_to(tot * (0.5 / D), l_ref.shape)

    row = pl.BlockSpec((tm, D), lambda i: (i, 0))
    return pl.pallas_call(
        body, name="loss_grad", grid=(S // tm,),
        in_specs=[row, row], out_specs=[row, row, pl.BlockSpec((8, 128), lambda i: (0, 0))],
        out_shape=[SDS((S, D), f32), SDS((S, D), bf16), SDS((8, 128), f32)], compiler_params=_params(),
    )(xf, tgt)


def _rms_proj(x, g_row, w_in_g, tm=512):
    S, D = x.shape
    wc = w_in_g.shape[2]
    n_out = N_DEV * wc

    def body(x_ref, g_ref, w_ref, proj_ref, h_ref):
        @pl.when(pl.program_id(1) == 0)
        def _():
            xv = x_ref[...]
            r = lax.rsqrt(jnp.mean(xv * xv, axis=-1, keepdims=True) + EPS)
            h_ref[...] = (xv * r * g_ref[...]).astype(bf16)

        proj_ref[...] = _dot(h_ref[...], w_ref[...])

    return pl.pallas_call(
        body, name="rms_proj", grid=(S // tm, N_DEV),
        in_specs=[pl.BlockSpec((tm, D), lambda i, j: (i, 0)),
                  pl.BlockSpec((1, D), lambda i, j: (0, 0)),
                  pl.BlockSpec((None, D, wc), lambda i, j: (j, 0, 0))],
        out_specs=[pl.BlockSpec((tm, wc), lambda i, j: (i, j)), pl.BlockSpec((tm, D), lambda i, j: (i, 0))],
        out_shape=[SDS((S, n_out), f32), SDS((S, D), bf16)], compiler_params=_params(),
    )(x, g_row, w_in_g)


def _out_proj(x, y, w_out_g, after, tm=512):
    S, D = x.shape
    rb = w_out_g.shape[1]

    def body(x_ref, y_ref, w_ref, after_ref, o_ref):
        w = w_ref[...].reshape(N_DEV * rb, D)
        o_ref[...] = x_ref[...] + _dot(y_ref[...], w)

    row = pl.BlockSpec((tm, D), lambda i: (i, 0))
    return pl.pallas_call(
        body, name="out_proj", grid=(S // tm,),
        in_specs=[row, row, pl.BlockSpec((N_DEV, rb, D), lambda i: (0, 0, 0)), pl.BlockSpec(memory_space=pl.ANY)],
        out_specs=row, out_shape=SDS((S, D), f32), compiler_params=_params(),
    )(x, y, w_out_g, after)


def _out_bwd_dy(dxb, w_out_g, tm=512):
    S, D = dxb.shape
    rb = w_out_g.shape[1]

    nb = 2

    def body(dx_ref, w_ref, o_ref):
        o_ref[...] = _dot(dx_ref[...], w_ref[...].reshape(nb * rb, D), NT)

    return pl.pallas_call(
        body, name="out_bwd_dy", grid=(S // tm, N_DEV // nb),
        in_specs=[pl.BlockSpec((tm, D), lambda i, j: (i, 0)),
                  pl.BlockSpec((nb, rb, D), lambda i, j: (j, 0, 0))],
        out_specs=pl.BlockSpec((tm, nb * rb), lambda i, j: (i, j)),
        out_shape=SDS((S, D), f32), compiler_params=_params(),
    )(dxb, w_out_g)


def _tn_grad(name, a, b, tm, tn, rows_major, after):
    S, M = a.shape
    N = b.shape[1]
    if rows_major:
        rb = M // N_DEV
        nb = tm // rb
        out_shape = SDS((N_DEV, rb, N), bf16)
        out_spec = pl.BlockSpec((nb, rb, tn), lambda i, j: (i, 0, j))
    else:
        out_shape = SDS((N_DEV, M, N // N_DEV), bf16)
        assert tn == N // N_DEV
        out_spec = pl.BlockSpec((None, tm, tn), lambda i, j: (j, i, 0))

    def body(a_ref, b_ref, after_ref, o_ref):
        o_ref[...] = _dot(a_ref[...], b_ref[...], TN).astype(bf16).reshape(o_ref.shape)

    return pl.pallas_call(
        body, name=name, grid=(M // tm, N // tn),
        in_specs=[pl.BlockSpec((S, tm), lambda i, j: (0, i)), pl.BlockSpec((S, tn), lambda i, j: (0, j)),
                  pl.BlockSpec(memory_space=pl.ANY)],
        out_specs=out_spec, out_shape=out_shape, compiler_params=_params(),
    )(a, b, after)


def _in_bwd_dh(dproj, w_in_g, after, tm=1024, tn=256):
    S = dproj.shape[0]
    _, D, wc = w_in_g.shape
    tm = min(tm, S)

    def body(dp_ref, w_ref, after_ref, o_ref):
        acc = _dot(dp_ref[:, 0:wc], w_ref[0], NT)
        for k in range(1, N_DEV):
            acc = acc + _dot(dp_ref[:, k * wc:(k + 1) * wc], w_ref[k], NT)
        o_ref[...] = acc

    return pl.pallas_call(
        body, name="in_bwd_dh", grid=(S // tm, D // tn),
        in_specs=[pl.BlockSpec((tm, N_DEV * wc), lambda i, j: (i, 0)),
                  pl.BlockSpec((N_DEV, tn, wc), lambda i, j: (0, j, 0)), pl.BlockSpec(memory_space=pl.ANY)],
        out_specs=pl.BlockSpec((tm, tn), lambda i, j: (i, j)),
        out_shape=SDS((S, D), f32), compiler_params=_params(),
    )(dproj, w_in_g, after)


def _rms_bwd(dh, x, g_row, dx_next, tm=256):
    S, D = x.shape

    def body(dh_ref, x_ref, g_ref, dxn_ref, dx_ref, dxb_ref, dg_ref):
        @pl.when(pl.program_id(0) == 0)
        def _():
            dg_ref[...] = jnp.zeros_like(dg_ref)

        dh = dh_ref[...]
        xv = x_ref[...]
        r = lax.rsqrt(jnp.mean(xv * xv, axis=-1, keepdims=True) + EPS)
        xhat = xv * r
        dxhat = dh * g_ref[...]
        dx = r * (dxhat - xhat * jnp.mean(dxhat * xhat, axis=-1, keepdims=True)) + dxn_ref[...]
        dx_ref[...] = dx
        dxb_ref[...] = dx.astype(bf16)
        dg_ref[...] += jnp.sum(dh * xhat, axis=0, keepdims=True)

    row = pl.BlockSpec((tm, D), lambda i: (i, 0))
    vec = pl.BlockSpec((1, D), lambda i: (0, 0))
    return pl.pallas_call(
        body, name="rms_bwd", grid=(S // tm,), in_specs=[row, row, vec, row], out_specs=[row, row, vec],
        out_shape=[SDS((S, D), f32), SDS((S, D), bf16), SDS((1, D), f32)], compiler_params=_params(),
    )(dh, x, g_row, dx_next)


def _sgu_fwd(proj, ln_g, ln_b, w_s, b_t):
    S = proj.shape[0]
    da = A_GROUPS * HEAD_DIM
    D = 2 * da

    def body(u_ref, v_ref, z_ref, lg_ref, lb_ref, w_ref, bt_ref, y_ref):
        u = _gelu(u_ref[...])
        v = _gelu(v_ref[...])
        z = z_ref[...]
        mu = jnp.mean(v, axis=-1, keepdims=True)
        xc = v - mu
        rs = lax.rsqrt(jnp.mean(xc * xc, axis=-1, keepdims=True) + EPS)
        vn = (xc * rs * lg_ref[...] + lb_ref[...]).astype(bf16)
        gate = u * (z * jax.nn.sigmoid(z))
        tri = lax.broadcasted_iota(jnp.int32, (CHUNK, CHUNK), 0) >= lax.broadcasted_iota(jnp.int32, (CHUNK, CHUNK), 1)
        for g in range(A_GROUPS):
            sl = slice(g * HEAD_DIM, (g + 1) * HEAD_DIM)
            wm = jnp.where(tri, w_ref[g], 0.0).astype(bf16)
            mixed = _dot(wm, vn[:, sl]) + bt_ref[:, g:g + 1]
            y_ref[:, sl] = (gate[:, sl] * mixed).astype(bf16)

    blk = lambda cb: pl.BlockSpec((CHUNK, da), lambda c: (c, cb))
    full = lambda shp: pl.BlockSpec(shp, lambda c: (0,) * len(shp))
    return pl.pallas_call(
        body, name="sgu_fwd", grid=(S // CHUNK,),
        in_specs=[blk(0), blk(1), blk(2), full((1, da)), full((1, da)),
                  full((A_GROUPS, CHUNK, CHUNK)), full((CHUNK, A_GROUPS))],
        out_specs=blk(0), out_shape=SDS((S, D), bf16), compiler_params=_params(),
    )(proj, proj, proj, ln_g, ln_b, w_s, b_t)


def _sgu_bwd(proj, dy, ln_g, ln_b, w_s, b_t):
    S = proj.shape[0]
    da = A_GROUPS * HEAD_DIM
    n_proj = proj.shape[1]

    def body(u_ref, v_ref, z_ref, dy_ref, lg_ref, lb_ref, w_ref, bt_ref,
             dp_ref, dw_ref, db_ref, dlg_ref, dlb_ref, dvn_ref):
        @pl.when(pl.program_id(0) == 0)
        def _():
            dw_ref[...] = jnp.zeros_like(dw_ref)
            db_ref[...] = jnp.zeros_like(db_ref)
            dlg_ref[...] = jnp.zeros_like(dlg_ref)
            dlb_ref[...] = jnp.zeros_like(dlb_ref)

        up, vp, z, dy = u_ref[...], v_ref[...], z_ref[...], dy_ref[...]
        u, gu = _gelu_and_grad(up)
        v, gv = _gelu_and_grad(vp)
        s, gs = _silu_and_grad(z)
        mu = jnp.mean(v, axis=-1, keepdims=True)
        xc = v - mu
        rs = lax.rsqrt(jnp.mean(xc * xc, axis=-1, keepdims=True) + EPS)
        vhat = xc * rs
        lg = lg_ref[...]
        vn = (vhat * lg + lb_ref[...]).astype(bf16)
        tri = lax.broadcasted_iota(jnp.int32, (CHUNK, CHUNK), 0) >= lax.broadcasted_iota(jnp.int32, (CHUNK, CHUNK), 1)
        lane = lax.broadcasted_iota(jnp.int32, (CHUNK, HEAD_DIM), 1)
        dys = dy * s
        db = jnp.zeros((CHUNK, HEAD_DIM), f32)
        for g in range(A_GROUPS):
            sl = slice(g * HEAD_DIM, (g + 1) * HEAD_DIM)
            wm = jnp.where(tri, w_ref[g], 0.0).astype(bf16)
            mixed = _dot(wm, vn[:, sl]) + bt_ref[:, g:g + 1]
            dmix = dys[:, sl] * u[:, sl]
            dp_ref[:, sl] = (dys[:, sl] * mixed * gu[:, sl]).astype(bf16)
            dp_ref[:, 2 * da + g * HEAD_DIM:2 * da + (g + 1) * HEAD_DIM] = (
                dy[:, sl] * u[:, sl] * mixed * gs[:, sl]).astype(bf16)
            dmb = dmix.astype(bf16)
            dw_ref[g] += jnp.where(tri, _dot(dmb, vn[:, sl], NT), 0.0)
            dvn_ref[:, sl] = _dot(wm, dmb, TN)
            db = db + jnp.where(lane == g, jnp.sum(dmix, axis=1, keepdims=True), 0.0)
        db_ref[...] += db
        dvn = dvn_ref[...]
        dlg_ref[...] += jnp.sum(dvn * vhat, axis=0, keepdims=True)
        dlb_ref[...] += jnp.sum(dvn, axis=0, keepdims=True)
        dvhat = dvn * lg
        dv = rs * (dvhat - jnp.mean(dvhat, axis=-1, keepdims=True)
                   - vhat * jnp.mean(dvhat * vhat, axis=-1, keepdims=True))
        dp_ref[:, da:2 * da] = (dv * gv).astype(bf16)

    blk = lambda cb: pl.BlockSpec((CHUNK, da), lambda c: (c, cb))
    full = lambda shp: pl.BlockSpec(shp, lambda c: (0,) * len(shp))
    return pl.pallas_call(
        body, name="sgu_bwd", grid=(S // CHUNK,),
        in_specs=[blk(0), blk(1), blk(2), blk(0), full((1, da)), full((1, da)),
                  full((A_GROUPS, CHUNK, CHUNK)), full((CHUNK, A_GROUPS))],
        out_specs=[pl.BlockSpec((CHUNK, 3 * da), lambda c: (c, 0)), full((A_GROUPS, CHUNK, CHUNK)),
                   full((CHUNK, HEAD_DIM)), full((1, da)), full((1, da))],
        out_shape=[SDS((S, n_proj), bf16), SDS((A_GROUPS, CHUNK, CHUNK), f32), SDS((CHUNK, HEAD_DIM), f32),
                   SDS((1, da), f32), SDS((1, da), f32)],
        scratch_shapes=[pltpu.VMEM((CHUNK, da), f32)], compiler_params=_params(),
    )(proj, proj, proj, dy, ln_g, ln_b, w_s, b_t)


def _sb_scores(q, kblk, kb, rows, cols, masked):
    z = _dot(q, kblk, NT) * (1.0 / math.sqrt(HEAD_DIM))
    t = jnp.log(1.0 + jnp.exp(-jnp.abs(z)))
    log_1mb = -(jnp.maximum(z, 0.0) + t)
    log_beta = jnp.minimum(z, 0.0) - t
    if not masked:
        return None, log_beta, log_1mb
    causal = (cols + kb * TK) < rows
    return causal, log_beta, jnp.where(causal, log_1mb, 0.0)


def _sb_tiles(i):
    rows = i * TQ + lax.broadcasted_iota(jnp.int32, (TQ, TK), 0)
    cols = lax.broadcasted_iota(jnp.int32, (TQ, TK), 1)
    r_i = lax.broadcasted_iota(jnp.int32, (TK, TK), 0)
    c_i = lax.broadcasted_iota(jnp.int32, (TK, TK), 1)
    upper, lower = (r_i > c_i).astype(bf16), (r_i < c_i).astype(bf16)
    return rows, cols, jnp.concatenate([upper, upper], axis=0), jnp.concatenate([lower, lower], axis=0)


def _suffix_sum(t, tri):
    hi = lax.bitcast_convert_type(lax.bitcast_convert_type(t, jnp.uint32) & jnp.uint32(0xFFFF0000), f32)
    both = jnp.concatenate([hi.astype(bf16), (t - hi).astype(bf16)], axis=1)
    return _dot(both, tri)


def _sb_fwd(proj, y_prev, col0, after):
    S = proj.shape[0]
    D = y_prev.shape[1]
    dh = N_HEADS * HEAD_DIM
    n_diag = TQ // TK

    def body(q_ref, k_ref, v_ref, z_ref, yp_ref, after_ref, y_ref, o_ref, car_ref, qb, kb_s, vb_s, c_ref):
        i = pl.program_id(0)

        @pl.when(i == 0)
        def _():
            kb_s[...] = k_ref[...].astype(bf16)
            vb_s[...] = v_ref[...].astype(bf16)

        qb[...] = q_ref[...].astype(bf16)
        o_ref[...] = jnp.zeros_like(o_ref)
        c_ref[...] = jnp.zeros_like(c_ref)
        car_ref[...] = jnp.zeros_like(car_ref)
        nkb = (i + 1) * n_diag
        rows, cols, upper, _ = _sb_tiles(i)

        def make_step(masked):
            def step(jj, carry):
                kb = nkb - 1 - jj
                off = pl.multiple_of(kb * TK, TK)
                hs = range(N_HEADS)
                sls = [slice(h * HEAD_DIM, (h + 1) * HEAD_DIM) for h in hs]
                sc = [_sb_scores(qb[:, sls[h]], kb_s[pl.ds(off, TK), sls[h]], kb, rows, cols, masked) for h in hs]
                suf = [_suffix_sum(sc[h][2], upper) for h in hs]
                cs = [c_ref[h] for h in hs]
                es = [jnp.exp(sc[h][1] + suf[h] + cs[h]) for h in hs]
                if masked:
                    es = [jnp.where(sc[h][0], es[h], 0.0) for h in hs]
                pv = [_dot(es[h].astype(bf16), vb_s[pl.ds(off, TK), sls[h]]) for h in hs]
                for h in hs:
                    o_ref[:, sls[h]] += pv[h]
                    car_ref[h] = jnp.where(cols == kb, cs[h], car_ref[h])
                    c_ref[h] = cs[h] + jnp.sum(sc[h][2], axis=1, keepdims=True)
                return carry
            return step

        lax.fori_loop(0, n_diag, make_step(True), 0)
        lax.fori_loop(n_diag, nkb, make_step(False), 0)
        z = z_ref[...]
        y_ref[...] = (o_ref[...] * (z * jax.nn.sigmoid(z))).astype(bf16)

    cb = col0 * HEAD_DIM // dh
    qspec = lambda k: pl.BlockSpec((TQ, dh), lambda i: (i, cb + k))
    kspec = lambda k: pl.BlockSpec((S, dh), lambda i: (0, cb + k))
    return pl.pallas_call(
        body, name="sb_fwd", grid=(S // TQ,),
        in_specs=[qspec(0), kspec(1), kspec(2), qspec(3), pl.BlockSpec(memory_space=pl.ANY),
                  pl.BlockSpec(memory_space=pl.ANY)],
        out_specs=[pl.BlockSpec((TQ, dh), lambda i: (i, A_GROUPS * HEAD_DIM // dh)),
                   pl.BlockSpec((TQ, dh), lambda i: (i, 0)),
                   pl.BlockSpec((N_HEADS, TQ, TK), lambda i: (0, i, 0))],
        out_shape=[SDS((S, D), bf16), SDS((S, dh), f32), SDS((N_HEADS, S, TK), f32)],
        input_output_aliases={4: 0},
        scratch_shapes=[pltpu.VMEM((TQ, dh), bf16), pltpu.VMEM((S, dh), bf16), pltpu.VMEM((S, dh), bf16),
                        pltpu.VMEM((N_HEADS, TQ, TK), f32)],
        compiler_params=_params(),
    )(proj, proj, proj, proj, y_prev, after)


def _sb_bwd(proj, o, car, dy, dproj_prev, col0, after):
    S = proj.shape[0]
    n_i = S // TQ
    dh = N_HEADS * HEAD_DIM
    n_diag = TQ // TK
    cb = col0 * HEAD_DIM // dh
    scale = 1.0 / math.sqrt(HEAD_DIM)

    def body(q_ref, k_ref, v_ref, z_ref, o_ref, car_ref, dy_ref, dpp_ref, after_ref,
             dp_ref, qb, kb_s, vb_s, dob, p_ref, dq_acc, dk_acc, dv_acc, st_a, st_b, st_k, st_v):
        i = pl.program_id(0)

        def put(stage_ref, row0, nrows, k):
            pltpu.sync_copy(stage_ref, dp_ref.at[pl.ds(row0, nrows), pl.ds((cb + k) * dh, dh)])

        @pl.when(i == 0)
        def _():
            kb_s[...] = k_ref[...].astype(bf16)
            vb_s[...] = v_ref[...].astype(bf16)
            dk_acc[...] = jnp.zeros_like(dk_acc)
            dv_acc[...] = jnp.zeros_like(dv_acc)

        s, gs = _silu_and_grad(z_ref[...])
        dy = dy_ref[...]
        st_b[...] = (dy * o_ref[...] * gs).astype(bf16)
        dob[...] = (dy * s).astype(bf16)
        qb[...] = q_ref[...].astype(bf16)
        p_ref[...] = jnp.zeros_like(p_ref)
        dq_acc[...] = jnp.zeros_like(dq_acc)
        nkb = (i + 1) * n_diag
        rows, cols, upper, lower = _sb_tiles(i)

        def make_step(masked):
            def step(kb, carry):
                off = pl.multiple_of(kb * TK, TK)
                hs = range(N_HEADS)
                sls = [slice(h * HEAD_DIM, (h + 1) * HEAD_DIM) for h in hs]
                qs = [qb[:, sls[h]] for h in hs]
                ks = [kb_s[pl.ds(off, TK), sls[h]] for h in hs]
                dos = [dob[:, sls[h]] for h in hs]
                sc = [_sb_scores(qs[h], ks[h], kb, rows, cols, masked) for h in hs]
                da = [_dot(dos[h], vb_s[pl.ds(off, TK), sls[h]], NT) for h in hs]
                suf = [_suffix_sum(sc[h][2], upper) for h in hs]
                onehot = cols == kb
                cs = [jnp.sum(jnp.where(onehot, car_ref[h], 0.0), axis=1, keepdims=True) for h in hs]
                es = [jnp.exp(sc[h][1] + suf[h] + cs[h]) for h in hs]
                if masked:
                    es = [jnp.where(sc[h][0], es[h], 0.0) for h in hs]
                gs_ = [da[h] * es[h] for h in hs]
                ps = [p_ref[h] for h in hs]
                pre = [_suffix_sum(gs_[h], lower) + ps[h] for h in hs]
                dzs = []
                for h in hs:
                    beta = jnp.exp(sc[h][1])
                    dzz = gs_[h] * (1.0 - beta) - beta * pre[h]
                    if masked:
                        dzz = jnp.where(sc[h][0], dzz, 0.0)
                    dzs.append((dzz * scale).astype(bf16))
                dqs = [_dot(dzs[h], ks[h]) for h in hs]
                dks = [_dot(dzs[h], qs[h], TN) for h in hs]
                dvs = [_dot(es[h].astype(bf16), dos[h], TN) for h in hs]
                for h in hs:
                    dq_acc[:, sls[h]] += dqs[h]
                    dk_acc[pl.ds(off, TK), sls[h]] += dks[h]
                    dv_acc[pl.ds(off, TK), sls[h]] += dvs[h]
                    p_ref[h] = ps[h] + jnp.sum(gs_[h], axis=1, keepdims=True)
                return carry
            return step

        lax.fori_loop(0, nkb - n_diag, make_step(False), 0)
        lax.fori_loop(nkb - n_diag, nkb, make_step(True), 0)
        st_a[...] = dq_acc[...].astype(bf16)
        row0 = pl.multiple_of(i * TQ, TQ)
        put(st_a, row0, TQ, 0)
        put(st_b, row0, TQ, 3)

        @pl.when(i == n_i - 1)
        def _():
            st_k[...] = dk_acc[...].astype(bf16)
            st_v[...] = dv_acc[...].astype(bf16)
            put(st_k, 0, S, 1)
            put(st_v, 0, S, 2)

    qspec = lambda k: pl.BlockSpec((TQ, dh), lambda i: (i, cb + k))
    kspec = lambda k: pl.BlockSpec((S, dh), lambda i: (0, cb + k))
    return pl.pallas_call(
        body, name="sb_bwd", grid=(n_i,),
        in_specs=[qspec(0), kspec(1), kspec(2), qspec(3),
                  pl.BlockSpec((TQ, dh), lambda i: (i, 0)),
                  pl.BlockSpec((N_HEADS, TQ, TK), lambda i: (0, i, 0)),
                  pl.BlockSpec((TQ, dh), lambda i: (i, A_GROUPS * HEAD_DIM // dh)),
                  pl.BlockSpec(memory_space=pl.ANY), pl.BlockSpec(memory_space=pl.ANY)],
        out_specs=pl.BlockSpec(memory_space=pl.ANY),
        out_shape=SDS(dproj_prev.shape, bf16),
        input_output_aliases={7: 0},
        scratch_shapes=[pltpu.VMEM((TQ, dh), bf16), pltpu.VMEM((S, dh), bf16), pltpu.VMEM((S, dh), bf16),
                        pltpu.VMEM((TQ, dh), bf16), pltpu.VMEM((N_HEADS, TQ, TK), f32), pltpu.VMEM((TQ, dh), f32),
                        pltpu.VMEM((S, dh), f32), pltpu.VMEM((S, dh), f32),
                        pltpu.VMEM((TQ, dh), bf16), pltpu.VMEM((TQ, dh), bf16),
                        pltpu.VMEM((S, dh), bf16), pltpu.VMEM((S, dh), bf16)],
        compiler_params=_params(56),
    )(proj, proj, proj, proj, o, car, dy, dproj_prev, after)


def _mem_kv(mem, mg_row, w_kv_g):
    M, D = mem.shape
    rb, n = w_kv_g.shape[1], w_kv_g.shape[2]

    def body(m_ref, g_ref, w_ref, kv_ref):
        mv = m_ref[...]
        r = lax.rsqrt(jnp.mean(mv * mv, axis=-1, keepdims=True) + EPS)
        mh = (mv * r * g_ref[...]).astype(bf16)
        kv_ref[...] = _dot(mh, w_ref[...].reshape(N_DEV * rb, n))

    return pl.pallas_call(
        body, name="mem_kv", grid=(1,),
        in_specs=[pl.BlockSpec((M, D), lambda i: (0, 0)), pl.BlockSpec((1, D), lambda i: (0, 0)),
                  pl.BlockSpec((N_DEV, rb, n), lambda i: (0, 0, 0))],
        out_specs=pl.BlockSpec((M, n), lambda i: (0, 0)),
        out_shape=SDS((M, n), f32), compiler_params=_params(),
    )(mem, mg_row, w_kv_g)


def _xattn_head(q_ref, kv_ref, qg, kg, h):
    dc = N_HEADS * HEAD_DIM
    sl = slice(h * HEAD_DIM, (h + 1) * HEAD_DIM)
    qh = q_ref[:, sl]
    rq = lax.rsqrt(jnp.mean(qh * qh, axis=-1, keepdims=True) + EPS)
    qhat = qh * rq
    qn = (qhat * qg).astype(bf16)
    kh = kv_ref[:, sl]
    rk = lax.rsqrt(jnp.mean(kh * kh, axis=-1, keepdims=True) + EPS)
    kn = (kh * rk * kg).astype(bf16)
    vh = kv_ref[:, dc + h * HEAD_DIM:dc + (h + 1) * HEAD_DIM].astype(bf16)
    s = _dot(qn, kn, NT) * (1.0 / math.sqrt(HEAD_DIM))
    e = jnp.exp(s - jnp.max(s, axis=-1, keepdims=True))
    p = e / jnp.sum(e, axis=-1, keepdims=True)
    o = _dot(p.astype(bf16), vh)
    return sl, rq, qhat, qn, kn, vh, p, o


def _xattn_fwd(proj, kv, qg_row, kg_row, y_prev, col0, tq=512):
    S = proj.shape[0]
    D = y_prev.shape[1]
    dc = N_HEADS * HEAD_DIM
    M = kv.shape[0]

    def body(q_ref, z_ref, kv_ref, qg_ref, kg_ref, yp_ref, y_ref):
        for h in range(N_HEADS):
            sl, _, _, _, _, _, _, o = _xattn_head(q_ref, kv_ref, qg_ref[...], kg_ref[...], h)
            z = z_ref[:, sl]
            y_ref[:, sl] = (o * (z * jax.nn.sigmoid(z))).astype(bf16)

    full = lambda shp: pl.BlockSpec(shp, lambda i: (0,) * len(shp))
    return pl.pallas_call(
        body, name="xattn_fwd", grid=(S // tq,),
        in_specs=[pl.BlockSpec((tq, dc), lambda i: (i, col0)), pl.BlockSpec((tq, dc), lambda i: (i, col0 + 1)),
                  full((M, 2 * dc)), full((1, HEAD_DIM)), full((1, HEAD_DIM)), pl.BlockSpec(memory_space=pl.ANY)],
        out_specs=pl.BlockSpec((tq, dc), lambda i: (i, D // dc - 1)),
        out_shape=SDS((S, D), bf16), input_output_aliases={5: 0}, compiler_params=_params(),
    )(proj, proj, kv, qg_row, kg_row, y_prev)


def _xattn_bwd(proj, kv, qg_row, kg_row, dy, dproj_prev, col0, tq=512):
    S = proj.shape[0]
    D = dy.shape[1]
    dc = N_HEADS * HEAD_DIM
    M = kv.shape[0]

    def body(q_ref, z_ref, kv_ref, qg_ref, kg_ref, dy_ref, dpp_ref, dp_ref, dkn_ref, dv_ref, dqg_ref):
        @pl.when(pl.program_id(0) == 0)
        def _():
            dkn_ref[...] = jnp.zeros_like(dkn_ref)
            dv_ref[...] = jnp.zeros_like(dv_ref)
            dqg_ref[...] = jnp.zeros_like(dqg_ref)

        qg = qg_ref[...]
        for h in range(N_HEADS):
            sl, rq, qhat, qn, kn, vh, p, o = _xattn_head(q_ref, kv_ref, qg, kg_ref[...], h)
            s, gs = _silu_and_grad(z_ref[:, sl])
            dyh = dy_ref[:, sl]
            dp_ref[:, dc + h * HEAD_DIM:dc + (h + 1) * HEAD_DIM] = (dyh * o * gs).astype(bf16)
            dob = (dyh * s).astype(bf16)
            dpr = _dot(dob, vh, NT)
            dv_ref[:, sl] += _dot(p.astype(bf16), dob, TN)
            ds = (p * (dpr - jnp.sum(p * dpr, axis=-1, keepdims=True)) * (1.0 / math.sqrt(HEAD_DIM))).astype(bf16)
            dqn = _dot(ds, kn)
            dkn_ref[:, sl] += _dot(ds, qn, TN)
            dqg_ref[...] += jnp.sum(dqn * qhat, axis=0, keepdims=True)
            dqhat = dqn * qg
            dp_ref[:, sl] = (rq * (dqhat - qhat * jnp.mean(dqhat * qhat, axis=-1, keepdims=True))).astype(bf16)

    full = lambda shp: pl.BlockSpec(shp, lambda i: (0,) * len(shp))
    return pl.pallas_call(
        body, name="xattn_bwd", grid=(S // tq,),
        in_specs=[pl.BlockSpec((tq, dc), lambda i: (i, col0)), pl.BlockSpec((tq, dc), lambda i: (i, col0 + 1)),
                  full((M, 2 * dc)), full((1, HEAD_DIM)), full((1, HEAD_DIM)),
                  pl.BlockSpec((tq, dc), lambda i: (i, D // dc - 1)), pl.BlockSpec(memory_space=pl.ANY)],
        out_specs=[pl.BlockSpec((tq, 2 * dc), lambda i: (i, col0 // 2)), full((M, dc)), full((M, dc)),
                   full((1, HEAD_DIM))],
        out_shape=[SDS(dproj_prev.shape, bf16), SDS((M, dc), f32), SDS((M, dc), f32), SDS((1, HEAD_DIM), f32)],
        input_output_aliases={6: 0}, compiler_params=_params(),
    )(proj, proj, kv, qg_row, kg_row, dy, dproj_prev)


def _mem_bwd(mem, mg_row, kv, dkn, dv, kg_row, w_kv_g):
    M, D = mem.shape
    rb, n = w_kv_g.shape[1], w_kv_g.shape[2]
    dc = n // 2

    def body(m_ref, g_ref, kv_ref, dkn_ref, dv_ref, kg_ref, w_ref, dw_ref, dmg_ref, dkg_ref, dkv_ref):
        mv = m_ref[...]
        r = lax.rsqrt(jnp.mean(mv * mv, axis=-1, keepdims=True) + EPS)
        mhat = mv * r
        mh = (mhat * g_ref[...]).astype(bf16)
        kg = kg_ref[...]
        dkg = jnp.zeros((1, HEAD_DIM), f32)
        for h in range(N_HEADS):
            sl = slice(h * HEAD_DIM, (h + 1) * HEAD_DIM)
            kh = kv_ref[:, sl]
            rk = lax.rsqrt(jnp.mean(kh * kh, axis=-1, keepdims=True) + EPS)
            khat = kh * rk
            dkn_h = dkn_ref[:, sl]
            dkg = dkg + jnp.sum(dkn_h * khat, axis=0, keepdims=True)
            dkhat = dkn_h * kg
            dkv_ref[:, sl] = (rk * (dkhat - khat * jnp.mean(dkhat * khat, axis=-1, keepdims=True))).astype(bf16)
        dkv_ref[:, dc:] = dv_ref[...].astype(bf16)
        dkg_ref[...] = dkg
        dkv = dkv_ref[...]
        dw_ref[...] = _dot(mh, dkv, TN).astype(bf16).reshape(N_DEV, rb, n)
        dmh = _dot(dkv, w_ref[...].reshape(N_DEV * rb, n), NT)
        dmg_ref[...] = jnp.sum(dmh * mhat, axis=0, keepdims=True)

    full = lambda shp: pl.BlockSpec(shp, lambda i: (0,) * len(shp))
    wspec = full((N_DEV, rb, n))
    return pl.pallas_call(
        body, name="mem_bwd", grid=(1,),
        in_specs=[full((M, D)), full((1, D)), full((M, n)), full((M, dc)), full((M, dc)), full((1, HEAD_DIM)), wspec],
        out_specs=[wspec, full((1, D)), full((1, HEAD_DIM))],
        out_shape=[SDS((N_DEV, rb, n), bf16), SDS((1, D), f32), SDS((1, HEAD_DIM), f32)],
        scratch_shapes=[pltpu.VMEM((M, n), bf16)], compiler_params=_params(),
    )(mem, mg_row, kv, dkn, dv, kg_row, w_kv_g)


SMALL = ("norm_g", "sgu_ln_g", "sgu_ln_b", "sgu_w", "sgu_b", "mem_norm_g", "q_norm_g", "k_norm_g")


def _pack_small(parts):
    flat = jnp.concatenate([parts[n].reshape(-1) for n in SMALL])
    pad = (-flat.shape[0]) % (8 * 128)
    return jnp.pad(flat, (0, pad)).reshape(-1, 128)


def _unpack_small(packed, like):
    flat = packed.reshape(-1)
    out, off = {}, 0
    for n in SMALL:
        size = math.prod(like[n].shape)
        out[n] = flat[off:off + size].reshape(like[n].shape)
        off += size
    return out


def kernel(x, mem, norm_g, w_in, sgu_ln_g, sgu_ln_b, sgu_w, sgu_b, mem_norm_g, w_mem_kv, q_norm_g, k_norm_g, w_out, loss_target, m_norm_g, m_w_in, m_sgu_ln_g, m_sgu_ln_b, m_sgu_w, m_sgu_b, m_mem_norm_g, m_w_mem_kv, m_q_norm_g, m_k_norm_g, m_w_out, v_norm_g, v_w_in, v_sgu_ln_g, v_sgu_ln_b, v_sgu_w, v_sgu_b, v_mem_norm_g, v_w_mem_kv, v_q_norm_g, v_k_norm_g, v_w_out):
    L, D, wc = w_in.shape
    S = x.shape[1]
    da = D // 2
    xs = x.reshape(S, D)
    mems = mem.reshape(mem.shape[1], D)
    tgt = loss_target.reshape(S, D)
    row = lambda a, l: a[l].reshape(1, -1)
    tie = lambda a, tok: a + tok[0:1, 0:1]
    sb_col, xa_col = 3 * da // HEAD_DIM, (3 * da + D) // (D // 4)

    ax, ay, ac = lax.axis_index("x"), lax.axis_index("y"), lax.axis_index("c")
    ids = jnp.stack([4 * ax + 2 * ay + ac, 2 * ax + ay, ac]).astype(jnp.int32)
    w_b = [(_cast_into_slot("cast_w_in", w_in, l, 512, ids), _cast_into_slot("cast_w_kv", w_mem_kv, l, 256, ids),
            _cast_into_slot("cast_w_out", w_out, l, 256, ids)) for l in range(L)]
    first = _gather2_start("gather_w_in0", [w_b[0][0]], ids)
    in_fwd = _gather2_forward("gather_w_in0_forward", _split_wait(first, first["token"]), ids)

    acts = []
    xl = xs
    for l in range(L):
        g_row = row(norm_g, l)
        (w_in_g,) = _split_wait(in_fwd, g_row if l else in_fwd["token"])
        rest = _gather2_start(f"gather_w_rest{l}", [w_b[l][1], w_b[l][2]], w_in_g)
        proj, h = _rms_proj(xl, tie(g_row, rest["token"]), w_in_g)
        rest_fwd = _gather2_forward(f"gather_w_rest{l}_forward", _split_wait(rest, proj), proj)
        lg_row = tie(row(sgu_ln_g, l), rest_fwd["token"])
        if l + 1 < L:
            nxt = _gather2_start(f"gather_w_in{l + 1}", [w_b[l + 1][0]], proj)
            lg_row = tie(lg_row, nxt["token"])
        y = _sgu_fwd(proj, lg_row, row(sgu_ln_b, l), sgu_w[l], sgu_b[l].T)
        y, o_b, car = _sb_fwd(proj, y, sb_col, lg_row)
        w_kv_g, w_out_g = _split_wait(rest_fwd, o_b)
        kv = _mem_kv(mems, row(mem_norm_g, l), w_kv_g)
        y = _xattn_fwd(proj, kv, row(q_norm_g, l), row(k_norm_g, l), y, xa_col)
        order = kv
        if l + 1 < L:
            in_fwd = _gather2_forward(f"gather_w_in{l + 1}_forward", _split_wait(nxt, y), y)
            order = in_fwd["token"]
        x_next = _out_proj(xl, y, w_out_g, order)
        acts.append((xl, proj, h, y, o_b, car, kv, w_in_g, w_kv_g, w_out_g))
        xl = x_next

    dx, dxb, loss_part = _loss_and_grad(xl, tgt, 512)
    loss = lax.psum(loss_part[0, 0], ("x", "y", "c"))

    weights = dict(norm_g=norm_g, sgu_ln_g=sgu_ln_g, sgu_ln_b=sgu_ln_b, sgu_w=sgu_w, sgu_b=sgu_b,
                   mem_norm_g=mem_norm_g, q_norm_g=q_norm_g, k_norm_g=k_norm_g)
    moms_m = dict(norm_g=m_norm_g, sgu_ln_g=m_sgu_ln_g, sgu_ln_b=m_sgu_ln_b, sgu_w=m_sgu_w, sgu_b=m_sgu_b,
                  mem_norm_g=m_mem_norm_g, q_norm_g=m_q_norm_g, k_norm_g=m_k_norm_g)
    moms_v = dict(norm_g=v_norm_g, sgu_ln_g=v_sgu_ln_g, sgu_ln_b=v_sgu_ln_b, sgu_w=v_sgu_w, sgu_b=v_sgu_b,
                  mem_norm_g=v_mem_norm_g, q_norm_g=v_q_norm_g, k_norm_g=v_k_norm_g)
    head_rows = 16
    assert D // 128 <= head_rows

    seconds = {}
    pending = None
    small = {n: [None] * L for n in SMALL}
    for l in reversed(range(L)):
        xl, proj, h, y, o_b, car, kv, w_in_g, w_kv_g, w_out_g = acts[l]
        dy = _out_bwd_dy(dxb, w_out_g)
        order = dy
        if pending is not None:
            seconds[pending[0]] = _scatter2_second_level(pending[0], pending[1], dy, ids)
            order = seconds[pending[0]]["token"]
        g_out = _tn_grad("out_bwd_dw", y, dxb, 512, 512, True, order)
        first = _scatter2_pair_start(f"scatter_g_out{l}_pair", [g_out], ids)
        dproj, d_sw, d_sb, d_lg, d_lb = _sgu_bwd(proj, dy, tie(row(sgu_ln_g, l), first["token"]), row(sgu_ln_b, l),
                                                 sgu_w[l], sgu_b[l].T)
        seconds[f"g_out{l}"] = _scatter2_second_level(f"g_out{l}", first, dproj, ids)
        dproj = _sb_bwd(proj, o_b, car, dy, dproj, sb_col, seconds[f"g_out{l}"]["token"])
        dproj, dkn, dv, d_qg = _xattn_bwd(proj, kv, row(q_norm_g, l), row(k_norm_g, l), dy, dproj, xa_col)
        g_kv, d_mg, d_kg = _mem_bwd(mems, row(mem_norm_g, l), kv, dkn, dv, row(k_norm_g, l), w_kv_g)
        for n, val in (("sgu_ln_g", d_lg), ("sgu_ln_b", d_lb), ("sgu_w", d_sw), ("sgu_b", d_sb[:, :A_GROUPS].T),
                       ("mem_norm_g", d_mg), ("q_norm_g", d_qg), ("k_norm_g", d_kg)):
            small[n][l] = val.reshape(-1)
        order = d_kg
        if l == 0:
            small["norm_g"][0] = jnp.zeros_like(small["norm_g"][1])
            part = _pack_small({n: jnp.stack(small[n]).reshape(weights[n].shape) for n in SMALL})
            tail = _gather2_start("gather_small_tail", [_into_slot("small_tail_slot", part[head_rows:], ids)], ids)
            order = tail["token"]
        g_in_l = _tn_grad("in_bwd_dw", h, dproj, 1024, wc, False, order)
        first = _scatter2_pair_start(f"scatter_g_rest{l}_pair", [g_kv, g_in_l], ids)
        g_row = tie(row(norm_g, l), first["token"])
        pending = (f"g_rest{l}", first)
        if l == 0:
            tail_fwd = _gather2_forward("gather_small_tail_forward", _split_wait(tail, g_in_l), g_in_l)
            seconds[pending[0]] = _scatter2_second_level(pending[0], pending[1], tail_fwd["token"], ids)
            g_row = tie(g_row, seconds[pending[0]]["token"])
        dh = _in_bwd_dh(dproj, w_in_g, g_row)
        dx, dxb, d_ng = _rms_bwd(dh, xl, g_row, dx)
        small["norm_g"][l] = d_ng.reshape(-1)
    head_part = jnp.concatenate([small["norm_g"][0].reshape(-1, 128), jnp.zeros((head_rows - D // 128, 128), f32)])
    head = _gather2_start("gather_small_head", [_into_slot("small_head_slot", head_part, ids)], ids)

    a_out = a_kv = a_in = None
    order = head["token"]
    for l in reversed(range(L)):
        (r_out,) = _scatter2_finish(seconds[f"g_out{l}"], order)
        a_out = _sum_adam("adam_w_out", r_out, w_out, m_w_out, v_w_out, l, a_out, 128, ids)
        order = a_out[0]
        if l == 0:
            head_fwd = _gather2_forward("gather_small_head_forward", _split_wait(head, order), ids)
            (r_tail,) = _split_wait(tail_fwd, order)
            (r_head,) = _split_wait(head_fwd, r_tail)
            sm = _small_sum_adam(r_head, r_tail, _pack_small(weights), _pack_small(moms_m), _pack_small(moms_v))
            order = sm[0]
        r_kv, r_in = _scatter2_finish(seconds[f"g_rest{l}"], order)
        a_kv = _sum_adam("adam_w_kv", r_kv, w_mem_kv, m_w_mem_kv, v_w_mem_kv, l, a_kv, 256, ids)
        a_in = _sum_adam("adam_w_in", r_in, w_in, m_w_in, v_w_in, l, a_in, 256, ids)
        order = a_in[0]
    res = {"w_out": a_out, "w_mem_kv": a_kv, "w_in": a_in}
    sm = [_unpack_small(a, weights) for a in sm]
    for n in SMALL:
        res[n] = [a[n] for a in sm]

    order = ("norm_g", "w_in", "sgu_ln_g", "sgu_ln_b", "sgu_w", "sgu_b", "mem_norm_g", "w_mem_kv", "q_norm_g",
             "k_norm_g", "w_out")
    outs = [loss, dx.reshape(x.shape)]
    for k in range(4):
        outs += [res[n][k] for n in order]
    return tuple(outs)
```

```python
import functools
import math

import jax
import jax.numpy as jnp
from jax import lax
from jax.experimental import pallas as pl
from jax.experimental.pallas import tpu as pltpu

f32 = jnp.float32
bf16 = jnp.bfloat16
SDS = jax.ShapeDtypeStruct

N_DEV = 8
EPS = 1e-6
CHUNK = 128
A_GROUPS = 8
HEAD_DIM = 128
N_HEADS = 4
TQ = 256
TK = 128
ADAM_LR, ADAM_B1, ADAM_B2, ADAM_EPS, ADAM_WD, ADAM_STEP = 0.001, 0.9, 0.999, 1e-08, 0.01, 10
MIB = 1024 * 1024

NT = (((1,), (1,)), ((), ()))
TN = (((0,), (0,)), ((), ()))


def _params(vmem_mib=48):
    return pltpu.CompilerParams(vmem_limit_bytes=vmem_mib * MIB)


def _gelu_and_grad(x):
    e = lax.erf(x * (1.0 / math.sqrt(2.0)))
    cdf = 0.5 * (1.0 + e)
    pdf = jnp.exp(-0.5 * x * x) * (1.0 / math.sqrt(2.0 * math.pi))
    return x * cdf, cdf + x * pdf


def _gelu(x):
    return 0.5 * x * (1.0 + lax.erf(x * (1.0 / math.sqrt(2.0))))


def _silu_and_grad(z):
    sg = jax.nn.sigmoid(z)
    return z * sg, sg * (1.0 + z * (1.0 - sg))


def _dot(a, b, dims=None):
    if dims is None:
        return jnp.dot(a, b, preferred_element_type=f32)
    return lax.dot_general(a, b, dims, preferred_element_type=f32)


_HBM = pl.BlockSpec(memory_space=pltpu.HBM)
_SEM = pl.BlockSpec(memory_space=pltpu.SEMAPHORE)
_EFFECT = pltpu.SideEffectType.DATAFLOW_SIDE_EFFECTING


def _split_start(name, bufs, n_remote, n_local, build, after):
    nb = len(bufs)

    def body(*refs):
        token = refs[-1]
        locals_, remotes = build(refs[:nb], *refs[nb + 1:nb + 4])
        for cp in locals_ + remotes:
            cp.start()
        token[...] = jnp.zeros_like(token)

    hbm = lambda a: pltpu.with_memory_space_constraint(a, pltpu.HBM)
    outs = pl.pallas_call(
        body, name=name,
        out_shape=(pltpu.SemaphoreType.DMA((n_remote,)), pltpu.SemaphoreType.DMA((n_remote,)),
                   pltpu.SemaphoreType.DMA((max(n_local, 1),)),
                   *[pltpu.HBM(b.shape, b.dtype) for b in bufs], SDS((8, 128), f32)),
        in_specs=[_HBM] * nb + [pl.BlockSpec(memory_space=pl.ANY)],
        out_specs=(_SEM, _SEM, _SEM, *[_HBM] * nb, pl.BlockSpec(memory_space=pltpu.VMEM)),
        input_output_aliases={k: 3 + k for k in range(nb)},
        compiler_params=pltpu.CompilerParams(has_side_effects=_EFFECT),
    )(*[hbm(b) for b in bufs], after)
    return dict(name=name, build=build, sems=outs[:3], bufs=outs[3:3 + nb], token=outs[-1])


def _split_wait(handle, after):
    build, bufs = handle["build"], handle["bufs"]
    nb = len(bufs)

    def body(*refs):
        locals_, remotes = build(refs[:nb], *refs[nb:nb + 3])
        for cp in remotes:
            cp.wait_recv()
        for cp in remotes:
            cp.wait_send()
        for cp in locals_:
            cp.wait()

    outs = pl.pallas_call(
        body, name=handle["name"] + "_wait",
        out_shape=tuple(pltpu.HBM(b.shape, b.dtype) for b in bufs),
        in_specs=[_HBM] * nb + [_SEM] * 3 + [pl.BlockSpec(memory_space=pl.ANY)],
        out_specs=tuple([_HBM] * nb),
        input_output_aliases={k: k for k in range(nb)},
        compiler_params=pltpu.CompilerParams(has_side_effects=_EFFECT),
    )(*bufs, *handle["sems"], after)
    return list(outs)


def _remote(src, dst, send_sems, recv_sems, k, to):
    return pltpu.make_async_remote_copy(src_ref=src, dst_ref=dst, send_sem=send_sems.at[k], recv_sem=recv_sems.at[k],
                                        device_id=to, device_id_type=pl.DeviceIdType.MESH)


def _other_chips(x, y):
    return [(1 - x, y), (x, 1 - y), (1 - x, 1 - y)]


def _gather2_start(name, lands, after):
    def build(refs, send, recv, loc):
        x, y, c = lax.axis_index("x"), lax.axis_index("y"), lax.axis_index("c")
        me = 4 * x + 2 * y + c
        remotes = []
        for a, d in enumerate(refs):
            remotes.append(_remote(d.at[me], d.at[me], send, recv, 4 * a, (x, y, 1 - c)))
            remotes += [_remote(d.at[me], d.at[me], send, recv, 4 * a + 1 + k, (px, py, c))
                        for k, (px, py) in enumerate(_other_chips(x, y))]
        return [], remotes

    return _split_start(name, list(lands), 4 * len(lands), 0, build, after)


def _gather2_forward(name, lands, after):
    n = len(lands)

    def build(refs, send, recv, loc):
        x, y, c = lax.axis_index("x"), lax.axis_index("y"), lax.axis_index("c")
        slots = [4 * px + 2 * py + c for px, py in _other_chips(x, y)]
        return [], [_remote(d.at[sl], d.at[sl], send, recv, 3 * a + k, (x, y, 1 - c))
                    for a, d in enumerate(refs) for k, sl in enumerate(slots)]

    return _split_start(name, list(lands), 3 * n, 0, build, after)


def _scatter2_pair_start(name, srcs, after):
    n = len(srcs)

    def build(refs, send, recv, loc):
        x, y, c = lax.axis_index("x"), lax.axis_index("y"), lax.axis_index("c")
        return [], [_remote(refs[a].at[2 * q + 1 - c], refs[n + a].at[q], send, recv, 4 * a + q, (x, y, 1 - c))
                    for a in range(n) for q in range(4)]

    lands = [lax.empty((4,) + s.shape[1:], s.dtype) for s in srcs]
    return _split_start(name, list(srcs) + lands, 4 * n, 0, build, after)


def _scatter2_chip_start(name, pairs, after):
    n = len(pairs)

    def build(refs, send, recv, loc):
        x, y, c = lax.axis_index("x"), lax.axis_index("y"), lax.axis_index("c")
        return [], [_remote(refs[a].at[2 * px + py], refs[n + a].at[2 * x + y], send, recv, 3 * a + k, (px, py, c))
                    for a in range(n) for k, (px, py) in enumerate(_other_chips(x, y))]

    return _split_start(name, list(pairs) + [lax.empty(p.shape, p.dtype) for p in pairs], 3 * n, 0, build, after)


def _pair_sum(name, src, theirs, ids):
    _, R, C = theirs.shape
    tr = min(R, 1024)

    def body(ids_ref, a_ref, b_ref, o_ref):
        o_ref[...] = (a_ref[...].astype(f32) + b_ref[...].astype(f32)).astype(bf16)

    spec = pl.BlockSpec((None, tr, C), lambda q, i, ids: (q, i, 0))
    return pl.pallas_call(
        body, name=name,
        grid_spec=pltpu.PrefetchScalarGridSpec(
            num_scalar_prefetch=1, grid=(4, R // tr),
            in_specs=[pl.BlockSpec((None, tr, C), lambda q, i, ids: (2 * q + ids[2], i, 0)), spec], out_specs=spec),
        out_shape=SDS(theirs.shape, bf16), compiler_params=_params(),
    )(ids, src, theirs)


def _scatter2_second_level(name, first, after, ids):
    outs = _split_wait(first, after)
    n = len(outs) // 2
    pairs = [_pair_sum(f"pair_sum_{name}{a}", outs[a], outs[n + a], ids) for a in range(n)]
    return _scatter2_chip_start(f"scatter_{name}_chip", pairs, ids)


def _scatter2_finish(second, after):
    outs = _split_wait(second, after)
    n = len(outs) // 2
    return [(outs[a], outs[n + a]) for a in range(n)]


def _cast_into_slot(name, w, l, tr, ids):
    _, R, C = w.shape

    def body(ids_ref, w_ref, o_ref):
        o_ref[...] = w_ref[...].astype(bf16)

    return pl.pallas_call(
        body, name=name,
        grid_spec=pltpu.PrefetchScalarGridSpec(
            num_scalar_prefetch=1, grid=(R // tr,),
            in_specs=[pl.BlockSpec((None, tr, C), lambda i, ids: (l, i, 0))],
            out_specs=pl.BlockSpec((None, tr, C), lambda i, ids: (ids[0], i, 0))),
        out_shape=SDS((N_DEV, R, C), bf16), compiler_params=_params(),
    )(ids, w)


def _into_slot(name, a, ids):
    R, C = a.shape

    def body(ids_ref, a_ref, o_ref):
        o_ref[...] = a_ref[...]

    return pl.pallas_call(
        body, name=name,
        grid_spec=pltpu.PrefetchScalarGridSpec(
            num_scalar_prefetch=1, grid=(1,),
            in_specs=[pl.BlockSpec((R, C), lambda i, ids: (0, 0))],
            out_specs=pl.BlockSpec((None, R, C), lambda i, ids: (ids[0], 0, 0))),
        out_shape=SDS((N_DEV, R, C), f32), compiler_params=_params(),
    )(ids, a)


def _adam_math(w, g, m, v):
    m2 = ADAM_B1 * m + (1.0 - ADAM_B1) * g
    v2 = ADAM_B2 * v + (1.0 - ADAM_B2) * (g * g)
    m_hat = m2 / (1.0 - ADAM_B1 ** ADAM_STEP)
    v_hat = v2 / (1.0 - ADAM_B2 ** ADAM_STEP)
    delta = -ADAM_LR * (m_hat / (jnp.sqrt(v_hat) + ADAM_EPS) + ADAM_WD * w)
    return delta, m2, v2


def _sum_adam(name, pair_recv, w, m, v, l, prev, tr, ids):
    own, recv = pair_recv
    L, R, C = w.shape
    slots = recv.shape[0]

    def body(ids_ref, r_ref, own_ref, w_ref, m_ref, v_ref, *rest):
        g_ref, d_ref, m2_ref, v2_ref = rest[-4:]
        terms = [jnp.where(ids_ref[1] == q, own_ref[...], r_ref[q]).astype(f32) for q in range(slots)]
        g = terms[0]
        for t in terms[1:]:
            g = g + t
        d, m2, v2 = _adam_math(w_ref[...], g, m_ref[...], v_ref[...])
        g_ref[...] = g
        d_ref[...] = d
        m2_ref[...] = m2
        v2_ref[...] = v2

    wspec = pl.BlockSpec((None, tr, C), lambda i, ids: (l, i, 0))
    in_specs = [pl.BlockSpec((slots, tr, C), lambda i, ids: (0, i, 0)),
                pl.BlockSpec((None, tr, C), lambda i, ids: (ids[1], i, 0)), wspec, wspec, wspec]
    args = [ids, recv, own, w, m, v]
    aliases = {}
    if prev is not None:
        in_specs += [pl.BlockSpec(memory_space=pl.ANY)] * 4
        args += list(prev)
        aliases = {6 + k: k for k in range(4)}
    return pl.pallas_call(
        body, name=name,
        grid_spec=pltpu.PrefetchScalarGridSpec(num_scalar_prefetch=1, grid=(R // tr,), in_specs=in_specs,
                                               out_specs=[wspec] * 4),
        out_shape=[SDS((L, R, C), f32)] * 4, input_output_aliases=aliases, compiler_params=_params(),
    )(*args)


def _small_sum_adam(recv_head, recv_tail, ws, ms, vs, offs):
    n = len(ws)
    r0 = recv_head.shape[1]

    def body(*refs):
        rh, rt = refs[0], refs[1]
        w_refs, m_refs, v_refs = refs[2:2 + n], refs[2 + n:2 + 2 * n], refs[2 + 2 * n:2 + 3 * n]
        outs = refs[2 + 3 * n:]
        for p in range(n):
            lo, hi = offs[p], offs[p] + ws[p].shape[0]
            pieces = []
            if lo < r0:
                pieces.append((rh, lo, 0, min(hi, r0) - lo))
            if hi > r0:
                pieces.append((rt, max(lo, r0) - r0, max(lo, r0) - lo, hi - max(lo, r0)))
            for src, a, b, cnt in pieces:
                g = src[0, a:a + cnt, :]
                for s in range(1, N_DEV):
                    g = g + src[s, a:a + cnt, :]
                d, m2, v2 = _adam_math(w_refs[p][b:b + cnt, :], g, m_refs[p][b:b + cnt, :], v_refs[p][b:b + cnt, :])
                for k, val in enumerate((g, d, m2, v2)):
                    outs[k * n + p][b:b + cnt, :] = val

    return pl.pallas_call(
        body, name="small_sum_adam", out_shape=[SDS(w.shape, f32) for w in ws] * 4, compiler_params=_params(),
    )(recv_head, recv_tail, *ws, *ms, *vs)


def _loss_and_grad(xf, tgt, tm):
    S, D = xf.shape

    def body(x_ref, t_ref, dx_ref, dxb_ref, l_ref):
        i = pl.program_id(0)
        d = x_ref[...] - t_ref[...]
        dx = d * (1.0 / D)
        dx_ref[...] = dx
        dxb_ref[...] = dx.astype(bf16)
        e = d * d
        part = e[:, 0:128]
        for k in range(1, D // 128):
            part = part + e[:, k * 128:(k + 1) * 128]
        part = jnp.sum(part.reshape(tm // 8, 8, 128), axis=0)

        @pl.when(i == 0)
        def _():
            l_ref[...] = jnp.zeros_like(l_ref)

        l_ref[...] += part

        @pl.when(i == pl.num_programs(0) - 1)
        def _():
            tot = jnp.sum(l_ref[...], axis=1, keepdims=True)
            tot = jnp.sum(tot, axis=0, keepdims=True)
            l_ref[...] = jnp.broadcast_to(tot * (0.5 / D), l_ref.shape)

    row = pl.BlockSpec((tm, D), lambda i: (i, 0))
    return pl.pallas_call(
        body, name="loss_grad", grid=(S // tm,),
        in_specs=[row, row], out_specs=[row, row, pl.BlockSpec((8, 128), lambda i: (0, 0))],
        out_shape=[SDS((S, D), f32), SDS((S, D), bf16), SDS((8, 128), f32)], compiler_params=_params(),
    )(xf, tgt)


def _rms_proj(x, g_row, w_in_g, tm=1024):
    S, D = x.shape
    wc = w_in_g.shape[2]
    n_out = N_DEV * wc

    def body(x_ref, g_ref, w_ref, proj_ref, h_ref):
        @pl.when(pl.program_id(1) == 0)
        def _():
            xv = x_ref[...]
            r = lax.rsqrt(jnp.mean(xv * xv, axis=-1, keepdims=True) + EPS)
            h_ref[...] = (xv * r * g_ref[...]).astype(bf16)

        proj_ref[...] = _dot(h_ref[...], w_ref[...])

    return pl.pallas_call(
        body, name="rms_proj", grid=(S // tm, N_DEV),
        in_specs=[pl.BlockSpec((tm, D), lambda i, j: (i, 0)),
                  pl.BlockSpec((1, D), lambda i, j: (0, 0)),
                  pl.BlockSpec((None, D, wc), lambda i, j: (j, 0, 0))],
        out_specs=[pl.BlockSpec((tm, wc), lambda i, j: (i, j)), pl.BlockSpec((tm, D), lambda i, j: (i, 0))],
        out_shape=[SDS((S, n_out), f32), SDS((S, D), bf16)], compiler_params=_params(),
    )(x, g_row, w_in_g)


def _out_proj(x, y, w_out_g, after, tm=512):
    S, D = x.shape
    rb = w_out_g.shape[1]

    def body(x_ref, y_ref, w_ref, after_ref, o_ref):
        w = w_ref[...].reshape(N_DEV * rb, D)
        o_ref[...] = x_ref[...] + _dot(y_ref[...], w)

    row = pl.BlockSpec((tm, D), lambda i: (i, 0))
    return pl.pallas_call(
        body, name="out_proj", grid=(S // tm,),
        in_specs=[row, row, pl.BlockSpec((N_DEV, rb, D), lambda i: (0, 0, 0)), pl.BlockSpec(memory_space=pl.ANY)],
        out_specs=row, out_shape=SDS((S, D), f32), compiler_params=_params(),
    )(x, y, w_out_g, after)


def _out_bwd_dy(dxb, w_out_g, tm=512):
    S, D = dxb.shape
    rb = w_out_g.shape[1]

    nb = 2

    def body(dx_ref, w_ref, o_ref):
        o_ref[...] = _dot(dx_ref[...], w_ref[...].reshape(nb * rb, D), NT)

    return pl.pallas_call(
        body, name="out_bwd_dy", grid=(S // tm, N_DEV // nb),
        in_specs=[pl.BlockSpec((tm, D), lambda i, j: (i, 0)),
                  pl.BlockSpec((nb, rb, D), lambda i, j: (j, 0, 0))],
        out_specs=pl.BlockSpec((tm, nb * rb), lambda i, j: (i, j)),
        out_shape=SDS((S, D), f32), compiler_params=_params(),
    )(dxb, w_out_g)


def _tn_grad(name, a, b, tm, tn, rows_major, after):
    S, M = a.shape
    N = b.shape[1]
    if rows_major:
        rb = M // N_DEV
        nb = tm // rb
        out_shape = SDS((N_DEV, rb, N), bf16)
        out_spec = pl.BlockSpec((nb, rb, tn), lambda i, j: (i, 0, j))
    else:
        out_shape = SDS((N_DEV, M, N // N_DEV), bf16)
        assert tn == N // N_DEV
        out_spec = pl.BlockSpec((None, tm, tn), lambda i, j: (j, i, 0))

    def body(a_ref, b_ref, after_ref, o_ref):
        o_ref[...] = _dot(a_ref[...], b_ref[...], TN).astype(bf16).reshape(o_ref.shape)

    return pl.pallas_call(
        body, name=name, grid=(M // tm, N // tn),
        in_specs=[pl.BlockSpec((S, tm), lambda i, j: (0, i)), pl.BlockSpec((S, tn), lambda i, j: (0, j)),
                  pl.BlockSpec(memory_space=pl.ANY)],
        out_specs=out_spec, out_shape=out_shape, compiler_params=_params(),
    )(a, b, after)


def _in_bwd_dh(dproj, w_in_g, after, tm=1024, tn=256):
    S = dproj.shape[0]
    _, D, wc = w_in_g.shape
    tm = min(tm, S)

    def body(dp_ref, w_ref, after_ref, o_ref):
        acc = _dot(dp_ref[:, 0:wc], w_ref[0], NT)
        for k in range(1, N_DEV):
            acc = acc + _dot(dp_ref[:, k * wc:(k + 1) * wc], w_ref[k], NT)
        o_ref[...] = acc

    return pl.pallas_call(
        body, name="in_bwd_dh", grid=(S // tm, D // tn),
        in_specs=[pl.BlockSpec((tm, N_DEV * wc), lambda i, j: (i, 0)),
                  pl.BlockSpec((N_DEV, tn, wc), lambda i, j: (0, j, 0)), pl.BlockSpec(memory_space=pl.ANY)],
        out_specs=pl.BlockSpec((tm, tn), lambda i, j: (i, j)),
        out_shape=SDS((S, D), f32), compiler_params=_params(),
    )(dproj, w_in_g, after)


def _rms_bwd(dh, x, g_row, dx_next, tm=256):
    S, D = x.shape

    def body(dh_ref, x_ref, g_ref, dxn_ref, dx_ref, dxb_ref, dg_ref):
        @pl.when(pl.program_id(0) == 0)
        def _():
            dg_ref[...] = jnp.zeros_like(dg_ref)

        dh = dh_ref[...]
        xv = x_ref[...]
        r = lax.rsqrt(jnp.mean(xv * xv, axis=-1, keepdims=True) + EPS)
        xhat = xv * r
        dxhat = dh * g_ref[...]
        dx = r * (dxhat - xhat * jnp.mean(dxhat * xhat, axis=-1, keepdims=True)) + dxn_ref[...]
        dx_ref[...] = dx
        dxb_ref[...] = dx.astype(bf16)
        dg_ref[...] += jnp.sum(dh * xhat, axis=0, keepdims=True)

    row = pl.BlockSpec((tm, D), lambda i: (i, 0))
    vec = pl.BlockSpec((1, D), lambda i: (0, 0))
    return pl.pallas_call(
        body, name="rms_bwd", grid=(S // tm,), in_specs=[row, row, vec, row], out_specs=[row, row, vec],
        out_shape=[SDS((S, D), f32), SDS((S, D), bf16), SDS((1, D), f32)], compiler_params=_params(),
    )(dh, x, g_row, dx_next)


def _sgu_fwd(proj, ln_g, ln_b, w_s, b_t):
    S = proj.shape[0]
    da = A_GROUPS * HEAD_DIM
    D = 2 * da

    def body(u_ref, v_ref, z_ref, lg_ref, lb_ref, w_ref, bt_ref, y_ref):
        u = _gelu(u_ref[...])
        v = _gelu(v_ref[...])
        z = z_ref[...]
        mu = jnp.mean(v, axis=-1, keepdims=True)
        xc = v - mu
        rs = lax.rsqrt(jnp.mean(xc * xc, axis=-1, keepdims=True) + EPS)
        vn = (xc * rs * lg_ref[...] + lb_ref[...]).astype(bf16)
        gate = u * (z * jax.nn.sigmoid(z))
        tri = lax.broadcasted_iota(jnp.int32, (CHUNK, CHUNK), 0) >= lax.broadcasted_iota(jnp.int32, (CHUNK, CHUNK), 1)
        for g in range(A_GROUPS):
            sl = slice(g * HEAD_DIM, (g + 1) * HEAD_DIM)
            wm = jnp.where(tri, w_ref[g], 0.0).astype(bf16)
            mixed = _dot(wm, vn[:, sl]) + bt_ref[:, g:g + 1]
            y_ref[:, sl] = (gate[:, sl] * mixed).astype(bf16)

    blk = lambda cb: pl.BlockSpec((CHUNK, da), lambda c: (c, cb))
    full = lambda shp: pl.BlockSpec(shp, lambda c: (0,) * len(shp))
    return pl.pallas_call(
        body, name="sgu_fwd", grid=(S // CHUNK,),
        in_specs=[blk(0), blk(1), blk(2), full((1, da)), full((1, da)),
                  full((A_GROUPS, CHUNK, CHUNK)), full((CHUNK, A_GROUPS))],
        out_specs=blk(0), out_shape=SDS((S, D), bf16), compiler_params=_params(),
    )(proj, proj, proj, ln_g, ln_b, w_s, b_t)


def _sgu_bwd(proj, dy, ln_g, ln_b, w_s, b_t):
    S = proj.shape[0]
    da = A_GROUPS * HEAD_DIM
    n_proj = proj.shape[1]

    def body(u_ref, v_ref, z_ref, dy_ref, lg_ref, lb_ref, w_ref, bt_ref,
             dp_ref, dw_ref, db_ref, dlg_ref, dlb_ref, dvn_ref):
        @pl.when(pl.program_id(0) == 0)
        def _():
            dw_ref[...] = jnp.zeros_like(dw_ref)
            db_ref[...] = jnp.zeros_like(db_ref)
            dlg_ref[...] = jnp.zeros_like(dlg_ref)
            dlb_ref[...] = jnp.zeros_like(dlb_ref)

        up, vp, z, dy = u_ref[...], v_ref[...], z_ref[...], dy_ref[...]
        u, gu = _gelu_and_grad(up)
        v, gv = _gelu_and_grad(vp)
        s, gs = _silu_and_grad(z)
        mu = jnp.mean(v, axis=-1, keepdims=True)
        xc = v - mu
        rs = lax.rsqrt(jnp.mean(xc * xc, axis=-1, keepdims=True) + EPS)
        vhat = xc * rs
        lg = lg_ref[...]
        vn = (vhat * lg + lb_ref[...]).astype(bf16)
        tri = lax.broadcasted_iota(jnp.int32, (CHUNK, CHUNK), 0) >= lax.broadcasted_iota(jnp.int32, (CHUNK, CHUNK), 1)
        lane = lax.broadcasted_iota(jnp.int32, (CHUNK, HEAD_DIM), 1)
        dys = dy * s
        db = jnp.zeros((CHUNK, HEAD_DIM), f32)
        for g in range(A_GROUPS):
            sl = slice(g * HEAD_DIM, (g + 1) * HEAD_DIM)
            wm = jnp.where(tri, w_ref[g], 0.0).astype(bf16)
            mixed = _dot(wm, vn[:, sl]) + bt_ref[:, g:g + 1]
            dmix = dys[:, sl] * u[:, sl]
            dp_ref[:, sl] = (dys[:, sl] * mixed * gu[:, sl]).astype(bf16)
            dp_ref[:, 2 * da + g * HEAD_DIM:2 * da + (g + 1) * HEAD_DIM] = (
                dy[:, sl] * u[:, sl] * mixed * gs[:, sl]).astype(bf16)
            dmb = dmix.astype(bf16)
            dw_ref[g] += jnp.where(tri, _dot(dmb, vn[:, sl], NT), 0.0)
            dvn_ref[:, sl] = _dot(wm, dmb, TN)
            db = db + jnp.where(lane == g, jnp.sum(dmix, axis=1, keepdims=True), 0.0)
        db_ref[...] += db
        dvn = dvn_ref[...]
        dlg_ref[...] += jnp.sum(dvn * vhat, axis=0, keepdims=True)
        dlb_ref[...] += jnp.sum(dvn, axis=0, keepdims=True)
        dvhat = dvn * lg
        dv = rs * (dvhat - jnp.mean(dvhat, axis=-1, keepdims=True)
                   - vhat * jnp.mean(dvhat * vhat, axis=-1, keepdims=True))
        dp_ref[:, da:2 * da] = (dv * gv).astype(bf16)

    blk = lambda cb: pl.BlockSpec((CHUNK, da), lambda c: (c, cb))
    full = lambda shp: pl.BlockSpec(shp, lambda c: (0,) * len(shp))
    return pl.pallas_call(
        body, name="sgu_bwd", grid=(S // CHUNK,),
        in_specs=[blk(0), blk(1), blk(2), blk(0), full((1, da)), full((1, da)),
                  full((A_GROUPS, CHUNK, CHUNK)), full((CHUNK, A_GROUPS))],
        out_specs=[pl.BlockSpec((CHUNK, 3 * da), lambda c: (c, 0)), full((A_GROUPS, CHUNK, CHUNK)),
                   full((CHUNK, HEAD_DIM)), full((1, da)), full((1, da))],
        out_shape=[SDS((S, n_proj), bf16), SDS((A_GROUPS, CHUNK, CHUNK), f32), SDS((CHUNK, HEAD_DIM), f32),
                   SDS((1, da), f32), SDS((1, da), f32)],
        scratch_shapes=[pltpu.VMEM((CHUNK, da), f32)], compiler_params=_params(),
    )(proj, proj, proj, dy, ln_g, ln_b, w_s, b_t)


def _sb_scores(q, kblk, kb, rows, cols, masked):
    z = _dot(q, kblk, NT) * (1.0 / math.sqrt(HEAD_DIM))
    t = jnp.log(1.0 + jnp.exp(-jnp.abs(z)))
    log_1mb = -(jnp.maximum(z, 0.0) + t)
    log_beta = jnp.minimum(z, 0.0) - t
    if not masked:
        return None, log_beta, log_1mb
    causal = (cols + kb * TK) < rows
    return causal, log_beta, jnp.where(causal, log_1mb, 0.0)


def _sb_tiles(i):
    rows = i * TQ + lax.broadcasted_iota(jnp.int32, (TQ, TK), 0)
    cols = lax.broadcasted_iota(jnp.int32, (TQ, TK), 1)
    r_i = lax.broadcasted_iota(jnp.int32, (TK, TK), 0)
    c_i = lax.broadcasted_iota(jnp.int32, (TK, TK), 1)
    upper, lower = (r_i > c_i).astype(bf16), (r_i < c_i).astype(bf16)
    return rows, cols, jnp.concatenate([upper, upper], axis=0), jnp.concatenate([lower, lower], axis=0)


def _suffix_sum(t, tri):
    hi = lax.bitcast_convert_type(lax.bitcast_convert_type(t, jnp.uint32) & jnp.uint32(0xFFFF0000), f32)
    both = jnp.concatenate([hi.astype(bf16), (t - hi).astype(bf16)], axis=1)
    return _dot(both, tri)


def _sb_fwd(proj, y_prev, col0, after):
    S = proj.shape[0]
    D = y_prev.shape[1]
    dh = N_HEADS * HEAD_DIM
    n_diag = TQ // TK

    def body(q_ref, k_ref, v_ref, z_ref, yp_ref, after_ref, y_ref, o_ref, car_ref, qb, kb_s, vb_s, c_ref):
        i = pl.program_id(0)

        @pl.when(i == 0)
        def _():
            kb_s[...] = k_ref[...].astype(bf16)
            vb_s[...] = v_ref[...].astype(bf16)

        qb[...] = q_ref[...].astype(bf16)
        o_ref[...] = jnp.zeros_like(o_ref)
        c_ref[...] = jnp.zeros_like(c_ref)
        car_ref[...] = jnp.zeros_like(car_ref)
        nkb = (i + 1) * n_diag
        rows, cols, upper, _ = _sb_tiles(i)

        def make_step(masked):
            def step(jj, carry):
                kb = nkb - 1 - jj
                off = pl.multiple_of(kb * TK, TK)
                hs = range(N_HEADS)
                sls = [slice(h * HEAD_DIM, (h + 1) * HEAD_DIM) for h in hs]
                sc = [_sb_scores(qb[:, sls[h]], kb_s[pl.ds(off, TK), sls[h]], kb, rows, cols, masked) for h in hs]
                suf = [_suffix_sum(sc[h][2], upper) for h in hs]
                cs = [c_ref[h] for h in hs]
                es = [jnp.exp(sc[h][1] + suf[h] + cs[h]) for h in hs]
                if masked:
                    es = [jnp.where(sc[h][0], es[h], 0.0) for h in hs]
                pv = [_dot(es[h].astype(bf16), vb_s[pl.ds(off, TK), sls[h]]) for h in hs]
                for h in hs:
                    o_ref[:, sls[h]] += pv[h]
                    car_ref[h] = jnp.where(cols == kb, cs[h], car_ref[h])
                    c_ref[h] = cs[h] + jnp.sum(sc[h][2], axis=1, keepdims=True)
                return carry
            return step

        lax.fori_loop(0, n_diag, make_step(True), 0)
        lax.fori_loop(n_diag, nkb, make_step(False), 0)
        z = z_ref[...]
        y_ref[...] = (o_ref[...] * (z * jax.nn.sigmoid(z))).astype(bf16)

    cb = col0 * HEAD_DIM // dh
    qspec = lambda k: pl.BlockSpec((TQ, dh), lambda i: (i, cb + k))
    kspec = lambda k: pl.BlockSpec((S, dh), lambda i: (0, cb + k))
    return pl.pallas_call(
        body, name="sb_fwd", grid=(S // TQ,),
        in_specs=[qspec(0), kspec(1), kspec(2), qspec(3), pl.BlockSpec(memory_space=pl.ANY),
                  pl.BlockSpec(memory_space=pl.ANY)],
        out_specs=[pl.BlockSpec((TQ, dh), lambda i: (i, A_GROUPS * HEAD_DIM // dh)),
                   pl.BlockSpec((TQ, dh), lambda i: (i, 0)),
                   pl.BlockSpec((N_HEADS, TQ, TK), lambda i: (0, i, 0))],
        out_shape=[SDS((S, D), bf16), SDS((S, dh), f32), SDS((N_HEADS, S, TK), f32)],
        input_output_aliases={4: 0},
        scratch_shapes=[pltpu.VMEM((TQ, dh), bf16), pltpu.VMEM((S, dh), bf16), pltpu.VMEM((S, dh), bf16),
                        pltpu.VMEM((N_HEADS, TQ, TK), f32)],
        compiler_params=_params(),
    )(proj, proj, proj, proj, y_prev, after)


def _sb_bwd(proj, o, car, dy, dproj_prev, col0, after):
    S = proj.shape[0]
    n_i = S // TQ
    dh = N_HEADS * HEAD_DIM
    n_diag = TQ // TK
    cb = col0 * HEAD_DIM // dh
    scale = 1.0 / math.sqrt(HEAD_DIM)

    def body(q_ref, k_ref, v_ref, z_ref, o_ref, car_ref, dy_ref, dpp_ref, after_ref,
             dp_ref, qb, kb_s, vb_s, dob, p_ref, dq_acc, dk_acc, dv_acc, st_a, st_b, st_k, st_v):
        i = pl.program_id(0)

        def put(stage_ref, row0, nrows, k):
            pltpu.sync_copy(stage_ref, dp_ref.at[pl.ds(row0, nrows), pl.ds((cb + k) * dh, dh)])

        @pl.when(i == 0)
        def _():
            kb_s[...] = k_ref[...].astype(bf16)
            vb_s[...] = v_ref[...].astype(bf16)
            dk_acc[...] = jnp.zeros_like(dk_acc)
            dv_acc[...] = jnp.zeros_like(dv_acc)

        s, gs = _silu_and_grad(z_ref[...])
        dy = dy_ref[...]
        st_b[...] = (dy * o_ref[...] * gs).astype(bf16)
        dob[...] = (dy * s).astype(bf16)
        qb[...] = q_ref[...].astype(bf16)
        p_ref[...] = jnp.zeros_like(p_ref)
        dq_acc[...] = jnp.zeros_like(dq_acc)
        nkb = (i + 1) * n_diag
        rows, cols, upper, lower = _sb_tiles(i)

        def make_step(masked):
            def step(kb, carry):
                off = pl.multiple_of(kb * TK, TK)
                hs = range(N_HEADS)
                sls = [slice(h * HEAD_DIM, (h + 1) * HEAD_DIM) for h in hs]
                qs = [qb[:, sls[h]] for h in hs]
                ks = [kb_s[pl.ds(off, TK), sls[h]] for h in hs]
                dos = [dob[:, sls[h]] for h in hs]
                sc = [_sb_scores(qs[h], ks[h], kb, rows, cols, masked) for h in hs]
                da = [_dot(dos[h], vb_s[pl.ds(off, TK), sls[h]], NT) for h in hs]
                suf = [_suffix_sum(sc[h][2], upper) for h in hs]
                onehot = cols == kb
                cs = [jnp.sum(jnp.where(onehot, car_ref[h], 0.0), axis=1, keepdims=True) for h in hs]
                es = [jnp.exp(sc[h][1] + suf[h] + cs[h]) for h in hs]
                if masked:
                    es = [jnp.where(sc[h][0], es[h], 0.0) for h in hs]
                gs_ = [da[h] * es[h] for h in hs]
                ps = [p_ref[h] for h in hs]
                pre = [_suffix_sum(gs_[h], lower) + ps[h] for h in hs]
                dzs = []
                for h in hs:
                    beta = jnp.exp(sc[h][1])
                    dzz = gs_[h] * (1.0 - beta) - beta * pre[h]
                    if masked:
                        dzz = jnp.where(sc[h][0], dzz, 0.0)
                    dzs.append((dzz * scale).astype(bf16))
                dqs = [_dot(dzs[h], ks[h]) for h in hs]
                dks = [_dot(dzs[h], qs[h], TN) for h in hs]
                dvs = [_dot(es[h].astype(bf16), dos[h], TN) for h in hs]
                for h in hs:
                    dq_acc[:, sls[h]] += dqs[h]
                    dk_acc[pl.ds(off, TK), sls[h]] += dks[h]
                    dv_acc[pl.ds(off, TK), sls[h]] += dvs[h]
                    p_ref[h] = ps[h] + jnp.sum(gs_[h], axis=1, keepdims=True)
                return carry
            return step

        lax.fori_loop(0, nkb - n_diag, make_step(False), 0)
        lax.fori_loop(nkb - n_diag, nkb, make_step(True), 0)
        st_a[...] = dq_acc[...].astype(bf16)
        row0 = pl.multiple_of(i * TQ, TQ)
        put(st_a, row0, TQ, 0)
        put(st_b, row0, TQ, 3)

        @pl.when(i == n_i - 1)
        def _():
            st_k[...] = dk_acc[...].astype(bf16)
            st_v[...] = dv_acc[...].astype(bf16)
            put(st_k, 0, S, 1)
            put(st_v, 0, S, 2)

    qspec = lambda k: pl.BlockSpec((TQ, dh), lambda i: (i, cb + k))
    kspec = lambda k: pl.BlockSpec((S, dh), lambda i: (0, cb + k))
    return pl.pallas_call(
        body, name="sb_bwd", grid=(n_i,),
        in_specs=[qspec(0), kspec(1), kspec(2), qspec(3),
                  pl.BlockSpec((TQ, dh), lambda i: (i, 0)),
                  pl.BlockSpec((N_HEADS, TQ, TK), lambda i: (0, i, 0)),
                  pl.BlockSpec((TQ, dh), lambda i: (i, A_GROUPS * HEAD_DIM // dh)),
                  pl.BlockSpec(memory_space=pl.ANY), pl.BlockSpec(memory_space=pl.ANY)],
        out_specs=pl.BlockSpec(memory_space=pl.ANY),
        out_shape=SDS(dproj_prev.shape, bf16),
        input_output_aliases={7: 0},
        scratch_shapes=[pltpu.VMEM((TQ, dh), bf16), pltpu.VMEM((S, dh), bf16), pltpu.VMEM((S, dh), bf16),
                        pltpu.VMEM((TQ, dh), bf16), pltpu.VMEM((N_HEADS, TQ, TK), f32), pltpu.VMEM((TQ, dh), f32),
                        pltpu.VMEM((S, dh), f32), pltpu.VMEM((S, dh), f32),
                        pltpu.VMEM((TQ, dh), bf16), pltpu.VMEM((TQ, dh), bf16),
                        pltpu.VMEM((S, dh), bf16), pltpu.VMEM((S, dh), bf16)],
        compiler_params=_params(56),
    )(proj, proj, proj, proj, o, car, dy, dproj_prev, after)


def _mem_kv(mem, mg_row, w_kv_g):
    M, D = mem.shape
    rb, n = w_kv_g.shape[1], w_kv_g.shape[2]

    def body(m_ref, g_ref, w_ref, kv_ref):
        mv = m_ref[...]
        r = lax.rsqrt(jnp.mean(mv * mv, axis=-1, keepdims=True) + EPS)
        mh = (mv * r * g_ref[...]).astype(bf16)
        kv_ref[...] = _dot(mh, w_ref[...].reshape(N_DEV * rb, n))

    return pl.pallas_call(
        body, name="mem_kv", grid=(1,),
        in_specs=[pl.BlockSpec((M, D), lambda i: (0, 0)), pl.BlockSpec((1, D), lambda i: (0, 0)),
                  pl.BlockSpec((N_DEV, rb, n), lambda i: (0, 0, 0))],
        out_specs=pl.BlockSpec((M, n), lambda i: (0, 0)),
        out_shape=SDS((M, n), f32), compiler_params=_params(),
    )(mem, mg_row, w_kv_g)


def _xattn_head(q_ref, kv_ref, qg, kg, h):
    dc = N_HEADS * HEAD_DIM
    sl = slice(h * HEAD_DIM, (h + 1) * HEAD_DIM)
    qh = q_ref[:, sl]
    rq = lax.rsqrt(jnp.mean(qh * qh, axis=-1, keepdims=True) + EPS)
    qhat = qh * rq
    qn = (qhat * qg).astype(bf16)
    kh = kv_ref[:, sl]
    rk = lax.rsqrt(jnp.mean(kh * kh, axis=-1, keepdims=True) + EPS)
    kn = (kh * rk * kg).astype(bf16)
    vh = kv_ref[:, dc + h * HEAD_DIM:dc + (h + 1) * HEAD_DIM].astype(bf16)
    s = _dot(qn, kn, NT) * (1.0 / math.sqrt(HEAD_DIM))
    e = jnp.exp(s - jnp.max(s, axis=-1, keepdims=True))
    p = e / jnp.sum(e, axis=-1, keepdims=True)
    o = _dot(p.astype(bf16), vh)
    return sl, rq, qhat, qn, kn, vh, p, o


def _xattn_fwd(proj, kv, qg_row, kg_row, y_prev, col0, tq=512):
    S = proj.shape[0]
    D = y_prev.shape[1]
    dc = N_HEADS * HEAD_DIM
    M = kv.shape[0]

    def body(q_ref, z_ref, kv_ref, qg_ref, kg_ref, yp_ref, y_ref):
        for h in range(N_HEADS):
            sl, _, _, _, _, _, _, o = _xattn_head(q_ref, kv_ref, qg_ref[...], kg_ref[...], h)
            z = z_ref[:, sl]
            y_ref[:, sl] = (o * (z * jax.nn.sigmoid(z))).astype(bf16)

    full = lambda shp: pl.BlockSpec(shp, lambda i: (0,) * len(shp))
    return pl.pallas_call(
        body, name="xattn_fwd", grid=(S // tq,),
        in_specs=[pl.BlockSpec((tq, dc), lambda i: (i, col0)), pl.BlockSpec((tq, dc), lambda i: (i, col0 + 1)),
                  full((M, 2 * dc)), full((1, HEAD_DIM)), full((1, HEAD_DIM)), pl.BlockSpec(memory_space=pl.ANY)],
        out_specs=pl.BlockSpec((tq, dc), lambda i: (i, D // dc - 1)),
        out_shape=SDS((S, D), bf16), input_output_aliases={5: 0}, compiler_params=_params(),
    )(proj, proj, kv, qg_row, kg_row, y_prev)


def _xattn_bwd(proj, kv, qg_row, kg_row, dy, dproj_prev, col0, tq=512):
    S = proj.shape[0]
    D = dy.shape[1]
    dc = N_HEADS * HEAD_DIM
    M = kv.shape[0]

    def body(q_ref, z_ref, kv_ref, qg_ref, kg_ref, dy_ref, dpp_ref, dp_ref, dkn_ref, dv_ref, dqg_ref):
        @pl.when(pl.program_id(0) == 0)
        def _():
            dkn_ref[...] = jnp.zeros_like(dkn_ref)
            dv_ref[...] = jnp.zeros_like(dv_ref)
            dqg_ref[...] = jnp.zeros_like(dqg_ref)

        qg = qg_ref[...]
        for h in range(N_HEADS):
            sl, rq, qhat, qn, kn, vh, p, o = _xattn_head(q_ref, kv_ref, qg, kg_ref[...], h)
            s, gs = _silu_and_grad(z_ref[:, sl])
            dyh = dy_ref[:, sl]
            dp_ref[:, dc + h * HEAD_DIM:dc + (h + 1) * HEAD_DIM] = (dyh * o * gs).astype(bf16)
            dob = (dyh * s).astype(bf16)
            dpr = _dot(dob, vh, NT)
            dv_ref[:, sl] += _dot(p.astype(bf16), dob, TN)
            ds = (p * (dpr - jnp.sum(p * dpr, axis=-1, keepdims=True)) * (1.0 / math.sqrt(HEAD_DIM))).astype(bf16)
            dqn = _dot(ds, kn)
            dkn_ref[:, sl] += _dot(ds, qn, TN)
            dqg_ref[...] += jnp.sum(dqn * qhat, axis=0, keepdims=True)
            dqhat = dqn * qg
            dp_ref[:, sl] = (rq * (dqhat - qhat * jnp.mean(dqhat * qhat, axis=-1, keepdims=True))).astype(bf16)

    full = lambda shp: pl.BlockSpec(shp, lambda i: (0,) * len(shp))
    return pl.pallas_call(
        body, name="xattn_bwd", grid=(S // tq,),
        in_specs=[pl.BlockSpec((tq, dc), lambda i: (i, col0)), pl.BlockSpec((tq, dc), lambda i: (i, col0 + 1)),
                  full((M, 2 * dc)), full((1, HEAD_DIM)), full((1, HEAD_DIM)),
                  pl.BlockSpec((tq, dc), lambda i: (i, D // dc - 1)), pl.BlockSpec(memory_space=pl.ANY)],
        out_specs=[pl.BlockSpec((tq, 2 * dc), lambda i: (i, col0 // 2)), full((M, dc)), full((M, dc)),
                   full((1, HEAD_DIM))],
        out_shape=[SDS(dproj_prev.shape, bf16), SDS((M, dc), f32), SDS((M, dc), f32), SDS((1, HEAD_DIM), f32)],
        input_output_aliases={6: 0}, compiler_params=_params(),
    )(proj, proj, kv, qg_row, kg_row, dy, dproj_prev)


def _mem_bwd(mem, mg_row, kv, dkn, dv, kg_row, w_kv_g):
    M, D = mem.shape
    rb, n = w_kv_g.shape[1], w_kv_g.shape[2]
    dc = n // 2

    def body(m_ref, g_ref, kv_ref, dkn_ref, dv_ref, kg_ref, w_ref, dw_ref, dmg_ref, dkg_ref, dkv_ref):
        mv = m_ref[...]
        r = lax.rsqrt(jnp.mean(mv * mv, axis=-1, keepdims=True) + EPS)
        mhat = mv * r
        mh = (mhat * g_ref[...]).astype(bf16)
        kg = kg_ref[...]
        dkg = jnp.zeros((1, HEAD_DIM), f32)
        for h in range(N_HEADS):
            sl = slice(h * HEAD_DIM, (h + 1) * HEAD_DIM)
            kh = kv_ref[:, sl]
            rk = lax.rsqrt(jnp.mean(kh * kh, axis=-1, keepdims=True) + EPS)
            khat = kh * rk
            dkn_h = dkn_ref[:, sl]
            dkg = dkg + jnp.sum(dkn_h * khat, axis=0, keepdims=True)
            dkhat = dkn_h * kg
            dkv_ref[:, sl] = (rk * (dkhat - khat * jnp.mean(dkhat * khat, axis=-1, keepdims=True))).astype(bf16)
        dkv_ref[:, dc:] = dv_ref[...].astype(bf16)
        dkg_ref[...] = dkg
        dkv = dkv_ref[...]
        dw_ref[...] = _dot(mh, dkv, TN).astype(bf16).reshape(N_DEV, rb, n)
        dmh = _dot(dkv, w_ref[...].reshape(N_DEV * rb, n), NT)
        dmg_ref[...] = jnp.sum(dmh * mhat, axis=0, keepdims=True)

    full = lambda shp: pl.BlockSpec(shp, lambda i: (0,) * len(shp))
    wspec = full((N_DEV, rb, n))
    return pl.pallas_call(
        body, name="mem_bwd", grid=(1,),
        in_specs=[full((M, D)), full((1, D)), full((M, n)), full((M, dc)), full((M, dc)), full((1, HEAD_DIM)), wspec],
        out_specs=[wspec, full((1, D)), full((1, HEAD_DIM))],
        out_shape=[SDS((N_DEV, rb, n), bf16), SDS((1, D), f32), SDS((1, HEAD_DIM), f32)],
        scratch_shapes=[pltpu.VMEM((M, n), bf16)], compiler_params=_params(),
    )(mem, mg_row, kv, dkn, dv, kg_row, w_kv_g)


SMALL = ("norm_g", "sgu_ln_g", "sgu_ln_b", "sgu_w", "sgu_b", "mem_norm_g", "q_norm_g", "k_norm_g")


def _small_rows(like):
    rows = [math.prod(like[n].shape) // 128 for n in SMALL]
    offs = [0]
    for r in rows:
        offs.append(offs[-1] + -(-r // 8) * 8)
    return rows, offs


def _pack_small(parts, offs):
    pieces = []
    for k, n in enumerate(SMALL):
        a = parts[n].reshape(-1, 128)
        pieces.append(jnp.pad(a, ((0, offs[k + 1] - offs[k] - a.shape[0]), (0, 0))))
    return jnp.concatenate(pieces)


def kernel(x, mem, norm_g, w_in, sgu_ln_g, sgu_ln_b, sgu_w, sgu_b, mem_norm_g, w_mem_kv, q_norm_g, k_norm_g, w_out, loss_target, m_norm_g, m_w_in, m_sgu_ln_g, m_sgu_ln_b, m_sgu_w, m_sgu_b, m_mem_norm_g, m_w_mem_kv, m_q_norm_g, m_k_norm_g, m_w_out, v_norm_g, v_w_in, v_sgu_ln_g, v_sgu_ln_b, v_sgu_w, v_sgu_b, v_mem_norm_g, v_w_mem_kv, v_q_norm_g, v_k_norm_g, v_w_out):
    L, D, wc = w_in.shape
    S = x.shape[1]
    da = D // 2
    xs = x.reshape(S, D)
    mems = mem.reshape(mem.shape[1], D)
    tgt = loss_target.reshape(S, D)
    row = lambda a, l: a[l].reshape(1, -1)
    tie = lambda a, tok: a + tok[0:1, 0:1]
    sb_col, xa_col = 3 * da // HEAD_DIM, (3 * da + D) // (D // 4)

    ax, ay, ac = lax.axis_index("x"), lax.axis_index("y"), lax.axis_index("c")
    ids = jnp.stack([4 * ax + 2 * ay + ac, 2 * ax + ay, ac]).astype(jnp.int32)
    w_b = [(_cast_into_slot("cast_w_in", w_in, l, 512, ids), _cast_into_slot("cast_w_kv", w_mem_kv, l, 256, ids),
            _cast_into_slot("cast_w_out", w_out, l, 256, ids)) for l in range(L)]
    first = _gather2_start("gather_w_in0", [w_b[0][0]], ids)
    in_fwd = _gather2_forward("gather_w_in0_forward", _split_wait(first, first["token"]), ids)

    acts = []
    xl = xs
    for l in range(L):
        g_row = row(norm_g, l)
        (w_in_g,) = _split_wait(in_fwd, g_row if l else in_fwd["token"])
        rest = _gather2_start(f"gather_w_rest{l}", [w_b[l][1], w_b[l][2]], w_in_g)
        proj, h = _rms_proj(xl, tie(g_row, rest["token"]), w_in_g)
        rest_fwd = _gather2_forward(f"gather_w_rest{l}_forward", _split_wait(rest, proj), proj)
        lg_row = tie(row(sgu_ln_g, l), rest_fwd["token"])
        if l + 1 < L:
            nxt = _gather2_start(f"gather_w_in{l + 1}", [w_b[l + 1][0]], proj)
            lg_row = tie(lg_row, nxt["token"])
        y = _sgu_fwd(proj, lg_row, row(sgu_ln_b, l), sgu_w[l], sgu_b[l].T)
        y, o_b, car = _sb_fwd(proj, y, sb_col, lg_row)
        w_kv_g, w_out_g = _split_wait(rest_fwd, o_b)
        kv = _mem_kv(mems, row(mem_norm_g, l), w_kv_g)
        y = _xattn_fwd(proj, kv, row(q_norm_g, l), row(k_norm_g, l), y, xa_col)
        order = kv
        if l + 1 < L:
            in_fwd = _gather2_forward(f"gather_w_in{l + 1}_forward", _split_wait(nxt, y), y)
            order = in_fwd["token"]
        x_next = _out_proj(xl, y, w_out_g, order)
        acts.append((xl, proj, h, y, o_b, car, kv, w_in_g, w_kv_g, w_out_g))
        xl = x_next

    dx, dxb, loss_part = _loss_and_grad(xl, tgt, 512)
    loss = lax.psum(loss_part[0, 0], ("x", "y", "c"))

    weights = dict(norm_g=norm_g, sgu_ln_g=sgu_ln_g, sgu_ln_b=sgu_ln_b, sgu_w=sgu_w, sgu_b=sgu_b,
                   mem_norm_g=mem_norm_g, q_norm_g=q_norm_g, k_norm_g=k_norm_g)
    moms_m = dict(norm_g=m_norm_g, sgu_ln_g=m_sgu_ln_g, sgu_ln_b=m_sgu_ln_b, sgu_w=m_sgu_w, sgu_b=m_sgu_b,
                  mem_norm_g=m_mem_norm_g, q_norm_g=m_q_norm_g, k_norm_g=m_k_norm_g)
    moms_v = dict(norm_g=v_norm_g, sgu_ln_g=v_sgu_ln_g, sgu_ln_b=v_sgu_ln_b, sgu_w=v_sgu_w, sgu_b=v_sgu_b,
                  mem_norm_g=v_mem_norm_g, q_norm_g=v_q_norm_g, k_norm_g=v_k_norm_g)
    small_rows, small_offs = _small_rows(weights)
    head_rows = D // 128
    assert SMALL[0] == "norm_g" and head_rows % 8 == 0

    seconds = {}
    pending = None
    small = {n: [None] * L for n in SMALL}
    for l in reversed(range(L)):
        xl, proj, h, y, o_b, car, kv, w_in_g, w_kv_g, w_out_g = acts[l]
        dy = _out_bwd_dy(dxb, w_out_g)
        order = dy
        if pending is not None:
            seconds[pending[0]] = _scatter2_second_level(pending[0], pending[1], dy, ids)
            order = seconds[pending[0]]["token"]
        g_out = _tn_grad("out_bwd_dw", y, dxb, 512, 512, True, order)
        first = _scatter2_pair_start(f"scatter_g_out{l}_pair", [g_out], ids)
        dproj, d_sw, d_sb, d_lg, d_lb = _sgu_bwd(proj, dy, tie(row(sgu_ln_g, l), first["token"]), row(sgu_ln_b, l),
                                                 sgu_w[l], sgu_b[l].T)
        seconds[f"g_out{l}"] = _scatter2_second_level(f"g_out{l}", first, dproj, ids)
        dproj = _sb_bwd(proj, o_b, car, dy, dproj, sb_col, seconds[f"g_out{l}"]["token"])
        dproj, dkn, dv, d_qg = _xattn_bwd(proj, kv, row(q_norm_g, l), row(k_norm_g, l), dy, dproj, xa_col)
        g_kv, d_mg, d_kg = _mem_bwd(mems, row(mem_norm_g, l), kv, dkn, dv, row(k_norm_g, l), w_kv_g)
        for n, val in (("sgu_ln_g", d_lg), ("sgu_ln_b", d_lb), ("sgu_w", d_sw), ("sgu_b", d_sb[:, :A_GROUPS].T),
                       ("mem_norm_g", d_mg), ("q_norm_g", d_qg), ("k_norm_g", d_kg)):
            small[n][l] = val.reshape(-1)
        order = d_kg
        if l == 0:
            small["norm_g"][0] = jnp.zeros_like(small["norm_g"][1])
            part = _pack_small({n: jnp.stack(small[n]) for n in SMALL}, small_offs)
            tail = _gather2_start("gather_small_tail", [_into_slot("small_tail_slot", part[head_rows:], ids)], ids)
            order = tail["token"]
        g_in_l = _tn_grad("in_bwd_dw", h, dproj, 1024, wc, False, order)
        first = _scatter2_pair_start(f"scatter_g_rest{l}_pair", [g_kv, g_in_l], ids)
        g_row = tie(row(norm_g, l), first["token"])
        pending = (f"g_rest{l}", first)
        if l == 0:
            tail_fwd = _gather2_forward("gather_small_tail_forward", _split_wait(tail, g_in_l), g_in_l)
            seconds[pending[0]] = _scatter2_second_level(pending[0], pending[1], tail_fwd["token"], ids)
            g_row = tie(g_row, seconds[pending[0]]["token"])
        dh = _in_bwd_dh(dproj, w_in_g, g_row)
        dx, dxb, d_ng = _rms_bwd(dh, xl, g_row, dx)
        small["norm_g"][l] = d_ng.reshape(-1)
    head = _gather2_start("gather_small_head",
                          [_into_slot("small_head_slot", small["norm_g"][0].reshape(head_rows, 128), ids)], ids)

    a_out = a_kv = a_in = None
    order = head["token"]
    for l in reversed(range(L)):
        (r_out,) = _scatter2_finish(seconds[f"g_out{l}"], order)
        a_out = _sum_adam("adam_w_out", r_out, w_out, m_w_out, v_w_out, l, a_out, 128, ids)
        order = a_out[0]
        if l == 0:
            head_fwd = _gather2_forward("gather_small_head_forward", _split_wait(head, order), ids)
            (r_tail,) = _split_wait(tail_fwd, order)
            (r_head,) = _split_wait(head_fwd, r_tail)
            as128 = lambda d: [d[n].reshape(-1, 128) for n in SMALL]
            sm = _small_sum_adam(r_head, r_tail, as128(weights), as128(moms_m), as128(moms_v), small_offs)
            order = sm[0]
        r_kv, r_in = _scatter2_finish(seconds[f"g_rest{l}"], order)
        a_kv = _sum_adam("adam_w_kv", r_kv, w_mem_kv, m_w_mem_kv, v_w_mem_kv, l, a_kv, 256, ids)
        a_in = _sum_adam("adam_w_in", r_in, w_in, m_w_in, v_w_in, l, a_in, 256, ids)
        order = a_in[0]
    res = {"w_out": a_out, "w_mem_kv": a_kv, "w_in": a_in}
    for p, n in enumerate(SMALL):
        res[n] = [sm[k * len(SMALL) + p].reshape(weights[n].shape) for k in range(4)]

    order = ("norm_g", "w_in", "sgu_ln_g", "sgu_ln_b", "sgu_w", "sgu_b", "mem_norm_g", "w_mem_kv", "q_norm_g",
             "k_norm_g", "w_out")
    outs = [loss, dx.reshape(x.shape)]
    for k in range(4):
        outs += [res[n][k] for n in order]
    return tuple(outs)
```

```python
import functools
import math

import jax
import jax.numpy as jnp
from jax import lax
from jax.experimental import pallas as pl
from jax.experimental.pallas import tpu as pltpu

f32 = jnp.float32
bf16 = jnp.bfloat16
SDS = jax.ShapeDtypeStruct

N_DEV = 8
EPS = 1e-6
CHUNK = 128
A_GROUPS = 8
HEAD_DIM = 128
N_HEADS = 4
TQ = 256
TK = 128
ADAM_LR, ADAM_B1, ADAM_B2, ADAM_EPS, ADAM_WD, ADAM_STEP = 0.001, 0.9, 0.999, 1e-08, 0.01, 10
MIB = 1024 * 1024

NT = (((1,), (1,)), ((), ()))
TN = (((0,), (0,)), ((), ()))


def _params(vmem_mib=48):
    return pltpu.CompilerParams(vmem_limit_bytes=vmem_mib * MIB)


def _gelu_and_grad(x):
    e = lax.erf(x * (1.0 / math.sqrt(2.0)))
    cdf = 0.5 * (1.0 + e)
    pdf = jnp.exp(-0.5 * x * x) * (1.0 / math.sqrt(2.0 * math.pi))
    return x * cdf, cdf + x * pdf


def _gelu(x):
    return 0.5 * x * (1.0 + lax.erf(x * (1.0 / math.sqrt(2.0))))


def _silu_and_grad(z):
    sg = jax.nn.sigmoid(z)
    return z * sg, sg * (1.0 + z * (1.0 - sg))


def _dot(a, b, dims=None):
    if dims is None:
        return jnp.dot(a, b, preferred_element_type=f32)
    return lax.dot_general(a, b, dims, preferred_element_type=f32)


_HBM = pl.BlockSpec(memory_space=pltpu.HBM)
_SEM = pl.BlockSpec(memory_space=pltpu.SEMAPHORE)
_EFFECT = pltpu.SideEffectType.DATAFLOW_SIDE_EFFECTING


def _split_start(name, bufs, n_remote, n_local, build, after):
    nb = len(bufs)

    def body(*refs):
        token = refs[-1]
        locals_, remotes = build(refs[:nb], *refs[nb + 1:nb + 4])
        for cp in locals_ + remotes:
            cp.start()
        token[...] = jnp.zeros_like(token)

    hbm = lambda a: pltpu.with_memory_space_constraint(a, pltpu.HBM)
    outs = pl.pallas_call(
        body, name=name,
        out_shape=(pltpu.SemaphoreType.DMA((n_remote,)), pltpu.SemaphoreType.DMA((n_remote,)),
                   pltpu.SemaphoreType.DMA((max(n_local, 1),)),
                   *[pltpu.HBM(b.shape, b.dtype) for b in bufs], SDS((8, 128), f32)),
        in_specs=[_HBM] * nb + [pl.BlockSpec(memory_space=pl.ANY)],
        out_specs=(_SEM, _SEM, _SEM, *[_HBM] * nb, pl.BlockSpec(memory_space=pltpu.VMEM)),
        input_output_aliases={k: 3 + k for k in range(nb)},
        compiler_params=pltpu.CompilerParams(has_side_effects=_EFFECT),
    )(*[hbm(b) for b in bufs], after)
    return dict(name=name, build=build, sems=outs[:3], bufs=outs[3:3 + nb], token=outs[-1])


def _split_wait(handle, *after):
    build, bufs = handle["build"], handle["bufs"]
    nb = len(bufs)

    def body(*refs):
        locals_, remotes = build(refs[:nb], *refs[nb:nb + 3])
        for cp in remotes:
            cp.wait_recv()
        for cp in remotes:
            cp.wait_send()
        for cp in locals_:
            cp.wait()

    outs = pl.pallas_call(
        body, name=handle["name"] + "_wait",
        out_shape=tuple(pltpu.HBM(b.shape, b.dtype) for b in bufs),
        in_specs=[_HBM] * nb + [_SEM] * 3 + [pl.BlockSpec(memory_space=pl.ANY)] * len(after),
        out_specs=tuple([_HBM] * nb),
        input_output_aliases={k: k for k in range(nb)},
        compiler_params=pltpu.CompilerParams(has_side_effects=_EFFECT),
    )(*bufs, *handle["sems"], *after)
    return list(outs)


def _remote(src, dst, send_sems, recv_sems, k, to):
    return pltpu.make_async_remote_copy(src_ref=src, dst_ref=dst, send_sem=send_sems.at[k], recv_sem=recv_sems.at[k],
                                        device_id=to, device_id_type=pl.DeviceIdType.MESH)


def _other_chips(x, y):
    return [(1 - x, y), (x, 1 - y), (1 - x, 1 - y)]


def _gather2_start(name, lands, after):
    def build(refs, send, recv, loc):
        x, y, c = lax.axis_index("x"), lax.axis_index("y"), lax.axis_index("c")
        me = 4 * x + 2 * y + c
        remotes = []
        for a, d in enumerate(refs):
            remotes.append(_remote(d.at[me], d.at[me], send, recv, 4 * a, (x, y, 1 - c)))
            remotes += [_remote(d.at[me], d.at[me], send, recv, 4 * a + 1 + k, (px, py, c))
                        for k, (px, py) in enumerate(_other_chips(x, y))]
        return [], remotes

    return _split_start(name, list(lands), 4 * len(lands), 0, build, after)


def _gather2_forward(name, lands, after):
    n = len(lands)

    def build(refs, send, recv, loc):
        x, y, c = lax.axis_index("x"), lax.axis_index("y"), lax.axis_index("c")
        slots = [4 * px + 2 * py + c for px, py in _other_chips(x, y)]
        return [], [_remote(d.at[sl], d.at[sl], send, recv, 3 * a + k, (x, y, 1 - c))
                    for a, d in enumerate(refs) for k, sl in enumerate(slots)]

    return _split_start(name, list(lands), 3 * n, 0, build, after)


def _scatter2_pair_start(name, srcs, after):
    n = len(srcs)

    def build(refs, send, recv, loc):
        x, y, c = lax.axis_index("x"), lax.axis_index("y"), lax.axis_index("c")
        return [], [_remote(refs[a].at[2 * q + 1 - c], refs[n + a].at[q], send, recv, 4 * a + q, (x, y, 1 - c))
                    for a in range(n) for q in range(4)]

    lands = [lax.empty((4,) + s.shape[1:], s.dtype) for s in srcs]
    return _split_start(name, list(srcs) + lands, 4 * n, 0, build, after)


def _scatter2_chip_start(name, pairs, after):
    n = len(pairs)

    def build(refs, send, recv, loc):
        x, y, c = lax.axis_index("x"), lax.axis_index("y"), lax.axis_index("c")
        return [], [_remote(refs[a].at[2 * px + py], refs[n + a].at[2 * x + y], send, recv, 3 * a + k, (px, py, c))
                    for a in range(n) for k, (px, py) in enumerate(_other_chips(x, y))]

    return _split_start(name, list(pairs) + [lax.empty(p.shape, p.dtype) for p in pairs], 3 * n, 0, build, after)


def _pair_sum(name, src, theirs, ids):
    _, R, C = theirs.shape
    tr = min(R, 1024)

    def body(ids_ref, a_ref, b_ref, o_ref):
        o_ref[...] = (a_ref[...].astype(f32) + b_ref[...].astype(f32)).astype(bf16)

    spec = pl.BlockSpec((None, tr, C), lambda q, i, ids: (q, i, 0))
    return pl.pallas_call(
        body, name=name,
        grid_spec=pltpu.PrefetchScalarGridSpec(
            num_scalar_prefetch=1, grid=(4, R // tr),
            in_specs=[pl.BlockSpec((None, tr, C), lambda q, i, ids: (2 * q + ids[2], i, 0)), spec], out_specs=spec),
        out_shape=SDS(theirs.shape, bf16), compiler_params=_params(),
    )(ids, src, theirs)


def _scatter2_second_level(name, first, after, ids):
    outs = _split_wait(first, after)
    n = len(outs) // 2
    pairs = [_pair_sum(f"pair_sum_{name}{a}", outs[a], outs[n + a], ids) for a in range(n)]
    return _scatter2_chip_start(f"scatter_{name}_chip", pairs, ids)


def _scatter2_finish(second, after):
    outs = _split_wait(second, after)
    n = len(outs) // 2
    return [(outs[a], outs[n + a]) for a in range(n)]


def _cast_into_slot(name, w, l, tr, ids, after):
    _, R, C = w.shape

    def body(ids_ref, w_ref, after_ref, o_ref):
        o_ref[...] = w_ref[...].astype(bf16)

    return pl.pallas_call(
        body, name=name,
        grid_spec=pltpu.PrefetchScalarGridSpec(
            num_scalar_prefetch=1, grid=(R // tr,),
            in_specs=[pl.BlockSpec((None, tr, C), lambda i, ids: (l, i, 0)), pl.BlockSpec(memory_space=pl.ANY)],
            out_specs=pl.BlockSpec((None, tr, C), lambda i, ids: (ids[0], i, 0))),
        out_shape=SDS((N_DEV, R, C), bf16), compiler_params=_params(),
    )(ids, w, after)


def _into_slot(name, a, ids):
    R, C = a.shape

    def body(ids_ref, a_ref, o_ref):
        o_ref[...] = a_ref[...]

    return pl.pallas_call(
        body, name=name,
        grid_spec=pltpu.PrefetchScalarGridSpec(
            num_scalar_prefetch=1, grid=(1,),
            in_specs=[pl.BlockSpec((R, C), lambda i, ids: (0, 0))],
            out_specs=pl.BlockSpec((None, R, C), lambda i, ids: (ids[0], 0, 0))),
        out_shape=SDS((N_DEV, R, C), f32), compiler_params=_params(),
    )(ids, a)


def _adam_math(w, g, m, v):
    m2 = ADAM_B1 * m + (1.0 - ADAM_B1) * g
    v2 = ADAM_B2 * v + (1.0 - ADAM_B2) * (g * g)
    m_hat = m2 / (1.0 - ADAM_B1 ** ADAM_STEP)
    v_hat = v2 / (1.0 - ADAM_B2 ** ADAM_STEP)
    delta = -ADAM_LR * (m_hat / (jnp.sqrt(v_hat) + ADAM_EPS) + ADAM_WD * w)
    return delta, m2, v2


def _sum_adam(name, pair_recv, w, m, v, l, prev, tr, ids):
    own, recv = pair_recv
    L, R, C = w.shape
    slots = recv.shape[0]

    def body(ids_ref, r_ref, own_ref, w_ref, m_ref, v_ref, *rest):
        g_ref, d_ref, m2_ref, v2_ref = rest[-4:]
        terms = [jnp.where(ids_ref[1] == q, own_ref[...], r_ref[q]).astype(f32) for q in range(slots)]
        g = terms[0]
        for t in terms[1:]:
            g = g + t
        d, m2, v2 = _adam_math(w_ref[...], g, m_ref[...], v_ref[...])
        g_ref[...] = g
        d_ref[...] = d
        m2_ref[...] = m2
        v2_ref[...] = v2

    wspec = pl.BlockSpec((None, tr, C), lambda i, ids: (l, i, 0))
    in_specs = [pl.BlockSpec((slots, tr, C), lambda i, ids: (0, i, 0)),
                pl.BlockSpec((None, tr, C), lambda i, ids: (ids[1], i, 0)), wspec, wspec, wspec]
    args = [ids, recv, own, w, m, v]
    aliases = {}
    if prev is not None:
        in_specs += [pl.BlockSpec(memory_space=pl.ANY)] * 4
        args += list(prev)
        aliases = {6 + k: k for k in range(4)}
    return pl.pallas_call(
        body, name=name,
        grid_spec=pltpu.PrefetchScalarGridSpec(num_scalar_prefetch=1, grid=(R // tr,), in_specs=in_specs,
                                               out_specs=[wspec] * 4),
        out_shape=[SDS((L, R, C), f32)] * 4, input_output_aliases=aliases, compiler_params=_params(),
    )(*args)


def _small_sum_adam(recv_head, recv_tail, ws, ms, vs, offs):
    n = len(ws)
    r0 = recv_head.shape[1]

    def body(*refs):
        rh, rt = refs[0], refs[1]
        w_refs, m_refs, v_refs = refs[2:2 + n], refs[2 + n:2 + 2 * n], refs[2 + 2 * n:2 + 3 * n]
        outs = refs[2 + 3 * n:]
        for p in range(n):
            lo, hi = offs[p], offs[p] + ws[p].shape[0]
            pieces = []
            if lo < r0:
                pieces.append((rh, lo, 0, min(hi, r0) - lo))
            if hi > r0:
                pieces.append((rt, max(lo, r0) - r0, max(lo, r0) - lo, hi - max(lo, r0)))
            for src, a, b, cnt in pieces:
                g = src[0, a:a + cnt, :]
                for s in range(1, N_DEV):
                    g = g + src[s, a:a + cnt, :]
                d, m2, v2 = _adam_math(w_refs[p][b:b + cnt, :], g, m_refs[p][b:b + cnt, :], v_refs[p][b:b + cnt, :])
                for k, val in enumerate((g, d, m2, v2)):
                    outs[k * n + p][b:b + cnt, :] = val

    return pl.pallas_call(
        body, name="small_sum_adam", out_shape=[SDS(w.shape, f32) for w in ws] * 4, compiler_params=_params(),
    )(recv_head, recv_tail, *ws, *ms, *vs)


def _loss_and_grad(xf, tgt, tm):
    S, D = xf.shape

    def body(x_ref, t_ref, dx_ref, dxb_ref, l_ref):
        i = pl.program_id(0)
        d = x_ref[...] - t_ref[...]
        dx = d * (1.0 / D)
        dx_ref[...] = dx
        dxb_ref[...] = dx.astype(bf16)
        e = d * d
        part = e[:, 0:128]
        for k in range(1, D // 128):
            part = part + e[:, k * 128:(k + 1) * 128]
        part = jnp.sum(part.reshape(tm // 8, 8, 128), axis=0)

        @pl.when(i == 0)
        def _():
            l_ref[...] = jnp.zeros_like(l_ref)

        l_ref[...] += part

        @pl.when(i == pl.num_programs(0) - 1)
        def _():
            tot = jnp.sum(l_ref[...], axis=1, keepdims=True)
            tot = jnp.sum(tot, axis=0, keepdims=True)
            l_ref[...] = jnp.broadcast_to(tot * (0.5 / D), l_ref.shape)

    row = pl.BlockSpec((tm, D), lambda i: (i, 0))
    return pl.pallas_call(
        body, name="loss_grad", grid=(S // tm,),
        in_specs=[row, row], out_specs=[row, row, pl.BlockSpec((8, 128), lambda i: (0, 0))],
        out_shape=[SDS((S, D), f32), SDS((S, D), bf16), SDS((8, 128), f32)], compiler_params=_params(),
    )(xf, tgt)


def _rms_proj(x, g_row, w_in_g, tm=1024):
    S, D = x.shape
    wc = w_in_g.shape[2]
    n_out = N_DEV * wc

    def body(x_ref, g_ref, w_ref, proj_ref, h_ref):
        @pl.when(pl.program_id(1) == 0)
        def _():
            xv = x_ref[...]
            r = lax.rsqrt(jnp.mean(xv * xv, axis=-1, keepdims=True) + EPS)
            h_ref[...] = (xv * r * g_ref[...]).astype(bf16)

        proj_ref[...] = _dot(h_ref[...], w_ref[...])

    return pl.pallas_call(
        body, name="rms_proj", grid=(S // tm, N_DEV),
        in_specs=[pl.BlockSpec((tm, D), lambda i, j: (i, 0)),
                  pl.BlockSpec((1, D), lambda i, j: (0, 0)),
                  pl.BlockSpec((None, D, wc), lambda i, j: (j, 0, 0))],
        out_specs=[pl.BlockSpec((tm, wc), lambda i, j: (i, j)), pl.BlockSpec((tm, D), lambda i, j: (i, 0))],
        out_shape=[SDS((S, n_out), f32), SDS((S, D), bf16)], compiler_params=_params(),
    )(x, g_row, w_in_g)


def _out_proj(x, y, w_out_g, after, tm=512):
    S, D = x.shape
    rb = w_out_g.shape[1]

    def body(x_ref, y_ref, w_ref, after_ref, o_ref):
        w = w_ref[...].reshape(N_DEV * rb, D)
        o_ref[...] = x_ref[...] + _dot(y_ref[...], w)

    row = pl.BlockSpec((tm, D), lambda i: (i, 0))
    return pl.pallas_call(
        body, name="out_proj", grid=(S // tm,),
        in_specs=[row, row, pl.BlockSpec((N_DEV, rb, D), lambda i: (0, 0, 0)), pl.BlockSpec(memory_space=pl.ANY)],
        out_specs=row, out_shape=SDS((S, D), f32), compiler_params=_params(),
    )(x, y, w_out_g, after)


def _out_bwd_dy(dxb, w_out_g, tm=512):
    S, D = dxb.shape
    rb = w_out_g.shape[1]

    nb = 2

    def body(dx_ref, w_ref, o_ref):
        o_ref[...] = _dot(dx_ref[...], w_ref[...].reshape(nb * rb, D), NT)

    return pl.pallas_call(
        body, name="out_bwd_dy", grid=(S // tm, N_DEV // nb),
        in_specs=[pl.BlockSpec((tm, D), lambda i, j: (i, 0)),
                  pl.BlockSpec((nb, rb, D), lambda i, j: (j, 0, 0))],
        out_specs=pl.BlockSpec((tm, nb * rb), lambda i, j: (i, j)),
        out_shape=SDS((S, D), f32), compiler_params=_params(),
    )(dxb, w_out_g)


def _tn_grad(name, a, b, tm, tn, rows_major, after):
    S, M = a.shape
    N = b.shape[1]
    if rows_major:
        rb = M // N_DEV
        nb = tm // rb
        out_shape = SDS((N_DEV, rb, N), bf16)
        out_spec = pl.BlockSpec((nb, rb, tn), lambda i, j: (i, 0, j))
    else:
        out_shape = SDS((N_DEV, M, N // N_DEV), bf16)
        assert tn == N // N_DEV
        out_spec = pl.BlockSpec((None, tm, tn), lambda i, j: (j, i, 0))

    def body(a_ref, b_ref, after_ref, o_ref):
        o_ref[...] = _dot(a_ref[...], b_ref[...], TN).astype(bf16).reshape(o_ref.shape)

    return pl.pallas_call(
        body, name=name, grid=(M // tm, N // tn),
        in_specs=[pl.BlockSpec((S, tm), lambda i, j: (0, i)), pl.BlockSpec((S, tn), lambda i, j: (0, j)),
                  pl.BlockSpec(memory_space=pl.ANY)],
        out_specs=out_spec, out_shape=out_shape, compiler_params=_params(),
    )(a, b, after)


def _in_bwd_dh(dproj, w_in_g, after, tm=1024, tn=256):
    S = dproj.shape[0]
    _, D, wc = w_in_g.shape
    tm = min(tm, S)

    def body(dp_ref, w_ref, after_ref, o_ref):
        acc = _dot(dp_ref[:, 0:wc], w_ref[0], NT)
        for k in range(1, N_DEV):
            acc = acc + _dot(dp_ref[:, k * wc:(k + 1) * wc], w_ref[k], NT)
        o_ref[...] = acc

    return pl.pallas_call(
        body, name="in_bwd_dh", grid=(S // tm, D // tn),
        in_specs=[pl.BlockSpec((tm, N_DEV * wc), lambda i, j: (i, 0)),
                  pl.BlockSpec((N_DEV, tn, wc), lambda i, j: (0, j, 0)), pl.BlockSpec(memory_space=pl.ANY)],
        out_specs=pl.BlockSpec((tm, tn), lambda i, j: (i, j)),
        out_shape=SDS((S, D), f32), compiler_params=_params(),
    )(dproj, w_in_g, after)


def _rms_bwd(dh, x, g_row, dx_next, tm=256):
    S, D = x.shape

    def body(dh_ref, x_ref, g_ref, dxn_ref, dx_ref, dxb_ref, dg_ref):
        @pl.when(pl.program_id(0) == 0)
        def _():
            dg_ref[...] = jnp.zeros_like(dg_ref)

        dh = dh_ref[...]
        xv = x_ref[...]
        r = lax.rsqrt(jnp.mean(xv * xv, axis=-1, keepdims=True) + EPS)
        xhat = xv * r
        dxhat = dh * g_ref[...]
        dx = r * (dxhat - xhat * jnp.mean(dxhat * xhat, axis=-1, keepdims=True)) + dxn_ref[...]
        dx_ref[...] = dx
        dxb_ref[...] = dx.astype(bf16)
        dg_ref[...] += jnp.sum(dh * xhat, axis=0, keepdims=True)

    row = pl.BlockSpec((tm, D), lambda i: (i, 0))
    vec = pl.BlockSpec((1, D), lambda i: (0, 0))
    return pl.pallas_call(
        body, name="rms_bwd", grid=(S // tm,), in_specs=[row, row, vec, row], out_specs=[row, row, vec],
        out_shape=[SDS((S, D), f32), SDS((S, D), bf16), SDS((1, D), f32)], compiler_params=_params(),
    )(dh, x, g_row, dx_next)


def _sgu_fwd(proj, ln_g, ln_b, w_s, b_t):
    S = proj.shape[0]
    da = A_GROUPS * HEAD_DIM
    D = 2 * da

    def body(u_ref, v_ref, z_ref, lg_ref, lb_ref, w_ref, bt_ref, y_ref):
        u = _gelu(u_ref[...])
        v = _gelu(v_ref[...])
        z = z_ref[...]
        mu = jnp.mean(v, axis=-1, keepdims=True)
        xc = v - mu
        rs = lax.rsqrt(jnp.mean(xc * xc, axis=-1, keepdims=True) + EPS)
        vn = (xc * rs * lg_ref[...] + lb_ref[...]).astype(bf16)
        gate = u * (z * jax.nn.sigmoid(z))
        tri = lax.broadcasted_iota(jnp.int32, (CHUNK, CHUNK), 0) >= lax.broadcasted_iota(jnp.int32, (CHUNK, CHUNK), 1)
        for g in range(A_GROUPS):
            sl = slice(g * HEAD_DIM, (g + 1) * HEAD_DIM)
            wm = jnp.where(tri, w_ref[g], 0.0).astype(bf16)
            mixed = _dot(wm, vn[:, sl]) + bt_ref[:, g:g + 1]
            y_ref[:, sl] = (gate[:, sl] * mixed).astype(bf16)

    blk = lambda cb: pl.BlockSpec((CHUNK, da), lambda c: (c, cb))
    full = lambda shp: pl.BlockSpec(shp, lambda c: (0,) * len(shp))
    return pl.pallas_call(
        body, name="sgu_fwd", grid=(S // CHUNK,),
        in_specs=[blk(0), blk(1), blk(2), full((1, da)), full((1, da)),
                  full((A_GROUPS, CHUNK, CHUNK)), full((CHUNK, A_GROUPS))],
        out_specs=blk(0), out_shape=SDS((S, D), bf16), compiler_params=_params(),
    )(proj, proj, proj, ln_g, ln_b, w_s, b_t)


def _sgu_bwd(proj, dy, ln_g, ln_b, w_s, b_t):
    S = proj.shape[0]
    da = A_GROUPS * HEAD_DIM
    n_proj = proj.shape[1]

    def body(u_ref, v_ref, z_ref, dy_ref, lg_ref, lb_ref, w_ref, bt_ref,
             dp_ref, dw_ref, db_ref, dlg_ref, dlb_ref, dvn_ref):
        @pl.when(pl.program_id(0) == 0)
        def _():
            dw_ref[...] = jnp.zeros_like(dw_ref)
            db_ref[...] = jnp.zeros_like(db_ref)
            dlg_ref[...] = jnp.zeros_like(dlg_ref)
            dlb_ref[...] = jnp.zeros_like(dlb_ref)

        up, vp, z, dy = u_ref[...], v_ref[...], z_ref[...], dy_ref[...]
        u, gu = _gelu_and_grad(up)
        v, gv = _gelu_and_grad(vp)
        s, gs = _silu_and_grad(z)
        mu = jnp.mean(v, axis=-1, keepdims=True)
        xc = v - mu
        rs = lax.rsqrt(jnp.mean(xc * xc, axis=-1, keepdims=True) + EPS)
        vhat = xc * rs
        lg = lg_ref[...]
        vn = (vhat * lg + lb_ref[...]).astype(bf16)
        tri = lax.broadcasted_iota(jnp.int32, (CHUNK, CHUNK), 0) >= lax.broadcasted_iota(jnp.int32, (CHUNK, CHUNK), 1)
        lane = lax.broadcasted_iota(jnp.int32, (CHUNK, HEAD_DIM), 1)
        dys = dy * s
        db = jnp.zeros((CHUNK, HEAD_DIM), f32)
        for g in range(A_GROUPS):
            sl = slice(g * HEAD_DIM, (g + 1) * HEAD_DIM)
            wm = jnp.where(tri, w_ref[g], 0.0).astype(bf16)
            mixed = _dot(wm, vn[:, sl]) + bt_ref[:, g:g + 1]
            dmix = dys[:, sl] * u[:, sl]
            dp_ref[:, sl] = (dys[:, sl] * mixed * gu[:, sl]).astype(bf16)
            dp_ref[:, 2 * da + g * HEAD_DIM:2 * da + (g + 1) * HEAD_DIM] = (
                dy[:, sl] * u[:, sl] * mixed * gs[:, sl]).astype(bf16)
            dmb = dmix.astype(bf16)
            dw_ref[g] += jnp.where(tri, _dot(dmb, vn[:, sl], NT), 0.0)
            dvn_ref[:, sl] = _dot(wm, dmb, TN)
            db = db + jnp.where(lane == g, jnp.sum(dmix, axis=1, keepdims=True), 0.0)
        db_ref[...] += db
        dvn = dvn_ref[...]
        dlg_ref[...] += jnp.sum(dvn * vhat, axis=0, keepdims=True)
        dlb_ref[...] += jnp.sum(dvn, axis=0, keepdims=True)
        dvhat = dvn * lg
        dv = rs * (dvhat - jnp.mean(dvhat, axis=-1, keepdims=True)
                   - vhat * jnp.mean(dvhat * vhat, axis=-1, keepdims=True))
        dp_ref[:, da:2 * da] = (dv * gv).astype(bf16)

    blk = lambda cb: pl.BlockSpec((CHUNK, da), lambda c: (c, cb))
    full = lambda shp: pl.BlockSpec(shp, lambda c: (0,) * len(shp))
    return pl.pallas_call(
        body, name="sgu_bwd", grid=(S // CHUNK,),
        in_specs=[blk(0), blk(1), blk(2), blk(0), full((1, da)), full((1, da)),
                  full((A_GROUPS, CHUNK, CHUNK)), full((CHUNK, A_GROUPS))],
        out_specs=[pl.BlockSpec((CHUNK, 3 * da), lambda c: (c, 0)), full((A_GROUPS, CHUNK, CHUNK)),
                   full((CHUNK, HEAD_DIM)), full((1, da)), full((1, da))],
        out_shape=[SDS((S, n_proj), bf16), SDS((A_GROUPS, CHUNK, CHUNK), f32), SDS((CHUNK, HEAD_DIM), f32),
                   SDS((1, da), f32), SDS((1, da), f32)],
        scratch_shapes=[pltpu.VMEM((CHUNK, da), f32)], compiler_params=_params(),
    )(proj, proj, proj, dy, ln_g, ln_b, w_s, b_t)


def _sb_scores(q, kblk, kb, rows, cols, masked):
    z = _dot(q, kblk, NT) * (1.0 / math.sqrt(HEAD_DIM))
    t = jnp.log(1.0 + jnp.exp(-jnp.abs(z)))
    log_1mb = -(jnp.maximum(z, 0.0) + t)
    log_beta = jnp.minimum(z, 0.0) - t
    if not masked:
        return None, log_beta, log_1mb
    causal = (cols + kb * TK) < rows
    return causal, log_beta, jnp.where(causal, log_1mb, 0.0)


def _sb_tiles(i):
    rows = i * TQ + lax.broadcasted_iota(jnp.int32, (TQ, TK), 0)
    cols = lax.broadcasted_iota(jnp.int32, (TQ, TK), 1)
    r_i = lax.broadcasted_iota(jnp.int32, (TK, TK), 0)
    c_i = lax.broadcasted_iota(jnp.int32, (TK, TK), 1)
    upper, lower = (r_i > c_i).astype(bf16), (r_i < c_i).astype(bf16)
    return rows, cols, jnp.concatenate([upper, upper], axis=0), jnp.concatenate([lower, lower], axis=0)


def _suffix_sum(t, tri):
    hi = lax.bitcast_convert_type(lax.bitcast_convert_type(t, jnp.uint32) & jnp.uint32(0xFFFF0000), f32)
    both = jnp.concatenate([hi.astype(bf16), (t - hi).astype(bf16)], axis=1)
    return _dot(both, tri)


def _sb_fwd(proj, y_prev, col0, after):
    S = proj.shape[0]
    D = y_prev.shape[1]
    dh = N_HEADS * HEAD_DIM
    n_diag = TQ // TK

    def body(q_ref, k_ref, v_ref, z_ref, yp_ref, after_ref, y_ref, o_ref, car_ref, qb, kb_s, vb_s, c_ref):
        i = pl.program_id(0)

        @pl.when(i == 0)
        def _():
            kb_s[...] = k_ref[...].astype(bf16)
            vb_s[...] = v_ref[...].astype(bf16)

        qb[...] = q_ref[...].astype(bf16)
        o_ref[...] = jnp.zeros_like(o_ref)
        c_ref[...] = jnp.zeros_like(c_ref)
        car_ref[...] = jnp.zeros_like(car_ref)
        nkb = (i + 1) * n_diag
        rows, cols, upper, _ = _sb_tiles(i)

        def make_step(masked):
            def step(jj, carry):
                kb = nkb - 1 - jj
                off = pl.multiple_of(kb * TK, TK)
                hs = range(N_HEADS)
                sls = [slice(h * HEAD_DIM, (h + 1) * HEAD_DIM) for h in hs]
                sc = [_sb_scores(qb[:, sls[h]], kb_s[pl.ds(off, TK), sls[h]], kb, rows, cols, masked) for h in hs]
                suf = [_suffix_sum(sc[h][2], upper) for h in hs]
                cs = [c_ref[h] for h in hs]
                es = [jnp.exp(sc[h][1] + suf[h] + cs[h]) for h in hs]
                if masked:
                    es = [jnp.where(sc[h][0], es[h], 0.0) for h in hs]
                pv = [_dot(es[h].astype(bf16), vb_s[pl.ds(off, TK), sls[h]]) for h in hs]
                for h in hs:
                    o_ref[:, sls[h]] += pv[h]
                    car_ref[h] = jnp.where(cols == kb, cs[h], car_ref[h])
                    c_ref[h] = cs[h] + jnp.sum(sc[h][2], axis=1, keepdims=True)
                return carry
            return step

        lax.fori_loop(0, n_diag, make_step(True), 0)
        lax.fori_loop(n_diag, nkb, make_step(False), 0)
        z = z_ref[...]
        y_ref[...] = (o_ref[...] * (z * jax.nn.sigmoid(z))).astype(bf16)

    cb = col0 * HEAD_DIM // dh
    qspec = lambda k: pl.BlockSpec((TQ, dh), lambda i: (i, cb + k))
    kspec = lambda k: pl.BlockSpec((S, dh), lambda i: (0, cb + k))
    return pl.pallas_call(
        body, name="sb_fwd", grid=(S // TQ,),
        in_specs=[qspec(0), kspec(1), kspec(2), qspec(3), pl.BlockSpec(memory_space=pl.ANY),
                  pl.BlockSpec(memory_space=pl.ANY)],
        out_specs=[pl.BlockSpec((TQ, dh), lambda i: (i, A_GROUPS * HEAD_DIM // dh)),
                   pl.BlockSpec((TQ, dh), lambda i: (i, 0)),
                   pl.BlockSpec((N_HEADS, TQ, TK), lambda i: (0, i, 0))],
        out_shape=[SDS((S, D), bf16), SDS((S, dh), f32), SDS((N_HEADS, S, TK), f32)],
        input_output_aliases={4: 0},
        scratch_shapes=[pltpu.VMEM((TQ, dh), bf16), pltpu.VMEM((S, dh), bf16), pltpu.VMEM((S, dh), bf16),
                        pltpu.VMEM((N_HEADS, TQ, TK), f32)],
        compiler_params=_params(),
    )(proj, proj, proj, proj, y_prev, after)


def _sb_bwd(proj, o, car, dy, dproj_prev, col0, after):
    S = proj.shape[0]
    n_i = S // TQ
    dh = N_HEADS * HEAD_DIM
    n_diag = TQ // TK
    cb = col0 * HEAD_DIM // dh
    scale = 1.0 / math.sqrt(HEAD_DIM)

    def body(q_ref, k_ref, v_ref, z_ref, o_ref, car_ref, dy_ref, dpp_ref, after_ref,
             dp_ref, qb, kb_s, vb_s, dob, p_ref, dq_acc, dk_acc, dv_acc, st_a, st_b, st_k, st_v):
        i = pl.program_id(0)

        def put(stage_ref, row0, nrows, k):
            pltpu.sync_copy(stage_ref, dp_ref.at[pl.ds(row0, nrows), pl.ds((cb + k) * dh, dh)])

        @pl.when(i == 0)
        def _():
            kb_s[...] = k_ref[...].astype(bf16)
            vb_s[...] = v_ref[...].astype(bf16)
            dk_acc[...] = jnp.zeros_like(dk_acc)
            dv_acc[...] = jnp.zeros_like(dv_acc)

        s, gs = _silu_and_grad(z_ref[...])
        dy = dy_ref[...]
        st_b[...] = (dy * o_ref[...] * gs).astype(bf16)
        dob[...] = (dy * s).astype(bf16)
        qb[...] = q_ref[...].astype(bf16)
        p_ref[...] = jnp.zeros_like(p_ref)
        dq_acc[...] = jnp.zeros_like(dq_acc)
        nkb = (i + 1) * n_diag
        rows, cols, upper, lower = _sb_tiles(i)

        def make_step(masked):
            def step(kb, carry):
                off = pl.multiple_of(kb * TK, TK)
                hs = range(N_HEADS)
                sls = [slice(h * HEAD_DIM, (h + 1) * HEAD_DIM) for h in hs]
                qs = [qb[:, sls[h]] for h in hs]
                ks = [kb_s[pl.ds(off, TK), sls[h]] for h in hs]
                dos = [dob[:, sls[h]] for h in hs]
                sc = [_sb_scores(qs[h], ks[h], kb, rows, cols, masked) for h in hs]
                da = [_dot(dos[h], vb_s[pl.ds(off, TK), sls[h]], NT) for h in hs]
                suf = [_suffix_sum(sc[h][2], upper) for h in hs]
                onehot = cols == kb
                cs = [jnp.sum(jnp.where(onehot, car_ref[h], 0.0), axis=1, keepdims=True) for h in hs]
                es = [jnp.exp(sc[h][1] + suf[h] + cs[h]) for h in hs]
                if masked:
                    es = [jnp.where(sc[h][0], es[h], 0.0) for h in hs]
                gs_ = [da[h] * es[h] for h in hs]
                ps = [p_ref[h] for h in hs]
                pre = [_suffix_sum(gs_[h], lower) + ps[h] for h in hs]
                dzs = []
                for h in hs:
                    beta = jnp.exp(sc[h][1])
                    dzz = gs_[h] * (1.0 - beta) - beta * pre[h]
                    if masked:
                        dzz = jnp.where(sc[h][0], dzz, 0.0)
                    dzs.append((dzz * scale).astype(bf16))
                dqs = [_dot(dzs[h], ks[h]) for h in hs]
                dks = [_dot(dzs[h], qs[h], TN) for h in hs]
                dvs = [_dot(es[h].astype(bf16), dos[h], TN) for h in hs]
                for h in hs:
                    dq_acc[:, sls[h]] += dqs[h]
                    dk_acc[pl.ds(off, TK), sls[h]] += dks[h]
                    dv_acc[pl.ds(off, TK), sls[h]] += dvs[h]
                    p_ref[h] = ps[h] + jnp.sum(gs_[h], axis=1, keepdims=True)
                return carry
            return step

        lax.fori_loop(0, nkb - n_diag, make_step(False), 0)
        lax.fori_loop(nkb - n_diag, nkb, make_step(True), 0)
        st_a[...] = dq_acc[...].astype(bf16)
        row0 = pl.multiple_of(i * TQ, TQ)
        put(st_a, row0, TQ, 0)
        put(st_b, row0, TQ, 3)

        @pl.when(i == n_i - 1)
        def _():
            st_k[...] = dk_acc[...].astype(bf16)
            st_v[...] = dv_acc[...].astype(bf16)
            put(st_k, 0, S, 1)
            put(st_v, 0, S, 2)

    qspec = lambda k: pl.BlockSpec((TQ, dh), lambda i: (i, cb + k))
    kspec = lambda k: pl.BlockSpec((S, dh), lambda i: (0, cb + k))
    return pl.pallas_call(
        body, name="sb_bwd", grid=(n_i,),
        in_specs=[qspec(0), kspec(1), kspec(2), qspec(3),
                  pl.BlockSpec((TQ, dh), lambda i: (i, 0)),
                  pl.BlockSpec((N_HEADS, TQ, TK), lambda i: (0, i, 0)),
                  pl.BlockSpec((TQ, dh), lambda i: (i, A_GROUPS * HEAD_DIM // dh)),
                  pl.BlockSpec(memory_space=pl.ANY), pl.BlockSpec(memory_space=pl.ANY)],
        out_specs=pl.BlockSpec(memory_space=pl.ANY),
        out_shape=SDS(dproj_prev.shape, bf16),
        input_output_aliases={7: 0},
        scratch_shapes=[pltpu.VMEM((TQ, dh), bf16), pltpu.VMEM((S, dh), bf16), pltpu.VMEM((S, dh), bf16),
                        pltpu.VMEM((TQ, dh), bf16), pltpu.VMEM((N_HEADS, TQ, TK), f32), pltpu.VMEM((TQ, dh), f32),
                        pltpu.VMEM((S, dh), f32), pltpu.VMEM((S, dh), f32),
                        pltpu.VMEM((TQ, dh), bf16), pltpu.VMEM((TQ, dh), bf16),
                        pltpu.VMEM((S, dh), bf16), pltpu.VMEM((S, dh), bf16)],
        compiler_params=_params(56),
    )(proj, proj, proj, proj, o, car, dy, dproj_prev, after)


def _mem_kv(mem, mg_row, w_kv_g):
    M, D = mem.shape
    rb, n = w_kv_g.shape[1], w_kv_g.shape[2]

    def body(m_ref, g_ref, w_ref, kv_ref):
        mv = m_ref[...]
        r = lax.rsqrt(jnp.mean(mv * mv, axis=-1, keepdims=True) + EPS)
        mh = (mv * r * g_ref[...]).astype(bf16)
        kv_ref[...] = _dot(mh, w_ref[...].reshape(N_DEV * rb, n))

    return pl.pallas_call(
        body, name="mem_kv", grid=(1,),
        in_specs=[pl.BlockSpec((M, D), lambda i: (0, 0)), pl.BlockSpec((1, D), lambda i: (0, 0)),
                  pl.BlockSpec((N_DEV, rb, n), lambda i: (0, 0, 0))],
        out_specs=pl.BlockSpec((M, n), lambda i: (0, 0)),
        out_shape=SDS((M, n), f32), compiler_params=_params(),
    )(mem, mg_row, w_kv_g)


def _xattn_head(q_ref, kv_ref, qg, kg, h):
    dc = N_HEADS * HEAD_DIM
    sl = slice(h * HEAD_DIM, (h + 1) * HEAD_DIM)
    qh = q_ref[:, sl]
    rq = lax.rsqrt(jnp.mean(qh * qh, axis=-1, keepdims=True) + EPS)
    qhat = qh * rq
    qn = (qhat * qg).astype(bf16)
    kh = kv_ref[:, sl]
    rk = lax.rsqrt(jnp.mean(kh * kh, axis=-1, keepdims=True) + EPS)
    kn = (kh * rk * kg).astype(bf16)
    vh = kv_ref[:, dc + h * HEAD_DIM:dc + (h + 1) * HEAD_DIM].astype(bf16)
    s = _dot(qn, kn, NT) * (1.0 / math.sqrt(HEAD_DIM))
    e = jnp.exp(s - jnp.max(s, axis=-1, keepdims=True))
    p = e / jnp.sum(e, axis=-1, keepdims=True)
    o = _dot(p.astype(bf16), vh)
    return sl, rq, qhat, qn, kn, vh, p, o


def _xattn_fwd(proj, kv, qg_row, kg_row, y_prev, col0, tq=512):
    S = proj.shape[0]
    D = y_prev.shape[1]
    dc = N_HEADS * HEAD_DIM
    M = kv.shape[0]

    def body(q_ref, z_ref, kv_ref, qg_ref, kg_ref, yp_ref, y_ref):
        for h in range(N_HEADS):
            sl, _, _, _, _, _, _, o = _xattn_head(q_ref, kv_ref, qg_ref[...], kg_ref[...], h)
            z = z_ref[:, sl]
            y_ref[:, sl] = (o * (z * jax.nn.sigmoid(z))).astype(bf16)

    full = lambda shp: pl.BlockSpec(shp, lambda i: (0,) * len(shp))
    return pl.pallas_call(
        body, name="xattn_fwd", grid=(S // tq,),
        in_specs=[pl.BlockSpec((tq, dc), lambda i: (i, col0)), pl.BlockSpec((tq, dc), lambda i: (i, col0 + 1)),
                  full((M, 2 * dc)), full((1, HEAD_DIM)), full((1, HEAD_DIM)), pl.BlockSpec(memory_space=pl.ANY)],
        out_specs=pl.BlockSpec((tq, dc), lambda i: (i, D // dc - 1)),
        out_shape=SDS((S, D), bf16), input_output_aliases={5: 0}, compiler_params=_params(),
    )(proj, proj, kv, qg_row, kg_row, y_prev)


def _xattn_bwd(proj, kv, qg_row, kg_row, dy, dproj_prev, col0, tq=512):
    S = proj.shape[0]
    D = dy.shape[1]
    dc = N_HEADS * HEAD_DIM
    M = kv.shape[0]

    def body(q_ref, z_ref, kv_ref, qg_ref, kg_ref, dy_ref, dpp_ref, dp_ref, dkn_ref, dv_ref, dqg_ref):
        @pl.when(pl.program_id(0) == 0)
        def _():
            dkn_ref[...] = jnp.zeros_like(dkn_ref)
            dv_ref[...] = jnp.zeros_like(dv_ref)
            dqg_ref[...] = jnp.zeros_like(dqg_ref)

        qg = qg_ref[...]
        for h in range(N_HEADS):
            sl, rq, qhat, qn, kn, vh, p, o = _xattn_head(q_ref, kv_ref, qg, kg_ref[...], h)
            s, gs = _silu_and_grad(z_ref[:, sl])
            dyh = dy_ref[:, sl]
            dp_ref[:, dc + h * HEAD_DIM:dc + (h + 1) * HEAD_DIM] = (dyh * o * gs).astype(bf16)
            dob = (dyh * s).astype(bf16)
            dpr = _dot(dob, vh, NT)
            dv_ref[:, sl] += _dot(p.astype(bf16), dob, TN)
            ds = (p * (dpr - jnp.sum(p * dpr, axis=-1, keepdims=True)) * (1.0 / math.sqrt(HEAD_DIM))).astype(bf16)
            dqn = _dot(ds, kn)
            dkn_ref[:, sl] += _dot(ds, qn, TN)
            dqg_ref[...] += jnp.sum(dqn * qhat, axis=0, keepdims=True)
            dqhat = dqn * qg
            dp_ref[:, sl] = (rq * (dqhat - qhat * jnp.mean(dqhat * qhat, axis=-1, keepdims=True))).astype(bf16)

    full = lambda shp: pl.BlockSpec(shp, lambda i: (0,) * len(shp))
    return pl.pallas_call(
        body, name="xattn_bwd", grid=(S // tq,),
        in_specs=[pl.BlockSpec((tq, dc), lambda i: (i, col0)), pl.BlockSpec((tq, dc), lambda i: (i, col0 + 1)),
                  full((M, 2 * dc)), full((1, HEAD_DIM)), full((1, HEAD_DIM)),
                  pl.BlockSpec((tq, dc), lambda i: (i, D // dc - 1)), pl.BlockSpec(memory_space=pl.ANY)],
        out_specs=[pl.BlockSpec((tq, 2 * dc), lambda i: (i, col0 // 2)), full((M, dc)), full((M, dc)),
                   full((1, HEAD_DIM))],
        out_shape=[SDS(dproj_prev.shape, bf16), SDS((M, dc), f32), SDS((M, dc), f32), SDS((1, HEAD_DIM), f32)],
        input_output_aliases={6: 0}, compiler_params=_params(),
    )(proj, proj, kv, qg_row, kg_row, dy, dproj_prev)


def _mem_bwd(mem, mg_row, kv, dkn, dv, kg_row, w_kv_g):
    M, D = mem.shape
    rb, n = w_kv_g.shape[1], w_kv_g.shape[2]
    dc = n // 2

    def body(m_ref, g_ref, kv_ref, dkn_ref, dv_ref, kg_ref, w_ref, dw_ref, dmg_ref, dkg_ref, dkv_ref):
        mv = m_ref[...]
        r = lax.rsqrt(jnp.mean(mv * mv, axis=-1, keepdims=True) + EPS)
        mhat = mv * r
        mh = (mhat * g_ref[...]).astype(bf16)
        kg = kg_ref[...]
        dkg = jnp.zeros((1, HEAD_DIM), f32)
        for h in range(N_HEADS):
            sl = slice(h * HEAD_DIM, (h + 1) * HEAD_DIM)
            kh = kv_ref[:, sl]
            rk = lax.rsqrt(jnp.mean(kh * kh, axis=-1, keepdims=True) + EPS)
            khat = kh * rk
            dkn_h = dkn_ref[:, sl]
            dkg = dkg + jnp.sum(dkn_h * khat, axis=0, keepdims=True)
            dkhat = dkn_h * kg
            dkv_ref[:, sl] = (rk * (dkhat - khat * jnp.mean(dkhat * khat, axis=-1, keepdims=True))).astype(bf16)
        dkv_ref[:, dc:] = dv_ref[...].astype(bf16)
        dkg_ref[...] = dkg
        dkv = dkv_ref[...]
        dw_ref[...] = _dot(mh, dkv, TN).astype(bf16).reshape(N_DEV, rb, n)
        dmh = _dot(dkv, w_ref[...].reshape(N_DEV * rb, n), NT)
        dmg_ref[...] = jnp.sum(dmh * mhat, axis=0, keepdims=True)

    full = lambda shp: pl.BlockSpec(shp, lambda i: (0,) * len(shp))
    wspec = full((N_DEV, rb, n))
    return pl.pallas_call(
        body, name="mem_bwd", grid=(1,),
        in_specs=[full((M, D)), full((1, D)), full((M, n)), full((M, dc)), full((M, dc)), full((1, HEAD_DIM)), wspec],
        out_specs=[wspec, full((1, D)), full((1, HEAD_DIM))],
        out_shape=[SDS((N_DEV, rb, n), bf16), SDS((1, D), f32), SDS((1, HEAD_DIM), f32)],
        scratch_shapes=[pltpu.VMEM((M, n), bf16)], compiler_params=_params(),
    )(mem, mg_row, kv, dkn, dv, kg_row, w_kv_g)


SMALL = ("norm_g", "sgu_ln_g", "sgu_ln_b", "sgu_w", "sgu_b", "mem_norm_g", "q_norm_g", "k_norm_g")


def _small_rows(like):
    rows = [math.prod(like[n].shape) // 128 for n in SMALL]
    offs = [0]
    for r in rows:
        offs.append(offs[-1] + -(-r // 8) * 8)
    return rows, offs


def _pack_small(parts, offs):
    pieces = []
    for k, n in enumerate(SMALL):
        a = parts[n].reshape(-1, 128)
        pieces.append(jnp.pad(a, ((0, offs[k + 1] - offs[k] - a.shape[0]), (0, 0))))
    return jnp.concatenate(pieces)


def kernel(x, mem, norm_g, w_in, sgu_ln_g, sgu_ln_b, sgu_w, sgu_b, mem_norm_g, w_mem_kv, q_norm_g, k_norm_g, w_out, loss_target, m_norm_g, m_w_in, m_sgu_ln_g, m_sgu_ln_b, m_sgu_w, m_sgu_b, m_mem_norm_g, m_w_mem_kv, m_q_norm_g, m_k_norm_g, m_w_out, v_norm_g, v_w_in, v_sgu_ln_g, v_sgu_ln_b, v_sgu_w, v_sgu_b, v_mem_norm_g, v_w_mem_kv, v_q_norm_g, v_k_norm_g, v_w_out):
    L, D, wc = w_in.shape
    S = x.shape[1]
    da = D // 2
    xs = x.reshape(S, D)
    mems = mem.reshape(mem.shape[1], D)
    tgt = loss_target.reshape(S, D)
    row = lambda a, l: a[l].reshape(1, -1)
    tie = lambda a, tok: a + tok[0:1, 0:1]
    sb_col, xa_col = 3 * da // HEAD_DIM, (3 * da + D) // (D // 4)

    ax, ay, ac = lax.axis_index("x"), lax.axis_index("y"), lax.axis_index("c")
    ids = jnp.stack([4 * ax + 2 * ay + ac, 2 * ax + ay, ac]).astype(jnp.int32)
    w_in0_b = _cast_into_slot("cast_w_in", w_in, 0, 512, ids, ids)
    first = _gather2_start("gather_w_in0", [w_in0_b], ids)
    late = first["token"]
    w_b = [(w_in0_b if l == 0 else _cast_into_slot("cast_w_in", w_in, l, 512, ids, late),
            _cast_into_slot("cast_w_kv", w_mem_kv, l, 256, ids, late),
            _cast_into_slot("cast_w_out", w_out, l, 256, ids, late)) for l in range(L)]
    in_fwd = _gather2_forward("gather_w_in0_forward",
                              _split_wait(first, *[a for wl in w_b for a in wl if a is not w_in0_b]), ids)

    acts = []
    xl = xs
    for l in range(L):
        g_row = row(norm_g, l)
        (w_in_g,) = _split_wait(in_fwd, g_row if l else in_fwd["token"])
        rest = _gather2_start(f"gather_w_rest{l}", [w_b[l][1], w_b[l][2]], w_in_g)
        proj, h = _rms_proj(xl, tie(g_row, rest["token"]), w_in_g)
        rest_fwd = _gather2_forward(f"gather_w_rest{l}_forward", _split_wait(rest, proj), proj)
        lg_row = tie(row(sgu_ln_g, l), rest_fwd["token"])
        if l + 1 < L:
            nxt = _gather2_start(f"gather_w_in{l + 1}", [w_b[l + 1][0]], proj)
            lg_row = tie(lg_row, nxt["token"])
        y = _sgu_fwd(proj, lg_row, row(sgu_ln_b, l), sgu_w[l], sgu_b[l].T)
        y, o_b, car = _sb_fwd(proj, y, sb_col, lg_row)
        w_kv_g, w_out_g = _split_wait(rest_fwd, o_b)
        kv = _mem_kv(mems, row(mem_norm_g, l), w_kv_g)
        y = _xattn_fwd(proj, kv, row(q_norm_g, l), row(k_norm_g, l), y, xa_col)
        order = kv
        if l + 1 < L:
            in_fwd = _gather2_forward(f"gather_w_in{l + 1}_forward", _split_wait(nxt, y), y)
            order = in_fwd["token"]
        x_next = _out_proj(xl, y, w_out_g, order)
        acts.append((xl, proj, h, y, o_b, car, kv, w_in_g, w_kv_g, w_out_g))
        xl = x_next

    dx, dxb, loss_part = _loss_and_grad(xl, tgt, 512)
    loss = lax.psum(loss_part[0, 0], ("x", "y", "c"))

    weights = dict(norm_g=norm_g, sgu_ln_g=sgu_ln_g, sgu_ln_b=sgu_ln_b, sgu_w=sgu_w, sgu_b=sgu_b,
                   mem_norm_g=mem_norm_g, q_norm_g=q_norm_g, k_norm_g=k_norm_g)
    moms_m = dict(norm_g=m_norm_g, sgu_ln_g=m_sgu_ln_g, sgu_ln_b=m_sgu_ln_b, sgu_w=m_sgu_w, sgu_b=m_sgu_b,
                  mem_norm_g=m_mem_norm_g, q_norm_g=m_q_norm_g, k_norm_g=m_k_norm_g)
    moms_v = dict(norm_g=v_norm_g, sgu_ln_g=v_sgu_ln_g, sgu_ln_b=v_sgu_ln_b, sgu_w=v_sgu_w, sgu_b=v_sgu_b,
                  mem_norm_g=v_mem_norm_g, q_norm_g=v_q_norm_g, k_norm_g=v_k_norm_g)
    small_rows, small_offs = _small_rows(weights)
    head_rows = D // 128
    assert SMALL[0] == "norm_g" and head_rows % 8 == 0

    seconds = {}
    pending = None
    adam = {"w_out": None, "w_mem_kv": None, "w_in": None}

    def update(lu, order):
        (r_out,) = _scatter2_finish(seconds[f"g_out{lu}"], order)
        adam["w_out"] = _sum_adam("adam_w_out", r_out, w_out, m_w_out, v_w_out, lu, adam["w_out"], 128, ids)
        r_kv, r_in = _scatter2_finish(seconds[f"g_rest{lu}"], adam["w_out"][0])
        adam["w_mem_kv"] = _sum_adam("adam_w_kv", r_kv, w_mem_kv, m_w_mem_kv, v_w_mem_kv, lu, adam["w_mem_kv"], 256, ids)
        adam["w_in"] = _sum_adam("adam_w_in", r_in, w_in, m_w_in, v_w_in, lu, adam["w_in"], 256, ids)
        return adam["w_in"][0]
    small = {n: [None] * L for n in SMALL}
    for l in reversed(range(L)):
        xl, proj, h, y, o_b, car, kv, w_in_g, w_kv_g, w_out_g = acts[l]
        dy = _out_bwd_dy(dxb, w_out_g)
        order = dy
        if pending is not None:
            seconds[pending[0]] = _scatter2_second_level(pending[0], pending[1], dy, ids)
            order = seconds[pending[0]]["token"]
        g_out = _tn_grad("out_bwd_dw", y, dxb, 512, 512, True, order)
        first = _scatter2_pair_start(f"scatter_g_out{l}_pair", [g_out], ids)
        dproj, d_sw, d_sb, d_lg, d_lb = _sgu_bwd(proj, dy, tie(row(sgu_ln_g, l), first["token"]), row(sgu_ln_b, l),
                                                 sgu_w[l], sgu_b[l].T)
        seconds[f"g_out{l}"] = _scatter2_second_level(f"g_out{l}", first, dproj, ids)
        dproj = _sb_bwd(proj, o_b, car, dy, dproj, sb_col, seconds[f"g_out{l}"]["token"])
        dproj, dkn, dv, d_qg = _xattn_bwd(proj, kv, row(q_norm_g, l), row(k_norm_g, l), dy, dproj, xa_col)
        g_kv, d_mg, d_kg = _mem_bwd(mems, row(mem_norm_g, l), kv, dkn, dv, row(k_norm_g, l), w_kv_g)
        for n, val in (("sgu_ln_g", d_lg), ("sgu_ln_b", d_lb), ("sgu_w", d_sw), ("sgu_b", d_sb[:, :A_GROUPS].T),
                       ("mem_norm_g", d_mg), ("q_norm_g", d_qg), ("k_norm_g", d_kg)):
            small[n][l] = val.reshape(-1)
        order = d_kg
        if l == 0:
            small["norm_g"][0] = jnp.zeros_like(small["norm_g"][1])
            part = _pack_small({n: jnp.stack(small[n]) for n in SMALL}, small_offs)
            tail = _gather2_start("gather_small_tail", [_into_slot("small_tail_slot", part[head_rows:], ids)], ids)
            order = tail["token"]
        g_in_l = _tn_grad("in_bwd_dw", h, dproj, 1024, wc, False, order)
        first = _scatter2_pair_start(f"scatter_g_rest{l}_pair", [g_kv, g_in_l], ids)
        g_row = tie(row(norm_g, l), first["token"])
        pending = (f"g_rest{l}", first)
        if l == 0:
            tail_fwd = _gather2_forward("gather_small_tail_forward", _split_wait(tail, g_in_l), g_in_l)
            order = tail_fwd["token"]
            for lu in reversed(range(1, L)):
                order = update(lu, order)
            seconds[pending[0]] = _scatter2_second_level(pending[0], pending[1], order, ids)
            g_row = tie(g_row, seconds[pending[0]]["token"])
        dh = _in_bwd_dh(dproj, w_in_g, g_row)
        dx, dxb, d_ng = _rms_bwd(dh, xl, g_row, dx)
        small["norm_g"][l] = d_ng.reshape(-1)
    head = _gather2_start("gather_small_head",
                          [_into_slot("small_head_slot", small["norm_g"][0].reshape(head_rows, 128), ids)], ids)

    (r_out,) = _scatter2_finish(seconds["g_out0"], head["token"])
    adam["w_out"] = _sum_adam("adam_w_out", r_out, w_out, m_w_out, v_w_out, 0, adam["w_out"], 128, ids)
    head_fwd = _gather2_forward("gather_small_head_forward", _split_wait(head, adam["w_out"][0]), ids)
    (r_tail,) = _split_wait(tail_fwd, adam["w_out"][0])
    (r_head,) = _split_wait(head_fwd, r_tail)
    as128 = lambda d: [d[n].reshape(-1, 128) for n in SMALL]
    sm = _small_sum_adam(r_head, r_tail, as128(weights), as128(moms_m), as128(moms_v), small_offs)
    r_kv, r_in = _scatter2_finish(seconds["g_rest0"], sm[0])
    adam["w_mem_kv"] = _sum_adam("adam_w_kv", r_kv, w_mem_kv, m_w_mem_kv, v_w_mem_kv, 0, adam["w_mem_kv"], 256, ids)
    adam["w_in"] = _sum_adam("adam_w_in", r_in, w_in, m_w_in, v_w_in, 0, adam["w_in"], 256, ids)
    res = dict(adam)
    for p, n in enumerate(SMALL):
        res[n] = [sm[k * len(SMALL) + p].reshape(weights[n].shape) for k in range(4)]

    order = ("norm_g", "w_in", "sgu_ln_g", "sgu_ln_b", "sgu_w", "sgu_b", "mem_norm_g", "w_mem_kv", "q_norm_g",
             "k_norm_g", "w_out")
    outs = [loss, dx.reshape(x.shape)]
    for k in range(4):
        outs += [res[n][k] for n in order]
    return tuple(outs)
```

```python
import functools
import math

import jax
import jax.numpy as jnp
from jax import lax
from jax.experimental import pallas as pl
from jax.experimental.pallas import tpu as pltpu

f32 = jnp.float32
bf16 = jnp.bfloat16
SDS = jax.ShapeDtypeStruct

N_DEV = 8
EPS = 1e-6
CHUNK = 128
A_GROUPS = 8
HEAD_DIM = 128
N_HEADS = 4
TQ = 256
TK = 256
CARRY_LANES = 128
ADAM_LR, ADAM_B1, ADAM_B2, ADAM_EPS, ADAM_WD, ADAM_STEP = 0.001, 0.9, 0.999, 1e-08, 0.01, 10
MIB = 1024 * 1024

NT = (((1,), (1,)), ((), ()))
TN = (((0,), (0,)), ((), ()))


def _params(vmem_mib=48):
    return pltpu.CompilerParams(vmem_limit_bytes=vmem_mib * MIB)


def _gelu_and_grad(x):
    e = lax.erf(x * (1.0 / math.sqrt(2.0)))
    cdf = 0.5 * (1.0 + e)
    pdf = jnp.exp(-0.5 * x * x) * (1.0 / math.sqrt(2.0 * math.pi))
    return x * cdf, cdf + x * pdf


def _gelu(x):
    return 0.5 * x * (1.0 + lax.erf(x * (1.0 / math.sqrt(2.0))))


def _silu_and_grad(z):
    sg = jax.nn.sigmoid(z)
    return z * sg, sg * (1.0 + z * (1.0 - sg))


def _dot(a, b, dims=None):
    if dims is None:
        return jnp.dot(a, b, preferred_element_type=f32)
    return lax.dot_general(a, b, dims, preferred_element_type=f32)


_HBM = pl.BlockSpec(memory_space=pltpu.HBM)
_SEM = pl.BlockSpec(memory_space=pltpu.SEMAPHORE)
_EFFECT = pltpu.SideEffectType.DATAFLOW_SIDE_EFFECTING


def _split_start(name, bufs, n_remote, n_local, build, after):
    nb = len(bufs)

    def body(*refs):
        token = refs[-1]
        locals_, remotes = build(refs[:nb], *refs[nb + 1:nb + 4])
        for cp in locals_ + remotes:
            cp.start()
        token[...] = jnp.zeros_like(token)

    hbm = lambda a: pltpu.with_memory_space_constraint(a, pltpu.HBM)
    outs = pl.pallas_call(
        body, name=name,
        out_shape=(pltpu.SemaphoreType.DMA((n_remote,)), pltpu.SemaphoreType.DMA((n_remote,)),
                   pltpu.SemaphoreType.DMA((max(n_local, 1),)),
                   *[pltpu.HBM(b.shape, b.dtype) for b in bufs], SDS((8, 128), f32)),
        in_specs=[_HBM] * nb + [pl.BlockSpec(memory_space=pl.ANY)],
        out_specs=(_SEM, _SEM, _SEM, *[_HBM] * nb, pl.BlockSpec(memory_space=pltpu.VMEM)),
        input_output_aliases={k: 3 + k for k in range(nb)},
        compiler_params=pltpu.CompilerParams(has_side_effects=_EFFECT),
    )(*[hbm(b) for b in bufs], after)
    return dict(name=name, build=build, sems=outs[:3], bufs=outs[3:3 + nb], token=outs[-1])


def _split_wait(handle, *after):
    build, bufs = handle["build"], handle["bufs"]
    nb = len(bufs)

    def body(*refs):
        locals_, remotes = build(refs[:nb], *refs[nb:nb + 3])
        for cp in remotes:
            cp.wait_recv()
        for cp in remotes:
            cp.wait_send()
        for cp in locals_:
            cp.wait()

    outs = pl.pallas_call(
        body, name=handle["name"] + "_wait",
        out_shape=tuple(pltpu.HBM(b.shape, b.dtype) for b in bufs),
        in_specs=[_HBM] * nb + [_SEM] * 3 + [pl.BlockSpec(memory_space=pl.ANY)] * len(after),
        out_specs=tuple([_HBM] * nb),
        input_output_aliases={k: k for k in range(nb)},
        compiler_params=pltpu.CompilerParams(has_side_effects=_EFFECT),
    )(*bufs, *handle["sems"], *after)
    return list(outs)


def _remote(src, dst, send_sems, recv_sems, k, to):
    return pltpu.make_async_remote_copy(src_ref=src, dst_ref=dst, send_sem=send_sems.at[k], recv_sem=recv_sems.at[k],
                                        device_id=to, device_id_type=pl.DeviceIdType.MESH)


def _other_chips(x, y):
    return [(1 - x, y), (x, 1 - y), (1 - x, 1 - y)]


def _gather2_start(name, lands, after):
    def build(refs, send, recv, loc):
        x, y, c = lax.axis_index("x"), lax.axis_index("y"), lax.axis_index("c")
        me = 4 * x + 2 * y + c
        remotes = []
        for a, d in enumerate(refs):
            remotes.append(_remote(d.at[me], d.at[me], send, recv, 4 * a, (x, y, 1 - c)))
            remotes += [_remote(d.at[me], d.at[me], send, recv, 4 * a + 1 + k, (px, py, c))
                        for k, (px, py) in enumerate(_other_chips(x, y))]
        return [], remotes

    return _split_start(name, list(lands), 4 * len(lands), 0, build, after)


def _gather2_forward(name, lands, after):
    n = len(lands)

    def build(refs, send, recv, loc):
        x, y, c = lax.axis_index("x"), lax.axis_index("y"), lax.axis_index("c")
        slots = [4 * px + 2 * py + c for px, py in _other_chips(x, y)]
        return [], [_remote(d.at[sl], d.at[sl], send, recv, 3 * a + k, (x, y, 1 - c))
                    for a, d in enumerate(refs) for k, sl in enumerate(slots)]

    return _split_start(name, list(lands), 3 * n, 0, build, after)


def _scatter2_pair_start(name, srcs, after):
    n = len(srcs)

    def build(refs, send, recv, loc):
        x, y, c = lax.axis_index("x"), lax.axis_index("y"), lax.axis_index("c")
        return [], [_remote(refs[a].at[2 * q + 1 - c], refs[n + a].at[q], send, recv, 4 * a + q, (x, y, 1 - c))
                    for a in range(n) for q in range(4)]

    lands = [lax.empty((4,) + s.shape[1:], s.dtype) for s in srcs]
    return _split_start(name, list(srcs) + lands, 4 * n, 0, build, after)


def _scatter2_chip_start(name, pairs, after):
    n = len(pairs)

    def build(refs, send, recv, loc):
        x, y, c = lax.axis_index("x"), lax.axis_index("y"), lax.axis_index("c")
        return [], [_remote(refs[a].at[2 * px + py], refs[n + a].at[2 * x + y], send, recv, 3 * a + k, (px, py, c))
                    for a in range(n) for k, (px, py) in enumerate(_other_chips(x, y))]

    return _split_start(name, list(pairs) + [lax.empty(p.shape, p.dtype) for p in pairs], 3 * n, 0, build, after)


def _pair_sum(name, src, theirs, ids):
    _, R, C = theirs.shape
    tr = min(R, 1024)

    def body(ids_ref, a_ref, b_ref, o_ref):
        o_ref[...] = (a_ref[...].astype(f32) + b_ref[...].astype(f32)).astype(bf16)

    spec = pl.BlockSpec((None, tr, C), lambda q, i, ids: (q, i, 0))
    return pl.pallas_call(
        body, name=name,
        grid_spec=pltpu.PrefetchScalarGridSpec(
            num_scalar_prefetch=1, grid=(4, R // tr),
            in_specs=[pl.BlockSpec((None, tr, C), lambda q, i, ids: (2 * q + ids[2], i, 0)), spec], out_specs=spec),
        out_shape=SDS(theirs.shape, bf16), compiler_params=_params(),
    )(ids, src, theirs)


def _scatter2_second_level(name, first, after, ids):
    outs = _split_wait(first, after)
    n = len(outs) // 2
    pairs = [_pair_sum(f"pair_sum_{name}{a}", outs[a], outs[n + a], ids) for a in range(n)]
    return _scatter2_chip_start(f"scatter_{name}_chip", pairs, ids)


def _scatter2_finish(second, after):
    outs = _split_wait(second, after)
    n = len(outs) // 2
    return [(outs[a], outs[n + a]) for a in range(n)]


def _cast_into_slot(name, w, l, tr, ids, after):
    _, R, C = w.shape

    def body(ids_ref, w_ref, after_ref, o_ref):
        o_ref[...] = w_ref[...].astype(bf16)

    return pl.pallas_call(
        body, name=name,
        grid_spec=pltpu.PrefetchScalarGridSpec(
            num_scalar_prefetch=1, grid=(R // tr,),
            in_specs=[pl.BlockSpec((None, tr, C), lambda i, ids: (l, i, 0)), pl.BlockSpec(memory_space=pl.ANY)],
            out_specs=pl.BlockSpec((None, tr, C), lambda i, ids: (ids[0], i, 0))),
        out_shape=SDS((N_DEV, R, C), bf16), compiler_params=_params(),
    )(ids, w, after)


def _into_slot(name, a, ids):
    R, C = a.shape

    def body(ids_ref, a_ref, o_ref):
        o_ref[...] = a_ref[...]

    return pl.pallas_call(
        body, name=name,
        grid_spec=pltpu.PrefetchScalarGridSpec(
            num_scalar_prefetch=1, grid=(1,),
            in_specs=[pl.BlockSpec((R, C), lambda i, ids: (0, 0))],
            out_specs=pl.BlockSpec((None, R, C), lambda i, ids: (ids[0], 0, 0))),
        out_shape=SDS((N_DEV, R, C), f32), compiler_params=_params(),
    )(ids, a)


def _adam_math(w, g, m, v):
    m2 = ADAM_B1 * m + (1.0 - ADAM_B1) * g
    v2 = ADAM_B2 * v + (1.0 - ADAM_B2) * (g * g)
    m_hat = m2 / (1.0 - ADAM_B1 ** ADAM_STEP)
    v_hat = v2 / (1.0 - ADAM_B2 ** ADAM_STEP)
    delta = -ADAM_LR * (m_hat / (jnp.sqrt(v_hat) + ADAM_EPS) + ADAM_WD * w)
    return delta, m2, v2


def _sum_adam(name, pair_recv, w, m, v, l, prev, tr, ids):
    own, recv = pair_recv
    L, R, C = w.shape
    slots = recv.shape[0]

    def body(ids_ref, r_ref, own_ref, w_ref, m_ref, v_ref, *rest):
        g_ref, d_ref, m2_ref, v2_ref = rest[-4:]
        terms = [jnp.where(ids_ref[1] == q, own_ref[...], r_ref[q]).astype(f32) for q in range(slots)]
        g = terms[0]
        for t in terms[1:]:
            g = g + t
        d, m2, v2 = _adam_math(w_ref[...], g, m_ref[...], v_ref[...])
        g_ref[...] = g
        d_ref[...] = d
        m2_ref[...] = m2
        v2_ref[...] = v2

    wspec = pl.BlockSpec((None, tr, C), lambda i, ids: (l, i, 0))
    in_specs = [pl.BlockSpec((slots, tr, C), lambda i, ids: (0, i, 0)),
                pl.BlockSpec((None, tr, C), lambda i, ids: (ids[1], i, 0)), wspec, wspec, wspec]
    args = [ids, recv, own, w, m, v]
    aliases = {}
    if prev is not None:
        in_specs += [pl.BlockSpec(memory_space=pl.ANY)] * 4
        args += list(prev)
        aliases = {6 + k: k for k in range(4)}
    return pl.pallas_call(
        body, name=name,
        grid_spec=pltpu.PrefetchScalarGridSpec(num_scalar_prefetch=1, grid=(R // tr,), in_specs=in_specs,
                                               out_specs=[wspec] * 4),
        out_shape=[SDS((L, R, C), f32)] * 4, input_output_aliases=aliases, compiler_params=_params(),
    )(*args)


def _small_sum_adam(recv_head, recv_tail, ws, ms, vs, offs):
    n = len(ws)
    r0 = recv_head.shape[1]

    def body(*refs):
        rh, rt = refs[0], refs[1]
        w_refs, m_refs, v_refs = refs[2:2 + n], refs[2 + n:2 + 2 * n], refs[2 + 2 * n:2 + 3 * n]
        outs = refs[2 + 3 * n:]
        for p in range(n):
            lo, hi = offs[p], offs[p] + ws[p].shape[0]
            pieces = []
            if lo < r0:
                pieces.append((rh, lo, 0, min(hi, r0) - lo))
            if hi > r0:
                pieces.append((rt, max(lo, r0) - r0, max(lo, r0) - lo, hi - max(lo, r0)))
            for src, a, b, cnt in pieces:
                g = src[0, a:a + cnt, :]
                for s in range(1, N_DEV):
                    g = g + src[s, a:a + cnt, :]
                d, m2, v2 = _adam_math(w_refs[p][b:b + cnt, :], g, m_refs[p][b:b + cnt, :], v_refs[p][b:b + cnt, :])
                for k, val in enumerate((g, d, m2, v2)):
                    outs[k * n + p][b:b + cnt, :] = val

    return pl.pallas_call(
        body, name="small_sum_adam", out_shape=[SDS(w.shape, f32) for w in ws] * 4, compiler_params=_params(),
    )(recv_head, recv_tail, *ws, *ms, *vs)


def _loss_and_grad(xf, tgt, tm):
    S, D = xf.shape

    def body(x_ref, t_ref, dx_ref, dxb_ref, l_ref):
        i = pl.program_id(0)
        d = x_ref[...] - t_ref[...]
        dx = d * (1.0 / D)
        dx_ref[...] = dx
        dxb_ref[...] = dx.astype(bf16)
        e = d * d
        part = e[:, 0:128]
        for k in range(1, D // 128):
            part = part + e[:, k * 128:(k + 1) * 128]
        part = jnp.sum(part.reshape(tm // 8, 8, 128), axis=0)

        @pl.when(i == 0)
        def _():
            l_ref[...] = jnp.zeros_like(l_ref)

        l_ref[...] += part

        @pl.when(i == pl.num_programs(0) - 1)
        def _():
            tot = jnp.sum(l_ref[...], axis=1, keepdims=True)
            tot = jnp.sum(tot, axis=0, keepdims=True)
            l_ref[...] = jnp.broadcast_to(tot * (0.5 / D), l_ref.shape)

    row = pl.BlockSpec((tm, D), lambda i: (i, 0))
    return pl.pallas_call(
        body, name="loss_grad", grid=(S // tm,),
        in_specs=[row, row], out_specs=[row, row, pl.BlockSpec((8, 128), lambda i: (0, 0))],
        out_shape=[SDS((S, D), f32), SDS((S, D), bf16), SDS((8, 128), f32)], compiler_params=_params(),
    )(xf, tgt)


def _rms_proj(x, g_row, w_in_g, tm=1024):
    S, D = x.shape
    wc = w_in_g.shape[2]
    n_out = N_DEV * wc

    def body(x_ref, g_ref, w_ref, proj_ref, h_ref):
        @pl.when(pl.program_id(1) == 0)
        def _():
            xv = x_ref[...]
            r = lax.rsqrt(jnp.mean(xv * xv, axis=-1, keepdims=True) + EPS)
            h_ref[...] = (xv * r * g_ref[...]).astype(bf16)

        proj_ref[...] = _dot(h_ref[...], w_ref[...])

    return pl.pallas_call(
        body, name="rms_proj", grid=(S // tm, N_DEV),
        in_specs=[pl.BlockSpec((tm, D), lambda i, j: (i, 0)),
                  pl.BlockSpec((1, D), lambda i, j: (0, 0)),
                  pl.BlockSpec((None, D, wc), lambda i, j: (j, 0, 0))],
        out_specs=[pl.BlockSpec((tm, wc), lambda i, j: (i, j)), pl.BlockSpec((tm, D), lambda i, j: (i, 0))],
        out_shape=[SDS((S, n_out), f32), SDS((S, D), bf16)], compiler_params=_params(),
    )(x, g_row, w_in_g)


def _out_proj(x, y, w_out_g, after, tm=512):
    S, D = x.shape
    rb = w_out_g.shape[1]

    def body(x_ref, y_ref, w_ref, after_ref, o_ref):
        w = w_ref[...].reshape(N_DEV * rb, D)
        o_ref[...] = x_ref[...] + _dot(y_ref[...], w)

    row = pl.BlockSpec((tm, D), lambda i: (i, 0))
    return pl.pallas_call(
        body, name="out_proj", grid=(S // tm,),
        in_specs=[row, row, pl.BlockSpec((N_DEV, rb, D), lambda i: (0, 0, 0)), pl.BlockSpec(memory_space=pl.ANY)],
        out_specs=row, out_shape=SDS((S, D), f32), compiler_params=_params(),
    )(x, y, w_out_g, after)


def _out_bwd_dy(dxb, w_out_g, tm=512):
    S, D = dxb.shape
    rb = w_out_g.shape[1]

    nb = 2

    def body(dx_ref, w_ref, o_ref):
        o_ref[...] = _dot(dx_ref[...], w_ref[...].reshape(nb * rb, D), NT)

    return pl.pallas_call(
        body, name="out_bwd_dy", grid=(S // tm, N_DEV // nb),
        in_specs=[pl.BlockSpec((tm, D), lambda i, j: (i, 0)),
                  pl.BlockSpec((nb, rb, D), lambda i, j: (j, 0, 0))],
        out_specs=pl.BlockSpec((tm, nb * rb), lambda i, j: (i, j)),
        out_shape=SDS((S, D), f32), compiler_params=_params(),
    )(dxb, w_out_g)


def _tn_grad(name, a, b, tm, tn, rows_major, after):
    S, M = a.shape
    N = b.shape[1]
    if rows_major:
        rb = M // N_DEV
        nb = tm // rb
        out_shape = SDS((N_DEV, rb, N), bf16)
        out_spec = pl.BlockSpec((nb, rb, tn), lambda i, j: (i, 0, j))
    else:
        out_shape = SDS((N_DEV, M, N // N_DEV), bf16)
        assert tn == N // N_DEV
        out_spec = pl.BlockSpec((None, tm, tn), lambda i, j: (j, i, 0))

    def body(a_ref, b_ref, after_ref, o_ref):
        o_ref[...] = _dot(a_ref[...], b_ref[...], TN).astype(bf16).reshape(o_ref.shape)

    return pl.pallas_call(
        body, name=name, grid=(M // tm, N // tn),
        in_specs=[pl.BlockSpec((S, tm), lambda i, j: (0, i)), pl.BlockSpec((S, tn), lambda i, j: (0, j)),
                  pl.BlockSpec(memory_space=pl.ANY)],
        out_specs=out_spec, out_shape=out_shape, compiler_params=_params(),
    )(a, b, after)


def _in_bwd_dh(dproj, w_in_g, after, tm=1024, tn=256):
    S = dproj.shape[0]
    _, D, wc = w_in_g.shape
    tm = min(tm, S)

    def body(dp_ref, w_ref, after_ref, o_ref):
        acc = _dot(dp_ref[:, 0:wc], w_ref[0], NT)
        for k in range(1, N_DEV):
            acc = acc + _dot(dp_ref[:, k * wc:(k + 1) * wc], w_ref[k], NT)
        o_ref[...] = acc

    return pl.pallas_call(
        body, name="in_bwd_dh", grid=(S // tm, D // tn),
        in_specs=[pl.BlockSpec((tm, N_DEV * wc), lambda i, j: (i, 0)),
                  pl.BlockSpec((N_DEV, tn, wc), lambda i, j: (0, j, 0)), pl.BlockSpec(memory_space=pl.ANY)],
        out_specs=pl.BlockSpec((tm, tn), lambda i, j: (i, j)),
        out_shape=SDS((S, D), f32), compiler_params=_params(),
    )(dproj, w_in_g, after)


def _rms_bwd(dh, x, g_row, dx_next, tm=256):
    S, D = x.shape

    def body(dh_ref, x_ref, g_ref, dxn_ref, dx_ref, dxb_ref, dg_ref):
        @pl.when(pl.program_id(0) == 0)
        def _():
            dg_ref[...] = jnp.zeros_like(dg_ref)

        dh = dh_ref[...]
        xv = x_ref[...]
        r = lax.rsqrt(jnp.mean(xv * xv, axis=-1, keepdims=True) + EPS)
        xhat = xv * r
        dxhat = dh * g_ref[...]
        dx = r * (dxhat - xhat * jnp.mean(dxhat * xhat, axis=-1, keepdims=True)) + dxn_ref[...]
        dx_ref[...] = dx
        dxb_ref[...] = dx.astype(bf16)
        dg_ref[...] += jnp.sum(dh * xhat, axis=0, keepdims=True)

    row = pl.BlockSpec((tm, D), lambda i: (i, 0))
    vec = pl.BlockSpec((1, D), lambda i: (0, 0))
    return pl.pallas_call(
        body, name="rms_bwd", grid=(S // tm,), in_specs=[row, row, vec, row], out_specs=[row, row, vec],
        out_shape=[SDS((S, D), f32), SDS((S, D), bf16), SDS((1, D), f32)], compiler_params=_params(),
    )(dh, x, g_row, dx_next)


def _sgu_fwd(proj, ln_g, ln_b, w_s, b_t):
    S = proj.shape[0]
    da = A_GROUPS * HEAD_DIM
    D = 2 * da

    def body(u_ref, v_ref, z_ref, lg_ref, lb_ref, w_ref, bt_ref, y_ref):
        u = _gelu(u_ref[...])
        v = _gelu(v_ref[...])
        z = z_ref[...]
        mu = jnp.mean(v, axis=-1, keepdims=True)
        xc = v - mu
        rs = lax.rsqrt(jnp.mean(xc * xc, axis=-1, keepdims=True) + EPS)
        vn = (xc * rs * lg_ref[...] + lb_ref[...]).astype(bf16)
        gate = u * (z * jax.nn.sigmoid(z))
        tri = lax.broadcasted_iota(jnp.int32, (CHUNK, CHUNK), 0) >= lax.broadcasted_iota(jnp.int32, (CHUNK, CHUNK), 1)
        for g in range(A_GROUPS):
            sl = slice(g * HEAD_DIM, (g + 1) * HEAD_DIM)
            wm = jnp.where(tri, w_ref[g], 0.0).astype(bf16)
            mixed = _dot(wm, vn[:, sl]) + bt_ref[:, g:g + 1]
            y_ref[:, sl] = (gate[:, sl] * mixed).astype(bf16)

    blk = lambda cb: pl.BlockSpec((CHUNK, da), lambda c: (c, cb))
    full = lambda shp: pl.BlockSpec(shp, lambda c: (0,) * len(shp))
    return pl.pallas_call(
        body, name="sgu_fwd", grid=(S // CHUNK,),
        in_specs=[blk(0), blk(1), blk(2), full((1, da)), full((1, da)),
                  full((A_GROUPS, CHUNK, CHUNK)), full((CHUNK, A_GROUPS))],
        out_specs=blk(0), out_shape=SDS((S, D), bf16), compiler_params=_params(),
    )(proj, proj, proj, ln_g, ln_b, w_s, b_t)


def _sgu_bwd(proj, dy, ln_g, ln_b, w_s, b_t):
    S = proj.shape[0]
    da = A_GROUPS * HEAD_DIM
    n_proj = proj.shape[1]

    def body(u_ref, v_ref, z_ref, dy_ref, lg_ref, lb_ref, w_ref, bt_ref,
             dp_ref, dw_ref, db_ref, dlg_ref, dlb_ref, dvn_ref):
        @pl.when(pl.program_id(0) == 0)
        def _():
            dw_ref[...] = jnp.zeros_like(dw_ref)
            db_ref[...] = jnp.zeros_like(db_ref)
            dlg_ref[...] = jnp.zeros_like(dlg_ref)
            dlb_ref[...] = jnp.zeros_like(dlb_ref)

        up, vp, z, dy = u_ref[...], v_ref[...], z_ref[...], dy_ref[...]
        u, gu = _gelu_and_grad(up)
        v, gv = _gelu_and_grad(vp)
        s, gs = _silu_and_grad(z)
        mu = jnp.mean(v, axis=-1, keepdims=True)
        xc = v - mu
        rs = lax.rsqrt(jnp.mean(xc * xc, axis=-1, keepdims=True) + EPS)
        vhat = xc * rs
        lg = lg_ref[...]
        vn = (vhat * lg + lb_ref[...]).astype(bf16)
        tri = lax.broadcasted_iota(jnp.int32, (CHUNK, CHUNK), 0) >= lax.broadcasted_iota(jnp.int32, (CHUNK, CHUNK), 1)
        lane = lax.broadcasted_iota(jnp.int32, (CHUNK, HEAD_DIM), 1)
        dys = dy * s
        db = jnp.zeros((CHUNK, HEAD_DIM), f32)
        for g in range(A_GROUPS):
            sl = slice(g * HEAD_DIM, (g + 1) * HEAD_DIM)
            wm = jnp.where(tri, w_ref[g], 0.0).astype(bf16)
            mixed = _dot(wm, vn[:, sl]) + bt_ref[:, g:g + 1]
            dmix = dys[:, sl] * u[:, sl]
            dp_ref[:, sl] = (dys[:, sl] * mixed * gu[:, sl]).astype(bf16)
            dp_ref[:, 2 * da + g * HEAD_DIM:2 * da + (g + 1) * HEAD_DIM] = (
                dy[:, sl] * u[:, sl] * mixed * gs[:, sl]).astype(bf16)
            dmb = dmix.astype(bf16)
            dw_ref[g] += jnp.where(tri, _dot(dmb, vn[:, sl], NT), 0.0)
            dvn_ref[:, sl] = _dot(wm, dmb, TN)
            db = db + jnp.where(lane == g, jnp.sum(dmix, axis=1, keepdims=True), 0.0)
        db_ref[...] += db
        dvn = dvn_ref[...]
        dlg_ref[...] += jnp.sum(dvn * vhat, axis=0, keepdims=True)
        dlb_ref[...] += jnp.sum(dvn, axis=0, keepdims=True)
        dvhat = dvn * lg
        dv = rs * (dvhat - jnp.mean(dvhat, axis=-1, keepdims=True)
                   - vhat * jnp.mean(dvhat * vhat, axis=-1, keepdims=True))
        dp_ref[:, da:2 * da] = (dv * gv).astype(bf16)

    blk = lambda cb: pl.BlockSpec((CHUNK, da), lambda c: (c, cb))
    full = lambda shp: pl.BlockSpec(shp, lambda c: (0,) * len(shp))
    return pl.pallas_call(
        body, name="sgu_bwd", grid=(S // CHUNK,),
        in_specs=[blk(0), blk(1), blk(2), blk(0), full((1, da)), full((1, da)),
                  full((A_GROUPS, CHUNK, CHUNK)), full((CHUNK, A_GROUPS))],
        out_specs=[pl.BlockSpec((CHUNK, 3 * da), lambda c: (c, 0)), full((A_GROUPS, CHUNK, CHUNK)),
                   full((CHUNK, HEAD_DIM)), full((1, da)), full((1, da))],
        out_shape=[SDS((S, n_proj), bf16), SDS((A_GROUPS, CHUNK, CHUNK), f32), SDS((CHUNK, HEAD_DIM), f32),
                   SDS((1, da), f32), SDS((1, da), f32)],
        scratch_shapes=[pltpu.VMEM((CHUNK, da), f32)], compiler_params=_params(),
    )(proj, proj, proj, dy, ln_g, ln_b, w_s, b_t)


def _sb_scores(q, kblk, kb, rows, cols, masked):
    z = _dot(q, kblk, NT) * (1.0 / math.sqrt(HEAD_DIM))
    t = jnp.log(1.0 + jnp.exp(-jnp.abs(z)))
    log_1mb = -(jnp.maximum(z, 0.0) + t)
    log_beta = jnp.minimum(z, 0.0) - t
    if not masked:
        return None, log_beta, log_1mb
    causal = (cols + kb * TK) < rows
    return causal, log_beta, jnp.where(causal, log_1mb, 0.0)


def _sb_tiles(i):
    rows = i * TQ + lax.broadcasted_iota(jnp.int32, (TQ, TK), 0)
    cols = lax.broadcasted_iota(jnp.int32, (TQ, TK), 1)
    r_i = lax.broadcasted_iota(jnp.int32, (TK, TK), 0)
    c_i = lax.broadcasted_iota(jnp.int32, (TK, TK), 1)
    upper, lower = (r_i > c_i).astype(bf16), (r_i < c_i).astype(bf16)
    slot = lax.broadcasted_iota(jnp.int32, (TQ, CARRY_LANES), 1)
    return rows, cols, slot, jnp.concatenate([upper, upper], axis=0), jnp.concatenate([lower, lower], axis=0)


def _suffix_sum(t, tri):
    hi = lax.bitcast_convert_type(lax.bitcast_convert_type(t, jnp.uint32) & jnp.uint32(0xFFFF0000), f32)
    both = jnp.concatenate([hi.astype(bf16), (t - hi).astype(bf16)], axis=1)
    return _dot(both, tri)


def _sb_fwd(proj, y_prev, col0, after):
    S = proj.shape[0]
    D = y_prev.shape[1]
    dh = N_HEADS * HEAD_DIM
    n_diag = TQ // TK

    def body(q_ref, k_ref, v_ref, z_ref, yp_ref, after_ref, y_ref, o_ref, car_ref, qb, kb_s, vb_s, c_ref):
        i = pl.program_id(0)

        @pl.when(i == 0)
        def _():
            kb_s[...] = k_ref[...].astype(bf16)
            vb_s[...] = v_ref[...].astype(bf16)

        qb[...] = q_ref[...].astype(bf16)
        o_ref[...] = jnp.zeros_like(o_ref)
        c_ref[...] = jnp.zeros_like(c_ref)
        car_ref[...] = jnp.zeros_like(car_ref)
        nkb = (i + 1) * n_diag
        rows, cols, slot, upper, _ = _sb_tiles(i)

        def make_step(masked):
            def step(jj, carry):
                kb = nkb - 1 - jj
                off = pl.multiple_of(kb * TK, TK)
                hs = range(N_HEADS)
                sls = [slice(h * HEAD_DIM, (h + 1) * HEAD_DIM) for h in hs]
                sc = [_sb_scores(qb[:, sls[h]], kb_s[pl.ds(off, TK), sls[h]], kb, rows, cols, masked) for h in hs]
                suf = [_suffix_sum(sc[h][2], upper) for h in hs]
                cs = [c_ref[h] for h in hs]
                es = [jnp.exp(sc[h][1] + suf[h] + cs[h][:, :1]) for h in hs]
                if masked:
                    es = [jnp.where(sc[h][0], es[h], 0.0) for h in hs]
                pv = [_dot(es[h].astype(bf16), vb_s[pl.ds(off, TK), sls[h]]) for h in hs]
                for h in hs:
                    o_ref[:, sls[h]] += pv[h]
                    car_ref[h] = jnp.where(slot == kb, cs[h], car_ref[h])
                    c_ref[h] = cs[h] + jnp.sum(sc[h][2], axis=1, keepdims=True)
                return carry
            return step

        lax.fori_loop(0, n_diag, make_step(True), 0)
        lax.fori_loop(n_diag, nkb, make_step(False), 0)
        z = z_ref[...]
        y_ref[...] = (o_ref[...] * (z * jax.nn.sigmoid(z))).astype(bf16)

    cb = col0 * HEAD_DIM // dh
    qspec = lambda k: pl.BlockSpec((TQ, dh), lambda i: (i, cb + k))
    kspec = lambda k: pl.BlockSpec((S, dh), lambda i: (0, cb + k))
    return pl.pallas_call(
        body, name="sb_fwd", grid=(S // TQ,),
        in_specs=[qspec(0), kspec(1), kspec(2), qspec(3), pl.BlockSpec(memory_space=pl.ANY),
                  pl.BlockSpec(memory_space=pl.ANY)],
        out_specs=[pl.BlockSpec((TQ, dh), lambda i: (i, A_GROUPS * HEAD_DIM // dh)),
                   pl.BlockSpec((TQ, dh), lambda i: (i, 0)),
                   pl.BlockSpec((N_HEADS, TQ, CARRY_LANES), lambda i: (0, i, 0))],
        out_shape=[SDS((S, D), bf16), SDS((S, dh), f32), SDS((N_HEADS, S, CARRY_LANES), f32)],
        input_output_aliases={4: 0},
        scratch_shapes=[pltpu.VMEM((TQ, dh), bf16), pltpu.VMEM((S, dh), bf16), pltpu.VMEM((S, dh), bf16),
                        pltpu.VMEM((N_HEADS, TQ, CARRY_LANES), f32)],
        compiler_params=_params(),
    )(proj, proj, proj, proj, y_prev, after)


def _sb_bwd(proj, o, car, dy, dproj_prev, col0, after):
    S = proj.shape[0]
    n_i = S // TQ
    dh = N_HEADS * HEAD_DIM
    n_diag = TQ // TK
    cb = col0 * HEAD_DIM // dh
    scale = 1.0 / math.sqrt(HEAD_DIM)

    def body(q_ref, k_ref, v_ref, z_ref, o_ref, car_ref, dy_ref, dpp_ref, after_ref,
             dp_ref, qb, kb_s, vb_s, dob, p_ref, dq_acc, dk_acc, dv_acc, st_a, st_b, st_k, st_v):
        i = pl.program_id(0)

        def put(stage_ref, row0, nrows, k):
            pltpu.sync_copy(stage_ref, dp_ref.at[pl.ds(row0, nrows), pl.ds((cb + k) * dh, dh)])

        @pl.when(i == 0)
        def _():
            kb_s[...] = k_ref[...].astype(bf16)
            vb_s[...] = v_ref[...].astype(bf16)
            dk_acc[...] = jnp.zeros_like(dk_acc)
            dv_acc[...] = jnp.zeros_like(dv_acc)

        s, gs = _silu_and_grad(z_ref[...])
        dy = dy_ref[...]
        st_b[...] = (dy * o_ref[...] * gs).astype(bf16)
        dob[...] = (dy * s).astype(bf16)
        qb[...] = q_ref[...].astype(bf16)
        p_ref[...] = jnp.zeros_like(p_ref)
        dq_acc[...] = jnp.zeros_like(dq_acc)
        nkb = (i + 1) * n_diag
        rows, cols, slot, upper, lower = _sb_tiles(i)

        def make_step(masked):
            def step(kb, carry):
                off = pl.multiple_of(kb * TK, TK)
                hs = range(N_HEADS)
                sls = [slice(h * HEAD_DIM, (h + 1) * HEAD_DIM) for h in hs]
                qs = [qb[:, sls[h]] for h in hs]
                ks = [kb_s[pl.ds(off, TK), sls[h]] for h in hs]
                dos = [dob[:, sls[h]] for h in hs]
                sc = [_sb_scores(qs[h], ks[h], kb, rows, cols, masked) for h in hs]
                da = [_dot(dos[h], vb_s[pl.ds(off, TK), sls[h]], NT) for h in hs]
                suf = [_suffix_sum(sc[h][2], upper) for h in hs]
                onehot = slot == kb
                cs = [jnp.sum(jnp.where(onehot, car_ref[h], 0.0), axis=1, keepdims=True) for h in hs]
                es = [jnp.exp(sc[h][1] + suf[h] + cs[h]) for h in hs]
                if masked:
                    es = [jnp.where(sc[h][0], es[h], 0.0) for h in hs]
                gs_ = [da[h] * es[h] for h in hs]
                ps = [p_ref[h] for h in hs]
                pre = [_suffix_sum(gs_[h], lower) + ps[h][:, :1] for h in hs]
                dzs = []
                for h in hs:
                    beta = jnp.exp(sc[h][1])
                    dzz = gs_[h] * (1.0 - beta) - beta * pre[h]
                    if masked:
                        dzz = jnp.where(sc[h][0], dzz, 0.0)
                    dzs.append((dzz * scale).astype(bf16))
                dqs = [_dot(dzs[h], ks[h]) for h in hs]
                dks = [_dot(dzs[h], qs[h], TN) for h in hs]
                dvs = [_dot(es[h].astype(bf16), dos[h], TN) for h in hs]
                for h in hs:
                    dq_acc[:, sls[h]] += dqs[h]
                    dk_acc[pl.ds(off, TK), sls[h]] += dks[h]
                    dv_acc[pl.ds(off, TK), sls[h]] += dvs[h]
                    p_ref[h] = ps[h] + jnp.sum(gs_[h], axis=1, keepdims=True)
                return carry
            return step

        lax.fori_loop(0, nkb - n_diag, make_step(False), 0)
        lax.fori_loop(nkb - n_diag, nkb, make_step(True), 0)
        st_a[...] = dq_acc[...].astype(bf16)
        row0 = pl.multiple_of(i * TQ, TQ)
        put(st_a, row0, TQ, 0)
        put(st_b, row0, TQ, 3)

        @pl.when(i == n_i - 1)
        def _():
            st_k[...] = dk_acc[...].astype(bf16)
            st_v[...] = dv_acc[...].astype(bf16)
            put(st_k, 0, S, 1)
            put(st_v, 0, S, 2)

    qspec = lambda k: pl.BlockSpec((TQ, dh), lambda i: (i, cb + k))
    kspec = lambda k: pl.BlockSpec((S, dh), lambda i: (0, cb + k))
    return pl.pallas_call(
        body, name="sb_bwd", grid=(n_i,),
        in_specs=[qspec(0), kspec(1), kspec(2), qspec(3),
                  pl.BlockSpec((TQ, dh), lambda i: (i, 0)),
                  pl.BlockSpec((N_HEADS, TQ, CARRY_LANES), lambda i: (0, i, 0)),
                  pl.BlockSpec((TQ, dh), lambda i: (i, A_GROUPS * HEAD_DIM // dh)),
                  pl.BlockSpec(memory_space=pl.ANY), pl.BlockSpec(memory_space=pl.ANY)],
        out_specs=pl.BlockSpec(memory_space=pl.ANY),
        out_shape=SDS(dproj_prev.shape, bf16),
        input_output_aliases={7: 0},
        scratch_shapes=[pltpu.VMEM((TQ, dh), bf16), pltpu.VMEM((S, dh), bf16), pltpu.VMEM((S, dh), bf16),
                        pltpu.VMEM((TQ, dh), bf16), pltpu.VMEM((N_HEADS, TQ, CARRY_LANES), f32), pltpu.VMEM((TQ, dh), f32),
                        pltpu.VMEM((S, dh), f32), pltpu.VMEM((S, dh), f32),
                        pltpu.VMEM((TQ, dh), bf16), pltpu.VMEM((TQ, dh), bf16),
                        pltpu.VMEM((S, dh), bf16), pltpu.VMEM((S, dh), bf16)],
        compiler_params=_params(56),
    )(proj, proj, proj, proj, o, car, dy, dproj_prev, after)


def _mem_kv(mem, mg_row, w_kv_g):
    M, D = mem.shape
    rb, n = w_kv_g.shape[1], w_kv_g.shape[2]

    def body(m_ref, g_ref, w_ref, kv_ref):
        mv = m_ref[...]
        r = lax.rsqrt(jnp.mean(mv * mv, axis=-1, keepdims=True) + EPS)
        mh = (mv * r * g_ref[...]).astype(bf16)
        kv_ref[...] = _dot(mh, w_ref[...].reshape(N_DEV * rb, n))

    return pl.pallas_call(
        body, name="mem_kv", grid=(1,),
        in_specs=[pl.BlockSpec((M, D), lambda i: (0, 0)), pl.BlockSpec((1, D), lambda i: (0, 0)),
                  pl.BlockSpec((N_DEV, rb, n), lambda i: (0, 0, 0))],
        out_specs=pl.BlockSpec((M, n), lambda i: (0, 0)),
        out_shape=SDS((M, n), f32), compiler_params=_params(),
    )(mem, mg_row, w_kv_g)


def _xattn_head(q_ref, kv_ref, qg, kg, h):
    dc = N_HEADS * HEAD_DIM
    sl = slice(h * HEAD_DIM, (h + 1) * HEAD_DIM)
    qh = q_ref[:, sl]
    rq = lax.rsqrt(jnp.mean(qh * qh, axis=-1, keepdims=True) + EPS)
    qhat = qh * rq
    qn = (qhat * qg).astype(bf16)
    kh = kv_ref[:, sl]
    rk = lax.rsqrt(jnp.mean(kh * kh, axis=-1, keepdims=True) + EPS)
    kn = (kh * rk * kg).astype(bf16)
    vh = kv_ref[:, dc + h * HEAD_DIM:dc + (h + 1) * HEAD_DIM].astype(bf16)
    s = _dot(qn, kn, NT) * (1.0 / math.sqrt(HEAD_DIM))
    e = jnp.exp(s - jnp.max(s, axis=-1, keepdims=True))
    p = e / jnp.sum(e, axis=-1, keepdims=True)
    o = _dot(p.astype(bf16), vh)
    return sl, rq, qhat, qn, kn, vh, p, o


def _xattn_fwd(proj, kv, qg_row, kg_row, y_prev, col0, tq=512):
    S = proj.shape[0]
    D = y_prev.shape[1]
    dc = N_HEADS * HEAD_DIM
    M = kv.shape[0]

    def body(q_ref, z_ref, kv_ref, qg_ref, kg_ref, yp_ref, y_ref):
        for h in range(N_HEADS):
            sl, _, _, _, _, _, _, o = _xattn_head(q_ref, kv_ref, qg_ref[...], kg_ref[...], h)
            z = z_ref[:, sl]
            y_ref[:, sl] = (o * (z * jax.nn.sigmoid(z))).astype(bf16)

    full = lambda shp: pl.BlockSpec(shp, lambda i: (0,) * len(shp))
    return pl.pallas_call(
        body, name="xattn_fwd", grid=(S // tq,),
        in_specs=[pl.BlockSpec((tq, dc), lambda i: (i, col0)), pl.BlockSpec((tq, dc), lambda i: (i, col0 + 1)),
                  full((M, 2 * dc)), full((1, HEAD_DIM)), full((1, HEAD_DIM)), pl.BlockSpec(memory_space=pl.ANY)],
        out_specs=pl.BlockSpec((tq, dc), lambda i: (i, D // dc - 1)),
        out_shape=SDS((S, D), bf16), input_output_aliases={5: 0}, compiler_params=_params(),
    )(proj, proj, kv, qg_row, kg_row, y_prev)


def _xattn_bwd(proj, kv, qg_row, kg_row, dy, dproj_prev, col0, tq=512):
    S = proj.shape[0]
    D = dy.shape[1]
    dc = N_HEADS * HEAD_DIM
    M = kv.shape[0]

    def body(q_ref, z_ref, kv_ref, qg_ref, kg_ref, dy_ref, dpp_ref, dp_ref, dkn_ref, dv_ref, dqg_ref):
        @pl.when(pl.program_id(0) == 0)
        def _():
            dkn_ref[...] = jnp.zeros_like(dkn_ref)
            dv_ref[...] = jnp.zeros_like(dv_ref)
            dqg_ref[...] = jnp.zeros_like(dqg_ref)

        qg = qg_ref[...]
        for h in range(N_HEADS):
            sl, rq, qhat, qn, kn, vh, p, o = _xattn_head(q_ref, kv_ref, qg, kg_ref[...], h)
            s, gs = _silu_and_grad(z_ref[:, sl])
            dyh = dy_ref[:, sl]
            dp_ref[:, dc + h * HEAD_DIM:dc + (h + 1) * HEAD_DIM] = (dyh * o * gs).astype(bf16)
            dob = (dyh * s).astype(bf16)
            dpr = _dot(dob, vh, NT)
            dv_ref[:, sl] += _dot(p.astype(bf16), dob, TN)
            ds = (p * (dpr - jnp.sum(p * dpr, axis=-1, keepdims=True)) * (1.0 / math.sqrt(HEAD_DIM))).astype(bf16)
            dqn = _dot(ds, kn)
            dkn_ref[:, sl] += _dot(ds, qn, TN)
            dqg_ref[...] += jnp.sum(dqn * qhat, axis=0, keepdims=True)
            dqhat = dqn * qg
            dp_ref[:, sl] = (rq * (dqhat - qhat * jnp.mean(dqhat * qhat, axis=-1, keepdims=True))).astype(bf16)

    full = lambda shp: pl.BlockSpec(shp, lambda i: (0,) * len(shp))
    return pl.pallas_call(
        body, name="xattn_bwd", grid=(S // tq,),
        in_specs=[pl.BlockSpec((tq, dc), lambda i: (i, col0)), pl.BlockSpec((tq, dc), lambda i: (i, col0 + 1)),
                  full((M, 2 * dc)), full((1, HEAD_DIM)), full((1, HEAD_DIM)),
                  pl.BlockSpec((tq, dc), lambda i: (i, D // dc - 1)), pl.BlockSpec(memory_space=pl.ANY)],
        out_specs=[pl.BlockSpec((tq, 2 * dc), lambda i: (i, col0 // 2)), full((M, dc)), full((M, dc)),
                   full((1, HEAD_DIM))],
        out_shape=[SDS(dproj_prev.shape, bf16), SDS((M, dc), f32), SDS((M, dc), f32), SDS((1, HEAD_DIM), f32)],
        input_output_aliases={6: 0}, compiler_params=_params(),
    )(proj, proj, kv, qg_row, kg_row, dy, dproj_prev)


def _mem_bwd(mem, mg_row, kv, dkn, dv, kg_row, w_kv_g):
    M, D = mem.shape
    rb, n = w_kv_g.shape[1], w_kv_g.shape[2]
    dc = n // 2

    def body(m_ref, g_ref, kv_ref, dkn_ref, dv_ref, kg_ref, w_ref, dw_ref, dmg_ref, dkg_ref, dkv_ref):
        mv = m_ref[...]
        r = lax.rsqrt(jnp.mean(mv * mv, axis=-1, keepdims=True) + EPS)
        mhat = mv * r
        mh = (mhat * g_ref[...]).astype(bf16)
        kg = kg_ref[...]
        dkg = jnp.zeros((1, HEAD_DIM), f32)
        for h in range(N_HEADS):
            sl = slice(h * HEAD_DIM, (h + 1) * HEAD_DIM)
            kh = kv_ref[:, sl]
            rk = lax.rsqrt(jnp.mean(kh * kh, axis=-1, keepdims=True) + EPS)
            khat = kh * rk
            dkn_h = dkn_ref[:, sl]
            dkg = dkg + jnp.sum(dkn_h * khat, axis=0, keepdims=True)
            dkhat = dkn_h * kg
            dkv_ref[:, sl] = (rk * (dkhat - khat * jnp.mean(dkhat * khat, axis=-1, keepdims=True))).astype(bf16)
        dkv_ref[:, dc:] = dv_ref[...].astype(bf16)
        dkg_ref[...] = dkg
        dkv = dkv_ref[...]
        dw_ref[...] = _dot(mh, dkv, TN).astype(bf16).reshape(N_DEV, rb, n)
        dmh = _dot(dkv, w_ref[...].reshape(N_DEV * rb, n), NT)
        dmg_ref[...] = jnp.sum(dmh * mhat, axis=0, keepdims=True)

    full = lambda shp: pl.BlockSpec(shp, lambda i: (0,) * len(shp))
    wspec = full((N_DEV, rb, n))
    return pl.pallas_call(
        body, name="mem_bwd", grid=(1,),
        in_specs=[full((M, D)), full((1, D)), full((M, n)), full((M, dc)), full((M, dc)), full((1, HEAD_DIM)), wspec],
        out_specs=[wspec, full((1, D)), full((1, HEAD_DIM))],
        out_shape=[SDS((N_DEV, rb, n), bf16), SDS((1, D), f32), SDS((1, HEAD_DIM), f32)],
        scratch_shapes=[pltpu.VMEM((M, n), bf16)], compiler_params=_params(),
    )(mem, mg_row, kv, dkn, dv, kg_row, w_kv_g)


SMALL = ("norm_g", "sgu_ln_g", "sgu_ln_b", "sgu_w", "sgu_b", "mem_norm_g", "q_norm_g", "k_norm_g")


def _small_rows(like):
    rows = [math.prod(like[n].shape) // 128 for n in SMALL]
    offs = [0]
    for r in rows:
        offs.append(offs[-1] + -(-r // 8) * 8)
    return rows, offs


def _pack_small(parts, offs):
    pieces = []
    for k, n in enumerate(SMALL):
        a = parts[n].reshape(-1, 128)
        pieces.append(jnp.pad(a, ((0, offs[k + 1] - offs[k] - a.shape[0]), (0, 0))))
    return jnp.concatenate(pieces)


def kernel(x, mem, norm_g, w_in, sgu_ln_g, sgu_ln_b, sgu_w, sgu_b, mem_norm_g, w_mem_kv, q_norm_g, k_norm_g, w_out, loss_target, m_norm_g, m_w_in, m_sgu_ln_g, m_sgu_ln_b, m_sgu_w, m_sgu_b, m_mem_norm_g, m_w_mem_kv, m_q_norm_g, m_k_norm_g, m_w_out, v_norm_g, v_w_in, v_sgu_ln_g, v_sgu_ln_b, v_sgu_w, v_sgu_b, v_mem_norm_g, v_w_mem_kv, v_q_norm_g, v_k_norm_g, v_w_out):
    L, D, wc = w_in.shape
    S = x.shape[1]
    da = D // 2
    xs = x.reshape(S, D)
    mems = mem.reshape(mem.shape[1], D)
    tgt = loss_target.reshape(S, D)
    row = lambda a, l: a[l].reshape(1, -1)
    tie = lambda a, tok: a + tok[0:1, 0:1]
    sb_col, xa_col = 3 * da // HEAD_DIM, (3 * da + D) // (D // 4)

    ax, ay, ac = lax.axis_index("x"), lax.axis_index("y"), lax.axis_index("c")
    ids = jnp.stack([4 * ax + 2 * ay + ac, 2 * ax + ay, ac]).astype(jnp.int32)
    w_in0_b = _cast_into_slot("cast_w_in", w_in, 0, 512, ids, ids)
    first = _gather2_start("gather_w_in0", [w_in0_b], ids)
    late = first["token"]
    w_b = [(w_in0_b if l == 0 else _cast_into_slot("cast_w_in", w_in, l, 512, ids, late),
            _cast_into_slot("cast_w_kv", w_mem_kv, l, 256, ids, late),
            _cast_into_slot("cast_w_out", w_out, l, 256, ids, late)) for l in range(L)]
    in_fwd = _gather2_forward("gather_w_in0_forward",
                              _split_wait(first, *[a for wl in w_b for a in wl if a is not w_in0_b]), ids)

    acts = []
    xl = xs
    for l in range(L):
        g_row = row(norm_g, l)
        (w_in_g,) = _split_wait(in_fwd, g_row if l else in_fwd["token"])
        rest = _gather2_start(f"gather_w_rest{l}", [w_b[l][1], w_b[l][2]], w_in_g)
        proj, h = _rms_proj(xl, tie(g_row, rest["token"]), w_in_g)
        rest_fwd = _gather2_forward(f"gather_w_rest{l}_forward", _split_wait(rest, proj), proj)
        lg_row = tie(row(sgu_ln_g, l), rest_fwd["token"])
        if l + 1 < L:
            nxt = _gather2_start(f"gather_w_in{l + 1}", [w_b[l + 1][0]], proj)
            lg_row = tie(lg_row, nxt["token"])
        y = _sgu_fwd(proj, lg_row, row(sgu_ln_b, l), sgu_w[l], sgu_b[l].T)
        y, o_b, car = _sb_fwd(proj, y, sb_col, lg_row)
        w_kv_g, w_out_g = _split_wait(rest_fwd, o_b)
        kv = _mem_kv(mems, row(mem_norm_g, l), w_kv_g)
        y = _xattn_fwd(proj, kv, row(q_norm_g, l), row(k_norm_g, l), y, xa_col)
        order = kv
        if l + 1 < L:
            in_fwd = _gather2_forward(f"gather_w_in{l + 1}_forward", _split_wait(nxt, y), y)
            order = in_fwd["token"]
        x_next = _out_proj(xl, y, w_out_g, order)
        acts.append((xl, proj, h, y, o_b, car, kv, w_in_g, w_kv_g, w_out_g))
        xl = x_next

    dx, dxb, loss_part = _loss_and_grad(xl, tgt, 512)
    loss = lax.psum(loss_part[0, 0], ("x", "y", "c"))

    weights = dict(norm_g=norm_g, sgu_ln_g=sgu_ln_g, sgu_ln_b=sgu_ln_b, sgu_w=sgu_w, sgu_b=sgu_b,
                   mem_norm_g=mem_norm_g, q_norm_g=q_norm_g, k_norm_g=k_norm_g)
    moms_m = dict(norm_g=m_norm_g, sgu_ln_g=m_sgu_ln_g, sgu_ln_b=m_sgu_ln_b, sgu_w=m_sgu_w, sgu_b=m_sgu_b,
                  mem_norm_g=m_mem_norm_g, q_norm_g=m_q_norm_g, k_norm_g=m_k_norm_g)
    moms_v = dict(norm_g=v_norm_g, sgu_ln_g=v_sgu_ln_g, sgu_ln_b=v_sgu_ln_b, sgu_w=v_sgu_w, sgu_b=v_sgu_b,
                  mem_norm_g=v_mem_norm_g, q_norm_g=v_q_norm_g, k_norm_g=v_k_norm_g)
    small_rows, small_offs = _small_rows(weights)
    head_rows = D // 128
    assert SMALL[0] == "norm_g" and head_rows % 8 == 0

    seconds = {}
    pending = None
    adam = {"w_out": None, "w_mem_kv": None, "w_in": None}

    def update(lu, order):
        (r_out,) = _scatter2_finish(seconds[f"g_out{lu}"], order)
        adam["w_out"] = _sum_adam("adam_w_out", r_out, w_out, m_w_out, v_w_out, lu, adam["w_out"], 128, ids)
        r_kv, r_in = _scatter2_finish(seconds[f"g_rest{lu}"], adam["w_out"][0])
        adam["w_mem_kv"] = _sum_adam("adam_w_kv", r_kv, w_mem_kv, m_w_mem_kv, v_w_mem_kv, lu, adam["w_mem_kv"], 256, ids)
        adam["w_in"] = _sum_adam("adam_w_in", r_in, w_in, m_w_in, v_w_in, lu, adam["w_in"], 256, ids)
        return adam["w_in"][0]
    small = {n: [None] * L for n in SMALL}
    for l in reversed(range(L)):
        xl, proj, h, y, o_b, car, kv, w_in_g, w_kv_g, w_out_g = acts[l]
        dy = _out_bwd_dy(dxb, w_out_g)
        order = dy
        if pending is not None:
            seconds[pending[0]] = _scatter2_second_level(pending[0], pending[1], dy, ids)
            order = seconds[pending[0]]["token"]
        g_out = _tn_grad("out_bwd_dw", y, dxb, 512, 512, True, order)
        first = _scatter2_pair_start(f"scatter_g_out{l}_pair", [g_out], ids)
        dproj, d_sw, d_sb, d_lg, d_lb = _sgu_bwd(proj, dy, tie(row(sgu_ln_g, l), first["token"]), row(sgu_ln_b, l),
                                                 sgu_w[l], sgu_b[l].T)
        seconds[f"g_out{l}"] = _scatter2_second_level(f"g_out{l}", first, dproj, ids)
        dproj = _sb_bwd(proj, o_b, car, dy, dproj, sb_col, seconds[f"g_out{l}"]["token"])
        dproj, dkn, dv, d_qg = _xattn_bwd(proj, kv, row(q_norm_g, l), row(k_norm_g, l), dy, dproj, xa_col)
        g_kv, d_mg, d_kg = _mem_bwd(mems, row(mem_norm_g, l), kv, dkn, dv, row(k_norm_g, l), w_kv_g)
        for n, val in (("sgu_ln_g", d_lg), ("sgu_ln_b", d_lb), ("sgu_w", d_sw), ("sgu_b", d_sb[:, :A_GROUPS].T),
                       ("mem_norm_g", d_mg), ("q_norm_g", d_qg), ("k_norm_g", d_kg)):
            small[n][l] = val.reshape(-1)
        order = d_kg
        if l == 0:
            small["norm_g"][0] = jnp.zeros_like(small["norm_g"][1])
            part = _pack_small({n: jnp.stack(small[n]) for n in SMALL}, small_offs)
            tail = _gather2_start("gather_small_tail", [_into_slot("small_tail_slot", part[head_rows:], ids)], ids)
            order = tail["token"]
        g_in_l = _tn_grad("in_bwd_dw", h, dproj, 1024, wc, False, order)
        first = _scatter2_pair_start(f"scatter_g_rest{l}_pair", [g_kv, g_in_l], ids)
        g_row = tie(row(norm_g, l), first["token"])
        pending = (f"g_rest{l}", first)
        if l == 0:
            tail_fwd = _gather2_forward("gather_small_tail_forward", _split_wait(tail, g_in_l), g_in_l)
            order = tail_fwd["token"]
            for lu in reversed(range(1, L)):
                order = update(lu, order)
            seconds[pending[0]] = _scatter2_second_level(pending[0], pending[1], order, ids)
            g_row = tie(g_row, seconds[pending[0]]["token"])
        dh = _in_bwd_dh(dproj, w_in_g, g_row)
        dx, dxb, d_ng = _rms_bwd(dh, xl, g_row, dx)
        small["norm_g"][l] = d_ng.reshape(-1)
    head = _gather2_start("gather_small_head",
                          [_into_slot("small_head_slot", small["norm_g"][0].reshape(head_rows, 128), ids)], ids)

    (r_out,) = _scatter2_finish(seconds["g_out0"], head["token"])
    adam["w_out"] = _sum_adam("adam_w_out", r_out, w_out, m_w_out, v_w_out, 0, adam["w_out"], 128, ids)
    head_fwd = _gather2_forward("gather_small_head_forward", _split_wait(head, adam["w_out"][0]), ids)
    (r_tail,) = _split_wait(tail_fwd, adam["w_out"][0])
    (r_head,) = _split_wait(head_fwd, r_tail)
    as128 = lambda d: [d[n].reshape(-1, 128) for n in SMALL]
    sm = _small_sum_adam(r_head, r_tail, as128(weights), as128(moms_m), as128(moms_v), small_offs)
    r_kv, r_in = _scatter2_finish(seconds["g_rest0"], sm[0])
    adam["w_mem_kv"] = _sum_adam("adam_w_kv", r_kv, w_mem_kv, m_w_mem_kv, v_w_mem_kv, 0, adam["w_mem_kv"], 256, ids)
    adam["w_in"] = _sum_adam("adam_w_in", r_in, w_in, m_w_in, v_w_in, 0, adam["w_in"], 256, ids)
    res = dict(adam)
    for p, n in enumerate(SMALL):
        res[n] = [sm[k * len(SMALL) + p].reshape(weights[n].shape) for k in range(4)]

    order = ("norm_g", "w_in", "sgu_ln_g", "sgu_ln_b", "sgu_w", "sgu_b", "mem_norm_g", "w_mem_kv", "q_norm_g",
             "k_norm_g", "w_out")
    outs = [loss, dx.reshape(x.shape)]
    for k in range(4):
        outs += [res[n][k] for n in order]
    return tuple(outs)
```

```python
import functools
import math

import jax
import jax.numpy as jnp
from jax import lax
from jax.experimental import pallas as pl
from jax.experimental.pallas import tpu as pltpu

f32 = jnp.float32
bf16 = jnp.bfloat16
SDS = jax.ShapeDtypeStruct

N_DEV = 8
EPS = 1e-6
CHUNK = 128
A_GROUPS = 8
HEAD_DIM = 128
N_HEADS = 4
TQ = 256
TK = 256
CARRY_LANES = 128
ADAM_LR, ADAM_B1, ADAM_B2, ADAM_EPS, ADAM_WD, ADAM_STEP = 0.001, 0.9, 0.999, 1e-08, 0.01, 10
MIB = 1024 * 1024

NT = (((1,), (1,)), ((), ()))
TN = (((0,), (0,)), ((), ()))


def _params(vmem_mib=48):
    return pltpu.CompilerParams(vmem_limit_bytes=vmem_mib * MIB)


def _gelu_and_grad(x):
    e = lax.erf(x * (1.0 / math.sqrt(2.0)))
    cdf = 0.5 * (1.0 + e)
    pdf = jnp.exp(-0.5 * x * x) * (1.0 / math.sqrt(2.0 * math.pi))
    return x * cdf, cdf + x * pdf


def _gelu(x):
    return 0.5 * x * (1.0 + lax.erf(x * (1.0 / math.sqrt(2.0))))


def _silu_and_grad(z):
    sg = jax.nn.sigmoid(z)
    return z * sg, sg * (1.0 + z * (1.0 - sg))


def _layer_spec(stacked, l):
    rest = stacked.shape[1:]
    return pl.BlockSpec((None,) + rest, lambda *idx: (l,) + (0,) * len(rest))


def _dot(a, b, dims=None):
    if dims is None:
        return jnp.dot(a, b, preferred_element_type=f32)
    return lax.dot_general(a, b, dims, preferred_element_type=f32)


_HBM = pl.BlockSpec(memory_space=pltpu.HBM)
_SEM = pl.BlockSpec(memory_space=pltpu.SEMAPHORE)
_EFFECT = pltpu.SideEffectType.DATAFLOW_SIDE_EFFECTING


def _split_start(name, bufs, n_remote, n_local, build, after):
    nb = len(bufs)

    def body(*refs):
        token = refs[-1]
        locals_, remotes = build(refs[:nb], *refs[nb + 1:nb + 4])
        for cp in locals_ + remotes:
            cp.start()
        token[...] = jnp.zeros_like(token)

    hbm = lambda a: pltpu.with_memory_space_constraint(a, pltpu.HBM)
    outs = pl.pallas_call(
        body, name=name,
        out_shape=(pltpu.SemaphoreType.DMA((n_remote,)), pltpu.SemaphoreType.DMA((n_remote,)),
                   pltpu.SemaphoreType.DMA((max(n_local, 1),)),
                   *[pltpu.HBM(b.shape, b.dtype) for b in bufs], SDS((8, 128), f32)),
        in_specs=[_HBM] * nb + [pl.BlockSpec(memory_space=pl.ANY)],
        out_specs=(_SEM, _SEM, _SEM, *[_HBM] * nb, pl.BlockSpec(memory_space=pltpu.VMEM)),
        input_output_aliases={k: 3 + k for k in range(nb)},
        compiler_params=pltpu.CompilerParams(has_side_effects=_EFFECT),
    )(*[hbm(b) for b in bufs], after)
    return dict(name=name, build=build, sems=outs[:3], bufs=outs[3:3 + nb], token=outs[-1])


def _split_wait(handle, *after):
    build, bufs = handle["build"], handle["bufs"]
    nb = len(bufs)

    def body(*refs):
        locals_, remotes = build(refs[:nb], *refs[nb:nb + 3])
        for cp in remotes:
            cp.wait_recv()
        for cp in remotes:
            cp.wait_send()
        for cp in locals_:
            cp.wait()

    outs = pl.pallas_call(
        body, name=handle["name"] + "_wait",
        out_shape=tuple(pltpu.HBM(b.shape, b.dtype) for b in bufs),
        in_specs=[_HBM] * nb + [_SEM] * 3 + [pl.BlockSpec(memory_space=pl.ANY)] * len(after),
        out_specs=tuple([_HBM] * nb),
        input_output_aliases={k: k for k in range(nb)},
        compiler_params=pltpu.CompilerParams(has_side_effects=_EFFECT),
    )(*bufs, *handle["sems"], *after)
    return list(outs)


def _remote(src, dst, send_sems, recv_sems, k, to):
    return pltpu.make_async_remote_copy(src_ref=src, dst_ref=dst, send_sem=send_sems.at[k], recv_sem=recv_sems.at[k],
                                        device_id=to, device_id_type=pl.DeviceIdType.MESH)


def _other_chips(x, y):
    return [(1 - x, y), (x, 1 - y), (1 - x, 1 - y)]


def _gather2_start(name, lands, after):
    def build(refs, send, recv, loc):
        x, y, c = lax.axis_index("x"), lax.axis_index("y"), lax.axis_index("c")
        me = 4 * x + 2 * y + c
        remotes = []
        for a, d in enumerate(refs):
            remotes.append(_remote(d.at[me], d.at[me], send, recv, 4 * a, (x, y, 1 - c)))
            remotes += [_remote(d.at[me], d.at[me], send, recv, 4 * a + 1 + k, (px, py, c))
                        for k, (px, py) in enumerate(_other_chips(x, y))]
        return [], remotes

    return _split_start(name, list(lands), 4 * len(lands), 0, build, after)


def _gather2_forward(name, lands, after):
    n = len(lands)

    def build(refs, send, recv, loc):
        x, y, c = lax.axis_index("x"), lax.axis_index("y"), lax.axis_index("c")
        slots = [4 * px + 2 * py + c for px, py in _other_chips(x, y)]
        return [], [_remote(d.at[sl], d.at[sl], send, recv, 3 * a + k, (x, y, 1 - c))
                    for a, d in enumerate(refs) for k, sl in enumerate(slots)]

    return _split_start(name, list(lands), 3 * n, 0, build, after)


def _scatter2_pair_start(name, srcs, after):
    n = len(srcs)

    def build(refs, send, recv, loc):
        x, y, c = lax.axis_index("x"), lax.axis_index("y"), lax.axis_index("c")
        return [], [_remote(refs[a].at[2 * q + 1 - c], refs[n + a].at[q], send, recv, 4 * a + q, (x, y, 1 - c))
                    for a in range(n) for q in range(4)]

    lands = [lax.empty((4,) + s.shape[1:], s.dtype) for s in srcs]
    return _split_start(name, list(srcs) + lands, 4 * n, 0, build, after)


def _scatter2_chip_start(name, pairs, after):
    n = len(pairs)

    def build(refs, send, recv, loc):
        x, y, c = lax.axis_index("x"), lax.axis_index("y"), lax.axis_index("c")
        return [], [_remote(refs[a].at[2 * px + py], refs[n + a].at[2 * x + y], send, recv, 3 * a + k, (px, py, c))
                    for a in range(n) for k, (px, py) in enumerate(_other_chips(x, y))]

    return _split_start(name, list(pairs) + [lax.empty(p.shape, p.dtype) for p in pairs], 3 * n, 0, build, after)


def _pair_sum(name, src, theirs, ids):
    _, R, C = theirs.shape
    tr = min(R, 1024)

    def body(ids_ref, a_ref, b_ref, o_ref):
        o_ref[...] = (a_ref[...].astype(f32) + b_ref[...].astype(f32)).astype(bf16)

    spec = pl.BlockSpec((None, tr, C), lambda q, i, ids: (q, i, 0))
    return pl.pallas_call(
        body, name=name,
        grid_spec=pltpu.PrefetchScalarGridSpec(
            num_scalar_prefetch=1, grid=(4, R // tr),
            in_specs=[pl.BlockSpec((None, tr, C), lambda q, i, ids: (2 * q + ids[2], i, 0)), spec], out_specs=spec),
        out_shape=SDS(theirs.shape, bf16), compiler_params=_params(),
    )(ids, src, theirs)


def _scatter2_second_level(name, first, after, ids):
    outs = _split_wait(first, after)
    n = len(outs) // 2
    pairs = [_pair_sum(f"pair_sum_{name}{a}", outs[a], outs[n + a], ids) for a in range(n)]
    return _scatter2_chip_start(f"scatter_{name}_chip", pairs, ids)


def _scatter2_finish(second, after):
    outs = _split_wait(second, after)
    n = len(outs) // 2
    return [(outs[a], outs[n + a]) for a in range(n)]


def _cast_into_slot(name, w, l, tr, ids, after):
    _, R, C = w.shape

    def body(ids_ref, w_ref, after_ref, o_ref):
        o_ref[...] = w_ref[...].astype(bf16)

    return pl.pallas_call(
        body, name=name,
        grid_spec=pltpu.PrefetchScalarGridSpec(
            num_scalar_prefetch=1, grid=(R // tr,),
            in_specs=[pl.BlockSpec((None, tr, C), lambda i, ids: (l, i, 0)), pl.BlockSpec(memory_space=pl.ANY)],
            out_specs=pl.BlockSpec((None, tr, C), lambda i, ids: (ids[0], i, 0))),
        out_shape=SDS((N_DEV, R, C), bf16), compiler_params=_params(),
    )(ids, w, after)


def _into_slot(name, a, ids):
    R, C = a.shape

    def body(ids_ref, a_ref, o_ref):
        o_ref[...] = a_ref[...]

    return pl.pallas_call(
        body, name=name,
        grid_spec=pltpu.PrefetchScalarGridSpec(
            num_scalar_prefetch=1, grid=(1,),
            in_specs=[pl.BlockSpec((R, C), lambda i, ids: (0, 0))],
            out_specs=pl.BlockSpec((None, R, C), lambda i, ids: (ids[0], 0, 0))),
        out_shape=SDS((N_DEV, R, C), f32), compiler_params=_params(),
    )(ids, a)


def _adam_math(w, g, m, v):
    m2 = ADAM_B1 * m + (1.0 - ADAM_B1) * g
    v2 = ADAM_B2 * v + (1.0 - ADAM_B2) * (g * g)
    m_hat = m2 / (1.0 - ADAM_B1 ** ADAM_STEP)
    v_hat = v2 / (1.0 - ADAM_B2 ** ADAM_STEP)
    delta = -ADAM_LR * (m_hat / (jnp.sqrt(v_hat) + ADAM_EPS) + ADAM_WD * w)
    return delta, m2, v2


def _sum_adam(name, pair_recv, w, m, v, l, prev, tr, ids):
    own, recv = pair_recv
    L, R, C = w.shape
    slots = recv.shape[0]

    def body(ids_ref, r_ref, own_ref, w_ref, m_ref, v_ref, *rest):
        g_ref, d_ref, m2_ref, v2_ref = rest[-4:]
        terms = [jnp.where(ids_ref[1] == q, own_ref[...], r_ref[q]).astype(f32) for q in range(slots)]
        g = terms[0]
        for t in terms[1:]:
            g = g + t
        d, m2, v2 = _adam_math(w_ref[...], g, m_ref[...], v_ref[...])
        g_ref[...] = g
        d_ref[...] = d
        m2_ref[...] = m2
        v2_ref[...] = v2

    wspec = pl.BlockSpec((None, tr, C), lambda i, ids: (l, i, 0))
    in_specs = [pl.BlockSpec((slots, tr, C), lambda i, ids: (0, i, 0)),
                pl.BlockSpec((None, tr, C), lambda i, ids: (ids[1], i, 0)), wspec, wspec, wspec]
    args = [ids, recv, own, w, m, v]
    aliases = {}
    if prev is not None:
        in_specs += [pl.BlockSpec(memory_space=pl.ANY)] * 4
        args += list(prev)
        aliases = {6 + k: k for k in range(4)}
    return pl.pallas_call(
        body, name=name,
        grid_spec=pltpu.PrefetchScalarGridSpec(num_scalar_prefetch=1, grid=(R // tr,), in_specs=in_specs,
                                               out_specs=[wspec] * 4),
        out_shape=[SDS((L, R, C), f32)] * 4, input_output_aliases=aliases, compiler_params=_params(),
    )(*args)


def _small_sum_adam(recv_head, recv_tail, ws, ms, vs, offs):
    n = len(ws)
    r0 = recv_head.shape[1]

    def body(*refs):
        rh, rt = refs[0], refs[1]
        w_refs, m_refs, v_refs = refs[2:2 + n], refs[2 + n:2 + 2 * n], refs[2 + 2 * n:2 + 3 * n]
        outs = refs[2 + 3 * n:]
        for p in range(n):
            lo, hi = offs[p], offs[p] + ws[p].shape[0]
            pieces = []
            if lo < r0:
                pieces.append((rh, lo, 0, min(hi, r0) - lo))
            if hi > r0:
                pieces.append((rt, max(lo, r0) - r0, max(lo, r0) - lo, hi - max(lo, r0)))
            for src, a, b, cnt in pieces:
                g = src[0, a:a + cnt, :]
                for s in range(1, N_DEV):
                    g = g + src[s, a:a + cnt, :]
                d, m2, v2 = _adam_math(w_refs[p][b:b + cnt, :], g, m_refs[p][b:b + cnt, :], v_refs[p][b:b + cnt, :])
                for k, val in enumerate((g, d, m2, v2)):
                    outs[k * n + p][b:b + cnt, :] = val

    return pl.pallas_call(
        body, name="small_sum_adam", out_shape=[SDS(w.shape, f32) for w in ws] * 4, compiler_params=_params(),
    )(recv_head, recv_tail, *ws, *ms, *vs)


def _loss_and_grad(xf, tgt, tm):
    S, D = xf.shape

    def body(x_ref, t_ref, dx_ref, dxb_ref, l_ref):
        i = pl.program_id(0)
        d = x_ref[...] - t_ref[...]
        dx = d * (1.0 / D)
        dx_ref[...] = dx
        dxb_ref[...] = dx.astype(bf16)
        e = d * d
        part = e[:, 0:128]
        for k in range(1, D // 128):
            part = part + e[:, k * 128:(k + 1) * 128]
        part = jnp.sum(part.reshape(tm // 8, 8, 128), axis=0)

        @pl.when(i == 0)
        def _():
            l_ref[...] = jnp.zeros_like(l_ref)

        l_ref[...] += part

        @pl.when(i == pl.num_programs(0) - 1)
        def _():
            tot = jnp.sum(l_ref[...], axis=1, keepdims=True)
            tot = jnp.sum(tot, axis=0, keepdims=True)
            l_ref[...] = jnp.broadcast_to(tot * (0.5 / D), l_ref.shape)

    row = pl.BlockSpec((tm, D), lambda i: (i, 0))
    return pl.pallas_call(
        body, name="loss_grad", grid=(S // tm,),
        in_specs=[row, row], out_specs=[row, row, pl.BlockSpec((8, 128), lambda i: (0, 0))],
        out_shape=[SDS((S, D), f32), SDS((S, D), bf16), SDS((8, 128), f32)], compiler_params=_params(),
    )(xf, tgt)


def _rms_proj(x, g, l, w_in_g, after, tm=1024):
    S, D = x.shape
    wc = w_in_g.shape[2]
    n_out = N_DEV * wc

    def body(x_ref, g_ref, w_ref, after_ref, proj_ref, h_ref):
        @pl.when(pl.program_id(1) == 0)
        def _():
            xv = x_ref[...]
            r = lax.rsqrt(jnp.mean(xv * xv, axis=-1, keepdims=True) + EPS)
            h_ref[...] = (xv * r * g_ref[...]).astype(bf16)

        proj_ref[...] = _dot(h_ref[...], w_ref[...])

    return pl.pallas_call(
        body, name="rms_proj", grid=(S // tm, N_DEV),
        in_specs=[pl.BlockSpec((tm, D), lambda i, j: (i, 0)), _layer_spec(g, l),
                  pl.BlockSpec((None, D, wc), lambda i, j: (j, 0, 0)), pl.BlockSpec(memory_space=pl.ANY)],
        out_specs=[pl.BlockSpec((tm, wc), lambda i, j: (i, j)), pl.BlockSpec((tm, D), lambda i, j: (i, 0))],
        out_shape=[SDS((S, n_out), f32), SDS((S, D), bf16)], compiler_params=_params(),
    )(x, g, w_in_g, after)


def _out_proj(x, y, w_out_g, after, tm=512):
    S, D = x.shape
    rb = w_out_g.shape[1]

    def body(x_ref, y_ref, w_ref, after_ref, o_ref):
        w = w_ref[...].reshape(N_DEV * rb, D)
        o_ref[...] = x_ref[...] + _dot(y_ref[...], w)

    row = pl.BlockSpec((tm, D), lambda i: (i, 0))
    return pl.pallas_call(
        body, name="out_proj", grid=(S // tm,),
        in_specs=[row, row, pl.BlockSpec((N_DEV, rb, D), lambda i: (0, 0, 0)), pl.BlockSpec(memory_space=pl.ANY)],
        out_specs=row, out_shape=SDS((S, D), f32), compiler_params=_params(),
    )(x, y, w_out_g, after)


def _out_bwd_dy(dxb, w_out_g, tm=512):
    S, D = dxb.shape
    rb = w_out_g.shape[1]

    nb = 2

    def body(dx_ref, w_ref, o_ref):
        o_ref[...] = _dot(dx_ref[...], w_ref[...].reshape(nb * rb, D), NT)

    return pl.pallas_call(
        body, name="out_bwd_dy", grid=(S // tm, N_DEV // nb),
        in_specs=[pl.BlockSpec((tm, D), lambda i, j: (i, 0)),
                  pl.BlockSpec((nb, rb, D), lambda i, j: (j, 0, 0))],
        out_specs=pl.BlockSpec((tm, nb * rb), lambda i, j: (i, j)),
        out_shape=SDS((S, D), f32), compiler_params=_params(),
    )(dxb, w_out_g)


def _tn_grad(name, a, b, tm, tn, rows_major, after):
    S, M = a.shape
    N = b.shape[1]
    if rows_major:
        rb = M // N_DEV
        nb = tm // rb
        out_shape = SDS((N_DEV, rb, N), bf16)
        out_spec = pl.BlockSpec((nb, rb, tn), lambda i, j: (i, 0, j))
    else:
        out_shape = SDS((N_DEV, M, N // N_DEV), bf16)
        assert tn == N // N_DEV
        out_spec = pl.BlockSpec((None, tm, tn), lambda i, j: (j, i, 0))

    def body(a_ref, b_ref, after_ref, o_ref):
        o_ref[...] = _dot(a_ref[...], b_ref[...], TN).astype(bf16).reshape(o_ref.shape)

    return pl.pallas_call(
        body, name=name, grid=(M // tm, N // tn),
        in_specs=[pl.BlockSpec((S, tm), lambda i, j: (0, i)), pl.BlockSpec((S, tn), lambda i, j: (0, j)),
                  pl.BlockSpec(memory_space=pl.ANY)],
        out_specs=out_spec, out_shape=out_shape, compiler_params=_params(),
    )(a, b, after)


def _in_bwd_dh(dproj, w_in_g, after, tm=1024, tn=256):
    S = dproj.shape[0]
    _, D, wc = w_in_g.shape
    tm = min(tm, S)

    def body(dp_ref, w_ref, after_ref, o_ref):
        acc = _dot(dp_ref[:, 0:wc], w_ref[0], NT)
        for k in range(1, N_DEV):
            acc = acc + _dot(dp_ref[:, k * wc:(k + 1) * wc], w_ref[k], NT)
        o_ref[...] = acc

    return pl.pallas_call(
        body, name="in_bwd_dh", grid=(S // tm, D // tn),
        in_specs=[pl.BlockSpec((tm, N_DEV * wc), lambda i, j: (i, 0)),
                  pl.BlockSpec((N_DEV, tn, wc), lambda i, j: (0, j, 0)), pl.BlockSpec(memory_space=pl.ANY)],
        out_specs=pl.BlockSpec((tm, tn), lambda i, j: (i, j)),
        out_shape=SDS((S, D), f32), compiler_params=_params(),
    )(dproj, w_in_g, after)


def _rms_bwd(dh, x, g, l, dx_next, after, tm=256):
    S, D = x.shape

    def body(dh_ref, x_ref, g_ref, dxn_ref, after_ref, dx_ref, dxb_ref, dg_ref):
        @pl.when(pl.program_id(0) == 0)
        def _():
            dg_ref[...] = jnp.zeros_like(dg_ref)

        dh = dh_ref[...]
        xv = x_ref[...]
        r = lax.rsqrt(jnp.mean(xv * xv, axis=-1, keepdims=True) + EPS)
        xhat = xv * r
        dxhat = dh * g_ref[...]
        dx = r * (dxhat - xhat * jnp.mean(dxhat * xhat, axis=-1, keepdims=True)) + dxn_ref[...]
        dx_ref[...] = dx
        dxb_ref[...] = dx.astype(bf16)
        dg_ref[...] += jnp.sum(dh * xhat, axis=0, keepdims=True)

    row = pl.BlockSpec((tm, D), lambda i: (i, 0))
    vec = pl.BlockSpec((1, D), lambda i: (0, 0))
    return pl.pallas_call(
        body, name="rms_bwd", grid=(S // tm,),
        in_specs=[row, row, _layer_spec(g, l), row, pl.BlockSpec(memory_space=pl.ANY)], out_specs=[row, row, vec],
        out_shape=[SDS((S, D), f32), SDS((S, D), bf16), SDS((1, D), f32)], compiler_params=_params(),
    )(dh, x, g, dx_next, after)


def _sgu_fwd(proj, ln_g, ln_b, w_s, b_t, l, after):
    S = proj.shape[0]
    da = A_GROUPS * HEAD_DIM
    D = 2 * da

    def body(u_ref, v_ref, z_ref, lg_ref, lb_ref, w_ref, bt_ref, after_ref, y_ref):
        u = _gelu(u_ref[...])
        v = _gelu(v_ref[...])
        z = z_ref[...]
        mu = jnp.mean(v, axis=-1, keepdims=True)
        xc = v - mu
        rs = lax.rsqrt(jnp.mean(xc * xc, axis=-1, keepdims=True) + EPS)
        vn = (xc * rs * lg_ref[...] + lb_ref[...]).astype(bf16)
        gate = u * (z * jax.nn.sigmoid(z))
        tri = lax.broadcasted_iota(jnp.int32, (CHUNK, CHUNK), 0) >= lax.broadcasted_iota(jnp.int32, (CHUNK, CHUNK), 1)
        for g in range(A_GROUPS):
            sl = slice(g * HEAD_DIM, (g + 1) * HEAD_DIM)
            wm = jnp.where(tri, w_ref[g], 0.0).astype(bf16)
            mixed = _dot(wm, vn[:, sl]) + bt_ref[:, g:g + 1]
            y_ref[:, sl] = (gate[:, sl] * mixed).astype(bf16)

    blk = lambda cb: pl.BlockSpec((CHUNK, da), lambda c: (c, cb))
    full = lambda shp: pl.BlockSpec(shp, lambda c: (0,) * len(shp))
    return pl.pallas_call(
        body, name="sgu_fwd", grid=(S // CHUNK,),
        in_specs=[blk(0), blk(1), blk(2), _layer_spec(ln_g, l), _layer_spec(ln_b, l), _layer_spec(w_s, l),
                  _layer_spec(b_t, l), pl.BlockSpec(memory_space=pl.ANY)],
        out_specs=blk(0), out_shape=SDS((S, D), bf16), compiler_params=_params(),
    )(proj, proj, proj, ln_g, ln_b, w_s, b_t, after)


def _sgu_bwd(proj, dy, ln_g, ln_b, w_s, b_t, l, after):
    S = proj.shape[0]
    da = A_GROUPS * HEAD_DIM
    n_proj = proj.shape[1]

    def body(u_ref, v_ref, z_ref, dy_ref, lg_ref, lb_ref, w_ref, bt_ref, after_ref,
             dp_ref, dw_ref, db_ref, dlg_ref, dlb_ref, dvn_ref):
        @pl.when(pl.program_id(0) == 0)
        def _():
            dw_ref[...] = jnp.zeros_like(dw_ref)
            db_ref[...] = jnp.zeros_like(db_ref)
            dlg_ref[...] = jnp.zeros_like(dlg_ref)
            dlb_ref[...] = jnp.zeros_like(dlb_ref)

        up, vp, z, dy = u_ref[...], v_ref[...], z_ref[...], dy_ref[...]
        u, gu = _gelu_and_grad(up)
        v, gv = _gelu_and_grad(vp)
        s, gs = _silu_and_grad(z)
        mu = jnp.mean(v, axis=-1, keepdims=True)
        xc = v - mu
        rs = lax.rsqrt(jnp.mean(xc * xc, axis=-1, keepdims=True) + EPS)
        vhat = xc * rs
        lg = lg_ref[...]
        vn = (vhat * lg + lb_ref[...]).astype(bf16)
        tri = lax.broadcasted_iota(jnp.int32, (CHUNK, CHUNK), 0) >= lax.broadcasted_iota(jnp.int32, (CHUNK, CHUNK), 1)
        lane = lax.broadcasted_iota(jnp.int32, (CHUNK, HEAD_DIM), 1)
        dys = dy * s
        db = jnp.zeros((CHUNK, HEAD_DIM), f32)
        for g in range(A_GROUPS):
            sl = slice(g * HEAD_DIM, (g + 1) * HEAD_DIM)
            wm = jnp.where(tri, w_ref[g], 0.0).astype(bf16)
            mixed = _dot(wm, vn[:, sl]) + bt_ref[:, g:g + 1]
            dmix = dys[:, sl] * u[:, sl]
            dp_ref[:, sl] = (dys[:, sl] * mixed * gu[:, sl]).astype(bf16)
            dp_ref[:, 2 * da + g * HEAD_DIM:2 * da + (g + 1) * HEAD_DIM] = (
                dy[:, sl] * u[:, sl] * mixed * gs[:, sl]).astype(bf16)
            dmb = dmix.astype(bf16)
            dw_ref[g] += jnp.where(tri, _dot(dmb, vn[:, sl], NT), 0.0)
            dvn_ref[:, sl] = _dot(wm, dmb, TN)
            db = db + jnp.where(lane == g, jnp.sum(dmix, axis=1, keepdims=True), 0.0)
        db_ref[...] += db
        dvn = dvn_ref[...]
        dlg_ref[...] += jnp.sum(dvn * vhat, axis=0, keepdims=True)
        dlb_ref[...] += jnp.sum(dvn, axis=0, keepdims=True)
        dvhat = dvn * lg
        dv = rs * (dvhat - jnp.mean(dvhat, axis=-1, keepdims=True)
                   - vhat * jnp.mean(dvhat * vhat, axis=-1, keepdims=True))
        dp_ref[:, da:2 * da] = (dv * gv).astype(bf16)

    blk = lambda cb: pl.BlockSpec((CHUNK, da), lambda c: (c, cb))
    full = lambda shp: pl.BlockSpec(shp, lambda c: (0,) * len(shp))
    return pl.pallas_call(
        body, name="sgu_bwd", grid=(S // CHUNK,),
        in_specs=[blk(0), blk(1), blk(2), blk(0), _layer_spec(ln_g, l), _layer_spec(ln_b, l), _layer_spec(w_s, l),
                  _layer_spec(b_t, l), pl.BlockSpec(memory_space=pl.ANY)],
        out_specs=[pl.BlockSpec((CHUNK, 3 * da), lambda c: (c, 0)), full((A_GROUPS, CHUNK, CHUNK)),
                   full((CHUNK, HEAD_DIM)), full((1, da)), full((1, da))],
        out_shape=[SDS((S, n_proj), bf16), SDS((A_GROUPS, CHUNK, CHUNK), f32), SDS((CHUNK, HEAD_DIM), f32),
                   SDS((1, da), f32), SDS((1, da), f32)],
        scratch_shapes=[pltpu.VMEM((CHUNK, da), f32)], compiler_params=_params(),
    )(proj, proj, proj, dy, ln_g, ln_b, w_s, b_t, after)


def _sb_scores(q, kblk, kb, rows, cols, masked):
    z = _dot(q, kblk, NT) * (1.0 / math.sqrt(HEAD_DIM))
    t = jnp.log(1.0 + jnp.exp(-jnp.abs(z)))
    log_1mb = -(jnp.maximum(z, 0.0) + t)
    log_beta = jnp.minimum(z, 0.0) - t
    if not masked:
        return None, log_beta, log_1mb
    causal = (cols + kb * TK) < rows
    return causal, log_beta, jnp.where(causal, log_1mb, 0.0)


def _sb_tiles(i):
    rows = i * TQ + lax.broadcasted_iota(jnp.int32, (TQ, TK), 0)
    cols = lax.broadcasted_iota(jnp.int32, (TQ, TK), 1)
    r_i = lax.broadcasted_iota(jnp.int32, (TK, TK), 0)
    c_i = lax.broadcasted_iota(jnp.int32, (TK, TK), 1)
    upper, lower = (r_i > c_i).astype(bf16), (r_i < c_i).astype(bf16)
    slot = lax.broadcasted_iota(jnp.int32, (TQ, CARRY_LANES), 1)
    return rows, cols, slot, jnp.concatenate([upper, upper], axis=0), jnp.concatenate([lower, lower], axis=0)


def _suffix_sum(t, tri):
    hi = lax.bitcast_convert_type(lax.bitcast_convert_type(t, jnp.uint32) & jnp.uint32(0xFFFF0000), f32)
    both = jnp.concatenate([hi.astype(bf16), (t - hi).astype(bf16)], axis=1)
    return _dot(both, tri)


def _sb_fwd(proj, y_prev, col0, after):
    S = proj.shape[0]
    D = y_prev.shape[1]
    dh = N_HEADS * HEAD_DIM
    n_diag = TQ // TK

    def body(q_ref, k_ref, v_ref, z_ref, yp_ref, after_ref, y_ref, o_ref, car_ref, qb, kb_s, vb_s, c_ref):
        i = pl.program_id(0)

        @pl.when(i == 0)
        def _():
            kb_s[...] = k_ref[...].astype(bf16)
            vb_s[...] = v_ref[...].astype(bf16)

        qb[...] = q_ref[...].astype(bf16)
        o_ref[...] = jnp.zeros_like(o_ref)
        c_ref[...] = jnp.zeros_like(c_ref)
        car_ref[...] = jnp.zeros_like(car_ref)
        nkb = (i + 1) * n_diag
        rows, cols, slot, upper, _ = _sb_tiles(i)

        def make_step(masked):
            def step(jj, carry):
                kb = nkb - 1 - jj
                off = pl.multiple_of(kb * TK, TK)
                hs = range(N_HEADS)
                sls = [slice(h * HEAD_DIM, (h + 1) * HEAD_DIM) for h in hs]
                sc = [_sb_scores(qb[:, sls[h]], kb_s[pl.ds(off, TK), sls[h]], kb, rows, cols, masked) for h in hs]
                suf = [_suffix_sum(sc[h][2], upper) for h in hs]
                cs = [c_ref[h] for h in hs]
                es = [jnp.exp(sc[h][1] + suf[h] + cs[h][:, :1]) for h in hs]
                if masked:
                    es = [jnp.where(sc[h][0], es[h], 0.0) for h in hs]
                pv = [_dot(es[h].astype(bf16), vb_s[pl.ds(off, TK), sls[h]]) for h in hs]
                for h in hs:
                    o_ref[:, sls[h]] += pv[h]
                    car_ref[h] = jnp.where(slot == kb, cs[h], car_ref[h])
                    c_ref[h] = cs[h] + jnp.sum(sc[h][2], axis=1, keepdims=True)
                return carry
            return step

        lax.fori_loop(0, n_diag, make_step(True), 0)
        lax.fori_loop(n_diag, nkb, make_step(False), 0)
        z = z_ref[...]
        y_ref[...] = (o_ref[...] * (z * jax.nn.sigmoid(z))).astype(bf16)

    cb = col0 * HEAD_DIM // dh
    qspec = lambda k: pl.BlockSpec((TQ, dh), lambda i: (i, cb + k))
    kspec = lambda k: pl.BlockSpec((S, dh), lambda i: (0, cb + k))
    return pl.pallas_call(
        body, name="sb_fwd", grid=(S // TQ,),
        in_specs=[qspec(0), kspec(1), kspec(2), qspec(3), pl.BlockSpec(memory_space=pl.ANY),
                  pl.BlockSpec(memory_space=pl.ANY)],
        out_specs=[pl.BlockSpec((TQ, dh), lambda i: (i, A_GROUPS * HEAD_DIM // dh)),
                   pl.BlockSpec((TQ, dh), lambda i: (i, 0)),
                   pl.BlockSpec((N_HEADS, TQ, CARRY_LANES), lambda i: (0, i, 0))],
        out_shape=[SDS((S, D), bf16), SDS((S, dh), f32), SDS((N_HEADS, S, CARRY_LANES), f32)],
        input_output_aliases={4: 0},
        scratch_shapes=[pltpu.VMEM((TQ, dh), bf16), pltpu.VMEM((S, dh), bf16), pltpu.VMEM((S, dh), bf16),
                        pltpu.VMEM((N_HEADS, TQ, CARRY_LANES), f32)],
        compiler_params=_params(),
    )(proj, proj, proj, proj, y_prev, after)


def _sb_bwd(proj, o, car, dy, dproj_prev, col0, after):
    S = proj.shape[0]
    n_i = S // TQ
    dh = N_HEADS * HEAD_DIM
    n_diag = TQ // TK
    cb = col0 * HEAD_DIM // dh
    scale = 1.0 / math.sqrt(HEAD_DIM)

    def body(q_ref, k_ref, v_ref, z_ref, o_ref, car_ref, dy_ref, dpp_ref, after_ref,
             dp_ref, qb, kb_s, vb_s, dob, p_ref, dq_acc, dk_acc, dv_acc, st_a, st_b, st_k, st_v):
        i = pl.program_id(0)

        def put(stage_ref, row0, nrows, k):
            pltpu.sync_copy(stage_ref, dp_ref.at[pl.ds(row0, nrows), pl.ds((cb + k) * dh, dh)])

        @pl.when(i == 0)
        def _():
            kb_s[...] = k_ref[...].astype(bf16)
            vb_s[...] = v_ref[...].astype(bf16)
            dk_acc[...] = jnp.zeros_like(dk_acc)
            dv_acc[...] = jnp.zeros_like(dv_acc)

        s, gs = _silu_and_grad(z_ref[...])
        dy = dy_ref[...]
        st_b[...] = (dy * o_ref[...] * gs).astype(bf16)
        dob[...] = (dy * s).astype(bf16)
        qb[...] = q_ref[...].astype(bf16)
        p_ref[...] = jnp.zeros_like(p_ref)
        dq_acc[...] = jnp.zeros_like(dq_acc)
        nkb = (i + 1) * n_diag
        rows, cols, slot, upper, lower = _sb_tiles(i)

        def make_step(masked):
            def step(kb, carry):
                off = pl.multiple_of(kb * TK, TK)
                hs = range(N_HEADS)
                sls = [slice(h * HEAD_DIM, (h + 1) * HEAD_DIM) for h in hs]
                qs = [qb[:, sls[h]] for h in hs]
                ks = [kb_s[pl.ds(off, TK), sls[h]] for h in hs]
                dos = [dob[:, sls[h]] for h in hs]
                sc = [_sb_scores(qs[h], ks[h], kb, rows, cols, masked) for h in hs]
                da = [_dot(dos[h], vb_s[pl.ds(off, TK), sls[h]], NT) for h in hs]
                suf = [_suffix_sum(sc[h][2], upper) for h in hs]
                onehot = slot == kb
                cs = [jnp.sum(jnp.where(onehot, car_ref[h], 0.0), axis=1, keepdims=True) for h in hs]
                es = [jnp.exp(sc[h][1] + suf[h] + cs[h]) for h in hs]
                if masked:
                    es = [jnp.where(sc[h][0], es[h], 0.0) for h in hs]
                gs_ = [da[h] * es[h] for h in hs]
                ps = [p_ref[h] for h in hs]
                pre = [_suffix_sum(gs_[h], lower) + ps[h][:, :1] for h in hs]
                dzs = []
                for h in hs:
                    beta = jnp.exp(sc[h][1])
                    dzz = gs_[h] * (1.0 - beta) - beta * pre[h]
                    if masked:
                        dzz = jnp.where(sc[h][0], dzz, 0.0)
                    dzs.append((dzz * scale).astype(bf16))
                dqs = [_dot(dzs[h], ks[h]) for h in hs]
                dks = [_dot(dzs[h], qs[h], TN) for h in hs]
                dvs = [_dot(es[h].astype(bf16), dos[h], TN) for h in hs]
                for h in hs:
                    dq_acc[:, sls[h]] += dqs[h]
                    dk_acc[pl.ds(off, TK), sls[h]] += dks[h]
                    dv_acc[pl.ds(off, TK), sls[h]] += dvs[h]
                    p_ref[h] = ps[h] + jnp.sum(gs_[h], axis=1, keepdims=True)
                return carry
            return step

        lax.fori_loop(0, nkb - n_diag, make_step(False), 0)
        lax.fori_loop(nkb - n_diag, nkb, make_step(True), 0)
        st_a[...] = dq_acc[...].astype(bf16)
        row0 = pl.multiple_of(i * TQ, TQ)
        put(st_a, row0, TQ, 0)
        put(st_b, row0, TQ, 3)

        @pl.when(i == n_i - 1)
        def _():
            st_k[...] = dk_acc[...].astype(bf16)
            st_v[...] = dv_acc[...].astype(bf16)
            put(st_k, 0, S, 1)
            put(st_v, 0, S, 2)

    qspec = lambda k: pl.BlockSpec((TQ, dh), lambda i: (i, cb + k))
    kspec = lambda k: pl.BlockSpec((S, dh), lambda i: (0, cb + k))
    return pl.pallas_call(
        body, name="sb_bwd", grid=(n_i,),
        in_specs=[qspec(0), kspec(1), kspec(2), qspec(3),
                  pl.BlockSpec((TQ, dh), lambda i: (i, 0)),
                  pl.BlockSpec((N_HEADS, TQ, CARRY_LANES), lambda i: (0, i, 0)),
                  pl.BlockSpec((TQ, dh), lambda i: (i, A_GROUPS * HEAD_DIM // dh)),
                  pl.BlockSpec(memory_space=pl.ANY), pl.BlockSpec(memory_space=pl.ANY)],
        out_specs=pl.BlockSpec(memory_space=pl.ANY),
        out_shape=SDS(dproj_prev.shape, bf16),
        input_output_aliases={7: 0},
        scratch_shapes=[pltpu.VMEM((TQ, dh), bf16), pltpu.VMEM((S, dh), bf16), pltpu.VMEM((S, dh), bf16),
                        pltpu.VMEM((TQ, dh), bf16), pltpu.VMEM((N_HEADS, TQ, CARRY_LANES), f32), pltpu.VMEM((TQ, dh), f32),
                        pltpu.VMEM((S, dh), f32), pltpu.VMEM((S, dh), f32),
                        pltpu.VMEM((TQ, dh), bf16), pltpu.VMEM((TQ, dh), bf16),
                        pltpu.VMEM((S, dh), bf16), pltpu.VMEM((S, dh), bf16)],
        compiler_params=_params(56),
    )(proj, proj, proj, proj, o, car, dy, dproj_prev, after)


def _mem_kv(mem, mg, l, w_kv_g):
    M, D = mem.shape
    rb, n = w_kv_g.shape[1], w_kv_g.shape[2]

    def body(m_ref, g_ref, w_ref, kv_ref):
        mv = m_ref[...]
        r = lax.rsqrt(jnp.mean(mv * mv, axis=-1, keepdims=True) + EPS)
        mh = (mv * r * g_ref[...]).astype(bf16)
        kv_ref[...] = _dot(mh, w_ref[...].reshape(N_DEV * rb, n))

    return pl.pallas_call(
        body, name="mem_kv", grid=(1,),
        in_specs=[pl.BlockSpec((M, D), lambda i: (0, 0)), _layer_spec(mg, l),
                  pl.BlockSpec((N_DEV, rb, n), lambda i: (0, 0, 0))],
        out_specs=pl.BlockSpec((M, n), lambda i: (0, 0)),
        out_shape=SDS((M, n), f32), compiler_params=_params(),
    )(mem, mg, w_kv_g)


def _xattn_head(q_ref, kv_ref, qg, kg, h):
    dc = N_HEADS * HEAD_DIM
    sl = slice(h * HEAD_DIM, (h + 1) * HEAD_DIM)
    qh = q_ref[:, sl]
    rq = lax.rsqrt(jnp.mean(qh * qh, axis=-1, keepdims=True) + EPS)
    qhat = qh * rq
    qn = (qhat * qg).astype(bf16)
    kh = kv_ref[:, sl]
    rk = lax.rsqrt(jnp.mean(kh * kh, axis=-1, keepdims=True) + EPS)
    kn = (kh * rk * kg).astype(bf16)
    vh = kv_ref[:, dc + h * HEAD_DIM:dc + (h + 1) * HEAD_DIM].astype(bf16)
    s = _dot(qn, kn, NT) * (1.0 / math.sqrt(HEAD_DIM))
    e = jnp.exp(s - jnp.max(s, axis=-1, keepdims=True))
    p = e / jnp.sum(e, axis=-1, keepdims=True)
    o = _dot(p.astype(bf16), vh)
    return sl, rq, qhat, qn, kn, vh, p, o


def _xattn_fwd(proj, kv, qg, kg, l, y_prev, col0, tq=512):
    S = proj.shape[0]
    D = y_prev.shape[1]
    dc = N_HEADS * HEAD_DIM
    M = kv.shape[0]

    def body(q_ref, z_ref, kv_ref, qg_ref, kg_ref, yp_ref, y_ref):
        for h in range(N_HEADS):
            sl, _, _, _, _, _, _, o = _xattn_head(q_ref, kv_ref, qg_ref[...], kg_ref[...], h)
            z = z_ref[:, sl]
            y_ref[:, sl] = (o * (z * jax.nn.sigmoid(z))).astype(bf16)

    full = lambda shp: pl.BlockSpec(shp, lambda i: (0,) * len(shp))
    return pl.pallas_call(
        body, name="xattn_fwd", grid=(S // tq,),
        in_specs=[pl.BlockSpec((tq, dc), lambda i: (i, col0)), pl.BlockSpec((tq, dc), lambda i: (i, col0 + 1)),
                  full((M, 2 * dc)), _layer_spec(qg, l), _layer_spec(kg, l), pl.BlockSpec(memory_space=pl.ANY)],
        out_specs=pl.BlockSpec((tq, dc), lambda i: (i, D // dc - 1)),
        out_shape=SDS((S, D), bf16), input_output_aliases={5: 0}, compiler_params=_params(),
    )(proj, proj, kv, qg, kg, y_prev)


def _xattn_bwd(proj, kv, qg, kg, l, dy, dproj_prev, col0, tq=512):
    S = proj.shape[0]
    D = dy.shape[1]
    dc = N_HEADS * HEAD_DIM
    M = kv.shape[0]

    def body(q_ref, z_ref, kv_ref, qg_ref, kg_ref, dy_ref, dpp_ref, dp_ref, dkn_ref, dv_ref, dqg_ref):
        @pl.when(pl.program_id(0) == 0)
        def _():
            dkn_ref[...] = jnp.zeros_like(dkn_ref)
            dv_ref[...] = jnp.zeros_like(dv_ref)
            dqg_ref[...] = jnp.zeros_like(dqg_ref)

        qg = qg_ref[...]
        for h in range(N_HEADS):
            sl, rq, qhat, qn, kn, vh, p, o = _xattn_head(q_ref, kv_ref, qg, kg_ref[...], h)
            s, gs = _silu_and_grad(z_ref[:, sl])
            dyh = dy_ref[:, sl]
            dp_ref[:, dc + h * HEAD_DIM:dc + (h + 1) * HEAD_DIM] = (dyh * o * gs).astype(bf16)
            dob = (dyh * s).astype(bf16)
            dpr = _dot(dob, vh, NT)
            dv_ref[:, sl] += _dot(p.astype(bf16), dob, TN)
            ds = (p * (dpr - jnp.sum(p * dpr, axis=-1, keepdims=True)) * (1.0 / math.sqrt(HEAD_DIM))).astype(bf16)
            dqn = _dot(ds, kn)
            dkn_ref[:, sl] += _dot(ds, qn, TN)
            dqg_ref[...] += jnp.sum(dqn * qhat, axis=0, keepdims=True)
            dqhat = dqn * qg
            dp_ref[:, sl] = (rq * (dqhat - qhat * jnp.mean(dqhat * qhat, axis=-1, keepdims=True))).astype(bf16)

    full = lambda shp: pl.BlockSpec(shp, lambda i: (0,) * len(shp))
    return pl.pallas_call(
        body, name="xattn_bwd", grid=(S // tq,),
        in_specs=[pl.BlockSpec((tq, dc), lambda i: (i, col0)), pl.BlockSpec((tq, dc), lambda i: (i, col0 + 1)),
                  full((M, 2 * dc)), _layer_spec(qg, l), _layer_spec(kg, l),
                  pl.BlockSpec((tq, dc), lambda i: (i, D // dc - 1)), pl.BlockSpec(memory_space=pl.ANY)],
        out_specs=[pl.BlockSpec((tq, 2 * dc), lambda i: (i, col0 // 2)), full((M, dc)), full((M, dc)),
                   full((1, HEAD_DIM))],
        out_shape=[SDS(dproj_prev.shape, bf16), SDS((M, dc), f32), SDS((M, dc), f32), SDS((1, HEAD_DIM), f32)],
        input_output_aliases={6: 0}, compiler_params=_params(),
    )(proj, proj, kv, qg, kg, dy, dproj_prev)


def _mem_bwd(mem, mg, kg, l, kv, dkn, dv, w_kv_g):
    M, D = mem.shape
    rb, n = w_kv_g.shape[1], w_kv_g.shape[2]
    dc = n // 2

    def body(m_ref, g_ref, kv_ref, dkn_ref, dv_ref, kg_ref, w_ref, dw_ref, dmg_ref, dkg_ref, dkv_ref):
        mv = m_ref[...]
        r = lax.rsqrt(jnp.mean(mv * mv, axis=-1, keepdims=True) + EPS)
        mhat = mv * r
        mh = (mhat * g_ref[...]).astype(bf16)
        kg = kg_ref[...]
        dkg = jnp.zeros((1, HEAD_DIM), f32)
        for h in range(N_HEADS):
            sl = slice(h * HEAD_DIM, (h + 1) * HEAD_DIM)
            kh = kv_ref[:, sl]
            rk = lax.rsqrt(jnp.mean(kh * kh, axis=-1, keepdims=True) + EPS)
            khat = kh * rk
            dkn_h = dkn_ref[:, sl]
            dkg = dkg + jnp.sum(dkn_h * khat, axis=0, keepdims=True)
            dkhat = dkn_h * kg
            dkv_ref[:, sl] = (rk * (dkhat - khat * jnp.mean(dkhat * khat, axis=-1, keepdims=True))).astype(bf16)
        dkv_ref[:, dc:] = dv_ref[...].astype(bf16)
        dkg_ref[...] = dkg
        dkv = dkv_ref[...]
        dw_ref[...] = _dot(mh, dkv, TN).astype(bf16).reshape(N_DEV, rb, n)
        dmh = _dot(dkv, w_ref[...].reshape(N_DEV * rb, n), NT)
        dmg_ref[...] = jnp.sum(dmh * mhat, axis=0, keepdims=True)

    full = lambda shp: pl.BlockSpec(shp, lambda i: (0,) * len(shp))
    wspec = full((N_DEV, rb, n))
    return pl.pallas_call(
        body, name="mem_bwd", grid=(1,),
        in_specs=[full((M, D)), _layer_spec(mg, l), full((M, n)), full((M, dc)), full((M, dc)), _layer_spec(kg, l), wspec],
        out_specs=[wspec, full((1, D)), full((1, HEAD_DIM))],
        out_shape=[SDS((N_DEV, rb, n), bf16), SDS((1, D), f32), SDS((1, HEAD_DIM), f32)],
        scratch_shapes=[pltpu.VMEM((M, n), bf16)], compiler_params=_params(),
    )(mem, mg, kv, dkn, dv, kg, w_kv_g)


SMALL = ("norm_g", "sgu_ln_g", "sgu_ln_b", "sgu_w", "sgu_b", "mem_norm_g", "q_norm_g", "k_norm_g")


def _small_rows(like):
    rows = [math.prod(like[n].shape) // 128 for n in SMALL]
    offs = [0]
    for r in rows:
        offs.append(offs[-1] + -(-r // 8) * 8)
    return rows, offs


def _pack_small(parts, offs):
    pieces = []
    for k, n in enumerate(SMALL):
        a = parts[n].reshape(-1, 128)
        pieces.append(jnp.pad(a, ((0, offs[k + 1] - offs[k] - a.shape[0]), (0, 0))))
    return jnp.concatenate(pieces)


def kernel(x, mem, norm_g, w_in, sgu_ln_g, sgu_ln_b, sgu_w, sgu_b, mem_norm_g, w_mem_kv, q_norm_g, k_norm_g, w_out, loss_target, m_norm_g, m_w_in, m_sgu_ln_g, m_sgu_ln_b, m_sgu_w, m_sgu_b, m_mem_norm_g, m_w_mem_kv, m_q_norm_g, m_k_norm_g, m_w_out, v_norm_g, v_w_in, v_sgu_ln_g, v_sgu_ln_b, v_sgu_w, v_sgu_b, v_mem_norm_g, v_w_mem_kv, v_q_norm_g, v_k_norm_g, v_w_out):
    L, D, wc = w_in.shape
    S = x.shape[1]
    da = D // 2
    xs = x.reshape(S, D)
    mems = mem.reshape(mem.shape[1], D)
    tgt = loss_target.reshape(S, D)
    stacked = lambda a: a.reshape(a.shape[0], 1, -1)
    ng, lng, lnb, mg, qg, kg = map(stacked, (norm_g, sgu_ln_g, sgu_ln_b, mem_norm_g, q_norm_g, k_norm_g))
    b_t = jnp.swapaxes(sgu_b, 1, 2)
    sb_col, xa_col = 3 * da // HEAD_DIM, (3 * da + D) // (D // 4)

    ax, ay, ac = lax.axis_index("x"), lax.axis_index("y"), lax.axis_index("c")
    ids = jnp.stack([4 * ax + 2 * ay + ac, 2 * ax + ay, ac]).astype(jnp.int32)
    w_in0_b = _cast_into_slot("cast_w_in", w_in, 0, 512, ids, ids)
    first = _gather2_start("gather_w_in0", [w_in0_b], ids)
    late = first["token"]
    w_b = [(w_in0_b if l == 0 else _cast_into_slot("cast_w_in", w_in, l, 512, ids, late),
            _cast_into_slot("cast_w_kv", w_mem_kv, l, 256, ids, late),
            _cast_into_slot("cast_w_out", w_out, l, 256, ids, late)) for l in range(L)]
    in_fwd = _gather2_forward("gather_w_in0_forward",
                              _split_wait(first, *[a for wl in w_b for a in wl if a is not w_in0_b]), ids)

    acts = []
    xl = xs
    for l in range(L):
        (w_in_g,) = _split_wait(in_fwd, xl if l else in_fwd["token"])
        rest = _gather2_start(f"gather_w_rest{l}", [w_b[l][1], w_b[l][2]], w_in_g)
        proj, h = _rms_proj(xl, ng, l, w_in_g, rest["token"])
        rest_fwd = _gather2_forward(f"gather_w_rest{l}_forward", _split_wait(rest, proj), proj)
        order = rest_fwd["token"]
        if l + 1 < L:
            nxt = _gather2_start(f"gather_w_in{l + 1}", [w_b[l + 1][0]], order)
            order = nxt["token"]
        y = _sgu_fwd(proj, lng, lnb, sgu_w, b_t, l, order)
        y, o_b, car = _sb_fwd(proj, y, sb_col, order)
        w_kv_g, w_out_g = _split_wait(rest_fwd, o_b)
        kv = _mem_kv(mems, mg, l, w_kv_g)
        y = _xattn_fwd(proj, kv, qg, kg, l, y, xa_col)
        order = kv
        if l + 1 < L:
            in_fwd = _gather2_forward(f"gather_w_in{l + 1}_forward", _split_wait(nxt, y), y)
            order = in_fwd["token"]
        x_next = _out_proj(xl, y, w_out_g, order)
        acts.append((xl, proj, h, y, o_b, car, kv, w_in_g, w_kv_g, w_out_g))
        xl = x_next

    dx, dxb, loss_part = _loss_and_grad(xl, tgt, 512)
    loss = lax.psum(loss_part[0, 0], ("x", "y", "c"))

    weights = dict(norm_g=norm_g, sgu_ln_g=sgu_ln_g, sgu_ln_b=sgu_ln_b, sgu_w=sgu_w, sgu_b=sgu_b,
                   mem_norm_g=mem_norm_g, q_norm_g=q_norm_g, k_norm_g=k_norm_g)
    moms_m = dict(norm_g=m_norm_g, sgu_ln_g=m_sgu_ln_g, sgu_ln_b=m_sgu_ln_b, sgu_w=m_sgu_w, sgu_b=m_sgu_b,
                  mem_norm_g=m_mem_norm_g, q_norm_g=m_q_norm_g, k_norm_g=m_k_norm_g)
    moms_v = dict(norm_g=v_norm_g, sgu_ln_g=v_sgu_ln_g, sgu_ln_b=v_sgu_ln_b, sgu_w=v_sgu_w, sgu_b=v_sgu_b,
                  mem_norm_g=v_mem_norm_g, q_norm_g=v_q_norm_g, k_norm_g=v_k_norm_g)
    small_rows, small_offs = _small_rows(weights)
    head_rows = D // 128
    assert SMALL[0] == "norm_g" and head_rows % 8 == 0

    seconds = {}
    pending = None
    adam = {"w_out": None, "w_mem_kv": None, "w_in": None}

    def update(lu, order):
        (r_out,) = _scatter2_finish(seconds[f"g_out{lu}"], order)
        adam["w_out"] = _sum_adam("adam_w_out", r_out, w_out, m_w_out, v_w_out, lu, adam["w_out"], 128, ids)
        r_kv, r_in = _scatter2_finish(seconds[f"g_rest{lu}"], adam["w_out"][0])
        adam["w_mem_kv"] = _sum_adam("adam_w_kv", r_kv, w_mem_kv, m_w_mem_kv, v_w_mem_kv, lu, adam["w_mem_kv"], 256, ids)
        adam["w_in"] = _sum_adam("adam_w_in", r_in, w_in, m_w_in, v_w_in, lu, adam["w_in"], 256, ids)
        return adam["w_in"][0]
    small = {n: [None] * L for n in SMALL}
    for l in reversed(range(L)):
        xl, proj, h, y, o_b, car, kv, w_in_g, w_kv_g, w_out_g = acts[l]
        dy = _out_bwd_dy(dxb, w_out_g)
        order = dy
        if pending is not None:
            seconds[pending[0]] = _scatter2_second_level(pending[0], pending[1], dy, ids)
            order = seconds[pending[0]]["token"]
        g_out = _tn_grad("out_bwd_dw", y, dxb, 512, 512, True, order)
        first = _scatter2_pair_start(f"scatter_g_out{l}_pair", [g_out], ids)
        dproj, d_sw, d_sb, d_lg, d_lb = _sgu_bwd(proj, dy, lng, lnb, sgu_w, b_t, l, first["token"])
        seconds[f"g_out{l}"] = _scatter2_second_level(f"g_out{l}", first, dproj, ids)
        dproj = _sb_bwd(proj, o_b, car, dy, dproj, sb_col, seconds[f"g_out{l}"]["token"])
        dproj, dkn, dv, d_qg = _xattn_bwd(proj, kv, qg, kg, l, dy, dproj, xa_col)
        g_kv, d_mg, d_kg = _mem_bwd(mems, mg, kg, l, kv, dkn, dv, w_kv_g)
        for n, val in (("sgu_ln_g", d_lg), ("sgu_ln_b", d_lb), ("sgu_w", d_sw), ("sgu_b", d_sb[:, :A_GROUPS].T),
                       ("mem_norm_g", d_mg), ("q_norm_g", d_qg), ("k_norm_g", d_kg)):
            small[n][l] = val.reshape(-1)
        order = d_kg
        if l == 0:
            small["norm_g"][0] = jnp.zeros_like(small["norm_g"][1])
            part = _pack_small({n: jnp.stack(small[n]) for n in SMALL}, small_offs)
            tail = _gather2_start("gather_small_tail", [_into_slot("small_tail_slot", part[head_rows:], ids)], ids)
            order = tail["token"]
        g_in_l = _tn_grad("in_bwd_dw", h, dproj, 1024, wc, False, order)
        first = _scatter2_pair_start(f"scatter_g_rest{l}_pair", [g_kv, g_in_l], ids)
        order = first["token"]
        pending = (f"g_rest{l}", first)
        if l == 0:
            tail_fwd = _gather2_forward("gather_small_tail_forward", _split_wait(tail, g_in_l), g_in_l)
            order = tail_fwd["token"]
            for lu in reversed(range(1, L)):
                order = update(lu, order)
            seconds[pending[0]] = _scatter2_second_level(pending[0], pending[1], order, ids)
            order = seconds[pending[0]]["token"]
        dh = _in_bwd_dh(dproj, w_in_g, order)
        dx, dxb, d_ng = _rms_bwd(dh, xl, ng, l, dx, order)
        small["norm_g"][l] = d_ng.reshape(-1)
    head = _gather2_start("gather_small_head",
                          [_into_slot("small_head_slot", small["norm_g"][0].reshape(head_rows, 128), ids)], ids)

    (r_out,) = _scatter2_finish(seconds["g_out0"], head["token"])
    adam["w_out"] = _sum_adam("adam_w_out", r_out, w_out, m_w_out, v_w_out, 0, adam["w_out"], 128, ids)
    head_fwd = _gather2_forward("gather_small_head_forward", _split_wait(head, adam["w_out"][0]), ids)
    (r_tail,) = _split_wait(tail_fwd, adam["w_out"][0])
    (r_head,) = _split_wait(head_fwd, r_tail)
    as128 = lambda d: [d[n].reshape(-1, 128) for n in SMALL]
    sm = _small_sum_adam(r_head, r_tail, as128(weights), as128(moms_m), as128(moms_v), small_offs)
    r_kv, r_in = _scatter2_finish(seconds["g_rest0"], sm[0])
    adam["w_mem_kv"] = _sum_adam("adam_w_kv", r_kv, w_mem_kv, m_w_mem_kv, v_w_mem_kv, 0, adam["w_mem_kv"], 256, ids)
    adam["w_in"] = _sum_adam("adam_w_in", r_in, w_in, m_w_in, v_w_in, 0, adam["w_in"], 256, ids)
    res = dict(adam)
    for p, n in enumerate(SMALL):
        res[n] = [sm[k * len(SMALL) + p].reshape(weights[n].shape) for k in range(4)]

    order = ("norm_g", "w_in", "sgu_ln_g", "sgu_ln_b", "sgu_w", "sgu_b", "mem_norm_g", "w_mem_kv", "q_norm_g",
             "k_norm_g", "w_out")
    outs = [loss, dx.reshape(x.shape)]
    for k in range(4):
        outs += [res[n][k] for n in order]
    return tuple(outs)
```

```python
import functools
import math

import jax
import jax.numpy as jnp
from jax import lax
from jax.experimental import pallas as pl
from jax.experimental.pallas import tpu as pltpu

f32 = jnp.float32
bf16 = jnp.bfloat16
SDS = jax.ShapeDtypeStruct

N_DEV = 8
EPS = 1e-6
CHUNK = 128
A_GROUPS = 8
HEAD_DIM = 128
N_HEADS = 4
TQ = 256
TK = 256
CARRY_LANES = 128
ADAM_LR, ADAM_B1, ADAM_B2, ADAM_EPS, ADAM_WD, ADAM_STEP = 0.001, 0.9, 0.999, 1e-08, 0.01, 10
MIB = 1024 * 1024

NT = (((1,), (1,)), ((), ()))
TN = (((0,), (0,)), ((), ()))


def _params(vmem_mib=48):
    return pltpu.CompilerParams(vmem_limit_bytes=vmem_mib * MIB)


def _gelu_and_grad(x):
    e = lax.erf(x * (1.0 / math.sqrt(2.0)))
    cdf = 0.5 * (1.0 + e)
    pdf = jnp.exp(-0.5 * x * x) * (1.0 / math.sqrt(2.0 * math.pi))
    return x * cdf, cdf + x * pdf


def _gelu(x):
    return 0.5 * x * (1.0 + lax.erf(x * (1.0 / math.sqrt(2.0))))


def _silu_and_grad(z):
    sg = jax.nn.sigmoid(z)
    return z * sg, sg * (1.0 + z * (1.0 - sg))


def _layer_spec(stacked, l):
    rest = stacked.shape[1:]
    return pl.BlockSpec((None,) + rest, lambda *idx: (l,) + (0,) * len(rest))


def _dot(a, b, dims=None):
    if dims is None:
        return jnp.dot(a, b, preferred_element_type=f32)
    return lax.dot_general(a, b, dims, preferred_element_type=f32)


_HBM = pl.BlockSpec(memory_space=pltpu.HBM)
_SEM = pl.BlockSpec(memory_space=pltpu.SEMAPHORE)
_EFFECT = pltpu.SideEffectType.DATAFLOW_SIDE_EFFECTING


def _split_start(name, bufs, n_remote, n_local, build, after):
    nb = len(bufs)

    def body(*refs):
        token = refs[-1]
        locals_, remotes = build(refs[:nb], *refs[nb + 1:nb + 4])
        for cp in locals_ + remotes:
            cp.start()
        token[...] = jnp.zeros_like(token)

    hbm = lambda a: pltpu.with_memory_space_constraint(a, pltpu.HBM)
    outs = pl.pallas_call(
        body, name=name,
        out_shape=(pltpu.SemaphoreType.DMA((n_remote,)), pltpu.SemaphoreType.DMA((n_remote,)),
                   pltpu.SemaphoreType.DMA((max(n_local, 1),)),
                   *[pltpu.HBM(b.shape, b.dtype) for b in bufs], SDS((8, 128), f32)),
        in_specs=[_HBM] * nb + [pl.BlockSpec(memory_space=pl.ANY)],
        out_specs=(_SEM, _SEM, _SEM, *[_HBM] * nb, pl.BlockSpec(memory_space=pltpu.VMEM)),
        input_output_aliases={k: 3 + k for k in range(nb)},
        compiler_params=pltpu.CompilerParams(has_side_effects=_EFFECT),
    )(*[hbm(b) for b in bufs], after)
    return dict(name=name, build=build, sems=outs[:3], bufs=outs[3:3 + nb], token=outs[-1])


def _split_wait(handle, *after):
    build, bufs = handle["build"], handle["bufs"]
    nb = len(bufs)

    def body(*refs):
        locals_, remotes = build(refs[:nb], *refs[nb:nb + 3])
        for cp in remotes:
            cp.wait_recv()
        for cp in remotes:
            cp.wait_send()
        for cp in locals_:
            cp.wait()

    outs = pl.pallas_call(
        body, name=handle["name"] + "_wait",
        out_shape=tuple(pltpu.HBM(b.shape, b.dtype) for b in bufs),
        in_specs=[_HBM] * nb + [_SEM] * 3 + [pl.BlockSpec(memory_space=pl.ANY)] * len(after),
        out_specs=tuple([_HBM] * nb),
        input_output_aliases={k: k for k in range(nb)},
        compiler_params=pltpu.CompilerParams(has_side_effects=_EFFECT),
    )(*bufs, *handle["sems"], *after)
    return list(outs)


def _remote(src, dst, send_sems, recv_sems, k, to):
    return pltpu.make_async_remote_copy(src_ref=src, dst_ref=dst, send_sem=send_sems.at[k], recv_sem=recv_sems.at[k],
                                        device_id=to, device_id_type=pl.DeviceIdType.MESH)


def _other_chips(x, y):
    return [(1 - x, y), (x, 1 - y), (1 - x, 1 - y)]


def _all_peers(x, y, c):
    return [(1 - x if m & 4 else x, 1 - y if m & 2 else y, 1 - c if m & 1 else c) for m in range(1, N_DEV)]


def _gather1_start(name, lands, after):
    def build(refs, send, recv, loc):
        x, y, c = lax.axis_index("x"), lax.axis_index("y"), lax.axis_index("c")
        me = 4 * x + 2 * y + c
        return [], [_remote(d.at[me], d.at[me], send, recv, 7 * a + k, peer)
                    for a, d in enumerate(refs) for k, peer in enumerate(_all_peers(x, y, c))]

    return _split_start(name, list(lands), 7 * len(lands), 0, build, after)


def _scatter1_start(name, srcs, after):
    n = len(srcs)

    def build(refs, send, recv, loc):
        x, y, c = lax.axis_index("x"), lax.axis_index("y"), lax.axis_index("c")
        me = 4 * x + 2 * y + c
        return [], [_remote(refs[a].at[4 * px + 2 * py + pc], refs[n + a].at[me], send, recv, 7 * a + k, (px, py, pc))
                    for a in range(n) for k, (px, py, pc) in enumerate(_all_peers(x, y, c))]

    return _split_start(name, list(srcs) + [lax.empty(s.shape, s.dtype) for s in srcs], 7 * n, 0, build, after)


def _gather2_start(name, lands, after):
    def build(refs, send, recv, loc):
        x, y, c = lax.axis_index("x"), lax.axis_index("y"), lax.axis_index("c")
        me = 4 * x + 2 * y + c
        remotes = []
        for a, d in enumerate(refs):
            remotes.append(_remote(d.at[me], d.at[me], send, recv, 4 * a, (x, y, 1 - c)))
            remotes += [_remote(d.at[me], d.at[me], send, recv, 4 * a + 1 + k, (px, py, c))
                        for k, (px, py) in enumerate(_other_chips(x, y))]
        return [], remotes

    return _split_start(name, list(lands), 4 * len(lands), 0, build, after)


def _gather2_forward(name, lands, after):
    n = len(lands)

    def build(refs, send, recv, loc):
        x, y, c = lax.axis_index("x"), lax.axis_index("y"), lax.axis_index("c")
        slots = [4 * px + 2 * py + c for px, py in _other_chips(x, y)]
        return [], [_remote(d.at[sl], d.at[sl], send, recv, 3 * a + k, (x, y, 1 - c))
                    for a, d in enumerate(refs) for k, sl in enumerate(slots)]

    return _split_start(name, list(lands), 3 * n, 0, build, after)


def _scatter2_pair_start(name, srcs, after):
    n = len(srcs)

    def build(refs, send, recv, loc):
        x, y, c = lax.axis_index("x"), lax.axis_index("y"), lax.axis_index("c")
        return [], [_remote(refs[a].at[2 * q + 1 - c], refs[n + a].at[q], send, recv, 4 * a + q, (x, y, 1 - c))
                    for a in range(n) for q in range(4)]

    lands = [lax.empty((4,) + s.shape[1:], s.dtype) for s in srcs]
    return _split_start(name, list(srcs) + lands, 4 * n, 0, build, after)


def _scatter2_chip_start(name, pairs, after):
    n = len(pairs)

    def build(refs, send, recv, loc):
        x, y, c = lax.axis_index("x"), lax.axis_index("y"), lax.axis_index("c")
        return [], [_remote(refs[a].at[2 * px + py], refs[n + a].at[2 * x + y], send, recv, 3 * a + k, (px, py, c))
                    for a in range(n) for k, (px, py) in enumerate(_other_chips(x, y))]

    return _split_start(name, list(pairs) + [lax.empty(p.shape, p.dtype) for p in pairs], 3 * n, 0, build, after)


def _pair_sum(name, src, theirs, ids):
    _, R, C = theirs.shape
    tr = min(R, 1024)

    def body(ids_ref, a_ref, b_ref, o_ref):
        o_ref[...] = (a_ref[...].astype(f32) + b_ref[...].astype(f32)).astype(bf16)

    spec = pl.BlockSpec((None, tr, C), lambda q, i, ids: (q, i, 0))
    return pl.pallas_call(
        body, name=name,
        grid_spec=pltpu.PrefetchScalarGridSpec(
            num_scalar_prefetch=1, grid=(4, R // tr),
            in_specs=[pl.BlockSpec((None, tr, C), lambda q, i, ids: (2 * q + ids[2], i, 0)), spec], out_specs=spec),
        out_shape=SDS(theirs.shape, bf16), compiler_params=_params(),
    )(ids, src, theirs)


def _scatter2_second_level(name, first, after, ids):
    outs = _split_wait(first, after)
    n = len(outs) // 2
    pairs = [_pair_sum(f"pair_sum_{name}{a}", outs[a], outs[n + a], ids) for a in range(n)]
    return _scatter2_chip_start(f"scatter_{name}_chip", pairs, ids)


def _scatter_finish(second, after):
    outs = _split_wait(second, after)
    n = len(outs) // 2
    return [(outs[a], outs[n + a]) for a in range(n)]


def _cast_into_slot(name, w, l, tr, ids, after):
    _, R, C = w.shape

    def body(ids_ref, w_ref, after_ref, o_ref):
        o_ref[...] = w_ref[...].astype(bf16)

    return pl.pallas_call(
        body, name=name,
        grid_spec=pltpu.PrefetchScalarGridSpec(
            num_scalar_prefetch=1, grid=(R // tr,),
            in_specs=[pl.BlockSpec((None, tr, C), lambda i, ids: (l, i, 0)), pl.BlockSpec(memory_space=pl.ANY)],
            out_specs=pl.BlockSpec((None, tr, C), lambda i, ids: (ids[0], i, 0))),
        out_shape=SDS((N_DEV, R, C), bf16), compiler_params=_params(),
    )(ids, w, after)


def _into_slot(name, a, ids):
    R, C = a.shape

    def body(ids_ref, a_ref, o_ref):
        o_ref[...] = a_ref[...]

    return pl.pallas_call(
        body, name=name,
        grid_spec=pltpu.PrefetchScalarGridSpec(
            num_scalar_prefetch=1, grid=(1,),
            in_specs=[pl.BlockSpec((R, C), lambda i, ids: (0, 0))],
            out_specs=pl.BlockSpec((None, R, C), lambda i, ids: (ids[0], 0, 0))),
        out_shape=SDS((N_DEV, R, C), f32), compiler_params=_params(),
    )(ids, a)


def _adam_math(w, g, m, v):
    m2 = ADAM_B1 * m + (1.0 - ADAM_B1) * g
    v2 = ADAM_B2 * v + (1.0 - ADAM_B2) * (g * g)
    m_hat = m2 / (1.0 - ADAM_B1 ** ADAM_STEP)
    v_hat = v2 / (1.0 - ADAM_B2 ** ADAM_STEP)
    delta = -ADAM_LR * (m_hat / (jnp.sqrt(v_hat) + ADAM_EPS) + ADAM_WD * w)
    return delta, m2, v2


def _sum_adam(name, pair_recv, w, m, v, l, prev, tr, ids):
    own, recv = pair_recv
    L, R, C = w.shape
    slots = recv.shape[0]
    mine = 0 if slots == N_DEV else 1

    def body(ids_ref, r_ref, own_ref, w_ref, m_ref, v_ref, *rest):
        g_ref, d_ref, m2_ref, v2_ref = rest[-4:]
        terms = [jnp.where(ids_ref[mine] == q, own_ref[...], r_ref[q]).astype(f32) for q in range(slots)]
        g = terms[0]
        for t in terms[1:]:
            g = g + t
        d, m2, v2 = _adam_math(w_ref[...], g, m_ref[...], v_ref[...])
        g_ref[...] = g
        d_ref[...] = d
        m2_ref[...] = m2
        v2_ref[...] = v2

    wspec = pl.BlockSpec((None, tr, C), lambda i, ids: (l, i, 0))
    in_specs = [pl.BlockSpec((slots, tr, C), lambda i, ids: (0, i, 0)),
                pl.BlockSpec((None, tr, C), lambda i, ids: (ids[mine], i, 0)), wspec, wspec, wspec]
    args = [ids, recv, own, w, m, v]
    aliases = {}
    if prev is not None:
        in_specs += [pl.BlockSpec(memory_space=pl.ANY)] * 4
        args += list(prev)
        aliases = {6 + k: k for k in range(4)}
    return pl.pallas_call(
        body, name=name,
        grid_spec=pltpu.PrefetchScalarGridSpec(num_scalar_prefetch=1, grid=(R // tr,), in_specs=in_specs,
                                               out_specs=[wspec] * 4),
        out_shape=[SDS((L, R, C), f32)] * 4, input_output_aliases=aliases, compiler_params=_params(),
    )(*args)


def _small_sum_adam(recv_head, recv_tail, ws, ms, vs, offs):
    n = len(ws)
    r0 = recv_head.shape[1]

    def body(*refs):
        rh, rt = refs[0], refs[1]
        w_refs, m_refs, v_refs = refs[2:2 + n], refs[2 + n:2 + 2 * n], refs[2 + 2 * n:2 + 3 * n]
        outs = refs[2 + 3 * n:]
        for p in range(n):
            lo, hi = offs[p], offs[p] + ws[p].shape[0]
            pieces = []
            if lo < r0:
                pieces.append((rh, lo, 0, min(hi, r0) - lo))
            if hi > r0:
                pieces.append((rt, max(lo, r0) - r0, max(lo, r0) - lo, hi - max(lo, r0)))
            for src, a, b, cnt in pieces:
                g = src[0, a:a + cnt, :]
                for s in range(1, N_DEV):
                    g = g + src[s, a:a + cnt, :]
                d, m2, v2 = _adam_math(w_refs[p][b:b + cnt, :], g, m_refs[p][b:b + cnt, :], v_refs[p][b:b + cnt, :])
                for k, val in enumerate((g, d, m2, v2)):
                    outs[k * n + p][b:b + cnt, :] = val

    return pl.pallas_call(
        body, name="small_sum_adam", out_shape=[SDS(w.shape, f32) for w in ws] * 4, compiler_params=_params(),
    )(recv_head, recv_tail, *ws, *ms, *vs)


def _loss_and_grad(xf, tgt, tm):
    S, D = xf.shape

    def body(x_ref, t_ref, dx_ref, dxb_ref, l_ref):
        i = pl.program_id(0)
        d = x_ref[...] - t_ref[...]
        dx = d * (1.0 / D)
        dx_ref[...] = dx
        dxb_ref[...] = dx.astype(bf16)
        e = d * d
        part = e[:, 0:128]
        for k in range(1, D // 128):
            part = part + e[:, k * 128:(k + 1) * 128]
        part = jnp.sum(part.reshape(tm // 8, 8, 128), axis=0)

        @pl.when(i == 0)
        def _():
            l_ref[...] = jnp.zeros_like(l_ref)

        l_ref[...] += part

        @pl.when(i == pl.num_programs(0) - 1)
        def _():
            tot = jnp.sum(l_ref[...], axis=1, keepdims=True)
            tot = jnp.sum(tot, axis=0, keepdims=True)
            l_ref[...] = jnp.broadcast_to(tot * (0.5 / D), l_ref.shape)

    row = pl.BlockSpec((tm, D), lambda i: (i, 0))
    return pl.pallas_call(
        body, name="loss_grad", grid=(S // tm,),
        in_specs=[row, row], out_specs=[row, row, pl.BlockSpec((8, 128), lambda i: (0, 0))],
        out_shape=[SDS((S, D), f32), SDS((S, D), bf16), SDS((8, 128), f32)], compiler_params=_params(),
    )(xf, tgt)


def _rms_proj(x, g, l, w_in_g, after, tm=1024):
    S, D = x.shape
    wc = w_in_g.shape[2]
    n_out = N_DEV * wc

    def body(x_ref, g_ref, w_ref, after_ref, proj_ref, h_ref):
        @pl.when(pl.program_id(1) == 0)
        def _():
            xv = x_ref[...]
            r = lax.rsqrt(jnp.mean(xv * xv, axis=-1, keepdims=True) + EPS)
            h_ref[...] = (xv * r * g_ref[...]).astype(bf16)

        proj_ref[...] = _dot(h_ref[...], w_ref[...])

    return pl.pallas_call(
        body, name="rms_proj", grid=(S // tm, N_DEV),
        in_specs=[pl.BlockSpec((tm, D), lambda i, j: (i, 0)), _layer_spec(g, l),
                  pl.BlockSpec((None, D, wc), lambda i, j: (j, 0, 0)), pl.BlockSpec(memory_space=pl.ANY)],
        out_specs=[pl.BlockSpec((tm, wc), lambda i, j: (i, j)), pl.BlockSpec((tm, D), lambda i, j: (i, 0))],
        out_shape=[SDS((S, n_out), f32), SDS((S, D), bf16)], compiler_params=_params(),
    )(x, g, w_in_g, after)


def _out_proj(x, y, w_out_g, after, tm=512):
    S, D = x.shape
    rb = w_out_g.shape[1]

    def body(x_ref, y_ref, w_ref, after_ref, o_ref):
        w = w_ref[...].reshape(N_DEV * rb, D)
        o_ref[...] = x_ref[...] + _dot(y_ref[...], w)

    row = pl.BlockSpec((tm, D), lambda i: (i, 0))
    return pl.pallas_call(
        body, name="out_proj", grid=(S // tm,),
        in_specs=[row, row, pl.BlockSpec((N_DEV, rb, D), lambda i: (0, 0, 0)), pl.BlockSpec(memory_space=pl.ANY)],
        out_specs=row, out_shape=SDS((S, D), f32), compiler_params=_params(),
    )(x, y, w_out_g, after)


def _out_bwd_dy(dxb, w_out_g, tm=512):
    S, D = dxb.shape
    rb = w_out_g.shape[1]

    nb = 2

    def body(dx_ref, w_ref, o_ref):
        o_ref[...] = _dot(dx_ref[...], w_ref[...].reshape(nb * rb, D), NT)

    return pl.pallas_call(
        body, name="out_bwd_dy", grid=(S // tm, N_DEV // nb),
        in_specs=[pl.BlockSpec((tm, D), lambda i, j: (i, 0)),
                  pl.BlockSpec((nb, rb, D), lambda i, j: (j, 0, 0))],
        out_specs=pl.BlockSpec((tm, nb * rb), lambda i, j: (i, j)),
        out_shape=SDS((S, D), f32), compiler_params=_params(),
    )(dxb, w_out_g)


def _tn_grad(name, a, b, tm, tn, rows_major, after):
    S, M = a.shape
    N = b.shape[1]
    if rows_major:
        rb = M // N_DEV
        nb = tm // rb
        out_shape = SDS((N_DEV, rb, N), bf16)
        out_spec = pl.BlockSpec((nb, rb, tn), lambda i, j: (i, 0, j))
    else:
        out_shape = SDS((N_DEV, M, N // N_DEV), bf16)
        assert tn == N // N_DEV
        out_spec = pl.BlockSpec((None, tm, tn), lambda i, j: (j, i, 0))

    def body(a_ref, b_ref, after_ref, o_ref):
        o_ref[...] = _dot(a_ref[...], b_ref[...], TN).astype(bf16).reshape(o_ref.shape)

    return pl.pallas_call(
        body, name=name, grid=(M // tm, N // tn),
        in_specs=[pl.BlockSpec((S, tm), lambda i, j: (0, i)), pl.BlockSpec((S, tn), lambda i, j: (0, j)),
                  pl.BlockSpec(memory_space=pl.ANY)],
        out_specs=out_spec, out_shape=out_shape, compiler_params=_params(),
    )(a, b, after)


def _in_bwd_dh(dproj, w_in_g, after, tm=1024, tn=256):
    S = dproj.shape[0]
    _, D, wc = w_in_g.shape
    tm = min(tm, S)

    def body(dp_ref, w_ref, after_ref, o_ref):
        acc = _dot(dp_ref[:, 0:wc], w_ref[0], NT)
        for k in range(1, N_DEV):
            acc = acc + _dot(dp_ref[:, k * wc:(k + 1) * wc], w_ref[k], NT)
        o_ref[...] = acc

    return pl.pallas_call(
        body, name="in_bwd_dh", grid=(S // tm, D // tn),
        in_specs=[pl.BlockSpec((tm, N_DEV * wc), lambda i, j: (i, 0)),
                  pl.BlockSpec((N_DEV, tn, wc), lambda i, j: (0, j, 0)), pl.BlockSpec(memory_space=pl.ANY)],
        out_specs=pl.BlockSpec((tm, tn), lambda i, j: (i, j)),
        out_shape=SDS((S, D), f32), compiler_params=_params(),
    )(dproj, w_in_g, after)


def _rms_bwd(dh, x, g, l, dx_next, after, tm=256):
    S, D = x.shape

    def body(dh_ref, x_ref, g_ref, dxn_ref, after_ref, dx_ref, dxb_ref, dg_ref):
        @pl.when(pl.program_id(0) == 0)
        def _():
            dg_ref[...] = jnp.zeros_like(dg_ref)

        dh = dh_ref[...]
        xv = x_ref[...]
        r = lax.rsqrt(jnp.mean(xv * xv, axis=-1, keepdims=True) + EPS)
        xhat = xv * r
        dxhat = dh * g_ref[...]
        dx = r * (dxhat - xhat * jnp.mean(dxhat * xhat, axis=-1, keepdims=True)) + dxn_ref[...]
        dx_ref[...] = dx
        dxb_ref[...] = dx.astype(bf16)
        dg_ref[...] += jnp.sum(dh * xhat, axis=0, keepdims=True)

    row = pl.BlockSpec((tm, D), lambda i: (i, 0))
    vec = pl.BlockSpec((1, D), lambda i: (0, 0))
    return pl.pallas_call(
        body, name="rms_bwd", grid=(S // tm,),
        in_specs=[row, row, _layer_spec(g, l), row, pl.BlockSpec(memory_space=pl.ANY)], out_specs=[row, row, vec],
        out_shape=[SDS((S, D), f32), SDS((S, D), bf16), SDS((1, D), f32)], compiler_params=_params(),
    )(dh, x, g, dx_next, after)


def _sgu_fwd(proj, ln_g, ln_b, w_s, b_t, l, after):
    S = proj.shape[0]
    da = A_GROUPS * HEAD_DIM
    D = 2 * da

    def body(u_ref, v_ref, z_ref, lg_ref, lb_ref, w_ref, bt_ref, after_ref, y_ref):
        u = _gelu(u_ref[...])
        v = _gelu(v_ref[...])
        z = z_ref[...]
        mu = jnp.mean(v, axis=-1, keepdims=True)
        xc = v - mu
        rs = lax.rsqrt(jnp.mean(xc * xc, axis=-1, keepdims=True) + EPS)
        vn = (xc * rs * lg_ref[...] + lb_ref[...]).astype(bf16)
        gate = u * (z * jax.nn.sigmoid(z))
        tri = lax.broadcasted_iota(jnp.int32, (CHUNK, CHUNK), 0) >= lax.broadcasted_iota(jnp.int32, (CHUNK, CHUNK), 1)
        for g in range(A_GROUPS):
            sl = slice(g * HEAD_DIM, (g + 1) * HEAD_DIM)
            wm = jnp.where(tri, w_ref[g], 0.0).astype(bf16)
            mixed = _dot(wm, vn[:, sl]) + bt_ref[:, g:g + 1]
            y_ref[:, sl] = (gate[:, sl] * mixed).astype(bf16)

    blk = lambda cb: pl.BlockSpec((CHUNK, da), lambda c: (c, cb))
    full = lambda shp: pl.BlockSpec(shp, lambda c: (0,) * len(shp))
    return pl.pallas_call(
        body, name="sgu_fwd", grid=(S // CHUNK,),
        in_specs=[blk(0), blk(1), blk(2), _layer_spec(ln_g, l), _layer_spec(ln_b, l), _layer_spec(w_s, l),
                  _layer_spec(b_t, l), pl.BlockSpec(memory_space=pl.ANY)],
        out_specs=blk(0), out_shape=SDS((S, D), bf16), compiler_params=_params(),
    )(proj, proj, proj, ln_g, ln_b, w_s, b_t, after)


def _sgu_bwd(proj, dy, ln_g, ln_b, w_s, b_t, l, after):
    S = proj.shape[0]
    da = A_GROUPS * HEAD_DIM
    n_proj = proj.shape[1]

    def body(u_ref, v_ref, z_ref, dy_ref, lg_ref, lb_ref, w_ref, bt_ref, after_ref,
             dp_ref, dw_ref, db_ref, dlg_ref, dlb_ref, dvn_ref):
        @pl.when(pl.program_id(0) == 0)
        def _():
            dw_ref[...] = jnp.zeros_like(dw_ref)
            db_ref[...] = jnp.zeros_like(db_ref)
            dlg_ref[...] = jnp.zeros_like(dlg_ref)
            dlb_ref[...] = jnp.zeros_like(dlb_ref)

        up, vp, z, dy = u_ref[...], v_ref[...], z_ref[...], dy_ref[...]
        u, gu = _gelu_and_grad(up)
        v, gv = _gelu_and_grad(vp)
        s, gs = _silu_and_grad(z)
        mu = jnp.mean(v, axis=-1, keepdims=True)
        xc = v - mu
        rs = lax.rsqrt(jnp.mean(xc * xc, axis=-1, keepdims=True) + EPS)
        vhat = xc * rs
        lg = lg_ref[...]
        vn = (vhat * lg + lb_ref[...]).astype(bf16)
        tri = lax.broadcasted_iota(jnp.int32, (CHUNK, CHUNK), 0) >= lax.broadcasted_iota(jnp.int32, (CHUNK, CHUNK), 1)
        lane = lax.broadcasted_iota(jnp.int32, (CHUNK, HEAD_DIM), 1)
        dys = dy * s
        db = jnp.zeros((CHUNK, HEAD_DIM), f32)
        for g in range(A_GROUPS):
            sl = slice(g * HEAD_DIM, (g + 1) * HEAD_DIM)
            wm = jnp.where(tri, w_ref[g], 0.0).astype(bf16)
            mixed = _dot(wm, vn[:, sl]) + bt_ref[:, g:g + 1]
            dmix = dys[:, sl] * u[:, sl]
            dp_ref[:, sl] = (dys[:, sl] * mixed * gu[:, sl]).astype(bf16)
            dp_ref[:, 2 * da + g * HEAD_DIM:2 * da + (g + 1) * HEAD_DIM] = (
                dy[:, sl] * u[:, sl] * mixed * gs[:, sl]).astype(bf16)
            dmb = dmix.astype(bf16)
            dw_ref[g] += jnp.where(tri, _dot(dmb, vn[:, sl], NT), 0.0)
            dvn_ref[:, sl] = _dot(wm, dmb, TN)
            db = db + jnp.where(lane == g, jnp.sum(dmix, axis=1, keepdims=True), 0.0)
        db_ref[...] += db
        dvn = dvn_ref[...]
        dlg_ref[...] += jnp.sum(dvn * vhat, axis=0, keepdims=True)
        dlb_ref[...] += jnp.sum(dvn, axis=0, keepdims=True)
        dvhat = dvn * lg
        dv = rs * (dvhat - jnp.mean(dvhat, axis=-1, keepdims=True)
                   - vhat * jnp.mean(dvhat * vhat, axis=-1, keepdims=True))
        dp_ref[:, da:2 * da] = (dv * gv).astype(bf16)

    blk = lambda cb: pl.BlockSpec((CHUNK, da), lambda c: (c, cb))
    full = lambda shp: pl.BlockSpec(shp, lambda c: (0,) * len(shp))
    return pl.pallas_call(
        body, name="sgu_bwd", grid=(S // CHUNK,),
        in_specs=[blk(0), blk(1), blk(2), blk(0), _layer_spec(ln_g, l), _layer_spec(ln_b, l), _layer_spec(w_s, l),
                  _layer_spec(b_t, l), pl.BlockSpec(memory_space=pl.ANY)],
        out_specs=[pl.BlockSpec((CHUNK, 3 * da), lambda c: (c, 0)), full((A_GROUPS, CHUNK, CHUNK)),
                   full((CHUNK, HEAD_DIM)), full((1, da)), full((1, da))],
        out_shape=[SDS((S, n_proj), bf16), SDS((A_GROUPS, CHUNK, CHUNK), f32), SDS((CHUNK, HEAD_DIM), f32),
                   SDS((1, da), f32), SDS((1, da), f32)],
        scratch_shapes=[pltpu.VMEM((CHUNK, da), f32)], compiler_params=_params(),
    )(proj, proj, proj, dy, ln_g, ln_b, w_s, b_t, after)


def _sb_scores(q, kblk, kb, rows, cols, masked):
    z = _dot(q, kblk, NT) * (1.0 / math.sqrt(HEAD_DIM))
    t = jnp.log(1.0 + jnp.exp(-jnp.abs(z)))
    log_1mb = -(jnp.maximum(z, 0.0) + t)
    log_beta = jnp.minimum(z, 0.0) - t
    if not masked:
        return None, log_beta, log_1mb
    causal = (cols + kb * TK) < rows
    return causal, log_beta, jnp.where(causal, log_1mb, 0.0)


def _sb_tiles(i):
    rows = i * TQ + lax.broadcasted_iota(jnp.int32, (TQ, TK), 0)
    cols = lax.broadcasted_iota(jnp.int32, (TQ, TK), 1)
    r_i = lax.broadcasted_iota(jnp.int32, (TK, TK), 0)
    c_i = lax.broadcasted_iota(jnp.int32, (TK, TK), 1)
    upper, lower = (r_i > c_i).astype(bf16), (r_i < c_i).astype(bf16)
    slot = lax.broadcasted_iota(jnp.int32, (TQ, CARRY_LANES), 1)
    return rows, cols, slot, jnp.concatenate([upper, upper], axis=0), jnp.concatenate([lower, lower], axis=0)


def _suffix_sum(t, tri):
    hi = lax.bitcast_convert_type(lax.bitcast_convert_type(t, jnp.uint32) & jnp.uint32(0xFFFF0000), f32)
    both = jnp.concatenate([hi.astype(bf16), (t - hi).astype(bf16)], axis=1)
    return _dot(both, tri)


def _sb_fwd(proj, y_prev, col0, after):
    S = proj.shape[0]
    D = y_prev.shape[1]
    dh = N_HEADS * HEAD_DIM
    n_diag = TQ // TK

    def body(q_ref, k_ref, v_ref, z_ref, yp_ref, after_ref, y_ref, o_ref, car_ref, qb, kb_s, vb_s, c_ref):
        i = pl.program_id(0)

        @pl.when(i == 0)
        def _():
            kb_s[...] = k_ref[...].astype(bf16)
            vb_s[...] = v_ref[...].astype(bf16)

        qb[...] = q_ref[...].astype(bf16)
        o_ref[...] = jnp.zeros_like(o_ref)
        c_ref[...] = jnp.zeros_like(c_ref)
        car_ref[...] = jnp.zeros_like(car_ref)
        nkb = (i + 1) * n_diag
        rows, cols, slot, upper, _ = _sb_tiles(i)

        def make_step(masked):
            def step(jj, carry):
                kb = nkb - 1 - jj
                off = pl.multiple_of(kb * TK, TK)
                hs = range(N_HEADS)
                sls = [slice(h * HEAD_DIM, (h + 1) * HEAD_DIM) for h in hs]
                sc = [_sb_scores(qb[:, sls[h]], kb_s[pl.ds(off, TK), sls[h]], kb, rows, cols, masked) for h in hs]
                suf = [_suffix_sum(sc[h][2], upper) for h in hs]
                cs = [c_ref[h] for h in hs]
                es = [jnp.exp(sc[h][1] + suf[h] + cs[h][:, :1]) for h in hs]
                if masked:
                    es = [jnp.where(sc[h][0], es[h], 0.0) for h in hs]
                pv = [_dot(es[h].astype(bf16), vb_s[pl.ds(off, TK), sls[h]]) for h in hs]
                for h in hs:
                    o_ref[:, sls[h]] += pv[h]
                    car_ref[h] = jnp.where(slot == kb, cs[h], car_ref[h])
                    c_ref[h] = cs[h] + jnp.sum(sc[h][2], axis=1, keepdims=True)
                return carry
            return step

        lax.fori_loop(0, n_diag, make_step(True), 0)
        lax.fori_loop(n_diag, nkb, make_step(False), 0)
        z = z_ref[...]
        y_ref[...] = (o_ref[...] * (z * jax.nn.sigmoid(z))).astype(bf16)

    cb = col0 * HEAD_DIM // dh
    qspec = lambda k: pl.BlockSpec((TQ, dh), lambda i: (i, cb + k))
    kspec = lambda k: pl.BlockSpec((S, dh), lambda i: (0, cb + k))
    return pl.pallas_call(
        body, name="sb_fwd", grid=(S // TQ,),
        in_specs=[qspec(0), kspec(1), kspec(2), qspec(3), pl.BlockSpec(memory_space=pl.ANY),
                  pl.BlockSpec(memory_space=pl.ANY)],
        out_specs=[pl.BlockSpec((TQ, dh), lambda i: (i, A_GROUPS * HEAD_DIM // dh)),
                   pl.BlockSpec((TQ, dh), lambda i: (i, 0)),
                   pl.BlockSpec((N_HEADS, TQ, CARRY_LANES), lambda i: (0, i, 0))],
        out_shape=[SDS((S, D), bf16), SDS((S, dh), f32), SDS((N_HEADS, S, CARRY_LANES), f32)],
        input_output_aliases={4: 0},
        scratch_shapes=[pltpu.VMEM((TQ, dh), bf16), pltpu.VMEM((S, dh), bf16), pltpu.VMEM((S, dh), bf16),
                        pltpu.VMEM((N_HEADS, TQ, CARRY_LANES), f32)],
        compiler_params=_params(),
    )(proj, proj, proj, proj, y_prev, after)


def _sb_bwd(proj, o, car, dy, dproj_prev, col0, after):
    S = proj.shape[0]
    n_i = S // TQ
    dh = N_HEADS * HEAD_DIM
    n_diag = TQ // TK
    cb = col0 * HEAD_DIM // dh
    scale = 1.0 / math.sqrt(HEAD_DIM)

    def body(q_ref, k_ref, v_ref, z_ref, o_ref, car_ref, dy_ref, dpp_ref, after_ref,
             dp_ref, qb, kb_s, vb_s, dob, p_ref, dq_acc, dk_acc, dv_acc, st_a, st_b, st_k, st_v):
        i = pl.program_id(0)

        def put(stage_ref, row0, nrows, k):
            pltpu.sync_copy(stage_ref, dp_ref.at[pl.ds(row0, nrows), pl.ds((cb + k) * dh, dh)])

        @pl.when(i == 0)
        def _():
            kb_s[...] = k_ref[...].astype(bf16)
            vb_s[...] = v_ref[...].astype(bf16)
            dk_acc[...] = jnp.zeros_like(dk_acc)
            dv_acc[...] = jnp.zeros_like(dv_acc)

        s, gs = _silu_and_grad(z_ref[...])
        dy = dy_ref[...]
        st_b[...] = (dy * o_ref[...] * gs).astype(bf16)
        dob[...] = (dy * s).astype(bf16)
        qb[...] = q_ref[...].astype(bf16)
        p_ref[...] = jnp.zeros_like(p_ref)
        dq_acc[...] = jnp.zeros_like(dq_acc)
        nkb = (i + 1) * n_diag
        rows, cols, slot, upper, lower = _sb_tiles(i)

        def make_step(masked):
            def step(kb, carry):
                off = pl.multiple_of(kb * TK, TK)
                hs = range(N_HEADS)
                sls = [slice(h * HEAD_DIM, (h + 1) * HEAD_DIM) for h in hs]
                qs = [qb[:, sls[h]] for h in hs]
                ks = [kb_s[pl.ds(off, TK), sls[h]] for h in hs]
                dos = [dob[:, sls[h]] for h in hs]
                sc = [_sb_scores(qs[h], ks[h], kb, rows, cols, masked) for h in hs]
                da = [_dot(dos[h], vb_s[pl.ds(off, TK), sls[h]], NT) for h in hs]
                suf = [_suffix_sum(sc[h][2], upper) for h in hs]
                onehot = slot == kb
                cs = [jnp.sum(jnp.where(onehot, car_ref[h], 0.0), axis=1, keepdims=True) for h in hs]
                es = [jnp.exp(sc[h][1] + suf[h] + cs[h]) for h in hs]
                if masked:
                    es = [jnp.where(sc[h][0], es[h], 0.0) for h in hs]
                gs_ = [da[h] * es[h] for h in hs]
                ps = [p_ref[h] for h in hs]
                pre = [_suffix_sum(gs_[h], lower) + ps[h][:, :1] for h in hs]
                dzs = []
                for h in hs:
                    beta = jnp.exp(sc[h][1])
                    dzz = gs_[h] * (1.0 - beta) - beta * pre[h]
                    if masked:
                        dzz = jnp.where(sc[h][0], dzz, 0.0)
                    dzs.append((dzz * scale).astype(bf16))
                dqs = [_dot(dzs[h], ks[h]) for h in hs]
                dks = [_dot(dzs[h], qs[h], TN) for h in hs]
                dvs = [_dot(es[h].astype(bf16), dos[h], TN) for h in hs]
                for h in hs:
                    dq_acc[:, sls[h]] += dqs[h]
                    dk_acc[pl.ds(off, TK), sls[h]] += dks[h]
                    dv_acc[pl.ds(off, TK), sls[h]] += dvs[h]
                    p_ref[h] = ps[h] + jnp.sum(gs_[h], axis=1, keepdims=True)
                return carry
            return step

        lax.fori_loop(0, nkb - n_diag, make_step(False), 0)
        lax.fori_loop(nkb - n_diag, nkb, make_step(True), 0)
        st_a[...] = dq_acc[...].astype(bf16)
        row0 = pl.multiple_of(i * TQ, TQ)
        put(st_a, row0, TQ, 0)
        put(st_b, row0, TQ, 3)

        @pl.when(i == n_i - 1)
        def _():
            st_k[...] = dk_acc[...].astype(bf16)
            st_v[...] = dv_acc[...].astype(bf16)
            put(st_k, 0, S, 1)
            put(st_v, 0, S, 2)

    qspec = lambda k: pl.BlockSpec((TQ, dh), lambda i: (i, cb + k))
    kspec = lambda k: pl.BlockSpec((S, dh), lambda i: (0, cb + k))
    return pl.pallas_call(
        body, name="sb_bwd", grid=(n_i,),
        in_specs=[qspec(0), kspec(1), kspec(2), qspec(3),
                  pl.BlockSpec((TQ, dh), lambda i: (i, 0)),
                  pl.BlockSpec((N_HEADS, TQ, CARRY_LANES), lambda i: (0, i, 0)),
                  pl.BlockSpec((TQ, dh), lambda i: (i, A_GROUPS * HEAD_DIM // dh)),
                  pl.BlockSpec(memory_space=pl.ANY), pl.BlockSpec(memory_space=pl.ANY)],
        out_specs=pl.BlockSpec(memory_space=pl.ANY),
        out_shape=SDS(dproj_prev.shape, bf16),
        input_output_aliases={7: 0},
        scratch_shapes=[pltpu.VMEM((TQ, dh), bf16), pltpu.VMEM((S, dh), bf16), pltpu.VMEM((S, dh), bf16),
                        pltpu.VMEM((TQ, dh), bf16), pltpu.VMEM((N_HEADS, TQ, CARRY_LANES), f32), pltpu.VMEM((TQ, dh), f32),
                        pltpu.VMEM((S, dh), f32), pltpu.VMEM((S, dh), f32),
                        pltpu.VMEM((TQ, dh), bf16), pltpu.VMEM((TQ, dh), bf16),
                        pltpu.VMEM((S, dh), bf16), pltpu.VMEM((S, dh), bf16)],
        compiler_params=_params(56),
    )(proj, proj, proj, proj, o, car, dy, dproj_prev, after)


def _mem_kv(mem, mg, l, w_kv_g):
    M, D = mem.shape
    rb, n = w_kv_g.shape[1], w_kv_g.shape[2]

    def body(m_ref, g_ref, w_ref, kv_ref):
        mv = m_ref[...]
        r = lax.rsqrt(jnp.mean(mv * mv, axis=-1, keepdims=True) + EPS)
        mh = (mv * r * g_ref[...]).astype(bf16)
        kv_ref[...] = _dot(mh, w_ref[...].reshape(N_DEV * rb, n))

    return pl.pallas_call(
        body, name="mem_kv", grid=(1,),
        in_specs=[pl.BlockSpec((M, D), lambda i: (0, 0)), _layer_spec(mg, l),
                  pl.BlockSpec((N_DEV, rb, n), lambda i: (0, 0, 0))],
        out_specs=pl.BlockSpec((M, n), lambda i: (0, 0)),
        out_shape=SDS((M, n), f32), compiler_params=_params(),
    )(mem, mg, w_kv_g)


def _xattn_head(q_ref, kv_ref, qg, kg, h):
    dc = N_HEADS * HEAD_DIM
    sl = slice(h * HEAD_DIM, (h + 1) * HEAD_DIM)
    qh = q_ref[:, sl]
    rq = lax.rsqrt(jnp.mean(qh * qh, axis=-1, keepdims=True) + EPS)
    qhat = qh * rq
    qn = (qhat * qg).astype(bf16)
    kh = kv_ref[:, sl]
    rk = lax.rsqrt(jnp.mean(kh * kh, axis=-1, keepdims=True) + EPS)
    kn = (kh * rk * kg).astype(bf16)
    vh = kv_ref[:, dc + h * HEAD_DIM:dc + (h + 1) * HEAD_DIM].astype(bf16)
    s = _dot(qn, kn, NT) * (1.0 / math.sqrt(HEAD_DIM))
    e = jnp.exp(s - jnp.max(s, axis=-1, keepdims=True))
    p = e / jnp.sum(e, axis=-1, keepdims=True)
    o = _dot(p.astype(bf16), vh)
    return sl, rq, qhat, qn, kn, vh, p, o


def _xattn_fwd(proj, kv, qg, kg, l, y_prev, col0, tq=512):
    S = proj.shape[0]
    D = y_prev.shape[1]
    dc = N_HEADS * HEAD_DIM
    M = kv.shape[0]

    def body(q_ref, z_ref, kv_ref, qg_ref, kg_ref, yp_ref, y_ref):
        for h in range(N_HEADS):
            sl, _, _, _, _, _, _, o = _xattn_head(q_ref, kv_ref, qg_ref[...], kg_ref[...], h)
            z = z_ref[:, sl]
            y_ref[:, sl] = (o * (z * jax.nn.sigmoid(z))).astype(bf16)

    full = lambda shp: pl.BlockSpec(shp, lambda i: (0,) * len(shp))
    return pl.pallas_call(
        body, name="xattn_fwd", grid=(S // tq,),
        in_specs=[pl.BlockSpec((tq, dc), lambda i: (i, col0)), pl.BlockSpec((tq, dc), lambda i: (i, col0 + 1)),
                  full((M, 2 * dc)), _layer_spec(qg, l), _layer_spec(kg, l), pl.BlockSpec(memory_space=pl.ANY)],
        out_specs=pl.BlockSpec((tq, dc), lambda i: (i, D // dc - 1)),
        out_shape=SDS((S, D), bf16), input_output_aliases={5: 0}, compiler_params=_params(),
    )(proj, proj, kv, qg, kg, y_prev)


def _xattn_bwd(proj, kv, qg, kg, l, dy, dproj_prev, col0, tq=512):
    S = proj.shape[0]
    D = dy.shape[1]
    dc = N_HEADS * HEAD_DIM
    M = kv.shape[0]

    def body(q_ref, z_ref, kv_ref, qg_ref, kg_ref, dy_ref, dpp_ref, dp_ref, dkn_ref, dv_ref, dqg_ref):
        @pl.when(pl.program_id(0) == 0)
        def _():
            dkn_ref[...] = jnp.zeros_like(dkn_ref)
            dv_ref[...] = jnp.zeros_like(dv_ref)
            dqg_ref[...] = jnp.zeros_like(dqg_ref)

        qg = qg_ref[...]
        for h in range(N_HEADS):
            sl, rq, qhat, qn, kn, vh, p, o = _xattn_head(q_ref, kv_ref, qg, kg_ref[...], h)
            s, gs = _silu_and_grad(z_ref[:, sl])
            dyh = dy_ref[:, sl]
            dp_ref[:, dc + h * HEAD_DIM:dc + (h + 1) * HEAD_DIM] = (dyh * o * gs).astype(bf16)
            dob = (dyh * s).astype(bf16)
            dpr = _dot(dob, vh, NT)
            dv_ref[:, sl] += _dot(p.astype(bf16), dob, TN)
            ds = (p * (dpr - jnp.sum(p * dpr, axis=-1, keepdims=True)) * (1.0 / math.sqrt(HEAD_DIM))).astype(bf16)
            dqn = _dot(ds, kn)
            dkn_ref[:, sl] += _dot(ds, qn, TN)
            dqg_ref[...] += jnp.sum(dqn * qhat, axis=0, keepdims=True)
            dqhat = dqn * qg
            dp_ref[:, sl] = (rq * (dqhat - qhat * jnp.mean(dqhat * qhat, axis=-1, keepdims=True))).astype(bf16)

    full = lambda shp: pl.BlockSpec(shp, lambda i: (0,) * len(shp))
    return pl.pallas_call(
        body, name="xattn_bwd", grid=(S // tq,),
        in_specs=[pl.BlockSpec((tq, dc), lambda i: (i, col0)), pl.BlockSpec((tq, dc), lambda i: (i, col0 + 1)),
                  full((M, 2 * dc)), _layer_spec(qg, l), _layer_spec(kg, l),
                  pl.BlockSpec((tq, dc), lambda i: (i, D // dc - 1)), pl.BlockSpec(memory_space=pl.ANY)],
        out_specs=[pl.BlockSpec((tq, 2 * dc), lambda i: (i, col0 // 2)), full((M, dc)), full((M, dc)),
                   full((1, HEAD_DIM))],
        out_shape=[SDS(dproj_prev.shape, bf16), SDS((M, dc), f32), SDS((M, dc), f32), SDS((1, HEAD_DIM), f32)],
        input_output_aliases={6: 0}, compiler_params=_params(),
    )(proj, proj, kv, qg, kg, dy, dproj_prev)


def _mem_bwd(mem, mg, kg, l, kv, dkn, dv, w_kv_g):
    M, D = mem.shape
    rb, n = w_kv_g.shape[1], w_kv_g.shape[2]
    dc = n // 2

    def body(m_ref, g_ref, kv_ref, dkn_ref, dv_ref, kg_ref, w_ref, dw_ref, dmg_ref, dkg_ref, dkv_ref):
        mv = m_ref[...]
        r = lax.rsqrt(jnp.mean(mv * mv, axis=-1, keepdims=True) + EPS)
        mhat = mv * r
        mh = (mhat * g_ref[...]).astype(bf16)
        kg = kg_ref[...]
        dkg = jnp.zeros((1, HEAD_DIM), f32)
        for h in range(N_HEADS):
            sl = slice(h * HEAD_DIM, (h + 1) * HEAD_DIM)
            kh = kv_ref[:, sl]
            rk = lax.rsqrt(jnp.mean(kh * kh, axis=-1, keepdims=True) + EPS)
            khat = kh * rk
            dkn_h = dkn_ref[:, sl]
            dkg = dkg + jnp.sum(dkn_h * khat, axis=0, keepdims=True)
            dkhat = dkn_h * kg
            dkv_ref[:, sl] = (rk * (dkhat - khat * jnp.mean(dkhat * khat, axis=-1, keepdims=True))).astype(bf16)
        dkv_ref[:, dc:] = dv_ref[...].astype(bf16)
        dkg_ref[...] = dkg
        dkv = dkv_ref[...]
        dw_ref[...] = _dot(mh, dkv, TN).astype(bf16).reshape(N_DEV, rb, n)
        dmh = _dot(dkv, w_ref[...].reshape(N_DEV * rb, n), NT)
        dmg_ref[...] = jnp.sum(dmh * mhat, axis=0, keepdims=True)

    full = lambda shp: pl.BlockSpec(shp, lambda i: (0,) * len(shp))
    wspec = full((N_DEV, rb, n))
    return pl.pallas_call(
        body, name="mem_bwd", grid=(1,),
        in_specs=[full((M, D)), _layer_spec(mg, l), full((M, n)), full((M, dc)), full((M, dc)), _layer_spec(kg, l), wspec],
        out_specs=[wspec, full((1, D)), full((1, HEAD_DIM))],
        out_shape=[SDS((N_DEV, rb, n), bf16), SDS((1, D), f32), SDS((1, HEAD_DIM), f32)],
        scratch_shapes=[pltpu.VMEM((M, n), bf16)], compiler_params=_params(),
    )(mem, mg, kv, dkn, dv, kg, w_kv_g)


SMALL = ("norm_g", "sgu_ln_g", "sgu_ln_b", "sgu_w", "sgu_b", "mem_norm_g", "q_norm_g", "k_norm_g")


def _small_rows(like):
    rows = [math.prod(like[n].shape) // 128 for n in SMALL]
    offs = [0]
    for r in rows:
        offs.append(offs[-1] + -(-r // 8) * 8)
    return rows, offs


def _pack_small(parts, offs):
    pieces = []
    for k, n in enumerate(SMALL):
        a = parts[n].reshape(-1, 128)
        pieces.append(jnp.pad(a, ((0, offs[k + 1] - offs[k] - a.shape[0]), (0, 0))))
    return jnp.concatenate(pieces)


def kernel(x, mem, norm_g, w_in, sgu_ln_g, sgu_ln_b, sgu_w, sgu_b, mem_norm_g, w_mem_kv, q_norm_g, k_norm_g, w_out, loss_target, m_norm_g, m_w_in, m_sgu_ln_g, m_sgu_ln_b, m_sgu_w, m_sgu_b, m_mem_norm_g, m_w_mem_kv, m_q_norm_g, m_k_norm_g, m_w_out, v_norm_g, v_w_in, v_sgu_ln_g, v_sgu_ln_b, v_sgu_w, v_sgu_b, v_mem_norm_g, v_w_mem_kv, v_q_norm_g, v_k_norm_g, v_w_out):
    L, D, wc = w_in.shape
    S = x.shape[1]
    da = D // 2
    xs = x.reshape(S, D)
    mems = mem.reshape(mem.shape[1], D)
    tgt = loss_target.reshape(S, D)
    stacked = lambda a: a.reshape(a.shape[0], 1, -1)
    ng, lng, lnb, mg, qg, kg = map(stacked, (norm_g, sgu_ln_g, sgu_ln_b, mem_norm_g, q_norm_g, k_norm_g))
    b_t = jnp.swapaxes(sgu_b, 1, 2)
    sb_col, xa_col = 3 * da // HEAD_DIM, (3 * da + D) // (D // 4)

    ax, ay, ac = lax.axis_index("x"), lax.axis_index("y"), lax.axis_index("c")
    ids = jnp.stack([4 * ax + 2 * ay + ac, 2 * ax + ay, ac]).astype(jnp.int32)
    w_in0_b = _cast_into_slot("cast_w_in", w_in, 0, 512, ids, ids)
    first = _gather2_start("gather_w_in0", [w_in0_b], ids)
    late = first["token"]
    w_b = [(w_in0_b if l == 0 else _cast_into_slot("cast_w_in", w_in, l, 512, ids, late),
            _cast_into_slot("cast_w_kv", w_mem_kv, l, 256, ids, late),
            _cast_into_slot("cast_w_out", w_out, l, 256, ids, late)) for l in range(L)]
    in_fwd = _gather2_forward("gather_w_in0_forward",
                              _split_wait(first, *[a for wl in w_b for a in wl if a is not w_in0_b]), ids)

    acts = []
    xl = xs
    for l in range(L):
        (w_in_g,) = _split_wait(in_fwd, xl if l else in_fwd["token"])
        rest = _gather2_start(f"gather_w_rest{l}", [w_b[l][1], w_b[l][2]], w_in_g)
        proj, h = _rms_proj(xl, ng, l, w_in_g, rest["token"])
        rest_fwd = _gather2_forward(f"gather_w_rest{l}_forward", _split_wait(rest, proj), proj)
        order = rest_fwd["token"]
        if l + 1 < L:
            nxt = _gather2_start(f"gather_w_in{l + 1}", [w_b[l + 1][0]], order)
            order = nxt["token"]
        y = _sgu_fwd(proj, lng, lnb, sgu_w, b_t, l, order)
        y, o_b, car = _sb_fwd(proj, y, sb_col, order)
        w_kv_g, w_out_g = _split_wait(rest_fwd, o_b)
        kv = _mem_kv(mems, mg, l, w_kv_g)
        y = _xattn_fwd(proj, kv, qg, kg, l, y, xa_col)
        order = kv
        if l + 1 < L:
            in_fwd = _gather2_forward(f"gather_w_in{l + 1}_forward", _split_wait(nxt, y), y)
            order = in_fwd["token"]
        x_next = _out_proj(xl, y, w_out_g, order)
        acts.append((xl, proj, h, y, o_b, car, kv, w_in_g, w_kv_g, w_out_g))
        xl = x_next

    dx, dxb, loss_part = _loss_and_grad(xl, tgt, 512)
    loss = lax.psum(loss_part[0, 0], ("x", "y", "c"))

    weights = dict(norm_g=norm_g, sgu_ln_g=sgu_ln_g, sgu_ln_b=sgu_ln_b, sgu_w=sgu_w, sgu_b=sgu_b,
                   mem_norm_g=mem_norm_g, q_norm_g=q_norm_g, k_norm_g=k_norm_g)
    moms_m = dict(norm_g=m_norm_g, sgu_ln_g=m_sgu_ln_g, sgu_ln_b=m_sgu_ln_b, sgu_w=m_sgu_w, sgu_b=m_sgu_b,
                  mem_norm_g=m_mem_norm_g, q_norm_g=m_q_norm_g, k_norm_g=m_k_norm_g)
    moms_v = dict(norm_g=v_norm_g, sgu_ln_g=v_sgu_ln_g, sgu_ln_b=v_sgu_ln_b, sgu_w=v_sgu_w, sgu_b=v_sgu_b,
                  mem_norm_g=v_mem_norm_g, q_norm_g=v_q_norm_g, k_norm_g=v_k_norm_g)
    small_rows, small_offs = _small_rows(weights)
    head_rows = D // 128
    assert SMALL[0] == "norm_g" and head_rows % 8 == 0

    seconds = {}
    pending = None
    adam = {"w_out": None, "w_mem_kv": None, "w_in": None}

    def update(lu, order):
        (r_out,) = _scatter_finish(seconds[f"g_out{lu}"], order)
        adam["w_out"] = _sum_adam("adam_w_out", r_out, w_out, m_w_out, v_w_out, lu, adam["w_out"], 128, ids)
        r_kv, r_in = _scatter_finish(seconds[f"g_rest{lu}"], adam["w_out"][0])
        adam["w_mem_kv"] = _sum_adam("adam_w_kv", r_kv, w_mem_kv, m_w_mem_kv, v_w_mem_kv, lu, adam["w_mem_kv"], 256, ids)
        adam["w_in"] = _sum_adam("adam_w_in", r_in, w_in, m_w_in, v_w_in, lu, adam["w_in"], 256, ids)
        return adam["w_in"][0]
    small = {n: [None] * L for n in SMALL}
    for l in reversed(range(L)):
        xl, proj, h, y, o_b, car, kv, w_in_g, w_kv_g, w_out_g = acts[l]
        dy = _out_bwd_dy(dxb, w_out_g)
        order = dy
        if pending is not None:
            seconds[pending[0]] = _scatter2_second_level(pending[0], pending[1], dy, ids)
            order = seconds[pending[0]]["token"]
        g_out = _tn_grad("out_bwd_dw", y, dxb, 512, 512, True, order)
        seconds[f"g_out{l}"] = _scatter1_start(f"scatter_g_out{l}", [g_out], ids)
        dproj, d_sw, d_sb, d_lg, d_lb = _sgu_bwd(proj, dy, lng, lnb, sgu_w, b_t, l, seconds[f"g_out{l}"]["token"])
        dproj = _sb_bwd(proj, o_b, car, dy, dproj, sb_col, d_lb)
        dproj, dkn, dv, d_qg = _xattn_bwd(proj, kv, qg, kg, l, dy, dproj, xa_col)
        g_kv, d_mg, d_kg = _mem_bwd(mems, mg, kg, l, kv, dkn, dv, w_kv_g)
        for n, val in (("sgu_ln_g", d_lg), ("sgu_ln_b", d_lb), ("sgu_w", d_sw), ("sgu_b", d_sb[:, :A_GROUPS].T),
                       ("mem_norm_g", d_mg), ("q_norm_g", d_qg), ("k_norm_g", d_kg)):
            small[n][l] = val.reshape(-1)
        order = d_kg
        if l == 0:
            small["norm_g"][0] = jnp.zeros_like(small["norm_g"][1])
            part = _pack_small({n: jnp.stack(small[n]) for n in SMALL}, small_offs)
            tail = _gather1_start("gather_small_tail", [_into_slot("small_tail_slot", part[head_rows:], ids)], ids)
            order = tail["token"]
        g_in_l = _tn_grad("in_bwd_dw", h, dproj, 1024, wc, False, order)
        first = _scatter2_pair_start(f"scatter_g_rest{l}_pair", [g_kv, g_in_l], ids)
        order = first["token"]
        pending = (f"g_rest{l}", first)
        if l == 0:
            for lu in reversed(range(1, L)):
                order = update(lu, order)
            seconds[pending[0]] = _scatter2_second_level(pending[0], pending[1], order, ids)
            order = seconds[pending[0]]["token"]
        dh = _in_bwd_dh(dproj, w_in_g, order)
        dx, dxb, d_ng = _rms_bwd(dh, xl, ng, l, dx, order)
        small["norm_g"][l] = d_ng.reshape(-1)
    head = _gather1_start("gather_small_head",
                          [_into_slot("small_head_slot", small["norm_g"][0].reshape(head_rows, 128), ids)], ids)

    (r_out,) = _scatter_finish(seconds["g_out0"], head["token"])
    adam["w_out"] = _sum_adam("adam_w_out", r_out, w_out, m_w_out, v_w_out, 0, adam["w_out"], 128, ids)
    (r_tail,) = _split_wait(tail, adam["w_out"][0])
    (r_head,) = _split_wait(head, r_tail)
    as128 = lambda d: [d[n].reshape(-1, 128) for n in SMALL]
    sm = _small_sum_adam(r_head, r_tail, as128(weights), as128(moms_m), as128(moms_v), small_offs)
    r_kv, r_in = _scatter_finish(seconds["g_rest0"], sm[0])
    adam["w_mem_kv"] = _sum_adam("adam_w_kv", r_kv, w_mem_kv, m_w_mem_kv, v_w_mem_kv, 0, adam["w_mem_kv"], 256, ids)
    adam["w_in"] = _sum_adam("adam_w_in", r_in, w_in, m_w_in, v_w_in, 0, adam["w_in"], 256, ids)
    res = dict(adam)
    for p, n in enumerate(SMALL):
        res[n] = [sm[k * len(SMALL) + p].reshape(weights[n].shape) for k in range(4)]

    order = ("norm_g", "w_in", "sgu_ln_g", "sgu_ln_b", "sgu_w", "sgu_b", "mem_norm_g", "w_mem_kv", "q_norm_g",
             "k_norm_g", "w_out")
    outs = [loss, dx.reshape(x.shape)]
    for k in range(4):
        outs += [res[n][k] for n in order]
    return tuple(outs)
```

```python
import functools
import math

import jax
import jax.numpy as jnp
from jax import lax
from jax.experimental import pallas as pl
from jax.experimental.pallas import tpu as pltpu

f32 = jnp.float32
bf16 = jnp.bfloat16
SDS = jax.ShapeDtypeStruct

N_DEV = 8
EPS = 1e-6
CHUNK = 128
A_GROUPS = 8
HEAD_DIM = 128
N_HEADS = 4
TQ = 256
TK = 256
CARRY_LANES = 128
ADAM_LR, ADAM_B1, ADAM_B2, ADAM_EPS, ADAM_WD, ADAM_STEP = 0.001, 0.9, 0.999, 1e-08, 0.01, 10
MIB = 1024 * 1024

NT = (((1,), (1,)), ((), ()))
TN = (((0,), (0,)), ((), ()))


def _params(vmem_mib=48):
    return pltpu.CompilerParams(vmem_limit_bytes=vmem_mib * MIB)


def _gelu_and_grad(x):
    e = lax.erf(x * (1.0 / math.sqrt(2.0)))
    cdf = 0.5 * (1.0 + e)
    pdf = jnp.exp(-0.5 * x * x) * (1.0 / math.sqrt(2.0 * math.pi))
    return x * cdf, cdf + x * pdf


def _gelu(x):
    return 0.5 * x * (1.0 + lax.erf(x * (1.0 / math.sqrt(2.0))))


def _silu_and_grad(z):
    sg = jax.nn.sigmoid(z)
    return z * sg, sg * (1.0 + z * (1.0 - sg))


def _layer_spec(stacked, l):
    rest = stacked.shape[1:]
    return pl.BlockSpec((None,) + rest, lambda *idx: (l,) + (0,) * len(rest))


def _dot(a, b, dims=None):
    if dims is None:
        return jnp.dot(a, b, preferred_element_type=f32)
    return lax.dot_general(a, b, dims, preferred_element_type=f32)


_HBM = pl.BlockSpec(memory_space=pltpu.HBM)
_SEM = pl.BlockSpec(memory_space=pltpu.SEMAPHORE)
_EFFECT = pltpu.SideEffectType.DATAFLOW_SIDE_EFFECTING


def _split_start(name, bufs, n_remote, n_local, build, after):
    nb = len(bufs)

    def body(*refs):
        token = refs[-1]
        locals_, remotes = build(refs[:nb], *refs[nb + 1:nb + 4])
        for cp in locals_ + remotes:
            cp.start()
        token[...] = jnp.zeros_like(token)

    hbm = lambda a: pltpu.with_memory_space_constraint(a, pltpu.HBM)
    outs = pl.pallas_call(
        body, name=name,
        out_shape=(pltpu.SemaphoreType.DMA((n_remote,)), pltpu.SemaphoreType.DMA((n_remote,)),
                   pltpu.SemaphoreType.DMA((max(n_local, 1),)),
                   *[pltpu.HBM(b.shape, b.dtype) for b in bufs], SDS((8, 128), f32)),
        in_specs=[_HBM] * nb + [pl.BlockSpec(memory_space=pl.ANY)],
        out_specs=(_SEM, _SEM, _SEM, *[_HBM] * nb, pl.BlockSpec(memory_space=pltpu.VMEM)),
        input_output_aliases={k: 3 + k for k in range(nb)},
        compiler_params=pltpu.CompilerParams(has_side_effects=_EFFECT),
    )(*[hbm(b) for b in bufs], after)
    return dict(name=name, build=build, sems=outs[:3], bufs=outs[3:3 + nb], token=outs[-1])


def _split_wait(handle, *after):
    build, bufs = handle["build"], handle["bufs"]
    nb = len(bufs)

    def body(*refs):
        locals_, remotes = build(refs[:nb], *refs[nb:nb + 3])
        for cp in remotes:
            cp.wait_recv()
        for cp in remotes:
            cp.wait_send()
        for cp in locals_:
            cp.wait()

    outs = pl.pallas_call(
        body, name=handle["name"] + "_wait",
        out_shape=tuple(pltpu.HBM(b.shape, b.dtype) for b in bufs),
        in_specs=[_HBM] * nb + [_SEM] * 3 + [pl.BlockSpec(memory_space=pl.ANY)] * len(after),
        out_specs=tuple([_HBM] * nb),
        input_output_aliases={k: k for k in range(nb)},
        compiler_params=pltpu.CompilerParams(has_side_effects=_EFFECT),
    )(*bufs, *handle["sems"], *after)
    return list(outs)


def _remote(src, dst, send_sems, recv_sems, k, to):
    return pltpu.make_async_remote_copy(src_ref=src, dst_ref=dst, send_sem=send_sems.at[k], recv_sem=recv_sems.at[k],
                                        device_id=to, device_id_type=pl.DeviceIdType.MESH)


def _other_chips(x, y):
    return [(1 - x, y), (x, 1 - y), (1 - x, 1 - y)]


def _all_peers(x, y, c):
    return [(1 - x if m & 4 else x, 1 - y if m & 2 else y, 1 - c if m & 1 else c) for m in range(1, N_DEV)]


def _gather1_start(name, lands, after):
    def build(refs, send, recv, loc):
        x, y, c = lax.axis_index("x"), lax.axis_index("y"), lax.axis_index("c")
        me = 4 * x + 2 * y + c
        return [], [_remote(d.at[me], d.at[me], send, recv, 7 * a + k, peer)
                    for a, d in enumerate(refs) for k, peer in enumerate(_all_peers(x, y, c))]

    return _split_start(name, list(lands), 7 * len(lands), 0, build, after)


def _scatter1_start(name, srcs, after):
    n = len(srcs)

    def build(refs, send, recv, loc):
        x, y, c = lax.axis_index("x"), lax.axis_index("y"), lax.axis_index("c")
        me = 4 * x + 2 * y + c
        return [], [_remote(refs[a].at[4 * px + 2 * py + pc], refs[n + a].at[me], send, recv, 7 * a + k, (px, py, pc))
                    for a in range(n) for k, (px, py, pc) in enumerate(_all_peers(x, y, c))]

    return _split_start(name, list(srcs) + [lax.empty(s.shape, s.dtype) for s in srcs], 7 * n, 0, build, after)


def _gather2_start(name, lands, after):
    def build(refs, send, recv, loc):
        x, y, c = lax.axis_index("x"), lax.axis_index("y"), lax.axis_index("c")
        me = 4 * x + 2 * y + c
        remotes = []
        for a, d in enumerate(refs):
            remotes.append(_remote(d.at[me], d.at[me], send, recv, 4 * a, (x, y, 1 - c)))
            remotes += [_remote(d.at[me], d.at[me], send, recv, 4 * a + 1 + k, (px, py, c))
                        for k, (px, py) in enumerate(_other_chips(x, y))]
        return [], remotes

    return _split_start(name, list(lands), 4 * len(lands), 0, build, after)


def _gather3_start(name, lands, after):
    def build(refs, send, recv, loc):
        x, y, c = lax.axis_index("x"), lax.axis_index("y"), lax.axis_index("c")
        me = 4 * x + 2 * y + c
        return [], [_remote(d.at[me], d.at[me], send, recv, 3 * a + k, to)
                    for a, d in enumerate(refs) for k, to in enumerate([(x, y, 1 - c), (1 - x, y, c), (x, 1 - y, c)])]

    return _split_start(name, list(lands), 3 * len(lands), 0, build, after)


def _gather3_relay(name, lands, after):
    def build(refs, send, recv, loc):
        x, y, c = lax.axis_index("x"), lax.axis_index("y"), lax.axis_index("c")
        from_x = c == 0
        slot = 4 * jnp.where(from_x, 1 - x, x) + 2 * jnp.where(from_x, y, 1 - y) + c
        to = (jnp.where(from_x, x, 1 - x), jnp.where(from_x, 1 - y, y), c)
        return [], [_remote(d.at[slot], d.at[slot], send, recv, a, to) for a, d in enumerate(refs)]

    return _split_start(name, list(lands), len(lands), 0, build, after)


def _gather2_forward(name, lands, after):
    n = len(lands)

    def build(refs, send, recv, loc):
        x, y, c = lax.axis_index("x"), lax.axis_index("y"), lax.axis_index("c")
        slots = [4 * px + 2 * py + c for px, py in _other_chips(x, y)]
        return [], [_remote(d.at[sl], d.at[sl], send, recv, 3 * a + k, (x, y, 1 - c))
                    for a, d in enumerate(refs) for k, sl in enumerate(slots)]

    return _split_start(name, list(lands), 3 * n, 0, build, after)


def _scatter2_pair_start(name, srcs, after):
    n = len(srcs)

    def build(refs, send, recv, loc):
        x, y, c = lax.axis_index("x"), lax.axis_index("y"), lax.axis_index("c")
        return [], [_remote(refs[a].at[2 * q + 1 - c], refs[n + a].at[q], send, recv, 4 * a + q, (x, y, 1 - c))
                    for a in range(n) for q in range(4)]

    lands = [lax.empty((4,) + s.shape[1:], s.dtype) for s in srcs]
    return _split_start(name, list(srcs) + lands, 4 * n, 0, build, after)


def _scatter2_chip_start(name, pairs, after):
    n = len(pairs)

    def build(refs, send, recv, loc):
        x, y, c = lax.axis_index("x"), lax.axis_index("y"), lax.axis_index("c")
        return [], [_remote(refs[a].at[2 * px + py], refs[n + a].at[2 * x + y], send, recv, 3 * a + k, (px, py, c))
                    for a in range(n) for k, (px, py) in enumerate(_other_chips(x, y))]

    return _split_start(name, list(pairs) + [lax.empty(p.shape, p.dtype) for p in pairs], 3 * n, 0, build, after)


def _pair_sum(name, src, theirs, ids):
    _, R, C = theirs.shape
    tr = min(R, 1024)

    def body(ids_ref, a_ref, b_ref, o_ref):
        o_ref[...] = (a_ref[...].astype(f32) + b_ref[...].astype(f32)).astype(bf16)

    spec = pl.BlockSpec((None, tr, C), lambda q, i, ids: (q, i, 0))
    return pl.pallas_call(
        body, name=name,
        grid_spec=pltpu.PrefetchScalarGridSpec(
            num_scalar_prefetch=1, grid=(4, R // tr),
            in_specs=[pl.BlockSpec((None, tr, C), lambda q, i, ids: (2 * q + ids[2], i, 0)), spec], out_specs=spec),
        out_shape=SDS(theirs.shape, bf16), compiler_params=_params(),
    )(ids, src, theirs)


def _scatter2_second_level(name, first, after, ids):
    outs = _split_wait(first, after)
    n = len(outs) // 2
    pairs = [_pair_sum(f"pair_sum_{name}{a}", outs[a], outs[n + a], ids) for a in range(n)]
    return _scatter2_chip_start(f"scatter_{name}_chip", pairs, ids)


def _scatter_finish(second, after):
    outs = _split_wait(second, after)
    n = len(outs) // 2
    return [(outs[a], outs[n + a]) for a in range(n)]


def _cast_into_slot(name, w, l, tr, ids, after):
    _, R, C = w.shape

    def body(ids_ref, w_ref, after_ref, o_ref):
        o_ref[...] = w_ref[...].astype(bf16)

    return pl.pallas_call(
        body, name=name,
        grid_spec=pltpu.PrefetchScalarGridSpec(
            num_scalar_prefetch=1, grid=(R // tr,),
            in_specs=[pl.BlockSpec((None, tr, C), lambda i, ids: (l, i, 0)), pl.BlockSpec(memory_space=pl.ANY)],
            out_specs=pl.BlockSpec((None, tr, C), lambda i, ids: (ids[0], i, 0))),
        out_shape=SDS((N_DEV, R, C), bf16), compiler_params=_params(),
    )(ids, w, after)


def _into_slot(name, a, ids):
    R, C = a.shape

    def body(ids_ref, a_ref, o_ref):
        o_ref[...] = a_ref[...]

    return pl.pallas_call(
        body, name=name,
        grid_spec=pltpu.PrefetchScalarGridSpec(
            num_scalar_prefetch=1, grid=(1,),
            in_specs=[pl.BlockSpec((R, C), lambda i, ids: (0, 0))],
            out_specs=pl.BlockSpec((None, R, C), lambda i, ids: (ids[0], 0, 0))),
        out_shape=SDS((N_DEV, R, C), f32), compiler_params=_params(),
    )(ids, a)


def _adam_math(w, g, m, v):
    m2 = ADAM_B1 * m + (1.0 - ADAM_B1) * g
    v2 = ADAM_B2 * v + (1.0 - ADAM_B2) * (g * g)
    m_hat = m2 / (1.0 - ADAM_B1 ** ADAM_STEP)
    v_hat = v2 / (1.0 - ADAM_B2 ** ADAM_STEP)
    delta = -ADAM_LR * (m_hat / (jnp.sqrt(v_hat) + ADAM_EPS) + ADAM_WD * w)
    return delta, m2, v2


def _sum_adam(name, pair_recv, w, m, v, l, prev, tr, ids):
    own, recv = pair_recv
    L, R, C = w.shape
    slots = recv.shape[0]
    mine = 0 if slots == N_DEV else 1

    def body(ids_ref, r_ref, own_ref, w_ref, m_ref, v_ref, *rest):
        g_ref, d_ref, m2_ref, v2_ref = rest[-4:]
        terms = [jnp.where(ids_ref[mine] == q, own_ref[...], r_ref[q]).astype(f32) for q in range(slots)]
        g = terms[0]
        for t in terms[1:]:
            g = g + t
        d, m2, v2 = _adam_math(w_ref[...], g, m_ref[...], v_ref[...])
        g_ref[...] = g
        d_ref[...] = d
        m2_ref[...] = m2
        v2_ref[...] = v2

    wspec = pl.BlockSpec((None, tr, C), lambda i, ids: (l, i, 0))
    in_specs = [pl.BlockSpec((slots, tr, C), lambda i, ids: (0, i, 0)),
                pl.BlockSpec((None, tr, C), lambda i, ids: (ids[mine], i, 0)), wspec, wspec, wspec]
    args = [ids, recv, own, w, m, v]
    aliases = {}
    if prev is not None:
        in_specs += [pl.BlockSpec(memory_space=pl.ANY)] * 4
        args += list(prev)
        aliases = {6 + k: k for k in range(4)}
    return pl.pallas_call(
        body, name=name,
        grid_spec=pltpu.PrefetchScalarGridSpec(num_scalar_prefetch=1, grid=(R // tr,), in_specs=in_specs,
                                               out_specs=[wspec] * 4),
        out_shape=[SDS((L, R, C), f32)] * 4, input_output_aliases=aliases, compiler_params=_params(),
    )(*args)


def _small_sum_adam(recv_head, recv_tail, ws, ms, vs, offs):
    n = len(ws)
    r0 = recv_head.shape[1]

    def body(*refs):
        rh, rt = refs[0], refs[1]
        w_refs, m_refs, v_refs = refs[2:2 + n], refs[2 + n:2 + 2 * n], refs[2 + 2 * n:2 + 3 * n]
        outs = refs[2 + 3 * n:]
        for p in range(n):
            lo, hi = offs[p], offs[p] + ws[p].shape[0]
            pieces = []
            if lo < r0:
                pieces.append((rh, lo, 0, min(hi, r0) - lo))
            if hi > r0:
                pieces.append((rt, max(lo, r0) - r0, max(lo, r0) - lo, hi - max(lo, r0)))
            for src, a, b, cnt in pieces:
                g = src[0, a:a + cnt, :]
                for s in range(1, N_DEV):
                    g = g + src[s, a:a + cnt, :]
                d, m2, v2 = _adam_math(w_refs[p][b:b + cnt, :], g, m_refs[p][b:b + cnt, :], v_refs[p][b:b + cnt, :])
                for k, val in enumerate((g, d, m2, v2)):
                    outs[k * n + p][b:b + cnt, :] = val

    return pl.pallas_call(
        body, name="small_sum_adam", out_shape=[SDS(w.shape, f32) for w in ws] * 4, compiler_params=_params(),
    )(recv_head, recv_tail, *ws, *ms, *vs)


def _loss_and_grad(xf, tgt, tm):
    S, D = xf.shape

    def body(x_ref, t_ref, dx_ref, dxb_ref, l_ref):
        i = pl.program_id(0)
        d = x_ref[...] - t_ref[...]
        dx = d * (1.0 / D)
        dx_ref[...] = dx
        dxb_ref[...] = dx.astype(bf16)
        e = d * d
        part = e[:, 0:128]
        for k in range(1, D // 128):
            part = part + e[:, k * 128:(k + 1) * 128]
        part = jnp.sum(part.reshape(tm // 8, 8, 128), axis=0)

        @pl.when(i == 0)
        def _():
            l_ref[...] = jnp.zeros_like(l_ref)

        l_ref[...] += part

        @pl.when(i == pl.num_programs(0) - 1)
        def _():
            tot = jnp.sum(l_ref[...], axis=1, keepdims=True)
            tot = jnp.sum(tot, axis=0, keepdims=True)
            l_ref[...] = jnp.broadcast_to(tot * (0.5 / D), l_ref.shape)

    row = pl.BlockSpec((tm, D), lambda i: (i, 0))
    return pl.pallas_call(
        body, name="loss_grad", grid=(S // tm,),
        in_specs=[row, row], out_specs=[row, row, pl.BlockSpec((8, 128), lambda i: (0, 0))],
        out_shape=[SDS((S, D), f32), SDS((S, D), bf16), SDS((8, 128), f32)], compiler_params=_params(),
    )(xf, tgt)


def _rms_proj(x, g, l, w_in_g, after, tm=1024):
    S, D = x.shape
    wc = w_in_g.shape[2]
    n_out = N_DEV * wc

    def body(x_ref, g_ref, w_ref, after_ref, proj_ref, h_ref):
        @pl.when(pl.program_id(1) == 0)
        def _():
            xv = x_ref[...]
            r = lax.rsqrt(jnp.mean(xv * xv, axis=-1, keepdims=True) + EPS)
            h_ref[...] = (xv * r * g_ref[...]).astype(bf16)

        proj_ref[...] = _dot(h_ref[...], w_ref[...])

    return pl.pallas_call(
        body, name="rms_proj", grid=(S // tm, N_DEV),
        in_specs=[pl.BlockSpec((tm, D), lambda i, j: (i, 0)), _layer_spec(g, l),
                  pl.BlockSpec((None, D, wc), lambda i, j: (j, 0, 0)), pl.BlockSpec(memory_space=pl.ANY)],
        out_specs=[pl.BlockSpec((tm, wc), lambda i, j: (i, j)), pl.BlockSpec((tm, D), lambda i, j: (i, 0))],
        out_shape=[SDS((S, n_out), f32), SDS((S, D), bf16)], compiler_params=_params(),
    )(x, g, w_in_g, after)


def _out_proj(x, y, w_out_g, after, tm=512):
    S, D = x.shape
    rb = w_out_g.shape[1]

    def body(x_ref, y_ref, w_ref, after_ref, o_ref):
        w = w_ref[...].reshape(N_DEV * rb, D)
        o_ref[...] = x_ref[...] + _dot(y_ref[...], w)

    row = pl.BlockSpec((tm, D), lambda i: (i, 0))
    return pl.pallas_call(
        body, name="out_proj", grid=(S // tm,),
        in_specs=[row, row, pl.BlockSpec((N_DEV, rb, D), lambda i: (0, 0, 0)), pl.BlockSpec(memory_space=pl.ANY)],
        out_specs=row, out_shape=SDS((S, D), f32), compiler_params=_params(),
    )(x, y, w_out_g, after)


def _out_bwd_dy(dxb, w_out_g, tm=512):
    S, D = dxb.shape
    rb = w_out_g.shape[1]

    nb = 2

    def body(dx_ref, w_ref, o_ref):
        o_ref[...] = _dot(dx_ref[...], w_ref[...].reshape(nb * rb, D), NT)

    return pl.pallas_call(
        body, name="out_bwd_dy", grid=(S // tm, N_DEV // nb),
        in_specs=[pl.BlockSpec((tm, D), lambda i, j: (i, 0)),
                  pl.BlockSpec((nb, rb, D), lambda i, j: (j, 0, 0))],
        out_specs=pl.BlockSpec((tm, nb * rb), lambda i, j: (i, j)),
        out_shape=SDS((S, D), f32), compiler_params=_params(),
    )(dxb, w_out_g)


def _tn_grad(name, a, b, tm, tn, rows_major, after):
    S, M = a.shape
    N = b.shape[1]
    if rows_major:
        rb = M // N_DEV
        nb = tm // rb
        out_shape = SDS((N_DEV, rb, N), bf16)
        out_spec = pl.BlockSpec((nb, rb, tn), lambda i, j: (i, 0, j))
    else:
        out_shape = SDS((N_DEV, M, N // N_DEV), bf16)
        assert tn == N // N_DEV
        out_spec = pl.BlockSpec((None, tm, tn), lambda i, j: (j, i, 0))

    def body(a_ref, b_ref, after_ref, o_ref):
        o_ref[...] = _dot(a_ref[...], b_ref[...], TN).astype(bf16).reshape(o_ref.shape)

    return pl.pallas_call(
        body, name=name, grid=(M // tm, N // tn),
        in_specs=[pl.BlockSpec((S, tm), lambda i, j: (0, i)), pl.BlockSpec((S, tn), lambda i, j: (0, j)),
                  pl.BlockSpec(memory_space=pl.ANY)],
        out_specs=out_spec, out_shape=out_shape, compiler_params=_params(),
    )(a, b, after)


def _in_bwd_dh(dproj, w_in_g, after, tm=1024, tn=256):
    S = dproj.shape[0]
    _, D, wc = w_in_g.shape
    tm = min(tm, S)

    def body(dp_ref, w_ref, after_ref, o_ref):
        acc = _dot(dp_ref[:, 0:wc], w_ref[0], NT)
        for k in range(1, N_DEV):
            acc = acc + _dot(dp_ref[:, k * wc:(k + 1) * wc], w_ref[k], NT)
        o_ref[...] = acc

    return pl.pallas_call(
        body, name="in_bwd_dh", grid=(S // tm, D // tn),
        in_specs=[pl.BlockSpec((tm, N_DEV * wc), lambda i, j: (i, 0)),
                  pl.BlockSpec((N_DEV, tn, wc), lambda i, j: (0, j, 0)), pl.BlockSpec(memory_space=pl.ANY)],
        out_specs=pl.BlockSpec((tm, tn), lambda i, j: (i, j)),
        out_shape=SDS((S, D), f32), compiler_params=_params(),
    )(dproj, w_in_g, after)


def _rms_bwd(dh, x, g, l, dx_next, after, tm=256):
    S, D = x.shape

    def body(dh_ref, x_ref, g_ref, dxn_ref, after_ref, dx_ref, dxb_ref, dg_ref):
        @pl.when(pl.program_id(0) == 0)
        def _():
            dg_ref[...] = jnp.zeros_like(dg_ref)

        dh = dh_ref[...]
        xv = x_ref[...]
        r = lax.rsqrt(jnp.mean(xv * xv, axis=-1, keepdims=True) + EPS)
        xhat = xv * r
        dxhat = dh * g_ref[...]
        dx = r * (dxhat - xhat * jnp.mean(dxhat * xhat, axis=-1, keepdims=True)) + dxn_ref[...]
        dx_ref[...] = dx
        dxb_ref[...] = dx.astype(bf16)
        dg_ref[...] += jnp.sum(dh * xhat, axis=0, keepdims=True)

    row = pl.BlockSpec((tm, D), lambda i: (i, 0))
    vec = pl.BlockSpec((1, D), lambda i: (0, 0))
    return pl.pallas_call(
        body, name="rms_bwd", grid=(S // tm,),
        in_specs=[row, row, _layer_spec(g, l), row, pl.BlockSpec(memory_space=pl.ANY)], out_specs=[row, row, vec],
        out_shape=[SDS((S, D), f32), SDS((S, D), bf16), SDS((1, D), f32)], compiler_params=_params(),
    )(dh, x, g, dx_next, after)


def _sgu_fwd(proj, ln_g, ln_b, w_s, b_t, l, after):
    S = proj.shape[0]
    da = A_GROUPS * HEAD_DIM
    D = 2 * da

    def body(u_ref, v_ref, z_ref, lg_ref, lb_ref, w_ref, bt_ref, after_ref, y_ref):
        u = _gelu(u_ref[...])
        v = _gelu(v_ref[...])
        z = z_ref[...]
        mu = jnp.mean(v, axis=-1, keepdims=True)
        xc = v - mu
        rs = lax.rsqrt(jnp.mean(xc * xc, axis=-1, keepdims=True) + EPS)
        vn = (xc * rs * lg_ref[...] + lb_ref[...]).astype(bf16)
        gate = u * (z * jax.nn.sigmoid(z))
        tri = lax.broadcasted_iota(jnp.int32, (CHUNK, CHUNK), 0) >= lax.broadcasted_iota(jnp.int32, (CHUNK, CHUNK), 1)
        for g in range(A_GROUPS):
            sl = slice(g * HEAD_DIM, (g + 1) * HEAD_DIM)
            wm = jnp.where(tri, w_ref[g], 0.0).astype(bf16)
            mixed = _dot(wm, vn[:, sl]) + bt_ref[:, g:g + 1]
            y_ref[:, sl] = (gate[:, sl] * mixed).astype(bf16)

    blk = lambda cb: pl.BlockSpec((CHUNK, da), lambda c: (c, cb))
    full = lambda shp: pl.BlockSpec(shp, lambda c: (0,) * len(shp))
    return pl.pallas_call(
        body, name="sgu_fwd", grid=(S // CHUNK,),
        in_specs=[blk(0), blk(1), blk(2), _layer_spec(ln_g, l), _layer_spec(ln_b, l), _layer_spec(w_s, l),
                  _layer_spec(b_t, l), pl.BlockSpec(memory_space=pl.ANY)],
        out_specs=blk(0), out_shape=SDS((S, D), bf16), compiler_params=_params(),
    )(proj, proj, proj, ln_g, ln_b, w_s, b_t, after)


def _sgu_bwd(proj, dy, ln_g, ln_b, w_s, b_t, l, after):
    S = proj.shape[0]
    da = A_GROUPS * HEAD_DIM
    n_proj = proj.shape[1]

    def body(u_ref, v_ref, z_ref, dy_ref, lg_ref, lb_ref, w_ref, bt_ref, after_ref,
             dp_ref, dw_ref, db_ref, dlg_ref, dlb_ref, dvn_ref):
        @pl.when(pl.program_id(0) == 0)
        def _():
            dw_ref[...] = jnp.zeros_like(dw_ref)
            db_ref[...] = jnp.zeros_like(db_ref)
            dlg_ref[...] = jnp.zeros_like(dlg_ref)
            dlb_ref[...] = jnp.zeros_like(dlb_ref)

        up, vp, z, dy = u_ref[...], v_ref[...], z_ref[...], dy_ref[...]
        u, gu = _gelu_and_grad(up)
        v, gv = _gelu_and_grad(vp)
        s, gs = _silu_and_grad(z)
        mu = jnp.mean(v, axis=-1, keepdims=True)
        xc = v - mu
        rs = lax.rsqrt(jnp.mean(xc * xc, axis=-1, keepdims=True) + EPS)
        vhat = xc * rs
        lg = lg_ref[...]
        vn = (vhat * lg + lb_ref[...]).astype(bf16)
        tri = lax.broadcasted_iota(jnp.int32, (CHUNK, CHUNK), 0) >= lax.broadcasted_iota(jnp.int32, (CHUNK, CHUNK), 1)
        lane = lax.broadcasted_iota(jnp.int32, (CHUNK, HEAD_DIM), 1)
        dys = dy * s
        db = jnp.zeros((CHUNK, HEAD_DIM), f32)
        for g in range(A_GROUPS):
            sl = slice(g * HEAD_DIM, (g + 1) * HEAD_DIM)
            wm = jnp.where(tri, w_ref[g], 0.0).astype(bf16)
            mixed = _dot(wm, vn[:, sl]) + bt_ref[:, g:g + 1]
            dmix = dys[:, sl] * u[:, sl]
            dp_ref[:, sl] = (dys[:, sl] * mixed * gu[:, sl]).astype(bf16)
            dp_ref[:, 2 * da + g * HEAD_DIM:2 * da + (g + 1) * HEAD_DIM] = (
                dy[:, sl] * u[:, sl] * mixed * gs[:, sl]).astype(bf16)
            dmb = dmix.astype(bf16)
            dw_ref[g] += jnp.where(tri, _dot(dmb, vn[:, sl], NT), 0.0)
            dvn_ref[:, sl] = _dot(wm, dmb, TN)
            db = db + jnp.where(lane == g, jnp.sum(dmix, axis=1, keepdims=True), 0.0)
        db_ref[...] += db
        dvn = dvn_ref[...]
        dlg_ref[...] += jnp.sum(dvn * vhat, axis=0, keepdims=True)
        dlb_ref[...] += jnp.sum(dvn, axis=0, keepdims=True)
        dvhat = dvn * lg
        dv = rs * (dvhat - jnp.mean(dvhat, axis=-1, keepdims=True)
                   - vhat * jnp.mean(dvhat * vhat, axis=-1, keepdims=True))
        dp_ref[:, da:2 * da] = (dv * gv).astype(bf16)

    blk = lambda cb: pl.BlockSpec((CHUNK, da), lambda c: (c, cb))
    full = lambda shp: pl.BlockSpec(shp, lambda c: (0,) * len(shp))
    return pl.pallas_call(
        body, name="sgu_bwd", grid=(S // CHUNK,),
        in_specs=[blk(0), blk(1), blk(2), blk(0), _layer_spec(ln_g, l), _layer_spec(ln_b, l), _layer_spec(w_s, l),
                  _layer_spec(b_t, l), pl.BlockSpec(memory_space=pl.ANY)],
        out_specs=[pl.BlockSpec((CHUNK, 3 * da), lambda c: (c, 0)), full((A_GROUPS, CHUNK, CHUNK)),
                   full((CHUNK, HEAD_DIM)), full((1, da)), full((1, da))],
        out_shape=[SDS((S, n_proj), bf16), SDS((A_GROUPS, CHUNK, CHUNK), f32), SDS((CHUNK, HEAD_DIM), f32),
                   SDS((1, da), f32), SDS((1, da), f32)],
        scratch_shapes=[pltpu.VMEM((CHUNK, da), f32)], compiler_params=_params(),
    )(proj, proj, proj, dy, ln_g, ln_b, w_s, b_t, after)


def _sb_scores(q, kblk, kb, rows, cols, masked):
    z = _dot(q, kblk, NT) * (1.0 / math.sqrt(HEAD_DIM))
    t = jnp.log(1.0 + jnp.exp(-jnp.abs(z)))
    log_1mb = -(jnp.maximum(z, 0.0) + t)
    log_beta = jnp.minimum(z, 0.0) - t
    if not masked:
        return None, log_beta, log_1mb
    causal = (cols + kb * TK) < rows
    return causal, log_beta, jnp.where(causal, log_1mb, 0.0)


def _sb_tiles(i):
    rows = i * TQ + lax.broadcasted_iota(jnp.int32, (TQ, TK), 0)
    cols = lax.broadcasted_iota(jnp.int32, (TQ, TK), 1)
    r_i = lax.broadcasted_iota(jnp.int32, (TK, TK), 0)
    c_i = lax.broadcasted_iota(jnp.int32, (TK, TK), 1)
    upper, lower = (r_i > c_i).astype(bf16), (r_i < c_i).astype(bf16)
    slot = lax.broadcasted_iota(jnp.int32, (TQ, CARRY_LANES), 1)
    return rows, cols, slot, jnp.concatenate([upper, upper], axis=0), jnp.concatenate([lower, lower], axis=0)


def _suffix_sum(t, tri):
    hi = lax.bitcast_convert_type(lax.bitcast_convert_type(t, jnp.uint32) & jnp.uint32(0xFFFF0000), f32)
    both = jnp.concatenate([hi.astype(bf16), (t - hi).astype(bf16)], axis=1)
    return _dot(both, tri)


def _sb_fwd(proj, y_prev, col0, after):
    S = proj.shape[0]
    D = y_prev.shape[1]
    dh = N_HEADS * HEAD_DIM
    n_diag = TQ // TK

    def body(q_ref, k_ref, v_ref, z_ref, yp_ref, after_ref, y_ref, o_ref, car_ref, qb, kb_s, vb_s, c_ref):
        i = pl.program_id(0)

        @pl.when(i == 0)
        def _():
            kb_s[...] = k_ref[...].astype(bf16)
            vb_s[...] = v_ref[...].astype(bf16)

        qb[...] = q_ref[...].astype(bf16)
        o_ref[...] = jnp.zeros_like(o_ref)
        c_ref[...] = jnp.zeros_like(c_ref)
        car_ref[...] = jnp.zeros_like(car_ref)
        nkb = (i + 1) * n_diag
        rows, cols, slot, upper, _ = _sb_tiles(i)

        def make_step(masked):
            def step(jj, carry):
                kb = nkb - 1 - jj
                off = pl.multiple_of(kb * TK, TK)
                hs = range(N_HEADS)
                sls = [slice(h * HEAD_DIM, (h + 1) * HEAD_DIM) for h in hs]
                sc = [_sb_scores(qb[:, sls[h]], kb_s[pl.ds(off, TK), sls[h]], kb, rows, cols, masked) for h in hs]
                suf = [_suffix_sum(sc[h][2], upper) for h in hs]
                cs = [c_ref[h] for h in hs]
                es = [jnp.exp(sc[h][1] + suf[h] + cs[h][:, :1]) for h in hs]
                if masked:
                    es = [jnp.where(sc[h][0], es[h], 0.0) for h in hs]
                pv = [_dot(es[h].astype(bf16), vb_s[pl.ds(off, TK), sls[h]]) for h in hs]
                for h in hs:
                    o_ref[:, sls[h]] += pv[h]
                    car_ref[h] = jnp.where(slot == kb, cs[h], car_ref[h])
                    c_ref[h] = cs[h] + jnp.sum(sc[h][2], axis=1, keepdims=True)
                return carry
            return step

        lax.fori_loop(0, n_diag, make_step(True), 0)
        lax.fori_loop(n_diag, nkb, make_step(False), 0)
        z = z_ref[...]
        y_ref[...] = (o_ref[...] * (z * jax.nn.sigmoid(z))).astype(bf16)

    cb = col0 * HEAD_DIM // dh
    qspec = lambda k: pl.BlockSpec((TQ, dh), lambda i: (i, cb + k))
    kspec = lambda k: pl.BlockSpec((S, dh), lambda i: (0, cb + k))
    return pl.pallas_call(
        body, name="sb_fwd", grid=(S // TQ,),
        in_specs=[qspec(0), kspec(1), kspec(2), qspec(3), pl.BlockSpec(memory_space=pl.ANY),
                  pl.BlockSpec(memory_space=pl.ANY)],
        out_specs=[pl.BlockSpec((TQ, dh), lambda i: (i, A_GROUPS * HEAD_DIM // dh)),
                   pl.BlockSpec((TQ, dh), lambda i: (i, 0)),
                   pl.BlockSpec((N_HEADS, TQ, CARRY_LANES), lambda i: (0, i, 0))],
        out_shape=[SDS((S, D), bf16), SDS((S, dh), f32), SDS((N_HEADS, S, CARRY_LANES), f32)],
        input_output_aliases={4: 0},
        scratch_shapes=[pltpu.VMEM((TQ, dh), bf16), pltpu.VMEM((S, dh), bf16), pltpu.VMEM((S, dh), bf16),
                        pltpu.VMEM((N_HEADS, TQ, CARRY_LANES), f32)],
        compiler_params=_params(),
    )(proj, proj, proj, proj, y_prev, after)


def _sb_bwd(proj, o, car, dy, dproj_prev, col0, after):
    S = proj.shape[0]
    n_i = S // TQ
    dh = N_HEADS * HEAD_DIM
    n_diag = TQ // TK
    cb = col0 * HEAD_DIM // dh
    scale = 1.0 / math.sqrt(HEAD_DIM)

    def body(q_ref, k_ref, v_ref, z_ref, o_ref, car_ref, dy_ref, dpp_ref, after_ref,
             dp_ref, qb, kb_s, vb_s, dob, p_ref, dq_acc, dk_acc, dv_acc, st_a, st_b, st_k, st_v):
        i = pl.program_id(0)

        def put(stage_ref, row0, nrows, k):
            pltpu.sync_copy(stage_ref, dp_ref.at[pl.ds(row0, nrows), pl.ds((cb + k) * dh, dh)])

        @pl.when(i == 0)
        def _():
            kb_s[...] = k_ref[...].astype(bf16)
            vb_s[...] = v_ref[...].astype(bf16)
            dk_acc[...] = jnp.zeros_like(dk_acc)
            dv_acc[...] = jnp.zeros_like(dv_acc)

        s, gs = _silu_and_grad(z_ref[...])
        dy = dy_ref[...]
        st_b[...] = (dy * o_ref[...] * gs).astype(bf16)
        dob[...] = (dy * s).astype(bf16)
        qb[...] = q_ref[...].astype(bf16)
        p_ref[...] = jnp.zeros_like(p_ref)
        dq_acc[...] = jnp.zeros_like(dq_acc)
        nkb = (i + 1) * n_diag
        rows, cols, slot, upper, lower = _sb_tiles(i)

        def make_step(masked):
            def step(kb, carry):
                off = pl.multiple_of(kb * TK, TK)
                hs = range(N_HEADS)
                sls = [slice(h * HEAD_DIM, (h + 1) * HEAD_DIM) for h in hs]
                qs = [qb[:, sls[h]] for h in hs]
                ks = [kb_s[pl.ds(off, TK), sls[h]] for h in hs]
                dos = [dob[:, sls[h]] for h in hs]
                sc = [_sb_scores(qs[h], ks[h], kb, rows, cols, masked) for h in hs]
                da = [_dot(dos[h], vb_s[pl.ds(off, TK), sls[h]], NT) for h in hs]
                suf = [_suffix_sum(sc[h][2], upper) for h in hs]
                onehot = slot == kb
                cs = [jnp.sum(jnp.where(onehot, car_ref[h], 0.0), axis=1, keepdims=True) for h in hs]
                es = [jnp.exp(sc[h][1] + suf[h] + cs[h]) for h in hs]
                if masked:
                    es = [jnp.where(sc[h][0], es[h], 0.0) for h in hs]
                gs_ = [da[h] * es[h] for h in hs]
                ps = [p_ref[h] for h in hs]
                pre = [_suffix_sum(gs_[h], lower) + ps[h][:, :1] for h in hs]
                dzs = []
                for h in hs:
                    beta = jnp.exp(sc[h][1])
                    dzz = gs_[h] * (1.0 - beta) - beta * pre[h]
                    if masked:
                        dzz = jnp.where(sc[h][0], dzz, 0.0)
                    dzs.append((dzz * scale).astype(bf16))
                dqs = [_dot(dzs[h], ks[h]) for h in hs]
                dks = [_dot(dzs[h], qs[h], TN) for h in hs]
                dvs = [_dot(es[h].astype(bf16), dos[h], TN) for h in hs]
                for h in hs:
                    dq_acc[:, sls[h]] += dqs[h]
                    dk_acc[pl.ds(off, TK), sls[h]] += dks[h]
                    dv_acc[pl.ds(off, TK), sls[h]] += dvs[h]
                    p_ref[h] = ps[h] + jnp.sum(gs_[h], axis=1, keepdims=True)
                return carry
            return step

        lax.fori_loop(0, nkb - n_diag, make_step(False), 0)
        lax.fori_loop(nkb - n_diag, nkb, make_step(True), 0)
        st_a[...] = dq_acc[...].astype(bf16)
        row0 = pl.multiple_of(i * TQ, TQ)
        put(st_a, row0, TQ, 0)
        put(st_b, row0, TQ, 3)

        @pl.when(i == n_i - 1)
        def _():
            st_k[...] = dk_acc[...].astype(bf16)
            st_v[...] = dv_acc[...].astype(bf16)
            put(st_k, 0, S, 1)
            put(st_v, 0, S, 2)

    qspec = lambda k: pl.BlockSpec((TQ, dh), lambda i: (i, cb + k))
    kspec = lambda k: pl.BlockSpec((S, dh), lambda i: (0, cb + k))
    return pl.pallas_call(
        body, name="sb_bwd", grid=(n_i,),
        in_specs=[qspec(0), kspec(1), kspec(2), qspec(3),
                  pl.BlockSpec((TQ, dh), lambda i: (i, 0)),
                  pl.BlockSpec((N_HEADS, TQ, CARRY_LANES), lambda i: (0, i, 0)),
                  pl.BlockSpec((TQ, dh), lambda i: (i, A_GROUPS * HEAD_DIM // dh)),
                  pl.BlockSpec(memory_space=pl.ANY), pl.BlockSpec(memory_space=pl.ANY)],
        out_specs=pl.BlockSpec(memory_space=pl.ANY),
        out_shape=SDS(dproj_prev.shape, bf16),
        input_output_aliases={7: 0},
        scratch_shapes=[pltpu.VMEM((TQ, dh), bf16), pltpu.VMEM((S, dh), bf16), pltpu.VMEM((S, dh), bf16),
                        pltpu.VMEM((TQ, dh), bf16), pltpu.VMEM((N_HEADS, TQ, CARRY_LANES), f32), pltpu.VMEM((TQ, dh), f32),
                        pltpu.VMEM((S, dh), f32), pltpu.VMEM((S, dh), f32),
                        pltpu.VMEM((TQ, dh), bf16), pltpu.VMEM((TQ, dh), bf16),
                        pltpu.VMEM((S, dh), bf16), pltpu.VMEM((S, dh), bf16)],
        compiler_params=_params(56),
    )(proj, proj, proj, proj, o, car, dy, dproj_prev, after)


def _mem_kv(mem, mg, l, w_kv_g):
    M, D = mem.shape
    rb, n = w_kv_g.shape[1], w_kv_g.shape[2]

    def body(m_ref, g_ref, w_ref, kv_ref):
        mv = m_ref[...]
        r = lax.rsqrt(jnp.mean(mv * mv, axis=-1, keepdims=True) + EPS)
        mh = (mv * r * g_ref[...]).astype(bf16)
        kv_ref[...] = _dot(mh, w_ref[...].reshape(N_DEV * rb, n))

    return pl.pallas_call(
        body, name="mem_kv", grid=(1,),
        in_specs=[pl.BlockSpec((M, D), lambda i: (0, 0)), _layer_spec(mg, l),
                  pl.BlockSpec((N_DEV, rb, n), lambda i: (0, 0, 0))],
        out_specs=pl.BlockSpec((M, n), lambda i: (0, 0)),
        out_shape=SDS((M, n), f32), compiler_params=_params(),
    )(mem, mg, w_kv_g)


def _xattn_head(q_ref, kv_ref, qg, kg, h):
    dc = N_HEADS * HEAD_DIM
    sl = slice(h * HEAD_DIM, (h + 1) * HEAD_DIM)
    qh = q_ref[:, sl]
    rq = lax.rsqrt(jnp.mean(qh * qh, axis=-1, keepdims=True) + EPS)
    qhat = qh * rq
    qn = (qhat * qg).astype(bf16)
    kh = kv_ref[:, sl]
    rk = lax.rsqrt(jnp.mean(kh * kh, axis=-1, keepdims=True) + EPS)
    kn = (kh * rk * kg).astype(bf16)
    vh = kv_ref[:, dc + h * HEAD_DIM:dc + (h + 1) * HEAD_DIM].astype(bf16)
    s = _dot(qn, kn, NT) * (1.0 / math.sqrt(HEAD_DIM))
    e = jnp.exp(s - jnp.max(s, axis=-1, keepdims=True))
    p = e / jnp.sum(e, axis=-1, keepdims=True)
    o = _dot(p.astype(bf16), vh)
    return sl, rq, qhat, qn, kn, vh, p, o


def _xattn_fwd(proj, kv, qg, kg, l, y_prev, col0, tq=512):
    S = proj.shape[0]
    D = y_prev.shape[1]
    dc = N_HEADS * HEAD_DIM
    M = kv.shape[0]

    def body(q_ref, z_ref, kv_ref, qg_ref, kg_ref, yp_ref, y_ref):
        for h in range(N_HEADS):
            sl, _, _, _, _, _, _, o = _xattn_head(q_ref, kv_ref, qg_ref[...], kg_ref[...], h)
            z = z_ref[:, sl]
            y_ref[:, sl] = (o * (z * jax.nn.sigmoid(z))).astype(bf16)

    full = lambda shp: pl.BlockSpec(shp, lambda i: (0,) * len(shp))
    return pl.pallas_call(
        body, name="xattn_fwd", grid=(S // tq,),
        in_specs=[pl.BlockSpec((tq, dc), lambda i: (i, col0)), pl.BlockSpec((tq, dc), lambda i: (i, col0 + 1)),
                  full((M, 2 * dc)), _layer_spec(qg, l), _layer_spec(kg, l), pl.BlockSpec(memory_space=pl.ANY)],
        out_specs=pl.BlockSpec((tq, dc), lambda i: (i, D // dc - 1)),
        out_shape=SDS((S, D), bf16), input_output_aliases={5: 0}, compiler_params=_params(),
    )(proj, proj, kv, qg, kg, y_prev)


def _xattn_bwd(proj, kv, qg, kg, l, dy, dproj_prev, col0, tq=512):
    S = proj.shape[0]
    D = dy.shape[1]
    dc = N_HEADS * HEAD_DIM
    M = kv.shape[0]

    def body(q_ref, z_ref, kv_ref, qg_ref, kg_ref, dy_ref, dpp_ref, dp_ref, dkn_ref, dv_ref, dqg_ref):
        @pl.when(pl.program_id(0) == 0)
        def _():
            dkn_ref[...] = jnp.zeros_like(dkn_ref)
            dv_ref[...] = jnp.zeros_like(dv_ref)
            dqg_ref[...] = jnp.zeros_like(dqg_ref)

        qg = qg_ref[...]
        for h in range(N_HEADS):
            sl, rq, qhat, qn, kn, vh, p, o = _xattn_head(q_ref, kv_ref, qg, kg_ref[...], h)
            s, gs = _silu_and_grad(z_ref[:, sl])
            dyh = dy_ref[:, sl]
            dp_ref[:, dc + h * HEAD_DIM:dc + (h + 1) * HEAD_DIM] = (dyh * o * gs).astype(bf16)
            dob = (dyh * s).astype(bf16)
            dpr = _dot(dob, vh, NT)
            dv_ref[:, sl] += _dot(p.astype(bf16), dob, TN)
            ds = (p * (dpr - jnp.sum(p * dpr, axis=-1, keepdims=True)) * (1.0 / math.sqrt(HEAD_DIM))).astype(bf16)
            dqn = _dot(ds, kn)
            dkn_ref[:, sl] += _dot(ds, qn, TN)
            dqg_ref[...] += jnp.sum(dqn * qhat, axis=0, keepdims=True)
            dqhat = dqn * qg
            dp_ref[:, sl] = (rq * (dqhat - qhat * jnp.mean(dqhat * qhat, axis=-1, keepdims=True))).astype(bf16)

    full = lambda shp: pl.BlockSpec(shp, lambda i: (0,) * len(shp))
    return pl.pallas_call(
        body, name="xattn_bwd", grid=(S // tq,),
        in_specs=[pl.BlockSpec((tq, dc), lambda i: (i, col0)), pl.BlockSpec((tq, dc), lambda i: (i, col0 + 1)),
                  full((M, 2 * dc)), _layer_spec(qg, l), _layer_spec(kg, l),
                  pl.BlockSpec((tq, dc), lambda i: (i, D // dc - 1)), pl.BlockSpec(memory_space=pl.ANY)],
        out_specs=[pl.BlockSpec((tq, 2 * dc), lambda i: (i, col0 // 2)), full((M, dc)), full((M, dc)),
                   full((1, HEAD_DIM))],
        out_shape=[SDS(dproj_prev.shape, bf16), SDS((M, dc), f32), SDS((M, dc), f32), SDS((1, HEAD_DIM), f32)],
        input_output_aliases={6: 0}, compiler_params=_params(),
    )(proj, proj, kv, qg, kg, dy, dproj_prev)


def _mem_bwd(mem, mg, kg, l, kv, dkn, dv, w_kv_g):
    M, D = mem.shape
    rb, n = w_kv_g.shape[1], w_kv_g.shape[2]
    dc = n // 2

    def body(m_ref, g_ref, kv_ref, dkn_ref, dv_ref, kg_ref, w_ref, dw_ref, dmg_ref, dkg_ref, dkv_ref):
        mv = m_ref[...]
        r = lax.rsqrt(jnp.mean(mv * mv, axis=-1, keepdims=True) + EPS)
        mhat = mv * r
        mh = (mhat * g_ref[...]).astype(bf16)
        kg = kg_ref[...]
        dkg = jnp.zeros((1, HEAD_DIM), f32)
        for h in range(N_HEADS):
            sl = slice(h * HEAD_DIM, (h + 1) * HEAD_DIM)
            kh = kv_ref[:, sl]
            rk = lax.rsqrt(jnp.mean(kh * kh, axis=-1, keepdims=True) + EPS)
            khat = kh * rk
            dkn_h = dkn_ref[:, sl]
            dkg = dkg + jnp.sum(dkn_h * khat, axis=0, keepdims=True)
            dkhat = dkn_h * kg
            dkv_ref[:, sl] = (rk * (dkhat - khat * jnp.mean(dkhat * khat, axis=-1, keepdims=True))).astype(bf16)
        dkv_ref[:, dc:] = dv_ref[...].astype(bf16)
        dkg_ref[...] = dkg
        dkv = dkv_ref[...]
        dw_ref[...] = _dot(mh, dkv, TN).astype(bf16).reshape(N_DEV, rb, n)
        dmh = _dot(dkv, w_ref[...].reshape(N_DEV * rb, n), NT)
        dmg_ref[...] = jnp.sum(dmh * mhat, axis=0, keepdims=True)

    full = lambda shp: pl.BlockSpec(shp, lambda i: (0,) * len(shp))
    wspec = full((N_DEV, rb, n))
    return pl.pallas_call(
        body, name="mem_bwd", grid=(1,),
        in_specs=[full((M, D)), _layer_spec(mg, l), full((M, n)), full((M, dc)), full((M, dc)), _layer_spec(kg, l), wspec],
        out_specs=[wspec, full((1, D)), full((1, HEAD_DIM))],
        out_shape=[SDS((N_DEV, rb, n), bf16), SDS((1, D), f32), SDS((1, HEAD_DIM), f32)],
        scratch_shapes=[pltpu.VMEM((M, n), bf16)], compiler_params=_params(),
    )(mem, mg, kv, dkn, dv, kg, w_kv_g)


SMALL = ("norm_g", "sgu_ln_g", "sgu_ln_b", "sgu_w", "sgu_b", "mem_norm_g", "q_norm_g", "k_norm_g")


def _small_rows(like):
    rows = [math.prod(like[n].shape) // 128 for n in SMALL]
    offs = [0]
    for r in rows:
        offs.append(offs[-1] + -(-r // 8) * 8)
    return rows, offs


def _pack_small(parts, offs):
    pieces = []
    for k, n in enumerate(SMALL):
        a = parts[n].reshape(-1, 128)
        pieces.append(jnp.pad(a, ((0, offs[k + 1] - offs[k] - a.shape[0]), (0, 0))))
    return jnp.concatenate(pieces)


def kernel(x, mem, norm_g, w_in, sgu_ln_g, sgu_ln_b, sgu_w, sgu_b, mem_norm_g, w_mem_kv, q_norm_g, k_norm_g, w_out, loss_target, m_norm_g, m_w_in, m_sgu_ln_g, m_sgu_ln_b, m_sgu_w, m_sgu_b, m_mem_norm_g, m_w_mem_kv, m_q_norm_g, m_k_norm_g, m_w_out, v_norm_g, v_w_in, v_sgu_ln_g, v_sgu_ln_b, v_sgu_w, v_sgu_b, v_mem_norm_g, v_w_mem_kv, v_q_norm_g, v_k_norm_g, v_w_out):
    L, D, wc = w_in.shape
    S = x.shape[1]
    da = D // 2
    xs = x.reshape(S, D)
    mems = mem.reshape(mem.shape[1], D)
    tgt = loss_target.reshape(S, D)
    stacked = lambda a: a.reshape(a.shape[0], 1, -1)
    ng, lng, lnb, mg, qg, kg = map(stacked, (norm_g, sgu_ln_g, sgu_ln_b, mem_norm_g, q_norm_g, k_norm_g))
    b_t = jnp.swapaxes(sgu_b, 1, 2)
    sb_col, xa_col = 3 * da // HEAD_DIM, (3 * da + D) // (D // 4)

    ax, ay, ac = lax.axis_index("x"), lax.axis_index("y"), lax.axis_index("c")
    ids = jnp.stack([4 * ax + 2 * ay + ac, 2 * ax + ay, ac]).astype(jnp.int32)
    w_in0_b = _cast_into_slot("cast_w_in", w_in, 0, 512, ids, ids)
    first = _gather3_start("gather_w_in0", [w_in0_b], ids)
    late = first["token"]
    w_b = [(w_in0_b if l == 0 else _cast_into_slot("cast_w_in", w_in, l, 512, ids, late),
            _cast_into_slot("cast_w_kv", w_mem_kv, l, 256, ids, late),
            _cast_into_slot("cast_w_out", w_out, l, 256, ids, late)) for l in range(L)]
    relay = _gather3_relay("gather_w_in0_relay",
                           _split_wait(first, *[a for wl in w_b for a in wl if a is not w_in0_b]), ids)
    in_fwd = _gather2_forward("gather_w_in0_forward", _split_wait(relay, relay["token"]), ids)

    acts = []
    xl = xs
    for l in range(L):
        (w_in_g,) = _split_wait(in_fwd, xl if l else in_fwd["token"])
        rest = _gather3_start(f"gather_w_rest{l}", [w_b[l][1], w_b[l][2]], w_in_g)
        order = rest["token"]
        if l + 1 < L:
            nxt = _gather3_start(f"gather_w_in{l + 1}", [w_b[l + 1][0]], order)
            order = nxt["token"]
        proj, h = _rms_proj(xl, ng, l, w_in_g, order)
        rest_relay = _gather3_relay(f"gather_w_rest{l}_relay", _split_wait(rest, proj), proj)
        y = _sgu_fwd(proj, lng, lnb, sgu_w, b_t, l, rest_relay["token"])
        rest_fwd = _gather2_forward(f"gather_w_rest{l}_forward", _split_wait(rest_relay, y), y)
        order = rest_fwd["token"]
        if l + 1 < L:
            nxt_relay = _gather3_relay(f"gather_w_in{l + 1}_relay", _split_wait(nxt, order), order)
            order = nxt_relay["token"]
        y, o_b, car = _sb_fwd(proj, y, sb_col, order)
        w_kv_g, w_out_g = _split_wait(rest_fwd, o_b)
        if l + 1 < L:
            in_fwd = _gather2_forward(f"gather_w_in{l + 1}_forward", _split_wait(nxt_relay, o_b), o_b)
        kv = _mem_kv(mems, mg, l, w_kv_g)
        y = _xattn_fwd(proj, kv, qg, kg, l, y, xa_col)
        x_next = _out_proj(xl, y, w_out_g, kv)
        acts.append((xl, proj, h, y, o_b, car, kv, w_in_g, w_kv_g, w_out_g))
        xl = x_next

    dx, dxb, loss_part = _loss_and_grad(xl, tgt, 512)
    loss = lax.psum(loss_part[0, 0], ("x", "y", "c"))

    weights = dict(norm_g=norm_g, sgu_ln_g=sgu_ln_g, sgu_ln_b=sgu_ln_b, sgu_w=sgu_w, sgu_b=sgu_b,
                   mem_norm_g=mem_norm_g, q_norm_g=q_norm_g, k_norm_g=k_norm_g)
    moms_m = dict(norm_g=m_norm_g, sgu_ln_g=m_sgu_ln_g, sgu_ln_b=m_sgu_ln_b, sgu_w=m_sgu_w, sgu_b=m_sgu_b,
                  mem_norm_g=m_mem_norm_g, q_norm_g=m_q_norm_g, k_norm_g=m_k_norm_g)
    moms_v = dict(norm_g=v_norm_g, sgu_ln_g=v_sgu_ln_g, sgu_ln_b=v_sgu_ln_b, sgu_w=v_sgu_w, sgu_b=v_sgu_b,
                  mem_norm_g=v_mem_norm_g, q_norm_g=v_q_norm_g, k_norm_g=v_k_norm_g)
    small_rows, small_offs = _small_rows(weights)
    head_rows = D // 128
    assert SMALL[0] == "norm_g" and head_rows % 8 == 0

    seconds = {}
    pending = None
    adam = {"w_out": None, "w_mem_kv": None, "w_in": None}

    def update(lu, order):
        (r_out,) = _scatter_finish(seconds[f"g_out{lu}"], order)
        adam["w_out"] = _sum_adam("adam_w_out", r_out, w_out, m_w_out, v_w_out, lu, adam["w_out"], 128, ids)
        r_kv, r_in = _scatter_finish(seconds[f"g_rest{lu}"], adam["w_out"][0])
        adam["w_mem_kv"] = _sum_adam("adam_w_kv", r_kv, w_mem_kv, m_w_mem_kv, v_w_mem_kv, lu, adam["w_mem_kv"], 256, ids)
        adam["w_in"] = _sum_adam("adam_w_in", r_in, w_in, m_w_in, v_w_in, lu, adam["w_in"], 256, ids)
        return adam["w_in"][0]
    small = {n: [None] * L for n in SMALL}
    for l in reversed(range(L)):
        xl, proj, h, y, o_b, car, kv, w_in_g, w_kv_g, w_out_g = acts[l]
        dy = _out_bwd_dy(dxb, w_out_g)
        order = dy
        if pending is not None:
            seconds[pending[0]] = _scatter2_second_level(pending[0], pending[1], dy, ids)
            order = seconds[pending[0]]["token"]
        g_out = _tn_grad("out_bwd_dw", y, dxb, 512, 512, True, order)
        seconds[f"g_out{l}"] = _scatter1_start(f"scatter_g_out{l}", [g_out], ids)
        dproj, d_sw, d_sb, d_lg, d_lb = _sgu_bwd(proj, dy, lng, lnb, sgu_w, b_t, l, seconds[f"g_out{l}"]["token"])
        dproj = _sb_bwd(proj, o_b, car, dy, dproj, sb_col, d_lb)
        dproj, dkn, dv, d_qg = _xattn_bwd(proj, kv, qg, kg, l, dy, dproj, xa_col)
        g_kv, d_mg, d_kg = _mem_bwd(mems, mg, kg, l, kv, dkn, dv, w_kv_g)
        for n, val in (("sgu_ln_g", d_lg), ("sgu_ln_b", d_lb), ("sgu_w", d_sw), ("sgu_b", d_sb[:, :A_GROUPS].T),
                       ("mem_norm_g", d_mg), ("q_norm_g", d_qg), ("k_norm_g", d_kg)):
            small[n][l] = val.reshape(-1)
        order = d_kg
        if l == 0:
            small["norm_g"][0] = jnp.zeros_like(small["norm_g"][1])
            part = _pack_small({n: jnp.stack(small[n]) for n in SMALL}, small_offs)
            tail = _gather1_start("gather_small_tail", [_into_slot("small_tail_slot", part[head_rows:], ids)], ids)
            order = tail["token"]
        g_in_l = _tn_grad("in_bwd_dw", h, dproj, 1024, wc, False, order)
        first = _scatter2_pair_start(f"scatter_g_rest{l}_pair", [g_kv, g_in_l], ids)
        order = first["token"]
        pending = (f"g_rest{l}", first)
        if l == 0:
            for lu in reversed(range(1, L)):
                order = update(lu, order)
            seconds[pending[0]] = _scatter2_second_level(pending[0], pending[1], order, ids)
            order = seconds[pending[0]]["token"]
        dh = _in_bwd_dh(dproj, w_in_g, order)
        dx, dxb, d_ng = _rms_bwd(dh, xl, ng, l, dx, order)
        small["norm_g"][l] = d_ng.reshape(-1)
    head = _gather1_start("gather_small_head",
                          [_into_slot("small_head_slot", small["norm_g"][0].reshape(head_rows, 128), ids)], ids)

    (r_out,) = _scatter_finish(seconds["g_out0"], head["token"])
    adam["w_out"] = _sum_adam("adam_w_out", r_out, w_out, m_w_out, v_w_out, 0, adam["w_out"], 128, ids)
    (r_tail,) = _split_wait(tail, adam["w_out"][0])
    (r_head,) = _split_wait(head, r_tail)
    as128 = lambda d: [d[n].reshape(-1, 128) for n in SMALL]
    sm = _small_sum_adam(r_head, r_tail, as128(weights), as128(moms_m), as128(moms_v), small_offs)
    r_kv, r_in = _scatter_finish(seconds["g_rest0"], sm[0])
    adam["w_mem_kv"] = _sum_adam("adam_w_kv", r_kv, w_mem_kv, m_w_mem_kv, v_w_mem_kv, 0, adam["w_mem_kv"], 256, ids)
    adam["w_in"] = _sum_adam("adam_w_in", r_in, w_in, m_w_in, v_w_in, 0, adam["w_in"], 256, ids)
    res = dict(adam)
    for p, n in enumerate(SMALL):
        res[n] = [sm[k * len(SMALL) + p].reshape(weights[n].shape) for k in range(4)]

    order = ("norm_g", "w_in", "sgu_ln_g", "sgu_ln_b", "sgu_w", "sgu_b", "mem_norm_g", "w_mem_kv", "q_norm_g",
             "k_norm_g", "w_out")
    outs = [loss, dx.reshape(x.shape)]
    for k in range(4):
        outs += [res[n][k] for n in order]
    return tuple(outs)
```

```python
import functools
import math

import jax
import jax.numpy as jnp
from jax import lax
from jax.experimental import pallas as pl
from jax.experimental.pallas import tpu as pltpu

f32 = jnp.float32
bf16 = jnp.bfloat16
SDS = jax.ShapeDtypeStruct

N_DEV = 8
EPS = 1e-6
CHUNK = 128
A_GROUPS = 8
HEAD_DIM = 128
N_HEADS = 4
TQ = 256
TK = 256
CARRY_LANES = 128
ADAM_LR, ADAM_B1, ADAM_B2, ADAM_EPS, ADAM_WD, ADAM_STEP = 0.001, 0.9, 0.999, 1e-08, 0.01, 10
MIB = 1024 * 1024

NT = (((1,), (1,)), ((), ()))
TN = (((0,), (0,)), ((), ()))


def _params(vmem_mib=48):
    return pltpu.CompilerParams(vmem_limit_bytes=vmem_mib * MIB)


def _gelu_and_grad(x):
    e = lax.erf(x * (1.0 / math.sqrt(2.0)))
    cdf = 0.5 * (1.0 + e)
    pdf = jnp.exp(-0.5 * x * x) * (1.0 / math.sqrt(2.0 * math.pi))
    return x * cdf, cdf + x * pdf


def _gelu(x):
    return 0.5 * x * (1.0 + lax.erf(x * (1.0 / math.sqrt(2.0))))


def _silu_and_grad(z):
    sg = jax.nn.sigmoid(z)
    return z * sg, sg * (1.0 + z * (1.0 - sg))


def _layer_spec(stacked, l):
    rest = stacked.shape[1:]
    return pl.BlockSpec((None,) + rest, lambda *idx: (l,) + (0,) * len(rest))


def _dot(a, b, dims=None):
    if dims is None:
        return jnp.dot(a, b, preferred_element_type=f32)
    return lax.dot_general(a, b, dims, preferred_element_type=f32)


_HBM = pl.BlockSpec(memory_space=pltpu.HBM)
_SEM = pl.BlockSpec(memory_space=pltpu.SEMAPHORE)
_EFFECT = pltpu.SideEffectType.DATAFLOW_SIDE_EFFECTING


def _split_start(name, bufs, n_remote, n_local, build, after):
    nb = len(bufs)

    def body(*refs):
        token = refs[-1]
        locals_, remotes = build(refs[:nb], *refs[nb + 1:nb + 4])
        for cp in locals_ + remotes:
            cp.start()
        token[...] = jnp.zeros_like(token)

    hbm = lambda a: pltpu.with_memory_space_constraint(a, pltpu.HBM)
    outs = pl.pallas_call(
        body, name=name,
        out_shape=(pltpu.SemaphoreType.DMA((n_remote,)), pltpu.SemaphoreType.DMA((n_remote,)),
                   pltpu.SemaphoreType.DMA((max(n_local, 1),)),
                   *[pltpu.HBM(b.shape, b.dtype) for b in bufs], SDS((8, 128), f32)),
        in_specs=[_HBM] * nb + [pl.BlockSpec(memory_space=pl.ANY)],
        out_specs=(_SEM, _SEM, _SEM, *[_HBM] * nb, pl.BlockSpec(memory_space=pltpu.VMEM)),
        input_output_aliases={k: 3 + k for k in range(nb)},
        compiler_params=pltpu.CompilerParams(has_side_effects=_EFFECT),
    )(*[hbm(b) for b in bufs], after)
    return dict(name=name, build=build, sems=outs[:3], bufs=outs[3:3 + nb], token=outs[-1])


def _split_wait(handle, *after):
    build, bufs = handle["build"], handle["bufs"]
    nb = len(bufs)

    def body(*refs):
        locals_, remotes = build(refs[:nb], *refs[nb:nb + 3])
        for cp in remotes:
            cp.wait_recv()
        for cp in remotes:
            cp.wait_send()
        for cp in locals_:
            cp.wait()

    outs = pl.pallas_call(
        body, name=handle["name"] + "_wait",
        out_shape=tuple(pltpu.HBM(b.shape, b.dtype) for b in bufs),
        in_specs=[_HBM] * nb + [_SEM] * 3 + [pl.BlockSpec(memory_space=pl.ANY)] * len(after),
        out_specs=tuple([_HBM] * nb),
        input_output_aliases={k: k for k in range(nb)},
        compiler_params=pltpu.CompilerParams(has_side_effects=_EFFECT),
    )(*bufs, *handle["sems"], *after)
    return list(outs)


def _remote(src, dst, send_sems, recv_sems, k, to):
    return pltpu.make_async_remote_copy(src_ref=src, dst_ref=dst, send_sem=send_sems.at[k], recv_sem=recv_sems.at[k],
                                        device_id=to, device_id_type=pl.DeviceIdType.MESH)


def _other_chips(x, y):
    return [(1 - x, y), (x, 1 - y), (1 - x, 1 - y)]


def _all_peers(x, y, c):
    return [(1 - x if m & 4 else x, 1 - y if m & 2 else y, 1 - c if m & 1 else c) for m in range(1, N_DEV)]


def _gather1_start(name, lands, after):
    def build(refs, send, recv, loc):
        x, y, c = lax.axis_index("x"), lax.axis_index("y"), lax.axis_index("c")
        me = 4 * x + 2 * y + c
        return [], [_remote(d.at[me], d.at[me], send, recv, 7 * a + k, peer)
                    for a, d in enumerate(refs) for k, peer in enumerate(_all_peers(x, y, c))]

    return _split_start(name, list(lands), 7 * len(lands), 0, build, after)


def _scatter1_start(name, srcs, after):
    n = len(srcs)

    def build(refs, send, recv, loc):
        x, y, c = lax.axis_index("x"), lax.axis_index("y"), lax.axis_index("c")
        me = 4 * x + 2 * y + c
        return [], [_remote(refs[a].at[4 * px + 2 * py + pc], refs[n + a].at[me], send, recv, 7 * a + k, (px, py, pc))
                    for a in range(n) for k, (px, py, pc) in enumerate(_all_peers(x, y, c))]

    return _split_start(name, list(srcs) + [lax.empty(s.shape, s.dtype) for s in srcs], 7 * n, 0, build, after)


def _gather2_start(name, lands, after):
    def build(refs, send, recv, loc):
        x, y, c = lax.axis_index("x"), lax.axis_index("y"), lax.axis_index("c")
        me = 4 * x + 2 * y + c
        remotes = []
        for a, d in enumerate(refs):
            remotes.append(_remote(d.at[me], d.at[me], send, recv, 4 * a, (x, y, 1 - c)))
            remotes += [_remote(d.at[me], d.at[me], send, recv, 4 * a + 1 + k, (px, py, c))
                        for k, (px, py) in enumerate(_other_chips(x, y))]
        return [], remotes

    return _split_start(name, list(lands), 4 * len(lands), 0, build, after)


def _gather3_start(name, lands, after):
    def build(refs, send, recv, loc):
        x, y, c = lax.axis_index("x"), lax.axis_index("y"), lax.axis_index("c")
        me = 4 * x + 2 * y + c
        return [], [_remote(d.at[me], d.at[me], send, recv, 3 * a + k, to)
                    for a, d in enumerate(refs) for k, to in enumerate([(x, y, 1 - c), (1 - x, y, c), (x, 1 - y, c)])]

    return _split_start(name, list(lands), 3 * len(lands), 0, build, after)


def _gather3_relay(name, lands, after):
    def build(refs, send, recv, loc):
        x, y, c = lax.axis_index("x"), lax.axis_index("y"), lax.axis_index("c")
        from_x = c == 0
        slot = 4 * jnp.where(from_x, 1 - x, x) + 2 * jnp.where(from_x, y, 1 - y) + c
        to = (jnp.where(from_x, x, 1 - x), jnp.where(from_x, 1 - y, y), c)
        return [], [_remote(d.at[slot], d.at[slot], send, recv, a, to) for a, d in enumerate(refs)]

    return _split_start(name, list(lands), len(lands), 0, build, after)


def _gather2_forward(name, lands, after):
    n = len(lands)

    def build(refs, send, recv, loc):
        x, y, c = lax.axis_index("x"), lax.axis_index("y"), lax.axis_index("c")
        slots = [4 * px + 2 * py + c for px, py in _other_chips(x, y)]
        return [], [_remote(d.at[sl], d.at[sl], send, recv, 3 * a + k, (x, y, 1 - c))
                    for a, d in enumerate(refs) for k, sl in enumerate(slots)]

    return _split_start(name, list(lands), 3 * n, 0, build, after)


def _scatter2_pair_start(name, srcs, after):
    n = len(srcs)

    def build(refs, send, recv, loc):
        x, y, c = lax.axis_index("x"), lax.axis_index("y"), lax.axis_index("c")
        return [], [_remote(refs[a].at[2 * q + 1 - c], refs[n + a].at[q], send, recv, 4 * a + q, (x, y, 1 - c))
                    for a in range(n) for q in range(4)]

    lands = [lax.empty((4,) + s.shape[1:], s.dtype) for s in srcs]
    return _split_start(name, list(srcs) + lands, 4 * n, 0, build, after)


def _scatter2_chip_start(name, pairs, after):
    n = len(pairs)

    def build(refs, send, recv, loc):
        x, y, c = lax.axis_index("x"), lax.axis_index("y"), lax.axis_index("c")
        return [], [_remote(refs[a].at[2 * px + py], refs[n + a].at[2 * x + y], send, recv, 3 * a + k, (px, py, c))
                    for a in range(n) for k, (px, py) in enumerate(_other_chips(x, y))]

    return _split_start(name, list(pairs) + [lax.empty(p.shape, p.dtype) for p in pairs], 3 * n, 0, build, after)


def _pair_sum(name, src, theirs, ids):
    _, R, C = theirs.shape
    tr = min(R, 1024)

    def body(ids_ref, a_ref, b_ref, o_ref):
        o_ref[...] = (a_ref[...].astype(f32) + b_ref[...].astype(f32)).astype(bf16)

    spec = pl.BlockSpec((None, tr, C), lambda q, i, ids: (q, i, 0))
    return pl.pallas_call(
        body, name=name,
        grid_spec=pltpu.PrefetchScalarGridSpec(
            num_scalar_prefetch=1, grid=(4, R // tr),
            in_specs=[pl.BlockSpec((None, tr, C), lambda q, i, ids: (2 * q + ids[2], i, 0)), spec], out_specs=spec),
        out_shape=SDS(theirs.shape, bf16), compiler_params=_params(),
    )(ids, src, theirs)


def _scatter2_second_level(name, first, after, ids):
    outs = _split_wait(first, after)
    n = len(outs) // 2
    pairs = [_pair_sum(f"pair_sum_{name}{a}", outs[a], outs[n + a], ids) for a in range(n)]
    return _scatter2_chip_start(f"scatter_{name}_chip", pairs, ids)


def _scatter_finish(second, after):
    outs = _split_wait(second, after)
    n = len(outs) // 2
    return [(outs[a], outs[n + a]) for a in range(n)]


def _cast_into_slot(name, w, l, tr, ids, after):
    _, R, C = w.shape

    def body(ids_ref, w_ref, after_ref, o_ref):
        o_ref[...] = w_ref[...].astype(bf16)

    return pl.pallas_call(
        body, name=name,
        grid_spec=pltpu.PrefetchScalarGridSpec(
            num_scalar_prefetch=1, grid=(R // tr,),
            in_specs=[pl.BlockSpec((None, tr, C), lambda i, ids: (l, i, 0)), pl.BlockSpec(memory_space=pl.ANY)],
            out_specs=pl.BlockSpec((None, tr, C), lambda i, ids: (ids[0], i, 0))),
        out_shape=SDS((N_DEV, R, C), bf16), compiler_params=_params(),
    )(ids, w, after)


def _into_slot(name, a, ids):
    R, C = a.shape

    def body(ids_ref, a_ref, o_ref):
        o_ref[...] = a_ref[...]

    return pl.pallas_call(
        body, name=name,
        grid_spec=pltpu.PrefetchScalarGridSpec(
            num_scalar_prefetch=1, grid=(1,),
            in_specs=[pl.BlockSpec((R, C), lambda i, ids: (0, 0))],
            out_specs=pl.BlockSpec((None, R, C), lambda i, ids: (ids[0], 0, 0))),
        out_shape=SDS((N_DEV, R, C), f32), compiler_params=_params(),
    )(ids, a)


def _adam_math(w, g, m, v):
    m2 = ADAM_B1 * m + (1.0 - ADAM_B1) * g
    v2 = ADAM_B2 * v + (1.0 - ADAM_B2) * (g * g)
    m_hat = m2 / (1.0 - ADAM_B1 ** ADAM_STEP)
    v_hat = v2 / (1.0 - ADAM_B2 ** ADAM_STEP)
    delta = -ADAM_LR * (m_hat / (jnp.sqrt(v_hat) + ADAM_EPS) + ADAM_WD * w)
    return delta, m2, v2


def _sum_adam(name, pair_recv, w, m, v, l, prev, tr, ids):
    own, recv = pair_recv
    L, R, C = w.shape
    slots = recv.shape[0]
    mine = 0 if slots == N_DEV else 1

    def body(ids_ref, r_ref, own_ref, w_ref, m_ref, v_ref, *rest):
        g_ref, d_ref, m2_ref, v2_ref = rest[-4:]
        terms = [jnp.where(ids_ref[mine] == q, own_ref[...], r_ref[q]).astype(f32) for q in range(slots)]
        g = terms[0]
        for t in terms[1:]:
            g = g + t
        d, m2, v2 = _adam_math(w_ref[...], g, m_ref[...], v_ref[...])
        g_ref[...] = g
        d_ref[...] = d
        m2_ref[...] = m2
        v2_ref[...] = v2

    wspec = pl.BlockSpec((None, tr, C), lambda i, ids: (l, i, 0))
    in_specs = [pl.BlockSpec((slots, tr, C), lambda i, ids: (0, i, 0)),
                pl.BlockSpec((None, tr, C), lambda i, ids: (ids[mine], i, 0)), wspec, wspec, wspec]
    args = [ids, recv, own, w, m, v]
    aliases = {}
    if prev is not None:
        in_specs += [pl.BlockSpec(memory_space=pl.ANY)] * 4
        args += list(prev)
        aliases = {6 + k: k for k in range(4)}
    return pl.pallas_call(
        body, name=name,
        grid_spec=pltpu.PrefetchScalarGridSpec(num_scalar_prefetch=1, grid=(R // tr,), in_specs=in_specs,
                                               out_specs=[wspec] * 4),
        out_shape=[SDS((L, R, C), f32)] * 4, input_output_aliases=aliases, compiler_params=_params(),
    )(*args)


def _small_sum_adam(recv_head, recv_tail, ws, ms, vs, offs):
    n = len(ws)
    r0 = recv_head.shape[1]

    def body(*refs):
        rh, rt = refs[0], refs[1]
        w_refs, m_refs, v_refs = refs[2:2 + n], refs[2 + n:2 + 2 * n], refs[2 + 2 * n:2 + 3 * n]
        outs = refs[2 + 3 * n:]
        for p in range(n):
            lo, hi = offs[p], offs[p] + ws[p].shape[0]
            pieces = []
            if lo < r0:
                pieces.append((rh, lo, 0, min(hi, r0) - lo))
            if hi > r0:
                pieces.append((rt, max(lo, r0) - r0, max(lo, r0) - lo, hi - max(lo, r0)))
            for src, a, b, cnt in pieces:
                g = src[0, a:a + cnt, :]
                for s in range(1, N_DEV):
                    g = g + src[s, a:a + cnt, :]
                d, m2, v2 = _adam_math(w_refs[p][b:b + cnt, :], g, m_refs[p][b:b + cnt, :], v_refs[p][b:b + cnt, :])
                for k, val in enumerate((g, d, m2, v2)):
                    outs[k * n + p][b:b + cnt, :] = val

    return pl.pallas_call(
        body, name="small_sum_adam", out_shape=[SDS(w.shape, f32) for w in ws] * 4, compiler_params=_params(),
    )(recv_head, recv_tail, *ws, *ms, *vs)


def _loss_and_grad(xf, tgt, tm):
    S, D = xf.shape

    def body(x_ref, t_ref, dx_ref, dxb_ref, l_ref):
        i = pl.program_id(0)
        d = x_ref[...] - t_ref[...]
        dx = d * (1.0 / D)
        dx_ref[...] = dx
        dxb_ref[...] = dx.astype(bf16)
        e = d * d
        part = e[:, 0:128]
        for k in range(1, D // 128):
            part = part + e[:, k * 128:(k + 1) * 128]
        part = jnp.sum(part.reshape(tm // 8, 8, 128), axis=0)

        @pl.when(i == 0)
        def _():
            l_ref[...] = jnp.zeros_like(l_ref)

        l_ref[...] += part

        @pl.when(i == pl.num_programs(0) - 1)
        def _():
            tot = jnp.sum(l_ref[...], axis=1, keepdims=True)
            tot = jnp.sum(tot, axis=0, keepdims=True)
            l_ref[...] = jnp.broadcast_to(tot * (0.5 / D), l_ref.shape)

    row = pl.BlockSpec((tm, D), lambda i: (i, 0))
    return pl.pallas_call(
        body, name="loss_grad", grid=(S // tm,),
        in_specs=[row, row], out_specs=[row, row, pl.BlockSpec((8, 128), lambda i: (0, 0))],
        out_shape=[SDS((S, D), f32), SDS((S, D), bf16), SDS((8, 128), f32)], compiler_params=_params(),
    )(xf, tgt)


def _rms_proj(x, g, l, w_in_g, after, tm=1024):
    S, D = x.shape
    wc = w_in_g.shape[2]
    n_out = N_DEV * wc

    def body(x_ref, g_ref, w_ref, after_ref, proj_ref, h_ref):
        @pl.when(pl.program_id(1) == 0)
        def _():
            xv = x_ref[...]
            r = lax.rsqrt(jnp.mean(xv * xv, axis=-1, keepdims=True) + EPS)
            h_ref[...] = (xv * r * g_ref[...]).astype(bf16)

        proj_ref[...] = _dot(h_ref[...], w_ref[...])

    return pl.pallas_call(
        body, name="rms_proj", grid=(S // tm, N_DEV),
        in_specs=[pl.BlockSpec((tm, D), lambda i, j: (i, 0)), _layer_spec(g, l),
                  pl.BlockSpec((None, D, wc), lambda i, j: (j, 0, 0)), pl.BlockSpec(memory_space=pl.ANY)],
        out_specs=[pl.BlockSpec((tm, wc), lambda i, j: (i, j)), pl.BlockSpec((tm, D), lambda i, j: (i, 0))],
        out_shape=[SDS((S, n_out), f32), SDS((S, D), bf16)], compiler_params=_params(),
    )(x, g, w_in_g, after)


def _out_proj(x, y, w_out_g, after, tm=512):
    S, D = x.shape
    rb = w_out_g.shape[1]

    def body(x_ref, y_ref, w_ref, after_ref, o_ref):
        w = w_ref[...].reshape(N_DEV * rb, D)
        o_ref[...] = x_ref[...] + _dot(y_ref[...], w)

    row = pl.BlockSpec((tm, D), lambda i: (i, 0))
    return pl.pallas_call(
        body, name="out_proj", grid=(S // tm,),
        in_specs=[row, row, pl.BlockSpec((N_DEV, rb, D), lambda i: (0, 0, 0)), pl.BlockSpec(memory_space=pl.ANY)],
        out_specs=row, out_shape=SDS((S, D), f32), compiler_params=_params(),
    )(x, y, w_out_g, after)


def _out_bwd_dy(dxb, w_out_g, tm=512):
    S, D = dxb.shape
    rb = w_out_g.shape[1]

    nb = 2

    def body(dx_ref, w_ref, o_ref):
        o_ref[...] = _dot(dx_ref[...], w_ref[...].reshape(nb * rb, D), NT)

    return pl.pallas_call(
        body, name="out_bwd_dy", grid=(S // tm, N_DEV // nb),
        in_specs=[pl.BlockSpec((tm, D), lambda i, j: (i, 0)),
                  pl.BlockSpec((nb, rb, D), lambda i, j: (j, 0, 0))],
        out_specs=pl.BlockSpec((tm, nb * rb), lambda i, j: (i, j)),
        out_shape=SDS((S, D), f32), compiler_params=_params(),
    )(dxb, w_out_g)


def _tn_grad(name, a, b, tm, tn, rows_major, after):
    S, M = a.shape
    N = b.shape[1]
    if rows_major:
        rb = M // N_DEV
        nb = tm // rb
        out_shape = SDS((N_DEV, rb, N), bf16)
        out_spec = pl.BlockSpec((nb, rb, tn), lambda i, j: (i, 0, j))
    else:
        out_shape = SDS((N_DEV, M, N // N_DEV), bf16)
        assert tn == N // N_DEV
        out_spec = pl.BlockSpec((None, tm, tn), lambda i, j: (j, i, 0))

    def body(a_ref, b_ref, after_ref, o_ref):
        o_ref[...] = _dot(a_ref[...], b_ref[...], TN).astype(bf16).reshape(o_ref.shape)

    return pl.pallas_call(
        body, name=name, grid=(M // tm, N // tn),
        in_specs=[pl.BlockSpec((S, tm), lambda i, j: (0, i)), pl.BlockSpec((S, tn), lambda i, j: (0, j)),
                  pl.BlockSpec(memory_space=pl.ANY)],
        out_specs=out_spec, out_shape=out_shape, compiler_params=_params(),
    )(a, b, after)


def _in_bwd_dh(dproj, w_in_g, after, tm=1024, tn=256):
    S = dproj.shape[0]
    _, D, wc = w_in_g.shape
    tm = min(tm, S)

    def body(dp_ref, w_ref, after_ref, o_ref):
        acc = _dot(dp_ref[:, 0:wc], w_ref[0], NT)
        for k in range(1, N_DEV):
            acc = acc + _dot(dp_ref[:, k * wc:(k + 1) * wc], w_ref[k], NT)
        o_ref[...] = acc

    return pl.pallas_call(
        body, name="in_bwd_dh", grid=(S // tm, D // tn),
        in_specs=[pl.BlockSpec((tm, N_DEV * wc), lambda i, j: (i, 0)),
                  pl.BlockSpec((N_DEV, tn, wc), lambda i, j: (0, j, 0)), pl.BlockSpec(memory_space=pl.ANY)],
        out_specs=pl.BlockSpec((tm, tn), lambda i, j: (i, j)),
        out_shape=SDS((S, D), f32), compiler_params=_params(),
    )(dproj, w_in_g, after)


def _rms_bwd(dh, x, g, l, dx_next, after, tm=256):
    S, D = x.shape

    def body(dh_ref, x_ref, g_ref, dxn_ref, after_ref, dx_ref, dxb_ref, dg_ref):
        @pl.when(pl.program_id(0) == 0)
        def _():
            dg_ref[...] = jnp.zeros_like(dg_ref)

        dh = dh_ref[...]
        xv = x_ref[...]
        r = lax.rsqrt(jnp.mean(xv * xv, axis=-1, keepdims=True) + EPS)
        xhat = xv * r
        dxhat = dh * g_ref[...]
        dx = r * (dxhat - xhat * jnp.mean(dxhat * xhat, axis=-1, keepdims=True)) + dxn_ref[...]
        dx_ref[...] = dx
        dxb_ref[...] = dx.astype(bf16)
        dg_ref[...] += jnp.sum(dh * xhat, axis=0, keepdims=True)

    row = pl.BlockSpec((tm, D), lambda i: (i, 0))
    vec = pl.BlockSpec((1, D), lambda i: (0, 0))
    return pl.pallas_call(
        body, name="rms_bwd", grid=(S // tm,),
        in_specs=[row, row, _layer_spec(g, l), row, pl.BlockSpec(memory_space=pl.ANY)], out_specs=[row, row, vec],
        out_shape=[SDS((S, D), f32), SDS((S, D), bf16), SDS((1, D), f32)], compiler_params=_params(),
    )(dh, x, g, dx_next, after)


def _sgu_fwd(proj, ln_g, ln_b, w_s, b_t, l, after):
    S = proj.shape[0]
    da = A_GROUPS * HEAD_DIM
    D = 2 * da

    def body(u_ref, v_ref, z_ref, lg_ref, lb_ref, w_ref, bt_ref, after_ref, y_ref):
        u = _gelu(u_ref[...])
        v = _gelu(v_ref[...])
        z = z_ref[...]
        mu = jnp.mean(v, axis=-1, keepdims=True)
        xc = v - mu
        rs = lax.rsqrt(jnp.mean(xc * xc, axis=-1, keepdims=True) + EPS)
        vn = (xc * rs * lg_ref[...] + lb_ref[...]).astype(bf16)
        gate = u * (z * jax.nn.sigmoid(z))
        tri = lax.broadcasted_iota(jnp.int32, (CHUNK, CHUNK), 0) >= lax.broadcasted_iota(jnp.int32, (CHUNK, CHUNK), 1)
        for g in range(A_GROUPS):
            sl = slice(g * HEAD_DIM, (g + 1) * HEAD_DIM)
            wm = jnp.where(tri, w_ref[g], 0.0).astype(bf16)
            mixed = _dot(wm, vn[:, sl]) + bt_ref[:, g:g + 1]
            y_ref[:, sl] = (gate[:, sl] * mixed).astype(bf16)

    blk = lambda cb: pl.BlockSpec((CHUNK, da), lambda c: (c, cb))
    full = lambda shp: pl.BlockSpec(shp, lambda c: (0,) * len(shp))
    return pl.pallas_call(
        body, name="sgu_fwd", grid=(S // CHUNK,),
        in_specs=[blk(0), blk(1), blk(2), _layer_spec(ln_g, l), _layer_spec(ln_b, l), _layer_spec(w_s, l),
                  _layer_spec(b_t, l), pl.BlockSpec(memory_space=pl.ANY)],
        out_specs=blk(0), out_shape=SDS((S, D), bf16), compiler_params=_params(),
    )(proj, proj, proj, ln_g, ln_b, w_s, b_t, after)


def _sgu_bwd(proj, dy, ln_g, ln_b, w_s, b_t, l, after):
    S = proj.shape[0]
    da = A_GROUPS * HEAD_DIM
    n_proj = proj.shape[1]

    def body(u_ref, v_ref, z_ref, dy_ref, lg_ref, lb_ref, w_ref, bt_ref, after_ref,
             dp_ref, dw_ref, db_ref, dlg_ref, dlb_ref, dvn_ref):
        @pl.when(pl.program_id(0) == 0)
        def _():
            dw_ref[...] = jnp.zeros_like(dw_ref)
            db_ref[...] = jnp.zeros_like(db_ref)
            dlg_ref[...] = jnp.zeros_like(dlg_ref)
            dlb_ref[...] = jnp.zeros_like(dlb_ref)

        up, vp, z, dy = u_ref[...], v_ref[...], z_ref[...], dy_ref[...]
        u, gu = _gelu_and_grad(up)
        v, gv = _gelu_and_grad(vp)
        s, gs = _silu_and_grad(z)
        mu = jnp.mean(v, axis=-1, keepdims=True)
        xc = v - mu
        rs = lax.rsqrt(jnp.mean(xc * xc, axis=-1, keepdims=True) + EPS)
        vhat = xc * rs
        lg = lg_ref[...]
        vn = (vhat * lg + lb_ref[...]).astype(bf16)
        tri = lax.broadcasted_iota(jnp.int32, (CHUNK, CHUNK), 0) >= lax.broadcasted_iota(jnp.int32, (CHUNK, CHUNK), 1)
        lane = lax.broadcasted_iota(jnp.int32, (CHUNK, HEAD_DIM), 1)
        dys = dy * s
        db = jnp.zeros((CHUNK, HEAD_DIM), f32)
        for g in range(A_GROUPS):
            sl = slice(g * HEAD_DIM, (g + 1) * HEAD_DIM)
            wm = jnp.where(tri, w_ref[g], 0.0).astype(bf16)
            mixed = _dot(wm, vn[:, sl]) + bt_ref[:, g:g + 1]
            dmix = dys[:, sl] * u[:, sl]
            dp_ref[:, sl] = (dys[:, sl] * mixed * gu[:, sl]).astype(bf16)
            dp_ref[:, 2 * da + g * HEAD_DIM:2 * da + (g + 1) * HEAD_DIM] = (
                dy[:, sl] * u[:, sl] * mixed * gs[:, sl]).astype(bf16)
            dmb = dmix.astype(bf16)
            dw_ref[g] += jnp.where(tri, _dot(dmb, vn[:, sl], NT), 0.0)
            dvn_ref[:, sl] = _dot(wm, dmb, TN)
            db = db + jnp.where(lane == g, jnp.sum(dmix, axis=1, keepdims=True), 0.0)
        db_ref[...] += db
        dvn = dvn_ref[...]
        dlg_ref[...] += jnp.sum(dvn * vhat, axis=0, keepdims=True)
        dlb_ref[...] += jnp.sum(dvn, axis=0, keepdims=True)
        dvhat = dvn * lg
        dv = rs * (dvhat - jnp.mean(dvhat, axis=-1, keepdims=True)
                   - vhat * jnp.mean(dvhat * vhat, axis=-1, keepdims=True))
        dp_ref[:, da:2 * da] = (dv * gv).astype(bf16)

    blk = lambda cb: pl.BlockSpec((CHUNK, da), lambda c: (c, cb))
    full = lambda shp: pl.BlockSpec(shp, lambda c: (0,) * len(shp))
    return pl.pallas_call(
        body, name="sgu_bwd", grid=(S // CHUNK,),
        in_specs=[blk(0), blk(1), blk(2), blk(0), _layer_spec(ln_g, l), _layer_spec(ln_b, l), _layer_spec(w_s, l),
                  _layer_spec(b_t, l), pl.BlockSpec(memory_space=pl.ANY)],
        out_specs=[pl.BlockSpec((CHUNK, 3 * da), lambda c: (c, 0)), full((A_GROUPS, CHUNK, CHUNK)),
                   full((CHUNK, HEAD_DIM)), full((1, da)), full((1, da))],
        out_shape=[SDS((S, n_proj), bf16), SDS((A_GROUPS, CHUNK, CHUNK), f32), SDS((CHUNK, HEAD_DIM), f32),
                   SDS((1, da), f32), SDS((1, da), f32)],
        scratch_shapes=[pltpu.VMEM((CHUNK, da), f32)], compiler_params=_params(),
    )(proj, proj, proj, dy, ln_g, ln_b, w_s, b_t, after)


def _sb_scores(q, kblk, kb, rows, cols, masked):
    z = _dot(q, kblk, NT) * (1.0 / math.sqrt(HEAD_DIM))
    t = jnp.log(1.0 + jnp.exp(-jnp.abs(z)))
    log_1mb = -(jnp.maximum(z, 0.0) + t)
    log_beta = jnp.minimum(z, 0.0) - t
    if not masked:
        return None, log_beta, log_1mb
    causal = (cols + kb * TK) < rows
    return causal, log_beta, jnp.where(causal, log_1mb, 0.0)


def _sb_tiles(i):
    rows = i * TQ + lax.broadcasted_iota(jnp.int32, (TQ, TK), 0)
    cols = lax.broadcasted_iota(jnp.int32, (TQ, TK), 1)
    r_i = lax.broadcasted_iota(jnp.int32, (TK, TK), 0)
    c_i = lax.broadcasted_iota(jnp.int32, (TK, TK), 1)
    upper, lower = (r_i > c_i).astype(bf16), (r_i < c_i).astype(bf16)
    slot = lax.broadcasted_iota(jnp.int32, (TQ, CARRY_LANES), 1)
    return rows, cols, slot, jnp.concatenate([upper, upper], axis=0), jnp.concatenate([lower, lower], axis=0)


def _suffix_sum(t, tri):
    hi = lax.bitcast_convert_type(lax.bitcast_convert_type(t, jnp.uint32) & jnp.uint32(0xFFFF0000), f32)
    both = jnp.concatenate([hi.astype(bf16), (t - hi).astype(bf16)], axis=1)
    return _dot(both, tri)


def _sb_fwd(proj, y_prev, col0, after):
    S = proj.shape[0]
    D = y_prev.shape[1]
    dh = N_HEADS * HEAD_DIM
    n_diag = TQ // TK

    def body(q_ref, k_ref, v_ref, z_ref, yp_ref, after_ref, y_ref, o_ref, car_ref, qb, kb_s, vb_s, c_ref):
        i = pl.program_id(0)

        @pl.when(i == 0)
        def _():
            kb_s[...] = k_ref[...].astype(bf16)
            vb_s[...] = v_ref[...].astype(bf16)

        qb[...] = q_ref[...].astype(bf16)
        o_ref[...] = jnp.zeros_like(o_ref)
        c_ref[...] = jnp.zeros_like(c_ref)
        car_ref[...] = jnp.zeros_like(car_ref)
        nkb = (i + 1) * n_diag
        rows, cols, slot, upper, _ = _sb_tiles(i)

        def make_step(masked):
            def step(jj, carry):
                kb = nkb - 1 - jj
                off = pl.multiple_of(kb * TK, TK)
                hs = range(N_HEADS)
                sls = [slice(h * HEAD_DIM, (h + 1) * HEAD_DIM) for h in hs]
                sc = [_sb_scores(qb[:, sls[h]], kb_s[pl.ds(off, TK), sls[h]], kb, rows, cols, masked) for h in hs]
                suf = [_suffix_sum(sc[h][2], upper) for h in hs]
                cs = [c_ref[h] for h in hs]
                es = [jnp.exp(sc[h][1] + suf[h] + cs[h][:, :1]) for h in hs]
                if masked:
                    es = [jnp.where(sc[h][0], es[h], 0.0) for h in hs]
                pv = [_dot(es[h].astype(bf16), vb_s[pl.ds(off, TK), sls[h]]) for h in hs]
                for h in hs:
                    o_ref[:, sls[h]] += pv[h]
                    car_ref[h] = jnp.where(slot == kb, cs[h], car_ref[h])
                    c_ref[h] = cs[h] + jnp.sum(sc[h][2], axis=1, keepdims=True)
                return carry
            return step

        lax.fori_loop(0, n_diag, make_step(True), 0)
        lax.fori_loop(n_diag, nkb, make_step(False), 0)
        z = z_ref[...]
        y_ref[...] = (o_ref[...] * (z * jax.nn.sigmoid(z))).astype(bf16)

    cb = col0 * HEAD_DIM // dh
    qspec = lambda k: pl.BlockSpec((TQ, dh), lambda i: (i, cb + k))
    kspec = lambda k: pl.BlockSpec((S, dh), lambda i: (0, cb + k))
    return pl.pallas_call(
        body, name="sb_fwd", grid=(S // TQ,),
        in_specs=[qspec(0), kspec(1), kspec(2), qspec(3), pl.BlockSpec(memory_space=pl.ANY),
                  pl.BlockSpec(memory_space=pl.ANY)],
        out_specs=[pl.BlockSpec((TQ, dh), lambda i: (i, A_GROUPS * HEAD_DIM // dh)),
                   pl.BlockSpec((TQ, dh), lambda i: (i, 0)),
                   pl.BlockSpec((N_HEADS, TQ, CARRY_LANES), lambda i: (0, i, 0))],
        out_shape=[SDS((S, D), bf16), SDS((S, dh), f32), SDS((N_HEADS, S, CARRY_LANES), f32)],
        input_output_aliases={4: 0},
        scratch_shapes=[pltpu.VMEM((TQ, dh), bf16), pltpu.VMEM((S, dh), bf16), pltpu.VMEM((S, dh), bf16),
                        pltpu.VMEM((N_HEADS, TQ, CARRY_LANES), f32)],
        compiler_params=_params(),
    )(proj, proj, proj, proj, y_prev, after)


def _sb_bwd(proj, o, car, dy, dproj_prev, col0, after):
    S = proj.shape[0]
    n_i = S // TQ
    dh = N_HEADS * HEAD_DIM
    n_diag = TQ // TK
    cb = col0 * HEAD_DIM // dh
    scale = 1.0 / math.sqrt(HEAD_DIM)

    def body(q_ref, k_ref, v_ref, z_ref, o_ref, car_ref, dy_ref, dpp_ref, after_ref,
             dp_ref, qb, kb_s, vb_s, dob, p_ref, dq_acc, dk_acc, dv_acc, st_a, st_b, st_k, st_v):
        i = pl.program_id(0)

        def put(stage_ref, row0, nrows, k):
            pltpu.sync_copy(stage_ref, dp_ref.at[pl.ds(row0, nrows), pl.ds((cb + k) * dh, dh)])

        @pl.when(i == 0)
        def _():
            kb_s[...] = k_ref[...].astype(bf16)
            vb_s[...] = v_ref[...].astype(bf16)
            dk_acc[...] = jnp.zeros_like(dk_acc)
            dv_acc[...] = jnp.zeros_like(dv_acc)

        s, gs = _silu_and_grad(z_ref[...])
        dy = dy_ref[...]
        st_b[...] = (dy * o_ref[...] * gs).astype(bf16)
        dob[...] = (dy * s).astype(bf16)
        qb[...] = q_ref[...].astype(bf16)
        p_ref[...] = jnp.zeros_like(p_ref)
        dq_acc[...] = jnp.zeros_like(dq_acc)
        nkb = (i + 1) * n_diag
        rows, cols, slot, upper, lower = _sb_tiles(i)

        def make_step(masked):
            def step(kb, carry):
                off = pl.multiple_of(kb * TK, TK)
                hs = range(N_HEADS)
                sls = [slice(h * HEAD_DIM, (h + 1) * HEAD_DIM) for h in hs]
                qs = [qb[:, sls[h]] for h in hs]
                ks = [kb_s[pl.ds(off, TK), sls[h]] for h in hs]
                dos = [dob[:, sls[h]] for h in hs]
                sc = [_sb_scores(qs[h], ks[h], kb, rows, cols, masked) for h in hs]
                da = [_dot(dos[h], vb_s[pl.ds(off, TK), sls[h]], NT) for h in hs]
                suf = [_suffix_sum(sc[h][2], upper) for h in hs]
                onehot = slot == kb
                cs = [jnp.sum(jnp.where(onehot, car_ref[h], 0.0), axis=1, keepdims=True) for h in hs]
                es = [jnp.exp(sc[h][1] + suf[h] + cs[h]) for h in hs]
                if masked:
                    es = [jnp.where(sc[h][0], es[h], 0.0) for h in hs]
                gs_ = [da[h] * es[h] for h in hs]
                ps = [p_ref[h] for h in hs]
                pre = [_suffix_sum(gs_[h], lower) + ps[h][:, :1] for h in hs]
                dzs = []
                for h in hs:
                    beta = jnp.exp(sc[h][1])
                    dzz = gs_[h] * (1.0 - beta) - beta * pre[h]
                    if masked:
                        dzz = jnp.where(sc[h][0], dzz, 0.0)
                    dzs.append((dzz * scale).astype(bf16))
                dqs = [_dot(dzs[h], ks[h]) for h in hs]
                dks = [_dot(dzs[h], qs[h], TN) for h in hs]
                dvs = [_dot(es[h].astype(bf16), dos[h], TN) for h in hs]
                for h in hs:
                    dq_acc[:, sls[h]] += dqs[h]
                    dk_acc[pl.ds(off, TK), sls[h]] += dks[h]
                    dv_acc[pl.ds(off, TK), sls[h]] += dvs[h]
                    p_ref[h] = ps[h] + jnp.sum(gs_[h], axis=1, keepdims=True)
                return carry
            return step

        lax.fori_loop(0, nkb - n_diag, make_step(False), 0)
        lax.fori_loop(nkb - n_diag, nkb, make_step(True), 0)
        st_a[...] = dq_acc[...].astype(bf16)
        row0 = pl.multiple_of(i * TQ, TQ)
        put(st_a, row0, TQ, 0)
        put(st_b, row0, TQ, 3)

        @pl.when(i == n_i - 1)
        def _():
            st_k[...] = dk_acc[...].astype(bf16)
            st_v[...] = dv_acc[...].astype(bf16)
            put(st_k, 0, S, 1)
            put(st_v, 0, S, 2)

    qspec = lambda k: pl.BlockSpec((TQ, dh), lambda i: (i, cb + k))
    kspec = lambda k: pl.BlockSpec((S, dh), lambda i: (0, cb + k))
    return pl.pallas_call(
        body, name="sb_bwd", grid=(n_i,),
        in_specs=[qspec(0), kspec(1), kspec(2), qspec(3),
                  pl.BlockSpec((TQ, dh), lambda i: (i, 0)),
                  pl.BlockSpec((N_HEADS, TQ, CARRY_LANES), lambda i: (0, i, 0)),
                  pl.BlockSpec((TQ, dh), lambda i: (i, A_GROUPS * HEAD_DIM // dh)),
                  pl.BlockSpec(memory_space=pl.ANY), pl.BlockSpec(memory_space=pl.ANY)],
        out_specs=pl.BlockSpec(memory_space=pl.ANY),
        out_shape=SDS(dproj_prev.shape, bf16),
        input_output_aliases={7: 0},
        scratch_shapes=[pltpu.VMEM((TQ, dh), bf16), pltpu.VMEM((S, dh), bf16), pltpu.VMEM((S, dh), bf16),
                        pltpu.VMEM((TQ, dh), bf16), pltpu.VMEM((N_HEADS, TQ, CARRY_LANES), f32), pltpu.VMEM((TQ, dh), f32),
                        pltpu.VMEM((S, dh), f32), pltpu.VMEM((S, dh), f32),
                        pltpu.VMEM((TQ, dh), bf16), pltpu.VMEM((TQ, dh), bf16),
                        pltpu.VMEM((S, dh), bf16), pltpu.VMEM((S, dh), bf16)],
        compiler_params=_params(56),
    )(proj, proj, proj, proj, o, car, dy, dproj_prev, after)


def _mem_kv(mem, mg, l, w_kv_g):
    M, D = mem.shape
    rb, n = w_kv_g.shape[1], w_kv_g.shape[2]

    def body(m_ref, g_ref, w_ref, kv_ref):
        mv = m_ref[...]
        r = lax.rsqrt(jnp.mean(mv * mv, axis=-1, keepdims=True) + EPS)
        mh = (mv * r * g_ref[...]).astype(bf16)
        kv_ref[...] = _dot(mh, w_ref[...].reshape(N_DEV * rb, n))

    return pl.pallas_call(
        body, name="mem_kv", grid=(1,),
        in_specs=[pl.BlockSpec((M, D), lambda i: (0, 0)), _layer_spec(mg, l),
                  pl.BlockSpec((N_DEV, rb, n), lambda i: (0, 0, 0))],
        out_specs=pl.BlockSpec((M, n), lambda i: (0, 0)),
        out_shape=SDS((M, n), f32), compiler_params=_params(),
    )(mem, mg, w_kv_g)


def _xattn_head(q_ref, kv_ref, qg, kg, h):
    dc = N_HEADS * HEAD_DIM
    sl = slice(h * HEAD_DIM, (h + 1) * HEAD_DIM)
    qh = q_ref[:, sl]
    rq = lax.rsqrt(jnp.mean(qh * qh, axis=-1, keepdims=True) + EPS)
    qhat = qh * rq
    qn = (qhat * qg).astype(bf16)
    kh = kv_ref[:, sl]
    rk = lax.rsqrt(jnp.mean(kh * kh, axis=-1, keepdims=True) + EPS)
    kn = (kh * rk * kg).astype(bf16)
    vh = kv_ref[:, dc + h * HEAD_DIM:dc + (h + 1) * HEAD_DIM].astype(bf16)
    s = _dot(qn, kn, NT) * (1.0 / math.sqrt(HEAD_DIM))
    e = jnp.exp(s - jnp.max(s, axis=-1, keepdims=True))
    p = e / jnp.sum(e, axis=-1, keepdims=True)
    o = _dot(p.astype(bf16), vh)
    return sl, rq, qhat, qn, kn, vh, p, o


def _xattn_fwd(proj, kv, qg, kg, l, y_prev, col0, tq=512):
    S = proj.shape[0]
    D = y_prev.shape[1]
    dc = N_HEADS * HEAD_DIM
    M = kv.shape[0]

    def body(q_ref, z_ref, kv_ref, qg_ref, kg_ref, yp_ref, y_ref):
        for h in range(N_HEADS):
            sl, _, _, _, _, _, _, o = _xattn_head(q_ref, kv_ref, qg_ref[...], kg_ref[...], h)
            z = z_ref[:, sl]
            y_ref[:, sl] = (o * (z * jax.nn.sigmoid(z))).astype(bf16)

    full = lambda shp: pl.BlockSpec(shp, lambda i: (0,) * len(shp))
    return pl.pallas_call(
        body, name="xattn_fwd", grid=(S // tq,),
        in_specs=[pl.BlockSpec((tq, dc), lambda i: (i, col0)), pl.BlockSpec((tq, dc), lambda i: (i, col0 + 1)),
                  full((M, 2 * dc)), _layer_spec(qg, l), _layer_spec(kg, l), pl.BlockSpec(memory_space=pl.ANY)],
        out_specs=pl.BlockSpec((tq, dc), lambda i: (i, D // dc - 1)),
        out_shape=SDS((S, D), bf16), input_output_aliases={5: 0}, compiler_params=_params(),
    )(proj, proj, kv, qg, kg, y_prev)


def _xattn_bwd(proj, kv, qg, kg, l, dy, dproj_prev, col0, tq=512):
    S = proj.shape[0]
    D = dy.shape[1]
    dc = N_HEADS * HEAD_DIM
    M = kv.shape[0]

    def body(q_ref, z_ref, kv_ref, qg_ref, kg_ref, dy_ref, dpp_ref, dp_ref, dkn_ref, dv_ref, dqg_ref):
        @pl.when(pl.program_id(0) == 0)
        def _():
            dkn_ref[...] = jnp.zeros_like(dkn_ref)
            dv_ref[...] = jnp.zeros_like(dv_ref)
            dqg_ref[...] = jnp.zeros_like(dqg_ref)

        qg = qg_ref[...]
        for h in range(N_HEADS):
            sl, rq, qhat, qn, kn, vh, p, o = _xattn_head(q_ref, kv_ref, qg, kg_ref[...], h)
            s, gs = _silu_and_grad(z_ref[:, sl])
            dyh = dy_ref[:, sl]
            dp_ref[:, dc + h * HEAD_DIM:dc + (h + 1) * HEAD_DIM] = (dyh * o * gs).astype(bf16)
            dob = (dyh * s).astype(bf16)
            dpr = _dot(dob, vh, NT)
            dv_ref[:, sl] += _dot(p.astype(bf16), dob, TN)
            ds = (p * (dpr - jnp.sum(p * dpr, axis=-1, keepdims=True)) * (1.0 / math.sqrt(HEAD_DIM))).astype(bf16)
            dqn = _dot(ds, kn)
            dkn_ref[:, sl] += _dot(ds, qn, TN)
            dqg_ref[...] += jnp.sum(dqn * qhat, axis=0, keepdims=True)
            dqhat = dqn * qg
            dp_ref[:, sl] = (rq * (dqhat - qhat * jnp.mean(dqhat * qhat, axis=-1, keepdims=True))).astype(bf16)

    full = lambda shp: pl.BlockSpec(shp, lambda i: (0,) * len(shp))
    return pl.pallas_call(
        body, name="xattn_bwd", grid=(S // tq,),
        in_specs=[pl.BlockSpec((tq, dc), lambda i: (i, col0)), pl.BlockSpec((tq, dc), lambda i: (i, col0 + 1)),
                  full((M, 2 * dc)), _layer_spec(qg, l), _layer_spec(kg, l),
                  pl.BlockSpec((tq, dc), lambda i: (i, D // dc - 1)), pl.BlockSpec(memory_space=pl.ANY)],
        out_specs=[pl.BlockSpec((tq, 2 * dc), lambda i: (i, col0 // 2)), full((M, dc)), full((M, dc)),
                   full((1, HEAD_DIM))],
        out_shape=[SDS(dproj_prev.shape, bf16), SDS((M, dc), f32), SDS((M, dc), f32), SDS((1, HEAD_DIM), f32)],
        input_output_aliases={6: 0}, compiler_params=_params(),
    )(proj, proj, kv, qg, kg, dy, dproj_prev)


def _mem_bwd(mem, mg, kg, l, kv, dkn, dv, w_kv_g):
    M, D = mem.shape
    rb, n = w_kv_g.shape[1], w_kv_g.shape[2]
    dc = n // 2

    def body(m_ref, g_ref, kv_ref, dkn_ref, dv_ref, kg_ref, w_ref, dw_ref, dmg_ref, dkg_ref, dkv_ref):
        mv = m_ref[...]
        r = lax.rsqrt(jnp.mean(mv * mv, axis=-1, keepdims=True) + EPS)
        mhat = mv * r
        mh = (mhat * g_ref[...]).astype(bf16)
        kg = kg_ref[...]
        dkg = jnp.zeros((1, HEAD_DIM), f32)
        for h in range(N_HEADS):
            sl = slice(h * HEAD_DIM, (h + 1) * HEAD_DIM)
            kh = kv_ref[:, sl]
            rk = lax.rsqrt(jnp.mean(kh * kh, axis=-1, keepdims=True) + EPS)
            khat = kh * rk
            dkn_h = dkn_ref[:, sl]
            dkg = dkg + jnp.sum(dkn_h * khat, axis=0, keepdims=True)
            dkhat = dkn_h * kg
            dkv_ref[:, sl] = (rk * (dkhat - khat * jnp.mean(dkhat * khat, axis=-1, keepdims=True))).astype(bf16)
        dkv_ref[:, dc:] = dv_ref[...].astype(bf16)
        dkg_ref[...] = dkg
        dkv = dkv_ref[...]
        dw_ref[...] = _dot(mh, dkv, TN).astype(bf16).reshape(N_DEV, rb, n)
        dmh = _dot(dkv, w_ref[...].reshape(N_DEV * rb, n), NT)
        dmg_ref[...] = jnp.sum(dmh * mhat, axis=0, keepdims=True)

    full = lambda shp: pl.BlockSpec(shp, lambda i: (0,) * len(shp))
    wspec = full((N_DEV, rb, n))
    return pl.pallas_call(
        body, name="mem_bwd", grid=(1,),
        in_specs=[full((M, D)), _layer_spec(mg, l), full((M, n)), full((M, dc)), full((M, dc)), _layer_spec(kg, l), wspec],
        out_specs=[wspec, full((1, D)), full((1, HEAD_DIM))],
        out_shape=[SDS((N_DEV, rb, n), bf16), SDS((1, D), f32), SDS((1, HEAD_DIM), f32)],
        scratch_shapes=[pltpu.VMEM((M, n), bf16)], compiler_params=_params(),
    )(mem, mg, kv, dkn, dv, kg, w_kv_g)


SMALL = ("norm_g", "sgu_ln_g", "sgu_ln_b", "sgu_w", "sgu_b", "mem_norm_g", "q_norm_g", "k_norm_g")


def _small_rows(like):
    rows = [math.prod(like[n].shape) // 128 for n in SMALL]
    offs = [0]
    for r in rows:
        offs.append(offs[-1] + -(-r // 8) * 8)
    return rows, offs


def _pack_small(parts, offs):
    pieces = []
    for k, n in enumerate(SMALL):
        a = parts[n].reshape(-1, 128)
        pieces.append(jnp.pad(a, ((0, offs[k + 1] - offs[k] - a.shape[0]), (0, 0))))
    return jnp.concatenate(pieces)


def kernel(x, mem, norm_g, w_in, sgu_ln_g, sgu_ln_b, sgu_w, sgu_b, mem_norm_g, w_mem_kv, q_norm_g, k_norm_g, w_out, loss_target, m_norm_g, m_w_in, m_sgu_ln_g, m_sgu_ln_b, m_sgu_w, m_sgu_b, m_mem_norm_g, m_w_mem_kv, m_q_norm_g, m_k_norm_g, m_w_out, v_norm_g, v_w_in, v_sgu_ln_g, v_sgu_ln_b, v_sgu_w, v_sgu_b, v_mem_norm_g, v_w_mem_kv, v_q_norm_g, v_k_norm_g, v_w_out):
    L, D, wc = w_in.shape
    S = x.shape[1]
    da = D // 2
    xs = x.reshape(S, D)
    mems = mem.reshape(mem.shape[1], D)
    tgt = loss_target.reshape(S, D)
    stacked = lambda a: a.reshape(a.shape[0], 1, -1)
    ng, lng, lnb, mg, qg, kg = map(stacked, (norm_g, sgu_ln_g, sgu_ln_b, mem_norm_g, q_norm_g, k_norm_g))
    b_t = jnp.swapaxes(sgu_b, 1, 2)
    sb_col, xa_col = 3 * da // HEAD_DIM, (3 * da + D) // (D // 4)

    ax, ay, ac = lax.axis_index("x"), lax.axis_index("y"), lax.axis_index("c")
    ids = jnp.stack([4 * ax + 2 * ay + ac, 2 * ax + ay, ac]).astype(jnp.int32)
    w_in0_b = _cast_into_slot("cast_w_in", w_in, 0, 512, ids, ids)
    first = _gather3_start("gather_w_in0", [w_in0_b], ids)
    late = first["token"]
    w_b = [(w_in0_b if l == 0 else _cast_into_slot("cast_w_in", w_in, l, 512, ids, late),
            _cast_into_slot("cast_w_kv", w_mem_kv, l, 256, ids, late),
            _cast_into_slot("cast_w_out", w_out, l, 256, ids, late)) for l in range(L)]
    relay = _gather3_relay("gather_w_in0_relay",
                           _split_wait(first, *[a for wl in w_b for a in wl if a is not w_in0_b]), ids)
    in_fwd = _gather2_forward("gather_w_in0_forward", _split_wait(relay, relay["token"]), ids)

    acts = []
    xl = xs
    for l in range(L):
        (w_in_g,) = _split_wait(in_fwd, xl if l else in_fwd["token"])
        rest = _gather3_start(f"gather_w_rest{l}", [w_b[l][1], w_b[l][2]], w_in_g)
        order = rest["token"]
        if l + 1 < L:
            nxt = _gather3_start(f"gather_w_in{l + 1}", [w_b[l + 1][0]], order)
            order = nxt["token"]
        proj, h = _rms_proj(xl, ng, l, w_in_g, order)
        rest_relay = _gather3_relay(f"gather_w_rest{l}_relay", _split_wait(rest, proj), proj)
        y = _sgu_fwd(proj, lng, lnb, sgu_w, b_t, l, rest_relay["token"])
        rest_fwd = _gather2_forward(f"gather_w_rest{l}_forward", _split_wait(rest_relay, y), y)
        order = rest_fwd["token"]
        if l + 1 < L:
            nxt_relay = _gather3_relay(f"gather_w_in{l + 1}_relay", _split_wait(nxt, order), order)
            order = nxt_relay["token"]
        y, o_b, car = _sb_fwd(proj, y, sb_col, order)
        w_kv_g, w_out_g = _split_wait(rest_fwd, o_b)
        order = o_b
        if l + 1 < L:
            in_fwd = _gather2_forward(f"gather_w_in{l + 1}_forward", _split_wait(nxt_relay, o_b), o_b)
            order = in_fwd["token"]
        kv = _mem_kv(mems, mg, l, w_kv_g)
        y = _xattn_fwd(proj, kv, qg, kg, l, y, xa_col)
        x_next = _out_proj(xl, y, w_out_g, order)
        acts.append((xl, proj, h, y, o_b, car, kv, w_in_g, w_kv_g, w_out_g))
        xl = x_next

    dx, dxb, loss_part = _loss_and_grad(xl, tgt, 512)
    loss = lax.psum(loss_part[0, 0], ("x", "y", "c"))

    weights = dict(norm_g=norm_g, sgu_ln_g=sgu_ln_g, sgu_ln_b=sgu_ln_b, sgu_w=sgu_w, sgu_b=sgu_b,
                   mem_norm_g=mem_norm_g, q_norm_g=q_norm_g, k_norm_g=k_norm_g)
    moms_m = dict(norm_g=m_norm_g, sgu_ln_g=m_sgu_ln_g, sgu_ln_b=m_sgu_ln_b, sgu_w=m_sgu_w, sgu_b=m_sgu_b,
                  mem_norm_g=m_mem_norm_g, q_norm_g=m_q_norm_g, k_norm_g=m_k_norm_g)
    moms_v = dict(norm_g=v_norm_g, sgu_ln_g=v_sgu_ln_g, sgu_ln_b=v_sgu_ln_b, sgu_w=v_sgu_w, sgu_b=v_sgu_b,
                  mem_norm_g=v_mem_norm_g, q_norm_g=v_q_norm_g, k_norm_g=v_k_norm_g)
    small_rows, small_offs = _small_rows(weights)
    head_rows = D // 128
    assert SMALL[0] == "norm_g" and head_rows % 8 == 0

    seconds = {}
    pending = None
    adam = {"w_out": None, "w_mem_kv": None, "w_in": None}

    def update(lu, order):
        (r_out,) = _scatter_finish(seconds[f"g_out{lu}"], order)
        adam["w_out"] = _sum_adam("adam_w_out", r_out, w_out, m_w_out, v_w_out, lu, adam["w_out"], 128, ids)
        r_kv, r_in = _scatter_finish(seconds[f"g_rest{lu}"], adam["w_out"][0])
        adam["w_mem_kv"] = _sum_adam("adam_w_kv", r_kv, w_mem_kv, m_w_mem_kv, v_w_mem_kv, lu, adam["w_mem_kv"], 256, ids)
        adam["w_in"] = _sum_adam("adam_w_in", r_in, w_in, m_w_in, v_w_in, lu, adam["w_in"], 256, ids)
        return adam["w_in"][0]
    small = {n: [None] * L for n in SMALL}
    for l in reversed(range(L)):
        xl, proj, h, y, o_b, car, kv, w_in_g, w_kv_g, w_out_g = acts[l]
        dy = _out_bwd_dy(dxb, w_out_g)
        order = dy
        if pending is not None:
            seconds[pending[0]] = _scatter2_second_level(pending[0], pending[1], dy, ids)
            order = seconds[pending[0]]["token"]
        g_out = _tn_grad("out_bwd_dw", y, dxb, 512, 512, True, order)
        seconds[f"g_out{l}"] = _scatter1_start(f"scatter_g_out{l}", [g_out], ids)
        dproj, d_sw, d_sb, d_lg, d_lb = _sgu_bwd(proj, dy, lng, lnb, sgu_w, b_t, l, seconds[f"g_out{l}"]["token"])
        dproj = _sb_bwd(proj, o_b, car, dy, dproj, sb_col, d_lb)
        dproj, dkn, dv, d_qg = _xattn_bwd(proj, kv, qg, kg, l, dy, dproj, xa_col)
        g_kv, d_mg, d_kg = _mem_bwd(mems, mg, kg, l, kv, dkn, dv, w_kv_g)
        for n, val in (("sgu_ln_g", d_lg), ("sgu_ln_b", d_lb), ("sgu_w", d_sw), ("sgu_b", d_sb[:, :A_GROUPS].T),
                       ("mem_norm_g", d_mg), ("q_norm_g", d_qg), ("k_norm_g", d_kg)):
            small[n][l] = val.reshape(-1)
        order = d_kg
        if l == 0:
            small["norm_g"][0] = jnp.zeros_like(small["norm_g"][1])
            part = _pack_small({n: jnp.stack(small[n]) for n in SMALL}, small_offs)
            tail = _gather1_start("gather_small_tail", [_into_slot("small_tail_slot", part[head_rows:], ids)], ids)
            order = tail["token"]
        g_in_l = _tn_grad("in_bwd_dw", h, dproj, 1024, wc, False, order)
        first = _scatter2_pair_start(f"scatter_g_rest{l}_pair", [g_kv, g_in_l], ids)
        order = first["token"]
        pending = (f"g_rest{l}", first)
        if l == 0:
            for lu in reversed(range(1, L)):
                order = update(lu, order)
            seconds[pending[0]] = _scatter2_second_level(pending[0], pending[1], order, ids)
            order = seconds[pending[0]]["token"]
        dh = _in_bwd_dh(dproj, w_in_g, order)
        dx, dxb, d_ng = _rms_bwd(dh, xl, ng, l, dx, order)
        small["norm_g"][l] = d_ng.reshape(-1)
    head = _gather1_start("gather_small_head",
                          [_into_slot("small_head_slot", small["norm_g"][0].reshape(head_rows, 128), ids)], ids)

    (r_out,) = _scatter_finish(seconds["g_out0"], head["token"])
    adam["w_out"] = _sum_adam("adam_w_out", r_out, w_out, m_w_out, v_w_out, 0, adam["w_out"], 128, ids)
    r_kv, r_in = _scatter_finish(seconds["g_rest0"], adam["w_out"][0])
    adam["w_mem_kv"] = _sum_adam("adam_w_kv", r_kv, w_mem_kv, m_w_mem_kv, v_w_mem_kv, 0, adam["w_mem_kv"], 256, ids)
    adam["w_in"] = _sum_adam("adam_w_in", r_in, w_in, m_w_in, v_w_in, 0, adam["w_in"], 256, ids)
    (r_tail,) = _split_wait(tail, adam["w_in"][0])
    (r_head,) = _split_wait(head, r_tail)
    as128 = lambda d: [d[n].reshape(-1, 128) for n in SMALL]
    sm = _small_sum_adam(r_head, r_tail, as128(weights), as128(moms_m), as128(moms_v), small_offs)
    res = dict(adam)
    for p, n in enumerate(SMALL):
        res[n] = [sm[k * len(SMALL) + p].reshape(weights[n].shape) for k in range(4)]

    order = ("norm_g", "w_in", "sgu_ln_g", "sgu_ln_b", "sgu_w", "sgu_b", "mem_norm_g", "w_mem_kv", "q_norm_g",
             "k_norm_g", "w_out")
    outs = [loss, dx.reshape(x.shape)]
    for k in range(4):
        outs += [res[n][k] for n in order]
    return tuple(outs)
```

```python
import functools
import math

import jax
import jax.numpy as jnp
from jax import lax
from jax.experimental import pallas as pl
from jax.experimental.pallas import tpu as pltpu

f32 = jnp.float32
bf16 = jnp.bfloat16
SDS = jax.ShapeDtypeStruct

N_DEV = 8
EPS = 1e-6
CHUNK = 128
A_GROUPS = 8
HEAD_DIM = 128
N_HEADS = 4
TQ = 256
TK = 256
CARRY_LANES = 128
ADAM_LR, ADAM_B1, ADAM_B2, ADAM_EPS, ADAM_WD, ADAM_STEP = 0.001, 0.9, 0.999, 1e-08, 0.01, 10
MIB = 1024 * 1024

NT = (((1,), (1,)), ((), ()))
TN = (((0,), (0,)), ((), ()))


def _params(vmem_mib=48):
    return pltpu.CompilerParams(vmem_limit_bytes=vmem_mib * MIB)


def _gelu_and_grad(x):
    e = lax.erf(x * (1.0 / math.sqrt(2.0)))
    cdf = 0.5 * (1.0 + e)
    pdf = jnp.exp(-0.5 * x * x) * (1.0 / math.sqrt(2.0 * math.pi))
    return x * cdf, cdf + x * pdf


def _gelu(x):
    return 0.5 * x * (1.0 + lax.erf(x * (1.0 / math.sqrt(2.0))))


def _silu_and_grad(z):
    sg = jax.nn.sigmoid(z)
    return z * sg, sg * (1.0 + z * (1.0 - sg))


def _layer_spec(stacked, l):
    rest = stacked.shape[1:]
    return pl.BlockSpec((None,) + rest, lambda *idx: (l,) + (0,) * len(rest))


def _dot(a, b, dims=None):
    if dims is None:
        return jnp.dot(a, b, preferred_element_type=f32)
    return lax.dot_general(a, b, dims, preferred_element_type=f32)


_HBM = pl.BlockSpec(memory_space=pltpu.HBM)
_SEM = pl.BlockSpec(memory_space=pltpu.SEMAPHORE)
_EFFECT = pltpu.SideEffectType.DATAFLOW_SIDE_EFFECTING


def _split_start(name, bufs, n_remote, n_local, build, after):
    nb = len(bufs)

    def body(*refs):
        token = refs[-1]
        locals_, remotes = build(refs[:nb], *refs[nb + 1:nb + 4])
        for cp in locals_ + remotes:
            cp.start()
        token[...] = jnp.zeros_like(token)

    hbm = lambda a: pltpu.with_memory_space_constraint(a, pltpu.HBM)
    outs = pl.pallas_call(
        body, name=name,
        out_shape=(pltpu.SemaphoreType.DMA((n_remote,)), pltpu.SemaphoreType.DMA((n_remote,)),
                   pltpu.SemaphoreType.DMA((max(n_local, 1),)),
                   *[pltpu.HBM(b.shape, b.dtype) for b in bufs], SDS((8, 128), f32)),
        in_specs=[_HBM] * nb + [pl.BlockSpec(memory_space=pl.ANY)],
        out_specs=(_SEM, _SEM, _SEM, *[_HBM] * nb, pl.BlockSpec(memory_space=pltpu.VMEM)),
        input_output_aliases={k: 3 + k for k in range(nb)},
        compiler_params=pltpu.CompilerParams(has_side_effects=_EFFECT),
    )(*[hbm(b) for b in bufs], after)
    return dict(name=name, build=build, sems=outs[:3], bufs=outs[3:3 + nb], token=outs[-1])


def _split_wait(handle, *after):
    build, bufs = handle["build"], handle["bufs"]
    nb = len(bufs)

    def body(*refs):
        locals_, remotes = build(refs[:nb], *refs[nb:nb + 3])
        for cp in remotes:
            cp.wait_recv()
        for cp in remotes:
            cp.wait_send()
        for cp in locals_:
            cp.wait()

    outs = pl.pallas_call(
        body, name=handle["name"] + "_wait",
        out_shape=tuple(pltpu.HBM(b.shape, b.dtype) for b in bufs),
        in_specs=[_HBM] * nb + [_SEM] * 3 + [pl.BlockSpec(memory_space=pl.ANY)] * len(after),
        out_specs=tuple([_HBM] * nb),
        input_output_aliases={k: k for k in range(nb)},
        compiler_params=pltpu.CompilerParams(has_side_effects=_EFFECT),
    )(*bufs, *handle["sems"], *after)
    return list(outs)


def _remote(src, dst, send_sems, recv_sems, k, to):
    return pltpu.make_async_remote_copy(src_ref=src, dst_ref=dst, send_sem=send_sems.at[k], recv_sem=recv_sems.at[k],
                                        device_id=to, device_id_type=pl.DeviceIdType.MESH)


def _other_chips(x, y):
    return [(1 - x, y), (x, 1 - y), (1 - x, 1 - y)]


def _all_peers(x, y, c):
    return [(1 - x if m & 4 else x, 1 - y if m & 2 else y, 1 - c if m & 1 else c) for m in range(1, N_DEV)]


def _gather1_start(name, lands, after):
    def build(refs, send, recv, loc):
        x, y, c = lax.axis_index("x"), lax.axis_index("y"), lax.axis_index("c")
        me = 4 * x + 2 * y + c
        return [], [_remote(d.at[me], d.at[me], send, recv, 7 * a + k, peer)
                    for a, d in enumerate(refs) for k, peer in enumerate(_all_peers(x, y, c))]

    return _split_start(name, list(lands), 7 * len(lands), 0, build, after)


def _scatter1_start(name, srcs, after):
    n = len(srcs)

    def build(refs, send, recv, loc):
        x, y, c = lax.axis_index("x"), lax.axis_index("y"), lax.axis_index("c")
        me = 4 * x + 2 * y + c
        return [], [_remote(refs[a].at[4 * px + 2 * py + pc], refs[n + a].at[me], send, recv, 7 * a + k, (px, py, pc))
                    for a in range(n) for k, (px, py, pc) in enumerate(_all_peers(x, y, c))]

    return _split_start(name, list(srcs) + [lax.empty(s.shape, s.dtype) for s in srcs], 7 * n, 0, build, after)


def _gather2_start(name, lands, after):
    def build(refs, send, recv, loc):
        x, y, c = lax.axis_index("x"), lax.axis_index("y"), lax.axis_index("c")
        me = 4 * x + 2 * y + c
        remotes = []
        for a, d in enumerate(refs):
            remotes.append(_remote(d.at[me], d.at[me], send, recv, 4 * a, (x, y, 1 - c)))
            remotes += [_remote(d.at[me], d.at[me], send, recv, 4 * a + 1 + k, (px, py, c))
                        for k, (px, py) in enumerate(_other_chips(x, y))]
        return [], remotes

    return _split_start(name, list(lands), 4 * len(lands), 0, build, after)


def _gather3_start(name, lands, after):
    def build(refs, send, recv, loc):
        x, y, c = lax.axis_index("x"), lax.axis_index("y"), lax.axis_index("c")
        me = 4 * x + 2 * y + c
        return [], [_remote(d.at[me], d.at[me], send, recv, 3 * a + k, to)
                    for a, d in enumerate(refs) for k, to in enumerate([(x, y, 1 - c), (1 - x, y, c), (x, 1 - y, c)])]

    return _split_start(name, list(lands), 3 * len(lands), 0, build, after)


def _gather3_relay(name, lands, after):
    def build(refs, send, recv, loc):
        x, y, c = lax.axis_index("x"), lax.axis_index("y"), lax.axis_index("c")
        from_x = c == 0
        slot = 4 * jnp.where(from_x, 1 - x, x) + 2 * jnp.where(from_x, y, 1 - y) + c
        to = (jnp.where(from_x, x, 1 - x), jnp.where(from_x, 1 - y, y), c)
        return [], [_remote(d.at[slot], d.at[slot], send, recv, a, to) for a, d in enumerate(refs)]

    return _split_start(name, list(lands), len(lands), 0, build, after)


def _gather2_forward(name, lands, after):
    n = len(lands)

    def build(refs, send, recv, loc):
        x, y, c = lax.axis_index("x"), lax.axis_index("y"), lax.axis_index("c")
        slots = [4 * px + 2 * py + c for px, py in _other_chips(x, y)]
        return [], [_remote(d.at[sl], d.at[sl], send, recv, 3 * a + k, (x, y, 1 - c))
                    for a, d in enumerate(refs) for k, sl in enumerate(slots)]

    return _split_start(name, list(lands), 3 * n, 0, build, after)


def _scatter2_pair_start(name, srcs, after):
    n = len(srcs)

    def build(refs, send, recv, loc):
        x, y, c = lax.axis_index("x"), lax.axis_index("y"), lax.axis_index("c")
        return [], [_remote(refs[a].at[2 * q + 1 - c], refs[n + a].at[q], send, recv, 4 * a + q, (x, y, 1 - c))
                    for a in range(n) for q in range(4)]

    lands = [lax.empty((4,) + s.shape[1:], s.dtype) for s in srcs]
    return _split_start(name, list(srcs) + lands, 4 * n, 0, build, after)


def _scatter2_chip_start(name, pairs, after):
    n = len(pairs)

    def build(refs, send, recv, loc):
        x, y, c = lax.axis_index("x"), lax.axis_index("y"), lax.axis_index("c")
        return [], [_remote(refs[a].at[2 * px + py], refs[n + a].at[2 * x + y], send, recv, 3 * a + k, (px, py, c))
                    for a in range(n) for k, (px, py) in enumerate(_other_chips(x, y))]

    return _split_start(name, list(pairs) + [lax.empty(p.shape, p.dtype) for p in pairs], 3 * n, 0, build, after)


def _pair_sum(name, src, theirs, ids):
    _, R, C = theirs.shape
    tr = min(R, 1024)

    def body(ids_ref, a_ref, b_ref, o_ref):
        o_ref[...] = (a_ref[...].astype(f32) + b_ref[...].astype(f32)).astype(bf16)

    spec = pl.BlockSpec((None, tr, C), lambda q, i, ids: (q, i, 0))
    return pl.pallas_call(
        body, name=name,
        grid_spec=pltpu.PrefetchScalarGridSpec(
            num_scalar_prefetch=1, grid=(4, R // tr),
            in_specs=[pl.BlockSpec((None, tr, C), lambda q, i, ids: (2 * q + ids[2], i, 0)), spec], out_specs=spec),
        out_shape=SDS(theirs.shape, bf16), compiler_params=_params(),
    )(ids, src, theirs)


def _scatter2_second_level(name, first, after, ids):
    outs = _split_wait(first, after)
    n = len(outs) // 2
    pairs = [_pair_sum(f"pair_sum_{name}{a}", outs[a], outs[n + a], ids) for a in range(n)]
    return _scatter2_chip_start(f"scatter_{name}_chip", pairs, ids)


def _scatter_finish(second, after):
    outs = _split_wait(second, after)
    n = len(outs) // 2
    return [(outs[a], outs[n + a]) for a in range(n)]


def _cast_into_slot(name, w, l, tr, ids, after):
    _, R, C = w.shape

    def body(ids_ref, w_ref, after_ref, o_ref):
        o_ref[...] = w_ref[...].astype(bf16)

    return pl.pallas_call(
        body, name=name,
        grid_spec=pltpu.PrefetchScalarGridSpec(
            num_scalar_prefetch=1, grid=(R // tr,),
            in_specs=[pl.BlockSpec((None, tr, C), lambda i, ids: (l, i, 0)), pl.BlockSpec(memory_space=pl.ANY)],
            out_specs=pl.BlockSpec((None, tr, C), lambda i, ids: (ids[0], i, 0))),
        out_shape=SDS((N_DEV, R, C), bf16), compiler_params=_params(),
    )(ids, w, after)


def _into_slot(name, a, ids):
    R, C = a.shape

    def body(ids_ref, a_ref, o_ref):
        o_ref[...] = a_ref[...]

    return pl.pallas_call(
        body, name=name,
        grid_spec=pltpu.PrefetchScalarGridSpec(
            num_scalar_prefetch=1, grid=(1,),
            in_specs=[pl.BlockSpec((R, C), lambda i, ids: (0, 0))],
            out_specs=pl.BlockSpec((None, R, C), lambda i, ids: (ids[0], 0, 0))),
        out_shape=SDS((N_DEV, R, C), f32), compiler_params=_params(),
    )(ids, a)


def _adam_math(w, g, m, v):
    m2 = ADAM_B1 * m + (1.0 - ADAM_B1) * g
    v2 = ADAM_B2 * v + (1.0 - ADAM_B2) * (g * g)
    m_hat = m2 / (1.0 - ADAM_B1 ** ADAM_STEP)
    v_hat = v2 / (1.0 - ADAM_B2 ** ADAM_STEP)
    delta = -ADAM_LR * (m_hat / (jnp.sqrt(v_hat) + ADAM_EPS) + ADAM_WD * w)
    return delta, m2, v2


def _sum_adam(name, pair_recv, w, m, v, l, prev, tr, ids, after):
    own, recv = pair_recv
    L, R, C = w.shape
    slots = recv.shape[0]
    mine = 0 if slots == N_DEV else 1

    def body(ids_ref, r_ref, own_ref, w_ref, m_ref, v_ref, after_ref, *rest):
        g_ref, d_ref, m2_ref, v2_ref = rest[-4:]
        terms = [jnp.where(ids_ref[mine] == q, own_ref[...], r_ref[q]).astype(f32) for q in range(slots)]
        g = terms[0]
        for t in terms[1:]:
            g = g + t
        d, m2, v2 = _adam_math(w_ref[...], g, m_ref[...], v_ref[...])
        g_ref[...] = g
        d_ref[...] = d
        m2_ref[...] = m2
        v2_ref[...] = v2

    wspec = pl.BlockSpec((None, tr, C), lambda i, ids: (l, i, 0))
    in_specs = [pl.BlockSpec((slots, tr, C), lambda i, ids: (0, i, 0)),
                pl.BlockSpec((None, tr, C), lambda i, ids: (ids[mine], i, 0)), wspec, wspec, wspec,
                pl.BlockSpec(memory_space=pl.ANY)]
    args = [ids, recv, own, w, m, v, after]
    aliases = {}
    if prev is not None:
        in_specs += [pl.BlockSpec(memory_space=pl.ANY)] * 4
        args += list(prev)
        aliases = {7 + k: k for k in range(4)}
    return pl.pallas_call(
        body, name=name,
        grid_spec=pltpu.PrefetchScalarGridSpec(num_scalar_prefetch=1, grid=(R // tr,), in_specs=in_specs,
                                               out_specs=[wspec] * 4),
        out_shape=[SDS((L, R, C), f32)] * 4, input_output_aliases=aliases, compiler_params=_params(),
    )(*args)


def _small_sum_adam(recv_head, recv_tail, ws, ms, vs, offs):
    n = len(ws)
    r0 = recv_head.shape[1]

    def body(*refs):
        rh, rt = refs[0], refs[1]
        w_refs, m_refs, v_refs = refs[2:2 + n], refs[2 + n:2 + 2 * n], refs[2 + 2 * n:2 + 3 * n]
        outs = refs[2 + 3 * n:]
        for p in range(n):
            lo, hi = offs[p], offs[p] + ws[p].shape[0]
            pieces = []
            if lo < r0:
                pieces.append((rh, lo, 0, min(hi, r0) - lo))
            if hi > r0:
                pieces.append((rt, max(lo, r0) - r0, max(lo, r0) - lo, hi - max(lo, r0)))
            for src, a, b, cnt in pieces:
                g = src[0, a:a + cnt, :]
                for s in range(1, N_DEV):
                    g = g + src[s, a:a + cnt, :]
                d, m2, v2 = _adam_math(w_refs[p][b:b + cnt, :], g, m_refs[p][b:b + cnt, :], v_refs[p][b:b + cnt, :])
                for k, val in enumerate((g, d, m2, v2)):
                    outs[k * n + p][b:b + cnt, :] = val

    return pl.pallas_call(
        body, name="small_sum_adam", out_shape=[SDS(w.shape, f32) for w in ws] * 4, compiler_params=_params(),
    )(recv_head, recv_tail, *ws, *ms, *vs)


def _loss_and_grad(xf, tgt, tm):
    S, D = xf.shape

    def body(x_ref, t_ref, dx_ref, dxb_ref, l_ref):
        i = pl.program_id(0)
        d = x_ref[...] - t_ref[...]
        dx = d * (1.0 / D)
        dx_ref[...] = dx
        dxb_ref[...] = dx.astype(bf16)
        e = d * d
        part = e[:, 0:128]
        for k in range(1, D // 128):
            part = part + e[:, k * 128:(k + 1) * 128]
        part = jnp.sum(part.reshape(tm // 8, 8, 128), axis=0)

        @pl.when(i == 0)
        def _():
            l_ref[...] = jnp.zeros_like(l_ref)

        l_ref[...] += part

        @pl.when(i == pl.num_programs(0) - 1)
        def _():
            tot = jnp.sum(l_ref[...], axis=1, keepdims=True)
            tot = jnp.sum(tot, axis=0, keepdims=True)
            l_ref[...] = jnp.broadcast_to(tot * (0.5 / D), l_ref.shape)

    row = pl.BlockSpec((tm, D), lambda i: (i, 0))
    return pl.pallas_call(
        body, name="loss_grad", grid=(S // tm,),
        in_specs=[row, row], out_specs=[row, row, pl.BlockSpec((8, 128), lambda i: (0, 0))],
        out_shape=[SDS((S, D), f32), SDS((S, D), bf16), SDS((8, 128), f32)], compiler_params=_params(),
    )(xf, tgt)


def _rms_proj(x, g, l, w_in_g, after, tm=1024):
    S, D = x.shape
    wc = w_in_g.shape[2]
    n_out = N_DEV * wc

    def body(x_ref, g_ref, w_ref, after_ref, proj_ref, h_ref):
        @pl.when(pl.program_id(1) == 0)
        def _():
            xv = x_ref[...]
            r = lax.rsqrt(jnp.mean(xv * xv, axis=-1, keepdims=True) + EPS)
            h_ref[...] = (xv * r * g_ref[...]).astype(bf16)

        proj_ref[...] = _dot(h_ref[...], w_ref[...])

    return pl.pallas_call(
        body, name="rms_proj", grid=(S // tm, N_DEV),
        in_specs=[pl.BlockSpec((tm, D), lambda i, j: (i, 0)), _layer_spec(g, l),
                  pl.BlockSpec((None, D, wc), lambda i, j: (j, 0, 0)), pl.BlockSpec(memory_space=pl.ANY)],
        out_specs=[pl.BlockSpec((tm, wc), lambda i, j: (i, j)), pl.BlockSpec((tm, D), lambda i, j: (i, 0))],
        out_shape=[SDS((S, n_out), f32), SDS((S, D), bf16)], compiler_params=_params(),
    )(x, g, w_in_g, after)


def _out_proj(x, y, w_out_g, after, tm=512):
    S, D = x.shape
    rb = w_out_g.shape[1]

    def body(x_ref, y_ref, w_ref, after_ref, o_ref):
        w = w_ref[...].reshape(N_DEV * rb, D)
        o_ref[...] = x_ref[...] + _dot(y_ref[...], w)

    row = pl.BlockSpec((tm, D), lambda i: (i, 0))
    return pl.pallas_call(
        body, name="out_proj", grid=(S // tm,),
        in_specs=[row, row, pl.BlockSpec((N_DEV, rb, D), lambda i: (0, 0, 0)), pl.BlockSpec(memory_space=pl.ANY)],
        out_specs=row, out_shape=SDS((S, D), f32), compiler_params=_params(),
    )(x, y, w_out_g, after)


def _out_bwd_dy(dxb, w_out_g, tm=512):
    S, D = dxb.shape
    rb = w_out_g.shape[1]

    nb = 2

    def body(dx_ref, w_ref, o_ref):
        o_ref[...] = _dot(dx_ref[...], w_ref[...].reshape(nb * rb, D), NT)

    return pl.pallas_call(
        body, name="out_bwd_dy", grid=(S // tm, N_DEV // nb),
        in_specs=[pl.BlockSpec((tm, D), lambda i, j: (i, 0)),
                  pl.BlockSpec((nb, rb, D), lambda i, j: (j, 0, 0))],
        out_specs=pl.BlockSpec((tm, nb * rb), lambda i, j: (i, j)),
        out_shape=SDS((S, D), f32), compiler_params=_params(),
    )(dxb, w_out_g)


def _tn_grad(name, a, b, tm, tn, rows_major, after):
    S, M = a.shape
    N = b.shape[1]
    if rows_major:
        rb = M // N_DEV
        nb = tm // rb
        out_shape = SDS((N_DEV, rb, N), bf16)
        out_spec = pl.BlockSpec((nb, rb, tn), lambda i, j: (i, 0, j))
    else:
        out_shape = SDS((N_DEV, M, N // N_DEV), bf16)
        assert tn == N // N_DEV
        out_spec = pl.BlockSpec((None, tm, tn), lambda i, j: (j, i, 0))

    def body(a_ref, b_ref, after_ref, o_ref):
        o_ref[...] = _dot(a_ref[...], b_ref[...], TN).astype(bf16).reshape(o_ref.shape)

    return pl.pallas_call(
        body, name=name, grid=(M // tm, N // tn),
        in_specs=[pl.BlockSpec((S, tm), lambda i, j: (0, i)), pl.BlockSpec((S, tn), lambda i, j: (0, j)),
                  pl.BlockSpec(memory_space=pl.ANY)],
        out_specs=out_spec, out_shape=out_shape, compiler_params=_params(),
    )(a, b, after)


def _in_bwd_dh(dproj, w_in_g, after, tm=1024, tn=256):
    S = dproj.shape[0]
    _, D, wc = w_in_g.shape
    tm = min(tm, S)

    def body(dp_ref, w_ref, after_ref, o_ref):
        acc = _dot(dp_ref[:, 0:wc], w_ref[0], NT)
        for k in range(1, N_DEV):
            acc = acc + _dot(dp_ref[:, k * wc:(k + 1) * wc], w_ref[k], NT)
        o_ref[...] = acc

    return pl.pallas_call(
        body, name="in_bwd_dh", grid=(S // tm, D // tn),
        in_specs=[pl.BlockSpec((tm, N_DEV * wc), lambda i, j: (i, 0)),
                  pl.BlockSpec((N_DEV, tn, wc), lambda i, j: (0, j, 0)), pl.BlockSpec(memory_space=pl.ANY)],
        out_specs=pl.BlockSpec((tm, tn), lambda i, j: (i, j)),
        out_shape=SDS((S, D), f32), compiler_params=_params(),
    )(dproj, w_in_g, after)


def _rms_bwd(dh, x, g, l, dx_next, after, tm=256):
    S, D = x.shape

    def body(dh_ref, x_ref, g_ref, dxn_ref, after_ref, dx_ref, dxb_ref, dg_ref):
        @pl.when(pl.program_id(0) == 0)
        def _():
            dg_ref[...] = jnp.zeros_like(dg_ref)

        dh = dh_ref[...]
        xv = x_ref[...]
        r = lax.rsqrt(jnp.mean(xv * xv, axis=-1, keepdims=True) + EPS)
        xhat = xv * r
        dxhat = dh * g_ref[...]
        dx = r * (dxhat - xhat * jnp.mean(dxhat * xhat, axis=-1, keepdims=True)) + dxn_ref[...]
        dx_ref[...] = dx
        dxb_ref[...] = dx.astype(bf16)
        dg_ref[...] += jnp.sum(dh * xhat, axis=0, keepdims=True)

    row = pl.BlockSpec((tm, D), lambda i: (i, 0))
    vec = pl.BlockSpec((1, D), lambda i: (0, 0))
    return pl.pallas_call(
        body, name="rms_bwd", grid=(S // tm,),
        in_specs=[row, row, _layer_spec(g, l), row, pl.BlockSpec(memory_space=pl.ANY)], out_specs=[row, row, vec],
        out_shape=[SDS((S, D), f32), SDS((S, D), bf16), SDS((1, D), f32)], compiler_params=_params(),
    )(dh, x, g, dx_next, after)


def _sgu_fwd(proj, ln_g, ln_b, w_s, b_t, l, after):
    S = proj.shape[0]
    da = A_GROUPS * HEAD_DIM
    D = 2 * da

    def body(u_ref, v_ref, z_ref, lg_ref, lb_ref, w_ref, bt_ref, after_ref, y_ref):
        u = _gelu(u_ref[...])
        v = _gelu(v_ref[...])
        z = z_ref[...]
        mu = jnp.mean(v, axis=-1, keepdims=True)
        xc = v - mu
        rs = lax.rsqrt(jnp.mean(xc * xc, axis=-1, keepdims=True) + EPS)
        vn = (xc * rs * lg_ref[...] + lb_ref[...]).astype(bf16)
        gate = u * (z * jax.nn.sigmoid(z))
        tri = lax.broadcasted_iota(jnp.int32, (CHUNK, CHUNK), 0) >= lax.broadcasted_iota(jnp.int32, (CHUNK, CHUNK), 1)
        for g in range(A_GROUPS):
            sl = slice(g * HEAD_DIM, (g + 1) * HEAD_DIM)
            wm = jnp.where(tri, w_ref[g], 0.0).astype(bf16)
            mixed = _dot(wm, vn[:, sl]) + bt_ref[:, g:g + 1]
            y_ref[:, sl] = (gate[:, sl] * mixed).astype(bf16)

    blk = lambda cb: pl.BlockSpec((CHUNK, da), lambda c: (c, cb))
    full = lambda shp: pl.BlockSpec(shp, lambda c: (0,) * len(shp))
    return pl.pallas_call(
        body, name="sgu_fwd", grid=(S // CHUNK,),
        in_specs=[blk(0), blk(1), blk(2), _layer_spec(ln_g, l), _layer_spec(ln_b, l), _layer_spec(w_s, l),
                  _layer_spec(b_t, l), pl.BlockSpec(memory_space=pl.ANY)],
        out_specs=blk(0), out_shape=SDS((S, D), bf16), compiler_params=_params(),
    )(proj, proj, proj, ln_g, ln_b, w_s, b_t, after)


def _sgu_bwd(proj, dy, ln_g, ln_b, w_s, b_t, l, after):
    S = proj.shape[0]
    da = A_GROUPS * HEAD_DIM
    n_proj = proj.shape[1]

    def body(u_ref, v_ref, z_ref, dy_ref, lg_ref, lb_ref, w_ref, bt_ref, after_ref,
             dp_ref, dw_ref, db_ref, dlg_ref, dlb_ref, dvn_ref):
        @pl.when(pl.program_id(0) == 0)
        def _():
            dw_ref[...] = jnp.zeros_like(dw_ref)
            db_ref[...] = jnp.zeros_like(db_ref)
            dlg_ref[...] = jnp.zeros_like(dlg_ref)
            dlb_ref[...] = jnp.zeros_like(dlb_ref)

        up, vp, z, dy = u_ref[...], v_ref[...], z_ref[...], dy_ref[...]
        u, gu = _gelu_and_grad(up)
        v, gv = _gelu_and_grad(vp)
        s, gs = _silu_and_grad(z)
        mu = jnp.mean(v, axis=-1, keepdims=True)
        xc = v - mu
        rs = lax.rsqrt(jnp.mean(xc * xc, axis=-1, keepdims=True) + EPS)
        vhat = xc * rs
        lg = lg_ref[...]
        vn = (vhat * lg + lb_ref[...]).astype(bf16)
        tri = lax.broadcasted_iota(jnp.int32, (CHUNK, CHUNK), 0) >= lax.broadcasted_iota(jnp.int32, (CHUNK, CHUNK), 1)
        lane = lax.broadcasted_iota(jnp.int32, (CHUNK, HEAD_DIM), 1)
        dys = dy * s
        db = jnp.zeros((CHUNK, HEAD_DIM), f32)
        for g in range(A_GROUPS):
            sl = slice(g * HEAD_DIM, (g + 1) * HEAD_DIM)
            wm = jnp.where(tri, w_ref[g], 0.0).astype(bf16)
            mixed = _dot(wm, vn[:, sl]) + bt_ref[:, g:g + 1]
            dmix = dys[:, sl] * u[:, sl]
            dp_ref[:, sl] = (dys[:, sl] * mixed * gu[:, sl]).astype(bf16)
            dp_ref[:, 2 * da + g * HEAD_DIM:2 * da + (g + 1) * HEAD_DIM] = (
                dy[:, sl] * u[:, sl] * mixed * gs[:, sl]).astype(bf16)
            dmb = dmix.astype(bf16)
            dw_ref[g] += jnp.where(tri, _dot(dmb, vn[:, sl], NT), 0.0)
            dvn_ref[:, sl] = _dot(wm, dmb, TN)
            db = db + jnp.where(lane == g, jnp.sum(dmix, axis=1, keepdims=True), 0.0)
        db_ref[...] += db
        dvn = dvn_ref[...]
        dlg_ref[...] += jnp.sum(dvn * vhat, axis=0, keepdims=True)
        dlb_ref[...] += jnp.sum(dvn, axis=0, keepdims=True)
        dvhat = dvn * lg
        dv = rs * (dvhat - jnp.mean(dvhat, axis=-1, keepdims=True)
                   - vhat * jnp.mean(dvhat * vhat, axis=-1, keepdims=True))
        dp_ref[:, da:2 * da] = (dv * gv).astype(bf16)

    blk = lambda cb: pl.BlockSpec((CHUNK, da), lambda c: (c, cb))
    full = lambda shp: pl.BlockSpec(shp, lambda c: (0,) * len(shp))
    return pl.pallas_call(
        body, name="sgu_bwd", grid=(S // CHUNK,),
        in_specs=[blk(0), blk(1), blk(2), blk(0), _layer_spec(ln_g, l), _layer_spec(ln_b, l), _layer_spec(w_s, l),
                  _layer_spec(b_t, l), pl.BlockSpec(memory_space=pl.ANY)],
        out_specs=[pl.BlockSpec((CHUNK, 3 * da), lambda c: (c, 0)), full((A_GROUPS, CHUNK, CHUNK)),
                   full((CHUNK, HEAD_DIM)), full((1, da)), full((1, da))],
        out_shape=[SDS((S, n_proj), bf16), SDS((A_GROUPS, CHUNK, CHUNK), f32), SDS((CHUNK, HEAD_DIM), f32),
                   SDS((1, da), f32), SDS((1, da), f32)],
        scratch_shapes=[pltpu.VMEM((CHUNK, da), f32)], compiler_params=_params(),
    )(proj, proj, proj, dy, ln_g, ln_b, w_s, b_t, after)


def _sb_scores(q, kblk, kb, rows, cols, masked):
    z = _dot(q, kblk, NT) * (1.0 / math.sqrt(HEAD_DIM))
    t = jnp.log(1.0 + jnp.exp(-jnp.abs(z)))
    log_1mb = -(jnp.maximum(z, 0.0) + t)
    log_beta = jnp.minimum(z, 0.0) - t
    if not masked:
        return None, log_beta, log_1mb
    causal = (cols + kb * TK) < rows
    return causal, log_beta, jnp.where(causal, log_1mb, 0.0)


def _sb_tiles(i):
    rows = i * TQ + lax.broadcasted_iota(jnp.int32, (TQ, TK), 0)
    cols = lax.broadcasted_iota(jnp.int32, (TQ, TK), 1)
    r_i = lax.broadcasted_iota(jnp.int32, (TK, TK), 0)
    c_i = lax.broadcasted_iota(jnp.int32, (TK, TK), 1)
    upper, lower = (r_i > c_i).astype(bf16), (r_i < c_i).astype(bf16)
    slot = lax.broadcasted_iota(jnp.int32, (TQ, CARRY_LANES), 1)
    return rows, cols, slot, jnp.concatenate([upper, upper], axis=0), jnp.concatenate([lower, lower], axis=0)


def _suffix_sum(t, tri):
    hi = lax.bitcast_convert_type(lax.bitcast_convert_type(t, jnp.uint32) & jnp.uint32(0xFFFF0000), f32)
    both = jnp.concatenate([hi.astype(bf16), (t - hi).astype(bf16)], axis=1)
    return _dot(both, tri)


def _sb_fwd(proj, y_prev, col0, after):
    S = proj.shape[0]
    D = y_prev.shape[1]
    dh = N_HEADS * HEAD_DIM
    n_diag = TQ // TK

    def body(q_ref, k_ref, v_ref, z_ref, yp_ref, after_ref, y_ref, o_ref, car_ref, qb, kb_s, vb_s, c_ref):
        i = pl.program_id(0)

        @pl.when(i == 0)
        def _():
            kb_s[...] = k_ref[...].astype(bf16)
            vb_s[...] = v_ref[...].astype(bf16)

        qb[...] = q_ref[...].astype(bf16)
        o_ref[...] = jnp.zeros_like(o_ref)
        c_ref[...] = jnp.zeros_like(c_ref)
        car_ref[...] = jnp.zeros_like(car_ref)
        nkb = (i + 1) * n_diag
        rows, cols, slot, upper, _ = _sb_tiles(i)

        def make_step(masked):
            def step(jj, carry):
                kb = nkb - 1 - jj
                off = pl.multiple_of(kb * TK, TK)
                hs = range(N_HEADS)
                sls = [slice(h * HEAD_DIM, (h + 1) * HEAD_DIM) for h in hs]
                sc = [_sb_scores(qb[:, sls[h]], kb_s[pl.ds(off, TK), sls[h]], kb, rows, cols, masked) for h in hs]
                suf = [_suffix_sum(sc[h][2], upper) for h in hs]
                cs = [c_ref[h] for h in hs]
                es = [jnp.exp(sc[h][1] + suf[h] + cs[h][:, :1]) for h in hs]
                if masked:
                    es = [jnp.where(sc[h][0], es[h], 0.0) for h in hs]
                pv = [_dot(es[h].astype(bf16), vb_s[pl.ds(off, TK), sls[h]]) for h in hs]
                for h in hs:
                    o_ref[:, sls[h]] += pv[h]
                    car_ref[h] = jnp.where(slot == kb, cs[h], car_ref[h])
                    c_ref[h] = cs[h] + jnp.sum(sc[h][2], axis=1, keepdims=True)
                return carry
            return step

        lax.fori_loop(0, n_diag, make_step(True), 0)
        lax.fori_loop(n_diag, nkb, make_step(False), 0)
        z = z_ref[...]
        y_ref[...] = (o_ref[...] * (z * jax.nn.sigmoid(z))).astype(bf16)

    cb = col0 * HEAD_DIM // dh
    qspec = lambda k: pl.BlockSpec((TQ, dh), lambda i: (i, cb + k))
    kspec = lambda k: pl.BlockSpec((S, dh), lambda i: (0, cb + k))
    return pl.pallas_call(
        body, name="sb_fwd", grid=(S // TQ,),
        in_specs=[qspec(0), kspec(1), kspec(2), qspec(3), pl.BlockSpec(memory_space=pl.ANY),
                  pl.BlockSpec(memory_space=pl.ANY)],
        out_specs=[pl.BlockSpec((TQ, dh), lambda i: (i, A_GROUPS * HEAD_DIM // dh)),
                   pl.BlockSpec((TQ, dh), lambda i: (i, 0)),
                   pl.BlockSpec((N_HEADS, TQ, CARRY_LANES), lambda i: (0, i, 0))],
        out_shape=[SDS((S, D), bf16), SDS((S, dh), f32), SDS((N_HEADS, S, CARRY_LANES), f32)],
        input_output_aliases={4: 0},
        scratch_shapes=[pltpu.VMEM((TQ, dh), bf16), pltpu.VMEM((S, dh), bf16), pltpu.VMEM((S, dh), bf16),
                        pltpu.VMEM((N_HEADS, TQ, CARRY_LANES), f32)],
        compiler_params=_params(),
    )(proj, proj, proj, proj, y_prev, after)


def _sb_bwd(proj, o, car, dy, dproj_prev, col0, after):
    S = proj.shape[0]
    n_i = S // TQ
    dh = N_HEADS * HEAD_DIM
    n_diag = TQ // TK
    cb = col0 * HEAD_DIM // dh
    scale = 1.0 / math.sqrt(HEAD_DIM)

    def body(q_ref, k_ref, v_ref, z_ref, o_ref, car_ref, dy_ref, dpp_ref, after_ref,
             dp_ref, qb, kb_s, vb_s, dob, p_ref, dq_acc, dk_acc, dv_acc, st_a, st_b, st_k, st_v):
        i = pl.program_id(0)

        def put(stage_ref, row0, nrows, k):
            pltpu.sync_copy(stage_ref, dp_ref.at[pl.ds(row0, nrows), pl.ds((cb + k) * dh, dh)])

        @pl.when(i == 0)
        def _():
            kb_s[...] = k_ref[...].astype(bf16)
            vb_s[...] = v_ref[...].astype(bf16)
            dk_acc[...] = jnp.zeros_like(dk_acc)
            dv_acc[...] = jnp.zeros_like(dv_acc)

        s, gs = _silu_and_grad(z_ref[...])
        dy = dy_ref[...]
        st_b[...] = (dy * o_ref[...] * gs).astype(bf16)
        dob[...] = (dy * s).astype(bf16)
        qb[...] = q_ref[...].astype(bf16)
        p_ref[...] = jnp.zeros_like(p_ref)
        dq_acc[...] = jnp.zeros_like(dq_acc)
        nkb = (i + 1) * n_diag
        rows, cols, slot, upper, lower = _sb_tiles(i)

        def make_step(masked):
            def step(kb, carry):
                off = pl.multiple_of(kb * TK, TK)
                hs = range(N_HEADS)
                sls = [slice(h * HEAD_DIM, (h + 1) * HEAD_DIM) for h in hs]
                qs = [qb[:, sls[h]] for h in hs]
                ks = [kb_s[pl.ds(off, TK), sls[h]] for h in hs]
                dos = [dob[:, sls[h]] for h in hs]
                sc = [_sb_scores(qs[h], ks[h], kb, rows, cols, masked) for h in hs]
                da = [_dot(dos[h], vb_s[pl.ds(off, TK), sls[h]], NT) for h in hs]
                suf = [_suffix_sum(sc[h][2], upper) for h in hs]
                onehot = slot == kb
                cs = [jnp.sum(jnp.where(onehot, car_ref[h], 0.0), axis=1, keepdims=True) for h in hs]
                es = [jnp.exp(sc[h][1] + suf[h] + cs[h]) for h in hs]
                if masked:
                    es = [jnp.where(sc[h][0], es[h], 0.0) for h in hs]
                gs_ = [da[h] * es[h] for h in hs]
                ps = [p_ref[h] for h in hs]
                pre = [_suffix_sum(gs_[h], lower) + ps[h][:, :1] for h in hs]
                dzs = []
                for h in hs:
                    beta = jnp.exp(sc[h][1])
                    dzz = gs_[h] * (1.0 - beta) - beta * pre[h]
                    if masked:
                        dzz = jnp.where(sc[h][0], dzz, 0.0)
                    dzs.append((dzz * scale).astype(bf16))
                dqs = [_dot(dzs[h], ks[h]) for h in hs]
                dks = [_dot(dzs[h], qs[h], TN) for h in hs]
                dvs = [_dot(es[h].astype(bf16), dos[h], TN) for h in hs]
                for h in hs:
                    dq_acc[:, sls[h]] += dqs[h]
                    dk_acc[pl.ds(off, TK), sls[h]] += dks[h]
                    dv_acc[pl.ds(off, TK), sls[h]] += dvs[h]
                    p_ref[h] = ps[h] + jnp.sum(gs_[h], axis=1, keepdims=True)
                return carry
            return step

        lax.fori_loop(0, nkb - n_diag, make_step(False), 0)
        lax.fori_loop(nkb - n_diag, nkb, make_step(True), 0)
        st_a[...] = dq_acc[...].astype(bf16)
        row0 = pl.multiple_of(i * TQ, TQ)
        put(st_a, row0, TQ, 0)
        put(st_b, row0, TQ, 3)

        @pl.when(i == n_i - 1)
        def _():
            st_k[...] = dk_acc[...].astype(bf16)
            st_v[...] = dv_acc[...].astype(bf16)
            put(st_k, 0, S, 1)
            put(st_v, 0, S, 2)

    qspec = lambda k: pl.BlockSpec((TQ, dh), lambda i: (i, cb + k))
    kspec = lambda k: pl.BlockSpec((S, dh), lambda i: (0, cb + k))
    return pl.pallas_call(
        body, name="sb_bwd", grid=(n_i,),
        in_specs=[qspec(0), kspec(1), kspec(2), qspec(3),
                  pl.BlockSpec((TQ, dh), lambda i: (i, 0)),
                  pl.BlockSpec((N_HEADS, TQ, CARRY_LANES), lambda i: (0, i, 0)),
                  pl.BlockSpec((TQ, dh), lambda i: (i, A_GROUPS * HEAD_DIM // dh)),
                  pl.BlockSpec(memory_space=pl.ANY), pl.BlockSpec(memory_space=pl.ANY)],
        out_specs=pl.BlockSpec(memory_space=pl.ANY),
        out_shape=SDS(dproj_prev.shape, bf16),
        input_output_aliases={7: 0},
        scratch_shapes=[pltpu.VMEM((TQ, dh), bf16), pltpu.VMEM((S, dh), bf16), pltpu.VMEM((S, dh), bf16),
                        pltpu.VMEM((TQ, dh), bf16), pltpu.VMEM((N_HEADS, TQ, CARRY_LANES), f32), pltpu.VMEM((TQ, dh), f32),
                        pltpu.VMEM((S, dh), f32), pltpu.VMEM((S, dh), f32),
                        pltpu.VMEM((TQ, dh), bf16), pltpu.VMEM((TQ, dh), bf16),
                        pltpu.VMEM((S, dh), bf16), pltpu.VMEM((S, dh), bf16)],
        compiler_params=_params(56),
    )(proj, proj, proj, proj, o, car, dy, dproj_prev, after)


def _mem_kv(mem, mg, l, w_kv_g):
    M, D = mem.shape
    rb, n = w_kv_g.shape[1], w_kv_g.shape[2]

    def body(m_ref, g_ref, w_ref, kv_ref):
        mv = m_ref[...]
        r = lax.rsqrt(jnp.mean(mv * mv, axis=-1, keepdims=True) + EPS)
        mh = (mv * r * g_ref[...]).astype(bf16)
        kv_ref[...] = _dot(mh, w_ref[...].reshape(N_DEV * rb, n))

    return pl.pallas_call(
        body, name="mem_kv", grid=(1,),
        in_specs=[pl.BlockSpec((M, D), lambda i: (0, 0)), _layer_spec(mg, l),
                  pl.BlockSpec((N_DEV, rb, n), lambda i: (0, 0, 0))],
        out_specs=pl.BlockSpec((M, n), lambda i: (0, 0)),
        out_shape=SDS((M, n), f32), compiler_params=_params(),
    )(mem, mg, w_kv_g)


def _xattn_head(q_ref, kv_ref, qg, kg, h):
    dc = N_HEADS * HEAD_DIM
    sl = slice(h * HEAD_DIM, (h + 1) * HEAD_DIM)
    qh = q_ref[:, sl]
    rq = lax.rsqrt(jnp.mean(qh * qh, axis=-1, keepdims=True) + EPS)
    qhat = qh * rq
    qn = (qhat * qg).astype(bf16)
    kh = kv_ref[:, sl]
    rk = lax.rsqrt(jnp.mean(kh * kh, axis=-1, keepdims=True) + EPS)
    kn = (kh * rk * kg).astype(bf16)
    vh = kv_ref[:, dc + h * HEAD_DIM:dc + (h + 1) * HEAD_DIM].astype(bf16)
    s = _dot(qn, kn, NT) * (1.0 / math.sqrt(HEAD_DIM))
    e = jnp.exp(s - jnp.max(s, axis=-1, keepdims=True))
    p = e / jnp.sum(e, axis=-1, keepdims=True)
    o = _dot(p.astype(bf16), vh)
    return sl, rq, qhat, qn, kn, vh, p, o


def _xattn_fwd(proj, kv, qg, kg, l, y_prev, col0, tq=512):
    S = proj.shape[0]
    D = y_prev.shape[1]
    dc = N_HEADS * HEAD_DIM
    M = kv.shape[0]

    def body(q_ref, z_ref, kv_ref, qg_ref, kg_ref, yp_ref, y_ref):
        for h in range(N_HEADS):
            sl, _, _, _, _, _, _, o = _xattn_head(q_ref, kv_ref, qg_ref[...], kg_ref[...], h)
            z = z_ref[:, sl]
            y_ref[:, sl] = (o * (z * jax.nn.sigmoid(z))).astype(bf16)

    full = lambda shp: pl.BlockSpec(shp, lambda i: (0,) * len(shp))
    return pl.pallas_call(
        body, name="xattn_fwd", grid=(S // tq,),
        in_specs=[pl.BlockSpec((tq, dc), lambda i: (i, col0)), pl.BlockSpec((tq, dc), lambda i: (i, col0 + 1)),
                  full((M, 2 * dc)), _layer_spec(qg, l), _layer_spec(kg, l), pl.BlockSpec(memory_space=pl.ANY)],
        out_specs=pl.BlockSpec((tq, dc), lambda i: (i, D // dc - 1)),
        out_shape=SDS((S, D), bf16), input_output_aliases={5: 0}, compiler_params=_params(),
    )(proj, proj, kv, qg, kg, y_prev)


def _xattn_bwd(proj, kv, qg, kg, l, dy, dproj_prev, col0, tq=512):
    S = proj.shape[0]
    D = dy.shape[1]
    dc = N_HEADS * HEAD_DIM
    M = kv.shape[0]

    def body(q_ref, z_ref, kv_ref, qg_ref, kg_ref, dy_ref, dpp_ref, dp_ref, dkn_ref, dv_ref, dqg_ref):
        @pl.when(pl.program_id(0) == 0)
        def _():
            dkn_ref[...] = jnp.zeros_like(dkn_ref)
            dv_ref[...] = jnp.zeros_like(dv_ref)
            dqg_ref[...] = jnp.zeros_like(dqg_ref)

        qg = qg_ref[...]
        for h in range(N_HEADS):
            sl, rq, qhat, qn, kn, vh, p, o = _xattn_head(q_ref, kv_ref, qg, kg_ref[...], h)
            s, gs = _silu_and_grad(z_ref[:, sl])
            dyh = dy_ref[:, sl]
            dp_ref[:, dc + h * HEAD_DIM:dc + (h + 1) * HEAD_DIM] = (dyh * o * gs).astype(bf16)
            dob = (dyh * s).astype(bf16)
            dpr = _dot(dob, vh, NT)
            dv_ref[:, sl] += _dot(p.astype(bf16), dob, TN)
            ds = (p * (dpr - jnp.sum(p * dpr, axis=-1, keepdims=True)) * (1.0 / math.sqrt(HEAD_DIM))).astype(bf16)
            dqn = _dot(ds, kn)
            dkn_ref[:, sl] += _dot(ds, qn, TN)
            dqg_ref[...] += jnp.sum(dqn * qhat, axis=0, keepdims=True)
            dqhat = dqn * qg
            dp_ref[:, sl] = (rq * (dqhat - qhat * jnp.mean(dqhat * qhat, axis=-1, keepdims=True))).astype(bf16)

    full = lambda shp: pl.BlockSpec(shp, lambda i: (0,) * len(shp))
    return pl.pallas_call(
        body, name="xattn_bwd", grid=(S // tq,),
        in_specs=[pl.BlockSpec((tq, dc), lambda i: (i, col0)), pl.BlockSpec((tq, dc), lambda i: (i, col0 + 1)),
                  full((M, 2 * dc)), _layer_spec(qg, l), _layer_spec(kg, l),
                  pl.BlockSpec((tq, dc), lambda i: (i, D // dc - 1)), pl.BlockSpec(memory_space=pl.ANY)],
        out_specs=[pl.BlockSpec((tq, 2 * dc), lambda i: (i, col0 // 2)), full((M, dc)), full((M, dc)),
                   full((1, HEAD_DIM))],
        out_shape=[SDS(dproj_prev.shape, bf16), SDS((M, dc), f32), SDS((M, dc), f32), SDS((1, HEAD_DIM), f32)],
        input_output_aliases={6: 0}, compiler_params=_params(),
    )(proj, proj, kv, qg, kg, dy, dproj_prev)


def _mem_bwd(mem, mg, kg, l, kv, dkn, dv, w_kv_g):
    M, D = mem.shape
    rb, n = w_kv_g.shape[1], w_kv_g.shape[2]
    dc = n // 2

    def body(m_ref, g_ref, kv_ref, dkn_ref, dv_ref, kg_ref, w_ref, dw_ref, dmg_ref, dkg_ref, dkv_ref):
        mv = m_ref[...]
        r = lax.rsqrt(jnp.mean(mv * mv, axis=-1, keepdims=True) + EPS)
        mhat = mv * r
        mh = (mhat * g_ref[...]).astype(bf16)
        kg = kg_ref[...]
        dkg = jnp.zeros((1, HEAD_DIM), f32)
        for h in range(N_HEADS):
            sl = slice(h * HEAD_DIM, (h + 1) * HEAD_DIM)
            kh = kv_ref[:, sl]
            rk = lax.rsqrt(jnp.mean(kh * kh, axis=-1, keepdims=True) + EPS)
            khat = kh * rk
            dkn_h = dkn_ref[:, sl]
            dkg = dkg + jnp.sum(dkn_h * khat, axis=0, keepdims=True)
            dkhat = dkn_h * kg
            dkv_ref[:, sl] = (rk * (dkhat - khat * jnp.mean(dkhat * khat, axis=-1, keepdims=True))).astype(bf16)
        dkv_ref[:, dc:] = dv_ref[...].astype(bf16)
        dkg_ref[...] = dkg
        dkv = dkv_ref[...]
        dw_ref[...] = _dot(mh, dkv, TN).astype(bf16).reshape(N_DEV, rb, n)
        dmh = _dot(dkv, w_ref[...].reshape(N_DEV * rb, n), NT)
        dmg_ref[...] = jnp.sum(dmh * mhat, axis=0, keepdims=True)

    full = lambda shp: pl.BlockSpec(shp, lambda i: (0,) * len(shp))
    wspec = full((N_DEV, rb, n))
    return pl.pallas_call(
        body, name="mem_bwd", grid=(1,),
        in_specs=[full((M, D)), _layer_spec(mg, l), full((M, n)), full((M, dc)), full((M, dc)), _layer_spec(kg, l), wspec],
        out_specs=[wspec, full((1, D)), full((1, HEAD_DIM))],
        out_shape=[SDS((N_DEV, rb, n), bf16), SDS((1, D), f32), SDS((1, HEAD_DIM), f32)],
        scratch_shapes=[pltpu.VMEM((M, n), bf16)], compiler_params=_params(),
    )(mem, mg, kv, dkn, dv, kg, w_kv_g)


SMALL = ("norm_g", "sgu_ln_g", "sgu_ln_b", "sgu_w", "sgu_b", "mem_norm_g", "q_norm_g", "k_norm_g")


def _small_rows(like):
    rows = [math.prod(like[n].shape) // 128 for n in SMALL]
    offs = [0]
    for r in rows:
        offs.append(offs[-1] + -(-r // 8) * 8)
    return rows, offs


def _pack_small(parts, offs):
    pieces = []
    for k, n in enumerate(SMALL):
        a = parts[n].reshape(-1, 128)
        pieces.append(jnp.pad(a, ((0, offs[k + 1] - offs[k] - a.shape[0]), (0, 0))))
    return jnp.concatenate(pieces)


def kernel(x, mem, norm_g, w_in, sgu_ln_g, sgu_ln_b, sgu_w, sgu_b, mem_norm_g, w_mem_kv, q_norm_g, k_norm_g, w_out, loss_target, m_norm_g, m_w_in, m_sgu_ln_g, m_sgu_ln_b, m_sgu_w, m_sgu_b, m_mem_norm_g, m_w_mem_kv, m_q_norm_g, m_k_norm_g, m_w_out, v_norm_g, v_w_in, v_sgu_ln_g, v_sgu_ln_b, v_sgu_w, v_sgu_b, v_mem_norm_g, v_w_mem_kv, v_q_norm_g, v_k_norm_g, v_w_out):
    L, D, wc = w_in.shape
    S = x.shape[1]
    da = D // 2
    xs = x.reshape(S, D)
    mems = mem.reshape(mem.shape[1], D)
    tgt = loss_target.reshape(S, D)
    stacked = lambda a: a.reshape(a.shape[0], 1, -1)
    ng, lng, lnb, mg, qg, kg = map(stacked, (norm_g, sgu_ln_g, sgu_ln_b, mem_norm_g, q_norm_g, k_norm_g))
    b_t = jnp.swapaxes(sgu_b, 1, 2)
    sb_col, xa_col = 3 * da // HEAD_DIM, (3 * da + D) // (D // 4)

    ax, ay, ac = lax.axis_index("x"), lax.axis_index("y"), lax.axis_index("c")
    ids = jnp.stack([4 * ax + 2 * ay + ac, 2 * ax + ay, ac]).astype(jnp.int32)
    w_in0_b = _cast_into_slot("cast_w_in", w_in, 0, 512, ids, ids)
    first = _gather3_start("gather_w_in0", [w_in0_b], ids)
    late = first["token"]
    w_b = [(w_in0_b if l == 0 else _cast_into_slot("cast_w_in", w_in, l, 512, ids, late),
            _cast_into_slot("cast_w_kv", w_mem_kv, l, 256, ids, late),
            _cast_into_slot("cast_w_out", w_out, l, 256, ids, late)) for l in range(L)]
    relay = _gather3_relay("gather_w_in0_relay",
                           _split_wait(first, *[a for wl in w_b for a in wl if a is not w_in0_b]), ids)
    in_fwd = _gather2_forward("gather_w_in0_forward", _split_wait(relay, relay["token"]), ids)

    acts = []
    xl = xs
    for l in range(L):
        (w_in_g,) = _split_wait(in_fwd, xl if l else in_fwd["token"])
        rest = _gather3_start(f"gather_w_rest{l}", [w_b[l][1], w_b[l][2]], w_in_g)
        order = rest["token"]
        if l + 1 < L:
            nxt = _gather3_start(f"gather_w_in{l + 1}", [w_b[l + 1][0]], order)
            order = nxt["token"]
        proj, h = _rms_proj(xl, ng, l, w_in_g, order)
        rest_relay = _gather3_relay(f"gather_w_rest{l}_relay", _split_wait(rest, proj), proj)
        y = _sgu_fwd(proj, lng, lnb, sgu_w, b_t, l, rest_relay["token"])
        rest_fwd = _gather2_forward(f"gather_w_rest{l}_forward", _split_wait(rest_relay, y), y)
        order = rest_fwd["token"]
        if l + 1 < L:
            nxt_relay = _gather3_relay(f"gather_w_in{l + 1}_relay", _split_wait(nxt, order), order)
            order = nxt_relay["token"]
        y, o_b, car = _sb_fwd(proj, y, sb_col, order)
        w_kv_g, w_out_g = _split_wait(rest_fwd, o_b)
        order = o_b
        if l + 1 < L:
            in_fwd = _gather2_forward(f"gather_w_in{l + 1}_forward", _split_wait(nxt_relay, o_b), o_b)
            order = in_fwd["token"]
        kv = _mem_kv(mems, mg, l, w_kv_g)
        y = _xattn_fwd(proj, kv, qg, kg, l, y, xa_col)
        x_next = _out_proj(xl, y, w_out_g, order)
        acts.append((xl, proj, h, y, o_b, car, kv, w_in_g, w_kv_g, w_out_g))
        xl = x_next

    dx, dxb, loss_part = _loss_and_grad(xl, tgt, 512)
    loss = lax.psum(loss_part[0, 0], ("x", "y", "c"))

    weights = dict(norm_g=norm_g, sgu_ln_g=sgu_ln_g, sgu_ln_b=sgu_ln_b, sgu_w=sgu_w, sgu_b=sgu_b,
                   mem_norm_g=mem_norm_g, q_norm_g=q_norm_g, k_norm_g=k_norm_g)
    moms_m = dict(norm_g=m_norm_g, sgu_ln_g=m_sgu_ln_g, sgu_ln_b=m_sgu_ln_b, sgu_w=m_sgu_w, sgu_b=m_sgu_b,
                  mem_norm_g=m_mem_norm_g, q_norm_g=m_q_norm_g, k_norm_g=m_k_norm_g)
    moms_v = dict(norm_g=v_norm_g, sgu_ln_g=v_sgu_ln_g, sgu_ln_b=v_sgu_ln_b, sgu_w=v_sgu_w, sgu_b=v_sgu_b,
                  mem_norm_g=v_mem_norm_g, q_norm_g=v_q_norm_g, k_norm_g=v_k_norm_g)
    small_rows, small_offs = _small_rows(weights)
    head_rows = D // 128
    assert SMALL[0] == "norm_g" and head_rows % 8 == 0

    seconds = {}
    pending = None
    adam = {"w_out": None, "w_mem_kv": None, "w_in": None}

    late_in = {}

    def update(lu, order):
        (r_out,) = _scatter_finish(seconds[f"g_out{lu}"], order)
        adam["w_out"] = _sum_adam("adam_w_out", r_out, w_out, m_w_out, v_w_out, lu, adam["w_out"], 128, ids, ids)
        r_kv, late_in[lu] = _scatter_finish(seconds[f"g_rest{lu}"], adam["w_out"][0])
        adam["w_mem_kv"] = _sum_adam("adam_w_kv", r_kv, w_mem_kv, m_w_mem_kv, v_w_mem_kv, lu, adam["w_mem_kv"], 256, ids, ids)
        return adam["w_mem_kv"][0]

    def update_in(lu, order):
        adam["w_in"] = _sum_adam("adam_w_in", late_in[lu], w_in, m_w_in, v_w_in, lu, adam["w_in"], 256, ids, order)
        return adam["w_in"][0]
    small = {n: [None] * L for n in SMALL}
    for l in reversed(range(L)):
        xl, proj, h, y, o_b, car, kv, w_in_g, w_kv_g, w_out_g = acts[l]
        dy = _out_bwd_dy(dxb, w_out_g)
        order = dy
        if pending is not None:
            seconds[pending[0]] = _scatter2_second_level(pending[0], pending[1], dy, ids)
            order = seconds[pending[0]]["token"]
        g_out = _tn_grad("out_bwd_dw", y, dxb, 512, 512, True, order)
        seconds[f"g_out{l}"] = _scatter1_start(f"scatter_g_out{l}", [g_out], ids)
        dproj, d_sw, d_sb, d_lg, d_lb = _sgu_bwd(proj, dy, lng, lnb, sgu_w, b_t, l, seconds[f"g_out{l}"]["token"])
        dproj = _sb_bwd(proj, o_b, car, dy, dproj, sb_col, d_lb)
        dproj, dkn, dv, d_qg = _xattn_bwd(proj, kv, qg, kg, l, dy, dproj, xa_col)
        g_kv, d_mg, d_kg = _mem_bwd(mems, mg, kg, l, kv, dkn, dv, w_kv_g)
        for n, val in (("sgu_ln_g", d_lg), ("sgu_ln_b", d_lb), ("sgu_w", d_sw), ("sgu_b", d_sb[:, :A_GROUPS].T),
                       ("mem_norm_g", d_mg), ("q_norm_g", d_qg), ("k_norm_g", d_kg)):
            small[n][l] = val.reshape(-1)
        order = d_kg
        if l == 0:
            small["norm_g"][0] = jnp.zeros_like(small["norm_g"][1])
            part = _pack_small({n: jnp.stack(small[n]) for n in SMALL}, small_offs)
            tail = _gather1_start("gather_small_tail", [_into_slot("small_tail_slot", part[head_rows:], ids)], ids)
            order = tail["token"]
        g_in_l = _tn_grad("in_bwd_dw", h, dproj, 1024, wc, False, order)
        first = _scatter2_pair_start(f"scatter_g_rest{l}_pair", [g_kv, g_in_l], ids)
        order = first["token"]
        pending = (f"g_rest{l}", first)
        if l == 0:
            for lu in reversed(range(1, L)):
                order = update(lu, order)
            seconds[pending[0]] = _scatter2_second_level(pending[0], pending[1], order, ids)
            order = seconds[pending[0]]["token"]
            for lu in reversed(range(1, L)):
                order = update_in(lu, order)
        dh = _in_bwd_dh(dproj, w_in_g, order)
        dx, dxb, d_ng = _rms_bwd(dh, xl, ng, l, dx, order)
        small["norm_g"][l] = d_ng.reshape(-1)
    head = _gather1_start("gather_small_head",
                          [_into_slot("small_head_slot", small["norm_g"][0].reshape(head_rows, 128), ids)], ids)

    (r_out,) = _scatter_finish(seconds["g_out0"], head["token"])
    adam["w_out"] = _sum_adam("adam_w_out", r_out, w_out, m_w_out, v_w_out, 0, adam["w_out"], 128, ids, ids)
    r_kv, r_in = _scatter_finish(seconds["g_rest0"], adam["w_out"][0])
    adam["w_mem_kv"] = _sum_adam("adam_w_kv", r_kv, w_mem_kv, m_w_mem_kv, v_w_mem_kv, 0, adam["w_mem_kv"], 256, ids, ids)
    adam["w_in"] = _sum_adam("adam_w_in", r_in, w_in, m_w_in, v_w_in, 0, adam["w_in"], 256, ids, ids)
    (r_tail,) = _split_wait(tail, adam["w_in"][0])
    (r_head,) = _split_wait(head, r_tail)
    as128 = lambda d: [d[n].reshape(-1, 128) for n in SMALL]
    sm = _small_sum_adam(r_head, r_tail, as128(weights), as128(moms_m), as128(moms_v), small_offs)
    res = dict(adam)
    for p, n in enumerate(SMALL):
        res[n] = [sm[k * len(SMALL) + p].reshape(weights[n].shape) for k in range(4)]

    order = ("norm_g", "w_in", "sgu_ln_g", "sgu_ln_b", "sgu_w", "sgu_b", "mem_norm_g", "w_mem_kv", "q_norm_g",
             "k_norm_g", "w_out")
    outs = [loss, dx.reshape(x.shape)]
    for k in range(4):
        outs += [res[n][k] for n in order]
    return tuple(outs)
```

```python
import functools
import math

import jax
import jax.numpy as jnp
from jax import lax
from jax.experimental import pallas as pl
from jax.experimental.pallas import tpu as pltpu

f32 = jnp.float32
bf16 = jnp.bfloat16
SDS = jax.ShapeDtypeStruct

N_DEV = 8
EPS = 1e-6
CHUNK = 128
A_GROUPS = 8
HEAD_DIM = 128
N_HEADS = 4
TQ = 256
TK = 256
CARRY_LANES = 128
ADAM_LR, ADAM_B1, ADAM_B2, ADAM_EPS, ADAM_WD, ADAM_STEP = 0.001, 0.9, 0.999, 1e-08, 0.01, 10
MIB = 1024 * 1024

NT = (((1,), (1,)), ((), ()))
TN = (((0,), (0,)), ((), ()))


def _params(vmem_mib=48):
    return pltpu.CompilerParams(vmem_limit_bytes=vmem_mib * MIB)


def _gelu_and_grad(x):
    e = lax.erf(x * (1.0 / math.sqrt(2.0)))
    cdf = 0.5 * (1.0 + e)
    pdf = jnp.exp(-0.5 * x * x) * (1.0 / math.sqrt(2.0 * math.pi))
    return x * cdf, cdf + x * pdf


def _gelu(x):
    return 0.5 * x * (1.0 + lax.erf(x * (1.0 / math.sqrt(2.0))))


def _silu_and_grad(z):
    sg = jax.nn.sigmoid(z)
    return z * sg, sg * (1.0 + z * (1.0 - sg))


def _layer_spec(stacked, l):
    rest = stacked.shape[1:]
    return pl.BlockSpec((None,) + rest, lambda *idx: (l,) + (0,) * len(rest))


def _dot(a, b, dims=None):
    if dims is None:
        return jnp.dot(a, b, preferred_element_type=f32)
    return lax.dot_general(a, b, dims, preferred_element_type=f32)


_HBM = pl.BlockSpec(memory_space=pltpu.HBM)
_SEM = pl.BlockSpec(memory_space=pltpu.SEMAPHORE)
_EFFECT = pltpu.SideEffectType.DATAFLOW_SIDE_EFFECTING


def _split_start(name, bufs, n_remote, n_local, build, after):
    nb = len(bufs)

    def body(*refs):
        token = refs[-1]
        locals_, remotes = build(refs[:nb], *refs[nb + 1:nb + 4])
        for cp in locals_ + remotes:
            cp.start()
        token[...] = jnp.zeros_like(token)

    hbm = lambda a: pltpu.with_memory_space_constraint(a, pltpu.HBM)
    outs = pl.pallas_call(
        body, name=name,
        out_shape=(pltpu.SemaphoreType.DMA((n_remote,)), pltpu.SemaphoreType.DMA((n_remote,)),
                   pltpu.SemaphoreType.DMA((max(n_local, 1),)),
                   *[pltpu.HBM(b.shape, b.dtype) for b in bufs], SDS((8, 128), f32)),
        in_specs=[_HBM] * nb + [pl.BlockSpec(memory_space=pl.ANY)],
        out_specs=(_SEM, _SEM, _SEM, *[_HBM] * nb, pl.BlockSpec(memory_space=pltpu.VMEM)),
        input_output_aliases={k: 3 + k for k in range(nb)},
        compiler_params=pltpu.CompilerParams(has_side_effects=_EFFECT),
    )(*[hbm(b) for b in bufs], after)
    return dict(name=name, build=build, sems=outs[:3], bufs=outs[3:3 + nb], token=outs[-1])


def _split_wait(handle, *after):
    build, bufs = handle["build"], handle["bufs"]
    nb = len(bufs)

    def body(*refs):
        locals_, remotes = build(refs[:nb], *refs[nb:nb + 3])
        for cp in remotes:
            cp.wait_recv()
        for cp in remotes:
            cp.wait_send()
        for cp in locals_:
            cp.wait()

    outs = pl.pallas_call(
        body, name=handle["name"] + "_wait",
        out_shape=tuple(pltpu.HBM(b.shape, b.dtype) for b in bufs),
        in_specs=[_HBM] * nb + [_SEM] * 3 + [pl.BlockSpec(memory_space=pl.ANY)] * len(after),
        out_specs=tuple([_HBM] * nb),
        input_output_aliases={k: k for k in range(nb)},
        compiler_params=pltpu.CompilerParams(has_side_effects=_EFFECT),
    )(*bufs, *handle["sems"], *after)
    return list(outs)


def _remote(src, dst, send_sems, recv_sems, k, to):
    return pltpu.make_async_remote_copy(src_ref=src, dst_ref=dst, send_sem=send_sems.at[k], recv_sem=recv_sems.at[k],
                                        device_id=to, device_id_type=pl.DeviceIdType.MESH)


def _other_chips(x, y):
    return [(1 - x, y), (x, 1 - y), (1 - x, 1 - y)]


def _all_peers(x, y, c):
    return [(1 - x if m & 4 else x, 1 - y if m & 2 else y, 1 - c if m & 1 else c) for m in range(1, N_DEV)]


def _gather1_start(name, lands, after):
    def build(refs, send, recv, loc):
        x, y, c = lax.axis_index("x"), lax.axis_index("y"), lax.axis_index("c")
        me = 4 * x + 2 * y + c
        return [], [_remote(d.at[me], d.at[me], send, recv, 7 * a + k, peer)
                    for a, d in enumerate(refs) for k, peer in enumerate(_all_peers(x, y, c))]

    return _split_start(name, list(lands), 7 * len(lands), 0, build, after)


def _scatter1_start(name, srcs, after):
    n = len(srcs)

    def build(refs, send, recv, loc):
        x, y, c = lax.axis_index("x"), lax.axis_index("y"), lax.axis_index("c")
        me = 4 * x + 2 * y + c
        return [], [_remote(refs[a].at[4 * px + 2 * py + pc], refs[n + a].at[me], send, recv, 7 * a + k, (px, py, pc))
                    for a in range(n) for k, (px, py, pc) in enumerate(_all_peers(x, y, c))]

    return _split_start(name, list(srcs) + [lax.empty(s.shape, s.dtype) for s in srcs], 7 * n, 0, build, after)


def _gather2_start(name, lands, after):
    def build(refs, send, recv, loc):
        x, y, c = lax.axis_index("x"), lax.axis_index("y"), lax.axis_index("c")
        me = 4 * x + 2 * y + c
        remotes = []
        for a, d in enumerate(refs):
            remotes.append(_remote(d.at[me], d.at[me], send, recv, 4 * a, (x, y, 1 - c)))
            remotes += [_remote(d.at[me], d.at[me], send, recv, 4 * a + 1 + k, (px, py, c))
                        for k, (px, py) in enumerate(_other_chips(x, y))]
        return [], remotes

    return _split_start(name, list(lands), 4 * len(lands), 0, build, after)


def _gather3_start(name, lands, after):
    def build(refs, send, recv, loc):
        x, y, c = lax.axis_index("x"), lax.axis_index("y"), lax.axis_index("c")
        me = 4 * x + 2 * y + c
        return [], [_remote(d.at[me], d.at[me], send, recv, 3 * a + k, to)
                    for a, d in enumerate(refs) for k, to in enumerate([(x, y, 1 - c), (1 - x, y, c), (x, 1 - y, c)])]

    return _split_start(name, list(lands), 3 * len(lands), 0, build, after)


def _gather3_relay(name, lands, after):
    def build(refs, send, recv, loc):
        x, y, c = lax.axis_index("x"), lax.axis_index("y"), lax.axis_index("c")
        from_x = c == 0
        slot = 4 * jnp.where(from_x, 1 - x, x) + 2 * jnp.where(from_x, y, 1 - y) + c
        to = (jnp.where(from_x, x, 1 - x), jnp.where(from_x, 1 - y, y), c)
        return [], [_remote(d.at[slot], d.at[slot], send, recv, a, to) for a, d in enumerate(refs)]

    return _split_start(name, list(lands), len(lands), 0, build, after)


def _gather2_forward(name, lands, after):
    n = len(lands)

    def build(refs, send, recv, loc):
        x, y, c = lax.axis_index("x"), lax.axis_index("y"), lax.axis_index("c")
        slots = [4 * px + 2 * py + c for px, py in _other_chips(x, y)]
        return [], [_remote(d.at[sl], d.at[sl], send, recv, 3 * a + k, (x, y, 1 - c))
                    for a, d in enumerate(refs) for k, sl in enumerate(slots)]

    return _split_start(name, list(lands), 3 * n, 0, build, after)


def _scatter2_pair_start(name, srcs, after):
    n = len(srcs)

    def build(refs, send, recv, loc):
        x, y, c = lax.axis_index("x"), lax.axis_index("y"), lax.axis_index("c")
        return [], [_remote(refs[a].at[2 * q + 1 - c], refs[n + a].at[q], send, recv, 4 * a + q, (x, y, 1 - c))
                    for a in range(n) for q in range(4)]

    lands = [lax.empty((4,) + s.shape[1:], s.dtype) for s in srcs]
    return _split_start(name, list(srcs) + lands, 4 * n, 0, build, after)


def _scatter2_chip_start(name, pairs, after):
    n = len(pairs)

    def build(refs, send, recv, loc):
        x, y, c = lax.axis_index("x"), lax.axis_index("y"), lax.axis_index("c")
        return [], [_remote(refs[a].at[2 * px + py], refs[n + a].at[2 * x + y], send, recv, 3 * a + k, (px, py, c))
                    for a in range(n) for k, (px, py) in enumerate(_other_chips(x, y))]

    return _split_start(name, list(pairs) + [lax.empty(p.shape, p.dtype) for p in pairs], 3 * n, 0, build, after)


def _pair_sum(name, src, theirs, ids):
    _, R, C = theirs.shape
    tr = min(R, 1024)

    def body(ids_ref, a_ref, b_ref, o_ref):
        o_ref[...] = (a_ref[...].astype(f32) + b_ref[...].astype(f32)).astype(bf16)

    spec = pl.BlockSpec((None, tr, C), lambda q, i, ids: (q, i, 0))
    return pl.pallas_call(
        body, name=name,
        grid_spec=pltpu.PrefetchScalarGridSpec(
            num_scalar_prefetch=1, grid=(4, R // tr),
            in_specs=[pl.BlockSpec((None, tr, C), lambda q, i, ids: (2 * q + ids[2], i, 0)), spec], out_specs=spec),
        out_shape=SDS(theirs.shape, bf16), compiler_params=_params(),
    )(ids, src, theirs)


def _scatter2_second_level(name, first, after, ids):
    outs = _split_wait(first, after)
    n = len(outs) // 2
    pairs = [_pair_sum(f"pair_sum_{name}{a}", outs[a], outs[n + a], ids) for a in range(n)]
    return _scatter2_chip_start(f"scatter_{name}_chip", pairs, ids)


def _scatter_finish(second, after):
    outs = _split_wait(second, after)
    n = len(outs) // 2
    return [(outs[a], outs[n + a]) for a in range(n)]


def _cast_into_slot(name, w, l, tr, ids, after):
    _, R, C = w.shape

    def body(ids_ref, w_ref, after_ref, o_ref):
        o_ref[...] = w_ref[...].astype(bf16)

    return pl.pallas_call(
        body, name=name,
        grid_spec=pltpu.PrefetchScalarGridSpec(
            num_scalar_prefetch=1, grid=(R // tr,),
            in_specs=[pl.BlockSpec((None, tr, C), lambda i, ids: (l, i, 0)), pl.BlockSpec(memory_space=pl.ANY)],
            out_specs=pl.BlockSpec((None, tr, C), lambda i, ids: (ids[0], i, 0))),
        out_shape=SDS((N_DEV, R, C), bf16), compiler_params=_params(),
    )(ids, w, after)


def _into_slot(name, a, ids):
    R, C = a.shape

    def body(ids_ref, a_ref, o_ref):
        o_ref[...] = a_ref[...]

    return pl.pallas_call(
        body, name=name,
        grid_spec=pltpu.PrefetchScalarGridSpec(
            num_scalar_prefetch=1, grid=(1,),
            in_specs=[pl.BlockSpec((R, C), lambda i, ids: (0, 0))],
            out_specs=pl.BlockSpec((None, R, C), lambda i, ids: (ids[0], 0, 0))),
        out_shape=SDS((N_DEV, R, C), f32), compiler_params=_params(),
    )(ids, a)


def _adam_math(w, g, m, v):
    m2 = ADAM_B1 * m + (1.0 - ADAM_B1) * g
    v2 = ADAM_B2 * v + (1.0 - ADAM_B2) * (g * g)
    m_hat = m2 / (1.0 - ADAM_B1 ** ADAM_STEP)
    v_hat = v2 / (1.0 - ADAM_B2 ** ADAM_STEP)
    delta = -ADAM_LR * (m_hat / (jnp.sqrt(v_hat) + ADAM_EPS) + ADAM_WD * w)
    return delta, m2, v2


def _sum_adam(name, pair_recv, w, m, v, l, prev, tr, ids, after):
    own, recv = pair_recv
    L, R, C = w.shape
    slots = recv.shape[0]
    mine = 0 if slots == N_DEV else 1

    def body(ids_ref, r_ref, own_ref, w_ref, m_ref, v_ref, after_ref, *rest):
        g_ref, d_ref, m2_ref, v2_ref = rest[-4:]
        terms = [jnp.where(ids_ref[mine] == q, own_ref[...], r_ref[q]).astype(f32) for q in range(slots)]
        g = terms[0]
        for t in terms[1:]:
            g = g + t
        d, m2, v2 = _adam_math(w_ref[...], g, m_ref[...], v_ref[...])
        g_ref[...] = g
        d_ref[...] = d
        m2_ref[...] = m2
        v2_ref[...] = v2

    wspec = pl.BlockSpec((None, tr, C), lambda i, ids: (l, i, 0))
    in_specs = [pl.BlockSpec((slots, tr, C), lambda i, ids: (0, i, 0)),
                pl.BlockSpec((None, tr, C), lambda i, ids: (ids[mine], i, 0)), wspec, wspec, wspec,
                pl.BlockSpec(memory_space=pl.ANY)]
    args = [ids, recv, own, w, m, v, after]
    aliases = {}
    if prev is not None:
        in_specs += [pl.BlockSpec(memory_space=pl.ANY)] * 4
        args += list(prev)
        aliases = {7 + k: k for k in range(4)}
    return pl.pallas_call(
        body, name=name,
        grid_spec=pltpu.PrefetchScalarGridSpec(num_scalar_prefetch=1, grid=(R // tr,), in_specs=in_specs,
                                               out_specs=[wspec] * 4),
        out_shape=[SDS((L, R, C), f32)] * 4, input_output_aliases=aliases, compiler_params=_params(),
    )(*args)


def _small_sum_adam(recv_head, recv_tail, ws, ms, vs, offs):
    n = len(ws)
    r0 = recv_head.shape[1]

    def body(*refs):
        rh, rt = refs[0], refs[1]
        w_refs, m_refs, v_refs = refs[2:2 + n], refs[2 + n:2 + 2 * n], refs[2 + 2 * n:2 + 3 * n]
        outs = refs[2 + 3 * n:]
        for p in range(n):
            lo, hi = offs[p], offs[p] + ws[p].shape[0]
            pieces = []
            if lo < r0:
                pieces.append((rh, lo, 0, min(hi, r0) - lo))
            if hi > r0:
                pieces.append((rt, max(lo, r0) - r0, max(lo, r0) - lo, hi - max(lo, r0)))
            for src, a, b, cnt in pieces:
                g = src[0, a:a + cnt, :]
                for s in range(1, N_DEV):
                    g = g + src[s, a:a + cnt, :]
                d, m2, v2 = _adam_math(w_refs[p][b:b + cnt, :], g, m_refs[p][b:b + cnt, :], v_refs[p][b:b + cnt, :])
                for k, val in enumerate((g, d, m2, v2)):
                    outs[k * n + p][b:b + cnt, :] = val

    return pl.pallas_call(
        body, name="small_sum_adam", out_shape=[SDS(w.shape, f32) for w in ws] * 4, compiler_params=_params(),
    )(recv_head, recv_tail, *ws, *ms, *vs)


def _loss_and_grad(xf, tgt, tm):
    S, D = xf.shape

    def body(x_ref, t_ref, dx_ref, dxb_ref, l_ref):
        i = pl.program_id(0)
        d = x_ref[...] - t_ref[...]
        dx = d * (1.0 / D)
        dx_ref[...] = dx
        dxb_ref[...] = dx.astype(bf16)
        e = d * d
        part = e[:, 0:128]
        for k in range(1, D // 128):
            part = part + e[:, k * 128:(k + 1) * 128]
        part = jnp.sum(part.reshape(tm // 8, 8, 128), axis=0)

        @pl.when(i == 0)
        def _():
            l_ref[...] = jnp.zeros_like(l_ref)

        l_ref[...] += part

        @pl.when(i == pl.num_programs(0) - 1)
        def _():
            tot = jnp.sum(l_ref[...], axis=1, keepdims=True)
            tot = jnp.sum(tot, axis=0, keepdims=True)
            l_ref[...] = jnp.broadcast_to(tot * (0.5 / D), l_ref.shape)

    row = pl.BlockSpec((tm, D), lambda i: (i, 0))
    return pl.pallas_call(
        body, name="loss_grad", grid=(S // tm,),
        in_specs=[row, row], out_specs=[row, row, pl.BlockSpec((8, 128), lambda i: (0, 0))],
        out_shape=[SDS((S, D), f32), SDS((S, D), bf16), SDS((8, 128), f32)], compiler_params=_params(),
    )(xf, tgt)


def _rms_proj(x, g, l, w_in_g, after, tm=1024):
    S, D = x.shape
    wc = w_in_g.shape[2]
    n_out = N_DEV * wc

    def body(x_ref, g_ref, w_ref, after_ref, proj_ref, h_ref):
        @pl.when(pl.program_id(1) == 0)
        def _():
            xv = x_ref[...]
            r = lax.rsqrt(jnp.mean(xv * xv, axis=-1, keepdims=True) + EPS)
            h_ref[...] = (xv * r * g_ref[...]).astype(bf16)

        proj_ref[...] = _dot(h_ref[...], w_ref[...])

    return pl.pallas_call(
        body, name="rms_proj", grid=(S // tm, N_DEV),
        in_specs=[pl.BlockSpec((tm, D), lambda i, j: (i, 0)), _layer_spec(g, l),
                  pl.BlockSpec((None, D, wc), lambda i, j: (j, 0, 0)), pl.BlockSpec(memory_space=pl.ANY)],
        out_specs=[pl.BlockSpec((tm, wc), lambda i, j: (i, j)), pl.BlockSpec((tm, D), lambda i, j: (i, 0))],
        out_shape=[SDS((S, n_out), f32), SDS((S, D), bf16)], compiler_params=_params(),
    )(x, g, w_in_g, after)


def _out_proj(x, y, w_out_g, after, tm=512):
    S, D = x.shape
    rb = w_out_g.shape[1]

    def body(x_ref, y_ref, w_ref, after_ref, o_ref):
        w = w_ref[...].reshape(N_DEV * rb, D)
        o_ref[...] = x_ref[...] + _dot(y_ref[...], w)

    row = pl.BlockSpec((tm, D), lambda i: (i, 0))
    return pl.pallas_call(
        body, name="out_proj", grid=(S // tm,),
        in_specs=[row, row, pl.BlockSpec((N_DEV, rb, D), lambda i: (0, 0, 0)), pl.BlockSpec(memory_space=pl.ANY)],
        out_specs=row, out_shape=SDS((S, D), f32), compiler_params=_params(),
    )(x, y, w_out_g, after)


def _out_bwd_dy(dxb, w_out_g, tm=512):
    S, D = dxb.shape
    rb = w_out_g.shape[1]

    nb = 2

    def body(dx_ref, w_ref, o_ref):
        o_ref[...] = _dot(dx_ref[...], w_ref[...].reshape(nb * rb, D), NT)

    return pl.pallas_call(
        body, name="out_bwd_dy", grid=(S // tm, N_DEV // nb),
        in_specs=[pl.BlockSpec((tm, D), lambda i, j: (i, 0)),
                  pl.BlockSpec((nb, rb, D), lambda i, j: (j, 0, 0))],
        out_specs=pl.BlockSpec((tm, nb * rb), lambda i, j: (i, j)),
        out_shape=SDS((S, D), f32), compiler_params=_params(),
    )(dxb, w_out_g)


def _tn_grad(name, a, b, tm, tn, rows_major, after):
    S, M = a.shape
    N = b.shape[1]
    if rows_major:
        rb = M // N_DEV
        nb = tm // rb
        out_shape = SDS((N_DEV, rb, N), bf16)
        out_spec = pl.BlockSpec((nb, rb, tn), lambda i, j: (i, 0, j))
    else:
        out_shape = SDS((N_DEV, M, N // N_DEV), bf16)
        assert tn == N // N_DEV
        out_spec = pl.BlockSpec((None, tm, tn), lambda i, j: (j, i, 0))

    def body(a_ref, b_ref, after_ref, o_ref):
        o_ref[...] = _dot(a_ref[...], b_ref[...], TN).astype(bf16).reshape(o_ref.shape)

    return pl.pallas_call(
        body, name=name, grid=(M // tm, N // tn),
        in_specs=[pl.BlockSpec((S, tm), lambda i, j: (0, i)), pl.BlockSpec((S, tn), lambda i, j: (0, j)),
                  pl.BlockSpec(memory_space=pl.ANY)],
        out_specs=out_spec, out_shape=out_shape, compiler_params=_params(),
    )(a, b, after)


def _in_bwd_dh(dproj, w_in_g, after, tm=1024, tn=256):
    S = dproj.shape[0]
    _, D, wc = w_in_g.shape
    tm = min(tm, S)

    def body(dp_ref, w_ref, after_ref, o_ref):
        acc = _dot(dp_ref[:, 0:wc], w_ref[0], NT)
        for k in range(1, N_DEV):
            acc = acc + _dot(dp_ref[:, k * wc:(k + 1) * wc], w_ref[k], NT)
        o_ref[...] = acc

    return pl.pallas_call(
        body, name="in_bwd_dh", grid=(S // tm, D // tn),
        in_specs=[pl.BlockSpec((tm, N_DEV * wc), lambda i, j: (i, 0)),
                  pl.BlockSpec((N_DEV, tn, wc), lambda i, j: (0, j, 0)), pl.BlockSpec(memory_space=pl.ANY)],
        out_specs=pl.BlockSpec((tm, tn), lambda i, j: (i, j)),
        out_shape=SDS((S, D), f32), compiler_params=_params(),
    )(dproj, w_in_g, after)


def _rms_bwd(dh, x, g, l, dx_next, after, tm=256):
    S, D = x.shape

    def body(dh_ref, x_ref, g_ref, dxn_ref, after_ref, dx_ref, dxb_ref, dg_ref):
        @pl.when(pl.program_id(0) == 0)
        def _():
            dg_ref[...] = jnp.zeros_like(dg_ref)

        dh = dh_ref[...]
        xv = x_ref[...]
        r = lax.rsqrt(jnp.mean(xv * xv, axis=-1, keepdims=True) + EPS)
        xhat = xv * r
        dxhat = dh * g_ref[...]
        dx = r * (dxhat - xhat * jnp.mean(dxhat * xhat, axis=-1, keepdims=True)) + dxn_ref[...]
        dx_ref[...] = dx
        dxb_ref[...] = dx.astype(bf16)
        dg_ref[...] += jnp.sum(dh * xhat, axis=0, keepdims=True)

    row = pl.BlockSpec((tm, D), lambda i: (i, 0))
    vec = pl.BlockSpec((1, D), lambda i: (0, 0))
    return pl.pallas_call(
        body, name="rms_bwd", grid=(S // tm,),
        in_specs=[row, row, _layer_spec(g, l), row, pl.BlockSpec(memory_space=pl.ANY)], out_specs=[row, row, vec],
        out_shape=[SDS((S, D), f32), SDS((S, D), bf16), SDS((1, D), f32)], compiler_params=_params(),
    )(dh, x, g, dx_next, after)


def _sgu_fwd(proj, ln_g, ln_b, w_s, b_t, l, after):
    S = proj.shape[0]
    da = A_GROUPS * HEAD_DIM
    D = 2 * da

    def body(u_ref, v_ref, z_ref, lg_ref, lb_ref, w_ref, bt_ref, after_ref, y_ref):
        u = _gelu(u_ref[...])
        v = _gelu(v_ref[...])
        z = z_ref[...]
        mu = jnp.mean(v, axis=-1, keepdims=True)
        xc = v - mu
        rs = lax.rsqrt(jnp.mean(xc * xc, axis=-1, keepdims=True) + EPS)
        vn = (xc * rs * lg_ref[...] + lb_ref[...]).astype(bf16)
        gate = u * (z * jax.nn.sigmoid(z))
        tri = lax.broadcasted_iota(jnp.int32, (CHUNK, CHUNK), 0) >= lax.broadcasted_iota(jnp.int32, (CHUNK, CHUNK), 1)
        for g in range(A_GROUPS):
            sl = slice(g * HEAD_DIM, (g + 1) * HEAD_DIM)
            wm = jnp.where(tri, w_ref[g], 0.0).astype(bf16)
            mixed = _dot(wm, vn[:, sl]) + bt_ref[:, g:g + 1]
            y_ref[:, sl] = (gate[:, sl] * mixed).astype(bf16)

    blk = lambda cb: pl.BlockSpec((CHUNK, da), lambda c: (c, cb))
    full = lambda shp: pl.BlockSpec(shp, lambda c: (0,) * len(shp))
    return pl.pallas_call(
        body, name="sgu_fwd", grid=(S // CHUNK,),
        in_specs=[blk(0), blk(1), blk(2), _layer_spec(ln_g, l), _layer_spec(ln_b, l), _layer_spec(w_s, l),
                  _layer_spec(b_t, l), pl.BlockSpec(memory_space=pl.ANY)],
        out_specs=blk(0), out_shape=SDS((S, D), bf16), compiler_params=_params(),
    )(proj, proj, proj, ln_g, ln_b, w_s, b_t, after)


def _sgu_bwd(proj, dy, ln_g, ln_b, w_s, b_t, l, after):
    S = proj.shape[0]
    da = A_GROUPS * HEAD_DIM
    n_proj = proj.shape[1]

    def body(u_ref, v_ref, z_ref, dy_ref, lg_ref, lb_ref, w_ref, bt_ref, after_ref,
             dp_ref, dw_ref, db_ref, dlg_ref, dlb_ref, dvn_ref):
        @pl.when(pl.program_id(0) == 0)
        def _():
            dw_ref[...] = jnp.zeros_like(dw_ref)
            db_ref[...] = jnp.zeros_like(db_ref)
            dlg_ref[...] = jnp.zeros_like(dlg_ref)
            dlb_ref[...] = jnp.zeros_like(dlb_ref)

        up, vp, z, dy = u_ref[...], v_ref[...], z_ref[...], dy_ref[...]
        u, gu = _gelu_and_grad(up)
        v, gv = _gelu_and_grad(vp)
        s, gs = _silu_and_grad(z)
        mu = jnp.mean(v, axis=-1, keepdims=True)
        xc = v - mu
        rs = lax.rsqrt(jnp.mean(xc * xc, axis=-1, keepdims=True) + EPS)
        vhat = xc * rs
        lg = lg_ref[...]
        vn = (vhat * lg + lb_ref[...]).astype(bf16)
        tri = lax.broadcasted_iota(jnp.int32, (CHUNK, CHUNK), 0) >= lax.broadcasted_iota(jnp.int32, (CHUNK, CHUNK), 1)
        lane = lax.broadcasted_iota(jnp.int32, (CHUNK, HEAD_DIM), 1)
        dys = dy * s
        db = jnp.zeros((CHUNK, HEAD_DIM), f32)
        for g in range(A_GROUPS):
            sl = slice(g * HEAD_DIM, (g + 1) * HEAD_DIM)
            wm = jnp.where(tri, w_ref[g], 0.0).astype(bf16)
            mixed = _dot(wm, vn[:, sl]) + bt_ref[:, g:g + 1]
            dmix = dys[:, sl] * u[:, sl]
            dp_ref[:, sl] = (dys[:, sl] * mixed * gu[:, sl]).astype(bf16)
            dp_ref[:, 2 * da + g * HEAD_DIM:2 * da + (g + 1) * HEAD_DIM] = (
                dy[:, sl] * u[:, sl] * mixed * gs[:, sl]).astype(bf16)
            dmb = dmix.astype(bf16)
            dw_ref[g] += jnp.where(tri, _dot(dmb, vn[:, sl], NT), 0.0)
            dvn_ref[:, sl] = _dot(wm, dmb, TN)
            db = db + jnp.where(lane == g, jnp.sum(dmix, axis=1, keepdims=True), 0.0)
        db_ref[...] += db
        dvn = dvn_ref[...]
        dlg_ref[...] += jnp.sum(dvn * vhat, axis=0, keepdims=True)
        dlb_ref[...] += jnp.sum(dvn, axis=0, keepdims=True)
        dvhat = dvn * lg
        dv = rs * (dvhat - jnp.mean(dvhat, axis=-1, keepdims=True)
                   - vhat * jnp.mean(dvhat * vhat, axis=-1, keepdims=True))
        dp_ref[:, da:2 * da] = (dv * gv).astype(bf16)

    blk = lambda cb: pl.BlockSpec((CHUNK, da), lambda c: (c, cb))
    full = lambda shp: pl.BlockSpec(shp, lambda c: (0,) * len(shp))
    return pl.pallas_call(
        body, name="sgu_bwd", grid=(S // CHUNK,),
        in_specs=[blk(0), blk(1), blk(2), blk(0), _layer_spec(ln_g, l), _layer_spec(ln_b, l), _layer_spec(w_s, l),
                  _layer_spec(b_t, l), pl.BlockSpec(memory_space=pl.ANY)],
        out_specs=[pl.BlockSpec((CHUNK, 3 * da), lambda c: (c, 0)), full((A_GROUPS, CHUNK, CHUNK)),
                   full((CHUNK, HEAD_DIM)), full((1, da)), full((1, da))],
        out_shape=[SDS((S, n_proj), bf16), SDS((A_GROUPS, CHUNK, CHUNK), f32), SDS((CHUNK, HEAD_DIM), f32),
                   SDS((1, da), f32), SDS((1, da), f32)],
        scratch_shapes=[pltpu.VMEM((CHUNK, da), f32)], compiler_params=_params(),
    )(proj, proj, proj, dy, ln_g, ln_b, w_s, b_t, after)


def _sb_scores(q, kblk, kb, rows, cols, masked):
    z = _dot(q, kblk, NT) * (1.0 / math.sqrt(HEAD_DIM))
    t = jnp.log(1.0 + jnp.exp(-jnp.abs(z)))
    log_1mb = -(jnp.maximum(z, 0.0) + t)
    log_beta = jnp.minimum(z, 0.0) - t
    if not masked:
        return None, log_beta, log_1mb
    causal = (cols + kb * TK) < rows
    return causal, log_beta, jnp.where(causal, log_1mb, 0.0)


def _sb_tiles(i):
    rows = i * TQ + lax.broadcasted_iota(jnp.int32, (TQ, TK), 0)
    cols = lax.broadcasted_iota(jnp.int32, (TQ, TK), 1)
    r_i = lax.broadcasted_iota(jnp.int32, (TK, TK), 0)
    c_i = lax.broadcasted_iota(jnp.int32, (TK, TK), 1)
    upper, lower = (r_i > c_i).astype(bf16), (r_i < c_i).astype(bf16)
    slot = lax.broadcasted_iota(jnp.int32, (TQ, CARRY_LANES), 1)
    return rows, cols, slot, jnp.concatenate([upper, upper], axis=0), jnp.concatenate([lower, lower], axis=0)


def _suffix_sum(t, tri):
    hi = lax.bitcast_convert_type(lax.bitcast_convert_type(t, jnp.uint32) & jnp.uint32(0xFFFF0000), f32)
    both = jnp.concatenate([hi.astype(bf16), (t - hi).astype(bf16)], axis=1)
    return _dot(both, tri)


def _sb_fwd(proj, y_prev, col0, after):
    S = proj.shape[0]
    D = y_prev.shape[1]
    dh = N_HEADS * HEAD_DIM
    n_diag = TQ // TK

    def body(q_ref, k_ref, v_ref, z_ref, yp_ref, after_ref, y_ref, o_ref, car_ref, qb, kb_s, vb_s, c_ref):
        i = pl.program_id(0)

        @pl.when(i == 0)
        def _():
            kb_s[...] = k_ref[...].astype(bf16)
            vb_s[...] = v_ref[...].astype(bf16)

        qb[...] = q_ref[...].astype(bf16)
        o_ref[...] = jnp.zeros_like(o_ref)
        c_ref[...] = jnp.zeros_like(c_ref)
        car_ref[...] = jnp.zeros_like(car_ref)
        nkb = (i + 1) * n_diag
        rows, cols, slot, upper, _ = _sb_tiles(i)

        def make_step(masked):
            def step(jj, carry):
                kb = nkb - 1 - jj
                off = pl.multiple_of(kb * TK, TK)
                hs = range(N_HEADS)
                sls = [slice(h * HEAD_DIM, (h + 1) * HEAD_DIM) for h in hs]
                sc = [_sb_scores(qb[:, sls[h]], kb_s[pl.ds(off, TK), sls[h]], kb, rows, cols, masked) for h in hs]
                suf = [_suffix_sum(sc[h][2], upper) for h in hs]
                cs = [c_ref[h] for h in hs]
                es = [jnp.exp(sc[h][1] + suf[h] + cs[h][:, :1]) for h in hs]
                if masked:
                    es = [jnp.where(sc[h][0], es[h], 0.0) for h in hs]
                pv = [_dot(es[h].astype(bf16), vb_s[pl.ds(off, TK), sls[h]]) for h in hs]
                for h in hs:
                    o_ref[:, sls[h]] += pv[h]
                    car_ref[h] = jnp.where(slot == kb, cs[h], car_ref[h])
                    c_ref[h] = cs[h] + jnp.sum(sc[h][2], axis=1, keepdims=True)
                return carry
            return step

        lax.fori_loop(0, n_diag, make_step(True), 0)
        lax.fori_loop(n_diag, nkb, make_step(False), 0)
        z = z_ref[...]
        y_ref[...] = (o_ref[...] * (z * jax.nn.sigmoid(z))).astype(bf16)

    cb = col0 * HEAD_DIM // dh
    qspec = lambda k: pl.BlockSpec((TQ, dh), lambda i: (i, cb + k))
    kspec = lambda k: pl.BlockSpec((S, dh), lambda i: (0, cb + k))
    return pl.pallas_call(
        body, name="sb_fwd", grid=(S // TQ,),
        in_specs=[qspec(0), kspec(1), kspec(2), qspec(3), pl.BlockSpec(memory_space=pl.ANY),
                  pl.BlockSpec(memory_space=pl.ANY)],
        out_specs=[pl.BlockSpec((TQ, dh), lambda i: (i, A_GROUPS * HEAD_DIM // dh)),
                   pl.BlockSpec((TQ, dh), lambda i: (i, 0)),
                   pl.BlockSpec((N_HEADS, TQ, CARRY_LANES), lambda i: (0, i, 0))],
        out_shape=[SDS((S, D), bf16), SDS((S, dh), f32), SDS((N_HEADS, S, CARRY_LANES), f32)],
        input_output_aliases={4: 0},
        scratch_shapes=[pltpu.VMEM((TQ, dh), bf16), pltpu.VMEM((S, dh), bf16), pltpu.VMEM((S, dh), bf16),
                        pltpu.VMEM((N_HEADS, TQ, CARRY_LANES), f32)],
        compiler_params=_params(),
    )(proj, proj, proj, proj, y_prev, after)


def _sb_bwd(proj, o, car, dy, dproj_prev, col0, after):
    S = proj.shape[0]
    n_i = S // TQ
    dh = N_HEADS * HEAD_DIM
    n_diag = TQ // TK
    cb = col0 * HEAD_DIM // dh
    scale = 1.0 / math.sqrt(HEAD_DIM)

    def body(q_ref, k_ref, v_ref, z_ref, o_ref, car_ref, dy_ref, dpp_ref, after_ref,
             dp_ref, qb, kb_s, vb_s, dob, p_ref, dq_acc, dk_acc, dv_acc, st_a, st_b, st_k, st_v, tile_sems):
        i = pl.program_id(0)

        def put(stage_ref, row0, nrows, k):
            pltpu.sync_copy(stage_ref, dp_ref.at[pl.ds(row0, nrows), pl.ds((cb + k) * dh, dh)])

        def tile_copies(step):
            rows = pl.ds(pl.multiple_of(step * TQ, TQ), TQ)
            return [pltpu.make_async_copy(st, dp_ref.at[rows, pl.ds((cb + k) * dh, dh)], tile_sems.at[n])
                    for n, (st, k) in enumerate(((st_a, 0), (st_b, 3)))]

        @pl.when(i > 0)
        def _():
            for cp in tile_copies(i - 1):
                cp.wait()

        @pl.when(i == 0)
        def _():
            kb_s[...] = k_ref[...].astype(bf16)
            vb_s[...] = v_ref[...].astype(bf16)
            dk_acc[...] = jnp.zeros_like(dk_acc)
            dv_acc[...] = jnp.zeros_like(dv_acc)

        s, gs = _silu_and_grad(z_ref[...])
        dy = dy_ref[...]
        st_b[...] = (dy * o_ref[...] * gs).astype(bf16)
        dob[...] = (dy * s).astype(bf16)
        qb[...] = q_ref[...].astype(bf16)
        p_ref[...] = jnp.zeros_like(p_ref)
        dq_acc[...] = jnp.zeros_like(dq_acc)
        nkb = (i + 1) * n_diag
        rows, cols, slot, upper, lower = _sb_tiles(i)

        def make_step(masked):
            def step(kb, carry):
                off = pl.multiple_of(kb * TK, TK)
                hs = range(N_HEADS)
                sls = [slice(h * HEAD_DIM, (h + 1) * HEAD_DIM) for h in hs]
                qs = [qb[:, sls[h]] for h in hs]
                ks = [kb_s[pl.ds(off, TK), sls[h]] for h in hs]
                dos = [dob[:, sls[h]] for h in hs]
                sc = [_sb_scores(qs[h], ks[h], kb, rows, cols, masked) for h in hs]
                da = [_dot(dos[h], vb_s[pl.ds(off, TK), sls[h]], NT) for h in hs]
                suf = [_suffix_sum(sc[h][2], upper) for h in hs]
                onehot = slot == kb
                cs = [jnp.sum(jnp.where(onehot, car_ref[h], 0.0), axis=1, keepdims=True) for h in hs]
                es = [jnp.exp(sc[h][1] + suf[h] + cs[h]) for h in hs]
                if masked:
                    es = [jnp.where(sc[h][0], es[h], 0.0) for h in hs]
                gs_ = [da[h] * es[h] for h in hs]
                ps = [p_ref[h] for h in hs]
                pre = [_suffix_sum(gs_[h], lower) + ps[h][:, :1] for h in hs]
                dzs = []
                for h in hs:
                    beta = jnp.exp(sc[h][1])
                    dzz = gs_[h] * (1.0 - beta) - beta * pre[h]
                    if masked:
                        dzz = jnp.where(sc[h][0], dzz, 0.0)
                    dzs.append((dzz * scale).astype(bf16))
                dqs = [_dot(dzs[h], ks[h]) for h in hs]
                dks = [_dot(dzs[h], qs[h], TN) for h in hs]
                dvs = [_dot(es[h].astype(bf16), dos[h], TN) for h in hs]
                for h in hs:
                    dq_acc[:, sls[h]] += dqs[h]
                    dk_acc[pl.ds(off, TK), sls[h]] += dks[h]
                    dv_acc[pl.ds(off, TK), sls[h]] += dvs[h]
                    p_ref[h] = ps[h] + jnp.sum(gs_[h], axis=1, keepdims=True)
                return carry
            return step

        lax.fori_loop(0, nkb - n_diag, make_step(False), 0)
        lax.fori_loop(nkb - n_diag, nkb, make_step(True), 0)
        st_a[...] = dq_acc[...].astype(bf16)
        for cp in tile_copies(i):
            cp.start()

        @pl.when(i == n_i - 1)
        def _():
            st_k[...] = dk_acc[...].astype(bf16)
            st_v[...] = dv_acc[...].astype(bf16)
            put(st_k, 0, S, 1)
            put(st_v, 0, S, 2)
            for cp in tile_copies(i):
                cp.wait()

    qspec = lambda k: pl.BlockSpec((TQ, dh), lambda i: (i, cb + k))
    kspec = lambda k: pl.BlockSpec((S, dh), lambda i: (0, cb + k))
    return pl.pallas_call(
        body, name="sb_bwd", grid=(n_i,),
        in_specs=[qspec(0), kspec(1), kspec(2), qspec(3),
                  pl.BlockSpec((TQ, dh), lambda i: (i, 0)),
                  pl.BlockSpec((N_HEADS, TQ, CARRY_LANES), lambda i: (0, i, 0)),
                  pl.BlockSpec((TQ, dh), lambda i: (i, A_GROUPS * HEAD_DIM // dh)),
                  pl.BlockSpec(memory_space=pl.ANY), pl.BlockSpec(memory_space=pl.ANY)],
        out_specs=pl.BlockSpec(memory_space=pl.ANY),
        out_shape=SDS(dproj_prev.shape, bf16),
        input_output_aliases={7: 0},
        scratch_shapes=[pltpu.VMEM((TQ, dh), bf16), pltpu.VMEM((S, dh), bf16), pltpu.VMEM((S, dh), bf16),
                        pltpu.VMEM((TQ, dh), bf16), pltpu.VMEM((N_HEADS, TQ, CARRY_LANES), f32), pltpu.VMEM((TQ, dh), f32),
                        pltpu.VMEM((S, dh), f32), pltpu.VMEM((S, dh), f32),
                        pltpu.VMEM((TQ, dh), bf16), pltpu.VMEM((TQ, dh), bf16),
                        pltpu.VMEM((S, dh), bf16), pltpu.VMEM((S, dh), bf16), pltpu.SemaphoreType.DMA((2,))],
        compiler_params=_params(56),
    )(proj, proj, proj, proj, o, car, dy, dproj_prev, after)


def _mem_kv(mem, mg, l, w_kv_g):
    M, D = mem.shape
    rb, n = w_kv_g.shape[1], w_kv_g.shape[2]

    def body(m_ref, g_ref, w_ref, kv_ref):
        mv = m_ref[...]
        r = lax.rsqrt(jnp.mean(mv * mv, axis=-1, keepdims=True) + EPS)
        mh = (mv * r * g_ref[...]).astype(bf16)
        kv_ref[...] = _dot(mh, w_ref[...].reshape(N_DEV * rb, n))

    return pl.pallas_call(
        body, name="mem_kv", grid=(1,),
        in_specs=[pl.BlockSpec((M, D), lambda i: (0, 0)), _layer_spec(mg, l),
                  pl.BlockSpec((N_DEV, rb, n), lambda i: (0, 0, 0))],
        out_specs=pl.BlockSpec((M, n), lambda i: (0, 0)),
        out_shape=SDS((M, n), f32), compiler_params=_params(),
    )(mem, mg, w_kv_g)


def _xattn_head(q_ref, kv_ref, qg, kg, h):
    dc = N_HEADS * HEAD_DIM
    sl = slice(h * HEAD_DIM, (h + 1) * HEAD_DIM)
    qh = q_ref[:, sl]
    rq = lax.rsqrt(jnp.mean(qh * qh, axis=-1, keepdims=True) + EPS)
    qhat = qh * rq
    qn = (qhat * qg).astype(bf16)
    kh = kv_ref[:, sl]
    rk = lax.rsqrt(jnp.mean(kh * kh, axis=-1, keepdims=True) + EPS)
    kn = (kh * rk * kg).astype(bf16)
    vh = kv_ref[:, dc + h * HEAD_DIM:dc + (h + 1) * HEAD_DIM].astype(bf16)
    s = _dot(qn, kn, NT) * (1.0 / math.sqrt(HEAD_DIM))
    e = jnp.exp(s - jnp.max(s, axis=-1, keepdims=True))
    p = e / jnp.sum(e, axis=-1, keepdims=True)
    o = _dot(p.astype(bf16), vh)
    return sl, rq, qhat, qn, kn, vh, p, o


def _xattn_fwd(proj, kv, qg, kg, l, y_prev, col0, tq=512):
    S = proj.shape[0]
    D = y_prev.shape[1]
    dc = N_HEADS * HEAD_DIM
    M = kv.shape[0]

    def body(q_ref, z_ref, kv_ref, qg_ref, kg_ref, yp_ref, y_ref):
        for h in range(N_HEADS):
            sl, _, _, _, _, _, _, o = _xattn_head(q_ref, kv_ref, qg_ref[...], kg_ref[...], h)
            z = z_ref[:, sl]
            y_ref[:, sl] = (o * (z * jax.nn.sigmoid(z))).astype(bf16)

    full = lambda shp: pl.BlockSpec(shp, lambda i: (0,) * len(shp))
    return pl.pallas_call(
        body, name="xattn_fwd", grid=(S // tq,),
        in_specs=[pl.BlockSpec((tq, dc), lambda i: (i, col0)), pl.BlockSpec((tq, dc), lambda i: (i, col0 + 1)),
                  full((M, 2 * dc)), _layer_spec(qg, l), _layer_spec(kg, l), pl.BlockSpec(memory_space=pl.ANY)],
        out_specs=pl.BlockSpec((tq, dc), lambda i: (i, D // dc - 1)),
        out_shape=SDS((S, D), bf16), input_output_aliases={5: 0}, compiler_params=_params(),
    )(proj, proj, kv, qg, kg, y_prev)


def _xattn_bwd(proj, kv, qg, kg, l, dy, dproj_prev, col0, tq=512):
    S = proj.shape[0]
    D = dy.shape[1]
    dc = N_HEADS * HEAD_DIM
    M = kv.shape[0]

    def body(q_ref, z_ref, kv_ref, qg_ref, kg_ref, dy_ref, dpp_ref, dp_ref, dkn_ref, dv_ref, dqg_ref):
        @pl.when(pl.program_id(0) == 0)
        def _():
            dkn_ref[...] = jnp.zeros_like(dkn_ref)
            dv_ref[...] = jnp.zeros_like(dv_ref)
            dqg_ref[...] = jnp.zeros_like(dqg_ref)

        qg = qg_ref[...]
        for h in range(N_HEADS):
            sl, rq, qhat, qn, kn, vh, p, o = _xattn_head(q_ref, kv_ref, qg, kg_ref[...], h)
            s, gs = _silu_and_grad(z_ref[:, sl])
            dyh = dy_ref[:, sl]
            dp_ref[:, dc + h * HEAD_DIM:dc + (h + 1) * HEAD_DIM] = (dyh * o * gs).astype(bf16)
            dob = (dyh * s).astype(bf16)
            dpr = _dot(dob, vh, NT)
            dv_ref[:, sl] += _dot(p.astype(bf16), dob, TN)
            ds = (p * (dpr - jnp.sum(p * dpr, axis=-1, keepdims=True)) * (1.0 / math.sqrt(HEAD_DIM))).astype(bf16)
            dqn = _dot(ds, kn)
            dkn_ref[:, sl] += _dot(ds, qn, TN)
            dqg_ref[...] += jnp.sum(dqn * qhat, axis=0, keepdims=True)
            dqhat = dqn * qg
            dp_ref[:, sl] = (rq * (dqhat - qhat * jnp.mean(dqhat * qhat, axis=-1, keepdims=True))).astype(bf16)

    full = lambda shp: pl.BlockSpec(shp, lambda i: (0,) * len(shp))
    return pl.pallas_call(
        body, name="xattn_bwd", grid=(S // tq,),
        in_specs=[pl.BlockSpec((tq, dc), lambda i: (i, col0)), pl.BlockSpec((tq, dc), lambda i: (i, col0 + 1)),
                  full((M, 2 * dc)), _layer_spec(qg, l), _layer_spec(kg, l),
                  pl.BlockSpec((tq, dc), lambda i: (i, D // dc - 1)), pl.BlockSpec(memory_space=pl.ANY)],
        out_specs=[pl.BlockSpec((tq, 2 * dc), lambda i: (i, col0 // 2)), full((M, dc)), full((M, dc)),
                   full((1, HEAD_DIM))],
        out_shape=[SDS(dproj_prev.shape, bf16), SDS((M, dc), f32), SDS((M, dc), f32), SDS((1, HEAD_DIM), f32)],
        input_output_aliases={6: 0}, compiler_params=_params(),
    )(proj, proj, kv, qg, kg, dy, dproj_prev)


def _mem_bwd(mem, mg, kg, l, kv, dkn, dv, w_kv_g):
    M, D = mem.shape
    rb, n = w_kv_g.shape[1], w_kv_g.shape[2]
    dc = n // 2

    def body(m_ref, g_ref, kv_ref, dkn_ref, dv_ref, kg_ref, w_ref, dw_ref, dmg_ref, dkg_ref, dkv_ref):
        mv = m_ref[...]
        r = lax.rsqrt(jnp.mean(mv * mv, axis=-1, keepdims=True) + EPS)
        mhat = mv * r
        mh = (mhat * g_ref[...]).astype(bf16)
        kg = kg_ref[...]
        dkg = jnp.zeros((1, HEAD_DIM), f32)
        for h in range(N_HEADS):
            sl = slice(h * HEAD_DIM, (h + 1) * HEAD_DIM)
            kh = kv_ref[:, sl]
            rk = lax.rsqrt(jnp.mean(kh * kh, axis=-1, keepdims=True) + EPS)
            khat = kh * rk
            dkn_h = dkn_ref[:, sl]
            dkg = dkg + jnp.sum(dkn_h * khat, axis=0, keepdims=True)
            dkhat = dkn_h * kg
            dkv_ref[:, sl] = (rk * (dkhat - khat * jnp.mean(dkhat * khat, axis=-1, keepdims=True))).astype(bf16)
        dkv_ref[:, dc:] = dv_ref[...].astype(bf16)
        dkg_ref[...] = dkg
        dkv = dkv_ref[...]
        dw_ref[...] = _dot(mh, dkv, TN).astype(bf16).reshape(N_DEV, rb, n)
        dmh = _dot(dkv, w_ref[...].reshape(N_DEV * rb, n), NT)
        dmg_ref[...] = jnp.sum(dmh * mhat, axis=0, keepdims=True)

    full = lambda shp: pl.BlockSpec(shp, lambda i: (0,) * len(shp))
    wspec = full((N_DEV, rb, n))
    return pl.pallas_call(
        body, name="mem_bwd", grid=(1,),
        in_specs=[full((M, D)), _layer_spec(mg, l), full((M, n)), full((M, dc)), full((M, dc)), _layer_spec(kg, l), wspec],
        out_specs=[wspec, full((1, D)), full((1, HEAD_DIM))],
        out_shape=[SDS((N_DEV, rb, n), bf16), SDS((1, D), f32), SDS((1, HEAD_DIM), f32)],
        scratch_shapes=[pltpu.VMEM((M, n), bf16)], compiler_params=_params(),
    )(mem, mg, kv, dkn, dv, kg, w_kv_g)


SMALL = ("norm_g", "sgu_ln_g", "sgu_ln_b", "sgu_w", "sgu_b", "mem_norm_g", "q_norm_g", "k_norm_g")


def _small_rows(like):
    rows = [math.prod(like[n].shape) // 128 for n in SMALL]
    offs = [0]
    for r in rows:
        offs.append(offs[-1] + -(-r // 8) * 8)
    return rows, offs


def _pack_small(parts, offs):
    pieces = []
    for k, n in enumerate(SMALL):
        a = parts[n].reshape(-1, 128)
        pieces.append(jnp.pad(a, ((0, offs[k + 1] - offs[k] - a.shape[0]), (0, 0))))
    return jnp.concatenate(pieces)


def kernel(x, mem, norm_g, w_in, sgu_ln_g, sgu_ln_b, sgu_w, sgu_b, mem_norm_g, w_mem_kv, q_norm_g, k_norm_g, w_out, loss_target, m_norm_g, m_w_in, m_sgu_ln_g, m_sgu_ln_b, m_sgu_w, m_sgu_b, m_mem_norm_g, m_w_mem_kv, m_q_norm_g, m_k_norm_g, m_w_out, v_norm_g, v_w_in, v_sgu_ln_g, v_sgu_ln_b, v_sgu_w, v_sgu_b, v_mem_norm_g, v_w_mem_kv, v_q_norm_g, v_k_norm_g, v_w_out):
    L, D, wc = w_in.shape
    S = x.shape[1]
    da = D // 2
    xs = x.reshape(S, D)
    mems = mem.reshape(mem.shape[1], D)
    tgt = loss_target.reshape(S, D)
    stacked = lambda a: a.reshape(a.shape[0], 1, -1)
    ng, lng, lnb, mg, qg, kg = map(stacked, (norm_g, sgu_ln_g, sgu_ln_b, mem_norm_g, q_norm_g, k_norm_g))
    b_t = jnp.swapaxes(sgu_b, 1, 2)
    sb_col, xa_col = 3 * da // HEAD_DIM, (3 * da + D) // (D // 4)

    ax, ay, ac = lax.axis_index("x"), lax.axis_index("y"), lax.axis_index("c")
    ids = jnp.stack([4 * ax + 2 * ay + ac, 2 * ax + ay, ac]).astype(jnp.int32)
    w_in0_b = _cast_into_slot("cast_w_in", w_in, 0, 512, ids, ids)
    first = _gather3_start("gather_w_in0", [w_in0_b], ids)
    late = first["token"]
    w_b = [(w_in0_b if l == 0 else _cast_into_slot("cast_w_in", w_in, l, 512, ids, late),
            _cast_into_slot("cast_w_kv", w_mem_kv, l, 256, ids, late),
            _cast_into_slot("cast_w_out", w_out, l, 256, ids, late)) for l in range(L)]
    relay = _gather3_relay("gather_w_in0_relay",
                           _split_wait(first, *[a for wl in w_b for a in wl if a is not w_in0_b]), ids)
    in_fwd = _gather2_forward("gather_w_in0_forward", _split_wait(relay, relay["token"]), ids)

    acts = []
    xl = xs
    for l in range(L):
        (w_in_g,) = _split_wait(in_fwd, xl if l else in_fwd["token"])
        rest = _gather3_start(f"gather_w_rest{l}", [w_b[l][1], w_b[l][2]], w_in_g)
        order = rest["token"]
        if l + 1 < L:
            nxt = _gather3_start(f"gather_w_in{l + 1}", [w_b[l + 1][0]], order)
            order = nxt["token"]
        proj, h = _rms_proj(xl, ng, l, w_in_g, order)
        rest_relay = _gather3_relay(f"gather_w_rest{l}_relay", _split_wait(rest, proj), proj)
        y = _sgu_fwd(proj, lng, lnb, sgu_w, b_t, l, rest_relay["token"])
        rest_fwd = _gather2_forward(f"gather_w_rest{l}_forward", _split_wait(rest_relay, y), y)
        order = rest_fwd["token"]
        if l + 1 < L:
            nxt_relay = _gather3_relay(f"gather_w_in{l + 1}_relay", _split_wait(nxt, order), order)
            order = nxt_relay["token"]
        y, o_b, car = _sb_fwd(proj, y, sb_col, order)
        w_kv_g, w_out_g = _split_wait(rest_fwd, o_b)
        order = o_b
        if l + 1 < L:
            in_fwd = _gather2_forward(f"gather_w_in{l + 1}_forward", _split_wait(nxt_relay, o_b), o_b)
            order = in_fwd["token"]
        kv = _mem_kv(mems, mg, l, w_kv_g)
        y = _xattn_fwd(proj, kv, qg, kg, l, y, xa_col)
        x_next = _out_proj(xl, y, w_out_g, order)
        acts.append((xl, proj, h, y, o_b, car, kv, w_in_g, w_kv_g, w_out_g))
        xl = x_next

    dx, dxb, loss_part = _loss_and_grad(xl, tgt, 512)
    loss = lax.psum(loss_part[0, 0], ("x", "y", "c"))

    weights = dict(norm_g=norm_g, sgu_ln_g=sgu_ln_g, sgu_ln_b=sgu_ln_b, sgu_w=sgu_w, sgu_b=sgu_b,
                   mem_norm_g=mem_norm_g, q_norm_g=q_norm_g, k_norm_g=k_norm_g)
    moms_m = dict(norm_g=m_norm_g, sgu_ln_g=m_sgu_ln_g, sgu_ln_b=m_sgu_ln_b, sgu_w=m_sgu_w, sgu_b=m_sgu_b,
                  mem_norm_g=m_mem_norm_g, q_norm_g=m_q_norm_g, k_norm_g=m_k_norm_g)
    moms_v = dict(norm_g=v_norm_g, sgu_ln_g=v_sgu_ln_g, sgu_ln_b=v_sgu_ln_b, sgu_w=v_sgu_w, sgu_b=v_sgu_b,
                  mem_norm_g=v_mem_norm_g, q_norm_g=v_q_norm_g, k_norm_g=v_k_norm_g)
    small_rows, small_offs = _small_rows(weights)
    head_rows = D // 128
    assert SMALL[0] == "norm_g" and head_rows % 8 == 0

    seconds = {}
    pending = None
    adam = {"w_out": None, "w_mem_kv": None, "w_in": None}

    late_in = {}

    def update(lu, order):
        (r_out,) = _scatter_finish(seconds[f"g_out{lu}"], order)
        adam["w_out"] = _sum_adam("adam_w_out", r_out, w_out, m_w_out, v_w_out, lu, adam["w_out"], 128, ids, ids)
        r_kv, late_in[lu] = _scatter_finish(seconds[f"g_rest{lu}"], adam["w_out"][0])
        adam["w_mem_kv"] = _sum_adam("adam_w_kv", r_kv, w_mem_kv, m_w_mem_kv, v_w_mem_kv, lu, adam["w_mem_kv"], 256, ids, ids)
        return adam["w_mem_kv"][0]

    def update_in(lu, order):
        adam["w_in"] = _sum_adam("adam_w_in", late_in[lu], w_in, m_w_in, v_w_in, lu, adam["w_in"], 256, ids, order)
        return adam["w_in"][0]
    small = {n: [None] * L for n in SMALL}
    for l in reversed(range(L)):
        xl, proj, h, y, o_b, car, kv, w_in_g, w_kv_g, w_out_g = acts[l]
        dy = _out_bwd_dy(dxb, w_out_g)
        order = dy
        if pending is not None:
            seconds[pending[0]] = _scatter2_second_level(pending[0], pending[1], dy, ids)
            order = seconds[pending[0]]["token"]
        g_out = _tn_grad("out_bwd_dw", y, dxb, 512, 512, True, order)
        seconds[f"g_out{l}"] = _scatter1_start(f"scatter_g_out{l}", [g_out], ids)
        dproj, d_sw, d_sb, d_lg, d_lb = _sgu_bwd(proj, dy, lng, lnb, sgu_w, b_t, l, seconds[f"g_out{l}"]["token"])
        dproj = _sb_bwd(proj, o_b, car, dy, dproj, sb_col, d_lb)
        dproj, dkn, dv, d_qg = _xattn_bwd(proj, kv, qg, kg, l, dy, dproj, xa_col)
        g_kv, d_mg, d_kg = _mem_bwd(mems, mg, kg, l, kv, dkn, dv, w_kv_g)
        for n, val in (("sgu_ln_g", d_lg), ("sgu_ln_b", d_lb), ("sgu_w", d_sw), ("sgu_b", d_sb[:, :A_GROUPS].T),
                       ("mem_norm_g", d_mg), ("q_norm_g", d_qg), ("k_norm_g", d_kg)):
            small[n][l] = val.reshape(-1)
        order = d_kg
        if l == 0:
            small["norm_g"][0] = jnp.zeros_like(small["norm_g"][1])
            part = _pack_small({n: jnp.stack(small[n]) for n in SMALL}, small_offs)
            tail = _gather1_start("gather_small_tail", [_into_slot("small_tail_slot", part[head_rows:], ids)], ids)
            order = tail["token"]
        g_in_l = _tn_grad("in_bwd_dw", h, dproj, 1024, wc, False, order)
        first = _scatter2_pair_start(f"scatter_g_rest{l}_pair", [g_kv, g_in_l], ids)
        order = first["token"]
        pending = (f"g_rest{l}", first)
        if l == 0:
            for lu in reversed(range(1, L)):
                order = update(lu, order)
            seconds[pending[0]] = _scatter2_second_level(pending[0], pending[1], order, ids)
            order = seconds[pending[0]]["token"]
            for lu in reversed(range(1, L)):
                order = update_in(lu, order)
        dh = _in_bwd_dh(dproj, w_in_g, order)
        dx, dxb, d_ng = _rms_bwd(dh, xl, ng, l, dx, order)
        small["norm_g"][l] = d_ng.reshape(-1)
    head = _gather1_start("gather_small_head",
                          [_into_slot("small_head_slot", small["norm_g"][0].reshape(head_rows, 128), ids)], ids)

    (r_out,) = _scatter_finish(seconds["g_out0"], head["token"])
    adam["w_out"] = _sum_adam("adam_w_out", r_out, w_out, m_w_out, v_w_out, 0, adam["w_out"], 128, ids, ids)
    r_kv, r_in = _scatter_finish(seconds["g_rest0"], adam["w_out"][0])
    adam["w_mem_kv"] = _sum_adam("adam_w_kv", r_kv, w_mem_kv, m_w_mem_kv, v_w_mem_kv, 0, adam["w_mem_kv"], 256, ids, ids)
    adam["w_in"] = _sum_adam("adam_w_in", r_in, w_in, m_w_in, v_w_in, 0, adam["w_in"], 256, ids, ids)
    (r_tail,) = _split_wait(tail, adam["w_in"][0])
    (r_head,) = _split_wait(head, r_tail)
    as128 = lambda d: [d[n].reshape(-1, 128) for n in SMALL]
    sm = _small_sum_adam(r_head, r_tail, as128(weights), as128(moms_m), as128(moms_v), small_offs)
    res = dict(adam)
    for p, n in enumerate(SMALL):
        res[n] = [sm[k * len(SMALL) + p].reshape(weights[n].shape) for k in range(4)]

    order = ("norm_g", "w_in", "sgu_ln_g", "sgu_ln_b", "sgu_w", "sgu_b", "mem_norm_g", "w_mem_kv", "q_norm_g",
             "k_norm_g", "w_out")
    outs = [loss, dx.reshape(x.shape)]
    for k in range(4):
        outs += [res[n][k] for n in order]
    return tuple(outs)
```

```python
import functools
import math

import jax
import jax.numpy as jnp
from jax import lax
from jax.experimental import pallas as pl
from jax.experimental.pallas import tpu as pltpu

f32 = jnp.float32
bf16 = jnp.bfloat16
SDS = jax.ShapeDtypeStruct

N_DEV = 8
EPS = 1e-6
CHUNK = 128
SGU_CHUNKS = 2
A_GROUPS = 8
HEAD_DIM = 128
N_HEADS = 4
TQ = 256
TK = 256
CARRY_LANES = 128
ADAM_LR, ADAM_B1, ADAM_B2, ADAM_EPS, ADAM_WD, ADAM_STEP = 0.001, 0.9, 0.999, 1e-08, 0.01, 10
MIB = 1024 * 1024

NT = (((1,), (1,)), ((), ()))
TN = (((0,), (0,)), ((), ()))


def _params(vmem_mib=48):
    return pltpu.CompilerParams(vmem_limit_bytes=vmem_mib * MIB)


def _gelu_and_grad(x):
    e = lax.erf(x * (1.0 / math.sqrt(2.0)))
    cdf = 0.5 * (1.0 + e)
    pdf = jnp.exp(-0.5 * x * x) * (1.0 / math.sqrt(2.0 * math.pi))
    return x * cdf, cdf + x * pdf


def _gelu(x):
    return 0.5 * x * (1.0 + lax.erf(x * (1.0 / math.sqrt(2.0))))


def _silu_and_grad(z):
    sg = jax.nn.sigmoid(z)
    return z * sg, sg * (1.0 + z * (1.0 - sg))


def _layer_spec(stacked, l):
    rest = stacked.shape[1:]
    return pl.BlockSpec((None,) + rest, lambda *idx: (l,) + (0,) * len(rest))


def _dot(a, b, dims=None):
    if dims is None:
        return jnp.dot(a, b, preferred_element_type=f32)
    return lax.dot_general(a, b, dims, preferred_element_type=f32)


_HBM = pl.BlockSpec(memory_space=pltpu.HBM)
_SEM = pl.BlockSpec(memory_space=pltpu.SEMAPHORE)
_EFFECT = pltpu.SideEffectType.DATAFLOW_SIDE_EFFECTING


def _split_start(name, bufs, n_remote, n_local, build, after):
    nb = len(bufs)

    def body(*refs):
        token = refs[-1]
        locals_, remotes = build(refs[:nb], *refs[nb + 1:nb + 4])
        for cp in locals_ + remotes:
            cp.start()
        token[...] = jnp.zeros_like(token)

    hbm = lambda a: pltpu.with_memory_space_constraint(a, pltpu.HBM)
    outs = pl.pallas_call(
        body, name=name,
        out_shape=(pltpu.SemaphoreType.DMA((n_remote,)), pltpu.SemaphoreType.DMA((n_remote,)),
                   pltpu.SemaphoreType.DMA((max(n_local, 1),)),
                   *[pltpu.HBM(b.shape, b.dtype) for b in bufs], SDS((8, 128), f32)),
        in_specs=[_HBM] * nb + [pl.BlockSpec(memory_space=pl.ANY)],
        out_specs=(_SEM, _SEM, _SEM, *[_HBM] * nb, pl.BlockSpec(memory_space=pltpu.VMEM)),
        input_output_aliases={k: 3 + k for k in range(nb)},
        compiler_params=pltpu.CompilerParams(has_side_effects=_EFFECT),
    )(*[hbm(b) for b in bufs], after)
    return dict(name=name, build=build, sems=outs[:3], bufs=outs[3:3 + nb], token=outs[-1])


def _split_wait(handle, *after):
    build, bufs = handle["build"], handle["bufs"]
    nb = len(bufs)

    def body(*refs):
        locals_, remotes = build(refs[:nb], *refs[nb:nb + 3])
        for cp in remotes:
            cp.wait_recv()
        for cp in remotes:
            cp.wait_send()
        for cp in locals_:
            cp.wait()

    outs = pl.pallas_call(
        body, name=handle["name"] + "_wait",
        out_shape=tuple(pltpu.HBM(b.shape, b.dtype) for b in bufs),
        in_specs=[_HBM] * nb + [_SEM] * 3 + [pl.BlockSpec(memory_space=pl.ANY)] * len(after),
        out_specs=tuple([_HBM] * nb),
        input_output_aliases={k: k for k in range(nb)},
        compiler_params=pltpu.CompilerParams(has_side_effects=_EFFECT),
    )(*bufs, *handle["sems"], *after)
    return list(outs)


def _remote(src, dst, send_sems, recv_sems, k, to):
    return pltpu.make_async_remote_copy(src_ref=src, dst_ref=dst, send_sem=send_sems.at[k], recv_sem=recv_sems.at[k],
                                        device_id=to, device_id_type=pl.DeviceIdType.MESH)


def _other_chips(x, y):
    return [(1 - x, y), (x, 1 - y), (1 - x, 1 - y)]


def _all_peers(x, y, c):
    return [(1 - x if m & 4 else x, 1 - y if m & 2 else y, 1 - c if m & 1 else c) for m in range(1, N_DEV)]


def _gather1_start(name, lands, after):
    def build(refs, send, recv, loc):
        x, y, c = lax.axis_index("x"), lax.axis_index("y"), lax.axis_index("c")
        me = 4 * x + 2 * y + c
        return [], [_remote(d.at[me], d.at[me], send, recv, 7 * a + k, peer)
                    for a, d in enumerate(refs) for k, peer in enumerate(_all_peers(x, y, c))]

    return _split_start(name, list(lands), 7 * len(lands), 0, build, after)


def _scatter1_start(name, srcs, after):
    n = len(srcs)

    def build(refs, send, recv, loc):
        x, y, c = lax.axis_index("x"), lax.axis_index("y"), lax.axis_index("c")
        me = 4 * x + 2 * y + c
        return [], [_remote(refs[a].at[4 * px + 2 * py + pc], refs[n + a].at[me], send, recv, 7 * a + k, (px, py, pc))
                    for a in range(n) for k, (px, py, pc) in enumerate(_all_peers(x, y, c))]

    return _split_start(name, list(srcs) + [lax.empty(s.shape, s.dtype) for s in srcs], 7 * n, 0, build, after)


def _gather2_start(name, lands, after):
    def build(refs, send, recv, loc):
        x, y, c = lax.axis_index("x"), lax.axis_index("y"), lax.axis_index("c")
        me = 4 * x + 2 * y + c
        remotes = []
        for a, d in enumerate(refs):
            remotes.append(_remote(d.at[me], d.at[me], send, recv, 4 * a, (x, y, 1 - c)))
            remotes += [_remote(d.at[me], d.at[me], send, recv, 4 * a + 1 + k, (px, py, c))
                        for k, (px, py) in enumerate(_other_chips(x, y))]
        return [], remotes

    return _split_start(name, list(lands), 4 * len(lands), 0, build, after)


def _gather3_start(name, lands, after):
    def build(refs, send, recv, loc):
        x, y, c = lax.axis_index("x"), lax.axis_index("y"), lax.axis_index("c")
        me = 4 * x + 2 * y + c
        return [], [_remote(d.at[me], d.at[me], send, recv, 3 * a + k, to)
                    for a, d in enumerate(refs) for k, to in enumerate([(x, y, 1 - c), (1 - x, y, c), (x, 1 - y, c)])]

    return _split_start(name, list(lands), 3 * len(lands), 0, build, after)


def _gather3_relay(name, lands, after):
    def build(refs, send, recv, loc):
        x, y, c = lax.axis_index("x"), lax.axis_index("y"), lax.axis_index("c")
        from_x = c == 0
        slot = 4 * jnp.where(from_x, 1 - x, x) + 2 * jnp.where(from_x, y, 1 - y) + c
        to = (jnp.where(from_x, x, 1 - x), jnp.where(from_x, 1 - y, y), c)
        return [], [_remote(d.at[slot], d.at[slot], send, recv, a, to) for a, d in enumerate(refs)]

    return _split_start(name, list(lands), len(lands), 0, build, after)


def _gather2_forward(name, lands, after):
    n = len(lands)

    def build(refs, send, recv, loc):
        x, y, c = lax.axis_index("x"), lax.axis_index("y"), lax.axis_index("c")
        slots = [4 * px + 2 * py + c for px, py in _other_chips(x, y)]
        return [], [_remote(d.at[sl], d.at[sl], send, recv, 3 * a + k, (x, y, 1 - c))
                    for a, d in enumerate(refs) for k, sl in enumerate(slots)]

    return _split_start(name, list(lands), 3 * n, 0, build, after)


def _scatter2_pair_start(name, srcs, after):
    n = len(srcs)

    def build(refs, send, recv, loc):
        x, y, c = lax.axis_index("x"), lax.axis_index("y"), lax.axis_index("c")
        return [], [_remote(refs[a].at[2 * q + 1 - c], refs[n + a].at[q], send, recv, 4 * a + q, (x, y, 1 - c))
                    for a in range(n) for q in range(4)]

    lands = [lax.empty((4,) + s.shape[1:], s.dtype) for s in srcs]
    return _split_start(name, list(srcs) + lands, 4 * n, 0, build, after)


def _scatter2_chip_start(name, pairs, after):
    n = len(pairs)

    def build(refs, send, recv, loc):
        x, y, c = lax.axis_index("x"), lax.axis_index("y"), lax.axis_index("c")
        return [], [_remote(refs[a].at[2 * px + py], refs[n + a].at[2 * x + y], send, recv, 3 * a + k, (px, py, c))
                    for a in range(n) for k, (px, py) in enumerate(_other_chips(x, y))]

    return _split_start(name, list(pairs) + [lax.empty(p.shape, p.dtype) for p in pairs], 3 * n, 0, build, after)


def _pair_sum(name, src, theirs, ids):
    _, R, C = theirs.shape
    tr = min(R, 1024)

    def body(ids_ref, a_ref, b_ref, o_ref):
        o_ref[...] = (a_ref[...].astype(f32) + b_ref[...].astype(f32)).astype(bf16)

    spec = pl.BlockSpec((None, tr, C), lambda q, i, ids: (q, i, 0))
    return pl.pallas_call(
        body, name=name,
        grid_spec=pltpu.PrefetchScalarGridSpec(
            num_scalar_prefetch=1, grid=(4, R // tr),
            in_specs=[pl.BlockSpec((None, tr, C), lambda q, i, ids: (2 * q + ids[2], i, 0)), spec], out_specs=spec),
        out_shape=SDS(theirs.shape, bf16), compiler_params=_params(),
    )(ids, src, theirs)


def _scatter2_second_level(name, first, after, ids):
    outs = _split_wait(first, after)
    n = len(outs) // 2
    pairs = [_pair_sum(f"pair_sum_{name}{a}", outs[a], outs[n + a], ids) for a in range(n)]
    return _scatter2_chip_start(f"scatter_{name}_chip", pairs, ids)


def _scatter_finish(second, after):
    outs = _split_wait(second, after)
    n = len(outs) // 2
    return [(outs[a], outs[n + a]) for a in range(n)]


def _cast_into_slot(name, w, l, tr, ids, after):
    _, R, C = w.shape

    def body(ids_ref, w_ref, after_ref, o_ref):
        o_ref[...] = w_ref[...].astype(bf16)

    return pl.pallas_call(
        body, name=name,
        grid_spec=pltpu.PrefetchScalarGridSpec(
            num_scalar_prefetch=1, grid=(R // tr,),
            in_specs=[pl.BlockSpec((None, tr, C), lambda i, ids: (l, i, 0)), pl.BlockSpec(memory_space=pl.ANY)],
            out_specs=pl.BlockSpec((None, tr, C), lambda i, ids: (ids[0], i, 0))),
        out_shape=SDS((N_DEV, R, C), bf16), compiler_params=_params(),
    )(ids, w, after)


def _into_slot(name, a, ids):
    R, C = a.shape

    def body(ids_ref, a_ref, o_ref):
        o_ref[...] = a_ref[...]

    return pl.pallas_call(
        body, name=name,
        grid_spec=pltpu.PrefetchScalarGridSpec(
            num_scalar_prefetch=1, grid=(1,),
            in_specs=[pl.BlockSpec((R, C), lambda i, ids: (0, 0))],
            out_specs=pl.BlockSpec((None, R, C), lambda i, ids: (ids[0], 0, 0))),
        out_shape=SDS((N_DEV, R, C), f32), compiler_params=_params(),
    )(ids, a)


def _adam_math(w, g, m, v):
    m2 = ADAM_B1 * m + (1.0 - ADAM_B1) * g
    v2 = ADAM_B2 * v + (1.0 - ADAM_B2) * (g * g)
    m_hat = m2 / (1.0 - ADAM_B1 ** ADAM_STEP)
    v_hat = v2 / (1.0 - ADAM_B2 ** ADAM_STEP)
    delta = -ADAM_LR * (m_hat / (jnp.sqrt(v_hat) + ADAM_EPS) + ADAM_WD * w)
    return delta, m2, v2


def _sum_adam(name, pair_recv, w, m, v, l, prev, tr, ids, after):
    own, recv = pair_recv
    L, R, C = w.shape
    slots = recv.shape[0]
    mine = 0 if slots == N_DEV else 1

    def body(ids_ref, r_ref, own_ref, w_ref, m_ref, v_ref, after_ref, *rest):
        g_ref, d_ref, m2_ref, v2_ref = rest[-4:]
        terms = [jnp.where(ids_ref[mine] == q, own_ref[...], r_ref[q]).astype(f32) for q in range(slots)]
        g = terms[0]
        for t in terms[1:]:
            g = g + t
        d, m2, v2 = _adam_math(w_ref[...], g, m_ref[...], v_ref[...])
        g_ref[...] = g
        d_ref[...] = d
        m2_ref[...] = m2
        v2_ref[...] = v2

    wspec = pl.BlockSpec((None, tr, C), lambda i, ids: (l, i, 0))
    in_specs = [pl.BlockSpec((slots, tr, C), lambda i, ids: (0, i, 0)),
                pl.BlockSpec((None, tr, C), lambda i, ids: (ids[mine], i, 0)), wspec, wspec, wspec,
                pl.BlockSpec(memory_space=pl.ANY)]
    args = [ids, recv, own, w, m, v, after]
    aliases = {}
    if prev is not None:
        in_specs += [pl.BlockSpec(memory_space=pl.ANY)] * 4
        args += list(prev)
        aliases = {7 + k: k for k in range(4)}
    return pl.pallas_call(
        body, name=name,
        grid_spec=pltpu.PrefetchScalarGridSpec(num_scalar_prefetch=1, grid=(R // tr,), in_specs=in_specs,
                                               out_specs=[wspec] * 4),
        out_shape=[SDS((L, R, C), f32)] * 4, input_output_aliases=aliases, compiler_params=_params(),
    )(*args)


def _small_sum_adam(recv_head, recv_tail, ws, ms, vs, offs):
    n = len(ws)
    r0 = recv_head.shape[1]

    def body(*refs):
        rh, rt = refs[0], refs[1]
        w_refs, m_refs, v_refs = refs[2:2 + n], refs[2 + n:2 + 2 * n], refs[2 + 2 * n:2 + 3 * n]
        outs = refs[2 + 3 * n:]
        for p in range(n):
            lo, hi = offs[p], offs[p] + ws[p].shape[0]
            pieces = []
            if lo < r0:
                pieces.append((rh, lo, 0, min(hi, r0) - lo))
            if hi > r0:
                pieces.append((rt, max(lo, r0) - r0, max(lo, r0) - lo, hi - max(lo, r0)))
            for src, a, b, cnt in pieces:
                g = src[0, a:a + cnt, :]
                for s in range(1, N_DEV):
                    g = g + src[s, a:a + cnt, :]
                d, m2, v2 = _adam_math(w_refs[p][b:b + cnt, :], g, m_refs[p][b:b + cnt, :], v_refs[p][b:b + cnt, :])
                for k, val in enumerate((g, d, m2, v2)):
                    outs[k * n + p][b:b + cnt, :] = val

    return pl.pallas_call(
        body, name="small_sum_adam", out_shape=[SDS(w.shape, f32) for w in ws] * 4, compiler_params=_params(),
    )(recv_head, recv_tail, *ws, *ms, *vs)


def _loss_and_grad(xf, tgt, tm):
    S, D = xf.shape

    def body(x_ref, t_ref, dx_ref, dxb_ref, l_ref):
        i = pl.program_id(0)
        d = x_ref[...] - t_ref[...]
        dx = d * (1.0 / D)
        dx_ref[...] = dx
        dxb_ref[...] = dx.astype(bf16)
        e = d * d
        part = e[:, 0:128]
        for k in range(1, D // 128):
            part = part + e[:, k * 128:(k + 1) * 128]
        part = jnp.sum(part.reshape(tm // 8, 8, 128), axis=0)

        @pl.when(i == 0)
        def _():
            l_ref[...] = jnp.zeros_like(l_ref)

        l_ref[...] += part

        @pl.when(i == pl.num_programs(0) - 1)
        def _():
            tot = jnp.sum(l_ref[...], axis=1, keepdims=True)
            tot = jnp.sum(tot, axis=0, keepdims=True)
            l_ref[...] = jnp.broadcast_to(tot * (0.5 / D), l_ref.shape)

    row = pl.BlockSpec((tm, D), lambda i: (i, 0))
    return pl.pallas_call(
        body, name="loss_grad", grid=(S // tm,),
        in_specs=[row, row], out_specs=[row, row, pl.BlockSpec((8, 128), lambda i: (0, 0))],
        out_shape=[SDS((S, D), f32), SDS((S, D), bf16), SDS((8, 128), f32)], compiler_params=_params(),
    )(xf, tgt)


def _rms_proj(x, g, l, w_in_g, after, tm=1024):
    S, D = x.shape
    wc = w_in_g.shape[2]
    n_out = N_DEV * wc

    def body(x_ref, g_ref, w_ref, after_ref, proj_ref, h_ref):
        @pl.when(pl.program_id(1) == 0)
        def _():
            xv = x_ref[...]
            r = lax.rsqrt(jnp.mean(xv * xv, axis=-1, keepdims=True) + EPS)
            h_ref[...] = (xv * r * g_ref[...]).astype(bf16)

        proj_ref[...] = _dot(h_ref[...], w_ref[...])

    return pl.pallas_call(
        body, name="rms_proj", grid=(S // tm, N_DEV),
        in_specs=[pl.BlockSpec((tm, D), lambda i, j: (i, 0)), _layer_spec(g, l),
                  pl.BlockSpec((None, D, wc), lambda i, j: (j, 0, 0)), pl.BlockSpec(memory_space=pl.ANY)],
        out_specs=[pl.BlockSpec((tm, wc), lambda i, j: (i, j)), pl.BlockSpec((tm, D), lambda i, j: (i, 0))],
        out_shape=[SDS((S, n_out), f32), SDS((S, D), bf16)], compiler_params=_params(),
    )(x, g, w_in_g, after)


def _out_proj(x, y, w_out_g, after, tm=512):
    S, D = x.shape
    rb = w_out_g.shape[1]

    def body(x_ref, y_ref, w_ref, after_ref, o_ref):
        w = w_ref[...].reshape(N_DEV * rb, D)
        o_ref[...] = x_ref[...] + _dot(y_ref[...], w)

    row = pl.BlockSpec((tm, D), lambda i: (i, 0))
    return pl.pallas_call(
        body, name="out_proj", grid=(S // tm,),
        in_specs=[row, row, pl.BlockSpec((N_DEV, rb, D), lambda i: (0, 0, 0)), pl.BlockSpec(memory_space=pl.ANY)],
        out_specs=row, out_shape=SDS((S, D), f32), compiler_params=_params(),
    )(x, y, w_out_g, after)


def _out_bwd_dy(dxb, w_out_g, tm=512):
    S, D = dxb.shape
    rb = w_out_g.shape[1]

    nb = 2

    def body(dx_ref, w_ref, o_ref):
        o_ref[...] = _dot(dx_ref[...], w_ref[...].reshape(nb * rb, D), NT)

    return pl.pallas_call(
        body, name="out_bwd_dy", grid=(S // tm, N_DEV // nb),
        in_specs=[pl.BlockSpec((tm, D), lambda i, j: (i, 0)),
                  pl.BlockSpec((nb, rb, D), lambda i, j: (j, 0, 0))],
        out_specs=pl.BlockSpec((tm, nb * rb), lambda i, j: (i, j)),
        out_shape=SDS((S, D), f32), compiler_params=_params(),
    )(dxb, w_out_g)


def _tn_grad(name, a, b, tm, tn, rows_major, after):
    S, M = a.shape
    N = b.shape[1]
    if rows_major:
        rb = M // N_DEV
        nb = tm // rb
        out_shape = SDS((N_DEV, rb, N), bf16)
        out_spec = pl.BlockSpec((nb, rb, tn), lambda i, j: (i, 0, j))
    else:
        out_shape = SDS((N_DEV, M, N // N_DEV), bf16)
        assert tn == N // N_DEV
        out_spec = pl.BlockSpec((None, tm, tn), lambda i, j: (j, i, 0))

    def body(a_ref, b_ref, after_ref, o_ref):
        o_ref[...] = _dot(a_ref[...], b_ref[...], TN).astype(bf16).reshape(o_ref.shape)

    return pl.pallas_call(
        body, name=name, grid=(M // tm, N // tn),
        in_specs=[pl.BlockSpec((S, tm), lambda i, j: (0, i)), pl.BlockSpec((S, tn), lambda i, j: (0, j)),
                  pl.BlockSpec(memory_space=pl.ANY)],
        out_specs=out_spec, out_shape=out_shape, compiler_params=_params(),
    )(a, b, after)


def _in_bwd_dh(dproj, w_in_g, after, tm=1024, tn=256):
    S = dproj.shape[0]
    _, D, wc = w_in_g.shape
    tm = min(tm, S)

    def body(dp_ref, w_ref, after_ref, o_ref):
        acc = _dot(dp_ref[:, 0:wc], w_ref[0], NT)
        for k in range(1, N_DEV):
            acc = acc + _dot(dp_ref[:, k * wc:(k + 1) * wc], w_ref[k], NT)
        o_ref[...] = acc

    return pl.pallas_call(
        body, name="in_bwd_dh", grid=(S // tm, D // tn),
        in_specs=[pl.BlockSpec((tm, N_DEV * wc), lambda i, j: (i, 0)),
                  pl.BlockSpec((N_DEV, tn, wc), lambda i, j: (0, j, 0)), pl.BlockSpec(memory_space=pl.ANY)],
        out_specs=pl.BlockSpec((tm, tn), lambda i, j: (i, j)),
        out_shape=SDS((S, D), f32), compiler_params=_params(),
    )(dproj, w_in_g, after)


def _rms_bwd(dh, x, g, l, dx_next, after, tm=256):
    S, D = x.shape

    def body(dh_ref, x_ref, g_ref, dxn_ref, after_ref, dx_ref, dxb_ref, dg_ref):
        @pl.when(pl.program_id(0) == 0)
        def _():
            dg_ref[...] = jnp.zeros_like(dg_ref)

        dh = dh_ref[...]
        xv = x_ref[...]
        r = lax.rsqrt(jnp.mean(xv * xv, axis=-1, keepdims=True) + EPS)
        xhat = xv * r
        dxhat = dh * g_ref[...]
        dx = r * (dxhat - xhat * jnp.mean(dxhat * xhat, axis=-1, keepdims=True)) + dxn_ref[...]
        dx_ref[...] = dx
        dxb_ref[...] = dx.astype(bf16)
        dg_ref[...] += jnp.sum(dh * xhat, axis=0, keepdims=True)

    row = pl.BlockSpec((tm, D), lambda i: (i, 0))
    vec = pl.BlockSpec((1, D), lambda i: (0, 0))
    return pl.pallas_call(
        body, name="rms_bwd", grid=(S // tm,),
        in_specs=[row, row, _layer_spec(g, l), row, pl.BlockSpec(memory_space=pl.ANY)], out_specs=[row, row, vec],
        out_shape=[SDS((S, D), f32), SDS((S, D), bf16), SDS((1, D), f32)], compiler_params=_params(),
    )(dh, x, g, dx_next, after)


def _sgu_fwd(proj, ln_g, ln_b, w_s, b_t, l, after):
    S = proj.shape[0]
    da = A_GROUPS * HEAD_DIM
    D = 2 * da

    def body(u_ref, v_ref, z_ref, lg_ref, lb_ref, w_ref, bt_ref, after_ref, y_ref):
        u = _gelu(u_ref[...])
        v = _gelu(v_ref[...])
        z = z_ref[...]
        mu = jnp.mean(v, axis=-1, keepdims=True)
        xc = v - mu
        rs = lax.rsqrt(jnp.mean(xc * xc, axis=-1, keepdims=True) + EPS)
        vn = (xc * rs * lg_ref[...] + lb_ref[...]).astype(bf16)
        gate = u * (z * jax.nn.sigmoid(z))
        tri = lax.broadcasted_iota(jnp.int32, (CHUNK, CHUNK), 0) >= lax.broadcasted_iota(jnp.int32, (CHUNK, CHUNK), 1)
        for g in range(A_GROUPS):
            sl = slice(g * HEAD_DIM, (g + 1) * HEAD_DIM)
            wm = jnp.where(tri, w_ref[g], 0.0).astype(bf16)
            for ck in range(SGU_CHUNKS):
                rw = slice(ck * CHUNK, (ck + 1) * CHUNK)
                mixed = _dot(wm, vn[rw, sl]) + bt_ref[:, g:g + 1]
                y_ref[rw, sl] = (gate[rw, sl] * mixed).astype(bf16)

    blk = lambda cb: pl.BlockSpec((SGU_CHUNKS * CHUNK, da), lambda c: (c, cb))
    full = lambda shp: pl.BlockSpec(shp, lambda c: (0,) * len(shp))
    return pl.pallas_call(
        body, name="sgu_fwd", grid=(S // (SGU_CHUNKS * CHUNK),),
        in_specs=[blk(0), blk(1), blk(2), _layer_spec(ln_g, l), _layer_spec(ln_b, l), _layer_spec(w_s, l),
                  _layer_spec(b_t, l), pl.BlockSpec(memory_space=pl.ANY)],
        out_specs=blk(0), out_shape=SDS((S, D), bf16), compiler_params=_params(),
    )(proj, proj, proj, ln_g, ln_b, w_s, b_t, after)


def _sgu_bwd(proj, dy, ln_g, ln_b, w_s, b_t, l, after):
    S = proj.shape[0]
    da = A_GROUPS * HEAD_DIM
    n_proj = proj.shape[1]

    def body(u_ref, v_ref, z_ref, dy_ref, lg_ref, lb_ref, w_ref, bt_ref, after_ref,
             dp_ref, dw_ref, db_ref, dlg_ref, dlb_ref, dvn_ref):
        @pl.when(pl.program_id(0) == 0)
        def _():
            dw_ref[...] = jnp.zeros_like(dw_ref)
            db_ref[...] = jnp.zeros_like(db_ref)
            dlg_ref[...] = jnp.zeros_like(dlg_ref)
            dlb_ref[...] = jnp.zeros_like(dlb_ref)

        up, vp, z, dy = u_ref[...], v_ref[...], z_ref[...], dy_ref[...]
        u, gu = _gelu_and_grad(up)
        v, gv = _gelu_and_grad(vp)
        s, gs = _silu_and_grad(z)
        mu = jnp.mean(v, axis=-1, keepdims=True)
        xc = v - mu
        rs = lax.rsqrt(jnp.mean(xc * xc, axis=-1, keepdims=True) + EPS)
        vhat = xc * rs
        lg = lg_ref[...]
        vn = (vhat * lg + lb_ref[...]).astype(bf16)
        tri = lax.broadcasted_iota(jnp.int32, (CHUNK, CHUNK), 0) >= lax.broadcasted_iota(jnp.int32, (CHUNK, CHUNK), 1)
        lane = lax.broadcasted_iota(jnp.int32, (CHUNK, HEAD_DIM), 1)
        dys = dy * s
        db = jnp.zeros((CHUNK, HEAD_DIM), f32)
        for g in range(A_GROUPS):
            sl = slice(g * HEAD_DIM, (g + 1) * HEAD_DIM)
            zl = slice(2 * da + g * HEAD_DIM, 2 * da + (g + 1) * HEAD_DIM)
            wm = jnp.where(tri, w_ref[g], 0.0).astype(bf16)
            for ck in range(SGU_CHUNKS):
                rw = slice(ck * CHUNK, (ck + 1) * CHUNK)
                mixed = _dot(wm, vn[rw, sl]) + bt_ref[:, g:g + 1]
                dmix = dys[rw, sl] * u[rw, sl]
                dp_ref[rw, sl] = (dys[rw, sl] * mixed * gu[rw, sl]).astype(bf16)
                dp_ref[rw, zl] = (dy[rw, sl] * u[rw, sl] * mixed * gs[rw, sl]).astype(bf16)
                dmb = dmix.astype(bf16)
                dw_ref[g] += jnp.where(tri, _dot(dmb, vn[rw, sl], NT), 0.0)
                dvn_ref[rw, sl] = _dot(wm, dmb, TN)
                db = db + jnp.where(lane == g, jnp.sum(dmix, axis=1, keepdims=True), 0.0)
        db_ref[...] += db
        dvn = dvn_ref[...]
        dlg_ref[...] += jnp.sum(dvn * vhat, axis=0, keepdims=True)
        dlb_ref[...] += jnp.sum(dvn, axis=0, keepdims=True)
        dvhat = dvn * lg
        dv = rs * (dvhat - jnp.mean(dvhat, axis=-1, keepdims=True)
                   - vhat * jnp.mean(dvhat * vhat, axis=-1, keepdims=True))
        dp_ref[:, da:2 * da] = (dv * gv).astype(bf16)

    rows = SGU_CHUNKS * CHUNK
    blk = lambda cb: pl.BlockSpec((rows, da), lambda c: (c, cb))
    full = lambda shp: pl.BlockSpec(shp, lambda c: (0,) * len(shp))
    return pl.pallas_call(
        body, name="sgu_bwd", grid=(S // rows,),
        in_specs=[blk(0), blk(1), blk(2), blk(0), _layer_spec(ln_g, l), _layer_spec(ln_b, l), _layer_spec(w_s, l),
                  _layer_spec(b_t, l), pl.BlockSpec(memory_space=pl.ANY)],
        out_specs=[pl.BlockSpec((rows, 3 * da), lambda c: (c, 0)), full((A_GROUPS, CHUNK, CHUNK)),
                   full((CHUNK, HEAD_DIM)), full((1, da)), full((1, da))],
        out_shape=[SDS((S, n_proj), bf16), SDS((A_GROUPS, CHUNK, CHUNK), f32), SDS((CHUNK, HEAD_DIM), f32),
                   SDS((1, da), f32), SDS((1, da), f32)],
        scratch_shapes=[pltpu.VMEM((rows, da), f32)], compiler_params=_params(),
    )(proj, proj, proj, dy, ln_g, ln_b, w_s, b_t, after)


def _sb_scores(q, kblk, kb, rows, cols, masked):
    z = _dot(q, kblk, NT) * (1.0 / math.sqrt(HEAD_DIM))
    t = jnp.log(1.0 + jnp.exp(-jnp.abs(z)))
    log_1mb = -(jnp.maximum(z, 0.0) + t)
    log_beta = jnp.minimum(z, 0.0) - t
    if not masked:
        return None, log_beta, log_1mb
    causal = (cols + kb * TK) < rows
    return causal, log_beta, jnp.where(causal, log_1mb, 0.0)


def _sb_tiles(i):
    rows = i * TQ + lax.broadcasted_iota(jnp.int32, (TQ, TK), 0)
    cols = lax.broadcasted_iota(jnp.int32, (TQ, TK), 1)
    r_i = lax.broadcasted_iota(jnp.int32, (TK, TK), 0)
    c_i = lax.broadcasted_iota(jnp.int32, (TK, TK), 1)
    upper, lower = (r_i > c_i).astype(bf16), (r_i < c_i).astype(bf16)
    slot = lax.broadcasted_iota(jnp.int32, (TQ, CARRY_LANES), 1)
    return rows, cols, slot, jnp.concatenate([upper, upper], axis=0), jnp.concatenate([lower, lower], axis=0)


def _suffix_sum(t, tri):
    hi = lax.bitcast_convert_type(lax.bitcast_convert_type(t, jnp.uint32) & jnp.uint32(0xFFFF0000), f32)
    both = jnp.concatenate([hi.astype(bf16), (t - hi).astype(bf16)], axis=1)
    return _dot(both, tri)


def _sb_fwd(proj, y_prev, col0, after):
    S = proj.shape[0]
    D = y_prev.shape[1]
    dh = N_HEADS * HEAD_DIM
    n_diag = TQ // TK

    def body(q_ref, k_ref, v_ref, z_ref, yp_ref, after_ref, y_ref, o_ref, car_ref, qb, kb_s, vb_s, c_ref):
        i = pl.program_id(0)

        @pl.when(i == 0)
        def _():
            kb_s[...] = k_ref[...].astype(bf16)
            vb_s[...] = v_ref[...].astype(bf16)

        qb[...] = q_ref[...].astype(bf16)
        o_ref[...] = jnp.zeros_like(o_ref)
        c_ref[...] = jnp.zeros_like(c_ref)
        car_ref[...] = jnp.zeros_like(car_ref)
        nkb = (i + 1) * n_diag
        rows, cols, slot, upper, _ = _sb_tiles(i)

        def make_step(masked):
            def step(jj, carry):
                kb = nkb - 1 - jj
                off = pl.multiple_of(kb * TK, TK)
                hs = range(N_HEADS)
                sls = [slice(h * HEAD_DIM, (h + 1) * HEAD_DIM) for h in hs]
                sc = [_sb_scores(qb[:, sls[h]], kb_s[pl.ds(off, TK), sls[h]], kb, rows, cols, masked) for h in hs]
                suf = [_suffix_sum(sc[h][2], upper) for h in hs]
                cs = [c_ref[h] for h in hs]
                es = [jnp.exp(sc[h][1] + suf[h] + cs[h][:, :1]) for h in hs]
                if masked:
                    es = [jnp.where(sc[h][0], es[h], 0.0) for h in hs]
                pv = [_dot(es[h].astype(bf16), vb_s[pl.ds(off, TK), sls[h]]) for h in hs]
                for h in hs:
                    o_ref[:, sls[h]] += pv[h]
                    car_ref[h] = jnp.where(slot == kb, cs[h], car_ref[h])
                    c_ref[h] = cs[h] + jnp.sum(sc[h][2], axis=1, keepdims=True)
                return carry
            return step

        lax.fori_loop(0, n_diag, make_step(True), 0)
        lax.fori_loop(n_diag, nkb, make_step(False), 0)
        z = z_ref[...]
        y_ref[...] = (o_ref[...] * (z * jax.nn.sigmoid(z))).astype(bf16)

    cb = col0 * HEAD_DIM // dh
    qspec = lambda k: pl.BlockSpec((TQ, dh), lambda i: (i, cb + k))
    kspec = lambda k: pl.BlockSpec((S, dh), lambda i: (0, cb + k))
    return pl.pallas_call(
        body, name="sb_fwd", grid=(S // TQ,),
        in_specs=[qspec(0), kspec(1), kspec(2), qspec(3), pl.BlockSpec(memory_space=pl.ANY),
                  pl.BlockSpec(memory_space=pl.ANY)],
        out_specs=[pl.BlockSpec((TQ, dh), lambda i: (i, A_GROUPS * HEAD_DIM // dh)),
                   pl.BlockSpec((TQ, dh), lambda i: (i, 0)),
                   pl.BlockSpec((N_HEADS, TQ, CARRY_LANES), lambda i: (0, i, 0))],
        out_shape=[SDS((S, D), bf16), SDS((S, dh), f32), SDS((N_HEADS, S, CARRY_LANES), f32)],
        input_output_aliases={4: 0},
        scratch_shapes=[pltpu.VMEM((TQ, dh), bf16), pltpu.VMEM((S, dh), bf16), pltpu.VMEM((S, dh), bf16),
                        pltpu.VMEM((N_HEADS, TQ, CARRY_LANES), f32)],
        compiler_params=_params(),
    )(proj, proj, proj, proj, y_prev, after)


def _sb_bwd(proj, o, car, dy, dproj_prev, col0, after):
    S = proj.shape[0]
    n_i = S // TQ
    dh = N_HEADS * HEAD_DIM
    n_diag = TQ // TK
    cb = col0 * HEAD_DIM // dh
    scale = 1.0 / math.sqrt(HEAD_DIM)

    def body(q_ref, k_ref, v_ref, z_ref, o_ref, car_ref, dy_ref, dpp_ref, after_ref,
             dp_ref, qb, kb_s, vb_s, dob, p_ref, dq_acc, dk_acc, dv_acc, st_a, st_b, st_k, st_v, tile_sems):
        i = pl.program_id(0)

        def put(stage_ref, row0, nrows, k):
            pltpu.sync_copy(stage_ref, dp_ref.at[pl.ds(row0, nrows), pl.ds((cb + k) * dh, dh)])

        def tile_copies(step):
            rows = pl.ds(pl.multiple_of(step * TQ, TQ), TQ)
            return [pltpu.make_async_copy(st, dp_ref.at[rows, pl.ds((cb + k) * dh, dh)], tile_sems.at[n])
                    for n, (st, k) in enumerate(((st_a, 0), (st_b, 3)))]

        @pl.when(i > 0)
        def _():
            for cp in tile_copies(i - 1):
                cp.wait()

        @pl.when(i == 0)
        def _():
            kb_s[...] = k_ref[...].astype(bf16)
            vb_s[...] = v_ref[...].astype(bf16)
            dk_acc[...] = jnp.zeros_like(dk_acc)
            dv_acc[...] = jnp.zeros_like(dv_acc)

        s, gs = _silu_and_grad(z_ref[...])
        dy = dy_ref[...]
        st_b[...] = (dy * o_ref[...] * gs).astype(bf16)
        dob[...] = (dy * s).astype(bf16)
        qb[...] = q_ref[...].astype(bf16)
        p_ref[...] = jnp.zeros_like(p_ref)
        dq_acc[...] = jnp.zeros_like(dq_acc)
        nkb = (i + 1) * n_diag
        rows, cols, slot, upper, lower = _sb_tiles(i)

        def make_step(masked):
            def step(kb, carry):
                off = pl.multiple_of(kb * TK, TK)
                hs = range(N_HEADS)
                sls = [slice(h * HEAD_DIM, (h + 1) * HEAD_DIM) for h in hs]
                qs = [qb[:, sls[h]] for h in hs]
                ks = [kb_s[pl.ds(off, TK), sls[h]] for h in hs]
                dos = [dob[:, sls[h]] for h in hs]
                sc = [_sb_scores(qs[h], ks[h], kb, rows, cols, masked) for h in hs]
                da = [_dot(dos[h], vb_s[pl.ds(off, TK), sls[h]], NT) for h in hs]
                suf = [_suffix_sum(sc[h][2], upper) for h in hs]
                onehot = slot == kb
                cs = [jnp.sum(jnp.where(onehot, car_ref[h], 0.0), axis=1, keepdims=True) for h in hs]
                es = [jnp.exp(sc[h][1] + suf[h] + cs[h]) for h in hs]
                if masked:
                    es = [jnp.where(sc[h][0], es[h], 0.0) for h in hs]
                gs_ = [da[h] * es[h] for h in hs]
                ps = [p_ref[h] for h in hs]
                pre = [_suffix_sum(gs_[h], lower) + ps[h][:, :1] for h in hs]
                dzs = []
                for h in hs:
                    beta = jnp.exp(sc[h][1])
                    dzz = gs_[h] * (1.0 - beta) - beta * pre[h]
                    if masked:
                        dzz = jnp.where(sc[h][0], dzz, 0.0)
                    dzs.append((dzz * scale).astype(bf16))
                dqs = [_dot(dzs[h], ks[h]) for h in hs]
                dks = [_dot(dzs[h], qs[h], TN) for h in hs]
                dvs = [_dot(es[h].astype(bf16), dos[h], TN) for h in hs]
                for h in hs:
                    dq_acc[:, sls[h]] += dqs[h]
                    dk_acc[pl.ds(off, TK), sls[h]] += dks[h]
                    dv_acc[pl.ds(off, TK), sls[h]] += dvs[h]
                    p_ref[h] = ps[h] + jnp.sum(gs_[h], axis=1, keepdims=True)
                return carry
            return step

        lax.fori_loop(0, nkb - n_diag, make_step(False), 0)
        lax.fori_loop(nkb - n_diag, nkb, make_step(True), 0)
        st_a[...] = dq_acc[...].astype(bf16)
        for cp in tile_copies(i):
            cp.start()

        @pl.when(i == n_i - 1)
        def _():
            st_k[...] = dk_acc[...].astype(bf16)
            st_v[...] = dv_acc[...].astype(bf16)
            put(st_k, 0, S, 1)
            put(st_v, 0, S, 2)
            for cp in tile_copies(i):
                cp.wait()

    qspec = lambda k: pl.BlockSpec((TQ, dh), lambda i: (i, cb + k))
    kspec = lambda k: pl.BlockSpec((S, dh), lambda i: (0, cb + k))
    return pl.pallas_call(
        body, name="sb_bwd", grid=(n_i,),
        in_specs=[qspec(0), kspec(1), kspec(2), qspec(3),
                  pl.BlockSpec((TQ, dh), lambda i: (i, 0)),
                  pl.BlockSpec((N_HEADS, TQ, CARRY_LANES), lambda i: (0, i, 0)),
                  pl.BlockSpec((TQ, dh), lambda i: (i, A_GROUPS * HEAD_DIM // dh)),
                  pl.BlockSpec(memory_space=pl.ANY), pl.BlockSpec(memory_space=pl.ANY)],
        out_specs=pl.BlockSpec(memory_space=pl.ANY),
        out_shape=SDS(dproj_prev.shape, bf16),
        input_output_aliases={7: 0},
        scratch_shapes=[pltpu.VMEM((TQ, dh), bf16), pltpu.VMEM((S, dh), bf16), pltpu.VMEM((S, dh), bf16),
                        pltpu.VMEM((TQ, dh), bf16), pltpu.VMEM((N_HEADS, TQ, CARRY_LANES), f32), pltpu.VMEM((TQ, dh), f32),
                        pltpu.VMEM((S, dh), f32), pltpu.VMEM((S, dh), f32),
                        pltpu.VMEM((TQ, dh), bf16), pltpu.VMEM((TQ, dh), bf16),
                        pltpu.VMEM((S, dh), bf16), pltpu.VMEM((S, dh), bf16), pltpu.SemaphoreType.DMA((2,))],
        compiler_params=_params(56),
    )(proj, proj, proj, proj, o, car, dy, dproj_prev, after)


def _mem_kv(mem, mg, l, w_kv_g):
    M, D = mem.shape
    rb, n = w_kv_g.shape[1], w_kv_g.shape[2]

    def body(m_ref, g_ref, w_ref, kv_ref):
        mv = m_ref[...]
        r = lax.rsqrt(jnp.mean(mv * mv, axis=-1, keepdims=True) + EPS)
        mh = (mv * r * g_ref[...]).astype(bf16)
        kv_ref[...] = _dot(mh, w_ref[...].reshape(N_DEV * rb, n))

    return pl.pallas_call(
        body, name="mem_kv", grid=(1,),
        in_specs=[pl.BlockSpec((M, D), lambda i: (0, 0)), _layer_spec(mg, l),
                  pl.BlockSpec((N_DEV, rb, n), lambda i: (0, 0, 0))],
        out_specs=pl.BlockSpec((M, n), lambda i: (0, 0)),
        out_shape=SDS((M, n), f32), compiler_params=_params(),
    )(mem, mg, w_kv_g)


def _xattn_head(q_ref, kv_ref, qg, kg, h):
    dc = N_HEADS * HEAD_DIM
    sl = slice(h * HEAD_DIM, (h + 1) * HEAD_DIM)
    qh = q_ref[:, sl]
    rq = lax.rsqrt(jnp.mean(qh * qh, axis=-1, keepdims=True) + EPS)
    qhat = qh * rq
    qn = (qhat * qg).astype(bf16)
    kh = kv_ref[:, sl]
    rk = lax.rsqrt(jnp.mean(kh * kh, axis=-1, keepdims=True) + EPS)
    kn = (kh * rk * kg).astype(bf16)
    vh = kv_ref[:, dc + h * HEAD_DIM:dc + (h + 1) * HEAD_DIM].astype(bf16)
    s = _dot(qn, kn, NT) * (1.0 / math.sqrt(HEAD_DIM))
    e = jnp.exp(s - jnp.max(s, axis=-1, keepdims=True))
    p = e / jnp.sum(e, axis=-1, keepdims=True)
    o = _dot(p.astype(bf16), vh)
    return sl, rq, qhat, qn, kn, vh, p, o


def _xattn_fwd(proj, kv, qg, kg, l, y_prev, col0, tq=512):
    S = proj.shape[0]
    D = y_prev.shape[1]
    dc = N_HEADS * HEAD_DIM
    M = kv.shape[0]

    def body(q_ref, z_ref, kv_ref, qg_ref, kg_ref, yp_ref, y_ref):
        for h in range(N_HEADS):
            sl, _, _, _, _, _, _, o = _xattn_head(q_ref, kv_ref, qg_ref[...], kg_ref[...], h)
            z = z_ref[:, sl]
            y_ref[:, sl] = (o * (z * jax.nn.sigmoid(z))).astype(bf16)

    full = lambda shp: pl.BlockSpec(shp, lambda i: (0,) * len(shp))
    return pl.pallas_call(
        body, name="xattn_fwd", grid=(S // tq,),
        in_specs=[pl.BlockSpec((tq, dc), lambda i: (i, col0)), pl.BlockSpec((tq, dc), lambda i: (i, col0 + 1)),
                  full((M, 2 * dc)), _layer_spec(qg, l), _layer_spec(kg, l), pl.BlockSpec(memory_space=pl.ANY)],
        out_specs=pl.BlockSpec((tq, dc), lambda i: (i, D // dc - 1)),
        out_shape=SDS((S, D), bf16), input_output_aliases={5: 0}, compiler_params=_params(),
    )(proj, proj, kv, qg, kg, y_prev)


def _xattn_bwd(proj, kv, qg, kg, l, dy, dproj_prev, col0, tq=512):
    S = proj.shape[0]
    D = dy.shape[1]
    dc = N_HEADS * HEAD_DIM
    M = kv.shape[0]

    def body(q_ref, z_ref, kv_ref, qg_ref, kg_ref, dy_ref, dpp_ref, dp_ref, dkn_ref, dv_ref, dqg_ref):
        @pl.when(pl.program_id(0) == 0)
        def _():
            dkn_ref[...] = jnp.zeros_like(dkn_ref)
            dv_ref[...] = jnp.zeros_like(dv_ref)
            dqg_ref[...] = jnp.zeros_like(dqg_ref)

        qg = qg_ref[...]
        for h in range(N_HEADS):
            sl, rq, qhat, qn, kn, vh, p, o = _xattn_head(q_ref, kv_ref, qg, kg_ref[...], h)
            s, gs = _silu_and_grad(z_ref[:, sl])
            dyh = dy_ref[:, sl]
            dp_ref[:, dc + h * HEAD_DIM:dc + (h + 1) * HEAD_DIM] = (dyh * o * gs).astype(bf16)
            dob = (dyh * s).astype(bf16)
            dpr = _dot(dob, vh, NT)
            dv_ref[:, sl] += _dot(p.astype(bf16), dob, TN)
            ds = (p * (dpr - jnp.sum(p * dpr, axis=-1, keepdims=True)) * (1.0 / math.sqrt(HEAD_DIM))).astype(bf16)
            dqn = _dot(ds, kn)
            dkn_ref[:, sl] += _dot(ds, qn, TN)
            dqg_ref[...] += jnp.sum(dqn * qhat, axis=0, keepdims=True)
            dqhat = dqn * qg
            dp_ref[:, sl] = (rq * (dqhat - qhat * jnp.mean(dqhat * qhat, axis=-1, keepdims=True))).astype(bf16)

    full = lambda shp: pl.BlockSpec(shp, lambda i: (0,) * len(shp))
    return pl.pallas_call(
        body, name="xattn_bwd", grid=(S // tq,),
        in_specs=[pl.BlockSpec((tq, dc), lambda i: (i, col0)), pl.BlockSpec((tq, dc), lambda i: (i, col0 + 1)),
                  full((M, 2 * dc)), _layer_spec(qg, l), _layer_spec(kg, l),
                  pl.BlockSpec((tq, dc), lambda i: (i, D // dc - 1)), pl.BlockSpec(memory_space=pl.ANY)],
        out_specs=[pl.BlockSpec((tq, 2 * dc), lambda i: (i, col0 // 2)), full((M, dc)), full((M, dc)),
                   full((1, HEAD_DIM))],
        out_shape=[SDS(dproj_prev.shape, bf16), SDS((M, dc), f32), SDS((M, dc), f32), SDS((1, HEAD_DIM), f32)],
        input_output_aliases={6: 0}, compiler_params=_params(),
    )(proj, proj, kv, qg, kg, dy, dproj_prev)


def _mem_bwd(mem, mg, kg, l, kv, dkn, dv, w_kv_g):
    M, D = mem.shape
    rb, n = w_kv_g.shape[1], w_kv_g.shape[2]
    dc = n // 2

    def body(m_ref, g_ref, kv_ref, dkn_ref, dv_ref, kg_ref, w_ref, dw_ref, dmg_ref, dkg_ref, dkv_ref):
        mv = m_ref[...]
        r = lax.rsqrt(jnp.mean(mv * mv, axis=-1, keepdims=True) + EPS)
        mhat = mv * r
        mh = (mhat * g_ref[...]).astype(bf16)
        kg = kg_ref[...]
        dkg = jnp.zeros((1, HEAD_DIM), f32)
        for h in range(N_HEADS):
            sl = slice(h * HEAD_DIM, (h + 1) * HEAD_DIM)
            kh = kv_ref[:, sl]
            rk = lax.rsqrt(jnp.mean(kh * kh, axis=-1, keepdims=True) + EPS)
            khat = kh * rk
            dkn_h = dkn_ref[:, sl]
            dkg = dkg + jnp.sum(dkn_h * khat, axis=0, keepdims=True)
            dkhat = dkn_h * kg
            dkv_ref[:, sl] = (rk * (dkhat - khat * jnp.mean(dkhat * khat, axis=-1, keepdims=True))).astype(bf16)
        dkv_ref[:, dc:] = dv_ref[...].astype(bf16)
        dkg_ref[...] = dkg
        dkv = dkv_ref[...]
        dw_ref[...] = _dot(mh, dkv, TN).astype(bf16).reshape(N_DEV, rb, n)
        dmh = _dot(dkv, w_ref[...].reshape(N_DEV * rb, n), NT)
        dmg_ref[...] = jnp.sum(dmh * mhat, axis=0, keepdims=True)

    full = lambda shp: pl.BlockSpec(shp, lambda i: (0,) * len(shp))
    wspec = full((N_DEV, rb, n))
    return pl.pallas_call(
        body, name="mem_bwd", grid=(1,),
        in_specs=[full((M, D)), _layer_spec(mg, l), full((M, n)), full((M, dc)), full((M, dc)), _layer_spec(kg, l), wspec],
        out_specs=[wspec, full((1, D)), full((1, HEAD_DIM))],
        out_shape=[SDS((N_DEV, rb, n), bf16), SDS((1, D), f32), SDS((1, HEAD_DIM), f32)],
        scratch_shapes=[pltpu.VMEM((M, n), bf16)], compiler_params=_params(),
    )(mem, mg, kv, dkn, dv, kg, w_kv_g)


SMALL = ("norm_g", "sgu_ln_g", "sgu_ln_b", "sgu_w", "sgu_b", "mem_norm_g", "q_norm_g", "k_norm_g")


def _small_rows(like):
    rows = [math.prod(like[n].shape) // 128 for n in SMALL]
    offs = [0]
    for r in rows:
        offs.append(offs[-1] + -(-r // 8) * 8)
    return rows, offs


def _pack_small(parts, offs):
    pieces = []
    for k, n in enumerate(SMALL):
        a = parts[n].reshape(-1, 128)
        pieces.append(jnp.pad(a, ((0, offs[k + 1] - offs[k] - a.shape[0]), (0, 0))))
    return jnp.concatenate(pieces)


def kernel(x, mem, norm_g, w_in, sgu_ln_g, sgu_ln_b, sgu_w, sgu_b, mem_norm_g, w_mem_kv, q_norm_g, k_norm_g, w_out, loss_target, m_norm_g, m_w_in, m_sgu_ln_g, m_sgu_ln_b, m_sgu_w, m_sgu_b, m_mem_norm_g, m_w_mem_kv, m_q_norm_g, m_k_norm_g, m_w_out, v_norm_g, v_w_in, v_sgu_ln_g, v_sgu_ln_b, v_sgu_w, v_sgu_b, v_mem_norm_g, v_w_mem_kv, v_q_norm_g, v_k_norm_g, v_w_out):
    L, D, wc = w_in.shape
    S = x.shape[1]
    da = D // 2
    xs = x.reshape(S, D)
    mems = mem.reshape(mem.shape[1], D)
    tgt = loss_target.reshape(S, D)
    stacked = lambda a: a.reshape(a.shape[0], 1, -1)
    ng, lng, lnb, mg, qg, kg = map(stacked, (norm_g, sgu_ln_g, sgu_ln_b, mem_norm_g, q_norm_g, k_norm_g))
    b_t = jnp.swapaxes(sgu_b, 1, 2)
    sb_col, xa_col = 3 * da // HEAD_DIM, (3 * da + D) // (D // 4)

    ax, ay, ac = lax.axis_index("x"), lax.axis_index("y"), lax.axis_index("c")
    ids = jnp.stack([4 * ax + 2 * ay + ac, 2 * ax + ay, ac]).astype(jnp.int32)
    w_in0_b = _cast_into_slot("cast_w_in", w_in, 0, 512, ids, ids)
    first = _gather3_start("gather_w_in0", [w_in0_b], ids)
    late = first["token"]
    w_b = [(w_in0_b if l == 0 else _cast_into_slot("cast_w_in", w_in, l, 512, ids, late),
            _cast_into_slot("cast_w_kv", w_mem_kv, l, 256, ids, late),
            _cast_into_slot("cast_w_out", w_out, l, 256, ids, late)) for l in range(L)]
    relay = _gather3_relay("gather_w_in0_relay",
                           _split_wait(first, *[a for wl in w_b for a in wl if a is not w_in0_b]), ids)
    in_fwd = _gather2_forward("gather_w_in0_forward", _split_wait(relay, relay["token"]), ids)

    acts = []
    xl = xs
    for l in range(L):
        (w_in_g,) = _split_wait(in_fwd, xl if l else in_fwd["token"])
        rest = _gather3_start(f"gather_w_rest{l}", [w_b[l][1], w_b[l][2]], w_in_g)
        order = rest["token"]
        if l + 1 < L:
            nxt = _gather3_start(f"gather_w_in{l + 1}", [w_b[l + 1][0]], order)
            order = nxt["token"]
        proj, h = _rms_proj(xl, ng, l, w_in_g, order)
        rest_relay = _gather3_relay(f"gather_w_rest{l}_relay", _split_wait(rest, proj), proj)
        y = _sgu_fwd(proj, lng, lnb, sgu_w, b_t, l, rest_relay["token"])
        rest_fwd = _gather2_forward(f"gather_w_rest{l}_forward", _split_wait(rest_relay, y), y)
        order = rest_fwd["token"]
        if l + 1 < L:
            nxt_relay = _gather3_relay(f"gather_w_in{l + 1}_relay", _split_wait(nxt, order), order)
            order = nxt_relay["token"]
        y, o_b, car = _sb_fwd(proj, y, sb_col, order)
        w_kv_g, w_out_g = _split_wait(rest_fwd, o_b)
        order = o_b
        if l + 1 < L:
            in_fwd = _gather2_forward(f"gather_w_in{l + 1}_forward", _split_wait(nxt_relay, o_b), o_b)
            order = in_fwd["token"]
        kv = _mem_kv(mems, mg, l, w_kv_g)
        y = _xattn_fwd(proj, kv, qg, kg, l, y, xa_col)
        x_next = _out_proj(xl, y, w_out_g, order)
        acts.append((xl, proj, h, y, o_b, car, kv, w_in_g, w_kv_g, w_out_g))
        xl = x_next

    dx, dxb, loss_part = _loss_and_grad(xl, tgt, 512)
    loss = lax.psum(loss_part[0, 0], ("x", "y", "c"))

    weights = dict(norm_g=norm_g, sgu_ln_g=sgu_ln_g, sgu_ln_b=sgu_ln_b, sgu_w=sgu_w, sgu_b=sgu_b,
                   mem_norm_g=mem_norm_g, q_norm_g=q_norm_g, k_norm_g=k_norm_g)
    moms_m = dict(norm_g=m_norm_g, sgu_ln_g=m_sgu_ln_g, sgu_ln_b=m_sgu_ln_b, sgu_w=m_sgu_w, sgu_b=m_sgu_b,
                  mem_norm_g=m_mem_norm_g, q_norm_g=m_q_norm_g, k_norm_g=m_k_norm_g)
    moms_v = dict(norm_g=v_norm_g, sgu_ln_g=v_sgu_ln_g, sgu_ln_b=v_sgu_ln_b, sgu_w=v_sgu_w, sgu_b=v_sgu_b,
                  mem_norm_g=v_mem_norm_g, q_norm_g=v_q_norm_g, k_norm_g=v_k_norm_g)
    small_rows, small_offs = _small_rows(weights)
    head_rows = D // 128
    assert SMALL[0] == "norm_g" and head_rows % 8 == 0

    seconds = {}
    pending = None
    adam = {"w_out": None, "w_mem_kv": None, "w_in": None}

    late_in = {}

    def update(lu, order):
        (r_out,) = _scatter_finish(seconds[f"g_out{lu}"], order)
        adam["w_out"] = _sum_adam("adam_w_out", r_out, w_out, m_w_out, v_w_out, lu, adam["w_out"], 128, ids, ids)
        r_kv, late_in[lu] = _scatter_finish(seconds[f"g_rest{lu}"], adam["w_out"][0])
        adam["w_mem_kv"] = _sum_adam("adam_w_kv", r_kv, w_mem_kv, m_w_mem_kv, v_w_mem_kv, lu, adam["w_mem_kv"], 256, ids, ids)
        return adam["w_mem_kv"][0]

    def update_in(lu, order):
        adam["w_in"] = _sum_adam("adam_w_in", late_in[lu], w_in, m_w_in, v_w_in, lu, adam["w_in"], 256, ids, order)
        return adam["w_in"][0]
    small = {n: [None] * L for n in SMALL}
    for l in reversed(range(L)):
        xl, proj, h, y, o_b, car, kv, w_in_g, w_kv_g, w_out_g = acts[l]
        dy = _out_bwd_dy(dxb, w_out_g)
        order = dy
        if pending is not None:
            seconds[pending[0]] = _scatter2_second_level(pending[0], pending[1], dy, ids)
            order = seconds[pending[0]]["token"]
        g_out = _tn_grad("out_bwd_dw", y, dxb, 512, 512, True, order)
        seconds[f"g_out{l}"] = _scatter1_start(f"scatter_g_out{l}", [g_out], ids)
        dproj, d_sw, d_sb, d_lg, d_lb = _sgu_bwd(proj, dy, lng, lnb, sgu_w, b_t, l, seconds[f"g_out{l}"]["token"])
        dproj = _sb_bwd(proj, o_b, car, dy, dproj, sb_col, d_lb)
        dproj, dkn, dv, d_qg = _xattn_bwd(proj, kv, qg, kg, l, dy, dproj, xa_col)
        g_kv, d_mg, d_kg = _mem_bwd(mems, mg, kg, l, kv, dkn, dv, w_kv_g)
        for n, val in (("sgu_ln_g", d_lg), ("sgu_ln_b", d_lb), ("sgu_w", d_sw), ("sgu_b", d_sb[:, :A_GROUPS].T),
                       ("mem_norm_g", d_mg), ("q_norm_g", d_qg), ("k_norm_g", d_kg)):
            small[n][l] = val.reshape(-1)
        order = d_kg
        if l == 0:
            small["norm_g"][0] = jnp.zeros_like(small["norm_g"][1])
            part = _pack_small({n: jnp.stack(small[n]) for n in SMALL}, small_offs)
            tail = _gather1_start("gather_small_tail", [_into_slot("small_tail_slot", part[head_rows:], ids)], ids)
            order = tail["token"]
        g_in_l = _tn_grad("in_bwd_dw", h, dproj, D, wc, False, order)
        first = _scatter2_pair_start(f"scatter_g_rest{l}_pair", [g_kv, g_in_l], ids)
        order = first["token"]
        pending = (f"g_rest{l}", first)
        if l == 0:
            for lu in reversed(range(1, L)):
                order = update(lu, order)
            seconds[pending[0]] = _scatter2_second_level(pending[0], pending[1], order, ids)
            order = seconds[pending[0]]["token"]
            for lu in reversed(range(1, L)):
                order = update_in(lu, order)
        dh = _in_bwd_dh(dproj, w_in_g, order)
        dx, dxb, d_ng = _rms_bwd(dh, xl, ng, l, dx, order)
        small["norm_g"][l] = d_ng.reshape(-1)
    head = _gather1_start("gather_small_head",
                          [_into_slot("small_head_slot", small["norm_g"][0].reshape(head_rows, 128), ids)], ids)

    (r_out,) = _scatter_finish(seconds["g_out0"], head["token"])
    adam["w_out"] = _sum_adam("adam_w_out", r_out, w_out, m_w_out, v_w_out, 0, adam["w_out"], 128, ids, ids)
    r_kv, r_in = _scatter_finish(seconds["g_rest0"], adam["w_out"][0])
    adam["w_mem_kv"] = _sum_adam("adam_w_kv", r_kv, w_mem_kv, m_w_mem_kv, v_w_mem_kv, 0, adam["w_mem_kv"], 256, ids, ids)
    adam["w_in"] = _sum_adam("adam_w_in", r_in, w_in, m_w_in, v_w_in, 0, adam["w_in"], 256, ids, ids)
    (r_tail,) = _split_wait(tail, adam["w_in"][0])
    (r_head,) = _split_wait(head, r_tail)
    as128 = lambda d: [d[n].reshape(-1, 128) for n in SMALL]
    sm = _small_sum_adam(r_head, r_tail, as128(weights), as128(moms_m), as128(moms_v), small_offs)
    res = dict(adam)
    for p, n in enumerate(SMALL):
        res[n] = [sm[k * len(SMALL) + p].reshape(weights[n].shape) for k in range(4)]

    order = ("norm_g", "w_in", "sgu_ln_g", "sgu_ln_b", "sgu_w", "sgu_b", "mem_norm_g", "w_mem_kv", "q_norm_g",
             "k_norm_g", "w_out")
    outs = [loss, dx.reshape(x.shape)]
    for k in range(4):
        outs += [res[n][k] for n in order]
    return tuple(outs)
```

```python
import functools
import math

import jax
import jax.numpy as jnp
from jax import lax
from jax.experimental import pallas as pl
from jax.experimental.pallas import tpu as pltpu

f32 = jnp.float32
bf16 = jnp.bfloat16
SDS = jax.ShapeDtypeStruct

N_DEV = 8
EPS = 1e-6
CHUNK = 128
SGU_CHUNKS = 2
A_GROUPS = 8
HEAD_DIM = 128
N_HEADS = 4
TQ = 256
TK = 256
CARRY_LANES = 128
ADAM_LR, ADAM_B1, ADAM_B2, ADAM_EPS, ADAM_WD, ADAM_STEP = 0.001, 0.9, 0.999, 1e-08, 0.01, 10
MIB = 1024 * 1024

NT = (((1,), (1,)), ((), ()))
TN = (((0,), (0,)), ((), ()))


def _params(vmem_mib=48):
    return pltpu.CompilerParams(vmem_limit_bytes=vmem_mib * MIB)


def _gelu_and_grad(x):
    e = lax.erf(x * (1.0 / math.sqrt(2.0)))
    cdf = 0.5 * (1.0 + e)
    pdf = jnp.exp(-0.5 * x * x) * (1.0 / math.sqrt(2.0 * math.pi))
    return x * cdf, cdf + x * pdf


def _gelu(x):
    return 0.5 * x * (1.0 + lax.erf(x * (1.0 / math.sqrt(2.0))))


def _silu_and_grad(z):
    sg = jax.nn.sigmoid(z)
    return z * sg, sg * (1.0 + z * (1.0 - sg))


def _layer_spec(stacked, l):
    rest = stacked.shape[1:]
    return pl.BlockSpec((None,) + rest, lambda *idx: (l,) + (0,) * len(rest))


def _dot(a, b, dims=None):
    if dims is None:
        return jnp.dot(a, b, preferred_element_type=f32)
    return lax.dot_general(a, b, dims, preferred_element_type=f32)


_HBM = pl.BlockSpec(memory_space=pltpu.HBM)
_SEM = pl.BlockSpec(memory_space=pltpu.SEMAPHORE)
_EFFECT = pltpu.SideEffectType.DATAFLOW_SIDE_EFFECTING


def _split_start(name, bufs, n_remote, n_local, build, after):
    nb = len(bufs)

    def body(*refs):
        token = refs[-1]
        locals_, remotes = build(refs[:nb], *refs[nb + 1:nb + 4])
        for cp in locals_ + remotes:
            cp.start()
        token[...] = jnp.zeros_like(token)

    hbm = lambda a: pltpu.with_memory_space_constraint(a, pltpu.HBM)
    outs = pl.pallas_call(
        body, name=name,
        out_shape=(pltpu.SemaphoreType.DMA((n_remote,)), pltpu.SemaphoreType.DMA((n_remote,)),
                   pltpu.SemaphoreType.DMA((max(n_local, 1),)),
                   *[pltpu.HBM(b.shape, b.dtype) for b in bufs], SDS((8, 128), f32)),
        in_specs=[_HBM] * nb + [pl.BlockSpec(memory_space=pl.ANY)],
        out_specs=(_SEM, _SEM, _SEM, *[_HBM] * nb, pl.BlockSpec(memory_space=pltpu.VMEM)),
        input_output_aliases={k: 3 + k for k in range(nb)},
        compiler_params=pltpu.CompilerParams(has_side_effects=_EFFECT),
    )(*[hbm(b) for b in bufs], after)
    return dict(name=name, build=build, sems=outs[:3], bufs=outs[3:3 + nb], token=outs[-1])


def _split_wait(handle, *after):
    build, bufs = handle["build"], handle["bufs"]
    nb = len(bufs)

    def body(*refs):
        locals_, remotes = build(refs[:nb], *refs[nb:nb + 3])
        for cp in remotes:
            cp.wait_recv()
        for cp in remotes:
            cp.wait_send()
        for cp in locals_:
            cp.wait()

    outs = pl.pallas_call(
        body, name=handle["name"] + "_wait",
        out_shape=tuple(pltpu.HBM(b.shape, b.dtype) for b in bufs),
        in_specs=[_HBM] * nb + [_SEM] * 3 + [pl.BlockSpec(memory_space=pl.ANY)] * len(after),
        out_specs=tuple([_HBM] * nb),
        input_output_aliases={k: k for k in range(nb)},
        compiler_params=pltpu.CompilerParams(has_side_effects=_EFFECT),
    )(*bufs, *handle["sems"], *after)
    return list(outs)


def _remote(src, dst, send_sems, recv_sems, k, to):
    return pltpu.make_async_remote_copy(src_ref=src, dst_ref=dst, send_sem=send_sems.at[k], recv_sem=recv_sems.at[k],
                                        device_id=to, device_id_type=pl.DeviceIdType.MESH)


def _other_chips(x, y):
    return [(1 - x, y), (x, 1 - y), (1 - x, 1 - y)]


def _all_peers(x, y, c):
    return [(1 - x if m & 4 else x, 1 - y if m & 2 else y, 1 - c if m & 1 else c) for m in range(1, N_DEV)]


def _gather1_start(name, lands, after):
    def build(refs, send, recv, loc):
        x, y, c = lax.axis_index("x"), lax.axis_index("y"), lax.axis_index("c")
        me = 4 * x + 2 * y + c
        return [], [_remote(d.at[me], d.at[me], send, recv, 7 * a + k, peer)
                    for a, d in enumerate(refs) for k, peer in enumerate(_all_peers(x, y, c))]

    return _split_start(name, list(lands), 7 * len(lands), 0, build, after)


def _scatter1_start(name, srcs, after):
    n = len(srcs)

    def build(refs, send, recv, loc):
        x, y, c = lax.axis_index("x"), lax.axis_index("y"), lax.axis_index("c")
        me = 4 * x + 2 * y + c
        return [], [_remote(refs[a].at[4 * px + 2 * py + pc], refs[n + a].at[me], send, recv, 7 * a + k, (px, py, pc))
                    for a in range(n) for k, (px, py, pc) in enumerate(_all_peers(x, y, c))]

    return _split_start(name, list(srcs) + [lax.empty(s.shape, s.dtype) for s in srcs], 7 * n, 0, build, after)


def _gather2_start(name, lands, after):
    def build(refs, send, recv, loc):
        x, y, c = lax.axis_index("x"), lax.axis_index("y"), lax.axis_index("c")
        me = 4 * x + 2 * y + c
        remotes = []
        for a, d in enumerate(refs):
            remotes.append(_remote(d.at[me], d.at[me], send, recv, 4 * a, (x, y, 1 - c)))
            remotes += [_remote(d.at[me], d.at[me], send, recv, 4 * a + 1 + k, (px, py, c))
                        for k, (px, py) in enumerate(_other_chips(x, y))]
        return [], remotes

    return _split_start(name, list(lands), 4 * len(lands), 0, build, after)


def _gather3_start(name, lands, after):
    def build(refs, send, recv, loc):
        x, y, c = lax.axis_index("x"), lax.axis_index("y"), lax.axis_index("c")
        me = 4 * x + 2 * y + c
        return [], [_remote(d.at[me], d.at[me], send, recv, 3 * a + k, to)
                    for a, d in enumerate(refs) for k, to in enumerate([(x, y, 1 - c), (1 - x, y, c), (x, 1 - y, c)])]

    return _split_start(name, list(lands), 3 * len(lands), 0, build, after)


def _gather3_relay(name, lands, after):
    def build(refs, send, recv, loc):
        x, y, c = lax.axis_index("x"), lax.axis_index("y"), lax.axis_index("c")
        from_x = c == 0
        slot = 4 * jnp.where(from_x, 1 - x, x) + 2 * jnp.where(from_x, y, 1 - y) + c
        to = (jnp.where(from_x, x, 1 - x), jnp.where(from_x, 1 - y, y), c)
        return [], [_remote(d.at[slot], d.at[slot], send, recv, a, to) for a, d in enumerate(refs)]

    return _split_start(name, list(lands), len(lands), 0, build, after)


def _gather2_forward(name, lands, after):
    n = len(lands)

    def build(refs, send, recv, loc):
        x, y, c = lax.axis_index("x"), lax.axis_index("y"), lax.axis_index("c")
        slots = [4 * px + 2 * py + c for px, py in _other_chips(x, y)]
        return [], [_remote(d.at[sl], d.at[sl], send, recv, 3 * a + k, (x, y, 1 - c))
                    for a, d in enumerate(refs) for k, sl in enumerate(slots)]

    return _split_start(name, list(lands), 3 * n, 0, build, after)


def _scatter2_pair_start(name, srcs, after):
    n = len(srcs)

    def build(refs, send, recv, loc):
        x, y, c = lax.axis_index("x"), lax.axis_index("y"), lax.axis_index("c")
        return [], [_remote(refs[a].at[2 * q + 1 - c], refs[n + a].at[q], send, recv, 4 * a + q, (x, y, 1 - c))
                    for a in range(n) for q in range(4)]

    lands = [lax.empty((4,) + s.shape[1:], s.dtype) for s in srcs]
    return _split_start(name, list(srcs) + lands, 4 * n, 0, build, after)


def _scatter2_chip_start(name, pairs, after):
    n = len(pairs)

    def build(refs, send, recv, loc):
        x, y, c = lax.axis_index("x"), lax.axis_index("y"), lax.axis_index("c")
        return [], [_remote(refs[a].at[2 * px + py], refs[n + a].at[2 * x + y], send, recv, 3 * a + k, (px, py, c))
                    for a in range(n) for k, (px, py) in enumerate(_other_chips(x, y))]

    return _split_start(name, list(pairs) + [lax.empty(p.shape, p.dtype) for p in pairs], 3 * n, 0, build, after)


def _pair_sum(name, src, theirs, ids):
    _, R, C = theirs.shape
    tr = min(R, 1024)

    def body(ids_ref, a_ref, b_ref, o_ref):
        o_ref[...] = (a_ref[...].astype(f32) + b_ref[...].astype(f32)).astype(bf16)

    spec = pl.BlockSpec((None, tr, C), lambda q, i, ids: (q, i, 0))
    return pl.pallas_call(
        body, name=name,
        grid_spec=pltpu.PrefetchScalarGridSpec(
            num_scalar_prefetch=1, grid=(4, R // tr),
            in_specs=[pl.BlockSpec((None, tr, C), lambda q, i, ids: (2 * q + ids[2], i, 0)), spec], out_specs=spec),
        out_shape=SDS(theirs.shape, bf16), compiler_params=_params(),
    )(ids, src, theirs)


def _scatter2_second_level(name, first, after, ids):
    outs = _split_wait(first, after)
    n = len(outs) // 2
    pairs = [_pair_sum(f"pair_sum_{name}{a}", outs[a], outs[n + a], ids) for a in range(n)]
    return _scatter2_chip_start(f"scatter_{name}_chip", pairs, ids)


def _scatter_finish(second, after):
    outs = _split_wait(second, after)
    n = len(outs) // 2
    return [(outs[a], outs[n + a]) for a in range(n)]


def _cast_into_slot(name, w, l, tr, ids, after):
    _, R, C = w.shape

    def body(ids_ref, w_ref, after_ref, o_ref):
        o_ref[...] = w_ref[...].astype(bf16)

    return pl.pallas_call(
        body, name=name,
        grid_spec=pltpu.PrefetchScalarGridSpec(
            num_scalar_prefetch=1, grid=(R // tr,),
            in_specs=[pl.BlockSpec((None, tr, C), lambda i, ids: (l, i, 0)), pl.BlockSpec(memory_space=pl.ANY)],
            out_specs=pl.BlockSpec((None, tr, C), lambda i, ids: (ids[0], i, 0))),
        out_shape=SDS((N_DEV, R, C), bf16), compiler_params=_params(),
    )(ids, w, after)


def _into_slot(name, a, ids):
    R, C = a.shape

    def body(ids_ref, a_ref, o_ref):
        o_ref[...] = a_ref[...]

    return pl.pallas_call(
        body, name=name,
        grid_spec=pltpu.PrefetchScalarGridSpec(
            num_scalar_prefetch=1, grid=(1,),
            in_specs=[pl.BlockSpec((R, C), lambda i, ids: (0, 0))],
            out_specs=pl.BlockSpec((None, R, C), lambda i, ids: (ids[0], 0, 0))),
        out_shape=SDS((N_DEV, R, C), f32), compiler_params=_params(),
    )(ids, a)


def _adam_math(w, g, m, v):
    m2 = ADAM_B1 * m + (1.0 - ADAM_B1) * g
    v2 = ADAM_B2 * v + (1.0 - ADAM_B2) * (g * g)
    m_hat = m2 / (1.0 - ADAM_B1 ** ADAM_STEP)
    v_hat = v2 / (1.0 - ADAM_B2 ** ADAM_STEP)
    delta = -ADAM_LR * (m_hat / (jnp.sqrt(v_hat) + ADAM_EPS) + ADAM_WD * w)
    return delta, m2, v2


def _sum_adam(name, pair_recv, w, m, v, l, prev, tr, ids, after):
    own, recv = pair_recv
    L, R, C = w.shape
    slots = recv.shape[0]
    mine = 0 if slots == N_DEV else 1

    def body(ids_ref, r_ref, own_ref, w_ref, m_ref, v_ref, after_ref, *rest):
        g_ref, d_ref, m2_ref, v2_ref = rest[-4:]
        terms = [jnp.where(ids_ref[mine] == q, own_ref[...], r_ref[q]).astype(f32) for q in range(slots)]
        g = terms[0]
        for t in terms[1:]:
            g = g + t
        d, m2, v2 = _adam_math(w_ref[...], g, m_ref[...], v_ref[...])
        g_ref[...] = g
        d_ref[...] = d
        m2_ref[...] = m2
        v2_ref[...] = v2

    wspec = pl.BlockSpec((None, tr, C), lambda i, ids: (l, i, 0))
    in_specs = [pl.BlockSpec((slots, tr, C), lambda i, ids: (0, i, 0)),
                pl.BlockSpec((None, tr, C), lambda i, ids: (ids[mine], i, 0)), wspec, wspec, wspec,
                pl.BlockSpec(memory_space=pl.ANY)]
    args = [ids, recv, own, w, m, v, after]
    aliases = {}
    if prev is not None:
        in_specs += [pl.BlockSpec(memory_space=pl.ANY)] * 4
        args += list(prev)
        aliases = {7 + k: k for k in range(4)}
    return pl.pallas_call(
        body, name=name,
        grid_spec=pltpu.PrefetchScalarGridSpec(num_scalar_prefetch=1, grid=(R // tr,), in_specs=in_specs,
                                               out_specs=[wspec] * 4),
        out_shape=[SDS((L, R, C), f32)] * 4, input_output_aliases=aliases, compiler_params=_params(),
    )(*args)


def _small_sum_adam(recv_head, recv_tail, ws, ms, vs, offs):
    n = len(ws)
    r0 = recv_head.shape[1]

    def body(*refs):
        rh, rt = refs[0], refs[1]
        w_refs, m_refs, v_refs = refs[2:2 + n], refs[2 + n:2 + 2 * n], refs[2 + 2 * n:2 + 3 * n]
        outs = refs[2 + 3 * n:]
        for p in range(n):
            lo, hi = offs[p], offs[p] + ws[p].shape[0]
            pieces = []
            if lo < r0:
                pieces.append((rh, lo, 0, min(hi, r0) - lo))
            if hi > r0:
                pieces.append((rt, max(lo, r0) - r0, max(lo, r0) - lo, hi - max(lo, r0)))
            for src, a, b, cnt in pieces:
                g = src[0, a:a + cnt, :]
                for s in range(1, N_DEV):
                    g = g + src[s, a:a + cnt, :]
                d, m2, v2 = _adam_math(w_refs[p][b:b + cnt, :], g, m_refs[p][b:b + cnt, :], v_refs[p][b:b + cnt, :])
                for k, val in enumerate((g, d, m2, v2)):
                    outs[k * n + p][b:b + cnt, :] = val

    return pl.pallas_call(
        body, name="small_sum_adam", out_shape=[SDS(w.shape, f32) for w in ws] * 4, compiler_params=_params(),
    )(recv_head, recv_tail, *ws, *ms, *vs)


def _rms_proj(x, g, l, w_in_g, after, tm=1024):
    S, D = x.shape
    wc = w_in_g.shape[2]
    n_out = N_DEV * wc

    def body(x_ref, g_ref, w_ref, after_ref, proj_ref, h_ref):
        @pl.when(pl.program_id(1) == 0)
        def _():
            xv = x_ref[...]
            r = lax.rsqrt(jnp.mean(xv * xv, axis=-1, keepdims=True) + EPS)
            h_ref[...] = (xv * r * g_ref[...]).astype(bf16)

        proj_ref[...] = _dot(h_ref[...], w_ref[...])

    return pl.pallas_call(
        body, name="rms_proj", grid=(S // tm, N_DEV),
        in_specs=[pl.BlockSpec((tm, D), lambda i, j: (i, 0)), _layer_spec(g, l),
                  pl.BlockSpec((None, D, wc), lambda i, j: (j, 0, 0)), pl.BlockSpec(memory_space=pl.ANY)],
        out_specs=[pl.BlockSpec((tm, wc), lambda i, j: (i, j)), pl.BlockSpec((tm, D), lambda i, j: (i, 0))],
        out_shape=[SDS((S, n_out), f32), SDS((S, D), bf16)], compiler_params=_params(),
    )(x, g, w_in_g, after)


def _out_proj(x, y, w_out_g, after, tm=512):
    S, D = x.shape
    rb = w_out_g.shape[1]

    def body(x_ref, y_ref, w_ref, after_ref, o_ref):
        w = w_ref[...].reshape(N_DEV * rb, D)
        o_ref[...] = x_ref[...] + _dot(y_ref[...], w)

    row = pl.BlockSpec((tm, D), lambda i: (i, 0))
    return pl.pallas_call(
        body, name="out_proj", grid=(S // tm,),
        in_specs=[row, row, pl.BlockSpec((N_DEV, rb, D), lambda i: (0, 0, 0)), pl.BlockSpec(memory_space=pl.ANY)],
        out_specs=row, out_shape=SDS((S, D), f32), compiler_params=_params(),
    )(x, y, w_out_g, after)


def _out_proj_loss(x, y, w_out_g, tgt, tm=256):
    S, D = x.shape
    rb = w_out_g.shape[1]

    def body(x_ref, y_ref, w_ref, t_ref, dx_ref, dxb_ref, l_ref):
        i = pl.program_id(0)
        w = w_ref[...].reshape(N_DEV * rb, D)
        d = x_ref[...] + _dot(y_ref[...], w) - t_ref[...]
        dx = d * (1.0 / D)
        dx_ref[...] = dx
        dxb_ref[...] = dx.astype(bf16)
        e = d * d
        part = e[:, 0:128]
        for k in range(1, D // 128):
            part = part + e[:, k * 128:(k + 1) * 128]
        part = jnp.sum(part.reshape(tm // 8, 8, 128), axis=0)

        @pl.when(i == 0)
        def _():
            l_ref[...] = jnp.zeros_like(l_ref)

        l_ref[...] += part

        @pl.when(i == pl.num_programs(0) - 1)
        def _():
            tot = jnp.sum(l_ref[...], axis=1, keepdims=True)
            tot = jnp.sum(tot, axis=0, keepdims=True)
            l_ref[...] = jnp.broadcast_to(tot * (0.5 / D), l_ref.shape)

    row = pl.BlockSpec((tm, D), lambda i: (i, 0))
    return pl.pallas_call(
        body, name="out_proj_loss", grid=(S // tm,),
        in_specs=[row, row, pl.BlockSpec((N_DEV, rb, D), lambda i: (0, 0, 0)), row],
        out_specs=[row, row, pl.BlockSpec((8, 128), lambda i: (0, 0))],
        out_shape=[SDS((S, D), f32), SDS((S, D), bf16), SDS((8, 128), f32)], compiler_params=_params(),
    )(x, y, w_out_g, tgt)


def _out_bwd_dy(dxb, w_out_g, tm=512):
    S, D = dxb.shape
    rb = w_out_g.shape[1]

    nb = 2

    def body(dx_ref, w_ref, o_ref):
        o_ref[...] = _dot(dx_ref[...], w_ref[...].reshape(nb * rb, D), NT)

    return pl.pallas_call(
        body, name="out_bwd_dy", grid=(S // tm, N_DEV // nb),
        in_specs=[pl.BlockSpec((tm, D), lambda i, j: (i, 0)),
                  pl.BlockSpec((nb, rb, D), lambda i, j: (j, 0, 0))],
        out_specs=pl.BlockSpec((tm, nb * rb), lambda i, j: (i, j)),
        out_shape=SDS((S, D), f32), compiler_params=_params(),
    )(dxb, w_out_g)


def _tn_grad(name, a, b, tm, tn, rows_major, after):
    S, M = a.shape
    N = b.shape[1]
    if rows_major:
        rb = M // N_DEV
        nb = tm // rb
        out_shape = SDS((N_DEV, rb, N), bf16)
        out_spec = pl.BlockSpec((nb, rb, tn), lambda i, j: (i, 0, j))
    else:
        out_shape = SDS((N_DEV, M, N // N_DEV), bf16)
        assert tn == N // N_DEV
        out_spec = pl.BlockSpec((None, tm, tn), lambda i, j: (j, i, 0))

    def body(a_ref, b_ref, after_ref, o_ref):
        o_ref[...] = _dot(a_ref[...], b_ref[...], TN).astype(bf16).reshape(o_ref.shape)

    return pl.pallas_call(
        body, name=name, grid=(M // tm, N // tn),
        in_specs=[pl.BlockSpec((S, tm), lambda i, j: (0, i)), pl.BlockSpec((S, tn), lambda i, j: (0, j)),
                  pl.BlockSpec(memory_space=pl.ANY)],
        out_specs=out_spec, out_shape=out_shape, compiler_params=_params(),
    )(a, b, after)


def _in_bwd_dh(dproj, w_in_g, after, tm=1024, tn=256):
    S = dproj.shape[0]
    _, D, wc = w_in_g.shape
    tm = min(tm, S)

    def body(dp_ref, w_ref, after_ref, o_ref):
        acc = _dot(dp_ref[:, 0:wc], w_ref[0], NT)
        for k in range(1, N_DEV):
            acc = acc + _dot(dp_ref[:, k * wc:(k + 1) * wc], w_ref[k], NT)
        o_ref[...] = acc

    return pl.pallas_call(
        body, name="in_bwd_dh", grid=(S // tm, D // tn),
        in_specs=[pl.BlockSpec((tm, N_DEV * wc), lambda i, j: (i, 0)),
                  pl.BlockSpec((N_DEV, tn, wc), lambda i, j: (0, j, 0)), pl.BlockSpec(memory_space=pl.ANY)],
        out_specs=pl.BlockSpec((tm, tn), lambda i, j: (i, j)),
        out_shape=SDS((S, D), f32), compiler_params=_params(),
    )(dproj, w_in_g, after)


def _rms_bwd(dh, x, g, l, dx_next, after, tm=256):
    S, D = x.shape

    def body(dh_ref, x_ref, g_ref, dxn_ref, after_ref, dx_ref, dxb_ref, dg_ref):
        @pl.when(pl.program_id(0) == 0)
        def _():
            dg_ref[...] = jnp.zeros_like(dg_ref)

        dh = dh_ref[...]
        xv = x_ref[...]
        r = lax.rsqrt(jnp.mean(xv * xv, axis=-1, keepdims=True) + EPS)
        xhat = xv * r
        dxhat = dh * g_ref[...]
        dx = r * (dxhat - xhat * jnp.mean(dxhat * xhat, axis=-1, keepdims=True)) + dxn_ref[...]
        dx_ref[...] = dx
        dxb_ref[...] = dx.astype(bf16)
        dg_ref[...] += jnp.sum(dh * xhat, axis=0, keepdims=True)

    row = pl.BlockSpec((tm, D), lambda i: (i, 0))
    vec = pl.BlockSpec((1, D), lambda i: (0, 0))
    return pl.pallas_call(
        body, name="rms_bwd", grid=(S // tm,),
        in_specs=[row, row, _layer_spec(g, l), row, pl.BlockSpec(memory_space=pl.ANY)], out_specs=[row, row, vec],
        out_shape=[SDS((S, D), f32), SDS((S, D), bf16), SDS((1, D), f32)], compiler_params=_params(),
    )(dh, x, g, dx_next, after)


def _sgu_fwd(proj, ln_g, ln_b, w_s, b_t, l, after):
    S = proj.shape[0]
    da = A_GROUPS * HEAD_DIM
    D = 2 * da

    def body(u_ref, v_ref, z_ref, lg_ref, lb_ref, w_ref, bt_ref, after_ref, y_ref):
        u = _gelu(u_ref[...])
        v = _gelu(v_ref[...])
        z = z_ref[...]
        mu = jnp.mean(v, axis=-1, keepdims=True)
        xc = v - mu
        rs = lax.rsqrt(jnp.mean(xc * xc, axis=-1, keepdims=True) + EPS)
        vn = (xc * rs * lg_ref[...] + lb_ref[...]).astype(bf16)
        gate = u * (z * jax.nn.sigmoid(z))
        tri = lax.broadcasted_iota(jnp.int32, (CHUNK, CHUNK), 0) >= lax.broadcasted_iota(jnp.int32, (CHUNK, CHUNK), 1)
        for g in range(A_GROUPS):
            sl = slice(g * HEAD_DIM, (g + 1) * HEAD_DIM)
            wm = jnp.where(tri, w_ref[g], 0.0).astype(bf16)
            for ck in range(SGU_CHUNKS):
                rw = slice(ck * CHUNK, (ck + 1) * CHUNK)
                mixed = _dot(wm, vn[rw, sl]) + bt_ref[:, g:g + 1]
                y_ref[rw, sl] = (gate[rw, sl] * mixed).astype(bf16)

    blk = lambda cb: pl.BlockSpec((SGU_CHUNKS * CHUNK, da), lambda c: (c, cb))
    full = lambda shp: pl.BlockSpec(shp, lambda c: (0,) * len(shp))
    return pl.pallas_call(
        body, name="sgu_fwd", grid=(S // (SGU_CHUNKS * CHUNK),),
        in_specs=[blk(0), blk(1), blk(2), _layer_spec(ln_g, l), _layer_spec(ln_b, l), _layer_spec(w_s, l),
                  _layer_spec(b_t, l), pl.BlockSpec(memory_space=pl.ANY)],
        out_specs=blk(0), out_shape=SDS((S, D), bf16), compiler_params=_params(),
    )(proj, proj, proj, ln_g, ln_b, w_s, b_t, after)


def _sgu_bwd(proj, dy, ln_g, ln_b, w_s, b_t, l, after):
    S = proj.shape[0]
    da = A_GROUPS * HEAD_DIM
    n_proj = proj.shape[1]

    def body(u_ref, v_ref, z_ref, dy_ref, lg_ref, lb_ref, w_ref, bt_ref, after_ref,
             dp_ref, dw_ref, db_ref, dlg_ref, dlb_ref, dvn_ref):
        @pl.when(pl.program_id(0) == 0)
        def _():
            dw_ref[...] = jnp.zeros_like(dw_ref)
            db_ref[...] = jnp.zeros_like(db_ref)
            dlg_ref[...] = jnp.zeros_like(dlg_ref)
            dlb_ref[...] = jnp.zeros_like(dlb_ref)

        up, vp, z, dy = u_ref[...], v_ref[...], z_ref[...], dy_ref[...]
        u, gu = _gelu_and_grad(up)
        v, gv = _gelu_and_grad(vp)
        s, gs = _silu_and_grad(z)
        mu = jnp.mean(v, axis=-1, keepdims=True)
        xc = v - mu
        rs = lax.rsqrt(jnp.mean(xc * xc, axis=-1, keepdims=True) + EPS)
        vhat = xc * rs
        lg = lg_ref[...]
        vn = (vhat * lg + lb_ref[...]).astype(bf16)
        tri = lax.broadcasted_iota(jnp.int32, (CHUNK, CHUNK), 0) >= lax.broadcasted_iota(jnp.int32, (CHUNK, CHUNK), 1)
        lane = lax.broadcasted_iota(jnp.int32, (CHUNK, HEAD_DIM), 1)
        dys = dy * s
        db = jnp.zeros((CHUNK, HEAD_DIM), f32)
        for g in range(A_GROUPS):
            sl = slice(g * HEAD_DIM, (g + 1) * HEAD_DIM)
            zl = slice(2 * da + g * HEAD_DIM, 2 * da + (g + 1) * HEAD_DIM)
            wm = jnp.where(tri, w_ref[g], 0.0).astype(bf16)
            for ck in range(SGU_CHUNKS):
                rw = slice(ck * CHUNK, (ck + 1) * CHUNK)
                mixed = _dot(wm, vn[rw, sl]) + bt_ref[:, g:g + 1]
                dmix = dys[rw, sl] * u[rw, sl]
                dp_ref[rw, sl] = (dys[rw, sl] * mixed * gu[rw, sl]).astype(bf16)
                dp_ref[rw, zl] = (dy[rw, sl] * u[rw, sl] * mixed * gs[rw, sl]).astype(bf16)
                dmb = dmix.astype(bf16)
                dw_ref[g] += jnp.where(tri, _dot(dmb, vn[rw, sl], NT), 0.0)
                dvn_ref[rw, sl] = _dot(wm, dmb, TN)
                db = db + jnp.where(lane == g, jnp.sum(dmix, axis=1, keepdims=True), 0.0)
        db_ref[...] += db
        dvn = dvn_ref[...]
        dlg_ref[...] += jnp.sum(dvn * vhat, axis=0, keepdims=True)
        dlb_ref[...] += jnp.sum(dvn, axis=0, keepdims=True)
        dvhat = dvn * lg
        dv = rs * (dvhat - jnp.mean(dvhat, axis=-1, keepdims=True)
                   - vhat * jnp.mean(dvhat * vhat, axis=-1, keepdims=True))
        dp_ref[:, da:2 * da] = (dv * gv).astype(bf16)

    rows = SGU_CHUNKS * CHUNK
    blk = lambda cb: pl.BlockSpec((rows, da), lambda c: (c, cb))
    full = lambda shp: pl.BlockSpec(shp, lambda c: (0,) * len(shp))
    return pl.pallas_call(
        body, name="sgu_bwd", grid=(S // rows,),
        in_specs=[blk(0), blk(1), blk(2), blk(0), _layer_spec(ln_g, l), _layer_spec(ln_b, l), _layer_spec(w_s, l),
                  _layer_spec(b_t, l), pl.BlockSpec(memory_space=pl.ANY)],
        out_specs=[pl.BlockSpec((rows, 3 * da), lambda c: (c, 0)), full((A_GROUPS, CHUNK, CHUNK)),
                   full((CHUNK, HEAD_DIM)), full((1, da)), full((1, da))],
        out_shape=[SDS((S, n_proj), bf16), SDS((A_GROUPS, CHUNK, CHUNK), f32), SDS((CHUNK, HEAD_DIM), f32),
                   SDS((1, da), f32), SDS((1, da), f32)],
        scratch_shapes=[pltpu.VMEM((rows, da), f32)], compiler_params=_params(),
    )(proj, proj, proj, dy, ln_g, ln_b, w_s, b_t, after)


def _sb_scores(q, kblk, kb, rows, cols, masked):
    z = _dot(q, kblk, NT) * (1.0 / math.sqrt(HEAD_DIM))
    t = jnp.log(1.0 + jnp.exp(-jnp.abs(z)))
    log_1mb = -(jnp.maximum(z, 0.0) + t)
    log_beta = jnp.minimum(z, 0.0) - t
    if not masked:
        return None, log_beta, log_1mb
    causal = (cols + kb * TK) < rows
    return causal, log_beta, jnp.where(causal, log_1mb, 0.0)


def _sb_tiles(i):
    rows = i * TQ + lax.broadcasted_iota(jnp.int32, (TQ, TK), 0)
    cols = lax.broadcasted_iota(jnp.int32, (TQ, TK), 1)
    r_i = lax.broadcasted_iota(jnp.int32, (TK, TK), 0)
    c_i = lax.broadcasted_iota(jnp.int32, (TK, TK), 1)
    upper, lower = (r_i > c_i).astype(bf16), (r_i < c_i).astype(bf16)
    slot = lax.broadcasted_iota(jnp.int32, (TQ, CARRY_LANES), 1)
    return rows, cols, slot, jnp.concatenate([upper, upper], axis=0), jnp.concatenate([lower, lower], axis=0)


def _suffix_sum(t, tri):
    hi = lax.bitcast_convert_type(lax.bitcast_convert_type(t, jnp.uint32) & jnp.uint32(0xFFFF0000), f32)
    both = jnp.concatenate([hi.astype(bf16), (t - hi).astype(bf16)], axis=1)
    return _dot(both, tri)


def _sb_fwd(proj, y_prev, col0, after):
    S = proj.shape[0]
    D = y_prev.shape[1]
    dh = N_HEADS * HEAD_DIM
    n_diag = TQ // TK

    def body(q_ref, k_ref, v_ref, z_ref, yp_ref, after_ref, y_ref, o_ref, car_ref, qb, kb_s, vb_s, c_ref):
        i = pl.program_id(0)

        @pl.when(i == 0)
        def _():
            kb_s[...] = k_ref[...].astype(bf16)
            vb_s[...] = v_ref[...].astype(bf16)

        qb[...] = q_ref[...].astype(bf16)
        o_ref[...] = jnp.zeros_like(o_ref)
        c_ref[...] = jnp.zeros_like(c_ref)
        car_ref[...] = jnp.zeros_like(car_ref)
        nkb = (i + 1) * n_diag
        rows, cols, slot, upper, _ = _sb_tiles(i)

        def make_step(masked):
            def step(jj, carry):
                kb = nkb - 1 - jj
                off = pl.multiple_of(kb * TK, TK)
                hs = range(N_HEADS)
                sls = [slice(h * HEAD_DIM, (h + 1) * HEAD_DIM) for h in hs]
                sc = [_sb_scores(qb[:, sls[h]], kb_s[pl.ds(off, TK), sls[h]], kb, rows, cols, masked) for h in hs]
                suf = [_suffix_sum(sc[h][2], upper) for h in hs]
                cs = [c_ref[h] for h in hs]
                es = [jnp.exp(sc[h][1] + suf[h] + cs[h][:, :1]) for h in hs]
                if masked:
                    es = [jnp.where(sc[h][0], es[h], 0.0) for h in hs]
                pv = [_dot(es[h].astype(bf16), vb_s[pl.ds(off, TK), sls[h]]) for h in hs]
                for h in hs:
                    o_ref[:, sls[h]] += pv[h]
                    car_ref[h] = jnp.where(slot == kb, cs[h], car_ref[h])
                    c_ref[h] = cs[h] + jnp.sum(sc[h][2], axis=1, keepdims=True)
                return carry
            return step

        lax.fori_loop(0, n_diag, make_step(True), 0)
        lax.fori_loop(n_diag, nkb, make_step(False), 0)
        z = z_ref[...]
        y_ref[...] = (o_ref[...] * (z * jax.nn.sigmoid(z))).astype(bf16)

    cb = col0 * HEAD_DIM // dh
    qspec = lambda k: pl.BlockSpec((TQ, dh), lambda i: (i, cb + k))
    kspec = lambda k: pl.BlockSpec((S, dh), lambda i: (0, cb + k))
    return pl.pallas_call(
        body, name="sb_fwd", grid=(S // TQ,),
        in_specs=[qspec(0), kspec(1), kspec(2), qspec(3), pl.BlockSpec(memory_space=pl.ANY),
                  pl.BlockSpec(memory_space=pl.ANY)],
        out_specs=[pl.BlockSpec((TQ, dh), lambda i: (i, A_GROUPS * HEAD_DIM // dh)),
                   pl.BlockSpec((TQ, dh), lambda i: (i, 0)),
                   pl.BlockSpec((N_HEADS, TQ, CARRY_LANES), lambda i: (0, i, 0))],
        out_shape=[SDS((S, D), bf16), SDS((S, dh), f32), SDS((N_HEADS, S, CARRY_LANES), f32)],
        input_output_aliases={4: 0},
        scratch_shapes=[pltpu.VMEM((TQ, dh), bf16), pltpu.VMEM((S, dh), bf16), pltpu.VMEM((S, dh), bf16),
                        pltpu.VMEM((N_HEADS, TQ, CARRY_LANES), f32)],
        compiler_params=_params(),
    )(proj, proj, proj, proj, y_prev, after)


def _sb_bwd(proj, o, car, dy, dproj_prev, col0, after):
    S = proj.shape[0]
    n_i = S // TQ
    dh = N_HEADS * HEAD_DIM
    n_diag = TQ // TK
    cb = col0 * HEAD_DIM // dh
    scale = 1.0 / math.sqrt(HEAD_DIM)

    def body(q_ref, k_ref, v_ref, z_ref, o_ref, car_ref, dy_ref, dpp_ref, after_ref,
             dp_ref, qb, kb_s, vb_s, dob, p_ref, dq_acc, dk_acc, dv_acc, st_a, st_b, st_k, st_v, tile_sems):
        i = pl.program_id(0)

        def put(stage_ref, row0, nrows, k):
            pltpu.sync_copy(stage_ref, dp_ref.at[pl.ds(row0, nrows), pl.ds((cb + k) * dh, dh)])

        def tile_copies(step):
            rows = pl.ds(pl.multiple_of(step * TQ, TQ), TQ)
            return [pltpu.make_async_copy(st, dp_ref.at[rows, pl.ds((cb + k) * dh, dh)], tile_sems.at[n])
                    for n, (st, k) in enumerate(((st_a, 0), (st_b, 3)))]

        @pl.when(i > 0)
        def _():
            for cp in tile_copies(i - 1):
                cp.wait()

        @pl.when(i == 0)
        def _():
            kb_s[...] = k_ref[...].astype(bf16)
            vb_s[...] = v_ref[...].astype(bf16)
            dk_acc[...] = jnp.zeros_like(dk_acc)
            dv_acc[...] = jnp.zeros_like(dv_acc)

        s, gs = _silu_and_grad(z_ref[...])
        dy = dy_ref[...]
        st_b[...] = (dy * o_ref[...] * gs).astype(bf16)
        dob[...] = (dy * s).astype(bf16)
        qb[...] = q_ref[...].astype(bf16)
        p_ref[...] = jnp.zeros_like(p_ref)
        dq_acc[...] = jnp.zeros_like(dq_acc)
        nkb = (i + 1) * n_diag
        rows, cols, slot, upper, lower = _sb_tiles(i)

        def make_step(masked):
            def step(kb, carry):
                off = pl.multiple_of(kb * TK, TK)
                hs = range(N_HEADS)
                sls = [slice(h * HEAD_DIM, (h + 1) * HEAD_DIM) for h in hs]
                qs = [qb[:, sls[h]] for h in hs]
                ks = [kb_s[pl.ds(off, TK), sls[h]] for h in hs]
                dos = [dob[:, sls[h]] for h in hs]
                sc = [_sb_scores(qs[h], ks[h], kb, rows, cols, masked) for h in hs]
                da = [_dot(dos[h], vb_s[pl.ds(off, TK), sls[h]], NT) for h in hs]
                suf = [_suffix_sum(sc[h][2], upper) for h in hs]
                onehot = slot == kb
                cs = [jnp.sum(jnp.where(onehot, car_ref[h], 0.0), axis=1, keepdims=True) for h in hs]
                es = [jnp.exp(sc[h][1] + suf[h] + cs[h]) for h in hs]
                if masked:
                    es = [jnp.where(sc[h][0], es[h], 0.0) for h in hs]
                gs_ = [da[h] * es[h] for h in hs]
                ps = [p_ref[h] for h in hs]
                pre = [_suffix_sum(gs_[h], lower) + ps[h][:, :1] for h in hs]
                dzs = []
                for h in hs:
                    beta = jnp.exp(sc[h][1])
                    dzz = gs_[h] * (1.0 - beta) - beta * pre[h]
                    if masked:
                        dzz = jnp.where(sc[h][0], dzz, 0.0)
                    dzs.append((dzz * scale).astype(bf16))
                dqs = [_dot(dzs[h], ks[h]) for h in hs]
                dks = [_dot(dzs[h], qs[h], TN) for h in hs]
                dvs = [_dot(es[h].astype(bf16), dos[h], TN) for h in hs]
                for h in hs:
                    dq_acc[:, sls[h]] += dqs[h]
                    dk_acc[pl.ds(off, TK), sls[h]] += dks[h]
                    dv_acc[pl.ds(off, TK), sls[h]] += dvs[h]
                    p_ref[h] = ps[h] + jnp.sum(gs_[h], axis=1, keepdims=True)
                return carry
            return step

        lax.fori_loop(0, nkb - n_diag, make_step(False), 0)
        lax.fori_loop(nkb - n_diag, nkb, make_step(True), 0)
        st_a[...] = dq_acc[...].astype(bf16)
        for cp in tile_copies(i):
            cp.start()

        @pl.when(i == n_i - 1)
        def _():
            st_k[...] = dk_acc[...].astype(bf16)
            st_v[...] = dv_acc[...].astype(bf16)
            put(st_k, 0, S, 1)
            put(st_v, 0, S, 2)
            for cp in tile_copies(i):
                cp.wait()

    qspec = lambda k: pl.BlockSpec((TQ, dh), lambda i: (i, cb + k))
    kspec = lambda k: pl.BlockSpec((S, dh), lambda i: (0, cb + k))
    return pl.pallas_call(
        body, name="sb_bwd", grid=(n_i,),
        in_specs=[qspec(0), kspec(1), kspec(2), qspec(3),
                  pl.BlockSpec((TQ, dh), lambda i: (i, 0)),
                  pl.BlockSpec((N_HEADS, TQ, CARRY_LANES), lambda i: (0, i, 0)),
                  pl.BlockSpec((TQ, dh), lambda i: (i, A_GROUPS * HEAD_DIM // dh)),
                  pl.BlockSpec(memory_space=pl.ANY), pl.BlockSpec(memory_space=pl.ANY)],
        out_specs=pl.BlockSpec(memory_space=pl.ANY),
        out_shape=SDS(dproj_prev.shape, bf16),
        input_output_aliases={7: 0},
        scratch_shapes=[pltpu.VMEM((TQ, dh), bf16), pltpu.VMEM((S, dh), bf16), pltpu.VMEM((S, dh), bf16),
                        pltpu.VMEM((TQ, dh), bf16), pltpu.VMEM((N_HEADS, TQ, CARRY_LANES), f32), pltpu.VMEM((TQ, dh), f32),
                        pltpu.VMEM((S, dh), f32), pltpu.VMEM((S, dh), f32),
                        pltpu.VMEM((TQ, dh), bf16), pltpu.VMEM((TQ, dh), bf16),
                        pltpu.VMEM((S, dh), bf16), pltpu.VMEM((S, dh), bf16), pltpu.SemaphoreType.DMA((2,))],
        compiler_params=_params(56),
    )(proj, proj, proj, proj, o, car, dy, dproj_prev, after)


def _mem_kv(mem, mg, l, w_kv_g):
    M, D = mem.shape
    rb, n = w_kv_g.shape[1], w_kv_g.shape[2]

    def body(m_ref, g_ref, w_ref, kv_ref):
        mv = m_ref[...]
        r = lax.rsqrt(jnp.mean(mv * mv, axis=-1, keepdims=True) + EPS)
        mh = (mv * r * g_ref[...]).astype(bf16)
        kv_ref[...] = _dot(mh, w_ref[...].reshape(N_DEV * rb, n))

    return pl.pallas_call(
        body, name="mem_kv", grid=(1,),
        in_specs=[pl.BlockSpec((M, D), lambda i: (0, 0)), _layer_spec(mg, l),
                  pl.BlockSpec((N_DEV, rb, n), lambda i: (0, 0, 0))],
        out_specs=pl.BlockSpec((M, n), lambda i: (0, 0)),
        out_shape=SDS((M, n), f32), compiler_params=_params(),
    )(mem, mg, w_kv_g)


def _xattn_head(q_ref, kv_ref, qg, kg, h):
    dc = N_HEADS * HEAD_DIM
    sl = slice(h * HEAD_DIM, (h + 1) * HEAD_DIM)
    qh = q_ref[:, sl]
    rq = lax.rsqrt(jnp.mean(qh * qh, axis=-1, keepdims=True) + EPS)
    qhat = qh * rq
    qn = (qhat * qg).astype(bf16)
    kh = kv_ref[:, sl]
    rk = lax.rsqrt(jnp.mean(kh * kh, axis=-1, keepdims=True) + EPS)
    kn = (kh * rk * kg).astype(bf16)
    vh = kv_ref[:, dc + h * HEAD_DIM:dc + (h + 1) * HEAD_DIM].astype(bf16)
    s = _dot(qn, kn, NT) * (1.0 / math.sqrt(HEAD_DIM))
    e = jnp.exp(s - jnp.max(s, axis=-1, keepdims=True))
    p = e / jnp.sum(e, axis=-1, keepdims=True)
    o = _dot(p.astype(bf16), vh)
    return sl, rq, qhat, qn, kn, vh, p, o


def _xattn_fwd(proj, kv, qg, kg, l, y_prev, col0, tq=512):
    S = proj.shape[0]
    D = y_prev.shape[1]
    dc = N_HEADS * HEAD_DIM
    M = kv.shape[0]

    def body(q_ref, z_ref, kv_ref, qg_ref, kg_ref, yp_ref, y_ref):
        for h in range(N_HEADS):
            sl, _, _, _, _, _, _, o = _xattn_head(q_ref, kv_ref, qg_ref[...], kg_ref[...], h)
            z = z_ref[:, sl]
            y_ref[:, sl] = (o * (z * jax.nn.sigmoid(z))).astype(bf16)

    full = lambda shp: pl.BlockSpec(shp, lambda i: (0,) * len(shp))
    return pl.pallas_call(
        body, name="xattn_fwd", grid=(S // tq,),
        in_specs=[pl.BlockSpec((tq, dc), lambda i: (i, col0)), pl.BlockSpec((tq, dc), lambda i: (i, col0 + 1)),
                  full((M, 2 * dc)), _layer_spec(qg, l), _layer_spec(kg, l), pl.BlockSpec(memory_space=pl.ANY)],
        out_specs=pl.BlockSpec((tq, dc), lambda i: (i, D // dc - 1)),
        out_shape=SDS((S, D), bf16), input_output_aliases={5: 0}, compiler_params=_params(),
    )(proj, proj, kv, qg, kg, y_prev)


def _xattn_bwd(proj, kv, qg, kg, l, dy, dproj_prev, col0, tq=512):
    S = proj.shape[0]
    D = dy.shape[1]
    dc = N_HEADS * HEAD_DIM
    M = kv.shape[0]

    def body(q_ref, z_ref, kv_ref, qg_ref, kg_ref, dy_ref, dpp_ref, dp_ref, dkn_ref, dv_ref, dqg_ref):
        @pl.when(pl.program_id(0) == 0)
        def _():
            dkn_ref[...] = jnp.zeros_like(dkn_ref)
            dv_ref[...] = jnp.zeros_like(dv_ref)
            dqg_ref[...] = jnp.zeros_like(dqg_ref)

        qg = qg_ref[...]
        for h in range(N_HEADS):
            sl, rq, qhat, qn, kn, vh, p, o = _xattn_head(q_ref, kv_ref, qg, kg_ref[...], h)
            s, gs = _silu_and_grad(z_ref[:, sl])
            dyh = dy_ref[:, sl]
            dp_ref[:, dc + h * HEAD_DIM:dc + (h + 1) * HEAD_DIM] = (dyh * o * gs).astype(bf16)
            dob = (dyh * s).astype(bf16)
            dpr = _dot(dob, vh, NT)
            dv_ref[:, sl] += _dot(p.astype(bf16), dob, TN)
            ds = (p * (dpr - jnp.sum(p * dpr, axis=-1, keepdims=True)) * (1.0 / math.sqrt(HEAD_DIM))).astype(bf16)
            dqn = _dot(ds, kn)
            dkn_ref[:, sl] += _dot(ds, qn, TN)
            dqg_ref[...] += jnp.sum(dqn * qhat, axis=0, keepdims=True)
            dqhat = dqn * qg
            dp_ref[:, sl] = (rq * (dqhat - qhat * jnp.mean(dqhat * qhat, axis=-1, keepdims=True))).astype(bf16)

    full = lambda shp: pl.BlockSpec(shp, lambda i: (0,) * len(shp))
    return pl.pallas_call(
        body, name="xattn_bwd", grid=(S // tq,),
        in_specs=[pl.BlockSpec((tq, dc), lambda i: (i, col0)), pl.BlockSpec((tq, dc), lambda i: (i, col0 + 1)),
                  full((M, 2 * dc)), _layer_spec(qg, l), _layer_spec(kg, l),
                  pl.BlockSpec((tq, dc), lambda i: (i, D // dc - 1)), pl.BlockSpec(memory_space=pl.ANY)],
        out_specs=[pl.BlockSpec((tq, 2 * dc), lambda i: (i, col0 // 2)), full((M, dc)), full((M, dc)),
                   full((1, HEAD_DIM))],
        out_shape=[SDS(dproj_prev.shape, bf16), SDS((M, dc), f32), SDS((M, dc), f32), SDS((1, HEAD_DIM), f32)],
        input_output_aliases={6: 0}, compiler_params=_params(),
    )(proj, proj, kv, qg, kg, dy, dproj_prev)


def _mem_bwd(mem, mg, kg, l, kv, dkn, dv, w_kv_g):
    M, D = mem.shape
    rb, n = w_kv_g.shape[1], w_kv_g.shape[2]
    dc = n // 2

    def body(m_ref, g_ref, kv_ref, dkn_ref, dv_ref, kg_ref, w_ref, dw_ref, dmg_ref, dkg_ref, dkv_ref):
        mv = m_ref[...]
        r = lax.rsqrt(jnp.mean(mv * mv, axis=-1, keepdims=True) + EPS)
        mhat = mv * r
        mh = (mhat * g_ref[...]).astype(bf16)
        kg = kg_ref[...]
        dkg = jnp.zeros((1, HEAD_DIM), f32)
        for h in range(N_HEADS):
            sl = slice(h * HEAD_DIM, (h + 1) * HEAD_DIM)
            kh = kv_ref[:, sl]
            rk = lax.rsqrt(jnp.mean(kh * kh, axis=-1, keepdims=True) + EPS)
            khat = kh * rk
            dkn_h = dkn_ref[:, sl]
            dkg = dkg + jnp.sum(dkn_h * khat, axis=0, keepdims=True)
            dkhat = dkn_h * kg
            dkv_ref[:, sl] = (rk * (dkhat - khat * jnp.mean(dkhat * khat, axis=-1, keepdims=True))).astype(bf16)
        dkv_ref[:, dc:] = dv_ref[...].astype(bf16)
        dkg_ref[...] = dkg
        dkv = dkv_ref[...]
        dw_ref[...] = _dot(mh, dkv, TN).astype(bf16).reshape(N_DEV, rb, n)
        dmh = _dot(dkv, w_ref[...].reshape(N_DEV * rb, n), NT)
        dmg_ref[...] = jnp.sum(dmh * mhat, axis=0, keepdims=True)

    full = lambda shp: pl.BlockSpec(shp, lambda i: (0,) * len(shp))
    wspec = full((N_DEV, rb, n))
    return pl.pallas_call(
        body, name="mem_bwd", grid=(1,),
        in_specs=[full((M, D)), _layer_spec(mg, l), full((M, n)), full((M, dc)), full((M, dc)), _layer_spec(kg, l), wspec],
        out_specs=[wspec, full((1, D)), full((1, HEAD_DIM))],
        out_shape=[SDS((N_DEV, rb, n), bf16), SDS((1, D), f32), SDS((1, HEAD_DIM), f32)],
        scratch_shapes=[pltpu.VMEM((M, n), bf16)], compiler_params=_params(),
    )(mem, mg, kv, dkn, dv, kg, w_kv_g)


SMALL = ("norm_g", "sgu_ln_g", "sgu_ln_b", "sgu_w", "sgu_b", "mem_norm_g", "q_norm_g", "k_norm_g")


def _small_rows(like):
    rows = [math.prod(like[n].shape) // 128 for n in SMALL]
    offs = [0]
    for r in rows:
        offs.append(offs[-1] + -(-r // 8) * 8)
    return rows, offs


def _pack_small(parts, offs):
    pieces = []
    for k, n in enumerate(SMALL):
        a = parts[n].reshape(-1, 128)
        pieces.append(jnp.pad(a, ((0, offs[k + 1] - offs[k] - a.shape[0]), (0, 0))))
    return jnp.concatenate(pieces)


def kernel(x, mem, norm_g, w_in, sgu_ln_g, sgu_ln_b, sgu_w, sgu_b, mem_norm_g, w_mem_kv, q_norm_g, k_norm_g, w_out, loss_target, m_norm_g, m_w_in, m_sgu_ln_g, m_sgu_ln_b, m_sgu_w, m_sgu_b, m_mem_norm_g, m_w_mem_kv, m_q_norm_g, m_k_norm_g, m_w_out, v_norm_g, v_w_in, v_sgu_ln_g, v_sgu_ln_b, v_sgu_w, v_sgu_b, v_mem_norm_g, v_w_mem_kv, v_q_norm_g, v_k_norm_g, v_w_out):
    L, D, wc = w_in.shape
    S = x.shape[1]
    da = D // 2
    xs = x.reshape(S, D)
    mems = mem.reshape(mem.shape[1], D)
    tgt = loss_target.reshape(S, D)
    stacked = lambda a: a.reshape(a.shape[0], 1, -1)
    ng, lng, lnb, mg, qg, kg = map(stacked, (norm_g, sgu_ln_g, sgu_ln_b, mem_norm_g, q_norm_g, k_norm_g))
    b_t = jnp.swapaxes(sgu_b, 1, 2)
    sb_col, xa_col = 3 * da // HEAD_DIM, (3 * da + D) // (D // 4)

    ax, ay, ac = lax.axis_index("x"), lax.axis_index("y"), lax.axis_index("c")
    ids = jnp.stack([4 * ax + 2 * ay + ac, 2 * ax + ay, ac]).astype(jnp.int32)
    w_in0_b = _cast_into_slot("cast_w_in", w_in, 0, 512, ids, ids)
    first = _gather3_start("gather_w_in0", [w_in0_b], ids)
    late = first["token"]
    w_b = [(w_in0_b if l == 0 else _cast_into_slot("cast_w_in", w_in, l, 512, ids, late),
            _cast_into_slot("cast_w_kv", w_mem_kv, l, 256, ids, late),
            _cast_into_slot("cast_w_out", w_out, l, 256, ids, late)) for l in range(L)]
    relay = _gather3_relay("gather_w_in0_relay",
                           _split_wait(first, *[a for wl in w_b for a in wl if a is not w_in0_b]), ids)
    in_fwd = _gather2_forward("gather_w_in0_forward", _split_wait(relay, relay["token"]), ids)

    acts = []
    xl = xs
    for l in range(L):
        (w_in_g,) = _split_wait(in_fwd, xl if l else in_fwd["token"])
        rest = _gather3_start(f"gather_w_rest{l}", [w_b[l][1], w_b[l][2]], w_in_g)
        order = rest["token"]
        if l + 1 < L:
            nxt = _gather3_start(f"gather_w_in{l + 1}", [w_b[l + 1][0]], order)
            order = nxt["token"]
        proj, h = _rms_proj(xl, ng, l, w_in_g, order)
        rest_relay = _gather3_relay(f"gather_w_rest{l}_relay", _split_wait(rest, proj), proj)
        y = _sgu_fwd(proj, lng, lnb, sgu_w, b_t, l, rest_relay["token"])
        rest_fwd = _gather2_forward(f"gather_w_rest{l}_forward", _split_wait(rest_relay, y), y)
        order = rest_fwd["token"]
        if l + 1 < L:
            nxt_relay = _gather3_relay(f"gather_w_in{l + 1}_relay", _split_wait(nxt, order), order)
            order = nxt_relay["token"]
        y, o_b, car = _sb_fwd(proj, y, sb_col, order)
        w_kv_g, w_out_g = _split_wait(rest_fwd, o_b)
        order = o_b
        if l + 1 < L:
            in_fwd = _gather2_forward(f"gather_w_in{l + 1}_forward", _split_wait(nxt_relay, o_b), o_b)
            order = in_fwd["token"]
        kv = _mem_kv(mems, mg, l, w_kv_g)
        y = _xattn_fwd(proj, kv, qg, kg, l, y, xa_col)
        acts.append((xl, proj, h, y, o_b, car, kv, w_in_g, w_kv_g, w_out_g))
        if l + 1 < L:
            xl = _out_proj(xl, y, w_out_g, order)
        else:
            dx, dxb, loss_part = _out_proj_loss(xl, y, w_out_g, tgt)
    loss = lax.psum(loss_part[0, 0], ("x", "y", "c"))

    weights = dict(norm_g=norm_g, sgu_ln_g=sgu_ln_g, sgu_ln_b=sgu_ln_b, sgu_w=sgu_w, sgu_b=sgu_b,
                   mem_norm_g=mem_norm_g, q_norm_g=q_norm_g, k_norm_g=k_norm_g)
    moms_m = dict(norm_g=m_norm_g, sgu_ln_g=m_sgu_ln_g, sgu_ln_b=m_sgu_ln_b, sgu_w=m_sgu_w, sgu_b=m_sgu_b,
                  mem_norm_g=m_mem_norm_g, q_norm_g=m_q_norm_g, k_norm_g=m_k_norm_g)
    moms_v = dict(norm_g=v_norm_g, sgu_ln_g=v_sgu_ln_g, sgu_ln_b=v_sgu_ln_b, sgu_w=v_sgu_w, sgu_b=v_sgu_b,
                  mem_norm_g=v_mem_norm_g, q_norm_g=v_q_norm_g, k_norm_g=v_k_norm_g)
    small_rows, small_offs = _small_rows(weights)
    head_rows = D // 128
    assert SMALL[0] == "norm_g" and head_rows % 8 == 0

    seconds = {}
    pending = None
    adam = {"w_out": None, "w_mem_kv": None, "w_in": None}

    late_in = {}

    def update(lu, order):
        (r_out,) = _scatter_finish(seconds[f"g_out{lu}"], order)
        adam["w_out"] = _sum_adam("adam_w_out", r_out, w_out, m_w_out, v_w_out, lu, adam["w_out"], 128, ids, ids)
        r_kv, late_in[lu] = _scatter_finish(seconds[f"g_rest{lu}"], adam["w_out"][0])
        adam["w_mem_kv"] = _sum_adam("adam_w_kv", r_kv, w_mem_kv, m_w_mem_kv, v_w_mem_kv, lu, adam["w_mem_kv"], 256, ids, ids)
        return adam["w_mem_kv"][0]

    def update_in(lu, order):
        adam["w_in"] = _sum_adam("adam_w_in", late_in[lu], w_in, m_w_in, v_w_in, lu, adam["w_in"], 256, ids, order)
        return adam["w_in"][0]
    small = {n: [None] * L for n in SMALL}
    for l in reversed(range(L)):
        xl, proj, h, y, o_b, car, kv, w_in_g, w_kv_g, w_out_g = acts[l]
        dy = _out_bwd_dy(dxb, w_out_g)
        order = dy
        if pending is not None:
            seconds[pending[0]] = _scatter2_second_level(pending[0], pending[1], dy, ids)
            order = seconds[pending[0]]["token"]
        g_out = _tn_grad("out_bwd_dw", y, dxb, 512, 512, True, order)
        seconds[f"g_out{l}"] = _scatter1_start(f"scatter_g_out{l}", [g_out], ids)
        dproj, d_sw, d_sb, d_lg, d_lb = _sgu_bwd(proj, dy, lng, lnb, sgu_w, b_t, l, seconds[f"g_out{l}"]["token"])
        dproj = _sb_bwd(proj, o_b, car, dy, dproj, sb_col, d_lb)
        dproj, dkn, dv, d_qg = _xattn_bwd(proj, kv, qg, kg, l, dy, dproj, xa_col)
        g_kv, d_mg, d_kg = _mem_bwd(mems, mg, kg, l, kv, dkn, dv, w_kv_g)
        for n, val in (("sgu_ln_g", d_lg), ("sgu_ln_b", d_lb), ("sgu_w", d_sw), ("sgu_b", d_sb[:, :A_GROUPS].T),
                       ("mem_norm_g", d_mg), ("q_norm_g", d_qg), ("k_norm_g", d_kg)):
            small[n][l] = val.reshape(-1)
        order = d_kg
        if l == 0:
            small["norm_g"][0] = jnp.zeros_like(small["norm_g"][1])
            part = _pack_small({n: jnp.stack(small[n]) for n in SMALL}, small_offs)
            tail = _gather1_start("gather_small_tail", [_into_slot("small_tail_slot", part[head_rows:], ids)], ids)
            order = tail["token"]
        g_in_l = _tn_grad("in_bwd_dw", h, dproj, D, wc, False, order)
        first = _scatter2_pair_start(f"scatter_g_rest{l}_pair", [g_kv, g_in_l], ids)
        order = first["token"]
        pending = (f"g_rest{l}", first)
        if l == 0:
            for lu in reversed(range(1, L)):
                order = update(lu, order)
            seconds[pending[0]] = _scatter2_second_level(pending[0], pending[1], order, ids)
            order = seconds[pending[0]]["token"]
            for lu in reversed(range(1, L)):
                order = update_in(lu, order)
        dh = _in_bwd_dh(dproj, w_in_g, order)
        dx, dxb, d_ng = _rms_bwd(dh, xl, ng, l, dx, order)
        small["norm_g"][l] = d_ng.reshape(-1)
    head = _gather1_start("gather_small_head",
                          [_into_slot("small_head_slot", small["norm_g"][0].reshape(head_rows, 128), ids)], ids)

    (r_out,) = _scatter_finish(seconds["g_out0"], head["token"])
    adam["w_out"] = _sum_adam("adam_w_out", r_out, w_out, m_w_out, v_w_out, 0, adam["w_out"], 128, ids, ids)
    r_kv, r_in = _scatter_finish(seconds["g_rest0"], adam["w_out"][0])
    adam["w_mem_kv"] = _sum_adam("adam_w_kv", r_kv, w_mem_kv, m_w_mem_kv, v_w_mem_kv, 0, adam["w_mem_kv"], 256, ids, ids)
    adam["w_in"] = _sum_adam("adam_w_in", r_in, w_in, m_w_in, v_w_in, 0, adam["w_in"], 256, ids, ids)
    (r_tail,) = _split_wait(tail, adam["w_in"][0])
    (r_head,) = _split_wait(head, r_tail)
    as128 = lambda d: [d[n].reshape(-1, 128) for n in SMALL]
    sm = _small_sum_adam(r_head, r_tail, as128(weights), as128(moms_m), as128(moms_v), small_offs)
    res = dict(adam)
    for p, n in enumerate(SMALL):
        res[n] = [sm[k * len(SMALL) + p].reshape(weights[n].shape) for k in range(4)]

    order = ("norm_g", "w_in", "sgu_ln_g", "sgu_ln_b", "sgu_w", "sgu_b", "mem_norm_g", "w_mem_kv", "q_norm_g",
             "k_norm_g", "w_out")
    outs = [loss, dx.reshape(x.shape)]
    for k in range(4):
        outs += [res[n][k] for n in order]
    return tuple(outs)
```

```python
import functools
import math

import jax
import jax.numpy as jnp
from jax import lax
from jax.experimental import pallas as pl
from jax.experimental.pallas import tpu as pltpu

f32 = jnp.float32
bf16 = jnp.bfloat16
SDS = jax.ShapeDtypeStruct

N_DEV = 8
EPS = 1e-6
CHUNK = 128
SGU_CHUNKS = 2
A_GROUPS = 8
HEAD_DIM = 128
N_HEADS = 4
TQ = 256
TK = 256
CARRY_LANES = 128
ADAM_LR, ADAM_B1, ADAM_B2, ADAM_EPS, ADAM_WD, ADAM_STEP = 0.001, 0.9, 0.999, 1e-08, 0.01, 10
MIB = 1024 * 1024

NT = (((1,), (1,)), ((), ()))
TN = (((0,), (0,)), ((), ()))


def _params(vmem_mib=48):
    return pltpu.CompilerParams(vmem_limit_bytes=vmem_mib * MIB)


def _gelu_and_grad(x):
    e = lax.erf(x * (1.0 / math.sqrt(2.0)))
    cdf = 0.5 * (1.0 + e)
    pdf = jnp.exp(-0.5 * x * x) * (1.0 / math.sqrt(2.0 * math.pi))
    return x * cdf, cdf + x * pdf


def _gelu(x):
    return 0.5 * x * (1.0 + lax.erf(x * (1.0 / math.sqrt(2.0))))


def _silu_and_grad(z):
    sg = jax.nn.sigmoid(z)
    return z * sg, sg * (1.0 + z * (1.0 - sg))


def _layer_spec(stacked, l):
    rest = stacked.shape[1:]
    return pl.BlockSpec((None,) + rest, lambda *idx: (l,) + (0,) * len(rest))


def _dot(a, b, dims=None):
    if dims is None:
        return jnp.dot(a, b, preferred_element_type=f32)
    return lax.dot_general(a, b, dims, preferred_element_type=f32)


_HBM = pl.BlockSpec(memory_space=pltpu.HBM)
_SEM = pl.BlockSpec(memory_space=pltpu.SEMAPHORE)
_EFFECT = pltpu.SideEffectType.DATAFLOW_SIDE_EFFECTING


def _split_start(name, bufs, n_remote, n_local, build, after):
    nb = len(bufs)

    def body(*refs):
        token = refs[-1]
        locals_, remotes = build(refs[:nb], *refs[nb + 1:nb + 4])
        for cp in locals_ + remotes:
            cp.start()
        token[...] = jnp.zeros_like(token)

    hbm = lambda a: pltpu.with_memory_space_constraint(a, pltpu.HBM)
    outs = pl.pallas_call(
        body, name=name,
        out_shape=(pltpu.SemaphoreType.DMA((n_remote,)), pltpu.SemaphoreType.DMA((n_remote,)),
                   pltpu.SemaphoreType.DMA((max(n_local, 1),)),
                   *[pltpu.HBM(b.shape, b.dtype) for b in bufs], SDS((8, 128), f32)),
        in_specs=[_HBM] * nb + [pl.BlockSpec(memory_space=pl.ANY)],
        out_specs=(_SEM, _SEM, _SEM, *[_HBM] * nb, pl.BlockSpec(memory_space=pltpu.VMEM)),
        input_output_aliases={k: 3 + k for k in range(nb)},
        compiler_params=pltpu.CompilerParams(has_side_effects=_EFFECT),
    )(*[hbm(b) for b in bufs], after)
    return dict(name=name, build=build, sems=outs[:3], bufs=outs[3:3 + nb], token=outs[-1])


def _split_wait(handle, *after):
    build, bufs = handle["build"], handle["bufs"]
    nb = len(bufs)

    def body(*refs):
        locals_, remotes = build(refs[:nb], *refs[nb:nb + 3])
        for cp in remotes:
            cp.wait_recv()
        for cp in remotes:
            cp.wait_send()
        for cp in locals_:
            cp.wait()

    outs = pl.pallas_call(
        body, name=handle["name"] + "_wait",
        out_shape=tuple(pltpu.HBM(b.shape, b.dtype) for b in bufs),
        in_specs=[_HBM] * nb + [_SEM] * 3 + [pl.BlockSpec(memory_space=pl.ANY)] * len(after),
        out_specs=tuple([_HBM] * nb),
        input_output_aliases={k: k for k in range(nb)},
        compiler_params=pltpu.CompilerParams(has_side_effects=_EFFECT),
    )(*bufs, *handle["sems"], *after)
    return list(outs)


def _remote(src, dst, send_sems, recv_sems, k, to):
    return pltpu.make_async_remote_copy(src_ref=src, dst_ref=dst, send_sem=send_sems.at[k], recv_sem=recv_sems.at[k],
                                        device_id=to, device_id_type=pl.DeviceIdType.MESH)


def _other_chips(x, y):
    return [(1 - x, y), (x, 1 - y), (1 - x, 1 - y)]


def _all_peers(x, y, c):
    return [(1 - x if m & 4 else x, 1 - y if m & 2 else y, 1 - c if m & 1 else c) for m in range(1, N_DEV)]


def _gather1_start(name, lands, after):
    def build(refs, send, recv, loc):
        x, y, c = lax.axis_index("x"), lax.axis_index("y"), lax.axis_index("c")
        me = 4 * x + 2 * y + c
        return [], [_remote(d.at[me], d.at[me], send, recv, 7 * a + k, peer)
                    for a, d in enumerate(refs) for k, peer in enumerate(_all_peers(x, y, c))]

    return _split_start(name, list(lands), 7 * len(lands), 0, build, after)


def _scatter1_start(name, srcs, after):
    n = len(srcs)

    def build(refs, send, recv, loc):
        x, y, c = lax.axis_index("x"), lax.axis_index("y"), lax.axis_index("c")
        me = 4 * x + 2 * y + c
        return [], [_remote(refs[a].at[4 * px + 2 * py + pc], refs[n + a].at[me], send, recv, 7 * a + k, (px, py, pc))
                    for a in range(n) for k, (px, py, pc) in enumerate(_all_peers(x, y, c))]

    return _split_start(name, list(srcs) + [lax.empty(s.shape, s.dtype) for s in srcs], 7 * n, 0, build, after)


def _gather2_start(name, lands, after):
    def build(refs, send, recv, loc):
        x, y, c = lax.axis_index("x"), lax.axis_index("y"), lax.axis_index("c")
        me = 4 * x + 2 * y + c
        remotes = []
        for a, d in enumerate(refs):
            remotes.append(_remote(d.at[me], d.at[me], send, recv, 4 * a, (x, y, 1 - c)))
            remotes += [_remote(d.at[me], d.at[me], send, recv, 4 * a + 1 + k, (px, py, c))
                        for k, (px, py) in enumerate(_other_chips(x, y))]
        return [], remotes

    return _split_start(name, list(lands), 4 * len(lands), 0, build, after)


def _gather3_start(name, lands, after):
    def build(refs, send, recv, loc):
        x, y, c = lax.axis_index("x"), lax.axis_index("y"), lax.axis_index("c")
        me = 4 * x + 2 * y + c
        return [], [_remote(d.at[me], d.at[me], send, recv, 3 * a + k, to)
                    for a, d in enumerate(refs) for k, to in enumerate([(x, y, 1 - c), (1 - x, y, c), (x, 1 - y, c)])]

    return _split_start(name, list(lands), 3 * len(lands), 0, build, after)


def _gather3_relay(name, lands, after):
    def build(refs, send, recv, loc):
        x, y, c = lax.axis_index("x"), lax.axis_index("y"), lax.axis_index("c")
        from_x = c == 0
        slot = 4 * jnp.where(from_x, 1 - x, x) + 2 * jnp.where(from_x, y, 1 - y) + c
        to = (jnp.where(from_x, x, 1 - x), jnp.where(from_x, 1 - y, y), c)
        return [], [_remote(d.at[slot], d.at[slot], send, recv, a, to) for a, d in enumerate(refs)]

    return _split_start(name, list(lands), len(lands), 0, build, after)


def _gather2_forward(name, lands, after):
    n = len(lands)

    def build(refs, send, recv, loc):
        x, y, c = lax.axis_index("x"), lax.axis_index("y"), lax.axis_index("c")
        slots = [4 * px + 2 * py + c for px, py in _other_chips(x, y)]
        return [], [_remote(d.at[sl], d.at[sl], send, recv, 3 * a + k, (x, y, 1 - c))
                    for a, d in enumerate(refs) for k, sl in enumerate(slots)]

    return _split_start(name, list(lands), 3 * n, 0, build, after)


def _scatter2_pair_start(name, srcs, after):
    n = len(srcs)

    def build(refs, send, recv, loc):
        x, y, c = lax.axis_index("x"), lax.axis_index("y"), lax.axis_index("c")
        return [], [_remote(refs[a].at[2 * q + 1 - c], refs[n + a].at[q], send, recv, 4 * a + q, (x, y, 1 - c))
                    for a in range(n) for q in range(4)]

    lands = [lax.empty((4,) + s.shape[1:], s.dtype) for s in srcs]
    return _split_start(name, list(srcs) + lands, 4 * n, 0, build, after)


def _scatter2_chip_start(name, pairs, after):
    n = len(pairs)

    def build(refs, send, recv, loc):
        x, y, c = lax.axis_index("x"), lax.axis_index("y"), lax.axis_index("c")
        return [], [_remote(refs[a].at[2 * px + py], refs[n + a].at[2 * x + y], send, recv, 3 * a + k, (px, py, c))
                    for a in range(n) for k, (px, py) in enumerate(_other_chips(x, y))]

    return _split_start(name, list(pairs) + [lax.empty(p.shape, p.dtype) for p in pairs], 3 * n, 0, build, after)


def _pair_sum(name, src, theirs, ids):
    _, R, C = theirs.shape
    tr = min(R, 1024)

    def body(ids_ref, a_ref, b_ref, o_ref):
        o_ref[...] = (a_ref[...].astype(f32) + b_ref[...].astype(f32)).astype(bf16)

    spec = pl.BlockSpec((None, tr, C), lambda q, i, ids: (q, i, 0))
    return pl.pallas_call(
        body, name=name,
        grid_spec=pltpu.PrefetchScalarGridSpec(
            num_scalar_prefetch=1, grid=(4, R // tr),
            in_specs=[pl.BlockSpec((None, tr, C), lambda q, i, ids: (2 * q + ids[2], i, 0)), spec], out_specs=spec),
        out_shape=SDS(theirs.shape, bf16), compiler_params=_params(),
    )(ids, src, theirs)


def _scatter2_second_level(name, first, after, ids):
    outs = _split_wait(first, after)
    n = len(outs) // 2
    pairs = [_pair_sum(f"pair_sum_{name}{a}", outs[a], outs[n + a], ids) for a in range(n)]
    return _scatter2_chip_start(f"scatter_{name}_chip", pairs, ids)


def _scatter_finish(second, after):
    outs = _split_wait(second, after)
    n = len(outs) // 2
    return [(outs[a], outs[n + a]) for a in range(n)]


def _cast_into_slot(name, w, l, tr, ids, after):
    _, R, C = w.shape

    def body(ids_ref, w_ref, after_ref, o_ref):
        o_ref[...] = w_ref[...].astype(bf16)

    return pl.pallas_call(
        body, name=name,
        grid_spec=pltpu.PrefetchScalarGridSpec(
            num_scalar_prefetch=1, grid=(R // tr,),
            in_specs=[pl.BlockSpec((None, tr, C), lambda i, ids: (l, i, 0)), pl.BlockSpec(memory_space=pl.ANY)],
            out_specs=pl.BlockSpec((None, tr, C), lambda i, ids: (ids[0], i, 0))),
        out_shape=SDS((N_DEV, R, C), bf16), compiler_params=_params(),
    )(ids, w, after)


def _into_slot(name, a, ids):
    R, C = a.shape

    def body(ids_ref, a_ref, o_ref):
        o_ref[...] = a_ref[...]

    return pl.pallas_call(
        body, name=name,
        grid_spec=pltpu.PrefetchScalarGridSpec(
            num_scalar_prefetch=1, grid=(1,),
            in_specs=[pl.BlockSpec((R, C), lambda i, ids: (0, 0))],
            out_specs=pl.BlockSpec((None, R, C), lambda i, ids: (ids[0], 0, 0))),
        out_shape=SDS((N_DEV, R, C), f32), compiler_params=_params(),
    )(ids, a)


def _adam_math(w, g, m, v):
    m2 = ADAM_B1 * m + (1.0 - ADAM_B1) * g
    v2 = ADAM_B2 * v + (1.0 - ADAM_B2) * (g * g)
    m_hat = m2 / (1.0 - ADAM_B1 ** ADAM_STEP)
    v_hat = v2 / (1.0 - ADAM_B2 ** ADAM_STEP)
    delta = -ADAM_LR * (m_hat / (jnp.sqrt(v_hat) + ADAM_EPS) + ADAM_WD * w)
    return delta, m2, v2


def _sum_adam(name, pair_recv, w, m, v, l, prev, tr, ids, after):
    own, recv = pair_recv
    L, R, C = w.shape
    slots = recv.shape[0]
    mine = 0 if slots == N_DEV else 1

    def body(ids_ref, r_ref, own_ref, w_ref, m_ref, v_ref, after_ref, *rest):
        g_ref, d_ref, m2_ref, v2_ref = rest[-4:]
        terms = [jnp.where(ids_ref[mine] == q, own_ref[...], r_ref[q]).astype(f32) for q in range(slots)]
        g = terms[0]
        for t in terms[1:]:
            g = g + t
        d, m2, v2 = _adam_math(w_ref[...], g, m_ref[...], v_ref[...])
        g_ref[...] = g
        d_ref[...] = d
        m2_ref[...] = m2
        v2_ref[...] = v2

    wspec = pl.BlockSpec((None, tr, C), lambda i, ids: (l, i, 0))
    in_specs = [pl.BlockSpec((slots, tr, C), lambda i, ids: (0, i, 0)),
                pl.BlockSpec((None, tr, C), lambda i, ids: (ids[mine], i, 0)), wspec, wspec, wspec,
                pl.BlockSpec(memory_space=pl.ANY)]
    args = [ids, recv, own, w, m, v, after]
    aliases = {}
    if prev is not None:
        in_specs += [pl.BlockSpec(memory_space=pl.ANY)] * 4
        args += list(prev)
        aliases = {7 + k: k for k in range(4)}
    return pl.pallas_call(
        body, name=name,
        grid_spec=pltpu.PrefetchScalarGridSpec(num_scalar_prefetch=1, grid=(R // tr,), in_specs=in_specs,
                                               out_specs=[wspec] * 4),
        out_shape=[SDS((L, R, C), f32)] * 4, input_output_aliases=aliases, compiler_params=_params(),
    )(*args)


def _small_sum_adam(recv_head, recv_tail, ws, ms, vs, offs):
    n = len(ws)
    r0 = recv_head.shape[1]

    def body(*refs):
        rh, rt = refs[0], refs[1]
        w_refs, m_refs, v_refs = refs[2:2 + n], refs[2 + n:2 + 2 * n], refs[2 + 2 * n:2 + 3 * n]
        outs = refs[2 + 3 * n:]
        for p in range(n):
            lo, hi = offs[p], offs[p] + ws[p].shape[0]
            pieces = []
            if lo < r0:
                pieces.append((rh, lo, 0, min(hi, r0) - lo))
            if hi > r0:
                pieces.append((rt, max(lo, r0) - r0, max(lo, r0) - lo, hi - max(lo, r0)))
            for src, a, b, cnt in pieces:
                g = src[0, a:a + cnt, :]
                for s in range(1, N_DEV):
                    g = g + src[s, a:a + cnt, :]
                d, m2, v2 = _adam_math(w_refs[p][b:b + cnt, :], g, m_refs[p][b:b + cnt, :], v_refs[p][b:b + cnt, :])
                for k, val in enumerate((g, d, m2, v2)):
                    outs[k * n + p][b:b + cnt, :] = val

    return pl.pallas_call(
        body, name="small_sum_adam", out_shape=[SDS(w.shape, f32) for w in ws] * 4, compiler_params=_params(),
    )(recv_head, recv_tail, *ws, *ms, *vs)


def _rms_proj(x, g, l, w_in_g, after, tm=1024):
    S, D = x.shape
    wc = w_in_g.shape[2]
    n_out = N_DEV * wc

    def body(x_ref, g_ref, w_ref, after_ref, proj_ref, h_ref):
        @pl.when(pl.program_id(1) == 0)
        def _():
            xv = x_ref[...]
            r = lax.rsqrt(jnp.mean(xv * xv, axis=-1, keepdims=True) + EPS)
            h_ref[...] = (xv * r * g_ref[...]).astype(bf16)

        proj_ref[...] = _dot(h_ref[...], w_ref[...])

    return pl.pallas_call(
        body, name="rms_proj", grid=(S // tm, N_DEV),
        in_specs=[pl.BlockSpec((tm, D), lambda i, j: (i, 0)), _layer_spec(g, l),
                  pl.BlockSpec((None, D, wc), lambda i, j: (j, 0, 0)), pl.BlockSpec(memory_space=pl.ANY)],
        out_specs=[pl.BlockSpec((tm, wc), lambda i, j: (i, j)), pl.BlockSpec((tm, D), lambda i, j: (i, 0))],
        out_shape=[SDS((S, n_out), f32), SDS((S, D), bf16)], compiler_params=_params(),
    )(x, g, w_in_g, after)


def _rms_h(x, g, l, after, tm=512):
    S, D = x.shape

    def body(x_ref, g_ref, after_ref, h_ref):
        xv = x_ref[...]
        r = lax.rsqrt(jnp.mean(xv * xv, axis=-1, keepdims=True) + EPS)
        h_ref[...] = (xv * r * g_ref[...]).astype(bf16)

    row = pl.BlockSpec((tm, D), lambda i: (i, 0))
    return pl.pallas_call(
        body, name="rms_h", grid=(S // tm,), in_specs=[row, _layer_spec(g, l), pl.BlockSpec(memory_space=pl.ANY)],
        out_specs=row, out_shape=SDS((S, D), bf16), compiler_params=_params(),
    )(x, g, after)


def _proj_blocks(h, w_in_g, table, first, count, proj_prev, tm=1024):
    S, D = h.shape
    wc = w_in_g.shape[2]

    def body(tbl_ref, h_ref, w_ref, *rest):
        rest[-1][...] = _dot(h_ref[...], w_ref[...])

    in_specs = [pl.BlockSpec((tm, D), lambda i, j, tbl: (i, 0)),
                pl.BlockSpec((None, D, wc), lambda i, j, tbl: (tbl[first + j], 0, 0))]
    args = [table, h, w_in_g]
    aliases = {}
    if proj_prev is not None:
        in_specs.append(pl.BlockSpec(memory_space=pl.ANY))
        args.append(proj_prev)
        aliases = {3: 0}
    return pl.pallas_call(
        body, name="proj_blocks",
        grid_spec=pltpu.PrefetchScalarGridSpec(
            num_scalar_prefetch=1, grid=(S // tm, count), in_specs=in_specs,
            out_specs=pl.BlockSpec((tm, wc), lambda i, j, tbl: (i, tbl[first + j]))),
        out_shape=SDS((S, N_DEV * wc), f32), input_output_aliases=aliases, compiler_params=_params(),
    )(*args)


def _out_proj(x, y, w_out_g, after, tm=512):
    S, D = x.shape
    rb = w_out_g.shape[1]

    def body(x_ref, y_ref, w_ref, after_ref, o_ref):
        w = w_ref[...].reshape(N_DEV * rb, D)
        o_ref[...] = x_ref[...] + _dot(y_ref[...], w)

    row = pl.BlockSpec((tm, D), lambda i: (i, 0))
    return pl.pallas_call(
        body, name="out_proj", grid=(S // tm,),
        in_specs=[row, row, pl.BlockSpec((N_DEV, rb, D), lambda i: (0, 0, 0)), pl.BlockSpec(memory_space=pl.ANY)],
        out_specs=row, out_shape=SDS((S, D), f32), compiler_params=_params(),
    )(x, y, w_out_g, after)


def _out_proj_loss(x, y, w_out_g, tgt, tm=256):
    S, D = x.shape
    rb = w_out_g.shape[1]

    def body(x_ref, y_ref, w_ref, t_ref, dx_ref, dxb_ref, l_ref):
        i = pl.program_id(0)
        w = w_ref[...].reshape(N_DEV * rb, D)
        d = x_ref[...] + _dot(y_ref[...], w) - t_ref[...]
        dx = d * (1.0 / D)
        dx_ref[...] = dx
        dxb_ref[...] = dx.astype(bf16)
        e = d * d
        part = e[:, 0:128]
        for k in range(1, D // 128):
            part = part + e[:, k * 128:(k + 1) * 128]
        part = jnp.sum(part.reshape(tm // 8, 8, 128), axis=0)

        @pl.when(i == 0)
        def _():
            l_ref[...] = jnp.zeros_like(l_ref)

        l_ref[...] += part

        @pl.when(i == pl.num_programs(0) - 1)
        def _():
            tot = jnp.sum(l_ref[...], axis=1, keepdims=True)
            tot = jnp.sum(tot, axis=0, keepdims=True)
            l_ref[...] = jnp.broadcast_to(tot * (0.5 / D), l_ref.shape)

    row = pl.BlockSpec((tm, D), lambda i: (i, 0))
    return pl.pallas_call(
        body, name="out_proj_loss", grid=(S // tm,),
        in_specs=[row, row, pl.BlockSpec((N_DEV, rb, D), lambda i: (0, 0, 0)), row],
        out_specs=[row, row, pl.BlockSpec((8, 128), lambda i: (0, 0))],
        out_shape=[SDS((S, D), f32), SDS((S, D), bf16), SDS((8, 128), f32)], compiler_params=_params(),
    )(x, y, w_out_g, tgt)


def _out_bwd_dy(dxb, w_out_g, tm=512):
    S, D = dxb.shape
    rb = w_out_g.shape[1]

    nb = 2

    def body(dx_ref, w_ref, o_ref):
        o_ref[...] = _dot(dx_ref[...], w_ref[...].reshape(nb * rb, D), NT)

    return pl.pallas_call(
        body, name="out_bwd_dy", grid=(S // tm, N_DEV // nb),
        in_specs=[pl.BlockSpec((tm, D), lambda i, j: (i, 0)),
                  pl.BlockSpec((nb, rb, D), lambda i, j: (j, 0, 0))],
        out_specs=pl.BlockSpec((tm, nb * rb), lambda i, j: (i, j)),
        out_shape=SDS((S, D), f32), compiler_params=_params(),
    )(dxb, w_out_g)


def _tn_grad(name, a, b, tm, tn, rows_major, after):
    S, M = a.shape
    N = b.shape[1]
    if rows_major:
        rb = M // N_DEV
        nb = tm // rb
        out_shape = SDS((N_DEV, rb, N), bf16)
        out_spec = pl.BlockSpec((nb, rb, tn), lambda i, j: (i, 0, j))
    else:
        out_shape = SDS((N_DEV, M, N // N_DEV), bf16)
        assert tn == N // N_DEV
        out_spec = pl.BlockSpec((None, tm, tn), lambda i, j: (j, i, 0))

    def body(a_ref, b_ref, after_ref, o_ref):
        o_ref[...] = _dot(a_ref[...], b_ref[...], TN).astype(bf16).reshape(o_ref.shape)

    return pl.pallas_call(
        body, name=name, grid=(M // tm, N // tn),
        in_specs=[pl.BlockSpec((S, tm), lambda i, j: (0, i)), pl.BlockSpec((S, tn), lambda i, j: (0, j)),
                  pl.BlockSpec(memory_space=pl.ANY)],
        out_specs=out_spec, out_shape=out_shape, compiler_params=_params(),
    )(a, b, after)


def _in_bwd_dh(dproj, w_in_g, after, tm=1024, tn=256):
    S = dproj.shape[0]
    _, D, wc = w_in_g.shape
    tm = min(tm, S)

    def body(dp_ref, w_ref, after_ref, o_ref):
        acc = _dot(dp_ref[:, 0:wc], w_ref[0], NT)
        for k in range(1, N_DEV):
            acc = acc + _dot(dp_ref[:, k * wc:(k + 1) * wc], w_ref[k], NT)
        o_ref[...] = acc

    return pl.pallas_call(
        body, name="in_bwd_dh", grid=(S // tm, D // tn),
        in_specs=[pl.BlockSpec((tm, N_DEV * wc), lambda i, j: (i, 0)),
                  pl.BlockSpec((N_DEV, tn, wc), lambda i, j: (0, j, 0)), pl.BlockSpec(memory_space=pl.ANY)],
        out_specs=pl.BlockSpec((tm, tn), lambda i, j: (i, j)),
        out_shape=SDS((S, D), f32), compiler_params=_params(),
    )(dproj, w_in_g, after)


def _rms_bwd(dh, x, g, l, dx_next, after, tm=256):
    S, D = x.shape

    def body(dh_ref, x_ref, g_ref, dxn_ref, after_ref, dx_ref, dxb_ref, dg_ref):
        @pl.when(pl.program_id(0) == 0)
        def _():
            dg_ref[...] = jnp.zeros_like(dg_ref)

        dh = dh_ref[...]
        xv = x_ref[...]
        r = lax.rsqrt(jnp.mean(xv * xv, axis=-1, keepdims=True) + EPS)
        xhat = xv * r
        dxhat = dh * g_ref[...]
        dx = r * (dxhat - xhat * jnp.mean(dxhat * xhat, axis=-1, keepdims=True)) + dxn_ref[...]
        dx_ref[...] = dx
        dxb_ref[...] = dx.astype(bf16)
        dg_ref[...] += jnp.sum(dh * xhat, axis=0, keepdims=True)

    row = pl.BlockSpec((tm, D), lambda i: (i, 0))
    vec = pl.BlockSpec((1, D), lambda i: (0, 0))
    return pl.pallas_call(
        body, name="rms_bwd", grid=(S // tm,),
        in_specs=[row, row, _layer_spec(g, l), row, pl.BlockSpec(memory_space=pl.ANY)], out_specs=[row, row, vec],
        out_shape=[SDS((S, D), f32), SDS((S, D), bf16), SDS((1, D), f32)], compiler_params=_params(),
    )(dh, x, g, dx_next, after)


def _sgu_fwd(proj, ln_g, ln_b, w_s, b_t, l, after):
    S = proj.shape[0]
    da = A_GROUPS * HEAD_DIM
    D = 2 * da

    def body(u_ref, v_ref, z_ref, lg_ref, lb_ref, w_ref, bt_ref, after_ref, y_ref):
        u = _gelu(u_ref[...])
        v = _gelu(v_ref[...])
        z = z_ref[...]
        mu = jnp.mean(v, axis=-1, keepdims=True)
        xc = v - mu
        rs = lax.rsqrt(jnp.mean(xc * xc, axis=-1, keepdims=True) + EPS)
        vn = (xc * rs * lg_ref[...] + lb_ref[...]).astype(bf16)
        gate = u * (z * jax.nn.sigmoid(z))
        tri = lax.broadcasted_iota(jnp.int32, (CHUNK, CHUNK), 0) >= lax.broadcasted_iota(jnp.int32, (CHUNK, CHUNK), 1)
        for g in range(A_GROUPS):
            sl = slice(g * HEAD_DIM, (g + 1) * HEAD_DIM)
            wm = jnp.where(tri, w_ref[g], 0.0).astype(bf16)
            for ck in range(SGU_CHUNKS):
                rw = slice(ck * CHUNK, (ck + 1) * CHUNK)
                mixed = _dot(wm, vn[rw, sl]) + bt_ref[:, g:g + 1]
                y_ref[rw, sl] = (gate[rw, sl] * mixed).astype(bf16)

    blk = lambda cb: pl.BlockSpec((SGU_CHUNKS * CHUNK, da), lambda c: (c, cb))
    full = lambda shp: pl.BlockSpec(shp, lambda c: (0,) * len(shp))
    return pl.pallas_call(
        body, name="sgu_fwd", grid=(S // (SGU_CHUNKS * CHUNK),),
        in_specs=[blk(0), blk(1), blk(2), _layer_spec(ln_g, l), _layer_spec(ln_b, l), _layer_spec(w_s, l),
                  _layer_spec(b_t, l), pl.BlockSpec(memory_space=pl.ANY)],
        out_specs=blk(0), out_shape=SDS((S, D), bf16), compiler_params=_params(),
    )(proj, proj, proj, ln_g, ln_b, w_s, b_t, after)


def _sgu_bwd(proj, dy, ln_g, ln_b, w_s, b_t, l, after):
    S = proj.shape[0]
    da = A_GROUPS * HEAD_DIM
    n_proj = proj.shape[1]

    def body(u_ref, v_ref, z_ref, dy_ref, lg_ref, lb_ref, w_ref, bt_ref, after_ref,
             dp_ref, dw_ref, db_ref, dlg_ref, dlb_ref, dvn_ref):
        @pl.when(pl.program_id(0) == 0)
        def _():
            dw_ref[...] = jnp.zeros_like(dw_ref)
            db_ref[...] = jnp.zeros_like(db_ref)
            dlg_ref[...] = jnp.zeros_like(dlg_ref)
            dlb_ref[...] = jnp.zeros_like(dlb_ref)

        up, vp, z, dy = u_ref[...], v_ref[...], z_ref[...], dy_ref[...]
        u, gu = _gelu_and_grad(up)
        v, gv = _gelu_and_grad(vp)
        s, gs = _silu_and_grad(z)
        mu = jnp.mean(v, axis=-1, keepdims=True)
        xc = v - mu
        rs = lax.rsqrt(jnp.mean(xc * xc, axis=-1, keepdims=True) + EPS)
        vhat = xc * rs
        lg = lg_ref[...]
        vn = (vhat * lg + lb_ref[...]).astype(bf16)
        tri = lax.broadcasted_iota(jnp.int32, (CHUNK, CHUNK), 0) >= lax.broadcasted_iota(jnp.int32, (CHUNK, CHUNK), 1)
        lane = lax.broadcasted_iota(jnp.int32, (CHUNK, HEAD_DIM), 1)
        dys = dy * s
        db = jnp.zeros((CHUNK, HEAD_DIM), f32)
        for g in range(A_GROUPS):
            sl = slice(g * HEAD_DIM, (g + 1) * HEAD_DIM)
            zl = slice(2 * da + g * HEAD_DIM, 2 * da + (g + 1) * HEAD_DIM)
            wm = jnp.where(tri, w_ref[g], 0.0).astype(bf16)
            for ck in range(SGU_CHUNKS):
                rw = slice(ck * CHUNK, (ck + 1) * CHUNK)
                mixed = _dot(wm, vn[rw, sl]) + bt_ref[:, g:g + 1]
                dmix = dys[rw, sl] * u[rw, sl]
                dp_ref[rw, sl] = (dys[rw, sl] * mixed * gu[rw, sl]).astype(bf16)
                dp_ref[rw, zl] = (dy[rw, sl] * u[rw, sl] * mixed * gs[rw, sl]).astype(bf16)
                dmb = dmix.astype(bf16)
                dw_ref[g] += jnp.where(tri, _dot(dmb, vn[rw, sl], NT), 0.0)
                dvn_ref[rw, sl] = _dot(wm, dmb, TN)
                db = db + jnp.where(lane == g, jnp.sum(dmix, axis=1, keepdims=True), 0.0)
        db_ref[...] += db
        dvn = dvn_ref[...]
        dlg_ref[...] += jnp.sum(dvn * vhat, axis=0, keepdims=True)
        dlb_ref[...] += jnp.sum(dvn, axis=0, keepdims=True)
        dvhat = dvn * lg
        dv = rs * (dvhat - jnp.mean(dvhat, axis=-1, keepdims=True)
                   - vhat * jnp.mean(dvhat * vhat, axis=-1, keepdims=True))
        dp_ref[:, da:2 * da] = (dv * gv).astype(bf16)

    rows = SGU_CHUNKS * CHUNK
    blk = lambda cb: pl.BlockSpec((rows, da), lambda c: (c, cb))
    full = lambda shp: pl.BlockSpec(shp, lambda c: (0,) * len(shp))
    return pl.pallas_call(
        body, name="sgu_bwd", grid=(S // rows,),
        in_specs=[blk(0), blk(1), blk(2), blk(0), _layer_spec(ln_g, l), _layer_spec(ln_b, l), _layer_spec(w_s, l),
                  _layer_spec(b_t, l), pl.BlockSpec(memory_space=pl.ANY)],
        out_specs=[pl.BlockSpec((rows, 3 * da), lambda c: (c, 0)), full((A_GROUPS, CHUNK, CHUNK)),
                   full((CHUNK, HEAD_DIM)), full((1, da)), full((1, da))],
        out_shape=[SDS((S, n_proj), bf16), SDS((A_GROUPS, CHUNK, CHUNK), f32), SDS((CHUNK, HEAD_DIM), f32),
                   SDS((1, da), f32), SDS((1, da), f32)],
        scratch_shapes=[pltpu.VMEM((rows, da), f32)], compiler_params=_params(),
    )(proj, proj, proj, dy, ln_g, ln_b, w_s, b_t, after)


def _sb_scores(q, kblk, kb, rows, cols, masked):
    z = _dot(q, kblk, NT) * (1.0 / math.sqrt(HEAD_DIM))
    t = jnp.log(1.0 + jnp.exp(-jnp.abs(z)))
    log_1mb = -(jnp.maximum(z, 0.0) + t)
    log_beta = jnp.minimum(z, 0.0) - t
    if not masked:
        return None, log_beta, log_1mb
    causal = (cols + kb * TK) < rows
    return causal, log_beta, jnp.where(causal, log_1mb, 0.0)


def _sb_tiles(i):
    rows = i * TQ + lax.broadcasted_iota(jnp.int32, (TQ, TK), 0)
    cols = lax.broadcasted_iota(jnp.int32, (TQ, TK), 1)
    r_i = lax.broadcasted_iota(jnp.int32, (TK, TK), 0)
    c_i = lax.broadcasted_iota(jnp.int32, (TK, TK), 1)
    upper, lower = (r_i > c_i).astype(bf16), (r_i < c_i).astype(bf16)
    slot = lax.broadcasted_iota(jnp.int32, (TQ, CARRY_LANES), 1)
    return rows, cols, slot, jnp.concatenate([upper, upper], axis=0), jnp.concatenate([lower, lower], axis=0)


def _suffix_sum(t, tri):
    hi = lax.bitcast_convert_type(lax.bitcast_convert_type(t, jnp.uint32) & jnp.uint32(0xFFFF0000), f32)
    both = jnp.concatenate([hi.astype(bf16), (t - hi).astype(bf16)], axis=1)
    return _dot(both, tri)


def _sb_fwd(proj, y_prev, col0, after):
    S = proj.shape[0]
    D = y_prev.shape[1]
    dh = N_HEADS * HEAD_DIM
    n_diag = TQ // TK

    def body(q_ref, k_ref, v_ref, z_ref, yp_ref, after_ref, y_ref, o_ref, car_ref, qb, kb_s, vb_s, c_ref):
        i = pl.program_id(0)

        @pl.when(i == 0)
        def _():
            kb_s[...] = k_ref[...].astype(bf16)
            vb_s[...] = v_ref[...].astype(bf16)

        qb[...] = q_ref[...].astype(bf16)
        o_ref[...] = jnp.zeros_like(o_ref)
        c_ref[...] = jnp.zeros_like(c_ref)
        car_ref[...] = jnp.zeros_like(car_ref)
        nkb = (i + 1) * n_diag
        rows, cols, slot, upper, _ = _sb_tiles(i)

        def make_step(masked):
            def step(jj, carry):
                kb = nkb - 1 - jj
                off = pl.multiple_of(kb * TK, TK)
                hs = range(N_HEADS)
                sls = [slice(h * HEAD_DIM, (h + 1) * HEAD_DIM) for h in hs]
                sc = [_sb_scores(qb[:, sls[h]], kb_s[pl.ds(off, TK), sls[h]], kb, rows, cols, masked) for h in hs]
                suf = [_suffix_sum(sc[h][2], upper) for h in hs]
                cs = [c_ref[h] for h in hs]
                es = [jnp.exp(sc[h][1] + suf[h] + cs[h][:, :1]) for h in hs]
                if masked:
                    es = [jnp.where(sc[h][0], es[h], 0.0) for h in hs]
                pv = [_dot(es[h].astype(bf16), vb_s[pl.ds(off, TK), sls[h]]) for h in hs]
                for h in hs:
                    o_ref[:, sls[h]] += pv[h]
                    car_ref[h] = jnp.where(slot == kb, cs[h], car_ref[h])
                    c_ref[h] = cs[h] + jnp.sum(sc[h][2], axis=1, keepdims=True)
                return carry
            return step

        lax.fori_loop(0, n_diag, make_step(True), 0)
        lax.fori_loop(n_diag, nkb, make_step(False), 0)
        z = z_ref[...]
        y_ref[...] = (o_ref[...] * (z * jax.nn.sigmoid(z))).astype(bf16)

    cb = col0 * HEAD_DIM // dh
    qspec = lambda k: pl.BlockSpec((TQ, dh), lambda i: (i, cb + k))
    kspec = lambda k: pl.BlockSpec((S, dh), lambda i: (0, cb + k))
    return pl.pallas_call(
        body, name="sb_fwd", grid=(S // TQ,),
        in_specs=[qspec(0), kspec(1), kspec(2), qspec(3), pl.BlockSpec(memory_space=pl.ANY),
                  pl.BlockSpec(memory_space=pl.ANY)],
        out_specs=[pl.BlockSpec((TQ, dh), lambda i: (i, A_GROUPS * HEAD_DIM // dh)),
                   pl.BlockSpec((TQ, dh), lambda i: (i, 0)),
                   pl.BlockSpec((N_HEADS, TQ, CARRY_LANES), lambda i: (0, i, 0))],
        out_shape=[SDS((S, D), bf16), SDS((S, dh), f32), SDS((N_HEADS, S, CARRY_LANES), f32)],
        input_output_aliases={4: 0},
        scratch_shapes=[pltpu.VMEM((TQ, dh), bf16), pltpu.VMEM((S, dh), bf16), pltpu.VMEM((S, dh), bf16),
                        pltpu.VMEM((N_HEADS, TQ, CARRY_LANES), f32)],
        compiler_params=_params(),
    )(proj, proj, proj, proj, y_prev, after)


def _sb_bwd(proj, o, car, dy, dproj_prev, col0, after):
    S = proj.shape[0]
    n_i = S // TQ
    dh = N_HEADS * HEAD_DIM
    n_diag = TQ // TK
    cb = col0 * HEAD_DIM // dh
    scale = 1.0 / math.sqrt(HEAD_DIM)

    def body(q_ref, k_ref, v_ref, z_ref, o_ref, car_ref, dy_ref, dpp_ref, after_ref,
             dp_ref, qb, kb_s, vb_s, dob, p_ref, dq_acc, dk_acc, dv_acc, st_a, st_b, st_k, st_v, tile_sems):
        i = pl.program_id(0)

        def put(stage_ref, row0, nrows, k):
            pltpu.sync_copy(stage_ref, dp_ref.at[pl.ds(row0, nrows), pl.ds((cb + k) * dh, dh)])

        def tile_copies(step):
            rows = pl.ds(pl.multiple_of(step * TQ, TQ), TQ)
            return [pltpu.make_async_copy(st, dp_ref.at[rows, pl.ds((cb + k) * dh, dh)], tile_sems.at[n])
                    for n, (st, k) in enumerate(((st_a, 0), (st_b, 3)))]

        @pl.when(i > 0)
        def _():
            for cp in tile_copies(i - 1):
                cp.wait()

        @pl.when(i == 0)
        def _():
            kb_s[...] = k_ref[...].astype(bf16)
            vb_s[...] = v_ref[...].astype(bf16)
            dk_acc[...] = jnp.zeros_like(dk_acc)
            dv_acc[...] = jnp.zeros_like(dv_acc)

        s, gs = _silu_and_grad(z_ref[...])
        dy = dy_ref[...]
        st_b[...] = (dy * o_ref[...] * gs).astype(bf16)
        dob[...] = (dy * s).astype(bf16)
        qb[...] = q_ref[...].astype(bf16)
        p_ref[...] = jnp.zeros_like(p_ref)
        dq_acc[...] = jnp.zeros_like(dq_acc)
        nkb = (i + 1) * n_diag
        rows, cols, slot, upper, lower = _sb_tiles(i)

        def make_step(masked):
            def step(kb, carry):
                off = pl.multiple_of(kb * TK, TK)
                hs = range(N_HEADS)
                sls = [slice(h * HEAD_DIM, (h + 1) * HEAD_DIM) for h in hs]
                qs = [qb[:, sls[h]] for h in hs]
                ks = [kb_s[pl.ds(off, TK), sls[h]] for h in hs]
                dos = [dob[:, sls[h]] for h in hs]
                sc = [_sb_scores(qs[h], ks[h], kb, rows, cols, masked) for h in hs]
                da = [_dot(dos[h], vb_s[pl.ds(off, TK), sls[h]], NT) for h in hs]
                suf = [_suffix_sum(sc[h][2], upper) for h in hs]
                onehot = slot == kb
                cs = [jnp.sum(jnp.where(onehot, car_ref[h], 0.0), axis=1, keepdims=True) for h in hs]
                es = [jnp.exp(sc[h][1] + suf[h] + cs[h]) for h in hs]
                if masked:
                    es = [jnp.where(sc[h][0], es[h], 0.0) for h in hs]
                gs_ = [da[h] * es[h] for h in hs]
                ps = [p_ref[h] for h in hs]
                pre = [_suffix_sum(gs_[h], lower) + ps[h][:, :1] for h in hs]
                dzs = []
                for h in hs:
                    beta = jnp.exp(sc[h][1])
                    dzz = gs_[h] * (1.0 - beta) - beta * pre[h]
                    if masked:
                        dzz = jnp.where(sc[h][0], dzz, 0.0)
                    dzs.append((dzz * scale).astype(bf16))
                dqs = [_dot(dzs[h], ks[h]) for h in hs]
                dks = [_dot(dzs[h], qs[h], TN) for h in hs]
                dvs = [_dot(es[h].astype(bf16), dos[h], TN) for h in hs]
                for h in hs:
                    dq_acc[:, sls[h]] += dqs[h]
                    dk_acc[pl.ds(off, TK), sls[h]] += dks[h]
                    dv_acc[pl.ds(off, TK), sls[h]] += dvs[h]
                    p_ref[h] = ps[h] + jnp.sum(gs_[h], axis=1, keepdims=True)
                return carry
            return step

        lax.fori_loop(0, nkb - n_diag, make_step(False), 0)
        lax.fori_loop(nkb - n_diag, nkb, make_step(True), 0)
        st_a[...] = dq_acc[...].astype(bf16)
        for cp in tile_copies(i):
            cp.start()

        @pl.when(i == n_i - 1)
        def _():
            st_k[...] = dk_acc[...].astype(bf16)
            st_v[...] = dv_acc[...].astype(bf16)
            put(st_k, 0, S, 1)
            put(st_v, 0, S, 2)
            for cp in tile_copies(i):
                cp.wait()

    qspec = lambda k: pl.BlockSpec((TQ, dh), lambda i: (i, cb + k))
    kspec = lambda k: pl.BlockSpec((S, dh), lambda i: (0, cb + k))
    return pl.pallas_call(
        body, name="sb_bwd", grid=(n_i,),
        in_specs=[qspec(0), kspec(1), kspec(2), qspec(3),
                  pl.BlockSpec((TQ, dh), lambda i: (i, 0)),
                  pl.BlockSpec((N_HEADS, TQ, CARRY_LANES), lambda i: (0, i, 0)),
                  pl.BlockSpec((TQ, dh), lambda i: (i, A_GROUPS * HEAD_DIM // dh)),
                  pl.BlockSpec(memory_space=pl.ANY), pl.BlockSpec(memory_space=pl.ANY)],
        out_specs=pl.BlockSpec(memory_space=pl.ANY),
        out_shape=SDS(dproj_prev.shape, bf16),
        input_output_aliases={7: 0},
        scratch_shapes=[pltpu.VMEM((TQ, dh), bf16), pltpu.VMEM((S, dh), bf16), pltpu.VMEM((S, dh), bf16),
                        pltpu.VMEM((TQ, dh), bf16), pltpu.VMEM((N_HEADS, TQ, CARRY_LANES), f32), pltpu.VMEM((TQ, dh), f32),
                        pltpu.VMEM((S, dh), f32), pltpu.VMEM((S, dh), f32),
                        pltpu.VMEM((TQ, dh), bf16), pltpu.VMEM((TQ, dh), bf16),
                        pltpu.VMEM((S, dh), bf16), pltpu.VMEM((S, dh), bf16), pltpu.SemaphoreType.DMA((2,))],
        compiler_params=_params(56),
    )(proj, proj, proj, proj, o, car, dy, dproj_prev, after)


def _mem_kv(mem, mg, l, w_kv_g):
    M, D = mem.shape
    rb, n = w_kv_g.shape[1], w_kv_g.shape[2]

    def body(m_ref, g_ref, w_ref, kv_ref):
        mv = m_ref[...]
        r = lax.rsqrt(jnp.mean(mv * mv, axis=-1, keepdims=True) + EPS)
        mh = (mv * r * g_ref[...]).astype(bf16)
        kv_ref[...] = _dot(mh, w_ref[...].reshape(N_DEV * rb, n))

    return pl.pallas_call(
        body, name="mem_kv", grid=(1,),
        in_specs=[pl.BlockSpec((M, D), lambda i: (0, 0)), _layer_spec(mg, l),
                  pl.BlockSpec((N_DEV, rb, n), lambda i: (0, 0, 0))],
        out_specs=pl.BlockSpec((M, n), lambda i: (0, 0)),
        out_shape=SDS((M, n), f32), compiler_params=_params(),
    )(mem, mg, w_kv_g)


def _xattn_head(q_ref, kv_ref, qg, kg, h):
    dc = N_HEADS * HEAD_DIM
    sl = slice(h * HEAD_DIM, (h + 1) * HEAD_DIM)
    qh = q_ref[:, sl]
    rq = lax.rsqrt(jnp.mean(qh * qh, axis=-1, keepdims=True) + EPS)
    qhat = qh * rq
    qn = (qhat * qg).astype(bf16)
    kh = kv_ref[:, sl]
    rk = lax.rsqrt(jnp.mean(kh * kh, axis=-1, keepdims=True) + EPS)
    kn = (kh * rk * kg).astype(bf16)
    vh = kv_ref[:, dc + h * HEAD_DIM:dc + (h + 1) * HEAD_DIM].astype(bf16)
    s = _dot(qn, kn, NT) * (1.0 / math.sqrt(HEAD_DIM))
    e = jnp.exp(s - jnp.max(s, axis=-1, keepdims=True))
    p = e / jnp.sum(e, axis=-1, keepdims=True)
    o = _dot(p.astype(bf16), vh)
    return sl, rq, qhat, qn, kn, vh, p, o


def _xattn_fwd(proj, kv, qg, kg, l, y_prev, col0, tq=512):
    S = proj.shape[0]
    D = y_prev.shape[1]
    dc = N_HEADS * HEAD_DIM
    M = kv.shape[0]

    def body(q_ref, z_ref, kv_ref, qg_ref, kg_ref, yp_ref, y_ref):
        for h in range(N_HEADS):
            sl, _, _, _, _, _, _, o = _xattn_head(q_ref, kv_ref, qg_ref[...], kg_ref[...], h)
            z = z_ref[:, sl]
            y_ref[:, sl] = (o * (z * jax.nn.sigmoid(z))).astype(bf16)

    full = lambda shp: pl.BlockSpec(shp, lambda i: (0,) * len(shp))
    return pl.pallas_call(
        body, name="xattn_fwd", grid=(S // tq,),
        in_specs=[pl.BlockSpec((tq, dc), lambda i: (i, col0)), pl.BlockSpec((tq, dc), lambda i: (i, col0 + 1)),
                  full((M, 2 * dc)), _layer_spec(qg, l), _layer_spec(kg, l), pl.BlockSpec(memory_space=pl.ANY)],
        out_specs=pl.BlockSpec((tq, dc), lambda i: (i, D // dc - 1)),
        out_shape=SDS((S, D), bf16), input_output_aliases={5: 0}, compiler_params=_params(),
    )(proj, proj, kv, qg, kg, y_prev)


def _xattn_bwd(proj, kv, qg, kg, l, dy, dproj_prev, col0, tq=512):
    S = proj.shape[0]
    D = dy.shape[1]
    dc = N_HEADS * HEAD_DIM
    M = kv.shape[0]

    def body(q_ref, z_ref, kv_ref, qg_ref, kg_ref, dy_ref, dpp_ref, dp_ref, dkn_ref, dv_ref, dqg_ref):
        @pl.when(pl.program_id(0) == 0)
        def _():
            dkn_ref[...] = jnp.zeros_like(dkn_ref)
            dv_ref[...] = jnp.zeros_like(dv_ref)
            dqg_ref[...] = jnp.zeros_like(dqg_ref)

        qg = qg_ref[...]
        for h in range(N_HEADS):
            sl, rq, qhat, qn, kn, vh, p, o = _xattn_head(q_ref, kv_ref, qg, kg_ref[...], h)
            s, gs = _silu_and_grad(z_ref[:, sl])
            dyh = dy_ref[:, sl]
            dp_ref[:, dc + h * HEAD_DIM:dc + (h + 1) * HEAD_DIM] = (dyh * o * gs).astype(bf16)
            dob = (dyh * s).astype(bf16)
            dpr = _dot(dob, vh, NT)
            dv_ref[:, sl] += _dot(p.astype(bf16), dob, TN)
            ds = (p * (dpr - jnp.sum(p * dpr, axis=-1, keepdims=True)) * (1.0 / math.sqrt(HEAD_DIM))).astype(bf16)
            dqn = _dot(ds, kn)
            dkn_ref[:, sl] += _dot(ds, qn, TN)
            dqg_ref[...] += jnp.sum(dqn * qhat, axis=0, keepdims=True)
            dqhat = dqn * qg
            dp_ref[:, sl] = (rq * (dqhat - qhat * jnp.mean(dqhat * qhat, axis=-1, keepdims=True))).astype(bf16)

    full = lambda shp: pl.BlockSpec(shp, lambda i: (0,) * len(shp))
    return pl.pallas_call(
        body, name="xattn_bwd", grid=(S // tq,),
        in_specs=[pl.BlockSpec((tq, dc), lambda i: (i, col0)), pl.BlockSpec((tq, dc), lambda i: (i, col0 + 1)),
                  full((M, 2 * dc)), _layer_spec(qg, l), _layer_spec(kg, l),
                  pl.BlockSpec((tq, dc), lambda i: (i, D // dc - 1)), pl.BlockSpec(memory_space=pl.ANY)],
        out_specs=[pl.BlockSpec((tq, 2 * dc), lambda i: (i, col0 // 2)), full((M, dc)), full((M, dc)),
                   full((1, HEAD_DIM))],
        out_shape=[SDS(dproj_prev.shape, bf16), SDS((M, dc), f32), SDS((M, dc), f32), SDS((1, HEAD_DIM), f32)],
        input_output_aliases={6: 0}, compiler_params=_params(),
    )(proj, proj, kv, qg, kg, dy, dproj_prev)


def _mem_bwd(mem, mg, kg, l, kv, dkn, dv, w_kv_g):
    M, D = mem.shape
    rb, n = w_kv_g.shape[1], w_kv_g.shape[2]
    dc = n // 2

    def body(m_ref, g_ref, kv_ref, dkn_ref, dv_ref, kg_ref, w_ref, dw_ref, dmg_ref, dkg_ref, dkv_ref):
        mv = m_ref[...]
        r = lax.rsqrt(jnp.mean(mv * mv, axis=-1, keepdims=True) + EPS)
        mhat = mv * r
        mh = (mhat * g_ref[...]).astype(bf16)
        kg = kg_ref[...]
        dkg = jnp.zeros((1, HEAD_DIM), f32)
        for h in range(N_HEADS):
            sl = slice(h * HEAD_DIM, (h + 1) * HEAD_DIM)
            kh = kv_ref[:, sl]
            rk = lax.rsqrt(jnp.mean(kh * kh, axis=-1, keepdims=True) + EPS)
            khat = kh * rk
            dkn_h = dkn_ref[:, sl]
            dkg = dkg + jnp.sum(dkn_h * khat, axis=0, keepdims=True)
            dkhat = dkn_h * kg
            dkv_ref[:, sl] = (rk * (dkhat - khat * jnp.mean(dkhat * khat, axis=-1, keepdims=True))).astype(bf16)
        dkv_ref[:, dc:] = dv_ref[...].astype(bf16)
        dkg_ref[...] = dkg
        dkv = dkv_ref[...]
        dw_ref[...] = _dot(mh, dkv, TN).astype(bf16).reshape(N_DEV, rb, n)
        dmh = _dot(dkv, w_ref[...].reshape(N_DEV * rb, n), NT)
        dmg_ref[...] = jnp.sum(dmh * mhat, axis=0, keepdims=True)

    full = lambda shp: pl.BlockSpec(shp, lambda i: (0,) * len(shp))
    wspec = full((N_DEV, rb, n))
    return pl.pallas_call(
        body, name="mem_bwd", grid=(1,),
        in_specs=[full((M, D)), _layer_spec(mg, l), full((M, n)), full((M, dc)), full((M, dc)), _layer_spec(kg, l), wspec],
        out_specs=[wspec, full((1, D)), full((1, HEAD_DIM))],
        out_shape=[SDS((N_DEV, rb, n), bf16), SDS((1, D), f32), SDS((1, HEAD_DIM), f32)],
        scratch_shapes=[pltpu.VMEM((M, n), bf16)], compiler_params=_params(),
    )(mem, mg, kv, dkn, dv, kg, w_kv_g)


SMALL = ("norm_g", "sgu_ln_g", "sgu_ln_b", "sgu_w", "sgu_b", "mem_norm_g", "q_norm_g", "k_norm_g")


def _small_rows(like):
    rows = [math.prod(like[n].shape) // 128 for n in SMALL]
    offs = [0]
    for r in rows:
        offs.append(offs[-1] + -(-r // 8) * 8)
    return rows, offs


def _pack_small(parts, offs):
    pieces = []
    for k, n in enumerate(SMALL):
        a = parts[n].reshape(-1, 128)
        pieces.append(jnp.pad(a, ((0, offs[k + 1] - offs[k] - a.shape[0]), (0, 0))))
    return jnp.concatenate(pieces)


def kernel(x, mem, norm_g, w_in, sgu_ln_g, sgu_ln_b, sgu_w, sgu_b, mem_norm_g, w_mem_kv, q_norm_g, k_norm_g, w_out, loss_target, m_norm_g, m_w_in, m_sgu_ln_g, m_sgu_ln_b, m_sgu_w, m_sgu_b, m_mem_norm_g, m_w_mem_kv, m_q_norm_g, m_k_norm_g, m_w_out, v_norm_g, v_w_in, v_sgu_ln_g, v_sgu_ln_b, v_sgu_w, v_sgu_b, v_mem_norm_g, v_w_mem_kv, v_q_norm_g, v_k_norm_g, v_w_out):
    L, D, wc = w_in.shape
    S = x.shape[1]
    da = D // 2
    xs = x.reshape(S, D)
    mems = mem.reshape(mem.shape[1], D)
    tgt = loss_target.reshape(S, D)
    stacked = lambda a: a.reshape(a.shape[0], 1, -1)
    ng, lng, lnb, mg, qg, kg = map(stacked, (norm_g, sgu_ln_g, sgu_ln_b, mem_norm_g, q_norm_g, k_norm_g))
    b_t = jnp.swapaxes(sgu_b, 1, 2)
    sb_col, xa_col = 3 * da // HEAD_DIM, (3 * da + D) // (D // 4)

    ax, ay, ac = lax.axis_index("x"), lax.axis_index("y"), lax.axis_index("c")
    ids = jnp.stack([4 * ax + 2 * ay + ac, 2 * ax + ay, ac]).astype(jnp.int32)
    w_in0_b = _cast_into_slot("cast_w_in", w_in, 0, 512, ids, ids)
    first = _gather3_start("gather_w_in0", [w_in0_b], ids)
    late = first["token"]
    w_b = [(w_in0_b if l == 0 else _cast_into_slot("cast_w_in", w_in, l, 512, ids, late),
            _cast_into_slot("cast_w_kv", w_mem_kv, l, 256, ids, late),
            _cast_into_slot("cast_w_out", w_out, l, 256, ids, late)) for l in range(L)]
    h0 = _rms_h(xs, ng, 0, late)
    relay = _gather3_relay("gather_w_in0_relay",
                           _split_wait(first, h0, *[a for wl in w_b for a in wl if a is not w_in0_b]), ids)
    ox, oy = 4 * (1 - ax) + 2 * ay, 4 * ax + 2 * (1 - ay)
    od = 4 * (1 - ax) + 2 * (1 - ay)
    arrival = jnp.stack([ids[0], ids[0] + 1 - 2 * ac, ox + ac, oy + ac, od + ac,
                         ox + 1 - ac, oy + 1 - ac, od + 1 - ac]).astype(jnp.int32)
    proj0 = _proj_blocks(h0, relay["bufs"][0], arrival, 0, 4, None)
    in_fwd = _gather2_forward("gather_w_in0_forward", _split_wait(relay, proj0), ids)
    proj0 = _proj_blocks(h0, in_fwd["bufs"][0], arrival, 4, 1, proj0)

    acts = []
    xl = xs
    for l in range(L):
        (w_in_g,) = _split_wait(in_fwd, xl if l else proj0)
        rest = _gather3_start(f"gather_w_rest{l}", [w_b[l][1], w_b[l][2]], w_in_g)
        order = rest["token"]
        if l + 1 < L:
            nxt = _gather3_start(f"gather_w_in{l + 1}", [w_b[l + 1][0]], order)
            order = nxt["token"]
        if l == 0:
            proj, h = _proj_blocks(h0, w_in_g, arrival, 5, 3, proj0), h0
        else:
            proj, h = _rms_proj(xl, ng, l, w_in_g, order)
        rest_relay = _gather3_relay(f"gather_w_rest{l}_relay", _split_wait(rest, proj), proj)
        y = _sgu_fwd(proj, lng, lnb, sgu_w, b_t, l, rest_relay["token"])
        rest_fwd = _gather2_forward(f"gather_w_rest{l}_forward", _split_wait(rest_relay, y), y)
        order = rest_fwd["token"]
        if l + 1 < L:
            nxt_relay = _gather3_relay(f"gather_w_in{l + 1}_relay", _split_wait(nxt, order), order)
            order = nxt_relay["token"]
        y, o_b, car = _sb_fwd(proj, y, sb_col, order)
        w_kv_g, w_out_g = _split_wait(rest_fwd, o_b)
        order = o_b
        if l + 1 < L:
            in_fwd = _gather2_forward(f"gather_w_in{l + 1}_forward", _split_wait(nxt_relay, o_b), o_b)
            order = in_fwd["token"]
        kv = _mem_kv(mems, mg, l, w_kv_g)
        y = _xattn_fwd(proj, kv, qg, kg, l, y, xa_col)
        acts.append((xl, proj, h, y, o_b, car, kv, w_in_g, w_kv_g, w_out_g))
        if l + 1 < L:
            xl = _out_proj(xl, y, w_out_g, order)
        else:
            dx, dxb, loss_part = _out_proj_loss(xl, y, w_out_g, tgt)
    loss = lax.psum(loss_part[0, 0], ("x", "y", "c"))

    weights = dict(norm_g=norm_g, sgu_ln_g=sgu_ln_g, sgu_ln_b=sgu_ln_b, sgu_w=sgu_w, sgu_b=sgu_b,
                   mem_norm_g=mem_norm_g, q_norm_g=q_norm_g, k_norm_g=k_norm_g)
    moms_m = dict(norm_g=m_norm_g, sgu_ln_g=m_sgu_ln_g, sgu_ln_b=m_sgu_ln_b, sgu_w=m_sgu_w, sgu_b=m_sgu_b,
                  mem_norm_g=m_mem_norm_g, q_norm_g=m_q_norm_g, k_norm_g=m_k_norm_g)
    moms_v = dict(norm_g=v_norm_g, sgu_ln_g=v_sgu_ln_g, sgu_ln_b=v_sgu_ln_b, sgu_w=v_sgu_w, sgu_b=v_sgu_b,
                  mem_norm_g=v_mem_norm_g, q_norm_g=v_q_norm_g, k_norm_g=v_k_norm_g)
    small_rows, small_offs = _small_rows(weights)
    head_rows = D // 128
    assert SMALL[0] == "norm_g" and head_rows % 8 == 0

    seconds = {}
    pending = None
    adam = {"w_out": None, "w_mem_kv": None, "w_in": None}

    late_in = {}

    def update(lu, order):
        (r_out,) = _scatter_finish(seconds[f"g_out{lu}"], order)
        adam["w_out"] = _sum_adam("adam_w_out", r_out, w_out, m_w_out, v_w_out, lu, adam["w_out"], 128, ids, ids)
        r_kv, late_in[lu] = _scatter_finish(seconds[f"g_rest{lu}"], adam["w_out"][0])
        adam["w_mem_kv"] = _sum_adam("adam_w_kv", r_kv, w_mem_kv, m_w_mem_kv, v_w_mem_kv, lu, adam["w_mem_kv"], 256, ids, ids)
        return adam["w_mem_kv"][0]

    def update_in(lu, order):
        adam["w_in"] = _sum_adam("adam_w_in", late_in[lu], w_in, m_w_in, v_w_in, lu, adam["w_in"], 256, ids, order)
        return adam["w_in"][0]
    small = {n: [None] * L for n in SMALL}
    for l in reversed(range(L)):
        xl, proj, h, y, o_b, car, kv, w_in_g, w_kv_g, w_out_g = acts[l]
        dy = _out_bwd_dy(dxb, w_out_g)
        order = dy
        if pending is not None:
            seconds[pending[0]] = _scatter2_second_level(pending[0], pending[1], dy, ids)
            order = seconds[pending[0]]["token"]
        g_out = _tn_grad("out_bwd_dw", y, dxb, 512, 512, True, order)
        seconds[f"g_out{l}"] = _scatter1_start(f"scatter_g_out{l}", [g_out], ids)
        dproj, d_sw, d_sb, d_lg, d_lb = _sgu_bwd(proj, dy, lng, lnb, sgu_w, b_t, l, seconds[f"g_out{l}"]["token"])
        dproj = _sb_bwd(proj, o_b, car, dy, dproj, sb_col, d_lb)
        dproj, dkn, dv, d_qg = _xattn_bwd(proj, kv, qg, kg, l, dy, dproj, xa_col)
        g_kv, d_mg, d_kg = _mem_bwd(mems, mg, kg, l, kv, dkn, dv, w_kv_g)
        for n, val in (("sgu_ln_g", d_lg), ("sgu_ln_b", d_lb), ("sgu_w", d_sw), ("sgu_b", d_sb[:, :A_GROUPS].T),
                       ("mem_norm_g", d_mg), ("q_norm_g", d_qg), ("k_norm_g", d_kg)):
            small[n][l] = val.reshape(-1)
        order = d_kg
        if l == 0:
            small["norm_g"][0] = jnp.zeros_like(small["norm_g"][1])
            part = _pack_small({n: jnp.stack(small[n]) for n in SMALL}, small_offs)
            tail = _gather1_start("gather_small_tail", [_into_slot("small_tail_slot", part[head_rows:], ids)], ids)
            order = tail["token"]
        g_in_l = _tn_grad("in_bwd_dw", h, dproj, D, wc, False, order)
        first = _scatter2_pair_start(f"scatter_g_rest{l}_pair", [g_kv, g_in_l], ids)
        order = first["token"]
        pending = (f"g_rest{l}", first)
        if l == 0:
            for lu in reversed(range(1, L)):
                order = update(lu, order)
            seconds[pending[0]] = _scatter2_second_level(pending[0], pending[1], order, ids)
            order = seconds[pending[0]]["token"]
            for lu in reversed(range(1, L)):
                order = update_in(lu, order)
        dh = _in_bwd_dh(dproj, w_in_g, order)
        dx, dxb, d_ng = _rms_bwd(dh, xl, ng, l, dx, order)
        small["norm_g"][l] = d_ng.reshape(-1)
    head = _gather1_start("gather_small_head",
                          [_into_slot("small_head_slot", small["norm_g"][0].reshape(head_rows, 128), ids)], ids)

    (r_out,) = _scatter_finish(seconds["g_out0"], head["token"])
    adam["w_out"] = _sum_adam("adam_w_out", r_out, w_out, m_w_out, v_w_out, 0, adam["w_out"], 128, ids, ids)
    r_kv, r_in = _scatter_finish(seconds["g_rest0"], adam["w_out"][0])
    adam["w_mem_kv"] = _sum_adam("adam_w_kv", r_kv, w_mem_kv, m_w_mem_kv, v_w_mem_kv, 0, adam["w_mem_kv"], 256, ids, ids)
    adam["w_in"] = _sum_adam("adam_w_in", r_in, w_in, m_w_in, v_w_in, 0, adam["w_in"], 256, ids, ids)
    (r_tail,) = _split_wait(tail, adam["w_in"][0])
    (r_head,) = _split_wait(head, r_tail)
    as128 = lambda d: [d[n].reshape(-1, 128) for n in SMALL]
    sm = _small_sum_adam(r_head, r_tail, as128(weights), as128(moms_m), as128(moms_v), small_offs)
    res = dict(adam)
    for p, n in enumerate(SMALL):
        res[n] = [sm[k * len(SMALL) + p].reshape(weights[n].shape) for k in range(4)]

    order = ("norm_g", "w_in", "sgu_ln_g", "sgu_ln_b", "sgu_w", "sgu_b", "mem_norm_g", "w_mem_kv", "q_norm_g",
             "k_norm_g", "w_out")
    outs = [loss, dx.reshape(x.shape)]
    for k in range(4):
        outs += [res[n][k] for n in order]
    return tuple(outs)
```

```python
import functools
import math

import jax
import jax.numpy as jnp
from jax import lax
from jax.experimental import pallas as pl
from jax.experimental.pallas import tpu as pltpu

f32 = jnp.float32
bf16 = jnp.bfloat16
SDS = jax.ShapeDtypeStruct

N_DEV = 8
EPS = 1e-6
CHUNK = 128
SGU_CHUNKS = 2
A_GROUPS = 8
HEAD_DIM = 128
N_HEADS = 4
TQ = 256
TK = 256
CARRY_LANES = 128
ADAM_LR, ADAM_B1, ADAM_B2, ADAM_EPS, ADAM_WD, ADAM_STEP = 0.001, 0.9, 0.999, 1e-08, 0.01, 10
MIB = 1024 * 1024

NT = (((1,), (1,)), ((), ()))
TN = (((0,), (0,)), ((), ()))


def _params(vmem_mib=48):
    return pltpu.CompilerParams(vmem_limit_bytes=vmem_mib * MIB)


def _gelu_and_grad(x):
    e = lax.erf(x * (1.0 / math.sqrt(2.0)))
    cdf = 0.5 * (1.0 + e)
    pdf = jnp.exp(-0.5 * x * x) * (1.0 / math.sqrt(2.0 * math.pi))
    return x * cdf, cdf + x * pdf


def _gelu(x):
    return 0.5 * x * (1.0 + lax.erf(x * (1.0 / math.sqrt(2.0))))


def _silu_and_grad(z):
    sg = jax.nn.sigmoid(z)
    return z * sg, sg * (1.0 + z * (1.0 - sg))


def _layer_spec(stacked, l):
    rest = stacked.shape[1:]
    return pl.BlockSpec((None,) + rest, lambda *idx: (l,) + (0,) * len(rest))


def _dot(a, b, dims=None):
    if dims is None:
        return jnp.dot(a, b, preferred_element_type=f32)
    return lax.dot_general(a, b, dims, preferred_element_type=f32)


_HBM = pl.BlockSpec(memory_space=pltpu.HBM)
_SEM = pl.BlockSpec(memory_space=pltpu.SEMAPHORE)
_EFFECT = pltpu.SideEffectType.DATAFLOW_SIDE_EFFECTING


def _split_start(name, bufs, n_remote, n_local, build, after):
    nb = len(bufs)

    def body(*refs):
        token = refs[-1]
        locals_, remotes = build(refs[:nb], *refs[nb + 1:nb + 4])
        for cp in locals_ + remotes:
            cp.start()
        token[...] = jnp.zeros_like(token)

    hbm = lambda a: pltpu.with_memory_space_constraint(a, pltpu.HBM)
    outs = pl.pallas_call(
        body, name=name,
        out_shape=(pltpu.SemaphoreType.DMA((n_remote,)), pltpu.SemaphoreType.DMA((n_remote,)),
                   pltpu.SemaphoreType.DMA((max(n_local, 1),)),
                   *[pltpu.HBM(b.shape, b.dtype) for b in bufs], SDS((8, 128), f32)),
        in_specs=[_HBM] * nb + [pl.BlockSpec(memory_space=pl.ANY)],
        out_specs=(_SEM, _SEM, _SEM, *[_HBM] * nb, pl.BlockSpec(memory_space=pltpu.VMEM)),
        input_output_aliases={k: 3 + k for k in range(nb)},
        compiler_params=pltpu.CompilerParams(has_side_effects=_EFFECT),
    )(*[hbm(b) for b in bufs], after)
    return dict(name=name, build=build, sems=outs[:3], bufs=outs[3:3 + nb], token=outs[-1])


def _split_wait(handle, *after):
    build, bufs = handle["build"], handle["bufs"]
    nb = len(bufs)

    def body(*refs):
        locals_, remotes = build(refs[:nb], *refs[nb:nb + 3])
        for cp in remotes:
            cp.wait_recv()
        for cp in remotes:
            cp.wait_send()
        for cp in locals_:
            cp.wait()

    outs = pl.pallas_call(
        body, name=handle["name"] + "_wait",
        out_shape=tuple(pltpu.HBM(b.shape, b.dtype) for b in bufs),
        in_specs=[_HBM] * nb + [_SEM] * 3 + [pl.BlockSpec(memory_space=pl.ANY)] * len(after),
        out_specs=tuple([_HBM] * nb),
        input_output_aliases={k: k for k in range(nb)},
        compiler_params=pltpu.CompilerParams(has_side_effects=_EFFECT),
    )(*bufs, *handle["sems"], *after)
    return list(outs)


def _remote(src, dst, send_sems, recv_sems, k, to):
    return pltpu.make_async_remote_copy(src_ref=src, dst_ref=dst, send_sem=send_sems.at[k], recv_sem=recv_sems.at[k],
                                        device_id=to, device_id_type=pl.DeviceIdType.MESH)


def _other_chips(x, y):
    return [(1 - x, y), (x, 1 - y), (1 - x, 1 - y)]


def _all_peers(x, y, c):
    return [(1 - x if m & 4 else x, 1 - y if m & 2 else y, 1 - c if m & 1 else c) for m in range(1, N_DEV)]


def _gather1_start(name, lands, after):
    def build(refs, send, recv, loc):
        x, y, c = lax.axis_index("x"), lax.axis_index("y"), lax.axis_index("c")
        me = 4 * x + 2 * y + c
        return [], [_remote(d.at[me], d.at[me], send, recv, 7 * a + k, peer)
                    for a, d in enumerate(refs) for k, peer in enumerate(_all_peers(x, y, c))]

    return _split_start(name, list(lands), 7 * len(lands), 0, build, after)


def _scatter1_start(name, srcs, after):
    n = len(srcs)

    def build(refs, send, recv, loc):
        x, y, c = lax.axis_index("x"), lax.axis_index("y"), lax.axis_index("c")
        me = 4 * x + 2 * y + c
        return [], [_remote(refs[a].at[4 * px + 2 * py + pc], refs[n + a].at[me], send, recv, 7 * a + k, (px, py, pc))
                    for a in range(n) for k, (px, py, pc) in enumerate(_all_peers(x, y, c))]

    return _split_start(name, list(srcs) + [lax.empty(s.shape, s.dtype) for s in srcs], 7 * n, 0, build, after)


def _gather2_start(name, lands, after):
    def build(refs, send, recv, loc):
        x, y, c = lax.axis_index("x"), lax.axis_index("y"), lax.axis_index("c")
        me = 4 * x + 2 * y + c
        remotes = []
        for a, d in enumerate(refs):
            remotes.append(_remote(d.at[me], d.at[me], send, recv, 4 * a, (x, y, 1 - c)))
            remotes += [_remote(d.at[me], d.at[me], send, recv, 4 * a + 1 + k, (px, py, c))
                        for k, (px, py) in enumerate(_other_chips(x, y))]
        return [], remotes

    return _split_start(name, list(lands), 4 * len(lands), 0, build, after)


def _gather3_start(name, lands, after):
    def build(refs, send, recv, loc):
        x, y, c = lax.axis_index("x"), lax.axis_index("y"), lax.axis_index("c")
        me = 4 * x + 2 * y + c
        return [], [_remote(d.at[me], d.at[me], send, recv, 3 * a + k, to)
                    for a, d in enumerate(refs) for k, to in enumerate([(x, y, 1 - c), (1 - x, y, c), (x, 1 - y, c)])]

    return _split_start(name, list(lands), 3 * len(lands), 0, build, after)


def _gather3_relay(name, lands, after):
    def build(refs, send, recv, loc):
        x, y, c = lax.axis_index("x"), lax.axis_index("y"), lax.axis_index("c")
        from_x = c == 0
        slot = 4 * jnp.where(from_x, 1 - x, x) + 2 * jnp.where(from_x, y, 1 - y) + c
        to = (jnp.where(from_x, x, 1 - x), jnp.where(from_x, 1 - y, y), c)
        return [], [_remote(d.at[slot], d.at[slot], send, recv, a, to) for a, d in enumerate(refs)]

    return _split_start(name, list(lands), len(lands), 0, build, after)


def _gather2_forward(name, lands, after):
    n = len(lands)

    def build(refs, send, recv, loc):
        x, y, c = lax.axis_index("x"), lax.axis_index("y"), lax.axis_index("c")
        slots = [4 * px + 2 * py + c for px, py in _other_chips(x, y)]
        return [], [_remote(d.at[sl], d.at[sl], send, recv, 3 * a + k, (x, y, 1 - c))
                    for a, d in enumerate(refs) for k, sl in enumerate(slots)]

    return _split_start(name, list(lands), 3 * n, 0, build, after)


def _scatter2_pair_start(name, srcs, after):
    n = len(srcs)

    def build(refs, send, recv, loc):
        x, y, c = lax.axis_index("x"), lax.axis_index("y"), lax.axis_index("c")
        return [], [_remote(refs[a].at[2 * q + 1 - c], refs[n + a].at[q], send, recv, 4 * a + q, (x, y, 1 - c))
                    for a in range(n) for q in range(4)]

    lands = [lax.empty((4,) + s.shape[1:], s.dtype) for s in srcs]
    return _split_start(name, list(srcs) + lands, 4 * n, 0, build, after)


def _scatter2_chip_start(name, pairs, after):
    n = len(pairs)

    def build(refs, send, recv, loc):
        x, y, c = lax.axis_index("x"), lax.axis_index("y"), lax.axis_index("c")
        return [], [_remote(refs[a].at[2 * px + py], refs[n + a].at[2 * x + y], send, recv, 3 * a + k, (px, py, c))
                    for a in range(n) for k, (px, py) in enumerate(_other_chips(x, y))]

    return _split_start(name, list(pairs) + [lax.empty(p.shape, p.dtype) for p in pairs], 3 * n, 0, build, after)


def _pair_sum(name, src, theirs, ids):
    _, R, C = theirs.shape
    tr = min(R, 1024)

    def body(ids_ref, a_ref, b_ref, o_ref):
        o_ref[...] = (a_ref[...].astype(f32) + b_ref[...].astype(f32)).astype(bf16)

    spec = pl.BlockSpec((None, tr, C), lambda q, i, ids: (q, i, 0))
    return pl.pallas_call(
        body, name=name,
        grid_spec=pltpu.PrefetchScalarGridSpec(
            num_scalar_prefetch=1, grid=(4, R // tr),
            in_specs=[pl.BlockSpec((None, tr, C), lambda q, i, ids: (2 * q + ids[2], i, 0)), spec], out_specs=spec),
        out_shape=SDS(theirs.shape, bf16), compiler_params=_params(),
    )(ids, src, theirs)


def _scatter2_second_level(name, first, after, ids):
    outs = _split_wait(first, after)
    n = len(outs) // 2
    pairs = [_pair_sum(f"pair_sum_{name}{a}", outs[a], outs[n + a], ids) for a in range(n)]
    return _scatter2_chip_start(f"scatter_{name}_chip", pairs, ids)


def _scatter_finish(second, after):
    outs = _split_wait(second, after)
    n = len(outs) // 2
    return [(outs[a], outs[n + a]) for a in range(n)]


def _cast_into_slot(name, w, l, tr, ids, after):
    _, R, C = w.shape

    def body(ids_ref, w_ref, after_ref, o_ref):
        o_ref[...] = w_ref[...].astype(bf16)

    return pl.pallas_call(
        body, name=name,
        grid_spec=pltpu.PrefetchScalarGridSpec(
            num_scalar_prefetch=1, grid=(R // tr,),
            in_specs=[pl.BlockSpec((None, tr, C), lambda i, ids: (l, i, 0)), pl.BlockSpec(memory_space=pl.ANY)],
            out_specs=pl.BlockSpec((None, tr, C), lambda i, ids: (ids[0], i, 0))),
        out_shape=SDS((N_DEV, R, C), bf16), compiler_params=_params(),
    )(ids, w, after)


def _into_slot(name, a, ids):
    R, C = a.shape

    def body(ids_ref, a_ref, o_ref):
        o_ref[...] = a_ref[...]

    return pl.pallas_call(
        body, name=name,
        grid_spec=pltpu.PrefetchScalarGridSpec(
            num_scalar_prefetch=1, grid=(1,),
            in_specs=[pl.BlockSpec((R, C), lambda i, ids: (0, 0))],
            out_specs=pl.BlockSpec((None, R, C), lambda i, ids: (ids[0], 0, 0))),
        out_shape=SDS((N_DEV, R, C), f32), compiler_params=_params(),
    )(ids, a)


def _adam_math(w, g, m, v):
    m2 = ADAM_B1 * m + (1.0 - ADAM_B1) * g
    v2 = ADAM_B2 * v + (1.0 - ADAM_B2) * (g * g)
    m_hat = m2 / (1.0 - ADAM_B1 ** ADAM_STEP)
    v_hat = v2 / (1.0 - ADAM_B2 ** ADAM_STEP)
    delta = -ADAM_LR * (m_hat / (jnp.sqrt(v_hat) + ADAM_EPS) + ADAM_WD * w)
    return delta, m2, v2


def _sum_adam(name, pair_recv, w, m, v, l, prev, tr, ids, after):
    own, recv = pair_recv
    L, R, C = w.shape
    slots = recv.shape[0]
    mine = 0 if slots == N_DEV else 1

    def body(ids_ref, r_ref, own_ref, w_ref, m_ref, v_ref, after_ref, *rest):
        g_ref, d_ref, m2_ref, v2_ref = rest[-4:]
        terms = [jnp.where(ids_ref[mine] == q, own_ref[...], r_ref[q]).astype(f32) for q in range(slots)]
        g = terms[0]
        for t in terms[1:]:
            g = g + t
        d, m2, v2 = _adam_math(w_ref[...], g, m_ref[...], v_ref[...])
        g_ref[...] = g
        d_ref[...] = d
        m2_ref[...] = m2
        v2_ref[...] = v2

    wspec = pl.BlockSpec((None, tr, C), lambda i, ids: (l, i, 0))
    in_specs = [pl.BlockSpec((slots, tr, C), lambda i, ids: (0, i, 0)),
                pl.BlockSpec((None, tr, C), lambda i, ids: (ids[mine], i, 0)), wspec, wspec, wspec,
                pl.BlockSpec(memory_space=pl.ANY)]
    args = [ids, recv, own, w, m, v, after]
    aliases = {}
    if prev is not None:
        in_specs += [pl.BlockSpec(memory_space=pl.ANY)] * 4
        args += list(prev)
        aliases = {7 + k: k for k in range(4)}
    return pl.pallas_call(
        body, name=name,
        grid_spec=pltpu.PrefetchScalarGridSpec(num_scalar_prefetch=1, grid=(R // tr,), in_specs=in_specs,
                                               out_specs=[wspec] * 4),
        out_shape=[SDS((L, R, C), f32)] * 4, input_output_aliases=aliases, compiler_params=_params(),
    )(*args)


def _small_sum_adam(recv_head, recv_tail, ws, ms, vs, offs):
    n = len(ws)
    r0 = recv_head.shape[1]

    def body(*refs):
        rh, rt = refs[0], refs[1]
        w_refs, m_refs, v_refs = refs[2:2 + n], refs[2 + n:2 + 2 * n], refs[2 + 2 * n:2 + 3 * n]
        outs = refs[2 + 3 * n:]
        for p in range(n):
            lo, hi = offs[p], offs[p] + ws[p].shape[0]
            pieces = []
            if lo < r0:
                pieces.append((rh, lo, 0, min(hi, r0) - lo))
            if hi > r0:
                pieces.append((rt, max(lo, r0) - r0, max(lo, r0) - lo, hi - max(lo, r0)))
            for src, a, b, cnt in pieces:
                g = src[0, a:a + cnt, :]
                for s in range(1, N_DEV):
                    g = g + src[s, a:a + cnt, :]
                d, m2, v2 = _adam_math(w_refs[p][b:b + cnt, :], g, m_refs[p][b:b + cnt, :], v_refs[p][b:b + cnt, :])
                for k, val in enumerate((g, d, m2, v2)):
                    outs[k * n + p][b:b + cnt, :] = val

    return pl.pallas_call(
        body, name="small_sum_adam", out_shape=[SDS(w.shape, f32) for w in ws] * 4, compiler_params=_params(),
    )(recv_head, recv_tail, *ws, *ms, *vs)


def _rms_proj(x, g, l, w_in_g, after, tm=1024):
    S, D = x.shape
    wc = w_in_g.shape[2]
    n_out = N_DEV * wc

    def body(x_ref, g_ref, w_ref, after_ref, proj_ref, h_ref):
        @pl.when(pl.program_id(1) == 0)
        def _():
            xv = x_ref[...]
            r = lax.rsqrt(jnp.mean(xv * xv, axis=-1, keepdims=True) + EPS)
            h_ref[...] = (xv * r * g_ref[...]).astype(bf16)

        proj_ref[...] = _dot(h_ref[...], w_ref[...])

    return pl.pallas_call(
        body, name="rms_proj", grid=(S // tm, N_DEV),
        in_specs=[pl.BlockSpec((tm, D), lambda i, j: (i, 0)), _layer_spec(g, l),
                  pl.BlockSpec((None, D, wc), lambda i, j: (j, 0, 0)), pl.BlockSpec(memory_space=pl.ANY)],
        out_specs=[pl.BlockSpec((tm, wc), lambda i, j: (i, j)), pl.BlockSpec((tm, D), lambda i, j: (i, 0))],
        out_shape=[SDS((S, n_out), f32), SDS((S, D), bf16)], compiler_params=_params(),
    )(x, g, w_in_g, after)


def _out_proj(x, y, w_out_g, after, tm=512):
    S, D = x.shape
    rb = w_out_g.shape[1]

    def body(x_ref, y_ref, w_ref, after_ref, o_ref):
        w = w_ref[...].reshape(N_DEV * rb, D)
        o_ref[...] = x_ref[...] + _dot(y_ref[...], w)

    row = pl.BlockSpec((tm, D), lambda i: (i, 0))
    return pl.pallas_call(
        body, name="out_proj", grid=(S // tm,),
        in_specs=[row, row, pl.BlockSpec((N_DEV, rb, D), lambda i: (0, 0, 0)), pl.BlockSpec(memory_space=pl.ANY)],
        out_specs=row, out_shape=SDS((S, D), f32), compiler_params=_params(),
    )(x, y, w_out_g, after)


def _out_proj_loss(x, y, w_out_g, tgt, tm=256):
    S, D = x.shape
    rb = w_out_g.shape[1]

    def body(x_ref, y_ref, w_ref, t_ref, dx_ref, dxb_ref, l_ref):
        i = pl.program_id(0)
        w = w_ref[...].reshape(N_DEV * rb, D)
        d = x_ref[...] + _dot(y_ref[...], w) - t_ref[...]
        dx = d * (1.0 / D)
        dx_ref[...] = dx
        dxb_ref[...] = dx.astype(bf16)
        e = d * d
        part = e[:, 0:128]
        for k in range(1, D // 128):
            part = part + e[:, k * 128:(k + 1) * 128]
        part = jnp.sum(part.reshape(tm // 8, 8, 128), axis=0)

        @pl.when(i == 0)
        def _():
            l_ref[...] = jnp.zeros_like(l_ref)

        l_ref[...] += part

        @pl.when(i == pl.num_programs(0) - 1)
        def _():
            tot = jnp.sum(l_ref[...], axis=1, keepdims=True)
            tot = jnp.sum(tot, axis=0, keepdims=True)
            l_ref[...] = jnp.broadcast_to(tot * (0.5 / D), l_ref.shape)

    row = pl.BlockSpec((tm, D), lambda i: (i, 0))
    return pl.pallas_call(
        body, name="out_proj_loss", grid=(S // tm,),
        in_specs=[row, row, pl.BlockSpec((N_DEV, rb, D), lambda i: (0, 0, 0)), row],
        out_specs=[row, row, pl.BlockSpec((8, 128), lambda i: (0, 0))],
        out_shape=[SDS((S, D), f32), SDS((S, D), bf16), SDS((8, 128), f32)], compiler_params=_params(),
    )(x, y, w_out_g, tgt)


def _out_bwd_dy(dxb, w_out_g, tm=512):
    S, D = dxb.shape
    rb = w_out_g.shape[1]

    nb = 2

    def body(dx_ref, w_ref, o_ref):
        o_ref[...] = _dot(dx_ref[...], w_ref[...].reshape(nb * rb, D), NT)

    return pl.pallas_call(
        body, name="out_bwd_dy", grid=(S // tm, N_DEV // nb),
        in_specs=[pl.BlockSpec((tm, D), lambda i, j: (i, 0)),
                  pl.BlockSpec((nb, rb, D), lambda i, j: (j, 0, 0))],
        out_specs=pl.BlockSpec((tm, nb * rb), lambda i, j: (i, j)),
        out_shape=SDS((S, D), f32), compiler_params=_params(),
    )(dxb, w_out_g)


def _tn_grad(name, a, b, tm, tn, rows_major, after):
    S, M = a.shape
    N = b.shape[1]
    if rows_major:
        rb = M // N_DEV
        nb = tm // rb
        out_shape = SDS((N_DEV, rb, N), bf16)
        out_spec = pl.BlockSpec((nb, rb, tn), lambda i, j: (i, 0, j))
    else:
        out_shape = SDS((N_DEV, M, N // N_DEV), bf16)
        assert tn == N // N_DEV
        out_spec = pl.BlockSpec((None, tm, tn), lambda i, j: (j, i, 0))

    def body(a_ref, b_ref, after_ref, o_ref):
        o_ref[...] = _dot(a_ref[...], b_ref[...], TN).astype(bf16).reshape(o_ref.shape)

    return pl.pallas_call(
        body, name=name, grid=(M // tm, N // tn),
        in_specs=[pl.BlockSpec((S, tm), lambda i, j: (0, i)), pl.BlockSpec((S, tn), lambda i, j: (0, j)),
                  pl.BlockSpec(memory_space=pl.ANY)],
        out_specs=out_spec, out_shape=out_shape, compiler_params=_params(),
    )(a, b, after)


def _in_bwd_dh(dproj, w_in_g, after, tm=1024, tn=256):
    S = dproj.shape[0]
    _, D, wc = w_in_g.shape
    tm = min(tm, S)

    def body(dp_ref, w_ref, after_ref, o_ref):
        acc = _dot(dp_ref[:, 0:wc], w_ref[0], NT)
        for k in range(1, N_DEV):
            acc = acc + _dot(dp_ref[:, k * wc:(k + 1) * wc], w_ref[k], NT)
        o_ref[...] = acc

    return pl.pallas_call(
        body, name="in_bwd_dh", grid=(S // tm, D // tn),
        in_specs=[pl.BlockSpec((tm, N_DEV * wc), lambda i, j: (i, 0)),
                  pl.BlockSpec((N_DEV, tn, wc), lambda i, j: (0, j, 0)), pl.BlockSpec(memory_space=pl.ANY)],
        out_specs=pl.BlockSpec((tm, tn), lambda i, j: (i, j)),
        out_shape=SDS((S, D), f32), compiler_params=_params(),
    )(dproj, w_in_g, after)


def _rms_bwd(dh, x, g, l, dx_next, after, tm=256):
    S, D = x.shape

    def body(dh_ref, x_ref, g_ref, dxn_ref, after_ref, dx_ref, dxb_ref, dg_ref):
        @pl.when(pl.program_id(0) == 0)
        def _():
            dg_ref[...] = jnp.zeros_like(dg_ref)

        dh = dh_ref[...]
        xv = x_ref[...]
        r = lax.rsqrt(jnp.mean(xv * xv, axis=-1, keepdims=True) + EPS)
        xhat = xv * r
        dxhat = dh * g_ref[...]
        dx = r * (dxhat - xhat * jnp.mean(dxhat * xhat, axis=-1, keepdims=True)) + dxn_ref[...]
        dx_ref[...] = dx
        dxb_ref[...] = dx.astype(bf16)
        dg_ref[...] += jnp.sum(dh * xhat, axis=0, keepdims=True)

    row = pl.BlockSpec((tm, D), lambda i: (i, 0))
    vec = pl.BlockSpec((1, D), lambda i: (0, 0))
    return pl.pallas_call(
        body, name="rms_bwd", grid=(S // tm,),
        in_specs=[row, row, _layer_spec(g, l), row, pl.BlockSpec(memory_space=pl.ANY)], out_specs=[row, row, vec],
        out_shape=[SDS((S, D), f32), SDS((S, D), bf16), SDS((1, D), f32)], compiler_params=_params(),
    )(dh, x, g, dx_next, after)


def _sgu_fwd(proj, ln_g, ln_b, w_s, b_t, l, after):
    S = proj.shape[0]
    da = A_GROUPS * HEAD_DIM
    D = 2 * da

    def body(u_ref, v_ref, z_ref, lg_ref, lb_ref, w_ref, bt_ref, after_ref, y_ref):
        u = _gelu(u_ref[...])
        v = _gelu(v_ref[...])
        z = z_ref[...]
        mu = jnp.mean(v, axis=-1, keepdims=True)
        xc = v - mu
        rs = lax.rsqrt(jnp.mean(xc * xc, axis=-1, keepdims=True) + EPS)
        vn = (xc * rs * lg_ref[...] + lb_ref[...]).astype(bf16)
        gate = u * (z * jax.nn.sigmoid(z))
        tri = lax.broadcasted_iota(jnp.int32, (CHUNK, CHUNK), 0) >= lax.broadcasted_iota(jnp.int32, (CHUNK, CHUNK), 1)
        for g in range(A_GROUPS):
            sl = slice(g * HEAD_DIM, (g + 1) * HEAD_DIM)
            wm = jnp.where(tri, w_ref[g], 0.0).astype(bf16)
            for ck in range(SGU_CHUNKS):
                rw = slice(ck * CHUNK, (ck + 1) * CHUNK)
                mixed = _dot(wm, vn[rw, sl]) + bt_ref[:, g:g + 1]
                y_ref[rw, sl] = (gate[rw, sl] * mixed).astype(bf16)

    blk = lambda cb: pl.BlockSpec((SGU_CHUNKS * CHUNK, da), lambda c: (c, cb))
    full = lambda shp: pl.BlockSpec(shp, lambda c: (0,) * len(shp))
    return pl.pallas_call(
        body, name="sgu_fwd", grid=(S // (SGU_CHUNKS * CHUNK),),
        in_specs=[blk(0), blk(1), blk(2), _layer_spec(ln_g, l), _layer_spec(ln_b, l), _layer_spec(w_s, l),
                  _layer_spec(b_t, l), pl.BlockSpec(memory_space=pl.ANY)],
        out_specs=blk(0), out_shape=SDS((S, D), bf16), compiler_params=_params(),
    )(proj, proj, proj, ln_g, ln_b, w_s, b_t, after)


def _sgu_bwd(proj, dy, ln_g, ln_b, w_s, b_t, l, after):
    S = proj.shape[0]
    da = A_GROUPS * HEAD_DIM
    n_proj = proj.shape[1]

    def body(u_ref, v_ref, z_ref, dy_ref, lg_ref, lb_ref, w_ref, bt_ref, after_ref,
             dp_ref, dw_ref, db_ref, dlg_ref, dlb_ref, dvn_ref):
        @pl.when(pl.program_id(0) == 0)
        def _():
            dw_ref[...] = jnp.zeros_like(dw_ref)
            db_ref[...] = jnp.zeros_like(db_ref)
            dlg_ref[...] = jnp.zeros_like(dlg_ref)
            dlb_ref[...] = jnp.zeros_like(dlb_ref)

        up, vp, z, dy = u_ref[...], v_ref[...], z_ref[...], dy_ref[...]
        u, gu = _gelu_and_grad(up)
        v, gv = _gelu_and_grad(vp)
        s, gs = _silu_and_grad(z)
        mu = jnp.mean(v, axis=-1, keepdims=True)
        xc = v - mu
        rs = lax.rsqrt(jnp.mean(xc * xc, axis=-1, keepdims=True) + EPS)
        vhat = xc * rs
        lg = lg_ref[...]
        vn = (vhat * lg + lb_ref[...]).astype(bf16)
        tri = lax.broadcasted_iota(jnp.int32, (CHUNK, CHUNK), 0) >= lax.broadcasted_iota(jnp.int32, (CHUNK, CHUNK), 1)
        lane = lax.broadcasted_iota(jnp.int32, (CHUNK, HEAD_DIM), 1)
        dys = dy * s
        db = jnp.zeros((CHUNK, HEAD_DIM), f32)
        for g in range(A_GROUPS):
            sl = slice(g * HEAD_DIM, (g + 1) * HEAD_DIM)
            zl = slice(2 * da + g * HEAD_DIM, 2 * da + (g + 1) * HEAD_DIM)
            wm = jnp.where(tri, w_ref[g], 0.0).astype(bf16)
            for ck in range(SGU_CHUNKS):
                rw = slice(ck * CHUNK, (ck + 1) * CHUNK)
                mixed = _dot(wm, vn[rw, sl]) + bt_ref[:, g:g + 1]
                dmix = dys[rw, sl] * u[rw, sl]
                dp_ref[rw, sl] = (dys[rw, sl] * mixed * gu[rw, sl]).astype(bf16)
                dp_ref[rw, zl] = (dy[rw, sl] * u[rw, sl] * mixed * gs[rw, sl]).astype(bf16)
                dmb = dmix.astype(bf16)
                dw_ref[g] += jnp.where(tri, _dot(dmb, vn[rw, sl], NT), 0.0)
                dvn_ref[rw, sl] = _dot(wm, dmb, TN)
                db = db + jnp.where(lane == g, jnp.sum(dmix, axis=1, keepdims=True), 0.0)
        db_ref[...] += db
        dvn = dvn_ref[...]
        dlg_ref[...] += jnp.sum(dvn * vhat, axis=0, keepdims=True)
        dlb_ref[...] += jnp.sum(dvn, axis=0, keepdims=True)
        dvhat = dvn * lg
        dv = rs * (dvhat - jnp.mean(dvhat, axis=-1, keepdims=True)
                   - vhat * jnp.mean(dvhat * vhat, axis=-1, keepdims=True))
        dp_ref[:, da:2 * da] = (dv * gv).astype(bf16)

    rows = SGU_CHUNKS * CHUNK
    blk = lambda cb: pl.BlockSpec((rows, da), lambda c: (c, cb))
    full = lambda shp: pl.BlockSpec(shp, lambda c: (0,) * len(shp))
    return pl.pallas_call(
        body, name="sgu_bwd", grid=(S // rows,),
        in_specs=[blk(0), blk(1), blk(2), blk(0), _layer_spec(ln_g, l), _layer_spec(ln_b, l), _layer_spec(w_s, l),
                  _layer_spec(b_t, l), pl.BlockSpec(memory_space=pl.ANY)],
        out_specs=[pl.BlockSpec((rows, 3 * da), lambda c: (c, 0)), full((A_GROUPS, CHUNK, CHUNK)),
                   full((CHUNK, HEAD_DIM)), full((1, da)), full((1, da))],
        out_shape=[SDS((S, n_proj), bf16), SDS((A_GROUPS, CHUNK, CHUNK), f32), SDS((CHUNK, HEAD_DIM), f32),
                   SDS((1, da), f32), SDS((1, da), f32)],
        scratch_shapes=[pltpu.VMEM((rows, da), f32)], compiler_params=_params(),
    )(proj, proj, proj, dy, ln_g, ln_b, w_s, b_t, after)


def _sb_scores(q, kblk, kb, rows, cols, masked):
    z = _dot(q, kblk, NT) * (1.0 / math.sqrt(HEAD_DIM))
    t = jnp.log(1.0 + jnp.exp(-jnp.abs(z)))
    log_1mb = -(jnp.maximum(z, 0.0) + t)
    log_beta = jnp.minimum(z, 0.0) - t
    if not masked:
        return None, log_beta, log_1mb
    causal = (cols + kb * TK) < rows
    return causal, log_beta, jnp.where(causal, log_1mb, 0.0)


def _sb_tiles(i):
    rows = i * TQ + lax.broadcasted_iota(jnp.int32, (TQ, TK), 0)
    cols = lax.broadcasted_iota(jnp.int32, (TQ, TK), 1)
    r_i = lax.broadcasted_iota(jnp.int32, (TK, TK), 0)
    c_i = lax.broadcasted_iota(jnp.int32, (TK, TK), 1)
    upper, lower = (r_i > c_i).astype(bf16), (r_i < c_i).astype(bf16)
    slot = lax.broadcasted_iota(jnp.int32, (TQ, CARRY_LANES), 1)
    return rows, cols, slot, jnp.concatenate([upper, upper], axis=0), jnp.concatenate([lower, lower], axis=0)


def _suffix_sum(t, tri):
    hi = lax.bitcast_convert_type(lax.bitcast_convert_type(t, jnp.uint32) & jnp.uint32(0xFFFF0000), f32)
    both = jnp.concatenate([hi.astype(bf16), (t - hi).astype(bf16)], axis=1)
    return _dot(both, tri)


def _sb_fwd(proj, y_prev, col0, after):
    S = proj.shape[0]
    D = y_prev.shape[1]
    dh = N_HEADS * HEAD_DIM
    n_diag = TQ // TK

    def body(q_ref, k_ref, v_ref, z_ref, yp_ref, after_ref, y_ref, o_ref, car_ref, qb, kb_s, vb_s, c_ref):
        i = pl.program_id(0)

        @pl.when(i == 0)
        def _():
            kb_s[...] = k_ref[...].astype(bf16)
            vb_s[...] = v_ref[...].astype(bf16)

        qb[...] = q_ref[...].astype(bf16)
        o_ref[...] = jnp.zeros_like(o_ref)
        c_ref[...] = jnp.zeros_like(c_ref)
        car_ref[...] = jnp.zeros_like(car_ref)
        nkb = (i + 1) * n_diag
        rows, cols, slot, upper, _ = _sb_tiles(i)

        def make_step(masked):
            def step(jj, carry):
                kb = nkb - 1 - jj
                off = pl.multiple_of(kb * TK, TK)
                hs = range(N_HEADS)
                sls = [slice(h * HEAD_DIM, (h + 1) * HEAD_DIM) for h in hs]
                sc = [_sb_scores(qb[:, sls[h]], kb_s[pl.ds(off, TK), sls[h]], kb, rows, cols, masked) for h in hs]
                suf = [_suffix_sum(sc[h][2], upper) for h in hs]
                cs = [c_ref[h] for h in hs]
                es = [jnp.exp(sc[h][1] + suf[h] + cs[h][:, :1]) for h in hs]
                if masked:
                    es = [jnp.where(sc[h][0], es[h], 0.0) for h in hs]
                pv = [_dot(es[h].astype(bf16), vb_s[pl.ds(off, TK), sls[h]]) for h in hs]
                for h in hs:
                    o_ref[:, sls[h]] += pv[h]
                    car_ref[h] = jnp.where(slot == kb, cs[h], car_ref[h])
                    c_ref[h] = cs[h] + jnp.sum(sc[h][2], axis=1, keepdims=True)
                return carry
            return step

        lax.fori_loop(0, n_diag, make_step(True), 0)
        lax.fori_loop(n_diag, nkb, make_step(False), 0)
        z = z_ref[...]
        y_ref[...] = (o_ref[...] * (z * jax.nn.sigmoid(z))).astype(bf16)

    cb = col0 * HEAD_DIM // dh
    qspec = lambda k: pl.BlockSpec((TQ, dh), lambda i: (i, cb + k))
    kspec = lambda k: pl.BlockSpec((S, dh), lambda i: (0, cb + k))
    return pl.pallas_call(
        body, name="sb_fwd", grid=(S // TQ,),
        in_specs=[qspec(0), kspec(1), kspec(2), qspec(3), pl.BlockSpec(memory_space=pl.ANY),
                  pl.BlockSpec(memory_space=pl.ANY)],
        out_specs=[pl.BlockSpec((TQ, dh), lambda i: (i, A_GROUPS * HEAD_DIM // dh)),
                   pl.BlockSpec((TQ, dh), lambda i: (i, 0)),
                   pl.BlockSpec((N_HEADS, TQ, CARRY_LANES), lambda i: (0, i, 0))],
        out_shape=[SDS((S, D), bf16), SDS((S, dh), f32), SDS((N_HEADS, S, CARRY_LANES), f32)],
        input_output_aliases={4: 0},
        scratch_shapes=[pltpu.VMEM((TQ, dh), bf16), pltpu.VMEM((S, dh), bf16), pltpu.VMEM((S, dh), bf16),
                        pltpu.VMEM((N_HEADS, TQ, CARRY_LANES), f32)],
        compiler_params=_params(),
    )(proj, proj, proj, proj, y_prev, after)


def _sb_bwd(proj, o, car, dy, dproj_prev, col0, after):
    S = proj.shape[0]
    n_i = S // TQ
    dh = N_HEADS * HEAD_DIM
    n_diag = TQ // TK
    cb = col0 * HEAD_DIM // dh
    scale = 1.0 / math.sqrt(HEAD_DIM)

    def body(q_ref, k_ref, v_ref, z_ref, o_ref, car_ref, dy_ref, dpp_ref, after_ref,
             dp_ref, qb, kb_s, vb_s, dob, p_ref, dq_acc, dk_acc, dv_acc, st_a, st_b, st_k, st_v, tile_sems):
        i = pl.program_id(0)

        def put(stage_ref, row0, nrows, k):
            pltpu.sync_copy(stage_ref, dp_ref.at[pl.ds(row0, nrows), pl.ds((cb + k) * dh, dh)])

        def tile_copies(step):
            rows = pl.ds(pl.multiple_of(step * TQ, TQ), TQ)
            return [pltpu.make_async_copy(st, dp_ref.at[rows, pl.ds((cb + k) * dh, dh)], tile_sems.at[n])
                    for n, (st, k) in enumerate(((st_a, 0), (st_b, 3)))]

        @pl.when(i > 0)
        def _():
            for cp in tile_copies(i - 1):
                cp.wait()

        @pl.when(i == 0)
        def _():
            kb_s[...] = k_ref[...].astype(bf16)
            vb_s[...] = v_ref[...].astype(bf16)
            dk_acc[...] = jnp.zeros_like(dk_acc)
            dv_acc[...] = jnp.zeros_like(dv_acc)

        s, gs = _silu_and_grad(z_ref[...])
        dy = dy_ref[...]
        st_b[...] = (dy * o_ref[...] * gs).astype(bf16)
        dob[...] = (dy * s).astype(bf16)
        qb[...] = q_ref[...].astype(bf16)
        p_ref[...] = jnp.zeros_like(p_ref)
        dq_acc[...] = jnp.zeros_like(dq_acc)
        nkb = (i + 1) * n_diag
        rows, cols, slot, upper, lower = _sb_tiles(i)

        def make_step(masked):
            def step(kb, carry):
                off = pl.multiple_of(kb * TK, TK)
                hs = range(N_HEADS)
                sls = [slice(h * HEAD_DIM, (h + 1) * HEAD_DIM) for h in hs]
                qs = [qb[:, sls[h]] for h in hs]
                ks = [kb_s[pl.ds(off, TK), sls[h]] for h in hs]
                dos = [dob[:, sls[h]] for h in hs]
                sc = [_sb_scores(qs[h], ks[h], kb, rows, cols, masked) for h in hs]
                da = [_dot(dos[h], vb_s[pl.ds(off, TK), sls[h]], NT) for h in hs]
                suf = [_suffix_sum(sc[h][2], upper) for h in hs]
                onehot = slot == kb
                cs = [jnp.sum(jnp.where(onehot, car_ref[h], 0.0), axis=1, keepdims=True) for h in hs]
                es = [jnp.exp(sc[h][1] + suf[h] + cs[h]) for h in hs]
                if masked:
                    es = [jnp.where(sc[h][0], es[h], 0.0) for h in hs]
                gs_ = [da[h] * es[h] for h in hs]
                ps = [p_ref[h] for h in hs]
                pre = [_suffix_sum(gs_[h], lower) + ps[h][:, :1] for h in hs]
                dzs = []
                for h in hs:
                    beta = jnp.exp(sc[h][1])
                    dzz = gs_[h] * (1.0 - beta) - beta * pre[h]
                    if masked:
                        dzz = jnp.where(sc[h][0], dzz, 0.0)
                    dzs.append((dzz * scale).astype(bf16))
                dqs = [_dot(dzs[h], ks[h]) for h in hs]
                dks = [_dot(dzs[h], qs[h], TN) for h in hs]
                dvs = [_dot(es[h].astype(bf16), dos[h], TN) for h in hs]
                for h in hs:
                    dq_acc[:, sls[h]] += dqs[h]
                    dk_acc[pl.ds(off, TK), sls[h]] += dks[h]
                    dv_acc[pl.ds(off, TK), sls[h]] += dvs[h]
                    p_ref[h] = ps[h] + jnp.sum(gs_[h], axis=1, keepdims=True)
                return carry
            return step

        lax.fori_loop(0, nkb - n_diag, make_step(False), 0)
        lax.fori_loop(nkb - n_diag, nkb, make_step(True), 0)
        st_a[...] = dq_acc[...].astype(bf16)
        for cp in tile_copies(i):
            cp.start()

        @pl.when(i == n_i - 1)
        def _():
            st_k[...] = dk_acc[...].astype(bf16)
            st_v[...] = dv_acc[...].astype(bf16)
            put(st_k, 0, S, 1)
            put(st_v, 0, S, 2)
            for cp in tile_copies(i):
                cp.wait()

    qspec = lambda k: pl.BlockSpec((TQ, dh), lambda i: (i, cb + k))
    kspec = lambda k: pl.BlockSpec((S, dh), lambda i: (0, cb + k))
    return pl.pallas_call(
        body, name="sb_bwd", grid=(n_i,),
        in_specs=[qspec(0), kspec(1), kspec(2), qspec(3),
                  pl.BlockSpec((TQ, dh), lambda i: (i, 0)),
                  pl.BlockSpec((N_HEADS, TQ, CARRY_LANES), lambda i: (0, i, 0)),
                  pl.BlockSpec((TQ, dh), lambda i: (i, A_GROUPS * HEAD_DIM // dh)),
                  pl.BlockSpec(memory_space=pl.ANY), pl.BlockSpec(memory_space=pl.ANY)],
        out_specs=pl.BlockSpec(memory_space=pl.ANY),
        out_shape=SDS(dproj_prev.shape, bf16),
        input_output_aliases={7: 0},
        scratch_shapes=[pltpu.VMEM((TQ, dh), bf16), pltpu.VMEM((S, dh), bf16), pltpu.VMEM((S, dh), bf16),
                        pltpu.VMEM((TQ, dh), bf16), pltpu.VMEM((N_HEADS, TQ, CARRY_LANES), f32), pltpu.VMEM((TQ, dh), f32),
                        pltpu.VMEM((S, dh), f32), pltpu.VMEM((S, dh), f32),
                        pltpu.VMEM((TQ, dh), bf16), pltpu.VMEM((TQ, dh), bf16),
                        pltpu.VMEM((S, dh), bf16), pltpu.VMEM((S, dh), bf16), pltpu.SemaphoreType.DMA((2,))],
        compiler_params=_params(56),
    )(proj, proj, proj, proj, o, car, dy, dproj_prev, after)


def _mem_kv(mem, mg, l, w_kv_g):
    M, D = mem.shape
    rb, n = w_kv_g.shape[1], w_kv_g.shape[2]

    def body(m_ref, g_ref, w_ref, kv_ref):
        mv = m_ref[...]
        r = lax.rsqrt(jnp.mean(mv * mv, axis=-1, keepdims=True) + EPS)
        mh = (mv * r * g_ref[...]).astype(bf16)
        kv_ref[...] = _dot(mh, w_ref[...].reshape(N_DEV * rb, n))

    return pl.pallas_call(
        body, name="mem_kv", grid=(1,),
        in_specs=[pl.BlockSpec((M, D), lambda i: (0, 0)), _layer_spec(mg, l),
                  pl.BlockSpec((N_DEV, rb, n), lambda i: (0, 0, 0))],
        out_specs=pl.BlockSpec((M, n), lambda i: (0, 0)),
        out_shape=SDS((M, n), f32), compiler_params=_params(),
    )(mem, mg, w_kv_g)


def _xattn_head(q_ref, kv_ref, qg, kg, h):
    dc = N_HEADS * HEAD_DIM
    sl = slice(h * HEAD_DIM, (h + 1) * HEAD_DIM)
    qh = q_ref[:, sl]
    rq = lax.rsqrt(jnp.mean(qh * qh, axis=-1, keepdims=True) + EPS)
    qhat = qh * rq
    qn = (qhat * qg).astype(bf16)
    kh = kv_ref[:, sl]
    rk = lax.rsqrt(jnp.mean(kh * kh, axis=-1, keepdims=True) + EPS)
    kn = (kh * rk * kg).astype(bf16)
    vh = kv_ref[:, dc + h * HEAD_DIM:dc + (h + 1) * HEAD_DIM].astype(bf16)
    s = _dot(qn, kn, NT) * (1.0 / math.sqrt(HEAD_DIM))
    e = jnp.exp(s - jnp.max(s, axis=-1, keepdims=True))
    p = e / jnp.sum(e, axis=-1, keepdims=True)
    o = _dot(p.astype(bf16), vh)
    return sl, rq, qhat, qn, kn, vh, p, o


def _xattn_fwd(proj, kv, qg, kg, l, y_prev, col0, tq=512):
    S = proj.shape[0]
    D = y_prev.shape[1]
    dc = N_HEADS * HEAD_DIM
    M = kv.shape[0]

    def body(q_ref, z_ref, kv_ref, qg_ref, kg_ref, yp_ref, y_ref):
        for h in range(N_HEADS):
            sl, _, _, _, _, _, _, o = _xattn_head(q_ref, kv_ref, qg_ref[...], kg_ref[...], h)
            z = z_ref[:, sl]
            y_ref[:, sl] = (o * (z * jax.nn.sigmoid(z))).astype(bf16)

    full = lambda shp: pl.BlockSpec(shp, lambda i: (0,) * len(shp))
    return pl.pallas_call(
        body, name="xattn_fwd", grid=(S // tq,),
        in_specs=[pl.BlockSpec((tq, dc), lambda i: (i, col0)), pl.BlockSpec((tq, dc), lambda i: (i, col0 + 1)),
                  full((M, 2 * dc)), _layer_spec(qg, l), _layer_spec(kg, l), pl.BlockSpec(memory_space=pl.ANY)],
        out_specs=pl.BlockSpec((tq, dc), lambda i: (i, D // dc - 1)),
        out_shape=SDS((S, D), bf16), input_output_aliases={5: 0}, compiler_params=_params(),
    )(proj, proj, kv, qg, kg, y_prev)


def _xattn_bwd(proj, kv, qg, kg, l, dy, dproj_prev, col0, tq=512):
    S = proj.shape[0]
    D = dy.shape[1]
    dc = N_HEADS * HEAD_DIM
    M = kv.shape[0]

    def body(q_ref, z_ref, kv_ref, qg_ref, kg_ref, dy_ref, dpp_ref, dp_ref, dkn_ref, dv_ref, dqg_ref):
        @pl.when(pl.program_id(0) == 0)
        def _():
            dkn_ref[...] = jnp.zeros_like(dkn_ref)
            dv_ref[...] = jnp.zeros_like(dv_ref)
            dqg_ref[...] = jnp.zeros_like(dqg_ref)

        qg = qg_ref[...]
        for h in range(N_HEADS):
            sl, rq, qhat, qn, kn, vh, p, o = _xattn_head(q_ref, kv_ref, qg, kg_ref[...], h)
            s, gs = _silu_and_grad(z_ref[:, sl])
            dyh = dy_ref[:, sl]
            dp_ref[:, dc + h * HEAD_DIM:dc + (h + 1) * HEAD_DIM] = (dyh * o * gs).astype(bf16)
            dob = (dyh * s).astype(bf16)
            dpr = _dot(dob, vh, NT)
            dv_ref[:, sl] += _dot(p.astype(bf16), dob, TN)
            ds = (p * (dpr - jnp.sum(p * dpr, axis=-1, keepdims=True)) * (1.0 / math.sqrt(HEAD_DIM))).astype(bf16)
            dqn = _dot(ds, kn)
            dkn_ref[:, sl] += _dot(ds, qn, TN)
            dqg_ref[...] += jnp.sum(dqn * qhat, axis=0, keepdims=True)
            dqhat = dqn * qg
            dp_ref[:, sl] = (rq * (dqhat - qhat * jnp.mean(dqhat * qhat, axis=-1, keepdims=True))).astype(bf16)

    full = lambda shp: pl.BlockSpec(shp, lambda i: (0,) * len(shp))
    return pl.pallas_call(
        body, name="xattn_bwd", grid=(S // tq,),
        in_specs=[pl.BlockSpec((tq, dc), lambda i: (i, col0)), pl.BlockSpec((tq, dc), lambda i: (i, col0 + 1)),
                  full((M, 2 * dc)), _layer_spec(qg, l), _layer_spec(kg, l),
                  pl.BlockSpec((tq, dc), lambda i: (i, D // dc - 1)), pl.BlockSpec(memory_space=pl.ANY)],
        out_specs=[pl.BlockSpec((tq, 2 * dc), lambda i: (i, col0 // 2)), full((M, dc)), full((M, dc)),
                   full((1, HEAD_DIM))],
        out_shape=[SDS(dproj_prev.shape, bf16), SDS((M, dc), f32), SDS((M, dc), f32), SDS((1, HEAD_DIM), f32)],
        input_output_aliases={6: 0}, compiler_params=_params(),
    )(proj, proj, kv, qg, kg, dy, dproj_prev)


def _mem_bwd(mem, mg, kg, l, kv, dkn, dv, w_kv_g):
    M, D = mem.shape
    rb, n = w_kv_g.shape[1], w_kv_g.shape[2]
    dc = n // 2

    def body(m_ref, g_ref, kv_ref, dkn_ref, dv_ref, kg_ref, w_ref, dw_ref, dmg_ref, dkg_ref, dkv_ref):
        mv = m_ref[...]
        r = lax.rsqrt(jnp.mean(mv * mv, axis=-1, keepdims=True) + EPS)
        mhat = mv * r
        mh = (mhat * g_ref[...]).astype(bf16)
        kg = kg_ref[...]
        dkg = jnp.zeros((1, HEAD_DIM), f32)
        for h in range(N_HEADS):
            sl = slice(h * HEAD_DIM, (h + 1) * HEAD_DIM)
            kh = kv_ref[:, sl]
            rk = lax.rsqrt(jnp.mean(kh * kh, axis=-1, keepdims=True) + EPS)
            khat = kh * rk
            dkn_h = dkn_ref[:, sl]
            dkg = dkg + jnp.sum(dkn_h * khat, axis=0, keepdims=True)
            dkhat = dkn_h * kg
            dkv_ref[:, sl] = (rk * (dkhat - khat * jnp.mean(dkhat * khat, axis=-1, keepdims=True))).astype(bf16)
        dkv_ref[:, dc:] = dv_ref[...].astype(bf16)
        dkg_ref[...] = dkg
        dkv = dkv_ref[...]
        dw_ref[...] = _dot(mh, dkv, TN).astype(bf16).reshape(N_DEV, rb, n)
        dmh = _dot(dkv, w_ref[...].reshape(N_DEV * rb, n), NT)
        dmg_ref[...] = jnp.sum(dmh * mhat, axis=0, keepdims=True)

    full = lambda shp: pl.BlockSpec(shp, lambda i: (0,) * len(shp))
    wspec = full((N_DEV, rb, n))
    return pl.pallas_call(
        body, name="mem_bwd", grid=(1,),
        in_specs=[full((M, D)), _layer_spec(mg, l), full((M, n)), full((M, dc)), full((M, dc)), _layer_spec(kg, l), wspec],
        out_specs=[wspec, full((1, D)), full((1, HEAD_DIM))],
        out_shape=[SDS((N_DEV, rb, n), bf16), SDS((1, D), f32), SDS((1, HEAD_DIM), f32)],
        scratch_shapes=[pltpu.VMEM((M, n), bf16)], compiler_params=_params(),
    )(mem, mg, kv, dkn, dv, kg, w_kv_g)


SMALL = ("norm_g", "sgu_ln_g", "sgu_ln_b", "sgu_w", "sgu_b", "mem_norm_g", "q_norm_g", "k_norm_g")


def _small_rows(like):
    rows = [math.prod(like[n].shape) // 128 for n in SMALL]
    offs = [0]
    for r in rows:
        offs.append(offs[-1] + -(-r // 8) * 8)
    return rows, offs


def _pack_small(parts, offs):
    pieces = []
    for k, n in enumerate(SMALL):
        a = parts[n].reshape(-1, 128)
        pieces.append(jnp.pad(a, ((0, offs[k + 1] - offs[k] - a.shape[0]), (0, 0))))
    return jnp.concatenate(pieces)


def kernel(x, mem, norm_g, w_in, sgu_ln_g, sgu_ln_b, sgu_w, sgu_b, mem_norm_g, w_mem_kv, q_norm_g, k_norm_g, w_out, loss_target, m_norm_g, m_w_in, m_sgu_ln_g, m_sgu_ln_b, m_sgu_w, m_sgu_b, m_mem_norm_g, m_w_mem_kv, m_q_norm_g, m_k_norm_g, m_w_out, v_norm_g, v_w_in, v_sgu_ln_g, v_sgu_ln_b, v_sgu_w, v_sgu_b, v_mem_norm_g, v_w_mem_kv, v_q_norm_g, v_k_norm_g, v_w_out):
    L, D, wc = w_in.shape
    S = x.shape[1]
    da = D // 2
    xs = x.reshape(S, D)
    mems = mem.reshape(mem.shape[1], D)
    tgt = loss_target.reshape(S, D)
    stacked = lambda a: a.reshape(a.shape[0], 1, -1)
    ng, lng, lnb, mg, qg, kg = map(stacked, (norm_g, sgu_ln_g, sgu_ln_b, mem_norm_g, q_norm_g, k_norm_g))
    b_t = jnp.swapaxes(sgu_b, 1, 2)
    sb_col, xa_col = 3 * da // HEAD_DIM, (3 * da + D) // (D // 4)

    ax, ay, ac = lax.axis_index("x"), lax.axis_index("y"), lax.axis_index("c")
    ids = jnp.stack([4 * ax + 2 * ay + ac, 2 * ax + ay, ac]).astype(jnp.int32)
    w_in0_b = _cast_into_slot("cast_w_in", w_in, 0, 512, ids, ids)
    first = _gather3_start("gather_w_in0", [w_in0_b], ids)
    late = first["token"]
    w_b = [(w_in0_b if l == 0 else _cast_into_slot("cast_w_in", w_in, l, 512, ids, late),
            _cast_into_slot("cast_w_kv", w_mem_kv, l, 256, ids, late),
            _cast_into_slot("cast_w_out", w_out, l, 256, ids, late)) for l in range(L)]
    relay = _gather3_relay("gather_w_in0_relay",
                           _split_wait(first, *[a for wl in w_b for a in wl if a is not w_in0_b]), ids)
    in_fwd = _gather2_forward("gather_w_in0_forward", _split_wait(relay, relay["token"]), ids)

    acts = []
    xl = xs
    for l in range(L):
        (w_in_g,) = _split_wait(in_fwd, xl if l else in_fwd["token"])
        rest = _gather3_start(f"gather_w_rest{l}", [w_b[l][1], w_b[l][2]], w_in_g)
        order = rest["token"]
        if l + 1 < L:
            nxt = _gather3_start(f"gather_w_in{l + 1}", [w_b[l + 1][0]], order)
            order = nxt["token"]
        proj, h = _rms_proj(xl, ng, l, w_in_g, order)
        rest_relay = _gather3_relay(f"gather_w_rest{l}_relay", _split_wait(rest, proj), proj)
        y = _sgu_fwd(proj, lng, lnb, sgu_w, b_t, l, rest_relay["token"])
        rest_fwd = _gather2_forward(f"gather_w_rest{l}_forward", _split_wait(rest_relay, y), y)
        order = rest_fwd["token"]
        if l + 1 < L:
            nxt_relay = _gather3_relay(f"gather_w_in{l + 1}_relay", _split_wait(nxt, order), order)
            order = nxt_relay["token"]
        y, o_b, car = _sb_fwd(proj, y, sb_col, order)
        w_kv_g, w_out_g = _split_wait(rest_fwd, o_b)
        order = o_b
        if l + 1 < L:
            in_fwd = _gather2_forward(f"gather_w_in{l + 1}_forward", _split_wait(nxt_relay, o_b), o_b)
            order = in_fwd["token"]
        kv = _mem_kv(mems, mg, l, w_kv_g)
        y = _xattn_fwd(proj, kv, qg, kg, l, y, xa_col)
        acts.append((xl, proj, h, y, o_b, car, kv, w_in_g, w_kv_g, w_out_g))
        if l + 1 < L:
            xl = _out_proj(xl, y, w_out_g, order)
        else:
            dx, dxb, loss_part = _out_proj_loss(xl, y, w_out_g, tgt)
    loss = lax.psum(loss_part[0, 0], ("x", "y", "c"))

    weights = dict(norm_g=norm_g, sgu_ln_g=sgu_ln_g, sgu_ln_b=sgu_ln_b, sgu_w=sgu_w, sgu_b=sgu_b,
                   mem_norm_g=mem_norm_g, q_norm_g=q_norm_g, k_norm_g=k_norm_g)
    moms_m = dict(norm_g=m_norm_g, sgu_ln_g=m_sgu_ln_g, sgu_ln_b=m_sgu_ln_b, sgu_w=m_sgu_w, sgu_b=m_sgu_b,
                  mem_norm_g=m_mem_norm_g, q_norm_g=m_q_norm_g, k_norm_g=m_k_norm_g)
    moms_v = dict(norm_g=v_norm_g, sgu_ln_g=v_sgu_ln_g, sgu_ln_b=v_sgu_ln_b, sgu_w=v_sgu_w, sgu_b=v_sgu_b,
                  mem_norm_g=v_mem_norm_g, q_norm_g=v_q_norm_g, k_norm_g=v_k_norm_g)
    small_rows, small_offs = _small_rows(weights)
    head_rows = D // 128
    assert SMALL[0] == "norm_g" and head_rows % 8 == 0

    seconds = {}
    pending = None
    adam = {"w_out": None, "w_mem_kv": None, "w_in": None}

    late_in = {}

    def update(lu, order):
        (r_out,) = _scatter_finish(seconds[f"g_out{lu}"], order)
        adam["w_out"] = _sum_adam("adam_w_out", r_out, w_out, m_w_out, v_w_out, lu, adam["w_out"], 128, ids, ids)
        (r_kv,) = _scatter_finish(seconds[f"g_kv{lu}"], adam["w_out"][0])
        (late_in[lu],) = _scatter_finish(seconds[f"g_in{lu}"], adam["w_out"][0])
        adam["w_mem_kv"] = _sum_adam("adam_w_kv", r_kv, w_mem_kv, m_w_mem_kv, v_w_mem_kv, lu, adam["w_mem_kv"], 256, ids, ids)
        return adam["w_mem_kv"][0]

    def update_in(lu, order):
        adam["w_in"] = _sum_adam("adam_w_in", late_in[lu], w_in, m_w_in, v_w_in, lu, adam["w_in"], 256, ids, order)
        return adam["w_in"][0]
    small = {n: [None] * L for n in SMALL}
    for l in reversed(range(L)):
        xl, proj, h, y, o_b, car, kv, w_in_g, w_kv_g, w_out_g = acts[l]
        dy = _out_bwd_dy(dxb, w_out_g)
        order = dy
        if pending is not None:
            seconds[pending[0]] = _scatter2_second_level(pending[0], pending[1], dy, ids)
            order = seconds[pending[0]]["token"]
        g_out = _tn_grad("out_bwd_dw", y, dxb, 512, 512, True, order)
        seconds[f"g_out{l}"] = _scatter1_start(f"scatter_g_out{l}", [g_out], ids)
        dproj, d_sw, d_sb, d_lg, d_lb = _sgu_bwd(proj, dy, lng, lnb, sgu_w, b_t, l, seconds[f"g_out{l}"]["token"])
        dproj = _sb_bwd(proj, o_b, car, dy, dproj, sb_col, d_lb)
        dproj, dkn, dv, d_qg = _xattn_bwd(proj, kv, qg, kg, l, dy, dproj, xa_col)
        g_kv, d_mg, d_kg = _mem_bwd(mems, mg, kg, l, kv, dkn, dv, w_kv_g)
        seconds[f"g_kv{l}"] = _scatter1_start(f"scatter_g_kv{l}", [g_kv], ids)
        for n, val in (("sgu_ln_g", d_lg), ("sgu_ln_b", d_lb), ("sgu_w", d_sw), ("sgu_b", d_sb[:, :A_GROUPS].T),
                       ("mem_norm_g", d_mg), ("q_norm_g", d_qg), ("k_norm_g", d_kg)):
            small[n][l] = val.reshape(-1)
        order = seconds[f"g_kv{l}"]["token"]
        if l == 0:
            small["norm_g"][0] = jnp.zeros_like(small["norm_g"][1])
            part = _pack_small({n: jnp.stack(small[n]) for n in SMALL}, small_offs)
            tail = _gather1_start("gather_small_tail", [_into_slot("small_tail_slot", part[head_rows:], ids)], order)
            order = tail["token"]
        g_in_l = _tn_grad("in_bwd_dw", h, dproj, D, wc, False, order)
        first = _scatter2_pair_start(f"scatter_g_in{l}_pair", [g_in_l], ids)
        order = first["token"]
        pending = (f"g_in{l}", first)
        if l == 0:
            for lu in reversed(range(1, L)):
                order = update(lu, order)
            seconds[pending[0]] = _scatter2_second_level(pending[0], pending[1], order, ids)
            order = seconds[pending[0]]["token"]
            for lu in reversed(range(1, L)):
                order = update_in(lu, order)
        dh = _in_bwd_dh(dproj, w_in_g, order)
        dx, dxb, d_ng = _rms_bwd(dh, xl, ng, l, dx, order)
        small["norm_g"][l] = d_ng.reshape(-1)
    head = _gather1_start("gather_small_head",
                          [_into_slot("small_head_slot", small["norm_g"][0].reshape(head_rows, 128), ids)], ids)

    (r_out,) = _scatter_finish(seconds["g_out0"], head["token"])
    adam["w_out"] = _sum_adam("adam_w_out", r_out, w_out, m_w_out, v_w_out, 0, adam["w_out"], 128, ids, ids)
    (r_kv,) = _scatter_finish(seconds["g_kv0"], adam["w_out"][0])
    (r_in,) = _scatter_finish(seconds["g_in0"], adam["w_out"][0])
    adam["w_mem_kv"] = _sum_adam("adam_w_kv", r_kv, w_mem_kv, m_w_mem_kv, v_w_mem_kv, 0, adam["w_mem_kv"], 256, ids, ids)
    adam["w_in"] = _sum_adam("adam_w_in", r_in, w_in, m_w_in, v_w_in, 0, adam["w_in"], 256, ids, ids)
    (r_tail,) = _split_wait(tail, adam["w_in"][0])
    (r_head,) = _split_wait(head, r_tail)
    as128 = lambda d: [d[n].reshape(-1, 128) for n in SMALL]
    sm = _small_sum_adam(r_head, r_tail, as128(weights), as128(moms_m), as128(moms_v), small_offs)
    res = dict(adam)
    for p, n in enumerate(SMALL):
        res[n] = [sm[k * len(SMALL) + p].reshape(weights[n].shape) for k in range(4)]

    order = ("norm_g", "w_in", "sgu_ln_g", "sgu_ln_b", "sgu_w", "sgu_b", "mem_norm_g", "w_mem_kv", "q_norm_g",
             "k_norm_g", "w_out")
    outs = [loss, dx.reshape(x.shape)]
    for k in range(4):
        outs += [res[n][k] for n in order]
    return tuple(outs)
```

```python
import functools
import math

import jax
import jax.numpy as jnp
from jax import lax
from jax.experimental import pallas as pl
from jax.experimental.pallas import tpu as pltpu

f32 = jnp.float32
bf16 = jnp.bfloat16
SDS = jax.ShapeDtypeStruct

N_DEV = 8
EPS = 1e-6
CHUNK = 128
SGU_CHUNKS = 2
A_GROUPS = 8
HEAD_DIM = 128
N_HEADS = 4
TQ = 256
TK = 256
CARRY_LANES = 128
ADAM_LR, ADAM_B1, ADAM_B2, ADAM_EPS, ADAM_WD, ADAM_STEP = 0.001, 0.9, 0.999, 1e-08, 0.01, 10
MIB = 1024 * 1024

NT = (((1,), (1,)), ((), ()))
TN = (((0,), (0,)), ((), ()))


def _params(vmem_mib=48):
    return pltpu.CompilerParams(vmem_limit_bytes=vmem_mib * MIB)


def _gelu_and_grad(x):
    e = lax.erf(x * (1.0 / math.sqrt(2.0)))
    cdf = 0.5 * (1.0 + e)
    pdf = jnp.exp(-0.5 * x * x) * (1.0 / math.sqrt(2.0 * math.pi))
    return x * cdf, cdf + x * pdf


def _gelu(x):
    return 0.5 * x * (1.0 + lax.erf(x * (1.0 / math.sqrt(2.0))))


def _silu_and_grad(z):
    sg = jax.nn.sigmoid(z)
    return z * sg, sg * (1.0 + z * (1.0 - sg))


def _layer_spec(stacked, l):
    rest = stacked.shape[1:]
    return pl.BlockSpec((None,) + rest, lambda *idx: (l,) + (0,) * len(rest))


def _dot(a, b, dims=None):
    if dims is None:
        return jnp.dot(a, b, preferred_element_type=f32)
    return lax.dot_general(a, b, dims, preferred_element_type=f32)


_HBM = pl.BlockSpec(memory_space=pltpu.HBM)
_SEM = pl.BlockSpec(memory_space=pltpu.SEMAPHORE)
_EFFECT = pltpu.SideEffectType.DATAFLOW_SIDE_EFFECTING


def _split_start(name, bufs, n_remote, n_local, build, after):
    nb = len(bufs)

    def body(*refs):
        token = refs[-1]
        locals_, remotes = build(refs[:nb], *refs[nb + 1:nb + 4])
        for cp in locals_ + remotes:
            cp.start()
        token[...] = jnp.zeros_like(token)

    hbm = lambda a: pltpu.with_memory_space_constraint(a, pltpu.HBM)
    outs = pl.pallas_call(
        body, name=name,
        out_shape=(pltpu.SemaphoreType.DMA((n_remote,)), pltpu.SemaphoreType.DMA((n_remote,)),
                   pltpu.SemaphoreType.DMA((max(n_local, 1),)),
                   *[pltpu.HBM(b.shape, b.dtype) for b in bufs], SDS((8, 128), f32)),
        in_specs=[_HBM] * nb + [pl.BlockSpec(memory_space=pl.ANY)],
        out_specs=(_SEM, _SEM, _SEM, *[_HBM] * nb, pl.BlockSpec(memory_space=pltpu.VMEM)),
        input_output_aliases={k: 3 + k for k in range(nb)},
        compiler_params=pltpu.CompilerParams(has_side_effects=_EFFECT),
    )(*[hbm(b) for b in bufs], after)
    return dict(name=name, build=build, sems=outs[:3], bufs=outs[3:3 + nb], token=outs[-1])


def _split_wait(handle, *after):
    build, bufs = handle["build"], handle["bufs"]
    nb = len(bufs)

    def body(*refs):
        locals_, remotes = build(refs[:nb], *refs[nb:nb + 3])
        for cp in remotes:
            cp.wait_recv()
        for cp in remotes:
            cp.wait_send()
        for cp in locals_:
            cp.wait()

    outs = pl.pallas_call(
        body, name=handle["name"] + "_wait",
        out_shape=tuple(pltpu.HBM(b.shape, b.dtype) for b in bufs),
        in_specs=[_HBM] * nb + [_SEM] * 3 + [pl.BlockSpec(memory_space=pl.ANY)] * len(after),
        out_specs=tuple([_HBM] * nb),
        input_output_aliases={k: k for k in range(nb)},
        compiler_params=pltpu.CompilerParams(has_side_effects=_EFFECT),
    )(*bufs, *handle["sems"], *after)
    return list(outs)


def _remote(src, dst, send_sems, recv_sems, k, to):
    return pltpu.make_async_remote_copy(src_ref=src, dst_ref=dst, send_sem=send_sems.at[k], recv_sem=recv_sems.at[k],
                                        device_id=to, device_id_type=pl.DeviceIdType.MESH)


def _other_chips(x, y):
    return [(1 - x, y), (x, 1 - y), (1 - x, 1 - y)]


def _all_peers(x, y, c):
    return [(1 - x if m & 4 else x, 1 - y if m & 2 else y, 1 - c if m & 1 else c) for m in range(1, N_DEV)]


def _gather1_start(name, lands, after):
    def build(refs, send, recv, loc):
        x, y, c = lax.axis_index("x"), lax.axis_index("y"), lax.axis_index("c")
        me = 4 * x + 2 * y + c
        return [], [_remote(d.at[me], d.at[me], send, recv, 7 * a + k, peer)
                    for a, d in enumerate(refs) for k, peer in enumerate(_all_peers(x, y, c))]

    return _split_start(name, list(lands), 7 * len(lands), 0, build, after)


def _scatter1_start(name, srcs, after):
    n = len(srcs)

    def build(refs, send, recv, loc):
        x, y, c = lax.axis_index("x"), lax.axis_index("y"), lax.axis_index("c")
        me = 4 * x + 2 * y + c
        return [], [_remote(refs[a].at[4 * px + 2 * py + pc], refs[n + a].at[me], send, recv, 7 * a + k, (px, py, pc))
                    for a in range(n) for k, (px, py, pc) in enumerate(_all_peers(x, y, c))]

    return _split_start(name, list(srcs) + [lax.empty(s.shape, s.dtype) for s in srcs], 7 * n, 0, build, after)


def _gather2_start(name, lands, after):
    def build(refs, send, recv, loc):
        x, y, c = lax.axis_index("x"), lax.axis_index("y"), lax.axis_index("c")
        me = 4 * x + 2 * y + c
        remotes = []
        for a, d in enumerate(refs):
            remotes.append(_remote(d.at[me], d.at[me], send, recv, 4 * a, (x, y, 1 - c)))
            remotes += [_remote(d.at[me], d.at[me], send, recv, 4 * a + 1 + k, (px, py, c))
                        for k, (px, py) in enumerate(_other_chips(x, y))]
        return [], remotes

    return _split_start(name, list(lands), 4 * len(lands), 0, build, after)


def _gather3_start(name, lands, after):
    def build(refs, send, recv, loc):
        x, y, c = lax.axis_index("x"), lax.axis_index("y"), lax.axis_index("c")
        me = 4 * x + 2 * y + c
        return [], [_remote(d.at[me], d.at[me], send, recv, 3 * a + k, to)
                    for a, d in enumerate(refs) for k, to in enumerate([(x, y, 1 - c), (1 - x, y, c), (x, 1 - y, c)])]

    return _split_start(name, list(lands), 3 * len(lands), 0, build, after)


def _gather3_relay(name, lands, after):
    def build(refs, send, recv, loc):
        x, y, c = lax.axis_index("x"), lax.axis_index("y"), lax.axis_index("c")
        from_x = c == 0
        slot = 4 * jnp.where(from_x, 1 - x, x) + 2 * jnp.where(from_x, y, 1 - y) + c
        to = (jnp.where(from_x, x, 1 - x), jnp.where(from_x, 1 - y, y), c)
        return [], [_remote(d.at[slot], d.at[slot], send, recv, a, to) for a, d in enumerate(refs)]

    return _split_start(name, list(lands), len(lands), 0, build, after)


def _gather2_forward(name, lands, after):
    n = len(lands)

    def build(refs, send, recv, loc):
        x, y, c = lax.axis_index("x"), lax.axis_index("y"), lax.axis_index("c")
        slots = [4 * px + 2 * py + c for px, py in _other_chips(x, y)]
        return [], [_remote(d.at[sl], d.at[sl], send, recv, 3 * a + k, (x, y, 1 - c))
                    for a, d in enumerate(refs) for k, sl in enumerate(slots)]

    return _split_start(name, list(lands), 3 * n, 0, build, after)


def _scatter2_pair_start(name, srcs, after):
    n = len(srcs)

    def build(refs, send, recv, loc):
        x, y, c = lax.axis_index("x"), lax.axis_index("y"), lax.axis_index("c")
        return [], [_remote(refs[a].at[2 * q + 1 - c], refs[n + a].at[q], send, recv, 4 * a + q, (x, y, 1 - c))
                    for a in range(n) for q in range(4)]

    lands = [lax.empty((4,) + s.shape[1:], s.dtype) for s in srcs]
    return _split_start(name, list(srcs) + lands, 4 * n, 0, build, after)


def _scatter2_chip_start(name, pairs, after):
    n = len(pairs)

    def build(refs, send, recv, loc):
        x, y, c = lax.axis_index("x"), lax.axis_index("y"), lax.axis_index("c")
        return [], [_remote(refs[a].at[2 * px + py], refs[n + a].at[2 * x + y], send, recv, 3 * a + k, (px, py, c))
                    for a in range(n) for k, (px, py) in enumerate(_other_chips(x, y))]

    return _split_start(name, list(pairs) + [lax.empty(p.shape, p.dtype) for p in pairs], 3 * n, 0, build, after)


def _pair_sum(name, src, theirs, ids):
    _, R, C = theirs.shape
    tr = min(R, 1024)

    def body(ids_ref, a_ref, b_ref, o_ref):
        o_ref[...] = (a_ref[...].astype(f32) + b_ref[...].astype(f32)).astype(bf16)

    spec = pl.BlockSpec((None, tr, C), lambda q, i, ids: (q, i, 0))
    return pl.pallas_call(
        body, name=name,
        grid_spec=pltpu.PrefetchScalarGridSpec(
            num_scalar_prefetch=1, grid=(4, R // tr),
            in_specs=[pl.BlockSpec((None, tr, C), lambda q, i, ids: (2 * q + ids[2], i, 0)), spec], out_specs=spec),
        out_shape=SDS(theirs.shape, bf16), compiler_params=_params(),
    )(ids, src, theirs)


def _scatter2_second_level(name, first, after, ids):
    outs = _split_wait(first, after)
    n = len(outs) // 2
    pairs = [_pair_sum(f"pair_sum_{name}{a}", outs[a], outs[n + a], ids) for a in range(n)]
    return _scatter2_chip_start(f"scatter_{name}_chip", pairs, ids)


def _scatter_finish(second, after):
    outs = _split_wait(second, after)
    n = len(outs) // 2
    return [(outs[a], outs[n + a]) for a in range(n)]


def _cast_into_slot(name, w, l, tr, ids, after):
    _, R, C = w.shape

    def body(ids_ref, w_ref, after_ref, o_ref):
        o_ref[...] = w_ref[...].astype(bf16)

    return pl.pallas_call(
        body, name=name,
        grid_spec=pltpu.PrefetchScalarGridSpec(
            num_scalar_prefetch=1, grid=(R // tr,),
            in_specs=[pl.BlockSpec((None, tr, C), lambda i, ids: (l, i, 0)), pl.BlockSpec(memory_space=pl.ANY)],
            out_specs=pl.BlockSpec((None, tr, C), lambda i, ids: (ids[0], i, 0))),
        out_shape=SDS((N_DEV, R, C), bf16), compiler_params=_params(),
    )(ids, w, after)


def _into_slot(name, a, ids):
    R, C = a.shape

    def body(ids_ref, a_ref, o_ref):
        o_ref[...] = a_ref[...]

    return pl.pallas_call(
        body, name=name,
        grid_spec=pltpu.PrefetchScalarGridSpec(
            num_scalar_prefetch=1, grid=(1,),
            in_specs=[pl.BlockSpec((R, C), lambda i, ids: (0, 0))],
            out_specs=pl.BlockSpec((None, R, C), lambda i, ids: (ids[0], 0, 0))),
        out_shape=SDS((N_DEV, R, C), f32), compiler_params=_params(),
    )(ids, a)


def _adam_math(w, g, m, v):
    m2 = ADAM_B1 * m + (1.0 - ADAM_B1) * g
    v2 = ADAM_B2 * v + (1.0 - ADAM_B2) * (g * g)
    m_hat = m2 / (1.0 - ADAM_B1 ** ADAM_STEP)
    v_hat = v2 / (1.0 - ADAM_B2 ** ADAM_STEP)
    delta = -ADAM_LR * (m_hat / (jnp.sqrt(v_hat) + ADAM_EPS) + ADAM_WD * w)
    return delta, m2, v2


def _sum_adam(name, pair_recv, w, m, v, l, prev, tr, ids, after):
    own, recv = pair_recv
    L, R, C = w.shape
    slots = recv.shape[0]
    mine = 0 if slots == N_DEV else 1

    def body(ids_ref, r_ref, own_ref, w_ref, m_ref, v_ref, after_ref, *rest):
        g_ref, d_ref, m2_ref, v2_ref = rest[-4:]
        terms = [jnp.where(ids_ref[mine] == q, own_ref[...], r_ref[q]).astype(f32) for q in range(slots)]
        g = terms[0]
        for t in terms[1:]:
            g = g + t
        d, m2, v2 = _adam_math(w_ref[...], g, m_ref[...], v_ref[...])
        g_ref[...] = g
        d_ref[...] = d
        m2_ref[...] = m2
        v2_ref[...] = v2

    wspec = pl.BlockSpec((None, tr, C), lambda i, ids: (l, i, 0))
    in_specs = [pl.BlockSpec((slots, tr, C), lambda i, ids: (0, i, 0)),
                pl.BlockSpec((None, tr, C), lambda i, ids: (ids[mine], i, 0)), wspec, wspec, wspec,
                pl.BlockSpec(memory_space=pl.ANY)]
    args = [ids, recv, own, w, m, v, after]
    aliases = {}
    if prev is not None:
        in_specs += [pl.BlockSpec(memory_space=pl.ANY)] * 4
        args += list(prev)
        aliases = {7 + k: k for k in range(4)}
    return pl.pallas_call(
        body, name=name,
        grid_spec=pltpu.PrefetchScalarGridSpec(num_scalar_prefetch=1, grid=(R // tr,), in_specs=in_specs,
                                               out_specs=[wspec] * 4),
        out_shape=[SDS((L, R, C), f32)] * 4, input_output_aliases=aliases, compiler_params=_params(),
    )(*args)


def _small_sum_adam(recv_head, recv_tail, ws, ms, vs, offs):
    n = len(ws)
    r0 = recv_head.shape[1]

    def body(*refs):
        rh, rt = refs[0], refs[1]
        w_refs, m_refs, v_refs = refs[2:2 + n], refs[2 + n:2 + 2 * n], refs[2 + 2 * n:2 + 3 * n]
        outs = refs[2 + 3 * n:]
        for p in range(n):
            lo, hi = offs[p], offs[p] + ws[p].shape[0]
            pieces = []
            if lo < r0:
                pieces.append((rh, lo, 0, min(hi, r0) - lo))
            if hi > r0:
                pieces.append((rt, max(lo, r0) - r0, max(lo, r0) - lo, hi - max(lo, r0)))
            for src, a, b, cnt in pieces:
                g = src[0, a:a + cnt, :]
                for s in range(1, N_DEV):
                    g = g + src[s, a:a + cnt, :]
                d, m2, v2 = _adam_math(w_refs[p][b:b + cnt, :], g, m_refs[p][b:b + cnt, :], v_refs[p][b:b + cnt, :])
                for k, val in enumerate((g, d, m2, v2)):
                    outs[k * n + p][b:b + cnt, :] = val

    return pl.pallas_call(
        body, name="small_sum_adam", out_shape=[SDS(w.shape, f32) for w in ws] * 4, compiler_params=_params(),
    )(recv_head, recv_tail, *ws, *ms, *vs)


def _rms_proj(x, g, l, w_in_g, after, tm=1024):
    S, D = x.shape
    wc = w_in_g.shape[2]
    n_out = N_DEV * wc

    def body(x_ref, g_ref, w_ref, after_ref, proj_ref, h_ref):
        @pl.when(pl.program_id(1) == 0)
        def _():
            xv = x_ref[...]
            r = lax.rsqrt(jnp.mean(xv * xv, axis=-1, keepdims=True) + EPS)
            h_ref[...] = (xv * r * g_ref[...]).astype(bf16)

        proj_ref[...] = _dot(h_ref[...], w_ref[...])

    return pl.pallas_call(
        body, name="rms_proj", grid=(S // tm, N_DEV),
        in_specs=[pl.BlockSpec((tm, D), lambda i, j: (i, 0)), _layer_spec(g, l),
                  pl.BlockSpec((None, D, wc), lambda i, j: (j, 0, 0)), pl.BlockSpec(memory_space=pl.ANY)],
        out_specs=[pl.BlockSpec((tm, wc), lambda i, j: (i, j)), pl.BlockSpec((tm, D), lambda i, j: (i, 0))],
        out_shape=[SDS((S, n_out), f32), SDS((S, D), bf16)], compiler_params=_params(),
    )(x, g, w_in_g, after)


def _out_proj(x, y, w_out_g, after, tm=512):
    S, D = x.shape
    rb = w_out_g.shape[1]

    def body(x_ref, y_ref, w_ref, after_ref, o_ref):
        w = w_ref[...].reshape(N_DEV * rb, D)
        o_ref[...] = x_ref[...] + _dot(y_ref[...], w)

    row = pl.BlockSpec((tm, D), lambda i: (i, 0))
    return pl.pallas_call(
        body, name="out_proj", grid=(S // tm,),
        in_specs=[row, row, pl.BlockSpec((N_DEV, rb, D), lambda i: (0, 0, 0)), pl.BlockSpec(memory_space=pl.ANY)],
        out_specs=row, out_shape=SDS((S, D), f32), compiler_params=_params(),
    )(x, y, w_out_g, after)


def _out_proj_loss(x, y, w_out_g, tgt, tm=256):
    S, D = x.shape
    rb = w_out_g.shape[1]

    def body(x_ref, y_ref, w_ref, t_ref, dx_ref, dxb_ref, l_ref):
        i = pl.program_id(0)
        w = w_ref[...].reshape(N_DEV * rb, D)
        d = x_ref[...] + _dot(y_ref[...], w) - t_ref[...]
        dx = d * (1.0 / D)
        dx_ref[...] = dx
        dxb_ref[...] = dx.astype(bf16)
        e = d * d
        part = e[:, 0:128]
        for k in range(1, D // 128):
            part = part + e[:, k * 128:(k + 1) * 128]
        part = jnp.sum(part.reshape(tm // 8, 8, 128), axis=0)

        @pl.when(i == 0)
        def _():
            l_ref[...] = jnp.zeros_like(l_ref)

        l_ref[...] += part

        @pl.when(i == pl.num_programs(0) - 1)
        def _():
            tot = jnp.sum(l_ref[...], axis=1, keepdims=True)
            tot = jnp.sum(tot, axis=0, keepdims=True)
            l_ref[...] = jnp.broadcast_to(tot * (0.5 / D), l_ref.shape)

    row = pl.BlockSpec((tm, D), lambda i: (i, 0))
    return pl.pallas_call(
        body, name="out_proj_loss", grid=(S // tm,),
        in_specs=[row, row, pl.BlockSpec((N_DEV, rb, D), lambda i: (0, 0, 0)), row],
        out_specs=[row, row, pl.BlockSpec((8, 128), lambda i: (0, 0))],
        out_shape=[SDS((S, D), f32), SDS((S, D), bf16), SDS((8, 128), f32)], compiler_params=_params(),
    )(x, y, w_out_g, tgt)


def _out_bwd_dy(dxb, w_out_g, tm=512):
    S, D = dxb.shape
    rb = w_out_g.shape[1]

    nb = 2

    def body(dx_ref, w_ref, o_ref):
        o_ref[...] = _dot(dx_ref[...], w_ref[...].reshape(nb * rb, D), NT)

    return pl.pallas_call(
        body, name="out_bwd_dy", grid=(S // tm, N_DEV // nb),
        in_specs=[pl.BlockSpec((tm, D), lambda i, j: (i, 0)),
                  pl.BlockSpec((nb, rb, D), lambda i, j: (j, 0, 0))],
        out_specs=pl.BlockSpec((tm, nb * rb), lambda i, j: (i, j)),
        out_shape=SDS((S, D), f32), compiler_params=_params(),
    )(dxb, w_out_g)


def _tn_grad(name, a, b, tm, tn, rows_major, after):
    S, M = a.shape
    N = b.shape[1]
    if rows_major:
        rb = M // N_DEV
        nb = tm // rb
        out_shape = SDS((N_DEV, rb, N), bf16)
        out_spec = pl.BlockSpec((nb, rb, tn), lambda i, j: (i, 0, j))
    else:
        out_shape = SDS((N_DEV, M, N // N_DEV), bf16)
        assert tn == N // N_DEV
        out_spec = pl.BlockSpec((None, tm, tn), lambda i, j: (j, i, 0))

    def body(a_ref, b_ref, after_ref, o_ref):
        o_ref[...] = _dot(a_ref[...], b_ref[...], TN).astype(bf16).reshape(o_ref.shape)

    return pl.pallas_call(
        body, name=name, grid=(M // tm, N // tn),
        in_specs=[pl.BlockSpec((S, tm), lambda i, j: (0, i)), pl.BlockSpec((S, tn), lambda i, j: (0, j)),
                  pl.BlockSpec(memory_space=pl.ANY)],
        out_specs=out_spec, out_shape=out_shape, compiler_params=_params(),
    )(a, b, after)


def _in_bwd_dh(dproj, w_in_g, after, tm=1024, tn=256):
    S = dproj.shape[0]
    _, D, wc = w_in_g.shape
    tm = min(tm, S)

    def body(dp_ref, w_ref, after_ref, o_ref):
        acc = _dot(dp_ref[:, 0:wc], w_ref[0], NT)
        for k in range(1, N_DEV):
            acc = acc + _dot(dp_ref[:, k * wc:(k + 1) * wc], w_ref[k], NT)
        o_ref[...] = acc

    return pl.pallas_call(
        body, name="in_bwd_dh", grid=(S // tm, D // tn),
        in_specs=[pl.BlockSpec((tm, N_DEV * wc), lambda i, j: (i, 0)),
                  pl.BlockSpec((N_DEV, tn, wc), lambda i, j: (0, j, 0)), pl.BlockSpec(memory_space=pl.ANY)],
        out_specs=pl.BlockSpec((tm, tn), lambda i, j: (i, j)),
        out_shape=SDS((S, D), f32), compiler_params=_params(),
    )(dproj, w_in_g, after)


def _rms_bwd(dh, x, g, l, dx_next, after, tm=256):
    S, D = x.shape

    def body(dh_ref, x_ref, g_ref, dxn_ref, after_ref, dx_ref, dxb_ref, dg_ref):
        @pl.when(pl.program_id(0) == 0)
        def _():
            dg_ref[...] = jnp.zeros_like(dg_ref)

        dh = dh_ref[...]
        xv = x_ref[...]
        r = lax.rsqrt(jnp.mean(xv * xv, axis=-1, keepdims=True) + EPS)
        xhat = xv * r
        dxhat = dh * g_ref[...]
        dx = r * (dxhat - xhat * jnp.mean(dxhat * xhat, axis=-1, keepdims=True)) + dxn_ref[...]
        dx_ref[...] = dx
        dxb_ref[...] = dx.astype(bf16)
        dg_ref[...] += jnp.sum(dh * xhat, axis=0, keepdims=True)

    row = pl.BlockSpec((tm, D), lambda i: (i, 0))
    vec = pl.BlockSpec((1, D), lambda i: (0, 0))
    return pl.pallas_call(
        body, name="rms_bwd", grid=(S // tm,),
        in_specs=[row, row, _layer_spec(g, l), row, pl.BlockSpec(memory_space=pl.ANY)], out_specs=[row, row, vec],
        out_shape=[SDS((S, D), f32), SDS((S, D), bf16), SDS((1, D), f32)], compiler_params=_params(),
    )(dh, x, g, dx_next, after)


def _sgu_fwd(proj, ln_g, ln_b, w_s, b_t, l, after):
    S = proj.shape[0]
    da = A_GROUPS * HEAD_DIM
    D = 2 * da

    def body(u_ref, v_ref, z_ref, lg_ref, lb_ref, w_ref, bt_ref, after_ref, y_ref):
        u = _gelu(u_ref[...])
        v = _gelu(v_ref[...])
        z = z_ref[...]
        mu = jnp.mean(v, axis=-1, keepdims=True)
        xc = v - mu
        rs = lax.rsqrt(jnp.mean(xc * xc, axis=-1, keepdims=True) + EPS)
        vn = (xc * rs * lg_ref[...] + lb_ref[...]).astype(bf16)
        gate = u * (z * jax.nn.sigmoid(z))
        tri = lax.broadcasted_iota(jnp.int32, (CHUNK, CHUNK), 0) >= lax.broadcasted_iota(jnp.int32, (CHUNK, CHUNK), 1)
        for g in range(A_GROUPS):
            sl = slice(g * HEAD_DIM, (g + 1) * HEAD_DIM)
            wm = jnp.where(tri, w_ref[g], 0.0).astype(bf16)
            for ck in range(SGU_CHUNKS):
                rw = slice(ck * CHUNK, (ck + 1) * CHUNK)
                mixed = _dot(wm, vn[rw, sl]) + bt_ref[:, g:g + 1]
                y_ref[rw, sl] = (gate[rw, sl] * mixed).astype(bf16)

    blk = lambda cb: pl.BlockSpec((SGU_CHUNKS * CHUNK, da), lambda c: (c, cb))
    full = lambda shp: pl.BlockSpec(shp, lambda c: (0,) * len(shp))
    return pl.pallas_call(
        body, name="sgu_fwd", grid=(S // (SGU_CHUNKS * CHUNK),),
        in_specs=[blk(0), blk(1), blk(2), _layer_spec(ln_g, l), _layer_spec(ln_b, l), _layer_spec(w_s, l),
                  _layer_spec(b_t, l), pl.BlockSpec(memory_space=pl.ANY)],
        out_specs=blk(0), out_shape=SDS((S, D), bf16), compiler_params=_params(),
    )(proj, proj, proj, ln_g, ln_b, w_s, b_t, after)


def _sgu_bwd(proj, dy, ln_g, ln_b, w_s, b_t, l, after):
    S = proj.shape[0]
    da = A_GROUPS * HEAD_DIM
    n_proj = proj.shape[1]

    def body(u_ref, v_ref, z_ref, dy_ref, lg_ref, lb_ref, w_ref, bt_ref, after_ref,
             dp_ref, dw_ref, db_ref, dlg_ref, dlb_ref, dvn_ref):
        @pl.when(pl.program_id(0) == 0)
        def _():
            dw_ref[...] = jnp.zeros_like(dw_ref)
            db_ref[...] = jnp.zeros_like(db_ref)
            dlg_ref[...] = jnp.zeros_like(dlg_ref)
            dlb_ref[...] = jnp.zeros_like(dlb_ref)

        up, vp, z, dy = u_ref[...], v_ref[...], z_ref[...], dy_ref[...]
        u, gu = _gelu_and_grad(up)
        v, gv = _gelu_and_grad(vp)
        s, gs = _silu_and_grad(z)
        mu = jnp.mean(v, axis=-1, keepdims=True)
        xc = v - mu
        rs = lax.rsqrt(jnp.mean(xc * xc, axis=-1, keepdims=True) + EPS)
        vhat = xc * rs
        lg = lg_ref[...]
        vn = (vhat * lg + lb_ref[...]).astype(bf16)
        tri = lax.broadcasted_iota(jnp.int32, (CHUNK, CHUNK), 0) >= lax.broadcasted_iota(jnp.int32, (CHUNK, CHUNK), 1)
        lane = lax.broadcasted_iota(jnp.int32, (CHUNK, HEAD_DIM), 1)
        dys = dy * s
        db = jnp.zeros((CHUNK, HEAD_DIM), f32)
        for g in range(A_GROUPS):
            sl = slice(g * HEAD_DIM, (g + 1) * HEAD_DIM)
            zl = slice(2 * da + g * HEAD_DIM, 2 * da + (g + 1) * HEAD_DIM)
            wm = jnp.where(tri, w_ref[g], 0.0).astype(bf16)
            for ck in range(SGU_CHUNKS):
                rw = slice(ck * CHUNK, (ck + 1) * CHUNK)
                mixed = _dot(wm, vn[rw, sl]) + bt_ref[:, g:g + 1]
                dmix = dys[rw, sl] * u[rw, sl]
                dp_ref[rw, sl] = (dys[rw, sl] * mixed * gu[rw, sl]).astype(bf16)
                dp_ref[rw, zl] = (dy[rw, sl] * u[rw, sl] * mixed * gs[rw, sl]).astype(bf16)
                dmb = dmix.astype(bf16)
                dw_ref[g] += jnp.where(tri, _dot(dmb, vn[rw, sl], NT), 0.0)
                dvn_ref[rw, sl] = _dot(wm, dmb, TN)
                db = db + jnp.where(lane == g, jnp.sum(dmix, axis=1, keepdims=True), 0.0)
        db_ref[...] += db
        dvn = dvn_ref[...]
        dlg_ref[...] += jnp.sum(dvn * vhat, axis=0, keepdims=True)
        dlb_ref[...] += jnp.sum(dvn, axis=0, keepdims=True)
        dvhat = dvn * lg
        dv = rs * (dvhat - jnp.mean(dvhat, axis=-1, keepdims=True)
                   - vhat * jnp.mean(dvhat * vhat, axis=-1, keepdims=True))
        dp_ref[:, da:2 * da] = (dv * gv).astype(bf16)

    rows = SGU_CHUNKS * CHUNK
    blk = lambda cb: pl.BlockSpec((rows, da), lambda c: (c, cb))
    full = lambda shp: pl.BlockSpec(shp, lambda c: (0,) * len(shp))
    return pl.pallas_call(
        body, name="sgu_bwd", grid=(S // rows,),
        in_specs=[blk(0), blk(1), blk(2), blk(0), _layer_spec(ln_g, l), _layer_spec(ln_b, l), _layer_spec(w_s, l),
                  _layer_spec(b_t, l), pl.BlockSpec(memory_space=pl.ANY)],
        out_specs=[pl.BlockSpec((rows, 3 * da), lambda c: (c, 0)), full((A_GROUPS, CHUNK, CHUNK)),
                   full((CHUNK, HEAD_DIM)), full((1, da)), full((1, da))],
        out_shape=[SDS((S, n_proj), bf16), SDS((A_GROUPS, CHUNK, CHUNK), f32), SDS((CHUNK, HEAD_DIM), f32),
                   SDS((1, da), f32), SDS((1, da), f32)],
        scratch_shapes=[pltpu.VMEM((rows, da), f32)], compiler_params=_params(),
    )(proj, proj, proj, dy, ln_g, ln_b, w_s, b_t, after)


def _sb_scores(q, kblk, kb, rows, cols, masked):
    z = _dot(q, kblk, NT) * (1.0 / math.sqrt(HEAD_DIM))
    t = jnp.log(1.0 + jnp.exp(-jnp.abs(z)))
    log_1mb = -(jnp.maximum(z, 0.0) + t)
    log_beta = jnp.minimum(z, 0.0) - t
    if not masked:
        return None, log_beta, log_1mb
    causal = (cols + kb * TK) < rows
    return causal, log_beta, jnp.where(causal, log_1mb, 0.0)


def _sb_tiles(i):
    rows = i * TQ + lax.broadcasted_iota(jnp.int32, (TQ, TK), 0)
    cols = lax.broadcasted_iota(jnp.int32, (TQ, TK), 1)
    r_i = lax.broadcasted_iota(jnp.int32, (TK, TK), 0)
    c_i = lax.broadcasted_iota(jnp.int32, (TK, TK), 1)
    upper, lower = (r_i > c_i).astype(bf16), (r_i < c_i).astype(bf16)
    slot = lax.broadcasted_iota(jnp.int32, (TQ, CARRY_LANES), 1)
    return rows, cols, slot, jnp.concatenate([upper, upper], axis=0), jnp.concatenate([lower, lower], axis=0)


def _suffix_sum(t, tri):
    hi = lax.bitcast_convert_type(lax.bitcast_convert_type(t, jnp.uint32) & jnp.uint32(0xFFFF0000), f32)
    both = jnp.concatenate([hi.astype(bf16), (t - hi).astype(bf16)], axis=1)
    return _dot(both, tri)


def _sb_fwd(proj, y_prev, col0, after):
    S = proj.shape[0]
    D = y_prev.shape[1]
    dh = N_HEADS * HEAD_DIM
    n_diag = TQ // TK

    def body(q_ref, k_ref, v_ref, z_ref, yp_ref, after_ref, y_ref, o_ref, car_ref, qb, kb_s, vb_s, c_ref):
        i = pl.program_id(0)

        @pl.when(i == 0)
        def _():
            kb_s[...] = k_ref[...].astype(bf16)
            vb_s[...] = v_ref[...].astype(bf16)

        qb[...] = q_ref[...].astype(bf16)
        o_ref[...] = jnp.zeros_like(o_ref)
        c_ref[...] = jnp.zeros_like(c_ref)
        car_ref[...] = jnp.zeros_like(car_ref)
        nkb = (i + 1) * n_diag
        rows, cols, slot, upper, _ = _sb_tiles(i)

        def make_step(masked):
            def step(jj, carry):
                kb = nkb - 1 - jj
                off = pl.multiple_of(kb * TK, TK)
                hs = range(N_HEADS)
                sls = [slice(h * HEAD_DIM, (h + 1) * HEAD_DIM) for h in hs]
                sc = [_sb_scores(qb[:, sls[h]], kb_s[pl.ds(off, TK), sls[h]], kb, rows, cols, masked) for h in hs]
                suf = [_suffix_sum(sc[h][2], upper) for h in hs]
                cs = [c_ref[h] for h in hs]
                es = [jnp.exp(sc[h][1] + suf[h] + cs[h][:, :1]) for h in hs]
                if masked:
                    es = [jnp.where(sc[h][0], es[h], 0.0) for h in hs]
                pv = [_dot(es[h].astype(bf16), vb_s[pl.ds(off, TK), sls[h]]) for h in hs]
                for h in hs:
                    o_ref[:, sls[h]] += pv[h]
                    car_ref[h] = jnp.where(slot == kb, cs[h], car_ref[h])
                    c_ref[h] = cs[h] + jnp.sum(sc[h][2], axis=1, keepdims=True)
                return carry
            return step

        lax.fori_loop(0, n_diag, make_step(True), 0)
        lax.fori_loop(n_diag, nkb, make_step(False), 0)
        z = z_ref[...]
        y_ref[...] = (o_ref[...] * (z * jax.nn.sigmoid(z))).astype(bf16)

    cb = col0 * HEAD_DIM // dh
    qspec = lambda k: pl.BlockSpec((TQ, dh), lambda i: (i, cb + k))
    kspec = lambda k: pl.BlockSpec((S, dh), lambda i: (0, cb + k))
    return pl.pallas_call(
        body, name="sb_fwd", grid=(S // TQ,),
        in_specs=[qspec(0), kspec(1), kspec(2), qspec(3), pl.BlockSpec(memory_space=pl.ANY),
                  pl.BlockSpec(memory_space=pl.ANY)],
        out_specs=[pl.BlockSpec((TQ, dh), lambda i: (i, A_GROUPS * HEAD_DIM // dh)),
                   pl.BlockSpec((TQ, dh), lambda i: (i, 0)),
                   pl.BlockSpec((N_HEADS, TQ, CARRY_LANES), lambda i: (0, i, 0))],
        out_shape=[SDS((S, D), bf16), SDS((S, dh), f32), SDS((N_HEADS, S, CARRY_LANES), f32)],
        input_output_aliases={4: 0},
        scratch_shapes=[pltpu.VMEM((TQ, dh), bf16), pltpu.VMEM((S, dh), bf16), pltpu.VMEM((S, dh), bf16),
                        pltpu.VMEM((N_HEADS, TQ, CARRY_LANES), f32)],
        compiler_params=_params(),
    )(proj, proj, proj, proj, y_prev, after)


def _sb_bwd(proj, o, car, dy, dproj_prev, col0, after):
    S = proj.shape[0]
    n_i = S // TQ
    dh = N_HEADS * HEAD_DIM
    n_diag = TQ // TK
    cb = col0 * HEAD_DIM // dh
    scale = 1.0 / math.sqrt(HEAD_DIM)

    def body(q_ref, k_ref, v_ref, z_ref, o_ref, car_ref, dy_ref, dpp_ref, after_ref,
             dp_ref, qb, kb_s, vb_s, dob, p_ref, dq_acc, dk_acc, dv_acc, st_a, st_b, st_k, st_v, tile_sems):
        i = pl.program_id(0)

        def put(stage_ref, row0, nrows, k):
            pltpu.sync_copy(stage_ref, dp_ref.at[pl.ds(row0, nrows), pl.ds((cb + k) * dh, dh)])

        def tile_copies(step):
            rows = pl.ds(pl.multiple_of(step * TQ, TQ), TQ)
            return [pltpu.make_async_copy(st, dp_ref.at[rows, pl.ds((cb + k) * dh, dh)], tile_sems.at[n])
                    for n, (st, k) in enumerate(((st_a, 0), (st_b, 3)))]

        @pl.when(i > 0)
        def _():
            for cp in tile_copies(i - 1):
                cp.wait()

        @pl.when(i == 0)
        def _():
            kb_s[...] = k_ref[...].astype(bf16)
            vb_s[...] = v_ref[...].astype(bf16)
            dk_acc[...] = jnp.zeros_like(dk_acc)
            dv_acc[...] = jnp.zeros_like(dv_acc)

        s, gs = _silu_and_grad(z_ref[...])
        dy = dy_ref[...]
        st_b[...] = (dy * o_ref[...] * gs).astype(bf16)
        dob[...] = (dy * s).astype(bf16)
        qb[...] = q_ref[...].astype(bf16)
        p_ref[...] = jnp.zeros_like(p_ref)
        dq_acc[...] = jnp.zeros_like(dq_acc)
        nkb = (i + 1) * n_diag
        rows, cols, slot, upper, lower = _sb_tiles(i)

        def make_step(masked):
            def step(kb, carry):
                off = pl.multiple_of(kb * TK, TK)
                hs = range(N_HEADS)
                sls = [slice(h * HEAD_DIM, (h + 1) * HEAD_DIM) for h in hs]
                qs = [qb[:, sls[h]] for h in hs]
                ks = [kb_s[pl.ds(off, TK), sls[h]] for h in hs]
                dos = [dob[:, sls[h]] for h in hs]
                sc = [_sb_scores(qs[h], ks[h], kb, rows, cols, masked) for h in hs]
                da = [_dot(dos[h], vb_s[pl.ds(off, TK), sls[h]], NT) for h in hs]
                suf = [_suffix_sum(sc[h][2], upper) for h in hs]
                onehot = slot == kb
                cs = [jnp.sum(jnp.where(onehot, car_ref[h], 0.0), axis=1, keepdims=True) for h in hs]
                es = [jnp.exp(sc[h][1] + suf[h] + cs[h]) for h in hs]
                if masked:
                    es = [jnp.where(sc[h][0], es[h], 0.0) for h in hs]
                gs_ = [da[h] * es[h] for h in hs]
                ps = [p_ref[h] for h in hs]
                pre = [_suffix_sum(gs_[h], lower) + ps[h][:, :1] for h in hs]
                dzs = []
                for h in hs:
                    beta = jnp.exp(sc[h][1])
                    dzz = gs_[h] * (1.0 - beta) - beta * pre[h]
                    if masked:
                        dzz = jnp.where(sc[h][0], dzz, 0.0)
                    dzs.append((dzz * scale).astype(bf16))
                dqs = [_dot(dzs[h], ks[h]) for h in hs]
                dks = [_dot(dzs[h], qs[h], TN) for h in hs]
                dvs = [_dot(es[h].astype(bf16), dos[h], TN) for h in hs]
                for h in hs:
                    dq_acc[:, sls[h]] += dqs[h]
                    dk_acc[pl.ds(off, TK), sls[h]] += dks[h]
                    dv_acc[pl.ds(off, TK), sls[h]] += dvs[h]
                    p_ref[h] = ps[h] + jnp.sum(gs_[h], axis=1, keepdims=True)
                return carry
            return step

        lax.fori_loop(0, nkb - n_diag, make_step(False), 0)
        lax.fori_loop(nkb - n_diag, nkb, make_step(True), 0)
        st_a[...] = dq_acc[...].astype(bf16)
        for cp in tile_copies(i):
            cp.start()

        @pl.when(i == n_i - 1)
        def _():
            st_k[...] = dk_acc[...].astype(bf16)
            st_v[...] = dv_acc[...].astype(bf16)
            put(st_k, 0, S, 1)
            put(st_v, 0, S, 2)
            for cp in tile_copies(i):
                cp.wait()

    qspec = lambda k: pl.BlockSpec((TQ, dh), lambda i: (i, cb + k))
    kspec = lambda k: pl.BlockSpec((S, dh), lambda i: (0, cb + k))
    return pl.pallas_call(
        body, name="sb_bwd", grid=(n_i,),
        in_specs=[qspec(0), kspec(1), kspec(2), qspec(3),
                  pl.BlockSpec((TQ, dh), lambda i: (i, 0)),
                  pl.BlockSpec((N_HEADS, TQ, CARRY_LANES), lambda i: (0, i, 0)),
                  pl.BlockSpec((TQ, dh), lambda i: (i, A_GROUPS * HEAD_DIM // dh)),
                  pl.BlockSpec(memory_space=pl.ANY), pl.BlockSpec(memory_space=pl.ANY)],
        out_specs=pl.BlockSpec(memory_space=pl.ANY),
        out_shape=SDS(dproj_prev.shape, bf16),
        input_output_aliases={7: 0},
        scratch_shapes=[pltpu.VMEM((TQ, dh), bf16), pltpu.VMEM((S, dh), bf16), pltpu.VMEM((S, dh), bf16),
                        pltpu.VMEM((TQ, dh), bf16), pltpu.VMEM((N_HEADS, TQ, CARRY_LANES), f32), pltpu.VMEM((TQ, dh), f32),
                        pltpu.VMEM((S, dh), f32), pltpu.VMEM((S, dh), f32),
                        pltpu.VMEM((TQ, dh), bf16), pltpu.VMEM((TQ, dh), bf16),
                        pltpu.VMEM((S, dh), bf16), pltpu.VMEM((S, dh), bf16), pltpu.SemaphoreType.DMA((2,))],
        compiler_params=_params(56),
    )(proj, proj, proj, proj, o, car, dy, dproj_prev, after)


def _mem_kv(mem, mg, l, w_kv_g):
    M, D = mem.shape
    rb, n = w_kv_g.shape[1], w_kv_g.shape[2]

    def body(m_ref, g_ref, w_ref, kv_ref):
        mv = m_ref[...]
        r = lax.rsqrt(jnp.mean(mv * mv, axis=-1, keepdims=True) + EPS)
        mh = (mv * r * g_ref[...]).astype(bf16)
        kv_ref[...] = _dot(mh, w_ref[...].reshape(N_DEV * rb, n))

    return pl.pallas_call(
        body, name="mem_kv", grid=(1,),
        in_specs=[pl.BlockSpec((M, D), lambda i: (0, 0)), _layer_spec(mg, l),
                  pl.BlockSpec((N_DEV, rb, n), lambda i: (0, 0, 0))],
        out_specs=pl.BlockSpec((M, n), lambda i: (0, 0)),
        out_shape=SDS((M, n), f32), compiler_params=_params(),
    )(mem, mg, w_kv_g)


def _xattn_head(q_ref, kv_ref, qg, kg, h):
    dc = N_HEADS * HEAD_DIM
    sl = slice(h * HEAD_DIM, (h + 1) * HEAD_DIM)
    qh = q_ref[:, sl]
    rq = lax.rsqrt(jnp.mean(qh * qh, axis=-1, keepdims=True) + EPS)
    qhat = qh * rq
    qn = (qhat * qg).astype(bf16)
    kh = kv_ref[:, sl]
    rk = lax.rsqrt(jnp.mean(kh * kh, axis=-1, keepdims=True) + EPS)
    kn = (kh * rk * kg).astype(bf16)
    vh = kv_ref[:, dc + h * HEAD_DIM:dc + (h + 1) * HEAD_DIM].astype(bf16)
    s = _dot(qn, kn, NT) * (1.0 / math.sqrt(HEAD_DIM))
    e = jnp.exp(s - jnp.max(s, axis=-1, keepdims=True))
    p = e / jnp.sum(e, axis=-1, keepdims=True)
    o = _dot(p.astype(bf16), vh)
    return sl, rq, qhat, qn, kn, vh, p, o


def _xattn_fwd(proj, kv, qg, kg, l, y_prev, col0, tq=512):
    S = proj.shape[0]
    D = y_prev.shape[1]
    dc = N_HEADS * HEAD_DIM
    M = kv.shape[0]

    def body(q_ref, z_ref, kv_ref, qg_ref, kg_ref, yp_ref, y_ref):
        for h in range(N_HEADS):
            sl, _, _, _, _, _, _, o = _xattn_head(q_ref, kv_ref, qg_ref[...], kg_ref[...], h)
            z = z_ref[:, sl]
            y_ref[:, sl] = (o * (z * jax.nn.sigmoid(z))).astype(bf16)

    full = lambda shp: pl.BlockSpec(shp, lambda i: (0,) * len(shp))
    return pl.pallas_call(
        body, name="xattn_fwd", grid=(S // tq,),
        in_specs=[pl.BlockSpec((tq, dc), lambda i: (i, col0)), pl.BlockSpec((tq, dc), lambda i: (i, col0 + 1)),
                  full((M, 2 * dc)), _layer_spec(qg, l), _layer_spec(kg, l), pl.BlockSpec(memory_space=pl.ANY)],
        out_specs=pl.BlockSpec((tq, dc), lambda i: (i, D // dc - 1)),
        out_shape=SDS((S, D), bf16), input_output_aliases={5: 0}, compiler_params=_params(),
    )(proj, proj, kv, qg, kg, y_prev)


def _xattn_bwd(proj, kv, qg, kg, l, dy, dproj_prev, col0, tq=512):
    S = proj.shape[0]
    D = dy.shape[1]
    dc = N_HEADS * HEAD_DIM
    M = kv.shape[0]

    def body(q_ref, z_ref, kv_ref, qg_ref, kg_ref, dy_ref, dpp_ref, dp_ref, dkn_ref, dv_ref, dqg_ref):
        @pl.when(pl.program_id(0) == 0)
        def _():
            dkn_ref[...] = jnp.zeros_like(dkn_ref)
            dv_ref[...] = jnp.zeros_like(dv_ref)
            dqg_ref[...] = jnp.zeros_like(dqg_ref)

        qg = qg_ref[...]
        for h in range(N_HEADS):
            sl, rq, qhat, qn, kn, vh, p, o = _xattn_head(q_ref, kv_ref, qg, kg_ref[...], h)
            s, gs = _silu_and_grad(z_ref[:, sl])
            dyh = dy_ref[:, sl]
            dp_ref[:, dc + h * HEAD_DIM:dc + (h + 1) * HEAD_DIM] = (dyh * o * gs).astype(bf16)
            dob = (dyh * s).astype(bf16)
            dpr = _dot(dob, vh, NT)
            dv_ref[:, sl] += _dot(p.astype(bf16), dob, TN)
            ds = (p * (dpr - jnp.sum(p * dpr, axis=-1, keepdims=True)) * (1.0 / math.sqrt(HEAD_DIM))).astype(bf16)
            dqn = _dot(ds, kn)
            dkn_ref[:, sl] += _dot(ds, qn, TN)
            dqg_ref[...] += jnp.sum(dqn * qhat, axis=0, keepdims=True)
            dqhat = dqn * qg
            dp_ref[:, sl] = (rq * (dqhat - qhat * jnp.mean(dqhat * qhat, axis=-1, keepdims=True))).astype(bf16)

    full = lambda shp: pl.BlockSpec(shp, lambda i: (0,) * len(shp))
    return pl.pallas_call(
        body, name="xattn_bwd", grid=(S // tq,),
        in_specs=[pl.BlockSpec((tq, dc), lambda i: (i, col0)), pl.BlockSpec((tq, dc), lambda i: (i, col0 + 1)),
                  full((M, 2 * dc)), _layer_spec(qg, l), _layer_spec(kg, l),
                  pl.BlockSpec((tq, dc), lambda i: (i, D // dc - 1)), pl.BlockSpec(memory_space=pl.ANY)],
        out_specs=[pl.BlockSpec((tq, 2 * dc), lambda i: (i, col0 // 2)), full((M, dc)), full((M, dc)),
                   full((1, HEAD_DIM))],
        out_shape=[SDS(dproj_prev.shape, bf16), SDS((M, dc), f32), SDS((M, dc), f32), SDS((1, HEAD_DIM), f32)],
        input_output_aliases={6: 0}, compiler_params=_params(),
    )(proj, proj, kv, qg, kg, dy, dproj_prev)


def _mem_bwd(mem, mg, kg, l, kv, dkn, dv, w_kv_g):
    M, D = mem.shape
    rb, n = w_kv_g.shape[1], w_kv_g.shape[2]
    dc = n // 2

    def body(m_ref, g_ref, kv_ref, dkn_ref, dv_ref, kg_ref, w_ref, dw_ref, dmg_ref, dkg_ref, dkv_ref):
        mv = m_ref[...]
        r = lax.rsqrt(jnp.mean(mv * mv, axis=-1, keepdims=True) + EPS)
        mhat = mv * r
        mh = (mhat * g_ref[...]).astype(bf16)
        kg = kg_ref[...]
        dkg = jnp.zeros((1, HEAD_DIM), f32)
        for h in range(N_HEADS):
            sl = slice(h * HEAD_DIM, (h + 1) * HEAD_DIM)
            kh = kv_ref[:, sl]
            rk = lax.rsqrt(jnp.mean(kh * kh, axis=-1, keepdims=True) + EPS)
            khat = kh * rk
            dkn_h = dkn_ref[:, sl]
            dkg = dkg + jnp.sum(dkn_h * khat, axis=0, keepdims=True)
            dkhat = dkn_h * kg
            dkv_ref[:, sl] = (rk * (dkhat - khat * jnp.mean(dkhat * khat, axis=-1, keepdims=True))).astype(bf16)
        dkv_ref[:, dc:] = dv_ref[...].astype(bf16)
        dkg_ref[...] = dkg
        dkv = dkv_ref[...]
        dw_ref[...] = _dot(mh, dkv, TN).astype(bf16).reshape(N_DEV, rb, n)
        dmh = _dot(dkv, w_ref[...].reshape(N_DEV * rb, n), NT)
        dmg_ref[...] = jnp.sum(dmh * mhat, axis=0, keepdims=True)

    full = lambda shp: pl.BlockSpec(shp, lambda i: (0,) * len(shp))
    wspec = full((N_DEV, rb, n))
    return pl.pallas_call(
        body, name="mem_bwd", grid=(1,),
        in_specs=[full((M, D)), _layer_spec(mg, l), full((M, n)), full((M, dc)), full((M, dc)), _layer_spec(kg, l), wspec],
        out_specs=[wspec, full((1, D)), full((1, HEAD_DIM))],
        out_shape=[SDS((N_DEV, rb, n), bf16), SDS((1, D), f32), SDS((1, HEAD_DIM), f32)],
        scratch_shapes=[pltpu.VMEM((M, n), bf16)], compiler_params=_params(),
    )(mem, mg, kv, dkn, dv, kg, w_kv_g)


SMALL = ("norm_g", "sgu_ln_g", "sgu_ln_b", "sgu_w", "sgu_b", "mem_norm_g", "q_norm_g", "k_norm_g")


def _small_rows(like):
    rows = [math.prod(like[n].shape) // 128 for n in SMALL]
    offs = [0]
    for r in rows:
        offs.append(offs[-1] + -(-r // 8) * 8)
    return rows, offs


def _pack_small(parts, offs):
    pieces = []
    for k, n in enumerate(SMALL):
        a = parts[n].reshape(-1, 128)
        pieces.append(jnp.pad(a, ((0, offs[k + 1] - offs[k] - a.shape[0]), (0, 0))))
    return jnp.concatenate(pieces)


def kernel(x, mem, norm_g, w_in, sgu_ln_g, sgu_ln_b, sgu_w, sgu_b, mem_norm_g, w_mem_kv, q_norm_g, k_norm_g, w_out, loss_target, m_norm_g, m_w_in, m_sgu_ln_g, m_sgu_ln_b, m_sgu_w, m_sgu_b, m_mem_norm_g, m_w_mem_kv, m_q_norm_g, m_k_norm_g, m_w_out, v_norm_g, v_w_in, v_sgu_ln_g, v_sgu_ln_b, v_sgu_w, v_sgu_b, v_mem_norm_g, v_w_mem_kv, v_q_norm_g, v_k_norm_g, v_w_out):
    L, D, wc = w_in.shape
    S = x.shape[1]
    da = D // 2
    xs = x.reshape(S, D)
    mems = mem.reshape(mem.shape[1], D)
    tgt = loss_target.reshape(S, D)
    stacked = lambda a: a.reshape(a.shape[0], 1, -1)
    ng, lng, lnb, mg, qg, kg = map(stacked, (norm_g, sgu_ln_g, sgu_ln_b, mem_norm_g, q_norm_g, k_norm_g))
    b_t = jnp.swapaxes(sgu_b, 1, 2)
    sb_col, xa_col = 3 * da // HEAD_DIM, (3 * da + D) // (D // 4)

    ax, ay, ac = lax.axis_index("x"), lax.axis_index("y"), lax.axis_index("c")
    ids = jnp.stack([4 * ax + 2 * ay + ac, 2 * ax + ay, ac]).astype(jnp.int32)
    w_in0_b = _cast_into_slot("cast_w_in", w_in, 0, 512, ids, ids)
    first = _gather3_start("gather_w_in0", [w_in0_b], ids)
    late = first["token"]
    w_b = [(w_in0_b if l == 0 else _cast_into_slot("cast_w_in", w_in, l, 512, ids, late),
            _cast_into_slot("cast_w_kv", w_mem_kv, l, 256, ids, late),
            _cast_into_slot("cast_w_out", w_out, l, 256, ids, late)) for l in range(L)]
    relay = _gather3_relay("gather_w_in0_relay",
                           _split_wait(first, *[a for wl in w_b for a in wl if a is not w_in0_b]), ids)
    in_fwd = _gather2_forward("gather_w_in0_forward", _split_wait(relay, relay["token"]), ids)

    acts = []
    xl = xs
    for l in range(L):
        last = l + 1 == L
        if l == 0:
            (w_in_g,) = _split_wait(in_fwd, in_fwd["token"])
        rest = _gather3_start(f"gather_w_rest{l}", [w_b[l][1], w_b[l][2]] if last else [w_b[l][1]], w_in_g)
        order = rest["token"]
        if not last:
            nxt = _gather3_start(f"gather_w_in{l + 1}", [w_b[l + 1][0], w_b[l][2]], order)
            order = nxt["token"]
        proj, h = _rms_proj(xl, ng, l, w_in_g, order)
        rest_relay = _gather3_relay(f"gather_w_rest{l}_relay", _split_wait(rest, proj), proj)
        y = _sgu_fwd(proj, lng, lnb, sgu_w, b_t, l, rest_relay["token"])
        rest_fwd = _gather2_forward(f"gather_w_rest{l}_forward", _split_wait(rest_relay, y), y)
        order = rest_fwd["token"]
        if l + 1 < L:
            nxt_relay = _gather3_relay(f"gather_w_in{l + 1}_relay", _split_wait(nxt, order), order)
            order = nxt_relay["token"]
        y, o_b, car = _sb_fwd(proj, y, sb_col, order)
        got = _split_wait(rest_fwd, o_b)
        w_kv_g = got[0]
        if not last:
            in_fwd = _gather2_forward(f"gather_w_in{l + 1}_forward", _split_wait(nxt_relay, o_b), o_b)
        kv = _mem_kv(mems, mg, l, w_kv_g)
        y = _xattn_fwd(proj, kv, qg, kg, l, y, xa_col)
        if last:
            w_out_g = got[1]
            dx, dxb, loss_part = _out_proj_loss(xl, y, w_out_g, tgt)
            acts.append((xl, proj, h, y, o_b, car, kv, w_in_g, w_kv_g, w_out_g))
        else:
            w_in_next, w_out_g = _split_wait(in_fwd, y)
            acts.append((xl, proj, h, y, o_b, car, kv, w_in_g, w_kv_g, w_out_g))
            xl = _out_proj(xl, y, w_out_g, kv)
            w_in_g = w_in_next
    loss = lax.psum(loss_part[0, 0], ("x", "y", "c"))

    weights = dict(norm_g=norm_g, sgu_ln_g=sgu_ln_g, sgu_ln_b=sgu_ln_b, sgu_w=sgu_w, sgu_b=sgu_b,
                   mem_norm_g=mem_norm_g, q_norm_g=q_norm_g, k_norm_g=k_norm_g)
    moms_m = dict(norm_g=m_norm_g, sgu_ln_g=m_sgu_ln_g, sgu_ln_b=m_sgu_ln_b, sgu_w=m_sgu_w, sgu_b=m_sgu_b,
                  mem_norm_g=m_mem_norm_g, q_norm_g=m_q_norm_g, k_norm_g=m_k_norm_g)
    moms_v = dict(norm_g=v_norm_g, sgu_ln_g=v_sgu_ln_g, sgu_ln_b=v_sgu_ln_b, sgu_w=v_sgu_w, sgu_b=v_sgu_b,
                  mem_norm_g=v_mem_norm_g, q_norm_g=v_q_norm_g, k_norm_g=v_k_norm_g)
    small_rows, small_offs = _small_rows(weights)
    head_rows = D // 128
    assert SMALL[0] == "norm_g" and head_rows % 8 == 0

    seconds = {}
    pending = None
    adam = {"w_out": None, "w_mem_kv": None, "w_in": None}

    late_in = {}

    def update(lu, order):
        (r_out,) = _scatter_finish(seconds[f"g_out{lu}"], order)
        adam["w_out"] = _sum_adam("adam_w_out", r_out, w_out, m_w_out, v_w_out, lu, adam["w_out"], 128, ids, ids)
        (r_kv,) = _scatter_finish(seconds[f"g_kv{lu}"], adam["w_out"][0])
        (late_in[lu],) = _scatter_finish(seconds[f"g_in{lu}"], adam["w_out"][0])
        adam["w_mem_kv"] = _sum_adam("adam_w_kv", r_kv, w_mem_kv, m_w_mem_kv, v_w_mem_kv, lu, adam["w_mem_kv"], 256, ids, ids)
        return adam["w_mem_kv"][0]

    def update_in(lu, order):
        adam["w_in"] = _sum_adam("adam_w_in", late_in[lu], w_in, m_w_in, v_w_in, lu, adam["w_in"], 256, ids, order)
        return adam["w_in"][0]
    small = {n: [None] * L for n in SMALL}
    for l in reversed(range(L)):
        xl, proj, h, y, o_b, car, kv, w_in_g, w_kv_g, w_out_g = acts[l]
        dy = _out_bwd_dy(dxb, w_out_g)
        order = dy
        if pending is not None:
            seconds[pending[0]] = _scatter2_second_level(pending[0], pending[1], dy, ids)
            order = seconds[pending[0]]["token"]
        g_out = _tn_grad("out_bwd_dw", y, dxb, 512, 512, True, order)
        seconds[f"g_out{l}"] = _scatter1_start(f"scatter_g_out{l}", [g_out], ids)
        dproj, d_sw, d_sb, d_lg, d_lb = _sgu_bwd(proj, dy, lng, lnb, sgu_w, b_t, l, seconds[f"g_out{l}"]["token"])
        dproj = _sb_bwd(proj, o_b, car, dy, dproj, sb_col, d_lb)
        dproj, dkn, dv, d_qg = _xattn_bwd(proj, kv, qg, kg, l, dy, dproj, xa_col)
        g_kv, d_mg, d_kg = _mem_bwd(mems, mg, kg, l, kv, dkn, dv, w_kv_g)
        seconds[f"g_kv{l}"] = _scatter1_start(f"scatter_g_kv{l}", [g_kv], ids)
        for n, val in (("sgu_ln_g", d_lg), ("sgu_ln_b", d_lb), ("sgu_w", d_sw), ("sgu_b", d_sb[:, :A_GROUPS].T),
                       ("mem_norm_g", d_mg), ("q_norm_g", d_qg), ("k_norm_g", d_kg)):
            small[n][l] = val.reshape(-1)
        order = seconds[f"g_kv{l}"]["token"]
        if l == 0:
            small["norm_g"][0] = jnp.zeros_like(small["norm_g"][1])
            part = _pack_small({n: jnp.stack(small[n]) for n in SMALL}, small_offs)
            tail = _gather1_start("gather_small_tail", [_into_slot("small_tail_slot", part[head_rows:], ids)], order)
            order = tail["token"]
        g_in_l = _tn_grad("in_bwd_dw", h, dproj, D, wc, False, order)
        first = _scatter2_pair_start(f"scatter_g_in{l}_pair", [g_in_l], ids)
        order = first["token"]
        pending = (f"g_in{l}", first)
        if l == 0:
            for lu in reversed(range(1, L)):
                order = update(lu, order)
            seconds[pending[0]] = _scatter2_second_level(pending[0], pending[1], order, ids)
            order = seconds[pending[0]]["token"]
            for lu in reversed(range(1, L)):
                order = update_in(lu, order)
        dh = _in_bwd_dh(dproj, w_in_g, order)
        dx, dxb, d_ng = _rms_bwd(dh, xl, ng, l, dx, order)
        small["norm_g"][l] = d_ng.reshape(-1)
    head = _gather1_start("gather_small_head",
                          [_into_slot("small_head_slot", small["norm_g"][0].reshape(head_rows, 128), ids)], ids)

    (r_out,) = _scatter_finish(seconds["g_out0"], head["token"])
    adam["w_out"] = _sum_adam("adam_w_out", r_out, w_out, m_w_out, v_w_out, 0, adam["w_out"], 128, ids, ids)
    (r_kv,) = _scatter_finish(seconds["g_kv0"], adam["w_out"][0])
    (r_in,) = _scatter_finish(seconds["g_in0"], adam["w_out"][0])
    adam["w_mem_kv"] = _sum_adam("adam_w_kv", r_kv, w_mem_kv, m_w_mem_kv, v_w_mem_kv, 0, adam["w_mem_kv"], 256, ids, ids)
    adam["w_in"] = _sum_adam("adam_w_in", r_in, w_in, m_w_in, v_w_in, 0, adam["w_in"], 256, ids, ids)
    (r_tail,) = _split_wait(tail, adam["w_in"][0])
    (r_head,) = _split_wait(head, r_tail)
    as128 = lambda d: [d[n].reshape(-1, 128) for n in SMALL]
    sm = _small_sum_adam(r_head, r_tail, as128(weights), as128(moms_m), as128(moms_v), small_offs)
    res = dict(adam)
    for p, n in enumerate(SMALL):
        res[n] = [sm[k * len(SMALL) + p].reshape(weights[n].shape) for k in range(4)]

    order = ("norm_g", "w_in", "sgu_ln_g", "sgu_ln_b", "sgu_w", "sgu_b", "mem_norm_g", "w_mem_kv", "q_norm_g",
             "k_norm_g", "w_out")
    outs = [loss, dx.reshape(x.shape)]
    for k in range(4):
        outs += [res[n][k] for n in order]
    return tuple(outs)
```

```python
import functools
import math

import jax
import jax.numpy as jnp
from jax import lax
from jax.experimental import pallas as pl
from jax.experimental.pallas import tpu as pltpu

f32 = jnp.float32
bf16 = jnp.bfloat16
SDS = jax.ShapeDtypeStruct

N_DEV = 8
EPS = 1e-6
CHUNK = 128
SGU_CHUNKS = 2
A_GROUPS = 8
HEAD_DIM = 128
N_HEADS = 4
TQ = 256
TK = 256
CARRY_LANES = 128
ADAM_LR, ADAM_B1, ADAM_B2, ADAM_EPS, ADAM_WD, ADAM_STEP = 0.001, 0.9, 0.999, 1e-08, 0.01, 10
MIB = 1024 * 1024

NT = (((1,), (1,)), ((), ()))
TN = (((0,), (0,)), ((), ()))


def _params(vmem_mib=48):
    return pltpu.CompilerParams(vmem_limit_bytes=vmem_mib * MIB)


def _gelu_and_grad(x):
    e = lax.erf(x * (1.0 / math.sqrt(2.0)))
    cdf = 0.5 * (1.0 + e)
    pdf = jnp.exp(-0.5 * x * x) * (1.0 / math.sqrt(2.0 * math.pi))
    return x * cdf, cdf + x * pdf


def _gelu(x):
    return 0.5 * x * (1.0 + lax.erf(x * (1.0 / math.sqrt(2.0))))


def _silu_and_grad(z):
    sg = jax.nn.sigmoid(z)
    return z * sg, sg * (1.0 + z * (1.0 - sg))


def _layer_spec(stacked, l):
    rest = stacked.shape[1:]
    return pl.BlockSpec((None,) + rest, lambda *idx: (l,) + (0,) * len(rest))


def _dot(a, b, dims=None):
    if dims is None:
        return jnp.dot(a, b, preferred_element_type=f32)
    return lax.dot_general(a, b, dims, preferred_element_type=f32)


_HBM = pl.BlockSpec(memory_space=pltpu.HBM)
_SEM = pl.BlockSpec(memory_space=pltpu.SEMAPHORE)
_EFFECT = pltpu.SideEffectType.DATAFLOW_SIDE_EFFECTING


def _split_start(name, bufs, n_remote, n_local, build, after):
    nb = len(bufs)

    def body(*refs):
        token = refs[-1]
        locals_, remotes = build(refs[:nb], *refs[nb + 1:nb + 4])
        for cp in locals_ + remotes:
            cp.start()
        token[...] = jnp.zeros_like(token)

    hbm = lambda a: pltpu.with_memory_space_constraint(a, pltpu.HBM)
    outs = pl.pallas_call(
        body, name=name,
        out_shape=(pltpu.SemaphoreType.DMA((n_remote,)), pltpu.SemaphoreType.DMA((n_remote,)),
                   pltpu.SemaphoreType.DMA((max(n_local, 1),)),
                   *[pltpu.HBM(b.shape, b.dtype) for b in bufs], SDS((8, 128), f32)),
        in_specs=[_HBM] * nb + [pl.BlockSpec(memory_space=pl.ANY)],
        out_specs=(_SEM, _SEM, _SEM, *[_HBM] * nb, pl.BlockSpec(memory_space=pltpu.VMEM)),
        input_output_aliases={k: 3 + k for k in range(nb)},
        compiler_params=pltpu.CompilerParams(has_side_effects=_EFFECT),
    )(*[hbm(b) for b in bufs], after)
    return dict(name=name, build=build, sems=outs[:3], bufs=outs[3:3 + nb], token=outs[-1])


def _split_wait(handle, *after):
    build, bufs = handle["build"], handle["bufs"]
    nb = len(bufs)

    def body(*refs):
        locals_, remotes = build(refs[:nb], *refs[nb:nb + 3])
        for cp in remotes:
            cp.wait_recv()
        for cp in remotes:
            cp.wait_send()
        for cp in locals_:
            cp.wait()

    outs = pl.pallas_call(
        body, name=handle["name"] + "_wait",
        out_shape=tuple(pltpu.HBM(b.shape, b.dtype) for b in bufs),
        in_specs=[_HBM] * nb + [_SEM] * 3 + [pl.BlockSpec(memory_space=pl.ANY)] * len(after),
        out_specs=tuple([_HBM] * nb),
        input_output_aliases={k: k for k in range(nb)},
        compiler_params=pltpu.CompilerParams(has_side_effects=_EFFECT),
    )(*bufs, *handle["sems"], *after)
    return list(outs)


def _remote(src, dst, send_sems, recv_sems, k, to):
    return pltpu.make_async_remote_copy(src_ref=src, dst_ref=dst, send_sem=send_sems.at[k], recv_sem=recv_sems.at[k],
                                        device_id=to, device_id_type=pl.DeviceIdType.MESH)


def _other_chips(x, y):
    return [(1 - x, y), (x, 1 - y), (1 - x, 1 - y)]


def _all_peers(x, y, c):
    return [(1 - x if m & 4 else x, 1 - y if m & 2 else y, 1 - c if m & 1 else c) for m in range(1, N_DEV)]


def _gather1_start(name, lands, after):
    def build(refs, send, recv, loc):
        x, y, c = lax.axis_index("x"), lax.axis_index("y"), lax.axis_index("c")
        me = 4 * x + 2 * y + c
        return [], [_remote(d.at[me], d.at[me], send, recv, 7 * a + k, peer)
                    for a, d in enumerate(refs) for k, peer in enumerate(_all_peers(x, y, c))]

    return _split_start(name, list(lands), 7 * len(lands), 0, build, after)


def _scatter1_start(name, srcs, after):
    n = len(srcs)

    def build(refs, send, recv, loc):
        x, y, c = lax.axis_index("x"), lax.axis_index("y"), lax.axis_index("c")
        me = 4 * x + 2 * y + c
        return [], [_remote(refs[a].at[4 * px + 2 * py + pc], refs[n + a].at[me], send, recv, 7 * a + k, (px, py, pc))
                    for a in range(n) for k, (px, py, pc) in enumerate(_all_peers(x, y, c))]

    return _split_start(name, list(srcs) + [lax.empty(s.shape, s.dtype) for s in srcs], 7 * n, 0, build, after)


def _gather2_start(name, lands, after):
    def build(refs, send, recv, loc):
        x, y, c = lax.axis_index("x"), lax.axis_index("y"), lax.axis_index("c")
        me = 4 * x + 2 * y + c
        remotes = []
        for a, d in enumerate(refs):
            remotes.append(_remote(d.at[me], d.at[me], send, recv, 4 * a, (x, y, 1 - c)))
            remotes += [_remote(d.at[me], d.at[me], send, recv, 4 * a + 1 + k, (px, py, c))
                        for k, (px, py) in enumerate(_other_chips(x, y))]
        return [], remotes

    return _split_start(name, list(lands), 4 * len(lands), 0, build, after)


def _gather3_start(name, lands, after):
    def build(refs, send, recv, loc):
        x, y, c = lax.axis_index("x"), lax.axis_index("y"), lax.axis_index("c")
        me = 4 * x + 2 * y + c
        return [], [_remote(d.at[me], d.at[me], send, recv, 3 * a + k, to)
                    for a, d in enumerate(refs) for k, to in enumerate([(x, y, 1 - c), (1 - x, y, c), (x, 1 - y, c)])]

    return _split_start(name, list(lands), 3 * len(lands), 0, build, after)


def _gather3_relay(name, lands, after):
    def build(refs, send, recv, loc):
        x, y, c = lax.axis_index("x"), lax.axis_index("y"), lax.axis_index("c")
        from_x = c == 0
        slot = 4 * jnp.where(from_x, 1 - x, x) + 2 * jnp.where(from_x, y, 1 - y) + c
        to = (jnp.where(from_x, x, 1 - x), jnp.where(from_x, 1 - y, y), c)
        return [], [_remote(d.at[slot], d.at[slot], send, recv, a, to) for a, d in enumerate(refs)]

    return _split_start(name, list(lands), len(lands), 0, build, after)


def _gather2_forward(name, lands, after):
    n = len(lands)

    def build(refs, send, recv, loc):
        x, y, c = lax.axis_index("x"), lax.axis_index("y"), lax.axis_index("c")
        slots = [4 * px + 2 * py + c for px, py in _other_chips(x, y)]
        return [], [_remote(d.at[sl], d.at[sl], send, recv, 3 * a + k, (x, y, 1 - c))
                    for a, d in enumerate(refs) for k, sl in enumerate(slots)]

    return _split_start(name, list(lands), 3 * n, 0, build, after)


def _scatter2_pair_start(name, srcs, after):
    n = len(srcs)

    def build(refs, send, recv, loc):
        x, y, c = lax.axis_index("x"), lax.axis_index("y"), lax.axis_index("c")
        return [], [_remote(refs[a].at[2 * q + 1 - c], refs[n + a].at[q], send, recv, 4 * a + q, (x, y, 1 - c))
                    for a in range(n) for q in range(4)]

    lands = [lax.empty((4,) + s.shape[1:], s.dtype) for s in srcs]
    return _split_start(name, list(srcs) + lands, 4 * n, 0, build, after)


def _scatter2_chip_start(name, pairs, after):
    n = len(pairs)

    def build(refs, send, recv, loc):
        x, y, c = lax.axis_index("x"), lax.axis_index("y"), lax.axis_index("c")
        return [], [_remote(refs[a].at[2 * px + py], refs[n + a].at[2 * x + y], send, recv, 3 * a + k, (px, py, c))
                    for a in range(n) for k, (px, py) in enumerate(_other_chips(x, y))]

    return _split_start(name, list(pairs) + [lax.empty(p.shape, p.dtype) for p in pairs], 3 * n, 0, build, after)


def _pair_sum(name, src, theirs, ids):
    _, R, C = theirs.shape
    tr = min(R, 1024)

    def body(ids_ref, a_ref, b_ref, o_ref):
        o_ref[...] = (a_ref[...].astype(f32) + b_ref[...].astype(f32)).astype(bf16)

    spec = pl.BlockSpec((None, tr, C), lambda q, i, ids: (q, i, 0))
    return pl.pallas_call(
        body, name=name,
        grid_spec=pltpu.PrefetchScalarGridSpec(
            num_scalar_prefetch=1, grid=(4, R // tr),
            in_specs=[pl.BlockSpec((None, tr, C), lambda q, i, ids: (2 * q + ids[2], i, 0)), spec], out_specs=spec),
        out_shape=SDS(theirs.shape, bf16), compiler_params=_params(),
    )(ids, src, theirs)


def _scatter2_second_level(name, first, after, ids):
    outs = _split_wait(first, after)
    n = len(outs) // 2
    pairs = [_pair_sum(f"pair_sum_{name}{a}", outs[a], outs[n + a], ids) for a in range(n)]
    return _scatter2_chip_start(f"scatter_{name}_chip", pairs, ids)


def _scatter_finish(second, after):
    outs = _split_wait(second, after)
    n = len(outs) // 2
    return [(outs[a], outs[n + a]) for a in range(n)]


def _cast_into_slot(name, w, l, tr, ids, after):
    _, R, C = w.shape

    def body(ids_ref, w_ref, after_ref, o_ref):
        o_ref[...] = w_ref[...].astype(bf16)

    return pl.pallas_call(
        body, name=name,
        grid_spec=pltpu.PrefetchScalarGridSpec(
            num_scalar_prefetch=1, grid=(R // tr,),
            in_specs=[pl.BlockSpec((None, tr, C), lambda i, ids: (l, i, 0)), pl.BlockSpec(memory_space=pl.ANY)],
            out_specs=pl.BlockSpec((None, tr, C), lambda i, ids: (ids[0], i, 0))),
        out_shape=SDS((N_DEV, R, C), bf16), compiler_params=_params(),
    )(ids, w, after)


def _into_slot(name, a, ids):
    R, C = a.shape

    def body(ids_ref, a_ref, o_ref):
        o_ref[...] = a_ref[...]

    return pl.pallas_call(
        body, name=name,
        grid_spec=pltpu.PrefetchScalarGridSpec(
            num_scalar_prefetch=1, grid=(1,),
            in_specs=[pl.BlockSpec((R, C), lambda i, ids: (0, 0))],
            out_specs=pl.BlockSpec((None, R, C), lambda i, ids: (ids[0], 0, 0))),
        out_shape=SDS((N_DEV, R, C), f32), compiler_params=_params(),
    )(ids, a)


def _adam_math(w, g, m, v):
    m2 = ADAM_B1 * m + (1.0 - ADAM_B1) * g
    v2 = ADAM_B2 * v + (1.0 - ADAM_B2) * (g * g)
    m_hat = m2 / (1.0 - ADAM_B1 ** ADAM_STEP)
    v_hat = v2 / (1.0 - ADAM_B2 ** ADAM_STEP)
    delta = -ADAM_LR * (m_hat / (jnp.sqrt(v_hat) + ADAM_EPS) + ADAM_WD * w)
    return delta, m2, v2


def _sum_adam(name, pair_recv, w, m, v, l, prev, tr, ids, after):
    own, recv = pair_recv
    L, R, C = w.shape
    slots = recv.shape[0]
    mine = 0 if slots == N_DEV else 1

    def body(ids_ref, r_ref, own_ref, w_ref, m_ref, v_ref, after_ref, *rest):
        g_ref, d_ref, m2_ref, v2_ref = rest[-4:]
        terms = [jnp.where(ids_ref[mine] == q, own_ref[...], r_ref[q]).astype(f32) for q in range(slots)]
        g = terms[0]
        for t in terms[1:]:
            g = g + t
        d, m2, v2 = _adam_math(w_ref[...], g, m_ref[...], v_ref[...])
        g_ref[...] = g
        d_ref[...] = d
        m2_ref[...] = m2
        v2_ref[...] = v2

    wspec = pl.BlockSpec((None, tr, C), lambda i, ids: (l, i, 0))
    in_specs = [pl.BlockSpec((slots, tr, C), lambda i, ids: (0, i, 0)),
                pl.BlockSpec((None, tr, C), lambda i, ids: (ids[mine], i, 0)), wspec, wspec, wspec,
                pl.BlockSpec(memory_space=pl.ANY)]
    args = [ids, recv, own, w, m, v, after]
    aliases = {}
    if prev is not None:
        in_specs += [pl.BlockSpec(memory_space=pl.ANY)] * 4
        args += list(prev)
        aliases = {7 + k: k for k in range(4)}
    return pl.pallas_call(
        body, name=name,
        grid_spec=pltpu.PrefetchScalarGridSpec(num_scalar_prefetch=1, grid=(R // tr,), in_specs=in_specs,
                                               out_specs=[wspec] * 4),
        out_shape=[SDS((L, R, C), f32)] * 4, input_output_aliases=aliases, compiler_params=_params(),
    )(*args)


def _small_sum_adam(recv_head, recv_tail, ws, ms, vs, offs):
    n = len(ws)
    r0 = recv_head.shape[1]

    def body(*refs):
        rh, rt = refs[0], refs[1]
        w_refs, m_refs, v_refs = refs[2:2 + n], refs[2 + n:2 + 2 * n], refs[2 + 2 * n:2 + 3 * n]
        outs = refs[2 + 3 * n:]
        for p in range(n):
            lo, hi = offs[p], offs[p] + ws[p].shape[0]
            pieces = []
            if lo < r0:
                pieces.append((rh, lo, 0, min(hi, r0) - lo))
            if hi > r0:
                pieces.append((rt, max(lo, r0) - r0, max(lo, r0) - lo, hi - max(lo, r0)))
            for src, a, b, cnt in pieces:
                g = src[0, a:a + cnt, :]
                for s in range(1, N_DEV):
                    g = g + src[s, a:a + cnt, :]
                d, m2, v2 = _adam_math(w_refs[p][b:b + cnt, :], g, m_refs[p][b:b + cnt, :], v_refs[p][b:b + cnt, :])
                for k, val in enumerate((g, d, m2, v2)):
                    outs[k * n + p][b:b + cnt, :] = val

    return pl.pallas_call(
        body, name="small_sum_adam", out_shape=[SDS(w.shape, f32) for w in ws] * 4, compiler_params=_params(),
    )(recv_head, recv_tail, *ws, *ms, *vs)


def _rms_proj(x, g, l, w_in_g, after, tm=1024):
    S, D = x.shape
    wc = w_in_g.shape[2]
    n_out = N_DEV * wc

    def body(x_ref, g_ref, w_ref, after_ref, proj_ref, h_ref):
        @pl.when(pl.program_id(1) == 0)
        def _():
            xv = x_ref[...]
            r = lax.rsqrt(jnp.mean(xv * xv, axis=-1, keepdims=True) + EPS)
            h_ref[...] = (xv * r * g_ref[...]).astype(bf16)

        proj_ref[...] = _dot(h_ref[...], w_ref[...])

    return pl.pallas_call(
        body, name="rms_proj", grid=(S // tm, N_DEV),
        in_specs=[pl.BlockSpec((tm, D), lambda i, j: (i, 0)), _layer_spec(g, l),
                  pl.BlockSpec((None, D, wc), lambda i, j: (j, 0, 0)), pl.BlockSpec(memory_space=pl.ANY)],
        out_specs=[pl.BlockSpec((tm, wc), lambda i, j: (i, j)), pl.BlockSpec((tm, D), lambda i, j: (i, 0))],
        out_shape=[SDS((S, n_out), f32), SDS((S, D), bf16)], compiler_params=_params(),
    )(x, g, w_in_g, after)


def _out_proj(x, y, w_out_g, after, tm=512):
    S, D = x.shape
    rb = w_out_g.shape[1]

    def body(x_ref, y_ref, w_ref, after_ref, o_ref):
        w = w_ref[...].reshape(N_DEV * rb, D)
        o_ref[...] = x_ref[...] + _dot(y_ref[...], w)

    row = pl.BlockSpec((tm, D), lambda i: (i, 0))
    return pl.pallas_call(
        body, name="out_proj", grid=(S // tm,),
        in_specs=[row, row, pl.BlockSpec((N_DEV, rb, D), lambda i: (0, 0, 0)), pl.BlockSpec(memory_space=pl.ANY)],
        out_specs=row, out_shape=SDS((S, D), f32), compiler_params=_params(),
    )(x, y, w_out_g, after)


def _out_proj_loss(x, y, w_out_g, tgt, tm=256):
    S, D = x.shape
    rb = w_out_g.shape[1]

    def body(x_ref, y_ref, w_ref, t_ref, dx_ref, dxb_ref, l_ref):
        i = pl.program_id(0)
        w = w_ref[...].reshape(N_DEV * rb, D)
        d = x_ref[...] + _dot(y_ref[...], w) - t_ref[...]
        dx = d * (1.0 / D)
        dx_ref[...] = dx
        dxb_ref[...] = dx.astype(bf16)
        e = d * d
        part = e[:, 0:128]
        for k in range(1, D // 128):
            part = part + e[:, k * 128:(k + 1) * 128]
        part = jnp.sum(part.reshape(tm // 8, 8, 128), axis=0)

        @pl.when(i == 0)
        def _():
            l_ref[...] = jnp.zeros_like(l_ref)

        l_ref[...] += part

        @pl.when(i == pl.num_programs(0) - 1)
        def _():
            tot = jnp.sum(l_ref[...], axis=1, keepdims=True)
            tot = jnp.sum(tot, axis=0, keepdims=True)
            l_ref[...] = jnp.broadcast_to(tot * (0.5 / D), l_ref.shape)

    row = pl.BlockSpec((tm, D), lambda i: (i, 0))
    return pl.pallas_call(
        body, name="out_proj_loss", grid=(S // tm,),
        in_specs=[row, row, pl.BlockSpec((N_DEV, rb, D), lambda i: (0, 0, 0)), row],
        out_specs=[row, row, pl.BlockSpec((8, 128), lambda i: (0, 0))],
        out_shape=[SDS((S, D), f32), SDS((S, D), bf16), SDS((8, 128), f32)], compiler_params=_params(),
    )(x, y, w_out_g, tgt)


def _out_bwd_dy(dxb, w_out_g, tm=512):
    S, D = dxb.shape
    rb = w_out_g.shape[1]

    nb = 2

    def body(dx_ref, w_ref, o_ref):
        o_ref[...] = _dot(dx_ref[...], w_ref[...].reshape(nb * rb, D), NT)

    return pl.pallas_call(
        body, name="out_bwd_dy", grid=(S // tm, N_DEV // nb),
        in_specs=[pl.BlockSpec((tm, D), lambda i, j: (i, 0)),
                  pl.BlockSpec((nb, rb, D), lambda i, j: (j, 0, 0))],
        out_specs=pl.BlockSpec((tm, nb * rb), lambda i, j: (i, j)),
        out_shape=SDS((S, D), f32), compiler_params=_params(),
    )(dxb, w_out_g)


def _tn_grad(name, a, b, tm, tn, rows_major, after):
    S, M = a.shape
    N = b.shape[1]
    if rows_major:
        rb = M // N_DEV
        nb = tm // rb
        out_shape = SDS((N_DEV, rb, N), bf16)
        out_spec = pl.BlockSpec((nb, rb, tn), lambda i, j: (i, 0, j))
    else:
        out_shape = SDS((N_DEV, M, N // N_DEV), bf16)
        assert tn == N // N_DEV
        out_spec = pl.BlockSpec((None, tm, tn), lambda i, j: (j, i, 0))

    def body(a_ref, b_ref, after_ref, o_ref):
        o_ref[...] = _dot(a_ref[...], b_ref[...], TN).astype(bf16).reshape(o_ref.shape)

    return pl.pallas_call(
        body, name=name, grid=(M // tm, N // tn),
        in_specs=[pl.BlockSpec((S, tm), lambda i, j: (0, i)), pl.BlockSpec((S, tn), lambda i, j: (0, j)),
                  pl.BlockSpec(memory_space=pl.ANY)],
        out_specs=out_spec, out_shape=out_shape, compiler_params=_params(),
    )(a, b, after)


def _in_bwd_dh(dproj, w_in_g, after, tm=1024, tn=256):
    S = dproj.shape[0]
    _, D, wc = w_in_g.shape
    tm = min(tm, S)

    def body(dp_ref, w_ref, after_ref, o_ref):
        acc = _dot(dp_ref[:, 0:wc], w_ref[0], NT)
        for k in range(1, N_DEV):
            acc = acc + _dot(dp_ref[:, k * wc:(k + 1) * wc], w_ref[k], NT)
        o_ref[...] = acc

    return pl.pallas_call(
        body, name="in_bwd_dh", grid=(S // tm, D // tn),
        in_specs=[pl.BlockSpec((tm, N_DEV * wc), lambda i, j: (i, 0)),
                  pl.BlockSpec((N_DEV, tn, wc), lambda i, j: (0, j, 0)), pl.BlockSpec(memory_space=pl.ANY)],
        out_specs=pl.BlockSpec((tm, tn), lambda i, j: (i, j)),
        out_shape=SDS((S, D), f32), compiler_params=_params(),
    )(dproj, w_in_g, after)


def _rms_bwd(dh, x, g, l, dx_next, after, tm=256):
    S, D = x.shape

    def body(dh_ref, x_ref, g_ref, dxn_ref, after_ref, dx_ref, dxb_ref, dg_ref):
        @pl.when(pl.program_id(0) == 0)
        def _():
            dg_ref[...] = jnp.zeros_like(dg_ref)

        dh = dh_ref[...]
        xv = x_ref[...]
        r = lax.rsqrt(jnp.mean(xv * xv, axis=-1, keepdims=True) + EPS)
        xhat = xv * r
        dxhat = dh * g_ref[...]
        dx = r * (dxhat - xhat * jnp.mean(dxhat * xhat, axis=-1, keepdims=True)) + dxn_ref[...]
        dx_ref[...] = dx
        dxb_ref[...] = dx.astype(bf16)
        dg_ref[...] += jnp.sum(dh * xhat, axis=0, keepdims=True)

    row = pl.BlockSpec((tm, D), lambda i: (i, 0))
    vec = pl.BlockSpec((1, D), lambda i: (0, 0))
    return pl.pallas_call(
        body, name="rms_bwd", grid=(S // tm,),
        in_specs=[row, row, _layer_spec(g, l), row, pl.BlockSpec(memory_space=pl.ANY)], out_specs=[row, row, vec],
        out_shape=[SDS((S, D), f32), SDS((S, D), bf16), SDS((1, D), f32)], compiler_params=_params(),
    )(dh, x, g, dx_next, after)


def _sgu_fwd(proj, ln_g, ln_b, w_s, b_t, l, after):
    S = proj.shape[0]
    da = A_GROUPS * HEAD_DIM
    D = 2 * da

    def body(u_ref, v_ref, z_ref, lg_ref, lb_ref, w_ref, bt_ref, after_ref, y_ref):
        u = _gelu(u_ref[...])
        v = _gelu(v_ref[...])
        z = z_ref[...]
        mu = jnp.mean(v, axis=-1, keepdims=True)
        xc = v - mu
        rs = lax.rsqrt(jnp.mean(xc * xc, axis=-1, keepdims=True) + EPS)
        vn = (xc * rs * lg_ref[...] + lb_ref[...]).astype(bf16)
        gate = u * (z * jax.nn.sigmoid(z))
        tri = lax.broadcasted_iota(jnp.int32, (CHUNK, CHUNK), 0) >= lax.broadcasted_iota(jnp.int32, (CHUNK, CHUNK), 1)
        for g in range(A_GROUPS):
            sl = slice(g * HEAD_DIM, (g + 1) * HEAD_DIM)
            wm = jnp.where(tri, w_ref[g], 0.0).astype(bf16)
            for ck in range(SGU_CHUNKS):
                rw = slice(ck * CHUNK, (ck + 1) * CHUNK)
                mixed = _dot(wm, vn[rw, sl]) + bt_ref[:, g:g + 1]
                y_ref[rw, sl] = (gate[rw, sl] * mixed).astype(bf16)

    blk = lambda cb: pl.BlockSpec((SGU_CHUNKS * CHUNK, da), lambda c: (c, cb))
    full = lambda shp: pl.BlockSpec(shp, lambda c: (0,) * len(shp))
    return pl.pallas_call(
        body, name="sgu_fwd", grid=(S // (SGU_CHUNKS * CHUNK),),
        in_specs=[blk(0), blk(1), blk(2), _layer_spec(ln_g, l), _layer_spec(ln_b, l), _layer_spec(w_s, l),
                  _layer_spec(b_t, l), pl.BlockSpec(memory_space=pl.ANY)],
        out_specs=blk(0), out_shape=SDS((S, D), bf16), compiler_params=_params(),
    )(proj, proj, proj, ln_g, ln_b, w_s, b_t, after)


def _sgu_bwd(proj, dy, ln_g, ln_b, w_s, b_t, l, after):
    S = proj.shape[0]
    da = A_GROUPS * HEAD_DIM
    n_proj = proj.shape[1]

    def body(u_ref, v_ref, z_ref, dy_ref, lg_ref, lb_ref, w_ref, bt_ref, after_ref,
             dp_ref, dw_ref, db_ref, dlg_ref, dlb_ref, dvn_ref):
        @pl.when(pl.program_id(0) == 0)
        def _():
            dw_ref[...] = jnp.zeros_like(dw_ref)
            db_ref[...] = jnp.zeros_like(db_ref)
            dlg_ref[...] = jnp.zeros_like(dlg_ref)
            dlb_ref[...] = jnp.zeros_like(dlb_ref)

        up, vp, z, dy = u_ref[...], v_ref[...], z_ref[...], dy_ref[...]
        u, gu = _gelu_and_grad(up)
        v, gv = _gelu_and_grad(vp)
        s, gs = _silu_and_grad(z)
        mu = jnp.mean(v, axis=-1, keepdims=True)
        xc = v - mu
        rs = lax.rsqrt(jnp.mean(xc * xc, axis=-1, keepdims=True) + EPS)
        vhat = xc * rs
        lg = lg_ref[...]
        vn = (vhat * lg + lb_ref[...]).astype(bf16)
        tri = lax.broadcasted_iota(jnp.int32, (CHUNK, CHUNK), 0) >= lax.broadcasted_iota(jnp.int32, (CHUNK, CHUNK), 1)
        lane = lax.broadcasted_iota(jnp.int32, (CHUNK, HEAD_DIM), 1)
        dys = dy * s
        db = jnp.zeros((CHUNK, HEAD_DIM), f32)
        for g in range(A_GROUPS):
            sl = slice(g * HEAD_DIM, (g + 1) * HEAD_DIM)
            zl = slice(2 * da + g * HEAD_DIM, 2 * da + (g + 1) * HEAD_DIM)
            wm = jnp.where(tri, w_ref[g], 0.0).astype(bf16)
            for ck in range(SGU_CHUNKS):
                rw = slice(ck * CHUNK, (ck + 1) * CHUNK)
                mixed = _dot(wm, vn[rw, sl]) + bt_ref[:, g:g + 1]
                dmix = dys[rw, sl] * u[rw, sl]
                dp_ref[rw, sl] = (dys[rw, sl] * mixed * gu[rw, sl]).astype(bf16)
                dp_ref[rw, zl] = (dy[rw, sl] * u[rw, sl] * mixed * gs[rw, sl]).astype(bf16)
                dmb = dmix.astype(bf16)
                dw_ref[g] += jnp.where(tri, _dot(dmb, vn[rw, sl], NT), 0.0)
                dvn_ref[rw, sl] = _dot(wm, dmb, TN)
                db = db + jnp.where(lane == g, jnp.sum(dmix, axis=1, keepdims=True), 0.0)
        db_ref[...] += db
        dvn = dvn_ref[...]
        dlg_ref[...] += jnp.sum(dvn * vhat, axis=0, keepdims=True)
        dlb_ref[...] += jnp.sum(dvn, axis=0, keepdims=True)
        dvhat = dvn * lg
        dv = rs * (dvhat - jnp.mean(dvhat, axis=-1, keepdims=True)
                   - vhat * jnp.mean(dvhat * vhat, axis=-1, keepdims=True))
        dp_ref[:, da:2 * da] = (dv * gv).astype(bf16)

    rows = SGU_CHUNKS * CHUNK
    blk = lambda cb: pl.BlockSpec((rows, da), lambda c: (c, cb))
    full = lambda shp: pl.BlockSpec(shp, lambda c: (0,) * len(shp))
    return pl.pallas_call(
        body, name="sgu_bwd", grid=(S // rows,),
        in_specs=[blk(0), blk(1), blk(2), blk(0), _layer_spec(ln_g, l), _layer_spec(ln_b, l), _layer_spec(w_s, l),
                  _layer_spec(b_t, l), pl.BlockSpec(memory_space=pl.ANY)],
        out_specs=[pl.BlockSpec((rows, 3 * da), lambda c: (c, 0)), full((A_GROUPS, CHUNK, CHUNK)),
                   full((CHUNK, HEAD_DIM)), full((1, da)), full((1, da))],
        out_shape=[SDS((S, n_proj), bf16), SDS((A_GROUPS, CHUNK, CHUNK), f32), SDS((CHUNK, HEAD_DIM), f32),
                   SDS((1, da), f32), SDS((1, da), f32)],
        scratch_shapes=[pltpu.VMEM((rows, da), f32)], compiler_params=_params(),
    )(proj, proj, proj, dy, ln_g, ln_b, w_s, b_t, after)


def _sb_scores(q, kblk, kb, rows, cols, masked):
    z = _dot(q, kblk, NT) * (1.0 / math.sqrt(HEAD_DIM))
    t = jnp.log(1.0 + jnp.exp(-jnp.abs(z)))
    log_1mb = -(jnp.maximum(z, 0.0) + t)
    log_beta = jnp.minimum(z, 0.0) - t
    if not masked:
        return None, log_beta, log_1mb
    causal = (cols + kb * TK) < rows
    return causal, log_beta, jnp.where(causal, log_1mb, 0.0)


def _sb_tiles(i):
    rows = i * TQ + lax.broadcasted_iota(jnp.int32, (TQ, TK), 0)
    cols = lax.broadcasted_iota(jnp.int32, (TQ, TK), 1)
    r_i = lax.broadcasted_iota(jnp.int32, (TK, TK), 0)
    c_i = lax.broadcasted_iota(jnp.int32, (TK, TK), 1)
    upper, lower = (r_i > c_i).astype(bf16), (r_i < c_i).astype(bf16)
    slot = lax.broadcasted_iota(jnp.int32, (TQ, CARRY_LANES), 1)
    return rows, cols, slot, jnp.concatenate([upper, upper], axis=0), jnp.concatenate([lower, lower], axis=0)


def _suffix_sum(t, tri):
    hi = lax.bitcast_convert_type(lax.bitcast_convert_type(t, jnp.uint32) & jnp.uint32(0xFFFF0000), f32)
    both = jnp.concatenate([hi.astype(bf16), (t - hi).astype(bf16)], axis=1)
    return _dot(both, tri)


def _sb_fwd(proj, y_prev, col0, after):
    S = proj.shape[0]
    D = y_prev.shape[1]
    dh = N_HEADS * HEAD_DIM
    n_diag = TQ // TK

    def body(q_ref, k_ref, v_ref, z_ref, yp_ref, after_ref, y_ref, o_ref, car_ref, qb, kb_s, vb_s, c_ref):
        i = pl.program_id(0)

        @pl.when(i == 0)
        def _():
            kb_s[...] = k_ref[...].astype(bf16)
            vb_s[...] = v_ref[...].astype(bf16)

        qb[...] = q_ref[...].astype(bf16)
        o_ref[...] = jnp.zeros_like(o_ref)
        c_ref[...] = jnp.zeros_like(c_ref)
        car_ref[...] = jnp.zeros_like(car_ref)
        nkb = (i + 1) * n_diag
        rows, cols, slot, upper, _ = _sb_tiles(i)

        def make_step(masked):
            def step(jj, carry):
                kb = nkb - 1 - jj
                off = pl.multiple_of(kb * TK, TK)
                hs = range(N_HEADS)
                sls = [slice(h * HEAD_DIM, (h + 1) * HEAD_DIM) for h in hs]
                sc = [_sb_scores(qb[:, sls[h]], kb_s[pl.ds(off, TK), sls[h]], kb, rows, cols, masked) for h in hs]
                suf = [_suffix_sum(sc[h][2], upper) for h in hs]
                cs = [c_ref[h] for h in hs]
                es = [jnp.exp(sc[h][1] + suf[h] + cs[h][:, :1]) for h in hs]
                if masked:
                    es = [jnp.where(sc[h][0], es[h], 0.0) for h in hs]
                pv = [_dot(es[h].astype(bf16), vb_s[pl.ds(off, TK), sls[h]]) for h in hs]
                for h in hs:
                    o_ref[:, sls[h]] += pv[h]
                    car_ref[h] = jnp.where(slot == kb, cs[h], car_ref[h])
                    c_ref[h] = cs[h] + jnp.sum(sc[h][2], axis=1, keepdims=True)
                return carry
            return step

        lax.fori_loop(0, n_diag, make_step(True), 0)
        lax.fori_loop(n_diag, nkb, make_step(False), 0)
        z = z_ref[...]
        y_ref[...] = (o_ref[...] * (z * jax.nn.sigmoid(z))).astype(bf16)

    cb = col0 * HEAD_DIM // dh
    qspec = lambda k: pl.BlockSpec((TQ, dh), lambda i: (i, cb + k))
    kspec = lambda k: pl.BlockSpec((S, dh), lambda i: (0, cb + k))
    return pl.pallas_call(
        body, name="sb_fwd", grid=(S // TQ,),
        in_specs=[qspec(0), kspec(1), kspec(2), qspec(3), pl.BlockSpec(memory_space=pl.ANY),
                  pl.BlockSpec(memory_space=pl.ANY)],
        out_specs=[pl.BlockSpec((TQ, dh), lambda i: (i, A_GROUPS * HEAD_DIM // dh)),
                   pl.BlockSpec((TQ, dh), lambda i: (i, 0)),
                   pl.BlockSpec((N_HEADS, TQ, CARRY_LANES), lambda i: (0, i, 0))],
        out_shape=[SDS((S, D), bf16), SDS((S, dh), f32), SDS((N_HEADS, S, CARRY_LANES), f32)],
        input_output_aliases={4: 0},
        scratch_shapes=[pltpu.VMEM((TQ, dh), bf16), pltpu.VMEM((S, dh), bf16), pltpu.VMEM((S, dh), bf16),
                        pltpu.VMEM((N_HEADS, TQ, CARRY_LANES), f32)],
        compiler_params=_params(),
    )(proj, proj, proj, proj, y_prev, after)


def _sb_bwd(proj, o, car, dy, dproj_prev, col0, after):
    S = proj.shape[0]
    n_i = S // TQ
    dh = N_HEADS * HEAD_DIM
    n_diag = TQ // TK
    cb = col0 * HEAD_DIM // dh
    scale = 1.0 / math.sqrt(HEAD_DIM)

    def body(q_ref, k_ref, v_ref, z_ref, o_ref, car_ref, dy_ref, dpp_ref, after_ref,
             dp_ref, qb, kb_s, vb_s, dob, p_ref, dq_acc, dk_acc, dv_acc, st_a, st_b, st_k, st_v, tile_sems):
        i = pl.program_id(0)

        def put(stage_ref, row0, nrows, k):
            pltpu.sync_copy(stage_ref, dp_ref.at[pl.ds(row0, nrows), pl.ds((cb + k) * dh, dh)])

        def tile_copies(step):
            rows = pl.ds(pl.multiple_of(step * TQ, TQ), TQ)
            return [pltpu.make_async_copy(st, dp_ref.at[rows, pl.ds((cb + k) * dh, dh)], tile_sems.at[n])
                    for n, (st, k) in enumerate(((st_a, 0), (st_b, 3)))]

        @pl.when(i > 0)
        def _():
            for cp in tile_copies(i - 1):
                cp.wait()

        @pl.when(i == 0)
        def _():
            kb_s[...] = k_ref[...].astype(bf16)
            vb_s[...] = v_ref[...].astype(bf16)
            dk_acc[...] = jnp.zeros_like(dk_acc)
            dv_acc[...] = jnp.zeros_like(dv_acc)

        s, gs = _silu_and_grad(z_ref[...])
        dy = dy_ref[...]
        st_b[...] = (dy * o_ref[...] * gs).astype(bf16)
        dob[...] = (dy * s).astype(bf16)
        qb[...] = q_ref[...].astype(bf16)
        p_ref[...] = jnp.zeros_like(p_ref)
        dq_acc[...] = jnp.zeros_like(dq_acc)
        nkb = (i + 1) * n_diag
        rows, cols, slot, upper, lower = _sb_tiles(i)

        def make_step(masked):
            def step(kb, carry):
                off = pl.multiple_of(kb * TK, TK)
                hs = range(N_HEADS)
                sls = [slice(h * HEAD_DIM, (h + 1) * HEAD_DIM) for h in hs]
                qs = [qb[:, sls[h]] for h in hs]
                ks = [kb_s[pl.ds(off, TK), sls[h]] for h in hs]
                dos = [dob[:, sls[h]] for h in hs]
                sc = [_sb_scores(qs[h], ks[h], kb, rows, cols, masked) for h in hs]
                da = [_dot(dos[h], vb_s[pl.ds(off, TK), sls[h]], NT) for h in hs]
                suf = [_suffix_sum(sc[h][2], upper) for h in hs]
                onehot = slot == kb
                cs = [jnp.sum(jnp.where(onehot, car_ref[h], 0.0), axis=1, keepdims=True) for h in hs]
                es = [jnp.exp(sc[h][1] + suf[h] + cs[h]) for h in hs]
                if masked:
                    es = [jnp.where(sc[h][0], es[h], 0.0) for h in hs]
                gs_ = [da[h] * es[h] for h in hs]
                ps = [p_ref[h] for h in hs]
                pre = [_suffix_sum(gs_[h], lower) + ps[h][:, :1] for h in hs]
                dzs = []
                for h in hs:
                    beta = jnp.exp(sc[h][1])
                    dzz = gs_[h] * (1.0 - beta) - beta * pre[h]
                    if masked:
                        dzz = jnp.where(sc[h][0], dzz, 0.0)
                    dzs.append((dzz * scale).astype(bf16))
                dqs = [_dot(dzs[h], ks[h]) for h in hs]
                dks = [_dot(dzs[h], qs[h], TN) for h in hs]
                dvs = [_dot(es[h].astype(bf16), dos[h], TN) for h in hs]
                for h in hs:
                    dq_acc[:, sls[h]] += dqs[h]
                    dk_acc[pl.ds(off, TK), sls[h]] += dks[h]
                    dv_acc[pl.ds(off, TK), sls[h]] += dvs[h]
                    p_ref[h] = ps[h] + jnp.sum(gs_[h], axis=1, keepdims=True)
                return carry
            return step

        lax.fori_loop(0, nkb - n_diag, make_step(False), 0)
        lax.fori_loop(nkb - n_diag, nkb, make_step(True), 0)
        st_a[...] = dq_acc[...].astype(bf16)
        for cp in tile_copies(i):
            cp.start()

        @pl.when(i == n_i - 1)
        def _():
            st_k[...] = dk_acc[...].astype(bf16)
            st_v[...] = dv_acc[...].astype(bf16)
            put(st_k, 0, S, 1)
            put(st_v, 0, S, 2)
            for cp in tile_copies(i):
                cp.wait()

    qspec = lambda k: pl.BlockSpec((TQ, dh), lambda i: (i, cb + k))
    kspec = lambda k: pl.BlockSpec((S, dh), lambda i: (0, cb + k))
    return pl.pallas_call(
        body, name="sb_bwd", grid=(n_i,),
        in_specs=[qspec(0), kspec(1), kspec(2), qspec(3),
                  pl.BlockSpec((TQ, dh), lambda i: (i, 0)),
                  pl.BlockSpec((N_HEADS, TQ, CARRY_LANES), lambda i: (0, i, 0)),
                  pl.BlockSpec((TQ, dh), lambda i: (i, A_GROUPS * HEAD_DIM // dh)),
                  pl.BlockSpec(memory_space=pl.ANY), pl.BlockSpec(memory_space=pl.ANY)],
        out_specs=pl.BlockSpec(memory_space=pl.ANY),
        out_shape=SDS(dproj_prev.shape, bf16),
        input_output_aliases={7: 0},
        scratch_shapes=[pltpu.VMEM((TQ, dh), bf16), pltpu.VMEM((S, dh), bf16), pltpu.VMEM((S, dh), bf16),
                        pltpu.VMEM((TQ, dh), bf16), pltpu.VMEM((N_HEADS, TQ, CARRY_LANES), f32), pltpu.VMEM((TQ, dh), f32),
                        pltpu.VMEM((S, dh), f32), pltpu.VMEM((S, dh), f32),
                        pltpu.VMEM((TQ, dh), bf16), pltpu.VMEM((TQ, dh), bf16),
                        pltpu.VMEM((S, dh), bf16), pltpu.VMEM((S, dh), bf16), pltpu.SemaphoreType.DMA((2,))],
        compiler_params=_params(56),
    )(proj, proj, proj, proj, o, car, dy, dproj_prev, after)


def _mem_kv(mem, mg, l, w_kv_g):
    M, D = mem.shape
    rb, n = w_kv_g.shape[1], w_kv_g.shape[2]

    def body(m_ref, g_ref, w_ref, kv_ref):
        mv = m_ref[...]
        r = lax.rsqrt(jnp.mean(mv * mv, axis=-1, keepdims=True) + EPS)
        mh = (mv * r * g_ref[...]).astype(bf16)
        kv_ref[...] = _dot(mh, w_ref[...].reshape(N_DEV * rb, n))

    return pl.pallas_call(
        body, name="mem_kv", grid=(1,),
        in_specs=[pl.BlockSpec((M, D), lambda i: (0, 0)), _layer_spec(mg, l),
                  pl.BlockSpec((N_DEV, rb, n), lambda i: (0, 0, 0))],
        out_specs=pl.BlockSpec((M, n), lambda i: (0, 0)),
        out_shape=SDS((M, n), f32), compiler_params=_params(),
    )(mem, mg, w_kv_g)


def _xattn_head(q_ref, kv_ref, qg, kg, h):
    dc = N_HEADS * HEAD_DIM
    sl = slice(h * HEAD_DIM, (h + 1) * HEAD_DIM)
    qh = q_ref[:, sl]
    rq = lax.rsqrt(jnp.mean(qh * qh, axis=-1, keepdims=True) + EPS)
    qhat = qh * rq
    qn = (qhat * qg).astype(bf16)
    kh = kv_ref[:, sl]
    rk = lax.rsqrt(jnp.mean(kh * kh, axis=-1, keepdims=True) + EPS)
    kn = (kh * rk * kg).astype(bf16)
    vh = kv_ref[:, dc + h * HEAD_DIM:dc + (h + 1) * HEAD_DIM].astype(bf16)
    s = _dot(qn, kn, NT) * (1.0 / math.sqrt(HEAD_DIM))
    e = jnp.exp(s - jnp.max(s, axis=-1, keepdims=True))
    p = e / jnp.sum(e, axis=-1, keepdims=True)
    o = _dot(p.astype(bf16), vh)
    return sl, rq, qhat, qn, kn, vh, p, o


def _xattn_fwd(proj, kv, qg, kg, l, y_prev, col0, tq=512):
    S = proj.shape[0]
    D = y_prev.shape[1]
    dc = N_HEADS * HEAD_DIM
    M = kv.shape[0]

    def body(q_ref, z_ref, kv_ref, qg_ref, kg_ref, yp_ref, y_ref):
        for h in range(N_HEADS):
            sl, _, _, _, _, _, _, o = _xattn_head(q_ref, kv_ref, qg_ref[...], kg_ref[...], h)
            z = z_ref[:, sl]
            y_ref[:, sl] = (o * (z * jax.nn.sigmoid(z))).astype(bf16)

    full = lambda shp: pl.BlockSpec(shp, lambda i: (0,) * len(shp))
    return pl.pallas_call(
        body, name="xattn_fwd", grid=(S // tq,),
        in_specs=[pl.BlockSpec((tq, dc), lambda i: (i, col0)), pl.BlockSpec((tq, dc), lambda i: (i, col0 + 1)),
                  full((M, 2 * dc)), _layer_spec(qg, l), _layer_spec(kg, l), pl.BlockSpec(memory_space=pl.ANY)],
        out_specs=pl.BlockSpec((tq, dc), lambda i: (i, D // dc - 1)),
        out_shape=SDS((S, D), bf16), input_output_aliases={5: 0}, compiler_params=_params(),
    )(proj, proj, kv, qg, kg, y_prev)


def _xattn_bwd(proj, kv, qg, kg, l, dy, dproj_prev, col0, tq=512):
    S = proj.shape[0]
    D = dy.shape[1]
    dc = N_HEADS * HEAD_DIM
    M = kv.shape[0]

    def body(q_ref, z_ref, kv_ref, qg_ref, kg_ref, dy_ref, dpp_ref, dp_ref, dkn_ref, dv_ref, dqg_ref):
        @pl.when(pl.program_id(0) == 0)
        def _():
            dkn_ref[...] = jnp.zeros_like(dkn_ref)
            dv_ref[...] = jnp.zeros_like(dv_ref)
            dqg_ref[...] = jnp.zeros_like(dqg_ref)

        qg = qg_ref[...]
        for h in range(N_HEADS):
            sl, rq, qhat, qn, kn, vh, p, o = _xattn_head(q_ref, kv_ref, qg, kg_ref[...], h)
            s, gs = _silu_and_grad(z_ref[:, sl])
            dyh = dy_ref[:, sl]
            dp_ref[:, dc + h * HEAD_DIM:dc + (h + 1) * HEAD_DIM] = (dyh * o * gs).astype(bf16)
            dob = (dyh * s).astype(bf16)
            dpr = _dot(dob, vh, NT)
            dv_ref[:, sl] += _dot(p.astype(bf16), dob, TN)
            ds = (p * (dpr - jnp.sum(p * dpr, axis=-1, keepdims=True)) * (1.0 / math.sqrt(HEAD_DIM))).astype(bf16)
            dqn = _dot(ds, kn)
            dkn_ref[:, sl] += _dot(ds, qn, TN)
            dqg_ref[...] += jnp.sum(dqn * qhat, axis=0, keepdims=True)
            dqhat = dqn * qg
            dp_ref[:, sl] = (rq * (dqhat - qhat * jnp.mean(dqhat * qhat, axis=-1, keepdims=True))).astype(bf16)

    full = lambda shp: pl.BlockSpec(shp, lambda i: (0,) * len(shp))
    return pl.pallas_call(
        body, name="xattn_bwd", grid=(S // tq,),
        in_specs=[pl.BlockSpec((tq, dc), lambda i: (i, col0)), pl.BlockSpec((tq, dc), lambda i: (i, col0 + 1)),
                  full((M, 2 * dc)), _layer_spec(qg, l), _layer_spec(kg, l),
                  pl.BlockSpec((tq, dc), lambda i: (i, D // dc - 1)), pl.BlockSpec(memory_space=pl.ANY)],
        out_specs=[pl.BlockSpec((tq, 2 * dc), lambda i: (i, col0 // 2)), full((M, dc)), full((M, dc)),
                   full((1, HEAD_DIM))],
        out_shape=[SDS(dproj_prev.shape, bf16), SDS((M, dc), f32), SDS((M, dc), f32), SDS((1, HEAD_DIM), f32)],
        input_output_aliases={6: 0}, compiler_params=_params(),
    )(proj, proj, kv, qg, kg, dy, dproj_prev)


def _mem_bwd(mem, mg, kg, l, kv, dkn, dv, w_kv_g):
    M, D = mem.shape
    rb, n = w_kv_g.shape[1], w_kv_g.shape[2]
    dc = n // 2

    def body(m_ref, g_ref, kv_ref, dkn_ref, dv_ref, kg_ref, w_ref, dw_ref, dmg_ref, dkg_ref, dkv_ref):
        mv = m_ref[...]
        r = lax.rsqrt(jnp.mean(mv * mv, axis=-1, keepdims=True) + EPS)
        mhat = mv * r
        mh = (mhat * g_ref[...]).astype(bf16)
        kg = kg_ref[...]
        dkg = jnp.zeros((1, HEAD_DIM), f32)
        for h in range(N_HEADS):
            sl = slice(h * HEAD_DIM, (h + 1) * HEAD_DIM)
            kh = kv_ref[:, sl]
            rk = lax.rsqrt(jnp.mean(kh * kh, axis=-1, keepdims=True) + EPS)
            khat = kh * rk
            dkn_h = dkn_ref[:, sl]
            dkg = dkg + jnp.sum(dkn_h * khat, axis=0, keepdims=True)
            dkhat = dkn_h * kg
            dkv_ref[:, sl] = (rk * (dkhat - khat * jnp.mean(dkhat * khat, axis=-1, keepdims=True))).astype(bf16)
        dkv_ref[:, dc:] = dv_ref[...].astype(bf16)
        dkg_ref[...] = dkg
        dkv = dkv_ref[...]
        dw_ref[...] = _dot(mh, dkv, TN).astype(bf16).reshape(N_DEV, rb, n)
        dmh = _dot(dkv, w_ref[...].reshape(N_DEV * rb, n), NT)
        dmg_ref[...] = jnp.sum(dmh * mhat, axis=0, keepdims=True)

    full = lambda shp: pl.BlockSpec(shp, lambda i: (0,) * len(shp))
    wspec = full((N_DEV, rb, n))
    return pl.pallas_call(
        body, name="mem_bwd", grid=(1,),
        in_specs=[full((M, D)), _layer_spec(mg, l), full((M, n)), full((M, dc)), full((M, dc)), _layer_spec(kg, l), wspec],
        out_specs=[wspec, full((1, D)), full((1, HEAD_DIM))],
        out_shape=[SDS((N_DEV, rb, n), bf16), SDS((1, D), f32), SDS((1, HEAD_DIM), f32)],
        scratch_shapes=[pltpu.VMEM((M, n), bf16)], compiler_params=_params(),
    )(mem, mg, kv, dkn, dv, kg, w_kv_g)


SMALL = ("norm_g", "sgu_ln_g", "sgu_ln_b", "sgu_w", "sgu_b", "mem_norm_g", "q_norm_g", "k_norm_g")


def _small_rows(like):
    rows = [math.prod(like[n].shape) // 128 for n in SMALL]
    offs = [0]
    for r in rows:
        offs.append(offs[-1] + -(-r // 8) * 8)
    return rows, offs


def _pack_small(parts, offs):
    pieces = []
    for k, n in enumerate(SMALL):
        a = parts[n].reshape(-1, 128)
        pieces.append(jnp.pad(a, ((0, offs[k + 1] - offs[k] - a.shape[0]), (0, 0))))
    return jnp.concatenate(pieces)


def kernel(x, mem, norm_g, w_in, sgu_ln_g, sgu_ln_b, sgu_w, sgu_b, mem_norm_g, w_mem_kv, q_norm_g, k_norm_g, w_out, loss_target, m_norm_g, m_w_in, m_sgu_ln_g, m_sgu_ln_b, m_sgu_w, m_sgu_b, m_mem_norm_g, m_w_mem_kv, m_q_norm_g, m_k_norm_g, m_w_out, v_norm_g, v_w_in, v_sgu_ln_g, v_sgu_ln_b, v_sgu_w, v_sgu_b, v_mem_norm_g, v_w_mem_kv, v_q_norm_g, v_k_norm_g, v_w_out):
    L, D, wc = w_in.shape
    S = x.shape[1]
    da = D // 2
    xs = x.reshape(S, D)
    mems = mem.reshape(mem.shape[1], D)
    tgt = loss_target.reshape(S, D)
    stacked = lambda a: a.reshape(a.shape[0], 1, -1)
    ng, lng, lnb, mg, qg, kg = map(stacked, (norm_g, sgu_ln_g, sgu_ln_b, mem_norm_g, q_norm_g, k_norm_g))
    b_t = jnp.swapaxes(sgu_b, 1, 2)
    sb_col, xa_col = 3 * da // HEAD_DIM, (3 * da + D) // (D // 4)

    ax, ay, ac = lax.axis_index("x"), lax.axis_index("y"), lax.axis_index("c")
    ids = jnp.stack([4 * ax + 2 * ay + ac, 2 * ax + ay, ac]).astype(jnp.int32)
    w_in0_b = _cast_into_slot("cast_w_in", w_in, 0, 512, ids, ids)
    first = _gather3_start("gather_w_in0", [w_in0_b], ids)
    late = first["token"]
    w_b = [(w_in0_b if l == 0 else _cast_into_slot("cast_w_in", w_in, l, 512, ids, late),
            _cast_into_slot("cast_w_kv", w_mem_kv, l, 256, ids, late),
            _cast_into_slot("cast_w_out", w_out, l, 256, ids, late)) for l in range(L)]
    relay = _gather3_relay("gather_w_in0_relay",
                           _split_wait(first, *[a for wl in w_b for a in wl if a is not w_in0_b]), ids)
    in_fwd = _gather2_forward("gather_w_in0_forward", _split_wait(relay, relay["token"]), ids)

    acts = []
    xl = xs
    for l in range(L):
        (w_in_g,) = _split_wait(in_fwd, xl if l else in_fwd["token"])
        rest = _gather3_start(f"gather_w_rest{l}", [w_b[l][1], w_b[l][2]], w_in_g)
        order = rest["token"]
        if l + 1 < L:
            nxt = _gather3_start(f"gather_w_in{l + 1}", [w_b[l + 1][0]], order)
            order = nxt["token"]
        proj, h = _rms_proj(xl, ng, l, w_in_g, order)
        rest_relay = _gather3_relay(f"gather_w_rest{l}_relay", _split_wait(rest, proj), proj)
        y = _sgu_fwd(proj, lng, lnb, sgu_w, b_t, l, rest_relay["token"])
        rest_fwd = _gather2_forward(f"gather_w_rest{l}_forward", _split_wait(rest_relay, y), y)
        order = rest_fwd["token"]
        if l + 1 < L:
            nxt_relay = _gather3_relay(f"gather_w_in{l + 1}_relay", _split_wait(nxt, order), order)
            order = nxt_relay["token"]
        y, o_b, car = _sb_fwd(proj, y, sb_col, order)
        w_kv_g, w_out_g = _split_wait(rest_fwd, o_b)
        order = o_b
        if l + 1 < L:
            in_fwd = _gather2_forward(f"gather_w_in{l + 1}_forward", _split_wait(nxt_relay, o_b), o_b)
            order = in_fwd["token"]
        kv = _mem_kv(mems, mg, l, w_kv_g)
        y = _xattn_fwd(proj, kv, qg, kg, l, y, xa_col)
        acts.append((xl, proj, h, y, o_b, car, kv, w_in_g, w_kv_g, w_out_g))
        if l + 1 < L:
            xl = _out_proj(xl, y, w_out_g, order)
        else:
            dx, dxb, loss_part = _out_proj_loss(xl, y, w_out_g, tgt)
    loss = lax.psum(loss_part[0, 0], ("x", "y", "c"))

    weights = dict(norm_g=norm_g, sgu_ln_g=sgu_ln_g, sgu_ln_b=sgu_ln_b, sgu_w=sgu_w, sgu_b=sgu_b,
                   mem_norm_g=mem_norm_g, q_norm_g=q_norm_g, k_norm_g=k_norm_g)
    moms_m = dict(norm_g=m_norm_g, sgu_ln_g=m_sgu_ln_g, sgu_ln_b=m_sgu_ln_b, sgu_w=m_sgu_w, sgu_b=m_sgu_b,
                  mem_norm_g=m_mem_norm_g, q_norm_g=m_q_norm_g, k_norm_g=m_k_norm_g)
    moms_v = dict(norm_g=v_norm_g, sgu_ln_g=v_sgu_ln_g, sgu_ln_b=v_sgu_ln_b, sgu_w=v_sgu_w, sgu_b=v_sgu_b,
                  mem_norm_g=v_mem_norm_g, q_norm_g=v_q_norm_g, k_norm_g=v_k_norm_g)
    small_rows, small_offs = _small_rows(weights)
    head_rows = D // 128
    assert SMALL[0] == "norm_g" and head_rows % 8 == 0

    seconds = {}
    pending = None
    adam = {"w_out": None, "w_mem_kv": None, "w_in": None}

    late_in = {}

    def update(lu, order):
        (r_out,) = _scatter_finish(seconds[f"g_out{lu}"], order)
        adam["w_out"] = _sum_adam("adam_w_out", r_out, w_out, m_w_out, v_w_out, lu, adam["w_out"], 128, ids, ids)
        (r_kv,) = _scatter_finish(seconds[f"g_kv{lu}"], adam["w_out"][0])
        (late_in[lu],) = _scatter_finish(seconds[f"g_in{lu}"], adam["w_out"][0])
        adam["w_mem_kv"] = _sum_adam("adam_w_kv", r_kv, w_mem_kv, m_w_mem_kv, v_w_mem_kv, lu, adam["w_mem_kv"], 256, ids, ids)
        return adam["w_mem_kv"][0]

    def update_in(lu, order):
        adam["w_in"] = _sum_adam("adam_w_in", late_in[lu], w_in, m_w_in, v_w_in, lu, adam["w_in"], 256, ids, order)
        return adam["w_in"][0]
    small = {n: [None] * L for n in SMALL}
    for l in reversed(range(L)):
        xl, proj, h, y, o_b, car, kv, w_in_g, w_kv_g, w_out_g = acts[l]
        dy = _out_bwd_dy(dxb, w_out_g)
        order = dy
        if pending is not None:
            seconds[pending[0]] = _scatter2_second_level(pending[0], pending[1], dy, ids)
            order = seconds[pending[0]]["token"]
        g_out = _tn_grad("out_bwd_dw", y, dxb, 512, 512, True, order)
        seconds[f"g_out{l}"] = _scatter1_start(f"scatter_g_out{l}", [g_out], ids)
        dproj, d_sw, d_sb, d_lg, d_lb = _sgu_bwd(proj, dy, lng, lnb, sgu_w, b_t, l, seconds[f"g_out{l}"]["token"])
        dproj, dkn, dv, d_qg = _xattn_bwd(proj, kv, qg, kg, l, dy, dproj, xa_col)
        g_kv, d_mg, d_kg = _mem_bwd(mems, mg, kg, l, kv, dkn, dv, w_kv_g)
        seconds[f"g_kv{l}"] = _scatter1_start(f"scatter_g_kv{l}", [g_kv], ids)
        dproj = _sb_bwd(proj, o_b, car, dy, dproj, sb_col, seconds[f"g_kv{l}"]["token"])
        for n, val in (("sgu_ln_g", d_lg), ("sgu_ln_b", d_lb), ("sgu_w", d_sw), ("sgu_b", d_sb[:, :A_GROUPS].T),
                       ("mem_norm_g", d_mg), ("q_norm_g", d_qg), ("k_norm_g", d_kg)):
            small[n][l] = val.reshape(-1)
        order = seconds[f"g_kv{l}"]["token"]
        if l == 0:
            small["norm_g"][0] = jnp.zeros_like(small["norm_g"][1])
            part = _pack_small({n: jnp.stack(small[n]) for n in SMALL}, small_offs)
            tail = _gather1_start("gather_small_tail", [_into_slot("small_tail_slot", part[head_rows:], ids)], order)
            order = tail["token"]
        g_in_l = _tn_grad("in_bwd_dw", h, dproj, D, wc, False, order)
        first = _scatter2_pair_start(f"scatter_g_in{l}_pair", [g_in_l], ids)
        order = first["token"]
        pending = (f"g_in{l}", first)
        if l == 0:
            for lu in reversed(range(1, L)):
                order = update(lu, order)
            seconds[pending[0]] = _scatter2_second_level(pending[0], pending[1], order, ids)
            order = seconds[pending[0]]["token"]
            for lu in reversed(range(1, L)):
                order = update_in(lu, order)
        dh = _in_bwd_dh(dproj, w_in_g, order)
        dx, dxb, d_ng = _rms_bwd(dh, xl, ng, l, dx, order)
        small["norm_g"][l] = d_ng.reshape(-1)
    head = _gather1_start("gather_small_head",
                          [_into_slot("small_head_slot", small["norm_g"][0].reshape(head_rows, 128), ids)], ids)

    (r_out,) = _scatter_finish(seconds["g_out0"], head["token"])
    adam["w_out"] = _sum_adam("adam_w_out", r_out, w_out, m_w_out, v_w_out, 0, adam["w_out"], 128, ids, ids)
    (r_kv,) = _scatter_finish(seconds["g_kv0"], adam["w_out"][0])
    (r_in,) = _scatter_finish(seconds["g_in0"], adam["w_out"][0])
    adam["w_mem_kv"] = _sum_adam("adam_w_kv", r_kv, w_mem_kv, m_w_mem_kv, v_w_mem_kv, 0, adam["w_mem_kv"], 256, ids, ids)
    adam["w_in"] = _sum_adam("adam_w_in", r_in, w_in, m_w_in, v_w_in, 0, adam["w_in"], 256, ids, ids)
    (r_tail,) = _split_wait(tail, adam["w_in"][0])
    (r_head,) = _split_wait(head, r_tail)
    as128 = lambda d: [d[n].reshape(-1, 128) for n in SMALL]
    sm = _small_sum_adam(r_head, r_tail, as128(weights), as128(moms_m), as128(moms_v), small_offs)
    res = dict(adam)
    for p, n in enumerate(SMALL):
        res[n] = [sm[k * len(SMALL) + p].reshape(weights[n].shape) for k in range(4)]

    order = ("norm_g", "w_in", "sgu_ln_g", "sgu_ln_b", "sgu_w", "sgu_b", "mem_norm_g", "w_mem_kv", "q_norm_g",
             "k_norm_g", "w_out")
    outs = [loss, dx.reshape(x.shape)]
    for k in range(4):
        outs += [res[n][k] for n in order]
    return tuple(outs)
```

```python
import functools
import math

import jax
import jax.numpy as jnp
from jax import lax
from jax.experimental import pallas as pl
from jax.experimental.pallas import tpu as pltpu

f32 = jnp.float32
bf16 = jnp.bfloat16
SDS = jax.ShapeDtypeStruct

N_DEV = 8
EPS = 1e-6
CHUNK = 128
SGU_CHUNKS = 2
A_GROUPS = 8
HEAD_DIM = 128
N_HEADS = 4
TQ = 256
TK = 256
CARRY_LANES = 128
ADAM_LR, ADAM_B1, ADAM_B2, ADAM_EPS, ADAM_WD, ADAM_STEP = 0.001, 0.9, 0.999, 1e-08, 0.01, 10
MIB = 1024 * 1024

NT = (((1,), (1,)), ((), ()))
TN = (((0,), (0,)), ((), ()))


def _params(vmem_mib=48):
    return pltpu.CompilerParams(vmem_limit_bytes=vmem_mib * MIB)


def _gelu_and_grad(x):
    e = lax.erf(x * (1.0 / math.sqrt(2.0)))
    cdf = 0.5 * (1.0 + e)
    pdf = jnp.exp(-0.5 * x * x) * (1.0 / math.sqrt(2.0 * math.pi))
    return x * cdf, cdf + x * pdf


def _gelu(x):
    return 0.5 * x * (1.0 + lax.erf(x * (1.0 / math.sqrt(2.0))))


def _silu_and_grad(z):
    sg = jax.nn.sigmoid(z)
    return z * sg, sg * (1.0 + z * (1.0 - sg))


def _layer_spec(stacked, l):
    rest = stacked.shape[1:]
    return pl.BlockSpec((None,) + rest, lambda *idx: (l,) + (0,) * len(rest))


def _dot(a, b, dims=None):
    if dims is None:
        return jnp.dot(a, b, preferred_element_type=f32)
    return lax.dot_general(a, b, dims, preferred_element_type=f32)


_HBM = pl.BlockSpec(memory_space=pltpu.HBM)
_SEM = pl.BlockSpec(memory_space=pltpu.SEMAPHORE)
_EFFECT = pltpu.SideEffectType.DATAFLOW_SIDE_EFFECTING


def _split_start(name, bufs, n_remote, n_local, build, after):
    nb = len(bufs)

    def body(*refs):
        token = refs[-1]
        locals_, remotes = build(refs[:nb], *refs[nb + 1:nb + 4])
        for cp in locals_ + remotes:
            cp.start()
        token[...] = jnp.zeros_like(token)

    hbm = lambda a: pltpu.with_memory_space_constraint(a, pltpu.HBM)
    outs = pl.pallas_call(
        body, name=name,
        out_shape=(pltpu.SemaphoreType.DMA((n_remote,)), pltpu.SemaphoreType.DMA((n_remote,)),
                   pltpu.SemaphoreType.DMA((max(n_local, 1),)),
                   *[pltpu.HBM(b.shape, b.dtype) for b in bufs], SDS((8, 128), f32)),
        in_specs=[_HBM] * nb + [pl.BlockSpec(memory_space=pl.ANY)],
        out_specs=(_SEM, _SEM, _SEM, *[_HBM] * nb, pl.BlockSpec(memory_space=pltpu.VMEM)),
        input_output_aliases={k: 3 + k for k in range(nb)},
        compiler_params=pltpu.CompilerParams(has_side_effects=_EFFECT),
    )(*[hbm(b) for b in bufs], after)
    return dict(name=name, build=build, sems=outs[:3], bufs=outs[3:3 + nb], token=outs[-1])


def _split_wait(handle, *after):
    build, bufs = handle["build"], handle["bufs"]
    nb = len(bufs)

    def body(*refs):
        locals_, remotes = build(refs[:nb], *refs[nb:nb + 3])
        for cp in remotes:
            cp.wait_recv()
        for cp in remotes:
            cp.wait_send()
        for cp in locals_:
            cp.wait()

    outs = pl.pallas_call(
        body, name=handle["name"] + "_wait",
        out_shape=tuple(pltpu.HBM(b.shape, b.dtype) for b in bufs),
        in_specs=[_HBM] * nb + [_SEM] * 3 + [pl.BlockSpec(memory_space=pl.ANY)] * len(after),
        out_specs=tuple([_HBM] * nb),
        input_output_aliases={k: k for k in range(nb)},
        compiler_params=pltpu.CompilerParams(has_side_effects=_EFFECT),
    )(*bufs, *handle["sems"], *after)
    return list(outs)


def _remote(src, dst, send_sems, recv_sems, k, to):
    return pltpu.make_async_remote_copy(src_ref=src, dst_ref=dst, send_sem=send_sems.at[k], recv_sem=recv_sems.at[k],
                                        device_id=to, device_id_type=pl.DeviceIdType.MESH)


def _other_chips(x, y):
    return [(1 - x, y), (x, 1 - y), (1 - x, 1 - y)]


def _all_peers(x, y, c):
    return [(1 - x if m & 4 else x, 1 - y if m & 2 else y, 1 - c if m & 1 else c) for m in range(1, N_DEV)]


def _gather1_start(name, lands, after):
    def build(refs, send, recv, loc):
        x, y, c = lax.axis_index("x"), lax.axis_index("y"), lax.axis_index("c")
        me = 4 * x + 2 * y + c
        return [], [_remote(d.at[me], d.at[me], send, recv, 7 * a + k, peer)
                    for a, d in enumerate(refs) for k, peer in enumerate(_all_peers(x, y, c))]

    return _split_start(name, list(lands), 7 * len(lands), 0, build, after)


def _scatter1_start(name, srcs, after):
    n = len(srcs)

    def build(refs, send, recv, loc):
        x, y, c = lax.axis_index("x"), lax.axis_index("y"), lax.axis_index("c")
        me = 4 * x + 2 * y + c
        return [], [_remote(refs[a].at[4 * px + 2 * py + pc], refs[n + a].at[me], send, recv, 7 * a + k, (px, py, pc))
                    for a in range(n) for k, (px, py, pc) in enumerate(_all_peers(x, y, c))]

    return _split_start(name, list(srcs) + [lax.empty(s.shape, s.dtype) for s in srcs], 7 * n, 0, build, after)


def _gather2_start(name, lands, after):
    def build(refs, send, recv, loc):
        x, y, c = lax.axis_index("x"), lax.axis_index("y"), lax.axis_index("c")
        me = 4 * x + 2 * y + c
        remotes = []
        for a, d in enumerate(refs):
            remotes.append(_remote(d.at[me], d.at[me], send, recv, 4 * a, (x, y, 1 - c)))
            remotes += [_remote(d.at[me], d.at[me], send, recv, 4 * a + 1 + k, (px, py, c))
                        for k, (px, py) in enumerate(_other_chips(x, y))]
        return [], remotes

    return _split_start(name, list(lands), 4 * len(lands), 0, build, after)


def _gather3_start(name, lands, after):
    def build(refs, send, recv, loc):
        x, y, c = lax.axis_index("x"), lax.axis_index("y"), lax.axis_index("c")
        me = 4 * x + 2 * y + c
        return [], [_remote(d.at[me], d.at[me], send, recv, 3 * a + k, to)
                    for a, d in enumerate(refs) for k, to in enumerate([(x, y, 1 - c), (1 - x, y, c), (x, 1 - y, c)])]

    return _split_start(name, list(lands), 3 * len(lands), 0, build, after)


def _gather3_relay(name, lands, after):
    def build(refs, send, recv, loc):
        x, y, c = lax.axis_index("x"), lax.axis_index("y"), lax.axis_index("c")
        from_x = c == 0
        slot = 4 * jnp.where(from_x, 1 - x, x) + 2 * jnp.where(from_x, y, 1 - y) + c
        to = (jnp.where(from_x, x, 1 - x), jnp.where(from_x, 1 - y, y), c)
        return [], [_remote(d.at[slot], d.at[slot], send, recv, a, to) for a, d in enumerate(refs)]

    return _split_start(name, list(lands), len(lands), 0, build, after)


def _gather2_forward(name, lands, after):
    n = len(lands)

    def build(refs, send, recv, loc):
        x, y, c = lax.axis_index("x"), lax.axis_index("y"), lax.axis_index("c")
        slots = [4 * px + 2 * py + c for px, py in _other_chips(x, y)]
        return [], [_remote(d.at[sl], d.at[sl], send, recv, 3 * a + k, (x, y, 1 - c))
                    for a, d in enumerate(refs) for k, sl in enumerate(slots)]

    return _split_start(name, list(lands), 3 * n, 0, build, after)


def _scatter2_pair_start(name, srcs, after):
    n = len(srcs)

    def build(refs, send, recv, loc):
        x, y, c = lax.axis_index("x"), lax.axis_index("y"), lax.axis_index("c")
        return [], [_remote(refs[a].at[2 * q + 1 - c], refs[n + a].at[q], send, recv, 4 * a + q, (x, y, 1 - c))
                    for a in range(n) for q in range(4)]

    lands = [lax.empty((4,) + s.shape[1:], s.dtype) for s in srcs]
    return _split_start(name, list(srcs) + lands, 4 * n, 0, build, after)


def _scatter2_chip_start(name, pairs, after):
    n = len(pairs)

    def build(refs, send, recv, loc):
        x, y, c = lax.axis_index("x"), lax.axis_index("y"), lax.axis_index("c")
        return [], [_remote(refs[a].at[2 * px + py], refs[n + a].at[2 * x + y], send, recv, 3 * a + k, (px, py, c))
                    for a in range(n) for k, (px, py) in enumerate(_other_chips(x, y))]

    return _split_start(name, list(pairs) + [lax.empty(p.shape, p.dtype) for p in pairs], 3 * n, 0, build, after)


def _pair_sum(name, src, theirs, ids):
    _, R, C = theirs.shape
    tr = min(R, 1024)

    def body(ids_ref, a_ref, b_ref, o_ref):
        o_ref[...] = (a_ref[...].astype(f32) + b_ref[...].astype(f32)).astype(bf16)

    spec = pl.BlockSpec((None, tr, C), lambda q, i, ids: (q, i, 0))
    return pl.pallas_call(
        body, name=name,
        grid_spec=pltpu.PrefetchScalarGridSpec(
            num_scalar_prefetch=1, grid=(4, R // tr),
            in_specs=[pl.BlockSpec((None, tr, C), lambda q, i, ids: (2 * q + ids[2], i, 0)), spec], out_specs=spec),
        out_shape=SDS(theirs.shape, bf16), compiler_params=_params(),
    )(ids, src, theirs)


def _scatter2_second_level(name, first, after, ids):
    outs = _split_wait(first, after)
    n = len(outs) // 2
    pairs = [_pair_sum(f"pair_sum_{name}{a}", outs[a], outs[n + a], ids) for a in range(n)]
    return _scatter2_chip_start(f"scatter_{name}_chip", pairs, ids)


def _scatter_finish(second, after):
    outs = _split_wait(second, after)
    n = len(outs) // 2
    return [(outs[a], outs[n + a]) for a in range(n)]


def _cast_into_slot(name, w, l, tr, ids, after):
    _, R, C = w.shape

    def body(ids_ref, w_ref, after_ref, o_ref):
        o_ref[...] = w_ref[...].astype(bf16)

    return pl.pallas_call(
        body, name=name,
        grid_spec=pltpu.PrefetchScalarGridSpec(
            num_scalar_prefetch=1, grid=(R // tr,),
            in_specs=[pl.BlockSpec((None, tr, C), lambda i, ids: (l, i, 0)), pl.BlockSpec(memory_space=pl.ANY)],
            out_specs=pl.BlockSpec((None, tr, C), lambda i, ids: (ids[0], i, 0))),
        out_shape=SDS((N_DEV, R, C), bf16), compiler_params=_params(),
    )(ids, w, after)


def _into_slot(name, a, ids):
    R, C = a.shape

    def body(ids_ref, a_ref, o_ref):
        o_ref[...] = a_ref[...]

    return pl.pallas_call(
        body, name=name,
        grid_spec=pltpu.PrefetchScalarGridSpec(
            num_scalar_prefetch=1, grid=(1,),
            in_specs=[pl.BlockSpec((R, C), lambda i, ids: (0, 0))],
            out_specs=pl.BlockSpec((None, R, C), lambda i, ids: (ids[0], 0, 0))),
        out_shape=SDS((N_DEV, R, C), f32), compiler_params=_params(),
    )(ids, a)


def _adam_math(w, g, m, v):
    m2 = ADAM_B1 * m + (1.0 - ADAM_B1) * g
    v2 = ADAM_B2 * v + (1.0 - ADAM_B2) * (g * g)
    m_hat = m2 / (1.0 - ADAM_B1 ** ADAM_STEP)
    v_hat = v2 / (1.0 - ADAM_B2 ** ADAM_STEP)
    delta = -ADAM_LR * (m_hat / (jnp.sqrt(v_hat) + ADAM_EPS) + ADAM_WD * w)
    return delta, m2, v2


def _sum_adam(name, pair_recv, w, m, v, l, prev, tr, ids, after):
    own, recv = pair_recv
    L, R, C = w.shape
    slots = recv.shape[0]
    mine = 0 if slots == N_DEV else 1

    def body(ids_ref, r_ref, own_ref, w_ref, m_ref, v_ref, after_ref, *rest):
        g_ref, d_ref, m2_ref, v2_ref = rest[-4:]
        terms = [jnp.where(ids_ref[mine] == q, own_ref[...], r_ref[q]).astype(f32) for q in range(slots)]
        g = terms[0]
        for t in terms[1:]:
            g = g + t
        d, m2, v2 = _adam_math(w_ref[...], g, m_ref[...], v_ref[...])
        g_ref[...] = g
        d_ref[...] = d
        m2_ref[...] = m2
        v2_ref[...] = v2

    wspec = pl.BlockSpec((None, tr, C), lambda i, ids: (l, i, 0))
    in_specs = [pl.BlockSpec((slots, tr, C), lambda i, ids: (0, i, 0)),
                pl.BlockSpec((None, tr, C), lambda i, ids: (ids[mine], i, 0)), wspec, wspec, wspec,
                pl.BlockSpec(memory_space=pl.ANY)]
    args = [ids, recv, own, w, m, v, after]
    aliases = {}
    if prev is not None:
        in_specs += [pl.BlockSpec(memory_space=pl.ANY)] * 4
        args += list(prev)
        aliases = {7 + k: k for k in range(4)}
    return pl.pallas_call(
        body, name=name,
        grid_spec=pltpu.PrefetchScalarGridSpec(num_scalar_prefetch=1, grid=(R // tr,), in_specs=in_specs,
                                               out_specs=[wspec] * 4),
        out_shape=[SDS((L, R, C), f32)] * 4, input_output_aliases=aliases, compiler_params=_params(),
    )(*args)


def _small_sum_adam(recv_head, recv_tail, ws, ms, vs, offs):
    n = len(ws)
    r0 = recv_head.shape[1]

    def body(*refs):
        rh, rt = refs[0], refs[1]
        w_refs, m_refs, v_refs = refs[2:2 + n], refs[2 + n:2 + 2 * n], refs[2 + 2 * n:2 + 3 * n]
        outs = refs[2 + 3 * n:]
        for p in range(n):
            lo, hi = offs[p], offs[p] + ws[p].shape[0]
            pieces = []
            if lo < r0:
                pieces.append((rh, lo, 0, min(hi, r0) - lo))
            if hi > r0:
                pieces.append((rt, max(lo, r0) - r0, max(lo, r0) - lo, hi - max(lo, r0)))
            for src, a, b, cnt in pieces:
                g = src[0, a:a + cnt, :]
                for s in range(1, N_DEV):
                    g = g + src[s, a:a + cnt, :]
                d, m2, v2 = _adam_math(w_refs[p][b:b + cnt, :], g, m_refs[p][b:b + cnt, :], v_refs[p][b:b + cnt, :])
                for k, val in enumerate((g, d, m2, v2)):
                    outs[k * n + p][b:b + cnt, :] = val

    return pl.pallas_call(
        body, name="small_sum_adam", out_shape=[SDS(w.shape, f32) for w in ws] * 4, compiler_params=_params(),
    )(recv_head, recv_tail, *ws, *ms, *vs)


def _rms_proj(x, g, l, w_in_g, after, tm=1024):
    S, D = x.shape
    wc = w_in_g.shape[2]
    n_out = N_DEV * wc

    def body(x_ref, g_ref, w_ref, after_ref, proj_ref, h_ref):
        @pl.when(pl.program_id(1) == 0)
        def _():
            xv = x_ref[...]
            r = lax.rsqrt(jnp.mean(xv * xv, axis=-1, keepdims=True) + EPS)
            h_ref[...] = (xv * r * g_ref[...]).astype(bf16)

        proj_ref[...] = _dot(h_ref[...], w_ref[...])

    return pl.pallas_call(
        body, name="rms_proj", grid=(S // tm, N_DEV),
        in_specs=[pl.BlockSpec((tm, D), lambda i, j: (i, 0)), _layer_spec(g, l),
                  pl.BlockSpec((None, D, wc), lambda i, j: (j, 0, 0)), pl.BlockSpec(memory_space=pl.ANY)],
        out_specs=[pl.BlockSpec((tm, wc), lambda i, j: (i, j)), pl.BlockSpec((tm, D), lambda i, j: (i, 0))],
        out_shape=[SDS((S, n_out), f32), SDS((S, D), bf16)], compiler_params=_params(),
    )(x, g, w_in_g, after)


def _out_proj(x, y, w_out_g, after, tm=512):
    S, D = x.shape
    rb = w_out_g.shape[1]

    def body(x_ref, y_ref, w_ref, after_ref, o_ref):
        w = w_ref[...].reshape(N_DEV * rb, D)
        o_ref[...] = x_ref[...] + _dot(y_ref[...], w)

    row = pl.BlockSpec((tm, D), lambda i: (i, 0))
    return pl.pallas_call(
        body, name="out_proj", grid=(S // tm,),
        in_specs=[row, row, pl.BlockSpec((N_DEV, rb, D), lambda i: (0, 0, 0)), pl.BlockSpec(memory_space=pl.ANY)],
        out_specs=row, out_shape=SDS((S, D), f32), compiler_params=_params(),
    )(x, y, w_out_g, after)


def _out_proj_loss(x, y, w_out_g, tgt, tm=256):
    S, D = x.shape
    rb = w_out_g.shape[1]

    def body(x_ref, y_ref, w_ref, t_ref, dx_ref, dxb_ref, l_ref):
        i = pl.program_id(0)
        w = w_ref[...].reshape(N_DEV * rb, D)
        d = x_ref[...] + _dot(y_ref[...], w) - t_ref[...]
        dx = d * (1.0 / D)
        dx_ref[...] = dx
        dxb_ref[...] = dx.astype(bf16)
        e = d * d
        part = e[:, 0:128]
        for k in range(1, D // 128):
            part = part + e[:, k * 128:(k + 1) * 128]
        part = jnp.sum(part.reshape(tm // 8, 8, 128), axis=0)

        @pl.when(i == 0)
        def _():
            l_ref[...] = jnp.zeros_like(l_ref)

        l_ref[...] += part

        @pl.when(i == pl.num_programs(0) - 1)
        def _():
            tot = jnp.sum(l_ref[...], axis=1, keepdims=True)
            tot = jnp.sum(tot, axis=0, keepdims=True)
            l_ref[...] = jnp.broadcast_to(tot * (0.5 / D), l_ref.shape)

    row = pl.BlockSpec((tm, D), lambda i: (i, 0))
    return pl.pallas_call(
        body, name="out_proj_loss", grid=(S // tm,),
        in_specs=[row, row, pl.BlockSpec((N_DEV, rb, D), lambda i: (0, 0, 0)), row],
        out_specs=[row, row, pl.BlockSpec((8, 128), lambda i: (0, 0))],
        out_shape=[SDS((S, D), f32), SDS((S, D), bf16), SDS((8, 128), f32)], compiler_params=_params(),
    )(x, y, w_out_g, tgt)


def _out_bwd_dy(dxb, w_out_g, tm=1024):
    S, D = dxb.shape
    rb = w_out_g.shape[1]

    nb = 2

    def body(dx_ref, w_ref, o_ref):
        o_ref[...] = _dot(dx_ref[...], w_ref[...].reshape(nb * rb, D), NT)

    return pl.pallas_call(
        body, name="out_bwd_dy", grid=(S // tm, N_DEV // nb),
        in_specs=[pl.BlockSpec((tm, D), lambda i, j: (i, 0)),
                  pl.BlockSpec((nb, rb, D), lambda i, j: (j, 0, 0))],
        out_specs=pl.BlockSpec((tm, nb * rb), lambda i, j: (i, j)),
        out_shape=SDS((S, D), f32), compiler_params=_params(),
    )(dxb, w_out_g)


def _tn_grad(name, a, b, tm, tn, rows_major, after):
    S, M = a.shape
    N = b.shape[1]
    if rows_major:
        rb = M // N_DEV
        nb = tm // rb
        out_shape = SDS((N_DEV, rb, N), bf16)
        out_spec = pl.BlockSpec((nb, rb, tn), lambda i, j: (i, 0, j))
    else:
        out_shape = SDS((N_DEV, M, N // N_DEV), bf16)
        assert tn == N // N_DEV
        out_spec = pl.BlockSpec((None, tm, tn), lambda i, j: (j, i, 0))

    def body(a_ref, b_ref, after_ref, o_ref):
        o_ref[...] = _dot(a_ref[...], b_ref[...], TN).astype(bf16).reshape(o_ref.shape)

    return pl.pallas_call(
        body, name=name, grid=(M // tm, N // tn),
        in_specs=[pl.BlockSpec((S, tm), lambda i, j: (0, i)), pl.BlockSpec((S, tn), lambda i, j: (0, j)),
                  pl.BlockSpec(memory_space=pl.ANY)],
        out_specs=out_spec, out_shape=out_shape, compiler_params=_params(),
    )(a, b, after)


def _in_bwd_dh(dproj, w_in_g, after, tm=1024, tn=256):
    S = dproj.shape[0]
    _, D, wc = w_in_g.shape
    tm = min(tm, S)

    def body(dp_ref, w_ref, after_ref, o_ref):
        acc = _dot(dp_ref[:, 0:wc], w_ref[0], NT)
        for k in range(1, N_DEV):
            acc = acc + _dot(dp_ref[:, k * wc:(k + 1) * wc], w_ref[k], NT)
        o_ref[...] = acc

    return pl.pallas_call(
        body, name="in_bwd_dh", grid=(S // tm, D // tn),
        in_specs=[pl.BlockSpec((tm, N_DEV * wc), lambda i, j: (i, 0)),
                  pl.BlockSpec((N_DEV, tn, wc), lambda i, j: (0, j, 0)), pl.BlockSpec(memory_space=pl.ANY)],
        out_specs=pl.BlockSpec((tm, tn), lambda i, j: (i, j)),
        out_shape=SDS((S, D), f32), compiler_params=_params(),
    )(dproj, w_in_g, after)


def _rms_bwd(dh, x, g, l, dx_next, after, tm=256):
    S, D = x.shape

    def body(dh_ref, x_ref, g_ref, dxn_ref, after_ref, dx_ref, dxb_ref, dg_ref):
        @pl.when(pl.program_id(0) == 0)
        def _():
            dg_ref[...] = jnp.zeros_like(dg_ref)

        dh = dh_ref[...]
        xv = x_ref[...]
        r = lax.rsqrt(jnp.mean(xv * xv, axis=-1, keepdims=True) + EPS)
        xhat = xv * r
        dxhat = dh * g_ref[...]
        dx = r * (dxhat - xhat * jnp.mean(dxhat * xhat, axis=-1, keepdims=True)) + dxn_ref[...]
        dx_ref[...] = dx
        dxb_ref[...] = dx.astype(bf16)
        dg_ref[...] += jnp.sum(dh * xhat, axis=0, keepdims=True)

    row = pl.BlockSpec((tm, D), lambda i: (i, 0))
    vec = pl.BlockSpec((1, D), lambda i: (0, 0))
    return pl.pallas_call(
        body, name="rms_bwd", grid=(S // tm,),
        in_specs=[row, row, _layer_spec(g, l), row, pl.BlockSpec(memory_space=pl.ANY)], out_specs=[row, row, vec],
        out_shape=[SDS((S, D), f32), SDS((S, D), bf16), SDS((1, D), f32)], compiler_params=_params(),
    )(dh, x, g, dx_next, after)


def _sgu_fwd(proj, ln_g, ln_b, w_s, b_t, l, after):
    S = proj.shape[0]
    da = A_GROUPS * HEAD_DIM
    D = 2 * da

    def body(u_ref, v_ref, z_ref, lg_ref, lb_ref, w_ref, bt_ref, after_ref, y_ref):
        u = _gelu(u_ref[...])
        v = _gelu(v_ref[...])
        z = z_ref[...]
        mu = jnp.mean(v, axis=-1, keepdims=True)
        xc = v - mu
        rs = lax.rsqrt(jnp.mean(xc * xc, axis=-1, keepdims=True) + EPS)
        vn = (xc * rs * lg_ref[...] + lb_ref[...]).astype(bf16)
        gate = u * (z * jax.nn.sigmoid(z))
        tri = lax.broadcasted_iota(jnp.int32, (CHUNK, CHUNK), 0) >= lax.broadcasted_iota(jnp.int32, (CHUNK, CHUNK), 1)
        for g in range(A_GROUPS):
            sl = slice(g * HEAD_DIM, (g + 1) * HEAD_DIM)
            wm = jnp.where(tri, w_ref[g], 0.0).astype(bf16)
            for ck in range(SGU_CHUNKS):
                rw = slice(ck * CHUNK, (ck + 1) * CHUNK)
                mixed = _dot(wm, vn[rw, sl]) + bt_ref[:, g:g + 1]
                y_ref[rw, sl] = (gate[rw, sl] * mixed).astype(bf16)

    blk = lambda cb: pl.BlockSpec((SGU_CHUNKS * CHUNK, da), lambda c: (c, cb))
    full = lambda shp: pl.BlockSpec(shp, lambda c: (0,) * len(shp))
    return pl.pallas_call(
        body, name="sgu_fwd", grid=(S // (SGU_CHUNKS * CHUNK),),
        in_specs=[blk(0), blk(1), blk(2), _layer_spec(ln_g, l), _layer_spec(ln_b, l), _layer_spec(w_s, l),
                  _layer_spec(b_t, l), pl.BlockSpec(memory_space=pl.ANY)],
        out_specs=blk(0), out_shape=SDS((S, D), bf16), compiler_params=_params(),
    )(proj, proj, proj, ln_g, ln_b, w_s, b_t, after)


def _sgu_bwd(proj, dy, ln_g, ln_b, w_s, b_t, l, after):
    S = proj.shape[0]
    da = A_GROUPS * HEAD_DIM
    n_proj = proj.shape[1]

    def body(u_ref, v_ref, z_ref, dy_ref, lg_ref, lb_ref, w_ref, bt_ref, after_ref,
             dp_ref, dw_ref, db_ref, dlg_ref, dlb_ref, dvn_ref):
        @pl.when(pl.program_id(0) == 0)
        def _():
            dw_ref[...] = jnp.zeros_like(dw_ref)
            db_ref[...] = jnp.zeros_like(db_ref)
            dlg_ref[...] = jnp.zeros_like(dlg_ref)
            dlb_ref[...] = jnp.zeros_like(dlb_ref)

        up, vp, z, dy = u_ref[...], v_ref[...], z_ref[...], dy_ref[...]
        u, gu = _gelu_and_grad(up)
        v, gv = _gelu_and_grad(vp)
        s, gs = _silu_and_grad(z)
        mu = jnp.mean(v, axis=-1, keepdims=True)
        xc = v - mu
        rs = lax.rsqrt(jnp.mean(xc * xc, axis=-1, keepdims=True) + EPS)
        vhat = xc * rs
        lg = lg_ref[...]
        vn = (vhat * lg + lb_ref[...]).astype(bf16)
        tri = lax.broadcasted_iota(jnp.int32, (CHUNK, CHUNK), 0) >= lax.broadcasted_iota(jnp.int32, (CHUNK, CHUNK), 1)
        lane = lax.broadcasted_iota(jnp.int32, (CHUNK, HEAD_DIM), 1)
        dys = dy * s
        db = jnp.zeros((CHUNK, HEAD_DIM), f32)
        for g in range(A_GROUPS):
            sl = slice(g * HEAD_DIM, (g + 1) * HEAD_DIM)
            zl = slice(2 * da + g * HEAD_DIM, 2 * da + (g + 1) * HEAD_DIM)
            wm = jnp.where(tri, w_ref[g], 0.0).astype(bf16)
            for ck in range(SGU_CHUNKS):
                rw = slice(ck * CHUNK, (ck + 1) * CHUNK)
                mixed = _dot(wm, vn[rw, sl]) + bt_ref[:, g:g + 1]
                dmix = dys[rw, sl] * u[rw, sl]
                dp_ref[rw, sl] = (dys[rw, sl] * mixed * gu[rw, sl]).astype(bf16)
                dp_ref[rw, zl] = (dy[rw, sl] * u[rw, sl] * mixed * gs[rw, sl]).astype(bf16)
                dmb = dmix.astype(bf16)
                dw_ref[g] += jnp.where(tri, _dot(dmb, vn[rw, sl], NT), 0.0)
                dvn_ref[rw, sl] = _dot(wm, dmb, TN)
                db = db + jnp.where(lane == g, jnp.sum(dmix, axis=1, keepdims=True), 0.0)
        db_ref[...] += db
        dvn = dvn_ref[...]
        dlg_ref[...] += jnp.sum(dvn * vhat, axis=0, keepdims=True)
        dlb_ref[...] += jnp.sum(dvn, axis=0, keepdims=True)
        dvhat = dvn * lg
        dv = rs * (dvhat - jnp.mean(dvhat, axis=-1, keepdims=True)
                   - vhat * jnp.mean(dvhat * vhat, axis=-1, keepdims=True))
        dp_ref[:, da:2 * da] = (dv * gv).astype(bf16)

    rows = SGU_CHUNKS * CHUNK
    blk = lambda cb: pl.BlockSpec((rows, da), lambda c: (c, cb))
    full = lambda shp: pl.BlockSpec(shp, lambda c: (0,) * len(shp))
    return pl.pallas_call(
        body, name="sgu_bwd", grid=(S // rows,),
        in_specs=[blk(0), blk(1), blk(2), blk(0), _layer_spec(ln_g, l), _layer_spec(ln_b, l), _layer_spec(w_s, l),
                  _layer_spec(b_t, l), pl.BlockSpec(memory_space=pl.ANY)],
        out_specs=[pl.BlockSpec((rows, 3 * da), lambda c: (c, 0)), full((A_GROUPS, CHUNK, CHUNK)),
                   full((CHUNK, HEAD_DIM)), full((1, da)), full((1, da))],
        out_shape=[SDS((S, n_proj), bf16), SDS((A_GROUPS, CHUNK, CHUNK), f32), SDS((CHUNK, HEAD_DIM), f32),
                   SDS((1, da), f32), SDS((1, da), f32)],
        scratch_shapes=[pltpu.VMEM((rows, da), f32)], compiler_params=_params(),
    )(proj, proj, proj, dy, ln_g, ln_b, w_s, b_t, after)


def _sb_scores(q, kblk, kb, rows, cols, masked):
    z = _dot(q, kblk, NT) * (1.0 / math.sqrt(HEAD_DIM))
    t = jnp.log(1.0 + jnp.exp(-jnp.abs(z)))
    log_1mb = -(jnp.maximum(z, 0.0) + t)
    log_beta = jnp.minimum(z, 0.0) - t
    if not masked:
        return None, log_beta, log_1mb
    causal = (cols + kb * TK) < rows
    return causal, log_beta, jnp.where(causal, log_1mb, 0.0)


def _sb_tiles(i):
    rows = i * TQ + lax.broadcasted_iota(jnp.int32, (TQ, TK), 0)
    cols = lax.broadcasted_iota(jnp.int32, (TQ, TK), 1)
    r_i = lax.broadcasted_iota(jnp.int32, (TK, TK), 0)
    c_i = lax.broadcasted_iota(jnp.int32, (TK, TK), 1)
    upper, lower = (r_i > c_i).astype(bf16), (r_i < c_i).astype(bf16)
    slot = lax.broadcasted_iota(jnp.int32, (TQ, CARRY_LANES), 1)
    return rows, cols, slot, jnp.concatenate([upper, upper], axis=0), jnp.concatenate([lower, lower], axis=0)


def _suffix_sum(t, tri):
    hi = lax.bitcast_convert_type(lax.bitcast_convert_type(t, jnp.uint32) & jnp.uint32(0xFFFF0000), f32)
    both = jnp.concatenate([hi.astype(bf16), (t - hi).astype(bf16)], axis=1)
    return _dot(both, tri)


def _sb_fwd(proj, y_prev, col0, after):
    S = proj.shape[0]
    D = y_prev.shape[1]
    dh = N_HEADS * HEAD_DIM
    n_diag = TQ // TK

    def body(q_ref, k_ref, v_ref, z_ref, yp_ref, after_ref, y_ref, o_ref, car_ref, qb, kb_s, vb_s, c_ref):
        i = pl.program_id(0)

        @pl.when(i == 0)
        def _():
            kb_s[...] = k_ref[...].astype(bf16)
            vb_s[...] = v_ref[...].astype(bf16)

        qb[...] = q_ref[...].astype(bf16)
        o_ref[...] = jnp.zeros_like(o_ref)
        c_ref[...] = jnp.zeros_like(c_ref)
        car_ref[...] = jnp.zeros_like(car_ref)
        nkb = (i + 1) * n_diag
        rows, cols, slot, upper, _ = _sb_tiles(i)

        def make_step(masked):
            def step(jj, carry):
                kb = nkb - 1 - jj
                off = pl.multiple_of(kb * TK, TK)
                hs = range(N_HEADS)
                sls = [slice(h * HEAD_DIM, (h + 1) * HEAD_DIM) for h in hs]
                sc = [_sb_scores(qb[:, sls[h]], kb_s[pl.ds(off, TK), sls[h]], kb, rows, cols, masked) for h in hs]
                suf = [_suffix_sum(sc[h][2], upper) for h in hs]
                cs = [c_ref[h] for h in hs]
                es = [jnp.exp(sc[h][1] + suf[h] + cs[h][:, :1]) for h in hs]
                if masked:
                    es = [jnp.where(sc[h][0], es[h], 0.0) for h in hs]
                pv = [_dot(es[h].astype(bf16), vb_s[pl.ds(off, TK), sls[h]]) for h in hs]
                for h in hs:
                    o_ref[:, sls[h]] += pv[h]
                    car_ref[h] = jnp.where(slot == kb, cs[h], car_ref[h])
                    c_ref[h] = cs[h] + jnp.sum(sc[h][2], axis=1, keepdims=True)
                return carry
            return step

        lax.fori_loop(0, n_diag, make_step(True), 0)
        lax.fori_loop(n_diag, nkb, make_step(False), 0)
        z = z_ref[...]
        y_ref[...] = (o_ref[...] * (z * jax.nn.sigmoid(z))).astype(bf16)

    cb = col0 * HEAD_DIM // dh
    qspec = lambda k: pl.BlockSpec((TQ, dh), lambda i: (i, cb + k))
    kspec = lambda k: pl.BlockSpec((S, dh), lambda i: (0, cb + k))
    return pl.pallas_call(
        body, name="sb_fwd", grid=(S // TQ,),
        in_specs=[qspec(0), kspec(1), kspec(2), qspec(3), pl.BlockSpec(memory_space=pl.ANY),
                  pl.BlockSpec(memory_space=pl.ANY)],
        out_specs=[pl.BlockSpec((TQ, dh), lambda i: (i, A_GROUPS * HEAD_DIM // dh)),
                   pl.BlockSpec((TQ, dh), lambda i: (i, 0)),
                   pl.BlockSpec((N_HEADS, TQ, CARRY_LANES), lambda i: (0, i, 0))],
        out_shape=[SDS((S, D), bf16), SDS((S, dh), f32), SDS((N_HEADS, S, CARRY_LANES), f32)],
        input_output_aliases={4: 0},
        scratch_shapes=[pltpu.VMEM((TQ, dh), bf16), pltpu.VMEM((S, dh), bf16), pltpu.VMEM((S, dh), bf16),
                        pltpu.VMEM((N_HEADS, TQ, CARRY_LANES), f32)],
        compiler_params=_params(),
    )(proj, proj, proj, proj, y_prev, after)


def _sb_bwd(proj, o, car, dy, dproj_prev, col0, after):
    S = proj.shape[0]
    n_i = S // TQ
    dh = N_HEADS * HEAD_DIM
    n_diag = TQ // TK
    cb = col0 * HEAD_DIM // dh
    scale = 1.0 / math.sqrt(HEAD_DIM)

    def body(q_ref, k_ref, v_ref, z_ref, o_ref, car_ref, dy_ref, dpp_ref, after_ref,
             dp_ref, qb, kb_s, vb_s, dob, p_ref, dq_acc, dk_acc, dv_acc, st_a, st_b, st_k, st_v, tile_sems):
        i = pl.program_id(0)

        def put(stage_ref, row0, nrows, k):
            pltpu.sync_copy(stage_ref, dp_ref.at[pl.ds(row0, nrows), pl.ds((cb + k) * dh, dh)])

        def tile_copies(step):
            rows = pl.ds(pl.multiple_of(step * TQ, TQ), TQ)
            return [pltpu.make_async_copy(st, dp_ref.at[rows, pl.ds((cb + k) * dh, dh)], tile_sems.at[n])
                    for n, (st, k) in enumerate(((st_a, 0), (st_b, 3)))]

        @pl.when(i > 0)
        def _():
            for cp in tile_copies(i - 1):
                cp.wait()

        @pl.when(i == 0)
        def _():
            kb_s[...] = k_ref[...].astype(bf16)
            vb_s[...] = v_ref[...].astype(bf16)
            dk_acc[...] = jnp.zeros_like(dk_acc)
            dv_acc[...] = jnp.zeros_like(dv_acc)

        s, gs = _silu_and_grad(z_ref[...])
        dy = dy_ref[...]
        st_b[...] = (dy * o_ref[...] * gs).astype(bf16)
        dob[...] = (dy * s).astype(bf16)
        qb[...] = q_ref[...].astype(bf16)
        p_ref[...] = jnp.zeros_like(p_ref)
        dq_acc[...] = jnp.zeros_like(dq_acc)
        nkb = (i + 1) * n_diag
        rows, cols, slot, upper, lower = _sb_tiles(i)

        def make_step(masked):
            def step(kb, carry):
                off = pl.multiple_of(kb * TK, TK)
                hs = range(N_HEADS)
                sls = [slice(h * HEAD_DIM, (h + 1) * HEAD_DIM) for h in hs]
                qs = [qb[:, sls[h]] for h in hs]
                ks = [kb_s[pl.ds(off, TK), sls[h]] for h in hs]
                dos = [dob[:, sls[h]] for h in hs]
                sc = [_sb_scores(qs[h], ks[h], kb, rows, cols, masked) for h in hs]
                da = [_dot(dos[h], vb_s[pl.ds(off, TK), sls[h]], NT) for h in hs]
                suf = [_suffix_sum(sc[h][2], upper) for h in hs]
                onehot = slot == kb
                cs = [jnp.sum(jnp.where(onehot, car_ref[h], 0.0), axis=1, keepdims=True) for h in hs]
                es = [jnp.exp(sc[h][1] + suf[h] + cs[h]) for h in hs]
                if masked:
                    es = [jnp.where(sc[h][0], es[h], 0.0) for h in hs]
                gs_ = [da[h] * es[h] for h in hs]
                ps = [p_ref[h] for h in hs]
                pre = [_suffix_sum(gs_[h], lower) + ps[h][:, :1] for h in hs]
                dzs = []
                for h in hs:
                    beta = jnp.exp(sc[h][1])
                    dzz = gs_[h] * (1.0 - beta) - beta * pre[h]
                    if masked:
                        dzz = jnp.where(sc[h][0], dzz, 0.0)
                    dzs.append((dzz * scale).astype(bf16))
                dqs = [_dot(dzs[h], ks[h]) for h in hs]
                dks = [_dot(dzs[h], qs[h], TN) for h in hs]
                dvs = [_dot(es[h].astype(bf16), dos[h], TN) for h in hs]
                for h in hs:
                    dq_acc[:, sls[h]] += dqs[h]
                    dk_acc[pl.ds(off, TK), sls[h]] += dks[h]
                    dv_acc[pl.ds(off, TK), sls[h]] += dvs[h]
                    p_ref[h] = ps[h] + jnp.sum(gs_[h], axis=1, keepdims=True)
                return carry
            return step

        lax.fori_loop(0, nkb - n_diag, make_step(False), 0)
        lax.fori_loop(nkb - n_diag, nkb, make_step(True), 0)
        st_a[...] = dq_acc[...].astype(bf16)
        for cp in tile_copies(i):
            cp.start()

        @pl.when(i == n_i - 1)
        def _():
            st_k[...] = dk_acc[...].astype(bf16)
            st_v[...] = dv_acc[...].astype(bf16)
            put(st_k, 0, S, 1)
            put(st_v, 0, S, 2)
            for cp in tile_copies(i):
                cp.wait()

    qspec = lambda k: pl.BlockSpec((TQ, dh), lambda i: (i, cb + k))
    kspec = lambda k: pl.BlockSpec((S, dh), lambda i: (0, cb + k))
    return pl.pallas_call(
        body, name="sb_bwd", grid=(n_i,),
        in_specs=[qspec(0), kspec(1), kspec(2), qspec(3),
                  pl.BlockSpec((TQ, dh), lambda i: (i, 0)),
                  pl.BlockSpec((N_HEADS, TQ, CARRY_LANES), lambda i: (0, i, 0)),
                  pl.BlockSpec((TQ, dh), lambda i: (i, A_GROUPS * HEAD_DIM // dh)),
                  pl.BlockSpec(memory_space=pl.ANY), pl.BlockSpec(memory_space=pl.ANY)],
        out_specs=pl.BlockSpec(memory_space=pl.ANY),
        out_shape=SDS(dproj_prev.shape, bf16),
        input_output_aliases={7: 0},
        scratch_shapes=[pltpu.VMEM((TQ, dh), bf16), pltpu.VMEM((S, dh), bf16), pltpu.VMEM((S, dh), bf16),
                        pltpu.VMEM((TQ, dh), bf16), pltpu.VMEM((N_HEADS, TQ, CARRY_LANES), f32), pltpu.VMEM((TQ, dh), f32),
                        pltpu.VMEM((S, dh), f32), pltpu.VMEM((S, dh), f32),
                        pltpu.VMEM((TQ, dh), bf16), pltpu.VMEM((TQ, dh), bf16),
                        pltpu.VMEM((S, dh), bf16), pltpu.VMEM((S, dh), bf16), pltpu.SemaphoreType.DMA((2,))],
        compiler_params=_params(56),
    )(proj, proj, proj, proj, o, car, dy, dproj_prev, after)


def _mem_kv(mem, mg, l, w_kv_g):
    M, D = mem.shape
    rb, n = w_kv_g.shape[1], w_kv_g.shape[2]

    def body(m_ref, g_ref, w_ref, kv_ref):
        mv = m_ref[...]
        r = lax.rsqrt(jnp.mean(mv * mv, axis=-1, keepdims=True) + EPS)
        mh = (mv * r * g_ref[...]).astype(bf16)
        kv_ref[...] = _dot(mh, w_ref[...].reshape(N_DEV * rb, n))

    return pl.pallas_call(
        body, name="mem_kv", grid=(1,),
        in_specs=[pl.BlockSpec((M, D), lambda i: (0, 0)), _layer_spec(mg, l),
                  pl.BlockSpec((N_DEV, rb, n), lambda i: (0, 0, 0))],
        out_specs=pl.BlockSpec((M, n), lambda i: (0, 0)),
        out_shape=SDS((M, n), f32), compiler_params=_params(),
    )(mem, mg, w_kv_g)


def _xattn_head(q_ref, kv_ref, qg, kg, h):
    dc = N_HEADS * HEAD_DIM
    sl = slice(h * HEAD_DIM, (h + 1) * HEAD_DIM)
    qh = q_ref[:, sl]
    rq = lax.rsqrt(jnp.mean(qh * qh, axis=-1, keepdims=True) + EPS)
    qhat = qh * rq
    qn = (qhat * qg).astype(bf16)
    kh = kv_ref[:, sl]
    rk = lax.rsqrt(jnp.mean(kh * kh, axis=-1, keepdims=True) + EPS)
    kn = (kh * rk * kg).astype(bf16)
    vh = kv_ref[:, dc + h * HEAD_DIM:dc + (h + 1) * HEAD_DIM].astype(bf16)
    s = _dot(qn, kn, NT) * (1.0 / math.sqrt(HEAD_DIM))
    e = jnp.exp(s - jnp.max(s, axis=-1, keepdims=True))
    p = e / jnp.sum(e, axis=-1, keepdims=True)
    o = _dot(p.astype(bf16), vh)
    return sl, rq, qhat, qn, kn, vh, p, o


def _xattn_fwd(proj, kv, qg, kg, l, y_prev, col0, tq=512):
    S = proj.shape[0]
    D = y_prev.shape[1]
    dc = N_HEADS * HEAD_DIM
    M = kv.shape[0]

    def body(q_ref, z_ref, kv_ref, qg_ref, kg_ref, yp_ref, y_ref):
        for h in range(N_HEADS):
            sl, _, _, _, _, _, _, o = _xattn_head(q_ref, kv_ref, qg_ref[...], kg_ref[...], h)
            z = z_ref[:, sl]
            y_ref[:, sl] = (o * (z * jax.nn.sigmoid(z))).astype(bf16)

    full = lambda shp: pl.BlockSpec(shp, lambda i: (0,) * len(shp))
    return pl.pallas_call(
        body, name="xattn_fwd", grid=(S // tq,),
        in_specs=[pl.BlockSpec((tq, dc), lambda i: (i, col0)), pl.BlockSpec((tq, dc), lambda i: (i, col0 + 1)),
                  full((M, 2 * dc)), _layer_spec(qg, l), _layer_spec(kg, l), pl.BlockSpec(memory_space=pl.ANY)],
        out_specs=pl.BlockSpec((tq, dc), lambda i: (i, D // dc - 1)),
        out_shape=SDS((S, D), bf16), input_output_aliases={5: 0}, compiler_params=_params(),
    )(proj, proj, kv, qg, kg, y_prev)


def _xattn_bwd(proj, kv, qg, kg, l, dy, dproj_prev, col0, tq=512):
    S = proj.shape[0]
    D = dy.shape[1]
    dc = N_HEADS * HEAD_DIM
    M = kv.shape[0]

    def body(q_ref, z_ref, kv_ref, qg_ref, kg_ref, dy_ref, dpp_ref, dp_ref, dkn_ref, dv_ref, dqg_ref):
        @pl.when(pl.program_id(0) == 0)
        def _():
            dkn_ref[...] = jnp.zeros_like(dkn_ref)
            dv_ref[...] = jnp.zeros_like(dv_ref)
            dqg_ref[...] = jnp.zeros_like(dqg_ref)

        qg = qg_ref[...]
        for h in range(N_HEADS):
            sl, rq, qhat, qn, kn, vh, p, o = _xattn_head(q_ref, kv_ref, qg, kg_ref[...], h)
            s, gs = _silu_and_grad(z_ref[:, sl])
            dyh = dy_ref[:, sl]
            dp_ref[:, dc + h * HEAD_DIM:dc + (h + 1) * HEAD_DIM] = (dyh * o * gs).astype(bf16)
            dob = (dyh * s).astype(bf16)
            dpr = _dot(dob, vh, NT)
            dv_ref[:, sl] += _dot(p.astype(bf16), dob, TN)
            ds = (p * (dpr - jnp.sum(p * dpr, axis=-1, keepdims=True)) * (1.0 / math.sqrt(HEAD_DIM))).astype(bf16)
            dqn = _dot(ds, kn)
            dkn_ref[:, sl] += _dot(ds, qn, TN)
            dqg_ref[...] += jnp.sum(dqn * qhat, axis=0, keepdims=True)
            dqhat = dqn * qg
            dp_ref[:, sl] = (rq * (dqhat - qhat * jnp.mean(dqhat * qhat, axis=-1, keepdims=True))).astype(bf16)

    full = lambda shp: pl.BlockSpec(shp, lambda i: (0,) * len(shp))
    return pl.pallas_call(
        body, name="xattn_bwd", grid=(S // tq,),
        in_specs=[pl.BlockSpec((tq, dc), lambda i: (i, col0)), pl.BlockSpec((tq, dc), lambda i: (i, col0 + 1)),
                  full((M, 2 * dc)), _layer_spec(qg, l), _layer_spec(kg, l),
                  pl.BlockSpec((tq, dc), lambda i: (i, D // dc - 1)), pl.BlockSpec(memory_space=pl.ANY)],
        out_specs=[pl.BlockSpec((tq, 2 * dc), lambda i: (i, col0 // 2)), full((M, dc)), full((M, dc)),
                   full((1, HEAD_DIM))],
        out_shape=[SDS(dproj_prev.shape, bf16), SDS((M, dc), f32), SDS((M, dc), f32), SDS((1, HEAD_DIM), f32)],
        input_output_aliases={6: 0}, compiler_params=_params(),
    )(proj, proj, kv, qg, kg, dy, dproj_prev)


def _mem_bwd(mem, mg, kg, l, kv, dkn, dv, w_kv_g):
    M, D = mem.shape
    rb, n = w_kv_g.shape[1], w_kv_g.shape[2]
    dc = n // 2

    def body(m_ref, g_ref, kv_ref, dkn_ref, dv_ref, kg_ref, w_ref, dw_ref, dmg_ref, dkg_ref, dkv_ref):
        mv = m_ref[...]
        r = lax.rsqrt(jnp.mean(mv * mv, axis=-1, keepdims=True) + EPS)
        mhat = mv * r
        mh = (mhat * g_ref[...]).astype(bf16)
        kg = kg_ref[...]
        dkg = jnp.zeros((1, HEAD_DIM), f32)
        for h in range(N_HEADS):
            sl = slice(h * HEAD_DIM, (h + 1) * HEAD_DIM)
            kh = kv_ref[:, sl]
            rk = lax.rsqrt(jnp.mean(kh * kh, axis=-1, keepdims=True) + EPS)
            khat = kh * rk
            dkn_h = dkn_ref[:, sl]
            dkg = dkg + jnp.sum(dkn_h * khat, axis=0, keepdims=True)
            dkhat = dkn_h * kg
            dkv_ref[:, sl] = (rk * (dkhat - khat * jnp.mean(dkhat * khat, axis=-1, keepdims=True))).astype(bf16)
        dkv_ref[:, dc:] = dv_ref[...].astype(bf16)
        dkg_ref[...] = dkg
        dkv = dkv_ref[...]
        dw_ref[...] = _dot(mh, dkv, TN).astype(bf16).reshape(N_DEV, rb, n)
        dmh = _dot(dkv, w_ref[...].reshape(N_DEV * rb, n), NT)
        dmg_ref[...] = jnp.sum(dmh * mhat, axis=0, keepdims=True)

    full = lambda shp: pl.BlockSpec(shp, lambda i: (0,) * len(shp))
    wspec = full((N_DEV, rb, n))
    return pl.pallas_call(
        body, name="mem_bwd", grid=(1,),
        in_specs=[full((M, D)), _layer_spec(mg, l), full((M, n)), full((M, dc)), full((M, dc)), _layer_spec(kg, l), wspec],
        out_specs=[wspec, full((1, D)), full((1, HEAD_DIM))],
        out_shape=[SDS((N_DEV, rb, n), bf16), SDS((1, D), f32), SDS((1, HEAD_DIM), f32)],
        scratch_shapes=[pltpu.VMEM((M, n), bf16)], compiler_params=_params(),
    )(mem, mg, kv, dkn, dv, kg, w_kv_g)


SMALL = ("norm_g", "sgu_ln_g", "sgu_ln_b", "sgu_w", "sgu_b", "mem_norm_g", "q_norm_g", "k_norm_g")


def _small_rows(like):
    rows = [math.prod(like[n].shape) // 128 for n in SMALL]
    offs = [0]
    for r in rows:
        offs.append(offs[-1] + -(-r // 8) * 8)
    return rows, offs


def _pack_small(parts, offs):
    pieces = []
    for k, n in enumerate(SMALL):
        a = parts[n].reshape(-1, 128)
        pieces.append(jnp.pad(a, ((0, offs[k + 1] - offs[k] - a.shape[0]), (0, 0))))
    return jnp.concatenate(pieces)


def kernel(x, mem, norm_g, w_in, sgu_ln_g, sgu_ln_b, sgu_w, sgu_b, mem_norm_g, w_mem_kv, q_norm_g, k_norm_g, w_out, loss_target, m_norm_g, m_w_in, m_sgu_ln_g, m_sgu_ln_b, m_sgu_w, m_sgu_b, m_mem_norm_g, m_w_mem_kv, m_q_norm_g, m_k_norm_g, m_w_out, v_norm_g, v_w_in, v_sgu_ln_g, v_sgu_ln_b, v_sgu_w, v_sgu_b, v_mem_norm_g, v_w_mem_kv, v_q_norm_g, v_k_norm_g, v_w_out):
    L, D, wc = w_in.shape
    S = x.shape[1]
    da = D // 2
    xs = x.reshape(S, D)
    mems = mem.reshape(mem.shape[1], D)
    tgt = loss_target.reshape(S, D)
    stacked = lambda a: a.reshape(a.shape[0], 1, -1)
    ng, lng, lnb, mg, qg, kg = map(stacked, (norm_g, sgu_ln_g, sgu_ln_b, mem_norm_g, q_norm_g, k_norm_g))
    b_t = jnp.swapaxes(sgu_b, 1, 2)
    sb_col, xa_col = 3 * da // HEAD_DIM, (3 * da + D) // (D // 4)

    ax, ay, ac = lax.axis_index("x"), lax.axis_index("y"), lax.axis_index("c")
    ids = jnp.stack([4 * ax + 2 * ay + ac, 2 * ax + ay, ac]).astype(jnp.int32)
    w_in0_b = _cast_into_slot("cast_w_in", w_in, 0, 512, ids, ids)
    first = _gather3_start("gather_w_in0", [w_in0_b], ids)
    late = first["token"]
    w_b = [(w_in0_b if l == 0 else _cast_into_slot("cast_w_in", w_in, l, 512, ids, late),
            _cast_into_slot("cast_w_kv", w_mem_kv, l, 256, ids, late),
            _cast_into_slot("cast_w_out", w_out, l, 256, ids, late)) for l in range(L)]
    relay = _gather3_relay("gather_w_in0_relay",
                           _split_wait(first, *[a for wl in w_b for a in wl if a is not w_in0_b]), ids)
    in_fwd = _gather2_forward("gather_w_in0_forward", _split_wait(relay, relay["token"]), ids)

    acts = []
    xl = xs
    for l in range(L):
        (w_in_g,) = _split_wait(in_fwd, xl if l else in_fwd["token"])
        rest = _gather3_start(f"gather_w_rest{l}", [w_b[l][1], w_b[l][2]], w_in_g)
        order = rest["token"]
        if l + 1 < L:
            nxt = _gather3_start(f"gather_w_in{l + 1}", [w_b[l + 1][0]], order)
            order = nxt["token"]
        proj, h = _rms_proj(xl, ng, l, w_in_g, order)
        rest_relay = _gather3_relay(f"gather_w_rest{l}_relay", _split_wait(rest, proj), proj)
        y = _sgu_fwd(proj, lng, lnb, sgu_w, b_t, l, rest_relay["token"])
        rest_fwd = _gather2_forward(f"gather_w_rest{l}_forward", _split_wait(rest_relay, y), y)
        order = rest_fwd["token"]
        if l + 1 < L:
            nxt_relay = _gather3_relay(f"gather_w_in{l + 1}_relay", _split_wait(nxt, order), order)
            order = nxt_relay["token"]
        y, o_b, car = _sb_fwd(proj, y, sb_col, order)
        w_kv_g, w_out_g = _split_wait(rest_fwd, o_b)
        order = o_b
        if l + 1 < L:
            in_fwd = _gather2_forward(f"gather_w_in{l + 1}_forward", _split_wait(nxt_relay, o_b), o_b)
            order = in_fwd["token"]
        kv = _mem_kv(mems, mg, l, w_kv_g)
        y = _xattn_fwd(proj, kv, qg, kg, l, y, xa_col)
        acts.append((xl, proj, h, y, o_b, car, kv, w_in_g, w_kv_g, w_out_g))
        if l + 1 < L:
            xl = _out_proj(xl, y, w_out_g, order)
        else:
            dx, dxb, loss_part = _out_proj_loss(xl, y, w_out_g, tgt)
    loss = lax.psum(loss_part[0, 0], ("x", "y", "c"))

    weights = dict(norm_g=norm_g, sgu_ln_g=sgu_ln_g, sgu_ln_b=sgu_ln_b, sgu_w=sgu_w, sgu_b=sgu_b,
                   mem_norm_g=mem_norm_g, q_norm_g=q_norm_g, k_norm_g=k_norm_g)
    moms_m = dict(norm_g=m_norm_g, sgu_ln_g=m_sgu_ln_g, sgu_ln_b=m_sgu_ln_b, sgu_w=m_sgu_w, sgu_b=m_sgu_b,
                  mem_norm_g=m_mem_norm_g, q_norm_g=m_q_norm_g, k_norm_g=m_k_norm_g)
    moms_v = dict(norm_g=v_norm_g, sgu_ln_g=v_sgu_ln_g, sgu_ln_b=v_sgu_ln_b, sgu_w=v_sgu_w, sgu_b=v_sgu_b,
                  mem_norm_g=v_mem_norm_g, q_norm_g=v_q_norm_g, k_norm_g=v_k_norm_g)
    small_rows, small_offs = _small_rows(weights)
    head_rows = D // 128
    assert SMALL[0] == "norm_g" and head_rows % 8 == 0

    seconds = {}
    pending = None
    adam = {"w_out": None, "w_mem_kv": None, "w_in": None}

    late_in = {}

    def update(lu, order):
        (r_out,) = _scatter_finish(seconds[f"g_out{lu}"], order)
        adam["w_out"] = _sum_adam("adam_w_out", r_out, w_out, m_w_out, v_w_out, lu, adam["w_out"], 128, ids, ids)
        (r_kv,) = _scatter_finish(seconds[f"g_kv{lu}"], adam["w_out"][0])
        (late_in[lu],) = _scatter_finish(seconds[f"g_in{lu}"], adam["w_out"][0])
        adam["w_mem_kv"] = _sum_adam("adam_w_kv", r_kv, w_mem_kv, m_w_mem_kv, v_w_mem_kv, lu, adam["w_mem_kv"], 256, ids, ids)
        return adam["w_mem_kv"][0]

    def update_in(lu, order):
        adam["w_in"] = _sum_adam("adam_w_in", late_in[lu], w_in, m_w_in, v_w_in, lu, adam["w_in"], 512, ids, order)
        return adam["w_in"][0]
    small = {n: [None] * L for n in SMALL}
    for l in reversed(range(L)):
        xl, proj, h, y, o_b, car, kv, w_in_g, w_kv_g, w_out_g = acts[l]
        dy = _out_bwd_dy(dxb, w_out_g)
        order = dy
        if pending is not None:
            seconds[pending[0]] = _scatter2_second_level(pending[0], pending[1], dy, ids)
            order = seconds[pending[0]]["token"]
        g_out = _tn_grad("out_bwd_dw", y, dxb, 512, 512, True, order)
        seconds[f"g_out{l}"] = _scatter1_start(f"scatter_g_out{l}", [g_out], ids)
        dproj, d_sw, d_sb, d_lg, d_lb = _sgu_bwd(proj, dy, lng, lnb, sgu_w, b_t, l, seconds[f"g_out{l}"]["token"])
        dproj, dkn, dv, d_qg = _xattn_bwd(proj, kv, qg, kg, l, dy, dproj, xa_col)
        g_kv, d_mg, d_kg = _mem_bwd(mems, mg, kg, l, kv, dkn, dv, w_kv_g)
        seconds[f"g_kv{l}"] = _scatter1_start(f"scatter_g_kv{l}", [g_kv], ids)
        dproj = _sb_bwd(proj, o_b, car, dy, dproj, sb_col, seconds[f"g_kv{l}"]["token"])
        for n, val in (("sgu_ln_g", d_lg), ("sgu_ln_b", d_lb), ("sgu_w", d_sw), ("sgu_b", d_sb[:, :A_GROUPS].T),
                       ("mem_norm_g", d_mg), ("q_norm_g", d_qg), ("k_norm_g", d_kg)):
            small[n][l] = val.reshape(-1)
        order = seconds[f"g_kv{l}"]["token"]
        if l == 0:
            small["norm_g"][0] = jnp.zeros_like(small["norm_g"][1])
            part = _pack_small({n: jnp.stack(small[n]) for n in SMALL}, small_offs)
            tail = _gather1_start("gather_small_tail", [_into_slot("small_tail_slot", part[head_rows:], ids)], order)
            order = tail["token"]
        g_in_l = _tn_grad("in_bwd_dw", h, dproj, D, wc, False, order)
        first = _scatter2_pair_start(f"scatter_g_in{l}_pair", [g_in_l], ids)
        order = first["token"]
        pending = (f"g_in{l}", first)
        if l == 0:
            for lu in reversed(range(1, L)):
                order = update(lu, order)
            seconds[pending[0]] = _scatter2_second_level(pending[0], pending[1], order, ids)
            order = seconds[pending[0]]["token"]
            for lu in reversed(range(1, L)):
                order = update_in(lu, order)
        dh = _in_bwd_dh(dproj, w_in_g, order)
        dx, dxb, d_ng = _rms_bwd(dh, xl, ng, l, dx, order)
        small["norm_g"][l] = d_ng.reshape(-1)
    head = _gather1_start("gather_small_head",
                          [_into_slot("small_head_slot", small["norm_g"][0].reshape(head_rows, 128), ids)], ids)

    (r_out,) = _scatter_finish(seconds["g_out0"], head["token"])
    adam["w_out"] = _sum_adam("adam_w_out", r_out, w_out, m_w_out, v_w_out, 0, adam["w_out"], 128, ids, ids)
    (r_kv,) = _scatter_finish(seconds["g_kv0"], adam["w_out"][0])
    (r_in,) = _scatter_finish(seconds["g_in0"], adam["w_out"][0])
    adam["w_mem_kv"] = _sum_adam("adam_w_kv", r_kv, w_mem_kv, m_w_mem_kv, v_w_mem_kv, 0, adam["w_mem_kv"], 256, ids, ids)
    adam["w_in"] = _sum_adam("adam_w_in", r_in, w_in, m_w_in, v_w_in, 0, adam["w_in"], 512, ids, ids)
    (r_tail,) = _split_wait(tail, adam["w_in"][0])
    (r_head,) = _split_wait(head, r_tail)
    as128 = lambda d: [d[n].reshape(-1, 128) for n in SMALL]
    sm = _small_sum_adam(r_head, r_tail, as128(weights), as128(moms_m), as128(moms_v), small_offs)
    res = dict(adam)
    for p, n in enumerate(SMALL):
        res[n] = [sm[k * len(SMALL) + p].reshape(weights[n].shape) for k in range(4)]

    order = ("norm_g", "w_in", "sgu_ln_g", "sgu_ln_b", "sgu_w", "sgu_b", "mem_norm_g", "w_mem_kv", "q_norm_g",
             "k_norm_g", "w_out")
    outs = [loss, dx.reshape(x.shape)]
    for k in range(4):
        outs += [res[n][k] for n in order]
    return tuple(outs)
```

```python
import functools
import math

import jax
import jax.numpy as jnp
from jax import lax
from jax.experimental import pallas as pl
from jax.experimental.pallas import tpu as pltpu

f32 = jnp.float32
bf16 = jnp.bfloat16
SDS = jax.ShapeDtypeStruct

N_DEV = 8
EPS = 1e-6
CHUNK = 128
SGU_CHUNKS = 2
A_GROUPS = 8
HEAD_DIM = 128
N_HEADS = 4
TQ = 256
TK = 256
CARRY_LANES = 128
ADAM_LR, ADAM_B1, ADAM_B2, ADAM_EPS, ADAM_WD, ADAM_STEP = 0.001, 0.9, 0.999, 1e-08, 0.01, 10
MIB = 1024 * 1024

NT = (((1,), (1,)), ((), ()))
TN = (((0,), (0,)), ((), ()))


def _params(vmem_mib=48):
    return pltpu.CompilerParams(vmem_limit_bytes=vmem_mib * MIB)


def _gelu_and_grad(x):
    e = lax.erf(x * (1.0 / math.sqrt(2.0)))
    cdf = 0.5 * (1.0 + e)
    pdf = jnp.exp(-0.5 * x * x) * (1.0 / math.sqrt(2.0 * math.pi))
    return x * cdf, cdf + x * pdf


def _gelu(x):
    return 0.5 * x * (1.0 + lax.erf(x * (1.0 / math.sqrt(2.0))))


def _silu_and_grad(z):
    sg = jax.nn.sigmoid(z)
    return z * sg, sg * (1.0 + z * (1.0 - sg))


def _layer_spec(stacked, l):
    rest = stacked.shape[1:]
    return pl.BlockSpec((None,) + rest, lambda *idx: (l,) + (0,) * len(rest))


def _dot(a, b, dims=None):
    if dims is None:
        return jnp.dot(a, b, preferred_element_type=f32)
    return lax.dot_general(a, b, dims, preferred_element_type=f32)


_HBM = pl.BlockSpec(memory_space=pltpu.HBM)
_SEM = pl.BlockSpec(memory_space=pltpu.SEMAPHORE)
_EFFECT = pltpu.SideEffectType.DATAFLOW_SIDE_EFFECTING


def _split_start(name, bufs, n_remote, n_local, build, after):
    nb = len(bufs)

    def body(*refs):
        token = refs[-1]
        locals_, remotes = build(refs[:nb], *refs[nb + 1:nb + 4])
        for cp in locals_ + remotes:
            cp.start()
        token[...] = jnp.zeros_like(token)

    hbm = lambda a: pltpu.with_memory_space_constraint(a, pltpu.HBM)
    outs = pl.pallas_call(
        body, name=name,
        out_shape=(pltpu.SemaphoreType.DMA((n_remote,)), pltpu.SemaphoreType.DMA((n_remote,)),
                   pltpu.SemaphoreType.DMA((max(n_local, 1),)),
                   *[pltpu.HBM(b.shape, b.dtype) for b in bufs], SDS((8, 128), f32)),
        in_specs=[_HBM] * nb + [pl.BlockSpec(memory_space=pl.ANY)],
        out_specs=(_SEM, _SEM, _SEM, *[_HBM] * nb, pl.BlockSpec(memory_space=pltpu.VMEM)),
        input_output_aliases={k: 3 + k for k in range(nb)},
        compiler_params=pltpu.CompilerParams(has_side_effects=_EFFECT),
    )(*[hbm(b) for b in bufs], after)
    return dict(name=name, build=build, sems=outs[:3], bufs=outs[3:3 + nb], token=outs[-1])


def _split_wait(handle, *after):
    build, bufs = handle["build"], handle["bufs"]
    nb = len(bufs)

    def body(*refs):
        locals_, remotes = build(refs[:nb], *refs[nb:nb + 3])
        for cp in remotes:
            cp.wait_recv()
        for cp in remotes:
            cp.wait_send()
        for cp in locals_:
            cp.wait()

    outs = pl.pallas_call(
        body, name=handle["name"] + "_wait",
        out_shape=tuple(pltpu.HBM(b.shape, b.dtype) for b in bufs),
        in_specs=[_HBM] * nb + [_SEM] * 3 + [pl.BlockSpec(memory_space=pl.ANY)] * len(after),
        out_specs=tuple([_HBM] * nb),
        input_output_aliases={k: k for k in range(nb)},
        compiler_params=pltpu.CompilerParams(has_side_effects=_EFFECT),
    )(*bufs, *handle["sems"], *after)
    return list(outs)


def _remote(src, dst, send_sems, recv_sems, k, to):
    return pltpu.make_async_remote_copy(src_ref=src, dst_ref=dst, send_sem=send_sems.at[k], recv_sem=recv_sems.at[k],
                                        device_id=to, device_id_type=pl.DeviceIdType.MESH)


def _other_chips(x, y):
    return [(1 - x, y), (x, 1 - y), (1 - x, 1 - y)]


def _all_peers(x, y, c):
    return [(1 - x if m & 4 else x, 1 - y if m & 2 else y, 1 - c if m & 1 else c) for m in range(1, N_DEV)]


def _gather1_start(name, lands, after):
    def build(refs, send, recv, loc):
        x, y, c = lax.axis_index("x"), lax.axis_index("y"), lax.axis_index("c")
        me = 4 * x + 2 * y + c
        return [], [_remote(d.at[me], d.at[me], send, recv, 7 * a + k, peer)
                    for a, d in enumerate(refs) for k, peer in enumerate(_all_peers(x, y, c))]

    return _split_start(name, list(lands), 7 * len(lands), 0, build, after)


def _scatter1_start(name, srcs, after):
    n = len(srcs)

    def build(refs, send, recv, loc):
        x, y, c = lax.axis_index("x"), lax.axis_index("y"), lax.axis_index("c")
        me = 4 * x + 2 * y + c
        return [], [_remote(refs[a].at[4 * px + 2 * py + pc], refs[n + a].at[me], send, recv, 7 * a + k, (px, py, pc))
                    for a in range(n) for k, (px, py, pc) in enumerate(_all_peers(x, y, c))]

    return _split_start(name, list(srcs) + [lax.empty(s.shape, s.dtype) for s in srcs], 7 * n, 0, build, after)


def _gather2_start(name, lands, after):
    def build(refs, send, recv, loc):
        x, y, c = lax.axis_index("x"), lax.axis_index("y"), lax.axis_index("c")
        me = 4 * x + 2 * y + c
        remotes = []
        for a, d in enumerate(refs):
            remotes.append(_remote(d.at[me], d.at[me], send, recv, 4 * a, (x, y, 1 - c)))
            remotes += [_remote(d.at[me], d.at[me], send, recv, 4 * a + 1 + k, (px, py, c))
                        for k, (px, py) in enumerate(_other_chips(x, y))]
        return [], remotes

    return _split_start(name, list(lands), 4 * len(lands), 0, build, after)


def _gather3_start(name, lands, after):
    def build(refs, send, recv, loc):
        x, y, c = lax.axis_index("x"), lax.axis_index("y"), lax.axis_index("c")
        me = 4 * x + 2 * y + c
        return [], [_remote(d.at[me], d.at[me], send, recv, 3 * a + k, to)
                    for a, d in enumerate(refs) for k, to in enumerate([(x, y, 1 - c), (1 - x, y, c), (x, 1 - y, c)])]

    return _split_start(name, list(lands), 3 * len(lands), 0, build, after)


def _gather3_relay(name, lands, after):
    def build(refs, send, recv, loc):
        x, y, c = lax.axis_index("x"), lax.axis_index("y"), lax.axis_index("c")
        from_x = c == 0
        slot = 4 * jnp.where(from_x, 1 - x, x) + 2 * jnp.where(from_x, y, 1 - y) + c
        to = (jnp.where(from_x, x, 1 - x), jnp.where(from_x, 1 - y, y), c)
        return [], [_remote(d.at[slot], d.at[slot], send, recv, a, to) for a, d in enumerate(refs)]

    return _split_start(name, list(lands), len(lands), 0, build, after)


def _gather2_forward(name, lands, after):
    n = len(lands)

    def build(refs, send, recv, loc):
        x, y, c = lax.axis_index("x"), lax.axis_index("y"), lax.axis_index("c")
        slots = [4 * px + 2 * py + c for px, py in _other_chips(x, y)]
        return [], [_remote(d.at[sl], d.at[sl], send, recv, 3 * a + k, (x, y, 1 - c))
                    for a, d in enumerate(refs) for k, sl in enumerate(slots)]

    return _split_start(name, list(lands), 3 * n, 0, build, after)


def _scatter2_pair_start(name, srcs, after):
    n = len(srcs)

    def build(refs, send, recv, loc):
        x, y, c = lax.axis_index("x"), lax.axis_index("y"), lax.axis_index("c")
        return [], [_remote(refs[a].at[2 * q + 1 - c], refs[n + a].at[q], send, recv, 4 * a + q, (x, y, 1 - c))
                    for a in range(n) for q in range(4)]

    lands = [lax.empty((4,) + s.shape[1:], s.dtype) for s in srcs]
    return _split_start(name, list(srcs) + lands, 4 * n, 0, build, after)


def _scatter2_chip_start(name, pairs, after):
    n = len(pairs)

    def build(refs, send, recv, loc):
        x, y, c = lax.axis_index("x"), lax.axis_index("y"), lax.axis_index("c")
        return [], [_remote(refs[a].at[2 * px + py], refs[n + a].at[2 * x + y], send, recv, 3 * a + k, (px, py, c))
                    for a in range(n) for k, (px, py) in enumerate(_other_chips(x, y))]

    return _split_start(name, list(pairs) + [lax.empty(p.shape, p.dtype) for p in pairs], 3 * n, 0, build, after)


def _pair_sum(name, src, theirs, ids):
    _, R, C = theirs.shape
    tr = min(R, 1024)

    def body(ids_ref, a_ref, b_ref, o_ref):
        o_ref[...] = (a_ref[...].astype(f32) + b_ref[...].astype(f32)).astype(bf16)

    spec = pl.BlockSpec((None, tr, C), lambda q, i, ids: (q, i, 0))
    return pl.pallas_call(
        body, name=name,
        grid_spec=pltpu.PrefetchScalarGridSpec(
            num_scalar_prefetch=1, grid=(4, R // tr),
            in_specs=[pl.BlockSpec((None, tr, C), lambda q, i, ids: (2 * q + ids[2], i, 0)), spec], out_specs=spec),
        out_shape=SDS(theirs.shape, bf16), compiler_params=_params(),
    )(ids, src, theirs)


def _scatter2_second_level(name, first, after, ids):
    outs = _split_wait(first, after)
    n = len(outs) // 2
    pairs = [_pair_sum(f"pair_sum_{name}{a}", outs[a], outs[n + a], ids) for a in range(n)]
    return _scatter2_chip_start(f"scatter_{name}_chip", pairs, ids)


def _scatter_finish(second, after):
    outs = _split_wait(second, after)
    n = len(outs) // 2
    return [(outs[a], outs[n + a]) for a in range(n)]


def _cast_into_slot(name, w, l, tr, ids, after):
    _, R, C = w.shape

    def body(ids_ref, w_ref, after_ref, o_ref):
        o_ref[...] = w_ref[...].astype(bf16)

    return pl.pallas_call(
        body, name=name,
        grid_spec=pltpu.PrefetchScalarGridSpec(
            num_scalar_prefetch=1, grid=(R // tr,),
            in_specs=[pl.BlockSpec((None, tr, C), lambda i, ids: (l, i, 0)), pl.BlockSpec(memory_space=pl.ANY)],
            out_specs=pl.BlockSpec((None, tr, C), lambda i, ids: (ids[0], i, 0))),
        out_shape=SDS((N_DEV, R, C), bf16), compiler_params=_params(),
    )(ids, w, after)


def _into_slot(name, a, ids):
    R, C = a.shape

    def body(ids_ref, a_ref, o_ref):
        o_ref[...] = a_ref[...]

    return pl.pallas_call(
        body, name=name,
        grid_spec=pltpu.PrefetchScalarGridSpec(
            num_scalar_prefetch=1, grid=(1,),
            in_specs=[pl.BlockSpec((R, C), lambda i, ids: (0, 0))],
            out_specs=pl.BlockSpec((None, R, C), lambda i, ids: (ids[0], 0, 0))),
        out_shape=SDS((N_DEV, R, C), f32), compiler_params=_params(),
    )(ids, a)


def _adam_math(w, g, m, v):
    m2 = ADAM_B1 * m + (1.0 - ADAM_B1) * g
    v2 = ADAM_B2 * v + (1.0 - ADAM_B2) * (g * g)
    m_hat = m2 / (1.0 - ADAM_B1 ** ADAM_STEP)
    v_hat = v2 / (1.0 - ADAM_B2 ** ADAM_STEP)
    delta = -ADAM_LR * (m_hat / (jnp.sqrt(v_hat) + ADAM_EPS) + ADAM_WD * w)
    return delta, m2, v2


def _sum_adam(name, pair_recv, w, m, v, l, prev, tr, ids, after):
    own, recv = pair_recv
    L, R, C = w.shape
    slots = recv.shape[0]
    mine = 0 if slots == N_DEV else 1

    def body(ids_ref, r_ref, own_ref, w_ref, m_ref, v_ref, after_ref, *rest):
        g_ref, d_ref, m2_ref, v2_ref = rest[-4:]
        terms = [jnp.where(ids_ref[mine] == q, own_ref[...], r_ref[q]).astype(f32) for q in range(slots)]
        g = terms[0]
        for t in terms[1:]:
            g = g + t
        d, m2, v2 = _adam_math(w_ref[...], g, m_ref[...], v_ref[...])
        g_ref[...] = g
        d_ref[...] = d
        m2_ref[...] = m2
        v2_ref[...] = v2

    wspec = pl.BlockSpec((None, tr, C), lambda i, ids: (l, i, 0))
    in_specs = [pl.BlockSpec((slots, tr, C), lambda i, ids: (0, i, 0)),
                pl.BlockSpec((None, tr, C), lambda i, ids: (ids[mine], i, 0)), wspec, wspec, wspec,
                pl.BlockSpec(memory_space=pl.ANY)]
    args = [ids, recv, own, w, m, v, after]
    aliases = {}
    if prev is not None:
        in_specs += [pl.BlockSpec(memory_space=pl.ANY)] * 4
        args += list(prev)
        aliases = {7 + k: k for k in range(4)}
    return pl.pallas_call(
        body, name=name,
        grid_spec=pltpu.PrefetchScalarGridSpec(num_scalar_prefetch=1, grid=(R // tr,), in_specs=in_specs,
                                               out_specs=[wspec] * 4),
        out_shape=[SDS((L, R, C), f32)] * 4, input_output_aliases=aliases, compiler_params=_params(),
    )(*args)


def _small_sum_adam(recv_head, recv_tail, ws, ms, vs, offs):
    n = len(ws)
    r0 = recv_head.shape[1]

    def body(*refs):
        rh, rt = refs[0], refs[1]
        w_refs, m_refs, v_refs = refs[2:2 + n], refs[2 + n:2 + 2 * n], refs[2 + 2 * n:2 + 3 * n]
        outs = refs[2 + 3 * n:]
        for p in range(n):
            lo, hi = offs[p], offs[p] + ws[p].shape[0]
            pieces = []
            if lo < r0:
                pieces.append((rh, lo, 0, min(hi, r0) - lo))
            if hi > r0:
                pieces.append((rt, max(lo, r0) - r0, max(lo, r0) - lo, hi - max(lo, r0)))
            for src, a, b, cnt in pieces:
                g = src[0, a:a + cnt, :]
                for s in range(1, N_DEV):
                    g = g + src[s, a:a + cnt, :]
                d, m2, v2 = _adam_math(w_refs[p][b:b + cnt, :], g, m_refs[p][b:b + cnt, :], v_refs[p][b:b + cnt, :])
                for k, val in enumerate((g, d, m2, v2)):
                    outs[k * n + p][b:b + cnt, :] = val

    return pl.pallas_call(
        body, name="small_sum_adam", out_shape=[SDS(w.shape, f32) for w in ws] * 4, compiler_params=_params(),
    )(recv_head, recv_tail, *ws, *ms, *vs)


def _rms_proj(x, g, l, w_in_g, after, tm=1024):
    S, D = x.shape
    wc = w_in_g.shape[2]
    n_out = N_DEV * wc

    def body(x_ref, g_ref, w_ref, after_ref, proj_ref, h_ref):
        @pl.when(pl.program_id(1) == 0)
        def _():
            xv = x_ref[...]
            r = lax.rsqrt(jnp.mean(xv * xv, axis=-1, keepdims=True) + EPS)
            h_ref[...] = (xv * r * g_ref[...]).astype(bf16)

        proj_ref[...] = _dot(h_ref[...], w_ref[...])

    return pl.pallas_call(
        body, name="rms_proj", grid=(S // tm, N_DEV),
        in_specs=[pl.BlockSpec((tm, D), lambda i, j: (i, 0)), _layer_spec(g, l),
                  pl.BlockSpec((None, D, wc), lambda i, j: (j, 0, 0)), pl.BlockSpec(memory_space=pl.ANY)],
        out_specs=[pl.BlockSpec((tm, wc), lambda i, j: (i, j)), pl.BlockSpec((tm, D), lambda i, j: (i, 0))],
        out_shape=[SDS((S, n_out), f32), SDS((S, D), bf16)], compiler_params=_params(),
    )(x, g, w_in_g, after)


def _out_proj(x, y, w_out_g, after, tm=512):
    S, D = x.shape
    rb = w_out_g.shape[1]

    def body(x_ref, y_ref, w_ref, after_ref, o_ref):
        w = w_ref[...].reshape(N_DEV * rb, D)
        o_ref[...] = x_ref[...] + _dot(y_ref[...], w)

    row = pl.BlockSpec((tm, D), lambda i: (i, 0))
    return pl.pallas_call(
        body, name="out_proj", grid=(S // tm,),
        in_specs=[row, row, pl.BlockSpec((N_DEV, rb, D), lambda i: (0, 0, 0)), pl.BlockSpec(memory_space=pl.ANY)],
        out_specs=row, out_shape=SDS((S, D), f32), compiler_params=_params(),
    )(x, y, w_out_g, after)


def _out_proj_loss(x, y, w_out_g, tgt, tm=256):
    S, D = x.shape
    rb = w_out_g.shape[1]

    def body(x_ref, y_ref, w_ref, t_ref, dx_ref, dxb_ref, l_ref):
        i = pl.program_id(0)
        w = w_ref[...].reshape(N_DEV * rb, D)
        d = x_ref[...] + _dot(y_ref[...], w) - t_ref[...]
        dx = d * (1.0 / D)
        dx_ref[...] = dx
        dxb_ref[...] = dx.astype(bf16)
        e = d * d
        part = e[:, 0:128]
        for k in range(1, D // 128):
            part = part + e[:, k * 128:(k + 1) * 128]
        part = jnp.sum(part.reshape(tm // 8, 8, 128), axis=0)

        @pl.when(i == 0)
        def _():
            l_ref[...] = jnp.zeros_like(l_ref)

        l_ref[...] += part

        @pl.when(i == pl.num_programs(0) - 1)
        def _():
            tot = jnp.sum(l_ref[...], axis=1, keepdims=True)
            tot = jnp.sum(tot, axis=0, keepdims=True)
            l_ref[...] = jnp.broadcast_to(tot * (0.5 / D), l_ref.shape)

    row = pl.BlockSpec((tm, D), lambda i: (i, 0))
    return pl.pallas_call(
        body, name="out_proj_loss", grid=(S // tm,),
        in_specs=[row, row, pl.BlockSpec((N_DEV, rb, D), lambda i: (0, 0, 0)), row],
        out_specs=[row, row, pl.BlockSpec((8, 128), lambda i: (0, 0))],
        out_shape=[SDS((S, D), f32), SDS((S, D), bf16), SDS((8, 128), f32)], compiler_params=_params(),
    )(x, y, w_out_g, tgt)


def _out_bwd_dy(dxb, w_out_g, tm=1024):
    S, D = dxb.shape
    rb = w_out_g.shape[1]

    nb = 2

    def body(dx_ref, w_ref, o_ref):
        o_ref[...] = _dot(dx_ref[...], w_ref[...].reshape(nb * rb, D), NT)

    return pl.pallas_call(
        body, name="out_bwd_dy", grid=(S // tm, N_DEV // nb),
        in_specs=[pl.BlockSpec((tm, D), lambda i, j: (i, 0)),
                  pl.BlockSpec((nb, rb, D), lambda i, j: (j, 0, 0))],
        out_specs=pl.BlockSpec((tm, nb * rb), lambda i, j: (i, j)),
        out_shape=SDS((S, D), f32), compiler_params=_params(),
    )(dxb, w_out_g)


def _tn_grad(name, a, b, tm, tn, rows_major, after):
    S, M = a.shape
    N = b.shape[1]
    if rows_major:
        rb = M // N_DEV
        nb = tm // rb
        out_shape = SDS((N_DEV, rb, N), bf16)
        out_spec = pl.BlockSpec((nb, rb, tn), lambda i, j: (i, 0, j))
    else:
        out_shape = SDS((N_DEV, M, N // N_DEV), bf16)
        assert tn == N // N_DEV
        out_spec = pl.BlockSpec((None, tm, tn), lambda i, j: (j, i, 0))

    def body(a_ref, b_ref, after_ref, o_ref):
        o_ref[...] = _dot(a_ref[...], b_ref[...], TN).astype(bf16).reshape(o_ref.shape)

    return pl.pallas_call(
        body, name=name, grid=(M // tm, N // tn),
        in_specs=[pl.BlockSpec((S, tm), lambda i, j: (0, i)), pl.BlockSpec((S, tn), lambda i, j: (0, j)),
                  pl.BlockSpec(memory_space=pl.ANY)],
        out_specs=out_spec, out_shape=out_shape, compiler_params=_params(),
    )(a, b, after)


def _in_bwd_dh(dproj, w_in_g, after, tm=1024, tn=256):
    S = dproj.shape[0]
    _, D, wc = w_in_g.shape
    tm = min(tm, S)

    def body(dp_ref, w_ref, after_ref, o_ref):
        acc = _dot(dp_ref[:, 0:wc], w_ref[0], NT)
        for k in range(1, N_DEV):
            acc = acc + _dot(dp_ref[:, k * wc:(k + 1) * wc], w_ref[k], NT)
        o_ref[...] = acc

    return pl.pallas_call(
        body, name="in_bwd_dh", grid=(S // tm, D // tn),
        in_specs=[pl.BlockSpec((tm, N_DEV * wc), lambda i, j: (i, 0)),
                  pl.BlockSpec((N_DEV, tn, wc), lambda i, j: (0, j, 0)), pl.BlockSpec(memory_space=pl.ANY)],
        out_specs=pl.BlockSpec((tm, tn), lambda i, j: (i, j)),
        out_shape=SDS((S, D), f32), compiler_params=_params(),
    )(dproj, w_in_g, after)


def _rms_bwd(dh, x, g, l, dx_next, after, tm=256):
    S, D = x.shape

    def body(dh_ref, x_ref, g_ref, dxn_ref, after_ref, dx_ref, dxb_ref, dg_ref):
        @pl.when(pl.program_id(0) == 0)
        def _():
            dg_ref[...] = jnp.zeros_like(dg_ref)

        dh = dh_ref[...]
        xv = x_ref[...]
        r = lax.rsqrt(jnp.mean(xv * xv, axis=-1, keepdims=True) + EPS)
        xhat = xv * r
        dxhat = dh * g_ref[...]
        dx = r * (dxhat - xhat * jnp.mean(dxhat * xhat, axis=-1, keepdims=True)) + dxn_ref[...]
        dx_ref[...] = dx
        dxb_ref[...] = dx.astype(bf16)
        dg_ref[...] += jnp.sum(dh * xhat, axis=0, keepdims=True)

    row = pl.BlockSpec((tm, D), lambda i: (i, 0))
    vec = pl.BlockSpec((1, D), lambda i: (0, 0))
    return pl.pallas_call(
        body, name="rms_bwd", grid=(S // tm,),
        in_specs=[row, row, _layer_spec(g, l), row, pl.BlockSpec(memory_space=pl.ANY)], out_specs=[row, row, vec],
        out_shape=[SDS((S, D), f32), SDS((S, D), bf16), SDS((1, D), f32)], compiler_params=_params(),
    )(dh, x, g, dx_next, after)


def _sgu_fwd(proj, ln_g, ln_b, w_s, b_t, l, after):
    S = proj.shape[0]
    da = A_GROUPS * HEAD_DIM
    D = 2 * da

    def body(u_ref, v_ref, z_ref, lg_ref, lb_ref, w_ref, bt_ref, after_ref, y_ref):
        u = _gelu(u_ref[...])
        v = _gelu(v_ref[...])
        z = z_ref[...]
        mu = jnp.mean(v, axis=-1, keepdims=True)
        xc = v - mu
        rs = lax.rsqrt(jnp.mean(xc * xc, axis=-1, keepdims=True) + EPS)
        vn = (xc * rs * lg_ref[...] + lb_ref[...]).astype(bf16)
        gate = u * (z * jax.nn.sigmoid(z))
        tri = lax.broadcasted_iota(jnp.int32, (CHUNK, CHUNK), 0) >= lax.broadcasted_iota(jnp.int32, (CHUNK, CHUNK), 1)
        for g in range(A_GROUPS):
            sl = slice(g * HEAD_DIM, (g + 1) * HEAD_DIM)
            wm = jnp.where(tri, w_ref[g], 0.0).astype(bf16)
            for ck in range(SGU_CHUNKS):
                rw = slice(ck * CHUNK, (ck + 1) * CHUNK)
                mixed = _dot(wm, vn[rw, sl]) + bt_ref[:, g:g + 1]
                y_ref[rw, sl] = (gate[rw, sl] * mixed).astype(bf16)

    blk = lambda cb: pl.BlockSpec((SGU_CHUNKS * CHUNK, da), lambda c: (c, cb))
    full = lambda shp: pl.BlockSpec(shp, lambda c: (0,) * len(shp))
    return pl.pallas_call(
        body, name="sgu_fwd", grid=(S // (SGU_CHUNKS * CHUNK),),
        in_specs=[blk(0), blk(1), blk(2), _layer_spec(ln_g, l), _layer_spec(ln_b, l), _layer_spec(w_s, l),
                  _layer_spec(b_t, l), pl.BlockSpec(memory_space=pl.ANY)],
        out_specs=blk(0), out_shape=SDS((S, D), bf16), compiler_params=_params(),
    )(proj, proj, proj, ln_g, ln_b, w_s, b_t, after)


def _sgu_bwd(proj, dy, ln_g, ln_b, w_s, b_t, l, after):
    S = proj.shape[0]
    da = A_GROUPS * HEAD_DIM
    n_proj = proj.shape[1]

    def body(u_ref, v_ref, z_ref, dy_ref, lg_ref, lb_ref, w_ref, bt_ref, after_ref,
             dp_ref, dw_ref, db_ref, dlg_ref, dlb_ref, dvn_ref):
        @pl.when(pl.program_id(0) == 0)
        def _():
            dw_ref[...] = jnp.zeros_like(dw_ref)
            db_ref[...] = jnp.zeros_like(db_ref)
            dlg_ref[...] = jnp.zeros_like(dlg_ref)
            dlb_ref[...] = jnp.zeros_like(dlb_ref)

        up, vp, z, dy = u_ref[...], v_ref[...], z_ref[...], dy_ref[...]
        u, gu = _gelu_and_grad(up)
        v, gv = _gelu_and_grad(vp)
        s, gs = _silu_and_grad(z)
        mu = jnp.mean(v, axis=-1, keepdims=True)
        xc = v - mu
        rs = lax.rsqrt(jnp.mean(xc * xc, axis=-1, keepdims=True) + EPS)
        vhat = xc * rs
        lg = lg_ref[...]
        vn = (vhat * lg + lb_ref[...]).astype(bf16)
        tri = lax.broadcasted_iota(jnp.int32, (CHUNK, CHUNK), 0) >= lax.broadcasted_iota(jnp.int32, (CHUNK, CHUNK), 1)
        lane = lax.broadcasted_iota(jnp.int32, (CHUNK, HEAD_DIM), 1)
        dys = dy * s
        db = jnp.zeros((CHUNK, HEAD_DIM), f32)
        for g in range(A_GROUPS):
            sl = slice(g * HEAD_DIM, (g + 1) * HEAD_DIM)
            zl = slice(2 * da + g * HEAD_DIM, 2 * da + (g + 1) * HEAD_DIM)
            wm = jnp.where(tri, w_ref[g], 0.0).astype(bf16)
            for ck in range(SGU_CHUNKS):
                rw = slice(ck * CHUNK, (ck + 1) * CHUNK)
                mixed = _dot(wm, vn[rw, sl]) + bt_ref[:, g:g + 1]
                dmix = dys[rw, sl] * u[rw, sl]
                dp_ref[rw, sl] = (dys[rw, sl] * mixed * gu[rw, sl]).astype(bf16)
                dp_ref[rw, zl] = (dy[rw, sl] * u[rw, sl] * mixed * gs[rw, sl]).astype(bf16)
                dmb = dmix.astype(bf16)
                dw_ref[g] += jnp.where(tri, _dot(dmb, vn[rw, sl], NT), 0.0)
                dvn_ref[rw, sl] = _dot(wm, dmb, TN)
                db = db + jnp.where(lane == g, jnp.sum(dmix, axis=1, keepdims=True), 0.0)
        db_ref[...] += db
        dvn = dvn_ref[...]
        dlg_ref[...] += jnp.sum(dvn * vhat, axis=0, keepdims=True)
        dlb_ref[...] += jnp.sum(dvn, axis=0, keepdims=True)
        dvhat = dvn * lg
        dv = rs * (dvhat - jnp.mean(dvhat, axis=-1, keepdims=True)
                   - vhat * jnp.mean(dvhat * vhat, axis=-1, keepdims=True))
        dp_ref[:, da:2 * da] = (dv * gv).astype(bf16)

    rows = SGU_CHUNKS * CHUNK
    blk = lambda cb: pl.BlockSpec((rows, da), lambda c: (c, cb))
    full = lambda shp: pl.BlockSpec(shp, lambda c: (0,) * len(shp))
    return pl.pallas_call(
        body, name="sgu_bwd", grid=(S // rows,),
        in_specs=[blk(0), blk(1), blk(2), blk(0), _layer_spec(ln_g, l), _layer_spec(ln_b, l), _layer_spec(w_s, l),
                  _layer_spec(b_t, l), pl.BlockSpec(memory_space=pl.ANY)],
        out_specs=[pl.BlockSpec((rows, 3 * da), lambda c: (c, 0)), full((A_GROUPS, CHUNK, CHUNK)),
                   full((CHUNK, HEAD_DIM)), full((1, da)), full((1, da))],
        out_shape=[SDS((S, n_proj), bf16), SDS((A_GROUPS, CHUNK, CHUNK), f32), SDS((CHUNK, HEAD_DIM), f32),
                   SDS((1, da), f32), SDS((1, da), f32)],
        scratch_shapes=[pltpu.VMEM((rows, da), f32)], compiler_params=_params(),
    )(proj, proj, proj, dy, ln_g, ln_b, w_s, b_t, after)


def _sb_scores(q, kblk, kb, rows, cols, masked):
    z = _dot(q, kblk, NT) * (1.0 / math.sqrt(HEAD_DIM))
    t = jnp.log(1.0 + jnp.exp(-jnp.abs(z)))
    log_1mb = -(jnp.maximum(z, 0.0) + t)
    log_beta = jnp.minimum(z, 0.0) - t
    if not masked:
        return None, log_beta, log_1mb
    causal = (cols + kb * TK) < rows
    return causal, log_beta, jnp.where(causal, log_1mb, 0.0)


def _sb_tiles(i):
    rows = i * TQ + lax.broadcasted_iota(jnp.int32, (TQ, TK), 0)
    cols = lax.broadcasted_iota(jnp.int32, (TQ, TK), 1)
    r_i = lax.broadcasted_iota(jnp.int32, (TK, TK), 0)
    c_i = lax.broadcasted_iota(jnp.int32, (TK, TK), 1)
    upper, lower = (r_i > c_i).astype(bf16), (r_i < c_i).astype(bf16)
    slot = lax.broadcasted_iota(jnp.int32, (TQ, CARRY_LANES), 1)
    return rows, cols, slot, jnp.concatenate([upper, upper], axis=0), jnp.concatenate([lower, lower], axis=0)


def _suffix_sum(t, tri):
    hi = lax.bitcast_convert_type(lax.bitcast_convert_type(t, jnp.uint32) & jnp.uint32(0xFFFF0000), f32)
    both = jnp.concatenate([hi.astype(bf16), (t - hi).astype(bf16)], axis=1)
    return _dot(both, tri)


def _sb_fwd(proj, y_prev, col0, after):
    S = proj.shape[0]
    D = y_prev.shape[1]
    dh = N_HEADS * HEAD_DIM
    n_diag = TQ // TK

    def body(q_ref, k_ref, v_ref, z_ref, yp_ref, after_ref, y_ref, o_ref, car_ref, qb, kb_s, vb_s, c_ref):
        i = pl.program_id(0)

        @pl.when(i == 0)
        def _():
            kb_s[...] = k_ref[...].astype(bf16)
            vb_s[...] = v_ref[...].astype(bf16)

        qb[...] = q_ref[...].astype(bf16)
        o_ref[...] = jnp.zeros_like(o_ref)
        c_ref[...] = jnp.zeros_like(c_ref)
        car_ref[...] = jnp.zeros_like(car_ref)
        nkb = (i + 1) * n_diag
        rows, cols, slot, upper, _ = _sb_tiles(i)

        def make_step(masked):
            def step(jj, carry):
                kb = nkb - 1 - jj
                off = pl.multiple_of(kb * TK, TK)
                hs = range(N_HEADS)
                sls = [slice(h * HEAD_DIM, (h + 1) * HEAD_DIM) for h in hs]
                sc = [_sb_scores(qb[:, sls[h]], kb_s[pl.ds(off, TK), sls[h]], kb, rows, cols, masked) for h in hs]
                suf = [_suffix_sum(sc[h][2], upper) for h in hs]
                cs = [c_ref[h] for h in hs]
                es = [jnp.exp(sc[h][1] + suf[h] + cs[h][:, :1]) for h in hs]
                if masked:
                    es = [jnp.where(sc[h][0], es[h], 0.0) for h in hs]
                pv = [_dot(es[h].astype(bf16), vb_s[pl.ds(off, TK), sls[h]]) for h in hs]
                for h in hs:
                    o_ref[:, sls[h]] += pv[h]
                    car_ref[h] = jnp.where(slot == kb, cs[h], car_ref[h])
                    c_ref[h] = cs[h] + jnp.sum(sc[h][2], axis=1, keepdims=True)
                return carry
            return step

        lax.fori_loop(0, n_diag, make_step(True), 0)
        lax.fori_loop(n_diag, nkb, make_step(False), 0)
        z = z_ref[...]
        y_ref[...] = (o_ref[...] * (z * jax.nn.sigmoid(z))).astype(bf16)

    cb = col0 * HEAD_DIM // dh
    qspec = lambda k: pl.BlockSpec((TQ, dh), lambda i: (i, cb + k))
    kspec = lambda k: pl.BlockSpec((S, dh), lambda i: (0, cb + k))
    return pl.pallas_call(
        body, name="sb_fwd", grid=(S // TQ,),
        in_specs=[qspec(0), kspec(1), kspec(2), qspec(3), pl.BlockSpec(memory_space=pl.ANY),
                  pl.BlockSpec(memory_space=pl.ANY)],
        out_specs=[pl.BlockSpec((TQ, dh), lambda i: (i, A_GROUPS * HEAD_DIM // dh)),
                   pl.BlockSpec((TQ, dh), lambda i: (i, 0)),
                   pl.BlockSpec((N_HEADS, TQ, CARRY_LANES), lambda i: (0, i, 0))],
        out_shape=[SDS((S, D), bf16), SDS((S, dh), f32), SDS((N_HEADS, S, CARRY_LANES), f32)],
        input_output_aliases={4: 0},
        scratch_shapes=[pltpu.VMEM((TQ, dh), bf16), pltpu.VMEM((S, dh), bf16), pltpu.VMEM((S, dh), bf16),
                        pltpu.VMEM((N_HEADS, TQ, CARRY_LANES), f32)],
        compiler_params=_params(),
    )(proj, proj, proj, proj, y_prev, after)


def _sb_bwd(proj, o, car, dy, dproj_prev, col0, after):
    S = proj.shape[0]
    n_i = S // TQ
    dh = N_HEADS * HEAD_DIM
    n_diag = TQ // TK
    cb = col0 * HEAD_DIM // dh
    scale = 1.0 / math.sqrt(HEAD_DIM)

    def body(q_ref, k_ref, v_ref, z_ref, o_ref, car_ref, dy_ref, dpp_ref, after_ref,
             dp_ref, qb, kb_s, vb_s, dob, p_ref, dq_acc, dk_acc, dv_acc, st_a, st_b, st_k, st_v, tile_sems):
        i = pl.program_id(0)

        def put(stage_ref, row0, nrows, k):
            pltpu.sync_copy(stage_ref, dp_ref.at[pl.ds(row0, nrows), pl.ds((cb + k) * dh, dh)])

        def tile_copies(step):
            rows = pl.ds(pl.multiple_of(step * TQ, TQ), TQ)
            return [pltpu.make_async_copy(st, dp_ref.at[rows, pl.ds((cb + k) * dh, dh)], tile_sems.at[n])
                    for n, (st, k) in enumerate(((st_a, 0), (st_b, 3)))]

        @pl.when(i > 0)
        def _():
            for cp in tile_copies(i - 1):
                cp.wait()

        @pl.when(i == 0)
        def _():
            kb_s[...] = k_ref[...].astype(bf16)
            vb_s[...] = v_ref[...].astype(bf16)
            dk_acc[...] = jnp.zeros_like(dk_acc)
            dv_acc[...] = jnp.zeros_like(dv_acc)

        s, gs = _silu_and_grad(z_ref[...])
        dy = dy_ref[...]
        st_b[...] = (dy * o_ref[...] * gs).astype(bf16)
        dob[...] = (dy * s).astype(bf16)
        qb[...] = q_ref[...].astype(bf16)
        p_ref[...] = jnp.zeros_like(p_ref)
        dq_acc[...] = jnp.zeros_like(dq_acc)
        nkb = (i + 1) * n_diag
        rows, cols, slot, upper, lower = _sb_tiles(i)

        def make_step(masked):
            def step(kb, carry):
                off = pl.multiple_of(kb * TK, TK)
                hs = range(N_HEADS)
                sls = [slice(h * HEAD_DIM, (h + 1) * HEAD_DIM) for h in hs]
                qs = [qb[:, sls[h]] for h in hs]
                ks = [kb_s[pl.ds(off, TK), sls[h]] for h in hs]
                dos = [dob[:, sls[h]] for h in hs]
                sc = [_sb_scores(qs[h], ks[h], kb, rows, cols, masked) for h in hs]
                da = [_dot(dos[h], vb_s[pl.ds(off, TK), sls[h]], NT) for h in hs]
                suf = [_suffix_sum(sc[h][2], upper) for h in hs]
                onehot = slot == kb
                cs = [jnp.sum(jnp.where(onehot, car_ref[h], 0.0), axis=1, keepdims=True) for h in hs]
                es = [jnp.exp(sc[h][1] + suf[h] + cs[h]) for h in hs]
                if masked:
                    es = [jnp.where(sc[h][0], es[h], 0.0) for h in hs]
                gs_ = [da[h] * es[h] for h in hs]
                ps = [p_ref[h] for h in hs]
                pre = [_suffix_sum(gs_[h], lower) + ps[h][:, :1] for h in hs]
                dzs = []
                for h in hs:
                    beta = jnp.exp(sc[h][1])
                    dzz = gs_[h] * (1.0 - beta) - beta * pre[h]
                    if masked:
                        dzz = jnp.where(sc[h][0], dzz, 0.0)
                    dzs.append((dzz * scale).astype(bf16))
                dqs = [_dot(dzs[h], ks[h]) for h in hs]
                dks = [_dot(dzs[h], qs[h], TN) for h in hs]
                dvs = [_dot(es[h].astype(bf16), dos[h], TN) for h in hs]
                for h in hs:
                    dq_acc[:, sls[h]] += dqs[h]
                    dk_acc[pl.ds(off, TK), sls[h]] += dks[h]
                    dv_acc[pl.ds(off, TK), sls[h]] += dvs[h]
                    p_ref[h] = ps[h] + jnp.sum(gs_[h], axis=1, keepdims=True)
                return carry
            return step

        lax.fori_loop(0, nkb - n_diag, make_step(False), 0)
        lax.fori_loop(nkb - n_diag, nkb, make_step(True), 0)
        st_a[...] = dq_acc[...].astype(bf16)
        for cp in tile_copies(i):
            cp.start()

        @pl.when(i == n_i - 1)
        def _():
            st_k[...] = dk_acc[...].astype(bf16)
            st_v[...] = dv_acc[...].astype(bf16)
            put(st_k, 0, S, 1)
            put(st_v, 0, S, 2)
            for cp in tile_copies(i):
                cp.wait()

    qspec = lambda k: pl.BlockSpec((TQ, dh), lambda i: (i, cb + k))
    kspec = lambda k: pl.BlockSpec((S, dh), lambda i: (0, cb + k))
    return pl.pallas_call(
        body, name="sb_bwd", grid=(n_i,),
        in_specs=[qspec(0), kspec(1), kspec(2), qspec(3),
                  pl.BlockSpec((TQ, dh), lambda i: (i, 0)),
                  pl.BlockSpec((N_HEADS, TQ, CARRY_LANES), lambda i: (0, i, 0)),
                  pl.BlockSpec((TQ, dh), lambda i: (i, A_GROUPS * HEAD_DIM // dh)),
                  pl.BlockSpec(memory_space=pl.ANY), pl.BlockSpec(memory_space=pl.ANY)],
        out_specs=pl.BlockSpec(memory_space=pl.ANY),
        out_shape=SDS(dproj_prev.shape, bf16),
        input_output_aliases={7: 0},
        scratch_shapes=[pltpu.VMEM((TQ, dh), bf16), pltpu.VMEM((S, dh), bf16), pltpu.VMEM((S, dh), bf16),
                        pltpu.VMEM((TQ, dh), bf16), pltpu.VMEM((N_HEADS, TQ, CARRY_LANES), f32), pltpu.VMEM((TQ, dh), f32),
                        pltpu.VMEM((S, dh), f32), pltpu.VMEM((S, dh), f32),
                        pltpu.VMEM((TQ, dh), bf16), pltpu.VMEM((TQ, dh), bf16),
                        pltpu.VMEM((S, dh), bf16), pltpu.VMEM((S, dh), bf16), pltpu.SemaphoreType.DMA((2,))],
        compiler_params=_params(56),
    )(proj, proj, proj, proj, o, car, dy, dproj_prev, after)


def _mem_kv(mem, mg, l, w_kv_g):
    M, D = mem.shape
    rb, n = w_kv_g.shape[1], w_kv_g.shape[2]

    def body(m_ref, g_ref, w_ref, kv_ref):
        mv = m_ref[...]
        r = lax.rsqrt(jnp.mean(mv * mv, axis=-1, keepdims=True) + EPS)
        mh = (mv * r * g_ref[...]).astype(bf16)
        kv_ref[...] = _dot(mh, w_ref[...].reshape(N_DEV * rb, n))

    return pl.pallas_call(
        body, name="mem_kv", grid=(1,),
        in_specs=[pl.BlockSpec((M, D), lambda i: (0, 0)), _layer_spec(mg, l),
                  pl.BlockSpec((N_DEV, rb, n), lambda i: (0, 0, 0))],
        out_specs=pl.BlockSpec((M, n), lambda i: (0, 0)),
        out_shape=SDS((M, n), f32), compiler_params=_params(),
    )(mem, mg, w_kv_g)


def _xattn_head(q_ref, kv_ref, qg, kg, h):
    dc = N_HEADS * HEAD_DIM
    sl = slice(h * HEAD_DIM, (h + 1) * HEAD_DIM)
    qh = q_ref[:, sl]
    rq = lax.rsqrt(jnp.mean(qh * qh, axis=-1, keepdims=True) + EPS)
    qhat = qh * rq
    qn = (qhat * qg).astype(bf16)
    kh = kv_ref[:, sl]
    rk = lax.rsqrt(jnp.mean(kh * kh, axis=-1, keepdims=True) + EPS)
    kn = (kh * rk * kg).astype(bf16)
    vh = kv_ref[:, dc + h * HEAD_DIM:dc + (h + 1) * HEAD_DIM].astype(bf16)
    s = _dot(qn, kn, NT) * (1.0 / math.sqrt(HEAD_DIM))
    e = jnp.exp(s - jnp.max(s, axis=-1, keepdims=True))
    p = e / jnp.sum(e, axis=-1, keepdims=True)
    o = _dot(p.astype(bf16), vh)
    return sl, rq, qhat, qn, kn, vh, p, o


def _xattn_fwd(proj, kv, qg, kg, l, y_prev, col0, tq=512):
    S = proj.shape[0]
    D = y_prev.shape[1]
    dc = N_HEADS * HEAD_DIM
    M = kv.shape[0]

    def body(q_ref, z_ref, kv_ref, qg_ref, kg_ref, yp_ref, y_ref):
        for h in range(N_HEADS):
            sl, _, _, _, _, _, _, o = _xattn_head(q_ref, kv_ref, qg_ref[...], kg_ref[...], h)
            z = z_ref[:, sl]
            y_ref[:, sl] = (o * (z * jax.nn.sigmoid(z))).astype(bf16)

    full = lambda shp: pl.BlockSpec(shp, lambda i: (0,) * len(shp))
    return pl.pallas_call(
        body, name="xattn_fwd", grid=(S // tq,),
        in_specs=[pl.BlockSpec((tq, dc), lambda i: (i, col0)), pl.BlockSpec((tq, dc), lambda i: (i, col0 + 1)),
                  full((M, 2 * dc)), _layer_spec(qg, l), _layer_spec(kg, l), pl.BlockSpec(memory_space=pl.ANY)],
        out_specs=pl.BlockSpec((tq, dc), lambda i: (i, D // dc - 1)),
        out_shape=SDS((S, D), bf16), input_output_aliases={5: 0}, compiler_params=_params(),
    )(proj, proj, kv, qg, kg, y_prev)


def _xattn_bwd(proj, kv, qg, kg, l, dy, dproj_prev, col0, tq=512):
    S = proj.shape[0]
    D = dy.shape[1]
    dc = N_HEADS * HEAD_DIM
    M = kv.shape[0]

    def body(q_ref, z_ref, kv_ref, qg_ref, kg_ref, dy_ref, dpp_ref, dp_ref, dkn_ref, dv_ref, dqg_ref):
        @pl.when(pl.program_id(0) == 0)
        def _():
            dkn_ref[...] = jnp.zeros_like(dkn_ref)
            dv_ref[...] = jnp.zeros_like(dv_ref)
            dqg_ref[...] = jnp.zeros_like(dqg_ref)

        qg = qg_ref[...]
        for h in range(N_HEADS):
            sl, rq, qhat, qn, kn, vh, p, o = _xattn_head(q_ref, kv_ref, qg, kg_ref[...], h)
            s, gs = _silu_and_grad(z_ref[:, sl])
            dyh = dy_ref[:, sl]
            dp_ref[:, dc + h * HEAD_DIM:dc + (h + 1) * HEAD_DIM] = (dyh * o * gs).astype(bf16)
            dob = (dyh * s).astype(bf16)
            dpr = _dot(dob, vh, NT)
            dv_ref[:, sl] += _dot(p.astype(bf16), dob, TN)
            ds = (p * (dpr - jnp.sum(p * dpr, axis=-1, keepdims=True)) * (1.0 / math.sqrt(HEAD_DIM))).astype(bf16)
            dqn = _dot(ds, kn)
            dkn_ref[:, sl] += _dot(ds, qn, TN)
            dqg_ref[...] += jnp.sum(dqn * qhat, axis=0, keepdims=True)
            dqhat = dqn * qg
            dp_ref[:, sl] = (rq * (dqhat - qhat * jnp.mean(dqhat * qhat, axis=-1, keepdims=True))).astype(bf16)

    full = lambda shp: pl.BlockSpec(shp, lambda i: (0,) * len(shp))
    return pl.pallas_call(
        body, name="xattn_bwd", grid=(S // tq,),
        in_specs=[pl.BlockSpec((tq, dc), lambda i: (i, col0)), pl.BlockSpec((tq, dc), lambda i: (i, col0 + 1)),
                  full((M, 2 * dc)), _layer_spec(qg, l), _layer_spec(kg, l),
                  pl.BlockSpec((tq, dc), lambda i: (i, D // dc - 1)), pl.BlockSpec(memory_space=pl.ANY)],
        out_specs=[pl.BlockSpec((tq, 2 * dc), lambda i: (i, col0 // 2)), full((M, dc)), full((M, dc)),
                   full((1, HEAD_DIM))],
        out_shape=[SDS(dproj_prev.shape, bf16), SDS((M, dc), f32), SDS((M, dc), f32), SDS((1, HEAD_DIM), f32)],
        input_output_aliases={6: 0}, compiler_params=_params(),
    )(proj, proj, kv, qg, kg, dy, dproj_prev)


def _mem_bwd(mem, mg, kg, l, kv, dkn, dv, w_kv_g):
    M, D = mem.shape
    rb, n = w_kv_g.shape[1], w_kv_g.shape[2]
    dc = n // 2

    def body(m_ref, g_ref, kv_ref, dkn_ref, dv_ref, kg_ref, w_ref, dw_ref, dmg_ref, dkg_ref, dkv_ref):
        mv = m_ref[...]
        r = lax.rsqrt(jnp.mean(mv * mv, axis=-1, keepdims=True) + EPS)
        mhat = mv * r
        mh = (mhat * g_ref[...]).astype(bf16)
        kg = kg_ref[...]
        dkg = jnp.zeros((1, HEAD_DIM), f32)
        for h in range(N_HEADS):
            sl = slice(h * HEAD_DIM, (h + 1) * HEAD_DIM)
            kh = kv_ref[:, sl]
            rk = lax.rsqrt(jnp.mean(kh * kh, axis=-1, keepdims=True) + EPS)
            khat = kh * rk
            dkn_h = dkn_ref[:, sl]
            dkg = dkg + jnp.sum(dkn_h * khat, axis=0, keepdims=True)
            dkhat = dkn_h * kg
            dkv_ref[:, sl] = (rk * (dkhat - khat * jnp.mean(dkhat * khat, axis=-1, keepdims=True))).astype(bf16)
        dkv_ref[:, dc:] = dv_ref[...].astype(bf16)
        dkg_ref[...] = dkg
        dkv = dkv_ref[...]
        dw_ref[...] = _dot(mh, dkv, TN).astype(bf16).reshape(N_DEV, rb, n)
        dmh = _dot(dkv, w_ref[...].reshape(N_DEV * rb, n), NT)
        dmg_ref[...] = jnp.sum(dmh * mhat, axis=0, keepdims=True)

    full = lambda shp: pl.BlockSpec(shp, lambda i: (0,) * len(shp))
    wspec = full((N_DEV, rb, n))
    return pl.pallas_call(
        body, name="mem_bwd", grid=(1,),
        in_specs=[full((M, D)), _layer_spec(mg, l), full((M, n)), full((M, dc)), full((M, dc)), _layer_spec(kg, l), wspec],
        out_specs=[wspec, full((1, D)), full((1, HEAD_DIM))],
        out_shape=[SDS((N_DEV, rb, n), bf16), SDS((1, D), f32), SDS((1, HEAD_DIM), f32)],
        scratch_shapes=[pltpu.VMEM((M, n), bf16)], compiler_params=_params(),
    )(mem, mg, kv, dkn, dv, kg, w_kv_g)


SMALL = ("norm_g", "sgu_ln_g", "sgu_ln_b", "sgu_w", "sgu_b", "mem_norm_g", "q_norm_g", "k_norm_g")


def _small_rows(like):
    rows = [math.prod(like[n].shape) // 128 for n in SMALL]
    offs = [0]
    for r in rows:
        offs.append(offs[-1] + -(-r // 8) * 8)
    return rows, offs


def _pack_small(parts, offs):
    pieces = []
    for k, n in enumerate(SMALL):
        a = parts[n].reshape(-1, 128)
        pieces.append(jnp.pad(a, ((0, offs[k + 1] - offs[k] - a.shape[0]), (0, 0))))
    return jnp.concatenate(pieces)


def kernel(x, mem, norm_g, w_in, sgu_ln_g, sgu_ln_b, sgu_w, sgu_b, mem_norm_g, w_mem_kv, q_norm_g, k_norm_g, w_out, loss_target, m_norm_g, m_w_in, m_sgu_ln_g, m_sgu_ln_b, m_sgu_w, m_sgu_b, m_mem_norm_g, m_w_mem_kv, m_q_norm_g, m_k_norm_g, m_w_out, v_norm_g, v_w_in, v_sgu_ln_g, v_sgu_ln_b, v_sgu_w, v_sgu_b, v_mem_norm_g, v_w_mem_kv, v_q_norm_g, v_k_norm_g, v_w_out):
    L, D, wc = w_in.shape
    S = x.shape[1]
    da = D // 2
    xs = x.reshape(S, D)
    mems = mem.reshape(mem.shape[1], D)
    tgt = loss_target.reshape(S, D)
    stacked = lambda a: a.reshape(a.shape[0], 1, -1)
    ng, lng, lnb, mg, qg, kg = map(stacked, (norm_g, sgu_ln_g, sgu_ln_b, mem_norm_g, q_norm_g, k_norm_g))
    b_t = jnp.swapaxes(sgu_b, 1, 2)
    sb_col, xa_col = 3 * da // HEAD_DIM, (3 * da + D) // (D // 4)

    ax, ay, ac = lax.axis_index("x"), lax.axis_index("y"), lax.axis_index("c")
    ids = jnp.stack([4 * ax + 2 * ay + ac, 2 * ax + ay, ac]).astype(jnp.int32)
    w_in0_b = _cast_into_slot("cast_w_in", w_in, 0, 512, ids, ids)
    first = _gather3_start("gather_w_in0", [w_in0_b], ids)
    late = first["token"]
    w_b = [(w_in0_b if l == 0 else _cast_into_slot("cast_w_in", w_in, l, 512, ids, late),
            _cast_into_slot("cast_w_kv", w_mem_kv, l, 256, ids, late),
            _cast_into_slot("cast_w_out", w_out, l, 256, ids, late)) for l in range(L)]
    relay = _gather3_relay("gather_w_in0_relay",
                           _split_wait(first, *[a for wl in w_b for a in wl if a is not w_in0_b]), ids)
    in_fwd = _gather2_forward("gather_w_in0_forward", _split_wait(relay, relay["token"]), ids)

    acts = []
    xl = xs
    for l in range(L):
        (w_in_g,) = _split_wait(in_fwd, xl if l else in_fwd["token"])
        rest = _gather3_start(f"gather_w_rest{l}", [w_b[l][1], w_b[l][2]], w_in_g)
        order = rest["token"]
        if l + 1 < L:
            nxt = _gather3_start(f"gather_w_in{l + 1}", [w_b[l + 1][0]], order)
            order = nxt["token"]
        proj, h = _rms_proj(xl, ng, l, w_in_g, order)
        rest_relay = _gather3_relay(f"gather_w_rest{l}_relay", _split_wait(rest, proj), proj)
        y = _sgu_fwd(proj, lng, lnb, sgu_w, b_t, l, rest_relay["token"])
        rest_fwd = _gather2_forward(f"gather_w_rest{l}_forward", _split_wait(rest_relay, y), y)
        order = rest_fwd["token"]
        if l + 1 < L:
            nxt_relay = _gather3_relay(f"gather_w_in{l + 1}_relay", _split_wait(nxt, order), order)
            order = nxt_relay["token"]
        y, o_b, car = _sb_fwd(proj, y, sb_col, order)
        w_kv_g, w_out_g = _split_wait(rest_fwd, o_b)
        order = o_b
        if l + 1 < L:
            in_fwd = _gather2_forward(f"gather_w_in{l + 1}_forward", _split_wait(nxt_relay, o_b), o_b)
            order = in_fwd["token"]
        kv = _mem_kv(mems, mg, l, w_kv_g)
        y = _xattn_fwd(proj, kv, qg, kg, l, y, xa_col)
        acts.append((xl, proj, h, y, o_b, car, kv, w_in_g, w_kv_g, w_out_g))
        if l + 1 < L:
            xl = _out_proj(xl, y, w_out_g, order)
        else:
            dx, dxb, loss_part = _out_proj_loss(xl, y, w_out_g, tgt)
    loss = lax.psum(loss_part[0, 0], ("x", "y", "c"))

    weights = dict(norm_g=norm_g, sgu_ln_g=sgu_ln_g, sgu_ln_b=sgu_ln_b, sgu_w=sgu_w, sgu_b=sgu_b,
                   mem_norm_g=mem_norm_g, q_norm_g=q_norm_g, k_norm_g=k_norm_g)
    moms_m = dict(norm_g=m_norm_g, sgu_ln_g=m_sgu_ln_g, sgu_ln_b=m_sgu_ln_b, sgu_w=m_sgu_w, sgu_b=m_sgu_b,
                  mem_norm_g=m_mem_norm_g, q_norm_g=m_q_norm_g, k_norm_g=m_k_norm_g)
    moms_v = dict(norm_g=v_norm_g, sgu_ln_g=v_sgu_ln_g, sgu_ln_b=v_sgu_ln_b, sgu_w=v_sgu_w, sgu_b=v_sgu_b,
                  mem_norm_g=v_mem_norm_g, q_norm_g=v_q_norm_g, k_norm_g=v_k_norm_g)
    small_rows, small_offs = _small_rows(weights)
    head_rows = D // 128
    assert SMALL[0] == "norm_g" and head_rows % 8 == 0

    seconds = {}
    pending = None
    adam = {"w_out": None, "w_mem_kv": None, "w_in": None}

    late_in = {}

    def update(lu, order):
        (r_out,) = _scatter_finish(seconds[f"g_out{lu}"], order)
        adam["w_out"] = _sum_adam("adam_w_out", r_out, w_out, m_w_out, v_w_out, lu, adam["w_out"], 128, ids, ids)
        (r_kv,) = _scatter_finish(seconds[f"g_kv{lu}"], adam["w_out"][0])
        (late_in[lu],) = _scatter_finish(seconds[f"g_in{lu}"], adam["w_out"][0])
        adam["w_mem_kv"] = _sum_adam("adam_w_kv", r_kv, w_mem_kv, m_w_mem_kv, v_w_mem_kv, lu, adam["w_mem_kv"], 256, ids, ids)
        return adam["w_mem_kv"][0]

    def update_in(lu, order):
        adam["w_in"] = _sum_adam("adam_w_in", late_in[lu], w_in, m_w_in, v_w_in, lu, adam["w_in"], 512, ids, order)
        return adam["w_in"][0]
    small = {n: [None] * L for n in SMALL}
    for l in reversed(range(L)):
        xl, proj, h, y, o_b, car, kv, w_in_g, w_kv_g, w_out_g = acts[l]
        dy = _out_bwd_dy(dxb, w_out_g)
        order = dy
        if pending is not None:
            seconds[pending[0]] = _scatter2_second_level(pending[0], pending[1], dy, ids)
            order = seconds[pending[0]]["token"]
        g_out = _tn_grad("out_bwd_dw", y, dxb, 512, 1024, True, order)
        seconds[f"g_out{l}"] = _scatter1_start(f"scatter_g_out{l}", [g_out], ids)
        dproj, d_sw, d_sb, d_lg, d_lb = _sgu_bwd(proj, dy, lng, lnb, sgu_w, b_t, l, seconds[f"g_out{l}"]["token"])
        dproj, dkn, dv, d_qg = _xattn_bwd(proj, kv, qg, kg, l, dy, dproj, xa_col)
        g_kv, d_mg, d_kg = _mem_bwd(mems, mg, kg, l, kv, dkn, dv, w_kv_g)
        seconds[f"g_kv{l}"] = _scatter1_start(f"scatter_g_kv{l}", [g_kv], ids)
        dproj = _sb_bwd(proj, o_b, car, dy, dproj, sb_col, seconds[f"g_kv{l}"]["token"])
        for n, val in (("sgu_ln_g", d_lg), ("sgu_ln_b", d_lb), ("sgu_w", d_sw), ("sgu_b", d_sb[:, :A_GROUPS].T),
                       ("mem_norm_g", d_mg), ("q_norm_g", d_qg), ("k_norm_g", d_kg)):
            small[n][l] = val.reshape(-1)
        order = seconds[f"g_kv{l}"]["token"]
        if l == 0:
            small["norm_g"][0] = jnp.zeros_like(small["norm_g"][1])
            part = _pack_small({n: jnp.stack(small[n]) for n in SMALL}, small_offs)
            tail = _gather1_start("gather_small_tail", [_into_slot("small_tail_slot", part[head_rows:], ids)], order)
            order = tail["token"]
        g_in_l = _tn_grad("in_bwd_dw", h, dproj, D, wc, False, order)
        first = _scatter2_pair_start(f"scatter_g_in{l}_pair", [g_in_l], ids)
        order = first["token"]
        pending = (f"g_in{l}", first)
        if l == 0:
            for lu in reversed(range(1, L)):
                order = update(lu, order)
            seconds[pending[0]] = _scatter2_second_level(pending[0], pending[1], order, ids)
            order = seconds[pending[0]]["token"]
            for lu in reversed(range(1, L)):
                order = update_in(lu, order)
        dh = _in_bwd_dh(dproj, w_in_g, order)
        dx, dxb, d_ng = _rms_bwd(dh, xl, ng, l, dx, order)
        small["norm_g"][l] = d_ng.reshape(-1)
    head = _gather1_start("gather_small_head",
                          [_into_slot("small_head_slot", small["norm_g"][0].reshape(head_rows, 128), ids)], ids)

    (r_out,) = _scatter_finish(seconds["g_out0"], head["token"])
    adam["w_out"] = _sum_adam("adam_w_out", r_out, w_out, m_w_out, v_w_out, 0, adam["w_out"], 128, ids, ids)
    (r_kv,) = _scatter_finish(seconds["g_kv0"], adam["w_out"][0])
    (r_in,) = _scatter_finish(seconds["g_in0"], adam["w_out"][0])
    adam["w_mem_kv"] = _sum_adam("adam_w_kv", r_kv, w_mem_kv, m_w_mem_kv, v_w_mem_kv, 0, adam["w_mem_kv"], 256, ids, ids)
    adam["w_in"] = _sum_adam("adam_w_in", r_in, w_in, m_w_in, v_w_in, 0, adam["w_in"], 512, ids, ids)
    (r_tail,) = _split_wait(tail, adam["w_in"][0])
    (r_head,) = _split_wait(head, r_tail)
    as128 = lambda d: [d[n].reshape(-1, 128) for n in SMALL]
    sm = _small_sum_adam(r_head, r_tail, as128(weights), as128(moms_m), as128(moms_v), small_offs)
    res = dict(adam)
    for p, n in enumerate(SMALL):
        res[n] = [sm[k * len(SMALL) + p].reshape(weights[n].shape) for k in range(4)]

    order = ("norm_g", "w_in", "sgu_ln_g", "sgu_ln_b", "sgu_w", "sgu_b", "mem_norm_g", "w_mem_kv", "q_norm_g",
             "k_norm_g", "w_out")
    outs = [loss, dx.reshape(x.shape)]
    for k in range(4):
        outs += [res[n][k] for n in order]
    return tuple(outs)
```
